```python
import math
import jax
import jax.numpy as jnp
from jax import lax
import numpy as np

D_MODEL = 1024
BATCH = 8
SEQ = 4096
DEPTH = 4

N_MIXERS = 4
GRID_W = 64
EPS = 1e-6
NEG_BIG = -1e30

SSD_DI = 2 * D_MODEL
SSD_HEADDIM = 64
SSD_HEADS = SSD_DI // SSD_HEADDIM
SSD_GROUPS = 4
SSD_HPG = SSD_HEADS // SSD_GROUPS
SSD_STATE = 128
SSD_CONV = 7
SSD_CHUNK = 128
SSD_CONV_CH = SSD_DI + 2 * SSD_GROUPS * SSD_STATE
SSD_IN = SSD_DI + SSD_CONV_CH + 2 * SSD_HEADS

HG_EXPAND = 128
HG_HEADS = D_MODEL // HG_EXPAND
HG_W = HG_HEADS * HG_EXPAND
HG_DV = HG_W // HG_HEADS
HG_CHUNK = 32
HG_IN = 5 * HG_W

AT_HEADS = 16
AT_KV = 8
AT_GRP = AT_HEADS // AT_KV
AT_HD = 128
AT_QBLK = 128
ROPE_THETA = 10000.0
ROPE_AXIS = AT_HD // 2
AT_QW = AT_HEADS * AT_HD
AT_KW = AT_KV * AT_HD
AT_IN = 2 * AT_QW + 2 * AT_KW

DL_PAIRS = ((128, 1), (512, 4), (2048, 16))
DL_HEADS = 16
DL_HD = 64
DL_W = DL_HEADS * DL_HD
DL_IN = 3 * len(DL_PAIRS) * DL_W + DL_W

REL_BUCKETS = 32
REL_MAX_DIST = 1024

N_SSD_LAYERS = (DEPTH - 0 + N_MIXERS - 1) // N_MIXERS
N_HG_LAYERS = (DEPTH - 1 + N_MIXERS - 1) // N_MIXERS
N_AT_LAYERS = (DEPTH - 2 + N_MIXERS - 1) // N_MIXERS
N_DL_LAYERS = (DEPTH - 3 + N_MIXERS - 1) // N_MIXERS

kernel_name = "hybrid_bidir_ssd_hgrn2_axialgqa_dilated"

F32 = jnp.float32


def rms_norm(x, g):
    xf = x.astype(F32)
    y = xf * lax.rsqrt(jnp.mean(xf * xf, axis=-1, keepdims=True) + EPS)
    return (y * g.astype(F32)).astype(x.dtype)


def centred_depthwise_conv(x, w, b):
    C = x.shape[-1]
    pad = w.shape[0] // 2
    y = lax.conv_general_dilated(x, w[:, None, :].astype(x.dtype), window_strides=(1,),
                                 padding=((pad, pad),), dimension_numbers=('NWC', 'WIO', 'NWC'),
                                 feature_group_count=C)
    return y + b.astype(x.dtype)


def exp_segsum(a):
    T = a.shape[-1]
    cs = jnp.cumsum(a, axis=-1)
    diff = cs[..., :, None] - cs[..., None, :]
    mask = jnp.tril(jnp.ones((T, T), dtype=bool))
    return jnp.exp(jnp.where(mask, diff, -jnp.inf))


def ssd_chunked(xdt, dA, Bm, Cm):
    b, S, G, J, P = xdt.shape
    N = Bm.shape[-1]
    Q = SSD_CHUNK
    nc = S // Q
    x = xdt.reshape(b, nc, Q, G, J, P)
    a = dA.reshape(b, nc, Q, G, J).transpose(0, 3, 4, 1, 2)
    Bc = Bm.reshape(b, nc, Q, G, N)
    Cc = Cm.reshape(b, nc, Q, G, N)
    a_cum = jnp.cumsum(a, axis=-1)
    L = exp_segsum(a)
    CB = jnp.einsum('bclgn,bcsgn->bcgls', Cc, Bc)
    y_diag = jnp.einsum('bcgls,bgjcls,bcsgjp->bclgjp', CB, L, x)
    decay_states = jnp.exp(a_cum[..., -1:] - a_cum)
    states = jnp.einsum('bclgn,bgjcl,bclgjp->bcgjpn', Bc, decay_states, x)
    states = jnp.concatenate([jnp.zeros_like(states[:, :1]), states], axis=1)
    chunk_decay = exp_segsum(jnp.pad(a_cum[..., -1], ((0, 0), (0, 0), (0, 0), (1, 0))))
    start_states = jnp.einsum('bgjzc,bcgjpn->bzgjpn', chunk_decay, states)[:, :-1]
    y_off = jnp.einsum('bclgn,bcgjpn,bgjcl->bclgjp', Cc, start_states, jnp.exp(a_cum))
    return (y_diag + y_off).reshape(b, S, G, J, P)


def ssd_mixer(h, w_in, conv_w, conv_b, dt_bias, a_log, d_skip, norm_g, w_out):
    b, S, _ = h.shape
    GN = SSD_GROUPS * SSD_STATE
    u = h @ w_in
    z = u[..., :SSD_DI]
    xbc = u[..., SSD_DI:SSD_DI + SSD_CONV_CH]
    dt_raw = u[..., SSD_DI + SSD_CONV_CH:].reshape(b, S, 2, SSD_HEADS)
    xbc = jax.nn.silu(centred_depthwise_conv(xbc, conv_w, conv_b))
    xs = xbc[..., :SSD_DI].reshape(b, S, SSD_GROUPS, SSD_HPG, SSD_HEADDIM)
    Bm = xbc[..., SSD_DI:SSD_DI + GN].reshape(b, S, SSD_GROUPS, SSD_STATE)
    Cm = xbc[..., SSD_DI + GN:].reshape(b, S, SSD_GROUPS, SSD_STATE)
    dt = jax.nn.softplus(dt_raw.astype(F32) + dt_bias.astype(F32))
    A = -jnp.exp(a_log.astype(F32))
    dA = (dt * A).reshape(b, S, 2, SSD_GROUPS, SSD_HPG)
    dtg = dt.reshape(b, S, 2, SSD_GROUPS, SSD_HPG)
    x_f = xs * dtg[:, :, 0, :, :, None]
    x_b = xs * dtg[:, :, 1, :, :, None]
    y_fwd = ssd_chunked(x_f, dA[:, :, 0], Bm, Cm)
    y_bwd = jnp.flip(ssd_chunked(jnp.flip(x_b, 1), jnp.flip(dA[:, :, 1], 1),
                                 jnp.flip(Bm, 1), jnp.flip(Cm, 1)), 1)
    y = y_fwd + y_bwd + xs * d_skip.reshape(SSD_GROUPS, SSD_HPG)[..., None]
    y = y.reshape(b, S, SSD_DI) * jax.nn.silu(z)
    gs = SSD_DI // SSD_GROUPS
    y = rms_norm(y.reshape(b, S, SSD_GROUPS, gs), norm_g.reshape(SSD_GROUPS, gs))
    return y.reshape(b, S, SSD_DI).astype(h.dtype) @ w_out


def hgrn2_chunked(q, k, v, g):
    b, S, h, dk = q.shape
    dv = v.shape[-1]
    C = HG_CHUNK
    nc = S // C

    def to_chunks(t):
        return t.reshape(b, nc, C, h, t.shape[-1]).transpose(1, 0, 3, 2, 4)

    mask = jnp.tril(jnp.ones((C, C), dtype=bool))

    def step(state, inp):
        qi, ki, vi, gi = inp
        G = jnp.cumsum(gi, axis=-2)
        Gr = G[..., C // 2:C // 2 + 1, :]
        q_t = qi * jnp.exp(G - Gr)
        k_t = ki * jnp.exp(Gr - G)
        att = jnp.where(mask, jnp.einsum('bhid,bhjd->bhij', q_t, k_t), 0.0)
        o = (jnp.einsum('bhij,bhjv->bhiv', att, vi)
             + jnp.einsum('bhid,bhdv->bhiv', qi * jnp.exp(G), state))
        G_last = G[..., -1:, :]
        new_state = (jnp.exp(G_last)[..., 0, :, None] * state
                     + jnp.einsum('bhjd,bhjv->bhdv', ki * jnp.exp(G_last - G), vi))
        return new_state, o

    s0 = jnp.zeros((b, h, dk, dv), F32)
    _, o = lax.scan(step, s0, (to_chunks(q), to_chunks(k), to_chunks(v), to_chunks(g)))
    return o.transpose(1, 0, 3, 2, 4).reshape(b, S, h, dv)


def hgrn2_mixer(h, lb, w_in, norm_g, w_out):
    b, S, _ = h.shape
    u = h @ w_in
    q, f_fwd, f_bwd, inp, gate = jnp.split(u, 5, axis=-1)
    shp = (b, S, HG_HEADS, HG_EXPAND)
    q = jax.nn.silu(q).astype(F32).reshape(shp)
    v = inp.astype(F32).reshape(b, S, HG_HEADS, HG_DV)
    lbh = lb.reshape(HG_HEADS, HG_EXPAND)

    def direction(fpre, reverse):
        f = lbh + (1.0 - lbh) * jax.nn.sigmoid(fpre.astype(F32).reshape(shp))
        args = (q, 1.0 - f, v, jnp.log(f))
        if reverse:
            return jnp.flip(hgrn2_chunked(*[jnp.flip(a, 1) for a in args]), 1)
        return hgrn2_chunked(*args)

    o = direction(f_fwd, False) + direction(f_bwd, True)
    o = rms_norm(o, norm_g.reshape(HG_HEADS, HG_DV)).reshape(b, S, HG_W).astype(h.dtype)
    return (o * jax.nn.silu(gate)) @ w_out


def axial_rope_tables(S):
    rows = S // GRID_W
    row = jnp.repeat(jnp.arange(rows), GRID_W).astype(F32)
    col = (jnp.arange(S) % GRID_W).astype(F32)
    inv = ROPE_THETA ** (-jnp.arange(0, ROPE_AXIS, 2, dtype=F32) / ROPE_AXIS)
    ang = jnp.stack([row[:, None] * inv, col[:, None] * inv], axis=1)
    return jnp.cos(ang), jnp.sin(ang)


def apply_axial_rope(x, cos, sin):
    shp = x.shape
    xf = x.astype(F32).reshape(*shp[:-1], 2, 2, ROPE_AXIS // 2)
    x1, x2 = xf[..., 0, :], xf[..., 1, :]
    c, s = cos[:, None], sin[:, None]
    out = jnp.stack([x1 * c - x2 * s, x2 * c + x1 * s], axis=-2)
    return out.reshape(shp).astype(x.dtype)


def gqa_mixer(h, w_in, q_g, k_g, w_out):
    b, S, _ = h.shape
    u = h @ w_in
    q = u[..., :AT_QW].reshape(b, S, AT_HEADS, AT_HD)
    k = u[..., AT_QW:AT_QW + AT_KW].reshape(b, S, AT_KV, AT_HD)
    v = u[..., AT_QW + AT_KW:AT_QW + 2 * AT_KW].reshape(b, S, AT_KV, AT_HD)
    gate = u[..., AT_QW + 2 * AT_KW:]
    cos, sin = axial_rope_tables(S)
    q = apply_axial_rope(rms_norm(q, q_g), cos, sin) * (AT_HD ** -0.5)
    k = apply_axial_rope(rms_norm(k, k_g), cos, sin)
    nq = S // AT_QBLK
    qb = q.reshape(b, nq, AT_QBLK, AT_KV, AT_GRP, AT_HD).transpose(1, 0, 2, 3, 4, 5)

    def block(qi):
        s = jnp.einsum('bqkgd,bskd->bkgqs', qi, k).astype(F32)
        p = jax.nn.softmax(s, axis=-1).astype(v.dtype)
        return jnp.einsum('bkgqs,bskd->bqkgd', p, v)

    o = lax.map(block, qb)
    o = o.transpose(1, 0, 2, 3, 4, 5).reshape(b, S, AT_QW)
    return (o * jax.nn.silu(gate)) @ w_out


def t5_bucket(rel):
    half = REL_BUCKETS // 2
    exact = half // 2
    n = jnp.abs(rel)
    large = exact + (jnp.log(jnp.maximum(n, 1).astype(F32) / exact)
                     / math.log(REL_MAX_DIST / exact) * (half - exact)).astype(jnp.int32)
    large = jnp.minimum(large, half - 1)
    return jnp.where(rel > 0, half, 0) + jnp.where(n < exact, n, large)


def dilated_group(q, k, v, dil, steps, rel_bias):
    b, S, h, e = q.shape
    Ls = S // dil
    blk = steps
    nb = -(-Ls // blk)
    Lp = nb * blk

    def sub(t):
        return t.reshape(b, Ls, dil, h, e).transpose(0, 2, 1, 3, 4)

    qb = jnp.pad(sub(q), ((0, 0), (0, 0), (0, Lp - Ls), (0, 0), (0, 0))).reshape(b, dil, nb, blk, h, e)
    kpad = ((0, 0), (0, 0), (blk, Lp - Ls + blk), (0, 0), (0, 0))
    kp = jnp.pad(sub(k), kpad).reshape(b, dil, nb + 2, blk, h, e)
    vp = jnp.pad(sub(v), kpad).reshape(b, dil, nb + 2, blk, h, e)
    kb = jnp.concatenate([kp[:, :, 0:nb], kp[:, :, 1:nb + 1], kp[:, :, 2:nb + 2]], axis=3)
    vb = jnp.concatenate([vp[:, :, 0:nb], vp[:, :, 1:nb + 1], vp[:, :, 2:nb + 2]], axis=3)
    i = jnp.arange(blk)[:, None]
    j = jnp.arange(3 * blk)[None, :]
    dm = j - blk - i
    m_k = jnp.arange(nb)[:, None, None] * blk + j[None] - blk
    mask = (jnp.abs(dm) <= steps)[None] & (m_k >= 0) & (m_k < Ls)
    bias = rel_bias[t5_bucket(dm * dil)].transpose(2, 0, 1).astype(F32)
    s = jnp.einsum('bdnqhe,bdnkhe->bdnhqk', qb, kb).astype(F32) + bias[None, None, None]
    s = jnp.where(mask[None, None, :, None], s, NEG_BIG)
    lse = jax.nn.logsumexp(s, axis=-1)
    p = jnp.exp(s - lse[..., None]).astype(v.dtype)
    o = jnp.einsum('bdnhqk,bdnkhe->bdnqhe', p, vb)
    o = o.reshape(b, dil, Lp, h, e)[:, :, :Ls].transpose(0, 2, 1, 3, 4).reshape(b, S, h, e)
    lse = lse.transpose(0, 1, 2, 4, 3).reshape(b, dil, Lp, h)[:, :, :Ls]
    lse = lse.transpose(0, 2, 1, 3).reshape(b, S, h)
    return o, lse


def dilated_mixer(h, rel_bias, w_in, w_out):
    b, S, _ = h.shape
    u = h @ w_in
    outs, lses = [], []
    for gi, (window, dil) in enumerate(DL_PAIRS):
        base = gi * 3 * DL_W
        q, k, v = [u[..., base + c * DL_W: base + (c + 1) * DL_W].reshape(b, S, DL_HEADS, DL_HD)
                   for c in range(3)]
        o, l = dilated_group(q * (DL_HD ** -0.5), k, v, dil, (window // 2) // dil, rel_bias)
        outs.append(o)
        lses.append(l)
    w = jax.nn.softmax(jnp.stack(lses), axis=0)
    o = jnp.sum(w[..., None] * jnp.stack(outs).astype(F32), axis=0)
    o = o.astype(h.dtype).reshape(b, S, DL_W)
    gate = u[..., 3 * len(DL_PAIRS) * DL_W:]
    return (o * jax.nn.silu(gate)) @ w_out


def _dense(key, shape, fan_in):
    return jax.random.normal(key, shape, F32) * (fan_in ** -0.5)


def _gain(key, shape):
    return 1.0 + 0.05 * jax.random.normal(key, shape, F32)


def _fwd_setup_inputs(seed: int = 0) -> dict:
    key = jax.random.key(seed)
    k = jax.random.split(key, 24)
    nA, nB, nC, nD = N_SSD_LAYERS, N_HG_LAYERS, N_AT_LAYERS, N_DL_LAYERS
    dt0 = jnp.exp(jax.random.uniform(k[8], (nA, 2, SSD_HEADS), F32,
                                     minval=math.log(1e-3), maxval=math.log(1e-1)))
    return {
        "x": jax.random.normal(k[0], (BATCH, SEQ, D_MODEL), F32),
        "norm_g": _gain(k[1], (DEPTH, D_MODEL)),
        "final_g": _gain(k[2], (D_MODEL,)),
        "rel_bias": 0.5 * jax.random.normal(k[3], (REL_BUCKETS, DL_HEADS), F32),
        "hgrn_lb": 0.5 * jax.random.normal(k[4], (DEPTH, HG_W), F32),
        "ssd_w_in": _dense(k[5], (nA, D_MODEL, SSD_IN), D_MODEL),
        "ssd_conv_w": _dense(k[6], (nA, SSD_CONV, SSD_CONV_CH), SSD_CONV),
        "ssd_conv_b": 0.02 * jax.random.normal(k[7], (nA, SSD_CONV_CH), F32),
        "ssd_dt_bias": dt0 + jnp.log(-jnp.expm1(-dt0)),
        "ssd_a_log": jnp.log(jax.random.uniform(k[9], (nA, 2, SSD_HEADS), F32, minval=1.0, maxval=16.0)),
        "ssd_d": 1.0 + 0.1 * jax.random.normal(k[10], (nA, SSD_HEADS), F32),
        "ssd_norm_g": _gain(k[11], (nA, SSD_DI)),
        "ssd_w_out": _dense(k[12], (nA, SSD_DI, D_MODEL), SSD_DI),
        "hg_w_in": _dense(k[13], (nB, D_MODEL, HG_IN), D_MODEL),
        "hg_norm_g": _gain(k[14], (nB, HG_W)),
        "hg_w_out": _dense(k[15], (nB, HG_W, D_MODEL), HG_W),
        "at_w_in": _dense(k[16], (nC, D_MODEL, AT_IN), D_MODEL),
        "at_q_norm_g": _gain(k[17], (nC, AT_HD)),
        "at_k_norm_g": _gain(k[18], (nC, AT_HD)),
        "at_w_out": _dense(k[19], (nC, AT_QW, D_MODEL), AT_QW),
        "dl_w_in": _dense(k[20], (nD, D_MODEL, DL_IN), D_MODEL),
        "dl_w_out": _dense(k[21], (nD, DL_W, D_MODEL), DL_W),
    }


def _fwd_reference(x, norm_g, final_g, rel_bias, hgrn_lb,
              ssd_w_in, ssd_conv_w, ssd_conv_b, ssd_dt_bias, ssd_a_log, ssd_d, ssd_norm_g, ssd_w_out,
              hg_w_in, hg_norm_g, hg_w_out,
              at_w_in, at_q_norm_g, at_k_norm_g, at_w_out,
              dl_w_in, dl_w_out):
    lb_sm = jax.nn.softmax(hgrn_lb.astype(F32), axis=0)
    lb_all = jnp.cumsum(lb_sm, axis=0) - lb_sm[0:1]
    for layer in range(DEPTH):
        kind = layer % N_MIXERS
        slot = layer // N_MIXERS
        hn = rms_norm(x, norm_g[layer])
        if kind == 0:
            y = ssd_mixer(hn, ssd_w_in[slot], ssd_conv_w[slot], ssd_conv_b[slot], ssd_dt_bias[slot],
                          ssd_a_log[slot], ssd_d[slot], ssd_norm_g[slot], ssd_w_out[slot])
        elif kind == 1:
            y = hgrn2_mixer(hn, lb_all[layer], hg_w_in[slot], hg_norm_g[slot], hg_w_out[slot])
        elif kind == 2:
            y = gqa_mixer(hn, at_w_in[slot], at_q_norm_g[slot], at_k_norm_g[slot], at_w_out[slot])
        else:
            y = dilated_mixer(hn, rel_bias, dl_w_in[slot], dl_w_out[slot])
        x = x + y.astype(x.dtype)
    return rms_norm(x, final_g)


import jax as _jax
import jax.numpy as _jnp

TWIN_FORMAT = 'train_step'
FWD_PARAMS = ['x', 'norm_g', 'final_g', 'rel_bias', 'hgrn_lb', 'ssd_w_in', 'ssd_conv_w', 'ssd_conv_b', 'ssd_dt_bias', 'ssd_a_log', 'ssd_d', 'ssd_norm_g', 'ssd_w_out', 'hg_w_in', 'hg_norm_g', 'hg_w_out', 'at_w_in', 'at_q_norm_g', 'at_k_norm_g', 'at_w_out', 'dl_w_in', 'dl_w_out']
TWIN_WEIGHTS = ['norm_g', 'final_g', 'rel_bias', 'hgrn_lb', 'ssd_w_in', 'ssd_conv_w', 'ssd_conv_b', 'ssd_dt_bias', 'ssd_a_log', 'ssd_d', 'ssd_norm_g', 'ssd_w_out', 'hg_w_in', 'hg_norm_g', 'hg_w_out', 'at_w_in', 'at_q_norm_g', 'at_k_norm_g', 'at_w_out', 'dl_w_in', 'dl_w_out']
TWIN_DIFF_INPUT = 'x'
TWIN_INPUTS = ['x', 'norm_g', 'final_g', 'rel_bias', 'hgrn_lb', 'ssd_w_in', 'ssd_conv_w', 'ssd_conv_b', 'ssd_dt_bias', 'ssd_a_log', 'ssd_d', 'ssd_norm_g', 'ssd_w_out', 'hg_w_in', 'hg_norm_g', 'hg_w_out', 'at_w_in', 'at_q_norm_g', 'at_k_norm_g', 'at_w_out', 'dl_w_in', 'dl_w_out', 'loss_target', 'm_norm_g', 'm_final_g', 'm_rel_bias', 'm_hgrn_lb', 'm_ssd_w_in', 'm_ssd_conv_w', 'm_ssd_conv_b', 'm_ssd_dt_bias', 'm_ssd_a_log', 'm_ssd_d', 'm_ssd_norm_g', 'm_ssd_w_out', 'm_hg_w_in', 'm_hg_norm_g', 'm_hg_w_out', 'm_at_w_in', 'm_at_q_norm_g', 'm_at_k_norm_g', 'm_at_w_out', 'm_dl_w_in', 'm_dl_w_out', 'v_norm_g', 'v_final_g', 'v_rel_bias', 'v_hgrn_lb', 'v_ssd_w_in', 'v_ssd_conv_w', 'v_ssd_conv_b', 'v_ssd_dt_bias', 'v_ssd_a_log', 'v_ssd_d', 'v_ssd_norm_g', 'v_ssd_w_out', 'v_hg_w_in', 'v_hg_norm_g', 'v_hg_w_out', 'v_at_w_in', 'v_at_q_norm_g', 'v_at_k_norm_g', 'v_at_w_out', 'v_dl_w_in', 'v_dl_w_out']
TWIN_OUTPUTS = ['loss', 'grad_x', 'grad_norm_g', 'grad_final_g', 'grad_rel_bias', 'grad_hgrn_lb', 'grad_ssd_w_in', 'grad_ssd_conv_w', 'grad_ssd_conv_b', 'grad_ssd_dt_bias', 'grad_ssd_a_log', 'grad_ssd_d', 'grad_ssd_norm_g', 'grad_ssd_w_out', 'grad_hg_w_in', 'grad_hg_norm_g', 'grad_hg_w_out', 'grad_at_w_in', 'grad_at_q_norm_g', 'grad_at_k_norm_g', 'grad_at_w_out', 'grad_dl_w_in', 'grad_dl_w_out', 'delta_norm_g', 'delta_final_g', 'delta_rel_bias', 'delta_hgrn_lb', 'delta_ssd_w_in', 'delta_ssd_conv_w', 'delta_ssd_conv_b', 'delta_ssd_dt_bias', 'delta_ssd_a_log', 'delta_ssd_d', 'delta_ssd_norm_g', 'delta_ssd_w_out', 'delta_hg_w_in', 'delta_hg_norm_g', 'delta_hg_w_out', 'delta_at_w_in', 'delta_at_q_norm_g', 'delta_at_k_norm_g', 'delta_at_w_out', 'delta_dl_w_in', 'delta_dl_w_out', 'new_m_norm_g', 'new_m_final_g', 'new_m_rel_bias', 'new_m_hgrn_lb', 'new_m_ssd_w_in', 'new_m_ssd_conv_w', 'new_m_ssd_conv_b', 'new_m_ssd_dt_bias', 'new_m_ssd_a_log', 'new_m_ssd_d', 'new_m_ssd_norm_g', 'new_m_ssd_w_out', 'new_m_hg_w_in', 'new_m_hg_norm_g', 'new_m_hg_w_out', 'new_m_at_w_in', 'new_m_at_q_norm_g', 'new_m_at_k_norm_g', 'new_m_at_w_out', 'new_m_dl_w_in', 'new_m_dl_w_out', 'new_v_norm_g', 'new_v_final_g', 'new_v_rel_bias', 'new_v_hgrn_lb', 'new_v_ssd_w_in', 'new_v_ssd_conv_w', 'new_v_ssd_conv_b', 'new_v_ssd_dt_bias', 'new_v_ssd_a_log', 'new_v_ssd_d', 'new_v_ssd_norm_g', 'new_v_ssd_w_out', 'new_v_hg_w_in', 'new_v_hg_norm_g', 'new_v_hg_w_out', 'new_v_at_w_in', 'new_v_at_q_norm_g', 'new_v_at_k_norm_g', 'new_v_at_w_out', 'new_v_dl_w_in', 'new_v_dl_w_out']
TWIN_LEAF_KINDS = {'loss': 'loss', 'grad_x': 'grad_x', 'grad_norm_g': 'grad_w', 'grad_final_g': 'grad_w', 'grad_rel_bias': 'grad_w', 'grad_hgrn_lb': 'grad_w', 'grad_ssd_w_in': 'grad_w', 'grad_ssd_conv_w': 'grad_w', 'grad_ssd_conv_b': 'grad_w', 'grad_ssd_dt_bias': 'grad_w', 'grad_ssd_a_log': 'grad_w', 'grad_ssd_d': 'grad_w', 'grad_ssd_norm_g': 'grad_w', 'grad_ssd_w_out': 'grad_w', 'grad_hg_w_in': 'grad_w', 'grad_hg_norm_g': 'grad_w', 'grad_hg_w_out': 'grad_w', 'grad_at_w_in': 'grad_w', 'grad_at_q_norm_g': 'grad_w', 'grad_at_k_norm_g': 'grad_w', 'grad_at_w_out': 'grad_w', 'grad_dl_w_in': 'grad_w', 'grad_dl_w_out': 'grad_w', 'delta_norm_g': 'delta_w', 'delta_final_g': 'delta_w', 'delta_rel_bias': 'delta_w', 'delta_hgrn_lb': 'delta_w', 'delta_ssd_w_in': 'delta_w', 'delta_ssd_conv_w': 'delta_w', 'delta_ssd_conv_b': 'delta_w', 'delta_ssd_dt_bias': 'delta_w', 'delta_ssd_a_log': 'delta_w', 'delta_ssd_d': 'delta_w', 'delta_ssd_norm_g': 'delta_w', 'delta_ssd_w_out': 'delta_w', 'delta_hg_w_in': 'delta_w', 'delta_hg_norm_g': 'delta_w', 'delta_hg_w_out': 'delta_w', 'delta_at_w_in': 'delta_w', 'delta_at_q_norm_g': 'delta_w', 'delta_at_k_norm_g': 'delta_w', 'delta_at_w_out': 'delta_w', 'delta_dl_w_in': 'delta_w', 'delta_dl_w_out': 'delta_w', 'new_m_norm_g': 'new_m', 'new_m_final_g': 'new_m', 'new_m_rel_bias': 'new_m', 'new_m_hgrn_lb': 'new_m', 'new_m_ssd_w_in': 'new_m', 'new_m_ssd_conv_w': 'new_m', 'new_m_ssd_conv_b': 'new_m', 'new_m_ssd_dt_bias': 'new_m', 'new_m_ssd_a_log': 'new_m', 'new_m_ssd_d': 'new_m', 'new_m_ssd_norm_g': 'new_m', 'new_m_ssd_w_out': 'new_m', 'new_m_hg_w_in': 'new_m', 'new_m_hg_norm_g': 'new_m', 'new_m_hg_w_out': 'new_m', 'new_m_at_w_in': 'new_m', 'new_m_at_q_norm_g': 'new_m', 'new_m_at_k_norm_g': 'new_m', 'new_m_at_w_out': 'new_m', 'new_m_dl_w_in': 'new_m', 'new_m_dl_w_out': 'new_m', 'new_v_norm_g': 'new_v', 'new_v_final_g': 'new_v', 'new_v_rel_bias': 'new_v', 'new_v_hgrn_lb': 'new_v', 'new_v_ssd_w_in': 'new_v', 'new_v_ssd_conv_w': 'new_v', 'new_v_ssd_conv_b': 'new_v', 'new_v_ssd_dt_bias': 'new_v', 'new_v_ssd_a_log': 'new_v', 'new_v_ssd_d': 'new_v', 'new_v_ssd_norm_g': 'new_v', 'new_v_ssd_w_out': 'new_v', 'new_v_hg_w_in': 'new_v', 'new_v_hg_norm_g': 'new_v', 'new_v_hg_w_out': 'new_v', 'new_v_at_w_in': 'new_v', 'new_v_at_q_norm_g': 'new_v', 'new_v_at_k_norm_g': 'new_v', 'new_v_at_w_out': 'new_v', 'new_v_dl_w_in': 'new_v', 'new_v_dl_w_out': 'new_v'}


def _forward(args):
    return _fwd_reference(*[args[k] for k in FWD_PARAMS])


def _output_shape():
    def fwd():
        inp = _fwd_setup_inputs(0)
        return _fwd_reference(*[inp[k] for k in FWD_PARAMS])
    out = _jax.eval_shape(fwd)
    return out.shape, out.dtype

N_MICROBATCH = 1
ADAM_LR = 0.001
ADAM_B1 = 0.9
ADAM_B2 = 0.999
ADAM_EPS = 1e-08
ADAM_WD = 0.01
ADAM_STEP = 10
PER_EXAMPLE_BATCH_AXIS = {'x': 0, 'loss_target': 0}
SHARED_INPUTS = []
_WEIGHT_DTYPES = {'norm_g': _jnp.float32, 'final_g': _jnp.float32, 'rel_bias': _jnp.float32, 'hgrn_lb': _jnp.float32, 'ssd_w_in': _jnp.float32, 'ssd_conv_w': _jnp.float32, 'ssd_conv_b': _jnp.float32, 'ssd_dt_bias': _jnp.float32, 'ssd_a_log': _jnp.float32, 'ssd_d': _jnp.float32, 'ssd_norm_g': _jnp.float32, 'ssd_w_out': _jnp.float32, 'hg_w_in': _jnp.float32, 'hg_norm_g': _jnp.float32, 'hg_w_out': _jnp.float32, 'at_w_in': _jnp.float32, 'at_q_norm_g': _jnp.float32, 'at_k_norm_g': _jnp.float32, 'at_w_out': _jnp.float32, 'dl_w_in': _jnp.float32, 'dl_w_out': _jnp.float32}
MOMENT_SCALE = {'norm_g': 1.147006e-01, 'final_g': 3.205105e+01, 'rel_bias': 1.031477e-02, 'hgrn_lb': 1.999636e-03, 'ssd_w_in': 8.906407e-02, 'ssd_conv_w': 8.025720e-02, 'ssd_conv_b': 1.714453e-01, 'ssd_dt_bias': 1.803151e-01, 'ssd_a_log': 2.247369e-01, 'ssd_d': 4.414051e-01, 'ssd_norm_g': 1.100004e-01, 'ssd_w_out': 1.532741e-01, 'hg_w_in': 4.604826e-02, 'hg_norm_g': 7.572005e-02, 'hg_w_out': 7.393779e-02, 'at_w_in': 8.795568e-03, 'at_q_norm_g': 1.982187e-02, 'at_k_norm_g': 1.963371e-02, 'at_w_out': 1.389991e-02, 'dl_w_in': 6.542713e-03, 'dl_w_out': 1.182906e-02}


def _to_microbatches(a, axis):
    t = _jnp.moveaxis(a, axis, 0)
    t = t.reshape((N_MICROBATCH, t.shape[0] // N_MICROBATCH) + t.shape[1:])
    return _jnp.moveaxis(t, 1, axis + 1)


def setup_inputs(seed: int = 0) -> dict:
    inp = _fwd_setup_inputs(seed)
    key = _jax.random.fold_in(_jax.random.key(seed), 7919)
    shape, _ = _output_shape()
    out = dict(inp)
    out["loss_target"] = _jax.random.normal(_jax.random.fold_in(key, 0), shape, _jnp.float32)
    for i, name in enumerate(TWIN_WEIGHTS):
        w = inp[name].astype(_jnp.float32)
        if MOMENT_SCALE is None:
            s = _jnp.sqrt(_jnp.mean(_jnp.square(w)) + 1e-30)
        else:
            s = MOMENT_SCALE[name]
        km, kv = _jax.random.split(_jax.random.fold_in(key, i + 1))
        out[name] = w
        out["m_" + name] = s * _jax.random.normal(km, w.shape, _jnp.float32)
        out["v_" + name] = (s * s) * _jax.random.uniform(kv, w.shape, _jnp.float32, 0.5, 1.5)
    if N_MICROBATCH > 1:
        for name, axis in PER_EXAMPLE_BATCH_AXIS.items():
            out[name] = _to_microbatches(out[name], axis)
    return {'x': out['x'], 'norm_g': out['norm_g'], 'final_g': out['final_g'], 'rel_bias': out['rel_bias'], 'hgrn_lb': out['hgrn_lb'], 'ssd_w_in': out['ssd_w_in'], 'ssd_conv_w': out['ssd_conv_w'], 'ssd_conv_b': out['ssd_conv_b'], 'ssd_dt_bias': out['ssd_dt_bias'], 'ssd_a_log': out['ssd_a_log'], 'ssd_d': out['ssd_d'], 'ssd_norm_g': out['ssd_norm_g'], 'ssd_w_out': out['ssd_w_out'], 'hg_w_in': out['hg_w_in'], 'hg_norm_g': out['hg_norm_g'], 'hg_w_out': out['hg_w_out'], 'at_w_in': out['at_w_in'], 'at_q_norm_g': out['at_q_norm_g'], 'at_k_norm_g': out['at_k_norm_g'], 'at_w_out': out['at_w_out'], 'dl_w_in': out['dl_w_in'], 'dl_w_out': out['dl_w_out'], 'loss_target': out['loss_target'], 'm_norm_g': out['m_norm_g'], 'm_final_g': out['m_final_g'], 'm_rel_bias': out['m_rel_bias'], 'm_hgrn_lb': out['m_hgrn_lb'], 'm_ssd_w_in': out['m_ssd_w_in'], 'm_ssd_conv_w': out['m_ssd_conv_w'], 'm_ssd_conv_b': out['m_ssd_conv_b'], 'm_ssd_dt_bias': out['m_ssd_dt_bias'], 'm_ssd_a_log': out['m_ssd_a_log'], 'm_ssd_d': out['m_ssd_d'], 'm_ssd_norm_g': out['m_ssd_norm_g'], 'm_ssd_w_out': out['m_ssd_w_out'], 'm_hg_w_in': out['m_hg_w_in'], 'm_hg_norm_g': out['m_hg_norm_g'], 'm_hg_w_out': out['m_hg_w_out'], 'm_at_w_in': out['m_at_w_in'], 'm_at_q_norm_g': out['m_at_q_norm_g'], 'm_at_k_norm_g': out['m_at_k_norm_g'], 'm_at_w_out': out['m_at_w_out'], 'm_dl_w_in': out['m_dl_w_in'], 'm_dl_w_out': out['m_dl_w_out'], 'v_norm_g': out['v_norm_g'], 'v_final_g': out['v_final_g'], 'v_rel_bias': out['v_rel_bias'], 'v_hgrn_lb': out['v_hgrn_lb'], 'v_ssd_w_in': out['v_ssd_w_in'], 'v_ssd_conv_w': out['v_ssd_conv_w'], 'v_ssd_conv_b': out['v_ssd_conv_b'], 'v_ssd_dt_bias': out['v_ssd_dt_bias'], 'v_ssd_a_log': out['v_ssd_a_log'], 'v_ssd_d': out['v_ssd_d'], 'v_ssd_norm_g': out['v_ssd_norm_g'], 'v_ssd_w_out': out['v_ssd_w_out'], 'v_hg_w_in': out['v_hg_w_in'], 'v_hg_norm_g': out['v_hg_norm_g'], 'v_hg_w_out': out['v_hg_w_out'], 'v_at_w_in': out['v_at_w_in'], 'v_at_q_norm_g': out['v_at_q_norm_g'], 'v_at_k_norm_g': out['v_at_k_norm_g'], 'v_at_w_out': out['v_at_w_out'], 'v_dl_w_in': out['v_dl_w_in'], 'v_dl_w_out': out['v_dl_w_out']}


def _loss(weights, diff, rest, loss_target):
    with _jax.named_scope("forward"):
        args = {**rest, TWIN_DIFF_INPUT: diff, **{k: w.astype(_WEIGHT_DTYPES[k]) for k, w in weights.items()}}
        y = _forward(args)
    with _jax.named_scope("loss_head"):
        err = _jnp.square(y.astype(_jnp.float32) - loss_target)
        return 0.5 * _jnp.sum(_jnp.mean(err, axis=-1)) if err.ndim else 0.5 * err


def _adamw(w, g, m, v):
    m = ADAM_B1 * m + (1.0 - ADAM_B1) * g
    v = ADAM_B2 * v + (1.0 - ADAM_B2) * _jnp.square(g)
    m_hat = m / (1.0 - ADAM_B1 ** ADAM_STEP)
    v_hat = v / (1.0 - ADAM_B2 ** ADAM_STEP)
    delta = -ADAM_LR * (m_hat / (_jnp.sqrt(v_hat) + ADAM_EPS) + ADAM_WD * w)
    return delta, m, v


def reference(x, norm_g, final_g, rel_bias, hgrn_lb, ssd_w_in, ssd_conv_w, ssd_conv_b, ssd_dt_bias, ssd_a_log, ssd_d, ssd_norm_g, ssd_w_out, hg_w_in, hg_norm_g, hg_w_out, at_w_in, at_q_norm_g, at_k_norm_g, at_w_out, dl_w_in, dl_w_out, loss_target, m_norm_g, m_final_g, m_rel_bias, m_hgrn_lb, m_ssd_w_in, m_ssd_conv_w, m_ssd_conv_b, m_ssd_dt_bias, m_ssd_a_log, m_ssd_d, m_ssd_norm_g, m_ssd_w_out, m_hg_w_in, m_hg_norm_g, m_hg_w_out, m_at_w_in, m_at_q_norm_g, m_at_k_norm_g, m_at_w_out, m_dl_w_in, m_dl_w_out, v_norm_g, v_final_g, v_rel_bias, v_hgrn_lb, v_ssd_w_in, v_ssd_conv_w, v_ssd_conv_b, v_ssd_dt_bias, v_ssd_a_log, v_ssd_d, v_ssd_norm_g, v_ssd_w_out, v_hg_w_in, v_hg_norm_g, v_hg_w_out, v_at_w_in, v_at_q_norm_g, v_at_k_norm_g, v_at_w_out, v_dl_w_in, v_dl_w_out):
    given = dict(x=x, norm_g=norm_g, final_g=final_g, rel_bias=rel_bias, hgrn_lb=hgrn_lb, ssd_w_in=ssd_w_in, ssd_conv_w=ssd_conv_w, ssd_conv_b=ssd_conv_b, ssd_dt_bias=ssd_dt_bias, ssd_a_log=ssd_a_log, ssd_d=ssd_d, ssd_norm_g=ssd_norm_g, ssd_w_out=ssd_w_out, hg_w_in=hg_w_in, hg_norm_g=hg_norm_g, hg_w_out=hg_w_out, at_w_in=at_w_in, at_q_norm_g=at_q_norm_g, at_k_norm_g=at_k_norm_g, at_w_out=at_w_out, dl_w_in=dl_w_in, dl_w_out=dl_w_out, loss_target=loss_target, m_norm_g=m_norm_g, m_final_g=m_final_g, m_rel_bias=m_rel_bias, m_hgrn_lb=m_hgrn_lb, m_ssd_w_in=m_ssd_w_in, m_ssd_conv_w=m_ssd_conv_w, m_ssd_conv_b=m_ssd_conv_b, m_ssd_dt_bias=m_ssd_dt_bias, m_ssd_a_log=m_ssd_a_log, m_ssd_d=m_ssd_d, m_ssd_norm_g=m_ssd_norm_g, m_ssd_w_out=m_ssd_w_out, m_hg_w_in=m_hg_w_in, m_hg_norm_g=m_hg_norm_g, m_hg_w_out=m_hg_w_out, m_at_w_in=m_at_w_in, m_at_q_norm_g=m_at_q_norm_g, m_at_k_norm_g=m_at_k_norm_g, m_at_w_out=m_at_w_out, m_dl_w_in=m_dl_w_in, m_dl_w_out=m_dl_w_out, v_norm_g=v_norm_g, v_final_g=v_final_g, v_rel_bias=v_rel_bias, v_hgrn_lb=v_hgrn_lb, v_ssd_w_in=v_ssd_w_in, v_ssd_conv_w=v_ssd_conv_w, v_ssd_conv_b=v_ssd_conv_b, v_ssd_dt_bias=v_ssd_dt_bias, v_ssd_a_log=v_ssd_a_log, v_ssd_d=v_ssd_d, v_ssd_norm_g=v_ssd_norm_g, v_ssd_w_out=v_ssd_w_out, v_hg_w_in=v_hg_w_in, v_hg_norm_g=v_hg_norm_g, v_hg_w_out=v_hg_w_out, v_at_w_in=v_at_w_in, v_at_q_norm_g=v_at_q_norm_g, v_at_k_norm_g=v_at_k_norm_g, v_at_w_out=v_at_w_out, v_dl_w_in=v_dl_w_in, v_dl_w_out=v_dl_w_out)
    weights = {n: given[n] for n in TWIN_WEIGHTS}
    shared = {n: given[n] for n in SHARED_INPUTS}
    per_example = {n: given[n] for n in ['x']}
    grad_fn = _jax.value_and_grad(_loss, argnums=(0, 1))

    def one_microbatch(ex, loss_target):
        ex = dict(ex)
        diff = ex.pop(TWIN_DIFF_INPUT)
        return grad_fn(weights, diff, {**shared, **ex}, loss_target)

    if N_MICROBATCH == 1:
        loss, (grad_w, grad_x) = one_microbatch(per_example, given["loss_target"])
    else:
        def body(carry, xs):
            loss_sum, grad_sum = carry
            l_k, (gw_k, gx_k) = one_microbatch(xs[0], xs[1])
            with _jax.named_scope("update"):
                return (loss_sum + l_k, _jax.tree.map(_jnp.add, grad_sum, gw_k)), gx_k

        init = (_jnp.zeros((), _jnp.float32), _jax.tree.map(_jnp.zeros_like, weights))
        (loss, grad_w), grad_x = _jax.lax.scan(body, init, (per_example, given["loss_target"]))
    with _jax.named_scope("update"):
        delta_w, new_m, new_v = {}, {}, {}
        for n in TWIN_WEIGHTS:
            delta_w[n], new_m[n], new_v[n] = _adamw(weights[n], grad_w[n], given["m_" + n], given["v_" + n])
    return (loss, grad_x, *[grad_w[n] for n in TWIN_WEIGHTS], *[delta_w[n] for n in TWIN_WEIGHTS],
            *[new_m[n] for n in TWIN_WEIGHTS], *[new_v[n] for n in TWIN_WEIGHTS])
```

```python
import functools
import math

import jax
import jax.numpy as jnp
import numpy as np
from jax import lax
from jax.experimental import pallas as pl
from jax.experimental.pallas import tpu as pltpu

F32 = jnp.float32
BF16 = jnp.bfloat16
MXU_DTYPE = jnp.bfloat16
HIGHEST = lax.Precision.HIGHEST
MESH_ID = pl.DeviceIdType.MESH
N_DEV = 8

D_MODEL = 1024
EPS = 1e-6
NEG_BIG = -1e30

SSD_DI = 2048
SSD_HEADDIM = 64
SSD_HEADS = 32
SSD_GROUPS = 4
SSD_HPG = 8
SSD_STATE = 128
SSD_CONV = 7
SSD_CHUNK = 128
SSD_CONV_CH = SSD_DI + 2 * SSD_GROUPS * SSD_STATE
SSD_IN = SSD_DI + SSD_CONV_CH + 2 * SSD_HEADS
SSD_IN_PAD = 5376

HG_CHUNK = 32
HG_HEADS = 8
HG_D = 128
HG_W = 1024

AT_HEADS = 16
AT_KV = 8
AT_HD = 128
AT_QW = 2048
AT_KW = 1024
GRID_W = 64
ROPE_THETA = 10000.0

DL_PAIRS = ((128, 1), (512, 4), (2048, 16))
DL_HEADS = 16
DL_HD = 64
DL_W = 1024
DL_HALF = 64
REL_BUCKETS = 32
REL_MAX_DIST = 1024

ADAM_LR = 0.001
ADAM_B1 = 0.9
ADAM_B2 = 0.999
ADAM_EPS = 1e-08
ADAM_WD = 0.01
ADAM_STEP = 10

VMEM_LIMIT = 56 * 1024 * 1024


def _cp(*sem):
    return pltpu.CompilerParams(dimension_semantics=tuple(sem), vmem_limit_bytes=VMEM_LIMIT)


def _tile(n, cands=(1024, 768, 512, 384, 256, 128)):
    for c in cands:
        if n % c == 0:
            return c
    return n


def _dot(a, b, dims):
    return lax.dot_general(a.astype(MXU_DTYPE), b.astype(MXU_DTYPE), (dims, ((), ())), preferred_element_type=F32)


def _dot_exact(a, b, dims):
    return lax.dot_general(a, b, (dims, ((), ())), precision=HIGHEST, preferred_element_type=F32)


def _silu(x):
    return x * jax.nn.sigmoid(x)


def matmul(a, b, *, name, ta=False, tb=False, residual=None, out_dtype=F32, exact=False, tm=None, tn=None, tk=None):
    M, K = (a.shape[1], a.shape[0]) if ta else a.shape
    N = b.shape[0] if tb else b.shape[1]
    tm = tm or _tile(M, (512, 256, 128))
    tn = tn or _tile(N, (1024, 768, 512, 384, 256, 128))
    tk = tk or _tile(K, (1024, 768, 512, 384, 256, 128))
    nk = K // tk
    dims = (((0,) if ta else (1,)), ((1,) if tb else (0,)))

    def body(*refs):
        if residual is None:
            a_ref, b_ref, o_ref, acc = refs
            r_ref = None
        else:
            a_ref, b_ref, r_ref, o_ref, acc = refs
        k = pl.program_id(2)

        @pl.when(k == 0)
        def _():
            acc[...] = jnp.zeros_like(acc)

        if exact:
            acc[...] += _dot_exact(a_ref[...], b_ref[...], dims)
        else:
            acc[...] += _dot(a_ref[...], b_ref[...], dims)

        @pl.when(k == nk - 1)
        def _():
            r = acc[...]
            if r_ref is not None:
                r = r + r_ref[...]
            o_ref[...] = r.astype(o_ref.dtype)

    a_spec = pl.BlockSpec((tk, tm), lambda i, j, k: (k, i)) if ta else pl.BlockSpec((tm, tk), lambda i, j, k: (i, k))
    b_spec = pl.BlockSpec((tn, tk), lambda i, j, k: (j, k)) if tb else pl.BlockSpec((tk, tn), lambda i, j, k: (k, j))
    in_specs = [a_spec, b_spec]
    args = [a, b]
    if residual is not None:
        in_specs.append(pl.BlockSpec((tm, tn), lambda i, j, k: (i, j)))
        args.append(residual)
    return pl.pallas_call(
        body, name=name, grid=(M // tm, N // tn, nk), in_specs=in_specs,
        out_specs=pl.BlockSpec((tm, tn), lambda i, j, k: (i, j)),
        out_shape=jax.ShapeDtypeStruct((M, N), out_dtype),
        scratch_shapes=[pltpu.VMEM((tm, tn), F32)],
        compiler_params=_cp("parallel", "parallel", "arbitrary"),
    )(*args)


def _row_specs(rows, shared, R, W):
    specs = []
    for arr, col0, per_j, *wd in rows:
        w = wd[0] if wd else W
        specs.append(pl.BlockSpec((R, w), (lambda j, i, c=col0: (i, c + j)) if per_j else (lambda j, i, c=col0: (i, c))))
    for arr, per_j in shared:
        specs.append(pl.BlockSpec((arr.shape[0], W), (lambda j, i: (0, j)) if per_j else (lambda j, i: (0, 0))))
    return specs


def rowwise_fwd(name, fn, rows, shared, outs, *, W, ncb=1, R=256):
    S = rows[0][0].shape[0]
    R = min(R, S)
    nr, ns = len(rows), len(shared)

    def body(*refs):
        vals = [r[...] for r in refs[:nr + ns]]
        res = fn(*vals)
        for o_ref, r in zip(refs[nr + ns:], res):
            o_ref[...] = r.astype(o_ref.dtype)

    return pl.pallas_call(
        body, name=name, grid=(ncb, S // R),
        in_specs=_row_specs(rows, shared, R, W),
        out_specs=[pl.BlockSpec((R, w), lambda j, i: (i, j)) for w, _ in outs],
        out_shape=[jax.ShapeDtypeStruct((S, ncb * w), dt) for w, dt in outs],
        compiler_params=_cp("parallel", "parallel"),
    )(*[r[0] for r in rows], *[s[0] for s in shared])


def rowwise_bwd(name, fn, rows, shared, cots, *, W, ncb=1, R=256, diff_rows, diff_shared, add=None):
    S = rows[0][0].shape[0]
    R = min(R, S)
    nr, ns, nc = len(rows), len(shared), len(cots)
    nsteps = S // R

    def body(*refs):
        ins = refs[:nr + ns]
        ct_refs = refs[nr + ns:nr + ns + nc]
        pos = nr + ns + nc
        add_ref = None
        if add is not None:
            add_ref = refs[pos]
            pos += 1
        drow_refs = refs[pos:pos + len(diff_rows)]
        dsh_refs = refs[pos + len(diff_rows):]
        j, i = pl.program_id(0), pl.program_id(1)
        vals = [r[...] for r in ins]

        def f(*dv):
            full = list(vals)
            for idx, v in zip(list(diff_rows) + [nr + s for s in diff_shared], dv):
                full[idx] = v
            return tuple(fn(*full))

        prim = [vals[idx] for idx in diff_rows] + [vals[nr + s] for s in diff_shared]
        _, vjp = jax.vjp(f, *prim)
        grads = vjp(tuple(c[...] for c in ct_refs))
        for k, d_ref in enumerate(drow_refs):
            g = grads[k]
            if k == 0 and add_ref is not None:
                g = g + add_ref[...]
            d_ref[...] = g
        for k, (d_ref, s) in enumerate(zip(dsh_refs, diff_shared)):
            g = grads[len(diff_rows) + k]
            first = (i == 0) if shared[s][1] else jnp.logical_and(i == 0, j == 0)

            @pl.when(first)
            def _(d_ref=d_ref, g=g):
                d_ref[...] = g

            @pl.when(jnp.logical_not(first))
            def _(d_ref=d_ref, g=g):
                d_ref[...] += g

    in_specs = _row_specs(rows, shared, R, W)
    wo = [c.shape[1] // ncb for c in cots]
    in_specs += [pl.BlockSpec((R, w), lambda j, i: (i, j)) for w in wo]
    args = [r[0] for r in rows] + [s[0] for s in shared] + list(cots)
    if add is not None:
        in_specs.append(pl.BlockSpec((R, W), lambda j, i: (i, j)))
        args.append(add)
    dws = [(rows[r][3] if len(rows[r]) > 3 else W) for r in diff_rows]
    out_specs = [pl.BlockSpec((R, w), lambda j, i: (i, j)) for w in dws]
    out_shape = [jax.ShapeDtypeStruct((S, ncb * w), F32) for w in dws]
    for s in diff_shared:
        arr, per_j = shared[s]
        out_specs.append(pl.BlockSpec((arr.shape[0], W), (lambda j, i: (0, j)) if per_j else (lambda j, i: (0, 0))))
        out_shape.append(jax.ShapeDtypeStruct((arr.shape[0], ncb * W if per_j else W), F32))
    return pl.pallas_call(
        body, name=name, grid=(ncb, nsteps), in_specs=in_specs, out_specs=out_specs, out_shape=out_shape,
        compiler_params=_cp("arbitrary", "arbitrary"),
    )(*args)


def _rms(x, g):
    return x * lax.rsqrt(jnp.mean(x * x, axis=-1, keepdims=True) + EPS) * g


def _prenorm_fn(x, g):
    return (_rms(x, g),)


def loss_head(x, g, tgt, *, R=256):
    S, D = x.shape
    R = min(R, S)

    def fn(xv, gv, tv):
        err = _rms(xv, gv) - tv
        return 0.5 * jnp.sum(jnp.mean(err * err, axis=-1, keepdims=True), axis=0, keepdims=True)

    def body(x_ref, g_ref, t_ref, loss_ref, dx_ref, dg_ref):
        i = pl.program_id(0)
        tv = t_ref[...]
        val, vjp = jax.vjp(lambda a, b: fn(a, b, tv), x_ref[...], g_ref[...])
        dx, dg = vjp(jnp.ones((1, 1), F32))
        dx_ref[...] = dx

        @pl.when(i == 0)
        def _():
            loss_ref[...] = jnp.zeros_like(loss_ref) + val
            dg_ref[...] = dg

        @pl.when(i != 0)
        def _():
            loss_ref[...] += val
            dg_ref[...] += dg

    return pl.pallas_call(
        body, name="loss_head", grid=(S // R,),
        in_specs=[pl.BlockSpec((R, D), lambda i: (i, 0)), pl.BlockSpec((1, D), lambda i: (0, 0)),
                  pl.BlockSpec((R, D), lambda i: (i, 0))],
        out_specs=[pl.BlockSpec((1, 128), lambda i: (0, 0)), pl.BlockSpec((R, D), lambda i: (i, 0)),
                   pl.BlockSpec((1, D), lambda i: (0, 0))],
        out_shape=[jax.ShapeDtypeStruct((1, 128), F32), jax.ShapeDtypeStruct((S, D), F32),
                   jax.ShapeDtypeStruct((1, D), F32)],
        compiler_params=_cp("arbitrary"),
    )(x, g, tgt)


@jax.custom_vjp
def _softplus(x):
    z = jnp.exp(-jnp.abs(x))
    u = 1.0 + z
    log1p = jnp.where(u == 1.0, z, jnp.log(u) * (z / jnp.where(u == 1.0, 1.0, u - 1.0)))
    return jnp.maximum(x, 0.0) + log1p


def _softplus_fwd(x):
    return _softplus(x), x


def _softplus_bwd(x, ct):
    return (ct * jax.nn.sigmoid(x),)


_softplus.defvjp(_softplus_fwd, _softplus_bwd)


def _dt_fn(raw, bias):
    return (_softplus(raw + bias),)


CONV_CB = 256
CONV_RB = 512
CONV_PAD = 8


def ssd_conv_fwd(u, conv_w, conv_b):
    S = u.shape[0]
    ncb = SSD_CONV_CH // CONV_CB
    col0 = SSD_DI // CONV_CB
    RB = min(CONV_RB, S)

    def body(x_ref, w_ref, b_ref, o_ref, pad):
        pad[0:CONV_PAD, :] = jnp.zeros((CONV_PAD, CONV_CB), F32)
        pad[S + CONV_PAD:S + 2 * CONV_PAD, :] = jnp.zeros((CONV_PAD, CONV_CB), F32)
        pad[CONV_PAD:S + CONV_PAD, :] = x_ref[...]
        w = w_ref[...]
        b = b_ref[...]
        for r in range(S // RB):
            acc = jnp.zeros((RB, CONV_CB), F32) + b
            for k in range(SSD_CONV):
                off = r * RB + CONV_PAD + k - SSD_CONV // 2
                acc = acc + pad[off:off + RB, :] * w[k:k + 1, :]
            o_ref[r * RB:(r + 1) * RB, :] = _silu(acc)

    return pl.pallas_call(
        body, name="ssd_conv_fwd", grid=(ncb,),
        in_specs=[pl.BlockSpec((S, CONV_CB), lambda j: (0, col0 + j)),
                  pl.BlockSpec((SSD_CONV, CONV_CB), lambda j: (0, j)),
                  pl.BlockSpec((1, CONV_CB), lambda j: (0, j))],
        out_specs=pl.BlockSpec((S, CONV_CB), lambda j: (0, j)),
        out_shape=jax.ShapeDtypeStruct((S, SSD_CONV_CH), F32),
        scratch_shapes=[pltpu.VMEM((S + 2 * CONV_PAD, CONV_CB), F32)],
        compiler_params=_cp("parallel"),
    )(u, conv_w, conv_b)


def ssd_conv_bwd(u, conv_w, conv_b, dact):
    S = u.shape[0]
    ncb = SSD_CONV_CH // CONV_CB
    col0 = SSD_DI // CONV_CB
    RB = min(CONV_RB, S)
    half = SSD_CONV // 2

    def body(x_ref, w_ref, b_ref, da_ref, dx_ref, dw_ref, db_ref, xpad, dpad):
        z8 = jnp.zeros((CONV_PAD, CONV_CB), F32)
        xpad[0:CONV_PAD, :] = z8
        xpad[S + CONV_PAD:S + 2 * CONV_PAD, :] = z8
        dpad[0:CONV_PAD, :] = z8
        dpad[S + CONV_PAD:S + 2 * CONV_PAD, :] = z8
        xpad[CONV_PAD:S + CONV_PAD, :] = x_ref[...]
        w = w_ref[...]
        b = b_ref[...]
        dws = [jnp.zeros((1, CONV_CB), F32) for _ in range(SSD_CONV)]
        db = jnp.zeros((1, CONV_CB), F32)
        for r in range(S // RB):
            acc = jnp.zeros((RB, CONV_CB), F32) + b
            xs = []
            for k in range(SSD_CONV):
                off = r * RB + CONV_PAD + k - half
                xk = xpad[off:off + RB, :]
                xs.append(xk)
                acc = acc + xk * w[k:k + 1, :]
            sg = jax.nn.sigmoid(acc)
            dc = da_ref[r * RB:(r + 1) * RB, :] * (sg * (1.0 + acc * (1.0 - sg)))
            dpad[r * RB + CONV_PAD:(r + 1) * RB + CONV_PAD, :] = dc
            db = db + jnp.sum(dc, axis=0, keepdims=True)
            for k in range(SSD_CONV):
                dws[k] = dws[k] + jnp.sum(xs[k] * dc, axis=0, keepdims=True)
        for r in range(S // RB):
            acc = jnp.zeros((RB, CONV_CB), F32)
            for k in range(SSD_CONV):
                off = r * RB + CONV_PAD + half - k
                acc = acc + dpad[off:off + RB, :] * w[k:k + 1, :]
            dx_ref[r * RB:(r + 1) * RB, :] = acc
        for k in range(SSD_CONV):
            dw_ref[k:k + 1, :] = dws[k]
        dw_ref[SSD_CONV:SSD_CONV + 1, :] = jnp.zeros((1, CONV_CB), F32)
        db_ref[...] = db

    return pl.pallas_call(
        body, name="ssd_conv_bwd", grid=(ncb,),
        in_specs=[pl.BlockSpec((S, CONV_CB), lambda j: (0, col0 + j)),
                  pl.BlockSpec((SSD_CONV, CONV_CB), lambda j: (0, j)),
                  pl.BlockSpec((1, CONV_CB), lambda j: (0, j)),
                  pl.BlockSpec((S, CONV_CB), lambda j: (0, j))],
        out_specs=[pl.BlockSpec((S, CONV_CB), lambda j: (0, j)),
                   pl.BlockSpec((SSD_CONV + 1, CONV_CB), lambda j: (0, j)),
                   pl.BlockSpec((1, CONV_CB), lambda j: (0, j))],
        out_shape=[jax.ShapeDtypeStruct((S, SSD_CONV_CH), F32),
                   jax.ShapeDtypeStruct((SSD_CONV + 1, SSD_CONV_CH), F32),
                   jax.ShapeDtypeStruct((1, SSD_CONV_CH), F32)],
        scratch_shapes=[pltpu.VMEM((S + 2 * CONV_PAD, CONV_CB), F32), pltpu.VMEM((S + 2 * CONV_PAD, CONV_CB), F32)],
        compiler_params=_cp("parallel"),
    )(u, conv_w, conv_b, dact)


def _ssd_chunk(states, xs, Bg, Cg, dt, alog, sel, *, reverse):
    Q = SSD_CHUNK
    r = lax.broadcasted_iota(jnp.int32, (Q, Q), 0)
    c = lax.broadcasted_iota(jnp.int32, (Q, Q), 1)
    keep = (c >= r) if reverse else (c <= r)
    tri = jnp.where(keep, 1.0, 0.0).astype(F32)
    a = dt * (-jnp.exp(alog))
    cum = _dot_exact(tri, a, ((1,), (0,)))
    cum_t = jnp.transpose(cum)
    last = 0 if reverse else Q - 1
    CB = _dot(Cg, Bg, ((1,), (1,)))
    new_states, ys = [], []
    for h in range(SSD_HPG):
        s1 = sel[h]
        cum_c = jnp.sum(cum * s1, axis=1, keepdims=True)
        cum_r = jnp.sum(cum_t * jnp.transpose(s1), axis=0, keepdims=True)
        dt_c = jnp.sum(dt * s1, axis=1, keepdims=True)
        cum_l = cum_c[last:last + 1, :]
        diff = jnp.where(keep, cum_c - cum_r, 0.0)
        L = jnp.where(keep, jnp.exp(diff), 0.0)
        xdt = xs[h] * dt_c
        y = _dot(CB * L, xdt, ((1,), (0,)))
        y = y + _dot(Cg * jnp.exp(cum_c), states[h], ((1,), (0,)))
        ns = jnp.exp(cum_l) * states[h] + _dot(Bg * jnp.exp(cum_l - cum_c), xdt, ((0,), (0,)))
        new_states.append(ns)
        ys.append(y)
    return new_states, ys


def _ssd_sel(g, reverse):
    lane = lax.broadcasted_iota(jnp.int32, (1, 128), 1)
    base = g * SSD_HPG + (SSD_HEADS if reverse else 0)
    return [jnp.where(lane == base + h, 1.0, 0.0).astype(F32) for h in range(SSD_HPG)]


def ssd_scan_fwd(act, dt, alog, *, reverse, y_prev=None):
    S = act.shape[0]
    Q, N, P = SSD_CHUNK, SSD_STATE, SSD_HEADDIM
    nc = S // Q
    GW = SSD_HPG * P

    def cidx(i):
        return (nc - 1 - i) if reverse else i

    def body(*refs):
        if y_prev is None:
            x_ref, b_ref, c_ref, dt_ref, al_ref, y_ref, st_ref, state = refs
            yp_ref = None
        else:
            x_ref, b_ref, c_ref, dt_ref, al_ref, yp_ref, y_ref, st_ref, state = refs
        g, i = pl.program_id(0), pl.program_id(1)

        @pl.when(i == 0)
        def _():
            state[...] = jnp.zeros_like(state)

        states = [state[h] for h in range(SSD_HPG)]
        for h in range(SSD_HPG):
            st_ref[0, h] = states[h]
        xv = x_ref[...]
        xs = [xv[:, h * P:(h + 1) * P] for h in range(SSD_HPG)]
        ns, ys = _ssd_chunk(states, xs, b_ref[...], c_ref[...], dt_ref[...], al_ref[...], _ssd_sel(g, reverse),
                            reverse=reverse)
        for h in range(SSD_HPG):
            state[h] = ns[h]
            yh = ys[h]
            if yp_ref is not None:
                yh = yh + yp_ref[:, h * P:(h + 1) * P]
            y_ref[:, h * P:(h + 1) * P] = yh

    in_specs = [pl.BlockSpec((Q, GW), lambda g, i: (cidx(i), g)),
                pl.BlockSpec((Q, N), lambda g, i: (cidx(i), SSD_DI // N + g)),
                pl.BlockSpec((Q, N), lambda g, i: (cidx(i), SSD_DI // N + SSD_GROUPS + g)),
                pl.BlockSpec((Q, 128), lambda g, i: (cidx(i), 0)),
                pl.BlockSpec((1, 128), lambda g, i: (0, 0))]
    args = [act, act, act, dt, alog]
    if y_prev is not None:
        in_specs.append(pl.BlockSpec((Q, GW), lambda g, i: (cidx(i), g)))
        args.append(y_prev)
    return pl.pallas_call(
        body, name=f"ssd_scan_fwd_{int(reverse)}", grid=(SSD_GROUPS, nc), in_specs=in_specs,
        out_specs=[pl.BlockSpec((Q, GW), lambda g, i: (cidx(i), g)),
                   pl.BlockSpec((1, SSD_HPG, N, P), lambda g, i: (cidx(i), g, 0, 0))],
        out_shape=[jax.ShapeDtypeStruct((S, SSD_DI), F32), jax.ShapeDtypeStruct((nc, SSD_HEADS, N, P), F32)],
        scratch_shapes=[pltpu.VMEM((SSD_HPG, N, P), F32)],
        compiler_params=_cp("arbitrary", "arbitrary"),
    )(*args)


def ssd_scan_bwd(act, dt, alog, states, dy, prev_x, *, reverse, prev=None):
    S = act.shape[0]
    Q, N, P = SSD_CHUNK, SSD_STATE, SSD_HEADDIM
    nc = S // Q
    GW = SSD_HPG * P

    def cidx(i):
        return i if reverse else (nc - 1 - i)

    def body(*refs):
        x_ref, b_ref, c_ref, dt_ref, al_ref, st_ref, dy_ref, px_ref = refs[:8]
        pos = 8
        if prev is not None:
            pb_ref, pc_ref, pdt_ref, pal_ref = refs[pos:pos + 4]
            pos += 4
        dx_ref, db_ref, dc_ref, ddt_ref, dal_ref, dstate = refs[pos:]
        g, i = pl.program_id(0), pl.program_id(1)

        @pl.when(i == 0)
        def _():
            dstate[...] = jnp.zeros_like(dstate)

        sel = _ssd_sel(g, reverse)
        xv = x_ref[...]
        dyv = dy_ref[...]
        xs = [xv[:, h * P:(h + 1) * P] for h in range(SSD_HPG)]
        dys = [dyv[:, h * P:(h + 1) * P] for h in range(SSD_HPG)]
        states = [st_ref[0, h] for h in range(SSD_HPG)]
        dstates = [dstate[h] for h in range(SSD_HPG)]

        def f(states, xs, Bg, Cg, dtv, al):
            return _ssd_chunk(states, xs, Bg, Cg, dtv, al, sel, reverse=reverse)

        _, vjp = jax.vjp(f, states, xs, b_ref[...], c_ref[...], dt_ref[...], al_ref[...])
        dst, dxs, dB, dC, ddt, dal = vjp((dstates, dys))
        for h in range(SSD_HPG):
            dstate[h] = dst[h]
            dx_ref[:, h * P:(h + 1) * P] = dxs[h] + px_ref[:, h * P:(h + 1) * P]
        if prev is not None:
            dB = dB + pb_ref[...]
            dC = dC + pc_ref[...]
            ddt = ddt + pdt_ref[0]
        db_ref[...] = dB
        dc_ref[...] = dC
        ddt_ref[0] = ddt

        @pl.when(i == 0)
        def _():
            dal_ref[0] = dal + (pal_ref[0] if prev is not None else 0.0)

        @pl.when(i != 0)
        def _():
            dal_ref[0] += dal

    xspec = pl.BlockSpec((Q, GW), lambda g, i: (cidx(i), g))
    nspec_b = pl.BlockSpec((Q, N), lambda g, i: (cidx(i), SSD_DI // N + g))
    nspec_c = pl.BlockSpec((Q, N), lambda g, i: (cidx(i), SSD_DI // N + SSD_GROUPS + g))
    dtspec = pl.BlockSpec((1, Q, 128), lambda g, i: (g, cidx(i), 0))
    alspec = pl.BlockSpec((1, 1, 128), lambda g, i: (g, 0, 0))
    in_specs = [xspec, nspec_b, nspec_c,
                pl.BlockSpec((Q, 128), lambda g, i: (cidx(i), 0)),
                pl.BlockSpec((1, 128), lambda g, i: (0, 0)),
                pl.BlockSpec((1, SSD_HPG, N, P), lambda g, i: (cidx(i), g, 0, 0)),
                xspec]
    gspec = pl.BlockSpec((Q, N), lambda g, i: (cidx(i), g))
    in_specs.append(xspec)
    args = [act, act, act, dt, alog, states, dy, prev_x]
    if prev is not None:
        in_specs += [gspec, gspec, dtspec, alspec]
        args += list(prev)
    outs = pl.pallas_call(
        body, name=f"ssd_scan_bwd_{int(reverse)}", grid=(SSD_GROUPS, nc), in_specs=in_specs,
        out_specs=[pl.BlockSpec((Q, GW), lambda g, i: (cidx(i), g)),
                   pl.BlockSpec((Q, N), lambda g, i: (cidx(i), g)),
                   pl.BlockSpec((Q, N), lambda g, i: (cidx(i), g)),
                   dtspec, alspec],
        out_shape=[jax.ShapeDtypeStruct((S, SSD_DI), F32), jax.ShapeDtypeStruct((S, SSD_GROUPS * N), F32),
                   jax.ShapeDtypeStruct((S, SSD_GROUPS * N), F32),
                   jax.ShapeDtypeStruct((SSD_GROUPS, S, 128), F32), jax.ShapeDtypeStruct((SSD_GROUPS, 1, 128), F32)],
        scratch_shapes=[pltpu.VMEM((SSD_HPG, N, P), F32)],
        compiler_params=_cp("arbitrary", "arbitrary"),
    )(*args)
    return outs


def _ssd_post_fn(y, xs, z, dexp, ng):
    t = (y + xs * dexp) * _silu(z)
    return (_rms(t, ng),)


def _hg_chunk(state, qraw, fraw, v, lb, *, reverse):
    C = HG_CHUNK
    r = lax.broadcasted_iota(jnp.int32, (C, C), 0)
    c = lax.broadcasted_iota(jnp.int32, (C, C), 1)
    keep = (c >= r) if reverse else (c <= r)
    tri = jnp.where(keep, 1.0, 0.0).astype(F32)
    q = _silu(qraw)
    f = lb + (1.0 - lb) * jax.nn.sigmoid(fraw)
    k = 1.0 - f
    g = jnp.log(f)
    G = _dot_exact(tri, g, ((1,), (0,)))
    ref_row = C // 2 - 1 if reverse else C // 2
    last_row = 0 if reverse else C - 1
    Gr = G[ref_row:ref_row + 1, :]
    Gl = G[last_row:last_row + 1, :]
    q_t = q * jnp.exp(G - Gr)
    k_t = k * jnp.exp(Gr - G)
    att = jnp.where(keep, _dot(q_t, k_t, ((1,), (1,))), 0.0)
    o = _dot(att, v, ((1,), (0,))) + _dot(q * jnp.exp(G), state, ((1,), (0,)))
    kd = k * jnp.exp(Gl - G)
    new_state = jnp.transpose(jnp.exp(Gl)) * state + _dot(kd, v, ((0,), (0,)))
    return new_state, o


def hg_scan_fwd(u, lb, *, reverse, o_prev=None, rows=256):
    S = u.shape[0]
    nh = HG_HEADS
    rows = min(rows, S)
    nsteps = S // rows
    ncb = rows // HG_CHUNK
    f_sec = 2 if reverse else 1

    def blk(i):
        return (nsteps - 1 - i) if reverse else i

    def body(*refs):
        if o_prev is None:
            q_ref, f_ref, v_ref, lb_ref, o_ref, st_ref, state = refs
            op_ref = None
        else:
            q_ref, f_ref, v_ref, lb_ref, op_ref, o_ref, st_ref, state = refs
        i = pl.program_id(1)

        @pl.when(i == 0)
        def _():
            state[...] = jnp.zeros_like(state)

        def chunk(cc, carry):
            ci = (ncb - 1 - cc) if reverse else cc
            r0 = pl.multiple_of(ci * HG_CHUNK, HG_CHUNK)
            st = state[...]
            st_ref[ci, 0] = st
            ns, o = _hg_chunk(st, q_ref[pl.ds(r0, HG_CHUNK), :], f_ref[pl.ds(r0, HG_CHUNK), :],
                              v_ref[pl.ds(r0, HG_CHUNK), :], lb_ref[...], reverse=reverse)
            state[...] = ns
            if op_ref is not None:
                o = o + op_ref[pl.ds(r0, HG_CHUNK), :]
            o_ref[pl.ds(r0, HG_CHUNK), :] = o
            return carry

        lax.fori_loop(0, ncb, chunk, 0)

    rowspec = lambda sec: pl.BlockSpec((rows, HG_D), lambda h, i: (blk(i), sec * nh + h))
    in_specs = [rowspec(0), rowspec(f_sec), rowspec(3), pl.BlockSpec((1, HG_D), lambda h, i: (0, h))]
    args = [u, u, u, lb]
    if o_prev is not None:
        in_specs.append(rowspec(0))
        args.append(o_prev)
    return pl.pallas_call(
        body, name=f"hg_scan_fwd_{int(reverse)}", grid=(nh, nsteps), in_specs=in_specs,
        out_specs=[rowspec(0), pl.BlockSpec((ncb, 1, HG_D, HG_D), lambda h, i: (blk(i), h, 0, 0))],
        out_shape=[jax.ShapeDtypeStruct((S, HG_W), F32), jax.ShapeDtypeStruct((S // HG_CHUNK, nh, HG_D, HG_D), F32)],
        scratch_shapes=[pltpu.VMEM((HG_D, HG_D), F32)],
        compiler_params=_cp("arbitrary", "arbitrary"),
    )(*args)


def hg_scan_bwd(u, lb, states, do, *, reverse, prev=None, rows=256):
    S = u.shape[0]
    nh = HG_HEADS
    rows = min(rows, S)
    nsteps = S // rows
    ncb = rows // HG_CHUNK
    f_sec = 2 if reverse else 1

    def blk(i):
        return i if reverse else (nsteps - 1 - i)

    def body(*refs):
        q_ref, f_ref, v_ref, lb_ref, st_ref, do_ref = refs[:6]
        pos = 6
        if prev is not None:
            pq_ref, pv_ref, plb_ref = refs[pos:pos + 3]
            pos += 3
        dq_ref, df_ref, dv_ref, dlb_ref, dstate = refs[pos:]
        i = pl.program_id(1)

        @pl.when(i == 0)
        def _():
            dstate[...] = jnp.zeros_like(dstate)
            dlb_ref[...] = plb_ref[...] if prev is not None else jnp.zeros_like(dlb_ref)

        def chunk(cc, carry):
            ci = cc if reverse else (ncb - 1 - cc)
            r0 = pl.multiple_of(ci * HG_CHUNK, HG_CHUNK)
            sl = pl.ds(r0, HG_CHUNK)
            _, vjp = jax.vjp(functools.partial(_hg_chunk, reverse=reverse), st_ref[ci, 0],
                             q_ref[sl, :], f_ref[sl, :], v_ref[sl, :], lb_ref[...])
            dst, dq, df, dv, dlb = vjp((dstate[...], do_ref[sl, :]))
            dstate[...] = dst
            if prev is not None:
                dq = dq + pq_ref[sl, :]
                dv = dv + pv_ref[sl, :]
            dq_ref[sl, :] = dq
            df_ref[sl, :] = df
            dv_ref[sl, :] = dv
            dlb_ref[...] += dlb
            return carry

        lax.fori_loop(0, ncb, chunk, 0)

    rowspec = lambda sec: pl.BlockSpec((rows, HG_D), lambda h, i: (blk(i), sec * nh + h))
    lbspec = pl.BlockSpec((1, HG_D), lambda h, i: (0, h))
    in_specs = [rowspec(0), rowspec(f_sec), rowspec(3), lbspec,
                pl.BlockSpec((ncb, 1, HG_D, HG_D), lambda h, i: (blk(i), h, 0, 0)), rowspec(0)]
    args = [u, u, u, lb, states, do]
    if prev is not None:
        in_specs += [rowspec(0), rowspec(0), lbspec]
        args += list(prev)
    return pl.pallas_call(
        body, name=f"hg_scan_bwd_{int(reverse)}", grid=(nh, nsteps), in_specs=in_specs,
        out_specs=[rowspec(0), rowspec(0), rowspec(0), lbspec],
        out_shape=[jax.ShapeDtypeStruct((S, HG_W), F32)] * 3 + [jax.ShapeDtypeStruct((1, HG_W), F32)],
        scratch_shapes=[pltpu.VMEM((HG_D, HG_D), F32)],
        compiler_params=_cp("arbitrary", "arbitrary"),
    )(*args)


def _hg_lb_fn(lbp):
    m = jnp.max(lbp, axis=0, keepdims=True)
    e = jnp.exp(lbp - m)
    sm = e / jnp.sum(e, axis=0, keepdims=True)
    return ((sm[0:1] + sm[1:2]) - sm[0:1],)


def hg_lb_fwd(lbp):
    def body(x_ref, o_ref):
        o_ref[...] = _hg_lb_fn(x_ref[...])[0]

    return pl.pallas_call(body, name="hg_lb_fwd", out_shape=jax.ShapeDtypeStruct((1, HG_W), F32))(lbp)


def hg_lb_bwd(lbp, dlb):
    def body(x_ref, d_ref, o_ref):
        _, vjp = jax.vjp(_hg_lb_fn, x_ref[...])
        o_ref[...] = vjp((d_ref[...],))[0]

    return pl.pallas_call(body, name="hg_lb_bwd", out_shape=jax.ShapeDtypeStruct(lbp.shape, F32))(lbp, dlb)


def _hg_post_fn(o, gate, ng):
    return (_rms(o, ng) * _silu(gate),)


def _gate_fn(o, gate):
    return (o * _silu(gate),)


def _rope_tables(S):
    t = np.arange(S)
    row = (t // GRID_W).astype(np.float32)
    col = (t % GRID_W).astype(np.float32)
    half = AT_HD // 4
    inv = (ROPE_THETA ** (-np.arange(0, 2 * half, 2, dtype=np.float32) / np.float32(2 * half))).astype(np.float32)
    ar = row[:, None] * inv[None, :]
    ac = col[:, None] * inv[None, :]
    return ar.astype(np.float32), ac.astype(np.float32)


def _rope_swap_matrix():
    p = np.zeros((AT_HD, AT_HD), np.float32)
    for i in range(AT_HD):
        p[(i + 32) if (i % 64) < 32 else (i - 32), i] = 1.0
    return p


def _make_qk_fn(scale):
    def fn(x, ct, st, g, perm):
        n = _rms(x, g)
        return ((n * ct + _dot_exact(n, perm, ((1,), (0,))) * st) * scale,)
    return fn


def flash_fwd(q, k, v, *, v_col0=0, tq=512, tk=512):
    S = q.shape[0]
    tq, tk = min(tq, S), min(tk, S)
    nk = S // tk
    G = AT_HEADS // AT_KV

    def body(q_ref, k_ref, v_ref, o_ref, lse_ref, m_s, l_s, acc):
        ki = pl.program_id(2)

        @pl.when(ki == 0)
        def _():
            m_s[...] = jnp.full_like(m_s, -jnp.inf)
            l_s[...] = jnp.zeros_like(l_s)
            acc[...] = jnp.zeros_like(acc)

        kv, vv = k_ref[...], v_ref[...]
        for g in range(G):
            s = _dot(q_ref[:, g * AT_HD:(g + 1) * AT_HD], kv, ((1,), (1,)))
            m_old = m_s[g]
            m_new = jnp.maximum(m_old, jnp.max(s, axis=1, keepdims=True))
            alpha = jnp.exp(m_old - m_new)
            p = jnp.exp(s - m_new)
            l_s[g] = alpha * l_s[g] + jnp.sum(p, axis=1, keepdims=True)
            acc[g] = alpha * acc[g] + _dot(p, vv, ((1,), (0,)))
            m_s[g] = m_new

        @pl.when(ki == nk - 1)
        def _():
            for g in range(G):
                o_ref[:, g * AT_HD:(g + 1) * AT_HD] = acc[g] / l_s[g]
                lse_ref[0, :, g:g + 1] = m_s[g] + jnp.log(l_s[g])

    return pl.pallas_call(
        body, name="flash_fwd", grid=(AT_KV, S // tq, nk),
        in_specs=[pl.BlockSpec((tq, G * AT_HD), lambda h, i, j: (i, h)),
                  pl.BlockSpec((tk, AT_HD), lambda h, i, j: (j, h)),
                  pl.BlockSpec((tk, AT_HD), lambda h, i, j: (j, v_col0 + h))],
        out_specs=[pl.BlockSpec((tq, G * AT_HD), lambda h, i, j: (i, h)),
                   pl.BlockSpec((1, tq, G), lambda h, i, j: (h, i, 0))],
        out_shape=[jax.ShapeDtypeStruct((S, AT_QW), F32), jax.ShapeDtypeStruct((AT_KV, S, G), F32)],
        scratch_shapes=[pltpu.VMEM((G, tq, 1), F32), pltpu.VMEM((G, tq, 1), F32), pltpu.VMEM((G, tq, AT_HD), F32)],
        compiler_params=_cp("parallel", "parallel", "arbitrary"),
    )(q, k, v)


def flash_bwd_dq(q, k, v, o, lse, do, *, v_col0=0, tq=512, tk=512):
    S = q.shape[0]
    tq, tk = min(tq, S), min(tk, S)
    nk = S // tk
    G = AT_HEADS // AT_KV

    def body(q_ref, k_ref, v_ref, o_ref, lse_ref, do_ref, dq_ref, dl_ref, acc, dl_s):
        ki = pl.program_id(2)

        @pl.when(ki == 0)
        def _():
            acc[...] = jnp.zeros_like(acc)
            for g in range(G):
                sl = slice(g * AT_HD, (g + 1) * AT_HD)
                dl_s[g] = jnp.sum(do_ref[:, sl] * o_ref[:, sl], axis=1, keepdims=True)

        kv, vv = k_ref[...], v_ref[...]
        for g in range(G):
            sl = slice(g * AT_HD, (g + 1) * AT_HD)
            s = _dot(q_ref[:, sl], kv, ((1,), (1,)))
            p = jnp.exp(s - lse_ref[0, :, g:g + 1])
            dp = _dot(do_ref[:, sl], vv, ((1,), (1,)))
            ds = p * (dp - dl_s[g])
            acc[g] += _dot(ds, kv, ((1,), (0,)))

        @pl.when(ki == nk - 1)
        def _():
            for g in range(G):
                dq_ref[:, g * AT_HD:(g + 1) * AT_HD] = acc[g]
                dl_ref[0, :, g:g + 1] = dl_s[g]

    qspec = pl.BlockSpec((tq, G * AT_HD), lambda h, i, j: (i, h))
    kspec = pl.BlockSpec((tk, AT_HD), lambda h, i, j: (j, h))
    lspec = pl.BlockSpec((1, tq, G), lambda h, i, j: (h, i, 0))
    return pl.pallas_call(
        body, name="flash_bwd_dq", grid=(AT_KV, S // tq, nk),
        in_specs=[qspec, kspec, pl.BlockSpec((tk, AT_HD), lambda h, i, j: (j, v_col0 + h)), qspec, lspec, qspec],
        out_specs=[qspec, lspec],
        out_shape=[jax.ShapeDtypeStruct((S, AT_QW), F32), jax.ShapeDtypeStruct((AT_KV, S, G), F32)],
        scratch_shapes=[pltpu.VMEM((G, tq, AT_HD), F32), pltpu.VMEM((G, tq, 1), F32)],
        compiler_params=_cp("parallel", "parallel", "arbitrary"),
    )(q, k, v, o, lse, do)


def flash_bwd_dkv(q, k, v, lse, delta, do, *, v_col0=0, tq=512, tk=512):
    S = q.shape[0]
    tq, tk = min(tq, S), min(tk, S)
    nq = S // tq
    G = AT_HEADS // AT_KV

    def body(q_ref, k_ref, v_ref, lse_ref, dl_ref, do_ref, dk_ref, dv_ref, dk_acc, dv_acc):
        qi = pl.program_id(2)

        @pl.when(qi == 0)
        def _():
            dk_acc[...] = jnp.zeros_like(dk_acc)
            dv_acc[...] = jnp.zeros_like(dv_acc)

        kv, vv = k_ref[...], v_ref[...]
        for g in range(G):
            sl = slice(g * AT_HD, (g + 1) * AT_HD)
            qg, dog = q_ref[:, sl], do_ref[:, sl]
            s = _dot(qg, kv, ((1,), (1,)))
            p = jnp.exp(s - lse_ref[0, :, g:g + 1])
            dv_acc[...] += _dot(p, dog, ((0,), (0,)))
            dp = _dot(dog, vv, ((1,), (1,)))
            ds = p * (dp - dl_ref[0, :, g:g + 1])
            dk_acc[...] += _dot(ds, qg, ((0,), (0,)))

        @pl.when(qi == nq - 1)
        def _():
            dk_ref[...] = dk_acc[...]
            dv_ref[...] = dv_acc[...]

    qspec = pl.BlockSpec((tq, G * AT_HD), lambda h, j, i: (i, h))
    kspec = pl.BlockSpec((tk, AT_HD), lambda h, j, i: (j, h))
    lspec = pl.BlockSpec((1, tq, G), lambda h, j, i: (h, i, 0))
    return pl.pallas_call(
        body, name="flash_bwd_dkv", grid=(AT_KV, S // tk, nq),
        in_specs=[qspec, kspec, pl.BlockSpec((tk, AT_HD), lambda h, j, i: (j, v_col0 + h)), lspec, lspec, qspec],
        out_specs=[kspec, kspec],
        out_shape=[jax.ShapeDtypeStruct((S, AT_KW), F32), jax.ShapeDtypeStruct((S, AT_KW), F32)],
        scratch_shapes=[pltpu.VMEM((tk, AT_HD), F32), pltpu.VMEM((tk, AT_HD), F32)],
        compiler_params=_cp("parallel", "parallel", "arbitrary"),
    )(q, k, v, lse, delta, do)


def _t5_bucket_np(rel):
    half = REL_BUCKETS // 2
    exact = half // 2
    n = np.abs(rel)
    large = exact + (np.log(np.maximum(n, 1).astype(np.float32) / np.float32(exact))
                     / np.float32(math.log(REL_MAX_DIST / exact)) * np.float32(half - exact)).astype(np.int32)
    large = np.minimum(large, half - 1)
    return np.where(rel > 0, half, 0) + np.where(n < exact, n, large)


def _dl_tiles(Ls):
    T = min(128, Ls)
    return T, T + 2 * DL_HALF


def _dl_bucket_tables(dil, T):
    W = T + 2 * DL_HALF
    i = np.arange(T)[:, None]
    j = np.arange(W)[None, :]
    bq = _t5_bucket_np((j - DL_HALF - i) * dil)
    iw = np.arange(W)[:, None]
    jk = np.arange(T)[None, :]
    bk = _t5_bucket_np((jk + DL_HALF - iw) * dil)
    return bq.astype(np.int32), bk.astype(np.int32)


def band_fwd(q, kp, vp, bias, *, scale):
    H, dil, Ls, E = q.shape
    T, W = _dl_tiles(Ls)

    def body(q_ref, k_ref, v_ref, b_ref, o_ref, lse_ref):
        n = pl.program_id(2)
        r0 = pl.multiple_of(n * T, T)
        kw = k_ref[0, 0, pl.ds(r0, W), :]
        vw = v_ref[0, 0, pl.ds(r0, W), :]
        i = lax.broadcasted_iota(jnp.int32, (T, W), 0)
        j = lax.broadcasted_iota(jnp.int32, (T, W), 1)
        kpos = n * T + j - DL_HALF
        mask = (jnp.abs(j - DL_HALF - i) <= DL_HALF) & (kpos >= 0) & (kpos < Ls)
        s = _dot(q_ref[0, 0], kw, ((1,), (1,))) * scale + b_ref[0]
        s = jnp.where(mask, s, NEG_BIG)
        m = jnp.max(s, axis=1, keepdims=True)
        lse = m + jnp.log(jnp.sum(jnp.exp(s - m), axis=1, keepdims=True))
        p = jnp.exp(s - lse)
        o_ref[0, 0] = _dot(p, vw, ((1,), (0,)))
        lse_ref[0, 0] = lse

    return pl.pallas_call(
        body, name=f"band_fwd_{dil}", grid=(H, dil, Ls // T),
        in_specs=[pl.BlockSpec((1, 1, T, E), lambda h, d, n: (h, d, n, 0)),
                  pl.BlockSpec((1, 1, Ls + 2 * DL_HALF, E), lambda h, d, n: (h, d, 0, 0)),
                  pl.BlockSpec((1, 1, Ls + 2 * DL_HALF, E), lambda h, d, n: (h, d, 0, 0)),
                  pl.BlockSpec((1, T, W), lambda h, d, n: (h, 0, 0))],
        out_specs=[pl.BlockSpec((1, 1, T, E), lambda h, d, n: (h, d, n, 0)),
                   pl.BlockSpec((1, 1, T, 1), lambda h, d, n: (h, d, n, 0))],
        out_shape=[jax.ShapeDtypeStruct((H, dil, Ls, E), F32), jax.ShapeDtypeStruct((H, dil, Ls, 1), F32)],
        compiler_params=_cp("parallel", "parallel", "arbitrary"),
    )(q, kp, vp, bias)


def band_bwd_dq(q, kp, vp, bias, lse, dm, do, *, scale):
    H, dil, Ls, E = q.shape
    T, W = _dl_tiles(Ls)

    def body(q_ref, k_ref, v_ref, b_ref, lse_ref, dm_ref, do_ref, dq_ref, db_ref):
        d, n = pl.program_id(1), pl.program_id(2)
        r0 = pl.multiple_of(n * T, T)
        kw = k_ref[0, 0, pl.ds(r0, W), :]
        vw = v_ref[0, 0, pl.ds(r0, W), :]
        i = lax.broadcasted_iota(jnp.int32, (T, W), 0)
        j = lax.broadcasted_iota(jnp.int32, (T, W), 1)
        kpos = n * T + j - DL_HALF
        mask = (jnp.abs(j - DL_HALF - i) <= DL_HALF) & (kpos >= 0) & (kpos < Ls)
        s = _dot(q_ref[0, 0], kw, ((1,), (1,))) * scale + b_ref[0]
        p = jnp.where(mask, jnp.exp(jnp.where(mask, s, 0.0) - lse_ref[0, 0]), 0.0)
        dp = _dot(do_ref[0, 0], vw, ((1,), (1,)))
        ds = p * (dp - dm_ref[0, 0])
        dq_ref[0, 0] = _dot(ds, kw, ((1,), (0,))) * scale
        first = jnp.logical_and(d == 0, n == 0)

        @pl.when(first)
        def _():
            db_ref[0] = ds

        @pl.when(jnp.logical_not(first))
        def _():
            db_ref[0] += ds

    qspec = pl.BlockSpec((1, 1, T, E), lambda h, d, n: (h, d, n, 0))
    kspec = pl.BlockSpec((1, 1, Ls + 2 * DL_HALF, E), lambda h, d, n: (h, d, 0, 0))
    rspec = pl.BlockSpec((1, 1, T, 1), lambda h, d, n: (h, d, n, 0))
    bspec = pl.BlockSpec((1, T, W), lambda h, d, n: (h, 0, 0))
    return pl.pallas_call(
        body, name=f"band_bwd_dq_{dil}", grid=(H, dil, Ls // T),
        in_specs=[qspec, kspec, kspec, bspec, rspec, rspec, qspec],
        out_specs=[qspec, bspec],
        out_shape=[jax.ShapeDtypeStruct((H, dil, Ls, E), F32), jax.ShapeDtypeStruct((H, T, W), F32)],
        compiler_params=_cp("arbitrary", "arbitrary", "arbitrary"),
    )(q, kp, vp, bias, lse, dm, do)


def band_bwd_dkv(qp, k, v, bias_t, lsep, dmp, dop, *, scale):
    H, dil, Ls, E = k.shape
    T, W = _dl_tiles(Ls)

    def body(q_ref, k_ref, v_ref, b_ref, lse_ref, dm_ref, do_ref, dk_ref, dv_ref):
        n = pl.program_id(2)
        r0 = pl.multiple_of(n * T, T)
        qw = q_ref[0, 0, pl.ds(r0, W), :]
        dow = do_ref[0, 0, pl.ds(r0, W), :]
        lsew = lse_ref[0, 0, pl.ds(r0, W), :]
        dmw = dm_ref[0, 0, pl.ds(r0, W), :]
        iw = lax.broadcasted_iota(jnp.int32, (W, T), 0)
        j = lax.broadcasted_iota(jnp.int32, (W, T), 1)
        qpos = n * T + iw - DL_HALF
        mask = (jnp.abs(j + DL_HALF - iw) <= DL_HALF) & (qpos >= 0) & (qpos < Ls)
        s = _dot(qw, k_ref[0, 0], ((1,), (1,))) * scale + b_ref[0]
        p = jnp.where(mask, jnp.exp(jnp.where(mask, s, 0.0) - lsew), 0.0)
        dv_ref[0, 0] = _dot(p, dow, ((0,), (0,)))
        dp = _dot(dow, v_ref[0, 0], ((1,), (1,)))
        ds = p * (dp - dmw)
        dk_ref[0, 0] = _dot(ds, qw, ((0,), (0,))) * scale

    kspec = pl.BlockSpec((1, 1, T, E), lambda h, d, n: (h, d, n, 0))
    wspec = pl.BlockSpec((1, 1, Ls + 2 * DL_HALF, E), lambda h, d, n: (h, d, 0, 0))
    w1spec = pl.BlockSpec((1, 1, Ls + 2 * DL_HALF, 1), lambda h, d, n: (h, d, 0, 0))
    return pl.pallas_call(
        body, name=f"band_bwd_dkv_{dil}", grid=(H, dil, Ls // T),
        in_specs=[wspec, kspec, kspec, pl.BlockSpec((1, W, T), lambda h, d, n: (h, 0, 0)), w1spec, w1spec, wspec],
        out_specs=[kspec, kspec],
        out_shape=[jax.ShapeDtypeStruct((H, dil, Ls, E), F32), jax.ShapeDtypeStruct((H, dil, Ls, E), F32)],
        compiler_params=_cp("parallel", "parallel", "arbitrary"),
    )(qp, k, v, bias_t, lsep, dmp, dop)


def _dl_merge_fn(o0, o1, o2, l0, l1, l2):
    m = jnp.maximum(jnp.maximum(l0, l1), l2)
    e0, e1, e2 = jnp.exp(l0 - m), jnp.exp(l1 - m), jnp.exp(l2 - m)
    den = e0 + e1 + e2
    return ((e0 / den) * o0 + (e1 / den) * o1 + (e2 / den) * o2,)


def _adamw_math(w, g, m, v):
    m = ADAM_B1 * m + (1.0 - ADAM_B1) * g
    v = ADAM_B2 * v + (1.0 - ADAM_B2) * (g * g)
    m_hat = m / (1.0 - ADAM_B1 ** ADAM_STEP)
    v_hat = v / (1.0 - ADAM_B2 ** ADAM_STEP)
    delta = -ADAM_LR * (m_hat / (jnp.sqrt(v_hat) + ADAM_EPS) + ADAM_WD * w)
    return delta, m, v


def adamw_sum(parts, w, m, v, *, name, R=128):
    rows, cols = w.shape
    R = min(R, rows)
    if rows % R:
        R = rows

    def body(p_ref, w_ref, m_ref, v_ref, g_ref, d_ref, nm_ref, nv_ref):
        g = p_ref[0]
        for s in range(1, N_DEV):
            g = g + p_ref[s]
        d, nm, nv = _adamw_math(w_ref[...], g, m_ref[...], v_ref[...])
        g_ref[...] = g
        d_ref[...] = d
        nm_ref[...] = nm
        nv_ref[...] = nv

    spec = pl.BlockSpec((R, cols), lambda i: (i, 0))
    return pl.pallas_call(
        body, name=name, grid=(rows // R,),
        in_specs=[pl.BlockSpec((N_DEV, R, cols), lambda i: (0, i, 0)), spec, spec, spec],
        out_specs=[spec] * 4, out_shape=[jax.ShapeDtypeStruct((rows, cols), F32)] * 4,
        compiler_params=_cp("parallel"),
    )(parts, w, m, v)


def sum_parts(parts, *, name):
    rows, cols = parts.shape[1:]

    def body(p_ref, o_ref):
        g = p_ref[0]
        for s in range(1, N_DEV):
            g = g + p_ref[s]
        o_ref[...] = g

    return pl.pallas_call(body, name=name, out_shape=jax.ShapeDtypeStruct((rows, cols), F32))(parts)


def adamw_plain(w, g, m, v, *, name):
    def body(w_ref, g_ref, m_ref, v_ref, d_ref, nm_ref, nv_ref):
        d, nm, nv = _adamw_math(w_ref[...], g_ref[...], m_ref[...], v_ref[...])
        d_ref[...] = d
        nm_ref[...] = nm
        nv_ref[...] = nv

    return pl.pallas_call(body, name=name, out_shape=[jax.ShapeDtypeStruct(w.shape, F32)] * 3)(w, g, m, v)


def _my_pos():
    return lax.axis_index("x"), lax.axis_index("y"), lax.axis_index("c")


def _flat(px, py, pc):
    return 4 * px + 2 * py + pc


def allgather_two_level(x, *, name):
    R, C = x.shape

    def body(x_ref, out_ref, send_sems, recv_sems, local_sem):
        x_, y_, c_ = _my_pos()
        me, sibling = (x_, y_, c_), (x_, y_, 1 - c_)
        chips = [(1 - x_, y_), (x_, 1 - y_), (1 - x_, 1 - y_)]

        def rows(p):
            return out_ref.at[_flat(*p)]

        def copy(k, block, to, src=None):
            return pltpu.make_async_remote_copy(
                src_ref=rows(block) if src is None else src, dst_ref=rows(block),
                send_sem=send_sems.at[k], recv_sem=recv_sems.at[k], device_id=to, device_id_type=MESH_ID)

        mine = pltpu.make_async_copy(x_ref, rows(me), local_sem)
        mine.start()
        first = [copy(0, me, sibling, src=x_ref)]
        first += [copy(1 + j, me, (*chip, c_), src=x_ref) for j, chip in enumerate(chips)]
        for cp in first:
            cp.start()
        passed = [copy(4 + j, (*chip, c_), sibling) for j, chip in enumerate(chips)]
        for j, chip in enumerate(chips):
            copy(1 + j, (*chip, c_), me).wait_recv()
            passed[j].start()
        copy(0, sibling, me).wait_recv()
        for j, chip in enumerate(chips):
            copy(4 + j, (*chip, 1 - c_), me).wait_recv()
        for cp in first + passed:
            cp.wait_send()
        mine.wait()

    return pl.pallas_call(
        body, name=name,
        out_shape=jax.ShapeDtypeStruct((N_DEV, R, C), x.dtype),
        in_specs=[pl.BlockSpec(memory_space=pl.ANY)],
        out_specs=pl.BlockSpec(memory_space=pl.ANY),
        scratch_shapes=[pltpu.SemaphoreType.DMA((7,)), pltpu.SemaphoreType.DMA((7,)), pltpu.SemaphoreType.DMA],
    )(x)


def all_to_all(bufs, *, name):
    nb = len(bufs)

    def body(*refs):
        in_refs = refs[:nb]
        out_refs = refs[nb:2 * nb]
        send_sems, recv_sems, local_sems = refs[2 * nb:]
        x_, y_, c_ = _my_pos()
        me = _flat(x_, y_, c_)
        peers = []
        for k in range(1, N_DEV):
            fx, fy, fc = (k >> 2) & 1, (k >> 1) & 1, k & 1
            peers.append(((1 - x_) if fx else x_, (1 - y_) if fy else y_, (1 - c_) if fc else c_))
        copies = []
        for b in range(nb):
            loc = pltpu.make_async_copy(in_refs[b].at[me], out_refs[b].at[me], local_sems.at[b])
            loc.start()
            copies.append(loc)
        remote = []
        for b in range(nb):
            for k, p in enumerate(peers):
                cp = pltpu.make_async_remote_copy(
                    src_ref=in_refs[b].at[_flat(*p)], dst_ref=out_refs[b].at[me],
                    send_sem=send_sems.at[b, k], recv_sem=recv_sems.at[b, k], device_id=p, device_id_type=MESH_ID)
                cp.start()
                remote.append((b, k, p))
        for b, k, p in remote:
            pltpu.make_async_remote_copy(
                src_ref=in_refs[b].at[me], dst_ref=out_refs[b].at[_flat(*p)],
                send_sem=send_sems.at[b, k], recv_sem=recv_sems.at[b, k], device_id=p, device_id_type=MESH_ID).wait()
        for loc in copies:
            loc.wait()

    return pl.pallas_call(
        body, name=name,
        out_shape=[jax.ShapeDtypeStruct(b.shape, b.dtype) for b in bufs],
        in_specs=[pl.BlockSpec(memory_space=pl.ANY)] * nb,
        out_specs=[pl.BlockSpec(memory_space=pl.ANY)] * nb,
        scratch_shapes=[pltpu.SemaphoreType.DMA((nb, 7)), pltpu.SemaphoreType.DMA((nb, 7)), pltpu.SemaphoreType.DMA((nb,))],
    )(*bufs)


def _prenorm(tag, x, ng):
    return rowwise_fwd(f"{tag}_prenorm", _prenorm_fn, [(x, 0, False)], [(ng, False)], [(D_MODEL, MXU_DTYPE)], W=D_MODEL)[0]


def _in_out_bwd(tag, x, ng, hn, du, w_in, dx):
    dhn = matmul(du, w_in, tb=True, name=f"{tag}_dhn")
    dw_in = matmul(hn, du, ta=True, name=f"{tag}_dw_in")
    dx_prev, dng = rowwise_bwd(f"{tag}_prenorm_bwd", _prenorm_fn, [(x, 0, False)], [(ng, False)], [dhn],
                               W=D_MODEL, diff_rows=[0], diff_shared=[0], add=dx)
    return dx_prev, dng, dw_in


def ssd_layer_fwd(x, ng, p):
    hn = _prenorm("ssd", x, ng)
    u = matmul(hn, p["w_in"], name="ssd_in")
    act = ssd_conv_fwd(u, p["conv_w"], p["conv_b"])
    dt = rowwise_fwd("ssd_dt", _dt_fn, [(u, (SSD_DI + SSD_CONV_CH) // 128, False)], [(p["dt_bias"], False)],
                     [(128, F32)], W=128)[0]
    y0, st0 = ssd_scan_fwd(act, dt, p["alog"], reverse=False)
    y, st1 = ssd_scan_fwd(act, dt, p["alog"], reverse=True, y_prev=y0)
    g = rowwise_fwd("ssd_post", _ssd_post_fn, [(y, 0, True), (act, 0, True), (u, 0, True)],
                    [(p["dexp"], True), (p["norm_g"], True)], [(512, MXU_DTYPE)], W=512, ncb=SSD_GROUPS)[0]
    xn = matmul(g, p["w_out"], residual=x, name="ssd_out")
    return xn, dict(x=x, ng=ng, hn=hn, u=u, act=act, dt=dt, y=y, st0=st0, st1=st1, g=g)


def ssd_layer_bwd(sv, p, dx):
    u, act, dt = sv["u"], sv["act"], sv["dt"]
    S = u.shape[0]
    dg = matmul(dx, p["w_out"], tb=True, name="ssd_dg")
    dw_out = matmul(sv["g"], dx, ta=True, name="ssd_dw_out")
    dy, dxs_skip, dz, ddexp, dnorm = rowwise_bwd(
        "ssd_post_bwd", _ssd_post_fn, [(sv["y"], 0, True), (act, 0, True), (u, 0, True)],
        [(p["dexp"], True), (p["norm_g"], True)], [dg], W=512, ncb=SSD_GROUPS, diff_rows=[0, 1, 2], diff_shared=[0, 1])
    dxa, dB, dC, ddt, dal = ssd_scan_bwd(act, dt, p["alog"], sv["st0"], dy, dxs_skip, reverse=False)
    dxa, dB, dC, ddt, dal = ssd_scan_bwd(act, dt, p["alog"], sv["st1"], dy, dxa, reverse=True, prev=(dB, dC, ddt, dal))
    dact = jnp.concatenate([dxa, dB, dC], axis=1)
    dxbc, dconv_w, dconv_b = ssd_conv_bwd(u, p["conv_w"], p["conv_b"], dact)
    ddt_all = ddt[0] + ddt[1] + ddt[2] + ddt[3]
    ddt_raw, ddt_bias = rowwise_bwd("ssd_dt_bwd", _dt_fn, [(u, (SSD_DI + SSD_CONV_CH) // 128, False)],
                                    [(p["dt_bias"], False)], [ddt_all], W=128, diff_rows=[0], diff_shared=[0])
    du = jnp.concatenate([dz, dxbc, ddt_raw, jnp.zeros((S, SSD_IN_PAD - SSD_IN - 64), F32)], axis=1)
    dx_prev, dng, dw_in = _in_out_bwd("ssd", sv["x"], sv["ng"], sv["hn"], du, p["w_in"], dx)
    grads = dict(
        w_in=dw_in[:, :SSD_IN], w_out=dw_out, conv_w=dconv_w[:SSD_CONV], conv_b=dconv_b,
        dt_bias=ddt_bias[:, :2 * SSD_HEADS], a_log=(dal[0] + dal[1] + dal[2] + dal[3])[:, :2 * SSD_HEADS],
        d=ddexp.reshape(SSD_HEADS, SSD_HEADDIM).sum(axis=1)[None, :], norm_g=dnorm, ng=dng)
    return dx_prev, grads


def hg_layer_fwd(x, ng, p):
    hn = _prenorm("hg", x, ng)
    u = matmul(hn, p["w_in"], name="hg_in")
    lb = hg_lb_fwd(p["hgrn_lb"])
    o0, st0 = hg_scan_fwd(u, lb, reverse=False)
    o, st1 = hg_scan_fwd(u, lb, reverse=True, o_prev=o0)
    g = rowwise_fwd("hg_post", _hg_post_fn, [(o, 0, True), (u, 4 * HG_HEADS, True)], [(p["norm_g"], True)],
                    [(HG_D, MXU_DTYPE)], W=HG_D, ncb=HG_HEADS)[0]
    xn = matmul(g, p["w_out"], residual=x, name="hg_out")
    return xn, dict(x=x, ng=ng, hn=hn, u=u, lb=lb, o=o, st0=st0, st1=st1, g=g)


def hg_layer_bwd(sv, p, dx):
    u, lb = sv["u"], sv["lb"]
    dg = matmul(dx, p["w_out"], tb=True, name="hg_dg")
    dw_out = matmul(sv["g"], dx, ta=True, name="hg_dw_out")
    do, dgate, dnorm = rowwise_bwd("hg_post_bwd", _hg_post_fn, [(sv["o"], 0, True), (u, 4 * HG_HEADS, True)],
                                   [(p["norm_g"], True)], [dg], W=HG_D, ncb=HG_HEADS, diff_rows=[0, 1], diff_shared=[0])
    dq0, df0, dv0, dlb0 = hg_scan_bwd(u, lb, sv["st0"], do, reverse=False)
    dq, df1, dv, dlb = hg_scan_bwd(u, lb, sv["st1"], do, reverse=True, prev=(dq0, dv0, dlb0))
    du = jnp.concatenate([dq, df0, df1, dv, dgate], axis=1)
    dhgrn_lb = hg_lb_bwd(p["hgrn_lb"], dlb)
    dx_prev, dng, dw_in = _in_out_bwd("hg", sv["x"], sv["ng"], sv["hn"], du, p["w_in"], dx)
    return dx_prev, dict(w_in=dw_in, w_out=dw_out, norm_g=dnorm, hgrn_lb=dhgrn_lb, ng=dng)


def _rope_consts(S):
    ar, ac = _rope_tables(S)
    ct = np.concatenate([np.cos(ar), np.cos(ar), np.cos(ac), np.cos(ac)], axis=1).astype(np.float32)
    st = np.concatenate([-np.sin(ar), np.sin(ar), -np.sin(ac), np.sin(ac)], axis=1).astype(np.float32)
    return jnp.asarray(ct), jnp.asarray(st), jnp.asarray(_rope_swap_matrix())


def _at_qk(tag, u, col0, nheads, scale, gain, consts, cot=None):
    ct, st, perm = consts
    rows = [(u, col0, True), (ct, 0, False), (st, 0, False)]
    shared = [(gain, False), (perm, False)]
    if cot is None:
        return rowwise_fwd(f"at_{tag}", _make_qk_fn(scale), rows, shared, [(AT_HD, MXU_DTYPE)], W=AT_HD, ncb=nheads)[0]
    return rowwise_bwd(f"at_{tag}_bwd", _make_qk_fn(scale), rows, shared, [cot], W=AT_HD, ncb=nheads,
                       diff_rows=[0], diff_shared=[0])


def at_layer_fwd(x, ng, p):
    S = x.shape[0]
    hn = _prenorm("at", x, ng)
    u = matmul(hn, p["w_in"], name="at_in")
    consts = _rope_consts(S)
    qr = _at_qk("q", u, 0, AT_HEADS, AT_HD ** -0.5, p["q_g"], consts)
    kr = _at_qk("k", u, AT_HEADS, AT_KV, 1.0, p["k_g"], consts)
    vc0 = (AT_QW + AT_KW) // AT_HD
    o, lse = flash_fwd(qr, kr, u, v_col0=vc0)
    g = rowwise_fwd("at_gate", _gate_fn, [(o, 0, True), (u, (AT_QW + 2 * AT_KW) // 1024, True)], [],
                    [(1024, MXU_DTYPE)], W=1024, ncb=AT_QW // 1024)[0]
    xn = matmul(g, p["w_out"], residual=x, name="at_out")
    return xn, dict(x=x, ng=ng, hn=hn, u=u, qr=qr, kr=kr, o=o, lse=lse, g=g)


def at_layer_bwd(sv, p, dx):
    u, qr, kr = sv["u"], sv["qr"], sv["kr"]
    S = u.shape[0]
    consts = _rope_consts(S)
    vc0 = (AT_QW + AT_KW) // AT_HD
    dg = matmul(dx, p["w_out"], tb=True, name="at_dg")
    dw_out = matmul(sv["g"], dx, ta=True, name="at_dw_out")
    do, dgate = rowwise_bwd("at_gate_bwd", _gate_fn, [(sv["o"], 0, True), (u, (AT_QW + 2 * AT_KW) // 1024, True)], [],
                            [dg], W=1024, ncb=AT_QW // 1024, diff_rows=[0, 1], diff_shared=[])
    dqs, delta = flash_bwd_dq(qr, kr, u, sv["o"], sv["lse"], do, v_col0=vc0)
    dkr, dv = flash_bwd_dkv(qr, kr, u, sv["lse"], delta, do, v_col0=vc0)
    dq_raw, dqg = _at_qk("q", u, 0, AT_HEADS, AT_HD ** -0.5, p["q_g"], consts, cot=dqs)
    dk_raw, dkg = _at_qk("k", u, AT_HEADS, AT_KV, 1.0, p["k_g"], consts, cot=dkr)
    du = jnp.concatenate([dq_raw, dk_raw, dv, dgate], axis=1)
    dx_prev, dng, dw_in = _in_out_bwd("at", sv["x"], sv["ng"], sv["hn"], du, p["w_in"], dx)
    return dx_prev, dict(w_in=dw_in, w_out=dw_out, q_g=dqg, k_g=dkg, ng=dng)


def _to_stream(t, dil):
    S = t.shape[0]
    return t.reshape(S // dil, dil, DL_HEADS, DL_HD).transpose(2, 1, 0, 3)


def _from_stream(t):
    H, dil, Ls, E = t.shape
    return t.transpose(2, 1, 0, 3).reshape(Ls * dil, H * E)


def _stream_to_hm(t):
    H, dil, Ls, w = t.shape
    return t.transpose(0, 2, 1, 3).reshape(H * Ls * dil, w)


def _hm_to_stream(t, dil):
    w = t.shape[1]
    S = t.shape[0] // DL_HEADS
    return t.reshape(DL_HEADS, S // dil, dil, w).transpose(0, 2, 1, 3)


def _pad_l(t):
    return jnp.pad(t, ((0, 0), (0, 0), (DL_HALF, DL_HALF), (0, 0)))


def _dl_dm_fn(do, o, dl):
    return (jnp.sum(do * o, axis=-1, keepdims=True) - dl,)


def dl_layer_fwd(x, ng, p):
    S = x.shape[0]
    hn = _prenorm("dl", x, ng)
    u = matmul(hn, p["w_in"], name="dl_in")
    scale = DL_HD ** -0.5
    per_group, o_hm, lse_hm = [], [], []
    for gi, (window, dil) in enumerate(DL_PAIRS):
        base = gi * 3 * DL_W
        Ls = S // dil
        T, _ = _dl_tiles(Ls)
        bq, bk = _dl_bucket_tables(dil, T)
        qs = _to_stream(u[:, base:base + DL_W], dil).astype(MXU_DTYPE)
        ks = _to_stream(u[:, base + DL_W:base + 2 * DL_W], dil).astype(MXU_DTYPE)
        vs = _to_stream(u[:, base + 2 * DL_W:base + 3 * DL_W], dil).astype(MXU_DTYPE)
        bias = p["rel_bias"][bq].transpose(2, 0, 1)
        o_s, lse_s = band_fwd(qs, _pad_l(ks), _pad_l(vs), bias, scale=scale)
        per_group.append(dict(qs=qs, ks=ks, vs=vs, lse_s=lse_s, bq=bq, bk=bk, dil=dil))
        o_hm.append(_stream_to_hm(o_s))
        lse_hm.append(_stream_to_hm(lse_s))
    rows = [(t, 0, False) for t in o_hm] + [(t, 0, False, 1) for t in lse_hm]
    om = rowwise_fwd("dl_merge", _dl_merge_fn, rows, [], [(DL_HD, F32)], W=DL_HD)[0]
    o = om.reshape(DL_HEADS, S, DL_HD).transpose(1, 0, 2).reshape(S, DL_W)
    g = rowwise_fwd("dl_gate", _gate_fn, [(o, 0, False), (u, 9, False)], [], [(DL_W, MXU_DTYPE)], W=DL_W)[0]
    xn = matmul(g, p["w_out"], residual=x, name="dl_out")
    return xn, dict(x=x, ng=ng, hn=hn, u=u, per_group=per_group, o_hm=o_hm, lse_hm=lse_hm, o=o, g=g)


def dl_layer_bwd(sv, p, dx):
    u = sv["u"]
    S = u.shape[0]
    scale = DL_HD ** -0.5
    dg = matmul(dx, p["w_out"], tb=True, name="dl_dg")
    dw_out = matmul(sv["g"], dx, ta=True, name="dl_dw_out")
    do, dgate = rowwise_bwd("dl_gate_bwd", _gate_fn, [(sv["o"], 0, False), (u, 9, False)], [], [dg], W=DL_W,
                            diff_rows=[0, 1], diff_shared=[])
    do_hm = do.reshape(S, DL_HEADS, DL_HD).transpose(1, 0, 2).reshape(DL_HEADS * S, DL_HD)
    rows = [(t, 0, False) for t in sv["o_hm"]] + [(t, 0, False, 1) for t in sv["lse_hm"]]
    dmerge = rowwise_bwd("dl_merge_bwd", _dl_merge_fn, rows, [], [do_hm], W=DL_HD, diff_rows=[0, 1, 2, 3, 4, 5],
                         diff_shared=[])
    parts, dbs, onehots = [], [], []
    for gi, pg in enumerate(sv["per_group"]):
        dil = pg["dil"]
        Ls = S // dil
        T, W = _dl_tiles(Ls)
        dog, dlg = dmerge[gi], dmerge[3 + gi]
        dm = rowwise_fwd(f"dl_dm_{gi}", _dl_dm_fn, [(dog, 0, False), (sv["o_hm"][gi], 0, False), (dlg, 0, False, 1)], [],
                         [(1, F32)], W=DL_HD)[0]
        do_s, dm_s = _hm_to_stream(dog, dil), _hm_to_stream(dm, dil)
        bias = p["rel_bias"][pg["bq"]].transpose(2, 0, 1)
        bias_t = p["rel_bias"][pg["bk"]].transpose(2, 0, 1)
        kp, vp = _pad_l(pg["ks"]), _pad_l(pg["vs"])
        dq_s, dbias = band_bwd_dq(pg["qs"], kp, vp, bias, pg["lse_s"], dm_s, do_s, scale=scale)
        dk_s, dv_s = band_bwd_dkv(_pad_l(pg["qs"]), pg["ks"], pg["vs"], bias_t, _pad_l(pg["lse_s"]), _pad_l(dm_s),
                                  _pad_l(do_s), scale=scale)
        parts += [_from_stream(dq_s), _from_stream(dk_s), _from_stream(dv_s)]
        dbs.append(dbias.reshape(DL_HEADS, T * W))
        onehots.append((pg["bq"].reshape(-1)[:, None] == np.arange(REL_BUCKETS)[None, :]).astype(np.float32))
    drel = matmul(jnp.concatenate(dbs, axis=1), jnp.asarray(np.concatenate(onehots, axis=0)), exact=True,
                  name="dl_drel", tm=DL_HEADS, tn=REL_BUCKETS, tk=2048)
    du = jnp.concatenate(parts + [dgate], axis=1)
    dx_prev, dng, dw_in = _in_out_bwd("dl", sv["x"], sv["ng"], sv["hn"], du, p["w_in"], dx)
    return dx_prev, dict(w_in=dw_in, w_out=dw_out, rel_bias=drel.T, ng=dng)


WEIGHT_ORDER = ['norm_g', 'final_g', 'rel_bias', 'hgrn_lb', 'ssd_w_in', 'ssd_conv_w', 'ssd_conv_b', 'ssd_dt_bias',
                'ssd_a_log', 'ssd_d', 'ssd_norm_g', 'ssd_w_out', 'hg_w_in', 'hg_norm_g', 'hg_w_out', 'at_w_in',
                'at_q_norm_g', 'at_k_norm_g', 'at_w_out', 'dl_w_in', 'dl_w_out']
BIG_IN = ['ssd_w_in', 'hg_w_in', 'at_w_in', 'dl_w_in']
BIG_OUT = ['ssd_w_out', 'hg_w_out', 'at_w_out', 'dl_w_out']
BIG = BIG_IN + BIG_OUT
SMALL = [n for n in WEIGHT_ORDER if n not in BIG]
LANES = 128


def _pack(arrs):
    flat = jnp.concatenate([a.reshape(-1).astype(F32) for a in arrs])
    n = flat.shape[0]
    rows = -(-n // (8 * LANES)) * 8
    return jnp.pad(flat, (0, rows * LANES - n)).reshape(rows, LANES)


def _unpack(buf, shapes):
    flat = buf.reshape(-1)
    out, off = [], 0
    for shp in shapes:
        n = int(np.prod(shp)) if len(shp) else 1
        out.append(flat[off:off + n].reshape(shp))
        off += n
    return out


def kernel(x, norm_g, final_g, rel_bias, hgrn_lb, ssd_w_in, ssd_conv_w, ssd_conv_b, ssd_dt_bias, ssd_a_log, ssd_d, ssd_norm_g, ssd_w_out, hg_w_in, hg_norm_g, hg_w_out, at_w_in, at_q_norm_g, at_k_norm_g, at_w_out, dl_w_in, dl_w_out, loss_target, m_norm_g, m_final_g, m_rel_bias, m_hgrn_lb, m_ssd_w_in, m_ssd_conv_w, m_ssd_conv_b, m_ssd_dt_bias, m_ssd_a_log, m_ssd_d, m_ssd_norm_g, m_ssd_w_out, m_hg_w_in, m_hg_norm_g, m_hg_w_out, m_at_w_in, m_at_q_norm_g, m_at_k_norm_g, m_at_w_out, m_dl_w_in, m_dl_w_out, v_norm_g, v_final_g, v_rel_bias, v_hgrn_lb, v_ssd_w_in, v_ssd_conv_w, v_ssd_conv_b, v_ssd_dt_bias, v_ssd_a_log, v_ssd_d, v_ssd_norm_g, v_ssd_w_out, v_hg_w_in, v_hg_norm_g, v_hg_w_out, v_at_w_in, v_at_q_norm_g, v_at_k_norm_g, v_at_w_out, v_dl_w_in, v_dl_w_out):
    w = dict(norm_g=norm_g, final_g=final_g, rel_bias=rel_bias, hgrn_lb=hgrn_lb, ssd_w_in=ssd_w_in, ssd_conv_w=ssd_conv_w, ssd_conv_b=ssd_conv_b, ssd_dt_bias=ssd_dt_bias, ssd_a_log=ssd_a_log, ssd_d=ssd_d, ssd_norm_g=ssd_norm_g, ssd_w_out=ssd_w_out, hg_w_in=hg_w_in, hg_norm_g=hg_norm_g, hg_w_out=hg_w_out, at_w_in=at_w_in, at_q_norm_g=at_q_norm_g, at_k_norm_g=at_k_norm_g, at_w_out=at_w_out, dl_w_in=dl_w_in, dl_w_out=dl_w_out)
    m = dict(norm_g=m_norm_g, final_g=m_final_g, rel_bias=m_rel_bias, hgrn_lb=m_hgrn_lb, ssd_w_in=m_ssd_w_in, ssd_conv_w=m_ssd_conv_w, ssd_conv_b=m_ssd_conv_b, ssd_dt_bias=m_ssd_dt_bias, ssd_a_log=m_ssd_a_log, ssd_d=m_ssd_d, ssd_norm_g=m_ssd_norm_g, ssd_w_out=m_ssd_w_out, hg_w_in=m_hg_w_in, hg_norm_g=m_hg_norm_g, hg_w_out=m_hg_w_out, at_w_in=m_at_w_in, at_q_norm_g=m_at_q_norm_g, at_k_norm_g=m_at_k_norm_g, at_w_out=m_at_w_out, dl_w_in=m_dl_w_in, dl_w_out=m_dl_w_out)
    v = dict(norm_g=v_norm_g, final_g=v_final_g, rel_bias=v_rel_bias, hgrn_lb=v_hgrn_lb, ssd_w_in=v_ssd_w_in, ssd_conv_w=v_ssd_conv_w, ssd_conv_b=v_ssd_conv_b, ssd_dt_bias=v_ssd_dt_bias, ssd_a_log=v_ssd_a_log, ssd_d=v_ssd_d, ssd_norm_g=v_ssd_norm_g, ssd_w_out=v_ssd_w_out, hg_w_in=v_hg_w_in, hg_norm_g=v_hg_norm_g, hg_w_out=v_hg_w_out, at_w_in=v_at_w_in, at_q_norm_g=v_at_q_norm_g, at_k_norm_g=v_at_k_norm_g, at_w_out=v_at_w_out, dl_w_in=v_dl_w_in, dl_w_out=v_dl_w_out)
    me = 4 * lax.axis_index("x") + 2 * lax.axis_index("y") + lax.axis_index("c")
    xs = x[0]
    S = xs.shape[0]

    shard2d = {n: w[n][0] for n in BIG}
    flat = jnp.concatenate([shard2d[n].astype(MXU_DTYPE).reshape(-1, D_MODEL) for n in BIG], axis=0)
    gathered = allgather_two_level(flat, name="allgather_weights")
    full, off = {}, 0
    for n in BIG:
        r, c = shard2d[n].shape
        nr = r * c // D_MODEL
        blk = gathered[:, off:off + nr].reshape(N_DEV, r, c)
        off += nr
        full[n] = blk.transpose(1, 0, 2).reshape(r, N_DEV * c) if n in BIG_IN else blk.reshape(N_DEV * r, c)
    ncw = ssd_conv_w.shape[2]
    nhg = hg_norm_g.shape[1]
    small_shard = jnp.zeros((8, 512), F32)
    small_shard = small_shard.at[:SSD_CONV, :ncw].set(ssd_conv_w[0]).at[SSD_CONV, :nhg].set(hg_norm_g[0])
    small_all = allgather_two_level(small_shard, name="allgather_small_weights")
    conv_w_full = small_all[:, :SSD_CONV, :ncw].transpose(1, 0, 2).reshape(SSD_CONV, N_DEV * ncw)
    hg_norm_full = small_all[:, SSD_CONV, :nhg].reshape(1, N_DEV * nhg)

    p_ssd = dict(w_in=jnp.pad(full["ssd_w_in"], ((0, 0), (0, SSD_IN_PAD - SSD_IN))), w_out=full["ssd_w_out"],
                 conv_w=conv_w_full, conv_b=ssd_conv_b,
                 dt_bias=jnp.pad(ssd_dt_bias.reshape(1, 2 * SSD_HEADS), ((0, 0), (0, 128 - 2 * SSD_HEADS))),
                 alog=jnp.pad(ssd_a_log.reshape(1, 2 * SSD_HEADS), ((0, 0), (0, 128 - 2 * SSD_HEADS))),
                 dexp=jnp.repeat(ssd_d.reshape(-1), SSD_HEADDIM)[None, :], norm_g=ssd_norm_g)
    p_hg = dict(w_in=full["hg_w_in"], w_out=full["hg_w_out"], norm_g=hg_norm_full, hgrn_lb=hgrn_lb)
    p_at = dict(w_in=full["at_w_in"], w_out=full["at_w_out"], q_g=at_q_norm_g, k_g=at_k_norm_g)
    p_dl = dict(w_in=full["dl_w_in"], w_out=full["dl_w_out"], rel_bias=rel_bias)

    x1, sv0 = ssd_layer_fwd(xs, norm_g[0:1], p_ssd)
    x2, sv1 = hg_layer_fwd(x1, norm_g[1:2], p_hg)
    x3, sv2 = at_layer_fwd(x2, norm_g[2:3], p_at)
    x4, sv3 = dl_layer_fwd(x3, norm_g[3:4], p_dl)
    loss_part, dx4, dfinal = loss_head(x4, final_g[None, :], loss_target[0])
    dx3, g3 = dl_layer_bwd(sv3, p_dl, dx4)
    dx2, g2 = at_layer_bwd(sv2, p_at, dx3)
    dx1, g1 = hg_layer_bwd(sv1, p_hg, dx2)
    dx0, g0 = ssd_layer_bwd(sv0, p_ssd, dx1)

    small_full = dict(
        norm_g=jnp.concatenate([g0["ng"], g1["ng"], g2["ng"], g3["ng"]], axis=0), final_g=dfinal[0],
        rel_bias=g3["rel_bias"], hgrn_lb=g1["hgrn_lb"], ssd_conv_w=g0["conv_w"][None], ssd_conv_b=g0["conv_b"],
        ssd_dt_bias=g0["dt_bias"].reshape(1, 2, SSD_HEADS), ssd_a_log=g0["a_log"].reshape(1, 2, SSD_HEADS),
        ssd_d=g0["d"], ssd_norm_g=g0["norm_g"], hg_norm_g=g1["norm_g"], at_q_norm_g=g2["q_g"], at_k_norm_g=g2["k_g"])
    packed = _pack([loss_part[0, 0:1]] + [small_full[n] for n in SMALL])
    summed = sum_parts(allgather_two_level(packed, name="allgather_small_grads"), name="sum_small_grads")
    parts = _unpack(summed, [()] + [small_full[n].shape for n in SMALL])
    loss = parts[0]
    gsmall = dict(zip(SMALL, parts[1:]))
    gsmall["ssd_conv_w"] = lax.dynamic_slice_in_dim(gsmall["ssd_conv_w"], me * ncw, ncw, axis=2)
    gsmall["hg_norm_g"] = lax.dynamic_slice_in_dim(gsmall["hg_norm_g"], me * nhg, nhg, axis=1)
    shapes = [w[n].shape for n in SMALL]
    d_p, m_p, v_p = adamw_plain(_pack([w[n] for n in SMALL]), _pack([gsmall[n] for n in SMALL]),
                                _pack([m[n] for n in SMALL]), _pack([v[n] for n in SMALL]), name="adamw_small")
    grads = dict(gsmall)
    deltas = dict(zip(SMALL, _unpack(d_p, shapes)))
    new_m = dict(zip(SMALL, _unpack(m_p, shapes)))
    new_v = dict(zip(SMALL, _unpack(v_p, shapes)))

    gbig = dict(ssd_w_in=g0["w_in"], ssd_w_out=g0["w_out"], hg_w_in=g1["w_in"], hg_w_out=g1["w_out"],
                at_w_in=g2["w_in"], at_w_out=g2["w_out"], dl_w_in=g3["w_in"], dl_w_out=g3["w_out"])
    send = []
    for n in BIG:
        r, c = shard2d[n].shape
        g = gbig[n]
        send.append(g.reshape(r, N_DEV, c).transpose(1, 0, 2) if n in BIG_IN else g.reshape(N_DEV, r, c))
    recv = all_to_all(send, name="exchange_weight_grads")
    for n, parts8 in zip(BIG, recv):
        gs, ds, ms, vs = adamw_sum(parts8, shard2d[n], m[n][0], v[n][0], name=f"adamw_{n}")
        grads[n], deltas[n], new_m[n], new_v[n] = gs[None], ds[None], ms[None], vs[None]

    return (loss, dx0[None], *[grads[n] for n in WEIGHT_ORDER], *[deltas[n] for n in WEIGHT_ORDER],
            *[new_m[n] for n in WEIGHT_ORDER], *[new_v[n] for n in WEIGHT_ORDER])
```

```python
import functools
import math

import jax
import jax.numpy as jnp
import numpy as np
from jax import lax
from jax.experimental import pallas as pl
from jax.experimental.pallas import tpu as pltpu

F32 = jnp.float32
BF16 = jnp.bfloat16
MXU_DTYPE = jnp.bfloat16
GRAD_WIRE_DTYPE = jnp.bfloat16
HIGHEST = lax.Precision.HIGHEST
MESH_ID = pl.DeviceIdType.MESH
N_DEV = 8

D_MODEL = 1024
EPS = 1e-6
NEG_BIG = -1e30

SSD_DI = 2048
SSD_HEADDIM = 64
SSD_HEADS = 32
SSD_GROUPS = 4
SSD_HPG = 8
SSD_STATE = 128
SSD_CONV = 7
SSD_CHUNK = 128
SSD_CONV_CH = SSD_DI + 2 * SSD_GROUPS * SSD_STATE
SSD_IN = SSD_DI + SSD_CONV_CH + 2 * SSD_HEADS
SSD_IN_PAD = 5376

HG_CHUNK = 32
HG_HEADS = 8
HG_D = 128
HG_W = 1024

AT_HEADS = 16
AT_KV = 8
AT_HD = 128
AT_QW = 2048
AT_KW = 1024
GRID_W = 64
ROPE_THETA = 10000.0

DL_PAIRS = ((128, 1), (512, 4), (2048, 16))
DL_HEADS = 16
DL_HD = 64
DL_W = 1024
DL_HALF = 64
REL_BUCKETS = 32
REL_MAX_DIST = 1024

ADAM_LR = 0.001
ADAM_B1 = 0.9
ADAM_B2 = 0.999
ADAM_EPS = 1e-08
ADAM_WD = 0.01
ADAM_STEP = 10

VMEM_LIMIT = 56 * 1024 * 1024


def _cp(*sem):
    return pltpu.CompilerParams(dimension_semantics=tuple(sem), vmem_limit_bytes=VMEM_LIMIT)


def _tile(n, cands=(1024, 768, 512, 384, 256, 128)):
    for c in cands:
        if n % c == 0:
            return c
    return n


def _dot(a, b, dims):
    return lax.dot_general(a.astype(MXU_DTYPE), b.astype(MXU_DTYPE), (dims, ((), ())), preferred_element_type=F32)


def _dot_exact(a, b, dims):
    return lax.dot_general(a, b, (dims, ((), ())), precision=HIGHEST, preferred_element_type=F32)


def _silu(x):
    return x * jax.nn.sigmoid(x)


def matmul(a, b, *, name, ta=False, tb=False, residual=None, out_dtype=F32, exact=False, tm=None, tn=None, tk=None):
    M, K = (a.shape[1], a.shape[0]) if ta else a.shape
    N = b.shape[0] if tb else b.shape[1]
    tm = tm or _tile(M, (512, 256, 128))
    tn = tn or _tile(N, (1024, 768, 512, 384, 256, 128))
    tk = tk or _tile(K, (1024, 768, 512, 384, 256, 128))
    nk = K // tk
    dims = (((0,) if ta else (1,)), ((1,) if tb else (0,)))

    def body(*refs):
        if residual is None:
            a_ref, b_ref, o_ref, acc = refs
            r_ref = None
        else:
            a_ref, b_ref, r_ref, o_ref, acc = refs
        k = pl.program_id(2)

        @pl.when(k == 0)
        def _():
            acc[...] = jnp.zeros_like(acc)

        if exact:
            acc[...] += _dot_exact(a_ref[...], b_ref[...], dims)
        else:
            acc[...] += _dot(a_ref[...], b_ref[...], dims)

        @pl.when(k == nk - 1)
        def _():
            r = acc[...]
            if r_ref is not None:
                r = r + r_ref[...]
            o_ref[...] = r.astype(o_ref.dtype)

    a_spec = pl.BlockSpec((tk, tm), lambda i, j, k: (k, i)) if ta else pl.BlockSpec((tm, tk), lambda i, j, k: (i, k))
    b_spec = pl.BlockSpec((tn, tk), lambda i, j, k: (j, k)) if tb else pl.BlockSpec((tk, tn), lambda i, j, k: (k, j))
    in_specs = [a_spec, b_spec]
    args = [a, b]
    if residual is not None:
        in_specs.append(pl.BlockSpec((tm, tn), lambda i, j, k: (i, j)))
        args.append(residual)
    return pl.pallas_call(
        body, name=name, grid=(M // tm, N // tn, nk), in_specs=in_specs,
        out_specs=pl.BlockSpec((tm, tn), lambda i, j, k: (i, j)),
        out_shape=jax.ShapeDtypeStruct((M, N), out_dtype),
        scratch_shapes=[pltpu.VMEM((tm, tn), F32)],
        compiler_params=_cp("parallel", "parallel", "arbitrary"),
    )(*args)


def _row_specs(rows, shared, R, W):
    specs = []
    for arr, col0, per_j, *wd in rows:
        w = wd[0] if wd else W
        specs.append(pl.BlockSpec((R, w), (lambda j, i, c=col0: (i, c + j)) if per_j else (lambda j, i, c=col0: (i, c))))
    for arr, per_j in shared:
        specs.append(pl.BlockSpec((arr.shape[0], W), (lambda j, i: (0, j)) if per_j else (lambda j, i: (0, 0))))
    return specs


def rowwise_fwd(name, fn, rows, shared, outs, *, W, ncb=1, R=256):
    S = rows[0][0].shape[0]
    R = min(R, S)
    nr, ns = len(rows), len(shared)

    def body(*refs):
        vals = [r[...] for r in refs[:nr + ns]]
        res = fn(*vals)
        for o_ref, r in zip(refs[nr + ns:], res):
            o_ref[...] = r.astype(o_ref.dtype)

    return pl.pallas_call(
        body, name=name, grid=(ncb, S // R),
        in_specs=_row_specs(rows, shared, R, W),
        out_specs=[pl.BlockSpec((R, w), lambda j, i: (i, j)) for w, _ in outs],
        out_shape=[jax.ShapeDtypeStruct((S, ncb * w), dt) for w, dt in outs],
        compiler_params=_cp("parallel", "parallel"),
    )(*[r[0] for r in rows], *[s[0] for s in shared])


def rowwise_bwd(name, fn, rows, shared, cots, *, W, ncb=1, R=256, diff_rows, diff_shared, add=None):
    S = rows[0][0].shape[0]
    R = min(R, S)
    nr, ns, nc = len(rows), len(shared), len(cots)
    nsteps = S // R

    def body(*refs):
        ins = refs[:nr + ns]
        ct_refs = refs[nr + ns:nr + ns + nc]
        pos = nr + ns + nc
        add_ref = None
        if add is not None:
            add_ref = refs[pos]
            pos += 1
        drow_refs = refs[pos:pos + len(diff_rows)]
        dsh_refs = refs[pos + len(diff_rows):]
        j, i = pl.program_id(0), pl.program_id(1)
        vals = [r[...] for r in ins]

        def f(*dv):
            full = list(vals)
            for idx, v in zip(list(diff_rows) + [nr + s for s in diff_shared], dv):
                full[idx] = v
            return tuple(fn(*full))

        prim = [vals[idx] for idx in diff_rows] + [vals[nr + s] for s in diff_shared]
        _, vjp = jax.vjp(f, *prim)
        grads = vjp(tuple(c[...] for c in ct_refs))
        for k, d_ref in enumerate(drow_refs):
            g = grads[k]
            if k == 0 and add_ref is not None:
                g = g + add_ref[...]
            d_ref[...] = g
        for k, (d_ref, s) in enumerate(zip(dsh_refs, diff_shared)):
            g = grads[len(diff_rows) + k]
            first = (i == 0) if shared[s][1] else jnp.logical_and(i == 0, j == 0)

            @pl.when(first)
            def _(d_ref=d_ref, g=g):
                d_ref[...] = g

            @pl.when(jnp.logical_not(first))
            def _(d_ref=d_ref, g=g):
                d_ref[...] += g

    in_specs = _row_specs(rows, shared, R, W)
    wo = [c.shape[1] // ncb for c in cots]
    in_specs += [pl.BlockSpec((R, w), lambda j, i: (i, j)) for w in wo]
    args = [r[0] for r in rows] + [s[0] for s in shared] + list(cots)
    if add is not None:
        in_specs.append(pl.BlockSpec((R, W), lambda j, i: (i, j)))
        args.append(add)
    dws = [(rows[r][3] if len(rows[r]) > 3 else W) for r in diff_rows]
    out_specs = [pl.BlockSpec((R, w), lambda j, i: (i, j)) for w in dws]
    out_shape = [jax.ShapeDtypeStruct((S, ncb * w), F32) for w in dws]
    for s in diff_shared:
        arr, per_j = shared[s]
        out_specs.append(pl.BlockSpec((arr.shape[0], W), (lambda j, i: (0, j)) if per_j else (lambda j, i: (0, 0))))
        out_shape.append(jax.ShapeDtypeStruct((arr.shape[0], ncb * W if per_j else W), F32))
    return pl.pallas_call(
        body, name=name, grid=(ncb, nsteps), in_specs=in_specs, out_specs=out_specs, out_shape=out_shape,
        compiler_params=_cp("arbitrary", "arbitrary"),
    )(*args)


def _rms(x, g):
    return x * lax.rsqrt(jnp.mean(x * x, axis=-1, keepdims=True) + EPS) * g


def _prenorm_fn(x, g):
    return (_rms(x, g),)


def loss_head(x, g, tgt, *, R=256):
    S, D = x.shape
    R = min(R, S)

    def fn(xv, gv, tv):
        err = _rms(xv, gv) - tv
        return 0.5 * jnp.sum(jnp.mean(err * err, axis=-1, keepdims=True), axis=0, keepdims=True)

    def body(x_ref, g_ref, t_ref, loss_ref, dx_ref, dg_ref):
        i = pl.program_id(0)
        tv = t_ref[...]
        val, vjp = jax.vjp(lambda a, b: fn(a, b, tv), x_ref[...], g_ref[...])
        dx, dg = vjp(jnp.ones((1, 1), F32))
        dx_ref[...] = dx

        @pl.when(i == 0)
        def _():
            loss_ref[...] = jnp.zeros_like(loss_ref) + val
            dg_ref[...] = dg

        @pl.when(i != 0)
        def _():
            loss_ref[...] += val
            dg_ref[...] += dg

    return pl.pallas_call(
        body, name="loss_head", grid=(S // R,),
        in_specs=[pl.BlockSpec((R, D), lambda i: (i, 0)), pl.BlockSpec((1, D), lambda i: (0, 0)),
                  pl.BlockSpec((R, D), lambda i: (i, 0))],
        out_specs=[pl.BlockSpec((1, 128), lambda i: (0, 0)), pl.BlockSpec((R, D), lambda i: (i, 0)),
                   pl.BlockSpec((1, D), lambda i: (0, 0))],
        out_shape=[jax.ShapeDtypeStruct((1, 128), F32), jax.ShapeDtypeStruct((S, D), F32),
                   jax.ShapeDtypeStruct((1, D), F32)],
        compiler_params=_cp("arbitrary"),
    )(x, g, tgt)


@jax.custom_vjp
def _softplus(x):
    z = jnp.exp(-jnp.abs(x))
    u = 1.0 + z
    log1p = jnp.where(u == 1.0, z, jnp.log(u) * (z / jnp.where(u == 1.0, 1.0, u - 1.0)))
    return jnp.maximum(x, 0.0) + log1p


def _softplus_fwd(x):
    return _softplus(x), x


def _softplus_bwd(x, ct):
    return (ct * jax.nn.sigmoid(x),)


_softplus.defvjp(_softplus_fwd, _softplus_bwd)


def _dt_fn(raw, bias):
    return (_softplus(raw + bias),)


CONV_CB = 256
CONV_RB = 512
CONV_PAD = 8


def ssd_conv_fwd(u, conv_w, conv_b):
    S = u.shape[0]
    ncb = SSD_CONV_CH // CONV_CB
    col0 = SSD_DI // CONV_CB
    RB = min(CONV_RB, S)

    def body(x_ref, w_ref, b_ref, o_ref, pad):
        pad[0:CONV_PAD, :] = jnp.zeros((CONV_PAD, CONV_CB), F32)
        pad[S + CONV_PAD:S + 2 * CONV_PAD, :] = jnp.zeros((CONV_PAD, CONV_CB), F32)
        pad[CONV_PAD:S + CONV_PAD, :] = x_ref[...]
        w = w_ref[...]
        b = b_ref[...]
        for r in range(S // RB):
            acc = jnp.zeros((RB, CONV_CB), F32) + b
            for k in range(SSD_CONV):
                off = r * RB + CONV_PAD + k - SSD_CONV // 2
                acc = acc + pad[off:off + RB, :] * w[k:k + 1, :]
            o_ref[r * RB:(r + 1) * RB, :] = _silu(acc)

    return pl.pallas_call(
        body, name="ssd_conv_fwd", grid=(ncb,),
        in_specs=[pl.BlockSpec((S, CONV_CB), lambda j: (0, col0 + j)),
                  pl.BlockSpec((SSD_CONV, CONV_CB), lambda j: (0, j)),
                  pl.BlockSpec((1, CONV_CB), lambda j: (0, j))],
        out_specs=pl.BlockSpec((S, CONV_CB), lambda j: (0, j)),
        out_shape=jax.ShapeDtypeStruct((S, SSD_CONV_CH), F32),
        scratch_shapes=[pltpu.VMEM((S + 2 * CONV_PAD, CONV_CB), F32)],
        compiler_params=_cp("parallel"),
    )(u, conv_w, conv_b)


def ssd_conv_bwd(u, conv_w, conv_b, dact):
    S = u.shape[0]
    ncb = SSD_CONV_CH // CONV_CB
    col0 = SSD_DI // CONV_CB
    RB = min(CONV_RB, S)
    half = SSD_CONV // 2

    def body(x_ref, w_ref, b_ref, da_ref, dx_ref, dw_ref, db_ref, xpad, dpad):
        z8 = jnp.zeros((CONV_PAD, CONV_CB), F32)
        xpad[0:CONV_PAD, :] = z8
        xpad[S + CONV_PAD:S + 2 * CONV_PAD, :] = z8
        dpad[0:CONV_PAD, :] = z8
        dpad[S + CONV_PAD:S + 2 * CONV_PAD, :] = z8
        xpad[CONV_PAD:S + CONV_PAD, :] = x_ref[...]
        w = w_ref[...]
        b = b_ref[...]
        dws = [jnp.zeros((1, CONV_CB), F32) for _ in range(SSD_CONV)]
        db = jnp.zeros((1, CONV_CB), F32)
        for r in range(S // RB):
            acc = jnp.zeros((RB, CONV_CB), F32) + b
            xs = []
            for k in range(SSD_CONV):
                off = r * RB + CONV_PAD + k - half
                xk = xpad[off:off + RB, :]
                xs.append(xk)
                acc = acc + xk * w[k:k + 1, :]
            sg = jax.nn.sigmoid(acc)
            dc = da_ref[r * RB:(r + 1) * RB, :] * (sg * (1.0 + acc * (1.0 - sg)))
            dpad[r * RB + CONV_PAD:(r + 1) * RB + CONV_PAD, :] = dc
            db = db + jnp.sum(dc, axis=0, keepdims=True)
            for k in range(SSD_CONV):
                dws[k] = dws[k] + jnp.sum(xs[k] * dc, axis=0, keepdims=True)
        for r in range(S // RB):
            acc = jnp.zeros((RB, CONV_CB), F32)
            for k in range(SSD_CONV):
                off = r * RB + CONV_PAD + half - k
                acc = acc + dpad[off:off + RB, :] * w[k:k + 1, :]
            dx_ref[r * RB:(r + 1) * RB, :] = acc
        for k in range(SSD_CONV):
            dw_ref[k:k + 1, :] = dws[k]
        dw_ref[SSD_CONV:SSD_CONV + 1, :] = jnp.zeros((1, CONV_CB), F32)
        db_ref[...] = db

    return pl.pallas_call(
        body, name="ssd_conv_bwd", grid=(ncb,),
        in_specs=[pl.BlockSpec((S, CONV_CB), lambda j: (0, col0 + j)),
                  pl.BlockSpec((SSD_CONV, CONV_CB), lambda j: (0, j)),
                  pl.BlockSpec((1, CONV_CB), lambda j: (0, j)),
                  pl.BlockSpec((S, CONV_CB), lambda j: (0, j))],
        out_specs=[pl.BlockSpec((S, CONV_CB), lambda j: (0, j)),
                   pl.BlockSpec((SSD_CONV + 1, CONV_CB), lambda j: (0, j)),
                   pl.BlockSpec((1, CONV_CB), lambda j: (0, j))],
        out_shape=[jax.ShapeDtypeStruct((S, SSD_CONV_CH), F32),
                   jax.ShapeDtypeStruct((SSD_CONV + 1, SSD_CONV_CH), F32),
                   jax.ShapeDtypeStruct((1, SSD_CONV_CH), F32)],
        scratch_shapes=[pltpu.VMEM((S + 2 * CONV_PAD, CONV_CB), F32), pltpu.VMEM((S + 2 * CONV_PAD, CONV_CB), F32)],
        compiler_params=_cp("parallel"),
    )(u, conv_w, conv_b, dact)


def _ssd_chunk(states, xs, Bg, Cg, dt, alog, sel, *, reverse):
    Q = SSD_CHUNK
    r = lax.broadcasted_iota(jnp.int32, (Q, Q), 0)
    c = lax.broadcasted_iota(jnp.int32, (Q, Q), 1)
    keep = (c >= r) if reverse else (c <= r)
    tri = jnp.where(keep, 1.0, 0.0).astype(F32)
    a = dt * (-jnp.exp(alog))
    cum = _dot_exact(tri, a, ((1,), (0,)))
    cum_t = jnp.transpose(cum)
    last = 0 if reverse else Q - 1
    CB = _dot(Cg, Bg, ((1,), (1,)))
    new_states, ys = [], []
    for h in range(SSD_HPG):
        s1 = sel[h]
        cum_c = jnp.sum(cum * s1, axis=1, keepdims=True)
        cum_r = jnp.sum(cum_t * jnp.transpose(s1), axis=0, keepdims=True)
        dt_c = jnp.sum(dt * s1, axis=1, keepdims=True)
        cum_l = cum_c[last:last + 1, :]
        diff = jnp.where(keep, cum_c - cum_r, 0.0)
        L = jnp.where(keep, jnp.exp(diff), 0.0)
        xdt = xs[h] * dt_c
        y = _dot(CB * L, xdt, ((1,), (0,)))
        y = y + _dot(Cg * jnp.exp(cum_c), states[h], ((1,), (0,)))
        ns = jnp.exp(cum_l) * states[h] + _dot(Bg * jnp.exp(cum_l - cum_c), xdt, ((0,), (0,)))
        new_states.append(ns)
        ys.append(y)
    return new_states, ys


def _ssd_sel(g, reverse):
    lane = lax.broadcasted_iota(jnp.int32, (1, 128), 1)
    base = g * SSD_HPG + (SSD_HEADS if reverse else 0)
    return [jnp.where(lane == base + h, 1.0, 0.0).astype(F32) for h in range(SSD_HPG)]


def ssd_scan_fwd(act, dt, alog, *, reverse, y_prev=None):
    S = act.shape[0]
    Q, N, P = SSD_CHUNK, SSD_STATE, SSD_HEADDIM
    nc = S // Q
    GW = SSD_HPG * P

    def cidx(i):
        return (nc - 1 - i) if reverse else i

    def body(*refs):
        if y_prev is None:
            x_ref, b_ref, c_ref, dt_ref, al_ref, y_ref, st_ref, state = refs
            yp_ref = None
        else:
            x_ref, b_ref, c_ref, dt_ref, al_ref, yp_ref, y_ref, st_ref, state = refs
        g, i = pl.program_id(0), pl.program_id(1)

        @pl.when(i == 0)
        def _():
            state[...] = jnp.zeros_like(state)

        states = [state[h] for h in range(SSD_HPG)]
        for h in range(SSD_HPG):
            st_ref[0, h] = states[h]
        xv = x_ref[...]
        xs = [xv[:, h * P:(h + 1) * P] for h in range(SSD_HPG)]
        ns, ys = _ssd_chunk(states, xs, b_ref[...], c_ref[...], dt_ref[...], al_ref[...], _ssd_sel(g, reverse),
                            reverse=reverse)
        for h in range(SSD_HPG):
            state[h] = ns[h]
            yh = ys[h]
            if yp_ref is not None:
                yh = yh + yp_ref[:, h * P:(h + 1) * P]
            y_ref[:, h * P:(h + 1) * P] = yh

    in_specs = [pl.BlockSpec((Q, GW), lambda g, i: (cidx(i), g)),
                pl.BlockSpec((Q, N), lambda g, i: (cidx(i), SSD_DI // N + g)),
                pl.BlockSpec((Q, N), lambda g, i: (cidx(i), SSD_DI // N + SSD_GROUPS + g)),
                pl.BlockSpec((Q, 128), lambda g, i: (cidx(i), 0)),
                pl.BlockSpec((1, 128), lambda g, i: (0, 0))]
    args = [act, act, act, dt, alog]
    if y_prev is not None:
        in_specs.append(pl.BlockSpec((Q, GW), lambda g, i: (cidx(i), g)))
        args.append(y_prev)
    return pl.pallas_call(
        body, name=f"ssd_scan_fwd_{int(reverse)}", grid=(SSD_GROUPS, nc), in_specs=in_specs,
        out_specs=[pl.BlockSpec((Q, GW), lambda g, i: (cidx(i), g)),
                   pl.BlockSpec((1, SSD_HPG, N, P), lambda g, i: (cidx(i), g, 0, 0))],
        out_shape=[jax.ShapeDtypeStruct((S, SSD_DI), F32), jax.ShapeDtypeStruct((nc, SSD_HEADS, N, P), F32)],
        scratch_shapes=[pltpu.VMEM((SSD_HPG, N, P), F32)],
        compiler_params=_cp("arbitrary", "arbitrary"),
    )(*args)


def ssd_scan_bwd(act, dt, alog, states, dy, prev_x, *, reverse, prev=None):
    S = act.shape[0]
    Q, N, P = SSD_CHUNK, SSD_STATE, SSD_HEADDIM
    nc = S // Q
    GW = SSD_HPG * P

    def cidx(i):
        return i if reverse else (nc - 1 - i)

    def body(*refs):
        x_ref, b_ref, c_ref, dt_ref, al_ref, st_ref, dy_ref, px_ref = refs[:8]
        pos = 8
        if prev is not None:
            pb_ref, pc_ref, pdt_ref, pal_ref = refs[pos:pos + 4]
            pos += 4
        dx_ref, db_ref, dc_ref, ddt_ref, dal_ref, dstate = refs[pos:]
        g, i = pl.program_id(0), pl.program_id(1)

        @pl.when(i == 0)
        def _():
            dstate[...] = jnp.zeros_like(dstate)

        sel = _ssd_sel(g, reverse)
        xv = x_ref[...]
        dyv = dy_ref[...]
        xs = [xv[:, h * P:(h + 1) * P] for h in range(SSD_HPG)]
        dys = [dyv[:, h * P:(h + 1) * P] for h in range(SSD_HPG)]
        states = [st_ref[0, h] for h in range(SSD_HPG)]
        dstates = [dstate[h] for h in range(SSD_HPG)]

        def f(states, xs, Bg, Cg, dtv, al):
            return _ssd_chunk(states, xs, Bg, Cg, dtv, al, sel, reverse=reverse)

        _, vjp = jax.vjp(f, states, xs, b_ref[...], c_ref[...], dt_ref[...], al_ref[...])
        dst, dxs, dB, dC, ddt, dal = vjp((dstates, dys))
        for h in range(SSD_HPG):
            dstate[h] = dst[h]
            dx_ref[:, h * P:(h + 1) * P] = dxs[h] + px_ref[:, h * P:(h + 1) * P]
        if prev is not None:
            dB = dB + pb_ref[...]
            dC = dC + pc_ref[...]
            ddt = ddt + pdt_ref[0]
        db_ref[...] = dB
        dc_ref[...] = dC
        ddt_ref[0] = ddt

        @pl.when(i == 0)
        def _():
            dal_ref[0] = dal + (pal_ref[0] if prev is not None else 0.0)

        @pl.when(i != 0)
        def _():
            dal_ref[0] += dal

    xspec = pl.BlockSpec((Q, GW), lambda g, i: (cidx(i), g))
    nspec_b = pl.BlockSpec((Q, N), lambda g, i: (cidx(i), SSD_DI // N + g))
    nspec_c = pl.BlockSpec((Q, N), lambda g, i: (cidx(i), SSD_DI // N + SSD_GROUPS + g))
    dtspec = pl.BlockSpec((1, Q, 128), lambda g, i: (g, cidx(i), 0))
    alspec = pl.BlockSpec((1, 1, 128), lambda g, i: (g, 0, 0))
    in_specs = [xspec, nspec_b, nspec_c,
                pl.BlockSpec((Q, 128), lambda g, i: (cidx(i), 0)),
                pl.BlockSpec((1, 128), lambda g, i: (0, 0)),
                pl.BlockSpec((1, SSD_HPG, N, P), lambda g, i: (cidx(i), g, 0, 0)),
                xspec]
    gspec = pl.BlockSpec((Q, N), lambda g, i: (cidx(i), g))
    in_specs.append(xspec)
    args = [act, act, act, dt, alog, states, dy, prev_x]
    if prev is not None:
        in_specs += [gspec, gspec, dtspec, alspec]
        args += list(prev)
    outs = pl.pallas_call(
        body, name=f"ssd_scan_bwd_{int(reverse)}", grid=(SSD_GROUPS, nc), in_specs=in_specs,
        out_specs=[pl.BlockSpec((Q, GW), lambda g, i: (cidx(i), g)),
                   pl.BlockSpec((Q, N), lambda g, i: (cidx(i), g)),
                   pl.BlockSpec((Q, N), lambda g, i: (cidx(i), g)),
                   dtspec, alspec],
        out_shape=[jax.ShapeDtypeStruct((S, SSD_DI), F32), jax.ShapeDtypeStruct((S, SSD_GROUPS * N), F32),
                   jax.ShapeDtypeStruct((S, SSD_GROUPS * N), F32),
                   jax.ShapeDtypeStruct((SSD_GROUPS, S, 128), F32), jax.ShapeDtypeStruct((SSD_GROUPS, 1, 128), F32)],
        scratch_shapes=[pltpu.VMEM((SSD_HPG, N, P), F32)],
        compiler_params=_cp("arbitrary", "arbitrary"),
    )(*args)
    return outs


def _ssd_post_fn(y, xs, z, dexp, ng):
    t = (y + xs * dexp) * _silu(z)
    return (_rms(t, ng),)


def _hg_chunk(state, qraw, fraw, v, lb, *, reverse):
    C = HG_CHUNK
    r = lax.broadcasted_iota(jnp.int32, (C, C), 0)
    c = lax.broadcasted_iota(jnp.int32, (C, C), 1)
    keep = (c >= r) if reverse else (c <= r)
    tri = jnp.where(keep, 1.0, 0.0).astype(F32)
    q = _silu(qraw)
    f = lb + (1.0 - lb) * jax.nn.sigmoid(fraw)
    k = 1.0 - f
    g = jnp.log(f)
    G = _dot_exact(tri, g, ((1,), (0,)))
    ref_row = C // 2 - 1 if reverse else C // 2
    last_row = 0 if reverse else C - 1
    Gr = G[ref_row:ref_row + 1, :]
    Gl = G[last_row:last_row + 1, :]
    q_t = q * jnp.exp(G - Gr)
    k_t = k * jnp.exp(Gr - G)
    att = jnp.where(keep, _dot(q_t, k_t, ((1,), (1,))), 0.0)
    o = _dot(att, v, ((1,), (0,))) + _dot(q * jnp.exp(G), state, ((1,), (0,)))
    kd = k * jnp.exp(Gl - G)
    new_state = jnp.transpose(jnp.exp(Gl)) * state + _dot(kd, v, ((0,), (0,)))
    return new_state, o


def hg_scan_fwd(u, lb, *, reverse, o_prev=None, rows=256):
    S = u.shape[0]
    nh = HG_HEADS
    rows = min(rows, S)
    nsteps = S // rows
    ncb = rows // HG_CHUNK
    f_sec = 2 if reverse else 1

    def blk(i):
        return (nsteps - 1 - i) if reverse else i

    def body(*refs):
        if o_prev is None:
            q_ref, f_ref, v_ref, lb_ref, o_ref, st_ref, state = refs
            op_ref = None
        else:
            q_ref, f_ref, v_ref, lb_ref, op_ref, o_ref, st_ref, state = refs
        i = pl.program_id(0)

        @pl.when(i == 0)
        def _():
            state[...] = jnp.zeros_like(state)

        def chunk(cc, carry):
            ci = (ncb - 1 - cc) if reverse else cc
            sl = pl.ds(pl.multiple_of(ci * HG_CHUNK, HG_CHUNK), HG_CHUNK)
            for h in range(nh):
                hs = slice(h * HG_D, (h + 1) * HG_D)
                st = state[h]
                st_ref[ci, h] = st
                ns, o = _hg_chunk(st, q_ref[sl, hs], f_ref[sl, hs], v_ref[sl, hs], lb_ref[:, hs], reverse=reverse)
                state[h] = ns
                if op_ref is not None:
                    o = o + op_ref[sl, hs]
                o_ref[sl, hs] = o
            return carry

        lax.fori_loop(0, ncb, chunk, 0)

    rowspec = lambda sec: pl.BlockSpec((rows, HG_W), lambda i: (blk(i), sec))
    in_specs = [rowspec(0), rowspec(f_sec), rowspec(3), pl.BlockSpec((1, HG_W), lambda i: (0, 0))]
    args = [u, u, u, lb]
    if o_prev is not None:
        in_specs.append(rowspec(0))
        args.append(o_prev)
    return pl.pallas_call(
        body, name=f"hg_scan_fwd_{int(reverse)}", grid=(nsteps,), in_specs=in_specs,
        out_specs=[rowspec(0), pl.BlockSpec((ncb, nh, HG_D, HG_D), lambda i: (blk(i), 0, 0, 0))],
        out_shape=[jax.ShapeDtypeStruct((S, HG_W), F32), jax.ShapeDtypeStruct((S // HG_CHUNK, nh, HG_D, HG_D), F32)],
        scratch_shapes=[pltpu.VMEM((nh, HG_D, HG_D), F32)],
        compiler_params=_cp("arbitrary"),
    )(*args)


def hg_scan_bwd(u, lb, states, do, *, reverse, prev=None, rows=256):
    S = u.shape[0]
    nh = HG_HEADS
    rows = min(rows, S)
    nsteps = S // rows
    ncb = rows // HG_CHUNK
    f_sec = 2 if reverse else 1

    def blk(i):
        return i if reverse else (nsteps - 1 - i)

    def body(*refs):
        q_ref, f_ref, v_ref, lb_ref, st_ref, do_ref = refs[:6]
        pos = 6
        if prev is not None:
            pq_ref, pv_ref, plb_ref = refs[pos:pos + 3]
            pos += 3
        dq_ref, df_ref, dv_ref, dlb_ref, dstate = refs[pos:]
        i = pl.program_id(0)

        @pl.when(i == 0)
        def _():
            dstate[...] = jnp.zeros_like(dstate)
            dlb_ref[...] = plb_ref[...] if prev is not None else jnp.zeros_like(dlb_ref)

        def chunk(cc, carry):
            ci = cc if reverse else (ncb - 1 - cc)
            sl = pl.ds(pl.multiple_of(ci * HG_CHUNK, HG_CHUNK), HG_CHUNK)
            for h in range(nh):
                hs = slice(h * HG_D, (h + 1) * HG_D)
                _, vjp = jax.vjp(functools.partial(_hg_chunk, reverse=reverse), st_ref[ci, h],
                                 q_ref[sl, hs], f_ref[sl, hs], v_ref[sl, hs], lb_ref[:, hs])
                dst, dq, df, dv, dlb = vjp((dstate[h], do_ref[sl, hs]))
                dstate[h] = dst
                if prev is not None:
                    dq = dq + pq_ref[sl, hs]
                    dv = dv + pv_ref[sl, hs]
                dq_ref[sl, hs] = dq
                df_ref[sl, hs] = df
                dv_ref[sl, hs] = dv
                dlb_ref[:, hs] += dlb
            return carry

        lax.fori_loop(0, ncb, chunk, 0)

    rowspec = lambda sec: pl.BlockSpec((rows, HG_W), lambda i: (blk(i), sec))
    lbspec = pl.BlockSpec((1, HG_W), lambda i: (0, 0))
    in_specs = [rowspec(0), rowspec(f_sec), rowspec(3), lbspec,
                pl.BlockSpec((ncb, nh, HG_D, HG_D), lambda i: (blk(i), 0, 0, 0)), rowspec(0)]
    args = [u, u, u, lb, states, do]
    if prev is not None:
        in_specs += [rowspec(0), rowspec(0), lbspec]
        args += list(prev)
    return pl.pallas_call(
        body, name=f"hg_scan_bwd_{int(reverse)}", grid=(nsteps,), in_specs=in_specs,
        out_specs=[rowspec(0), rowspec(0), rowspec(0), lbspec],
        out_shape=[jax.ShapeDtypeStruct((S, HG_W), F32)] * 3 + [jax.ShapeDtypeStruct((1, HG_W), F32)],
        scratch_shapes=[pltpu.VMEM((nh, HG_D, HG_D), F32)],
        compiler_params=_cp("arbitrary"),
    )(*args)


def _hg_lb_fn(lbp):
    m = jnp.max(lbp, axis=0, keepdims=True)
    e = jnp.exp(lbp - m)
    sm = e / jnp.sum(e, axis=0, keepdims=True)
    return ((sm[0:1] + sm[1:2]) - sm[0:1],)


def hg_lb_fwd(lbp):
    def body(x_ref, o_ref):
        o_ref[...] = _hg_lb_fn(x_ref[...])[0]

    return pl.pallas_call(body, name="hg_lb_fwd", out_shape=jax.ShapeDtypeStruct((1, HG_W), F32))(lbp)


def hg_lb_bwd(lbp, dlb):
    def body(x_ref, d_ref, o_ref):
        _, vjp = jax.vjp(_hg_lb_fn, x_ref[...])
        o_ref[...] = vjp((d_ref[...],))[0]

    return pl.pallas_call(body, name="hg_lb_bwd", out_shape=jax.ShapeDtypeStruct(lbp.shape, F32))(lbp, dlb)


def _hg_post_fn(o, gate, ng):
    return (_rms(o, ng) * _silu(gate),)


def _gate_fn(o, gate):
    return (o * _silu(gate),)


def _rope_tables(S):
    t = np.arange(S)
    row = (t // GRID_W).astype(np.float32)
    col = (t % GRID_W).astype(np.float32)
    half = AT_HD // 4
    inv = (ROPE_THETA ** (-np.arange(0, 2 * half, 2, dtype=np.float32) / np.float32(2 * half))).astype(np.float32)
    ar = row[:, None] * inv[None, :]
    ac = col[:, None] * inv[None, :]
    return ar.astype(np.float32), ac.astype(np.float32)


def _rope_swap_matrix():
    p = np.zeros((AT_HD, AT_HD), np.float32)
    for i in range(AT_HD):
        p[(i + 32) if (i % 64) < 32 else (i - 32), i] = 1.0
    return p


def _make_qk_fn(scale):
    def fn(x, ct, st, g, perm):
        n = _rms(x, g)
        return ((n * ct + _dot_exact(n, perm, ((1,), (0,))) * st) * scale,)
    return fn


def flash_fwd(q, k, v, *, v_col0=0, tq=512, tk=512):
    S = q.shape[0]
    tq, tk = min(tq, S), min(tk, S)
    nk = S // tk
    G = AT_HEADS // AT_KV

    def body(q_ref, k_ref, v_ref, o_ref, lse_ref, m_s, l_s, acc):
        ki = pl.program_id(2)

        @pl.when(ki == 0)
        def _():
            m_s[...] = jnp.full_like(m_s, -jnp.inf)
            l_s[...] = jnp.zeros_like(l_s)
            acc[...] = jnp.zeros_like(acc)

        kv, vv = k_ref[...], v_ref[...]
        for g in range(G):
            s = _dot(q_ref[:, g * AT_HD:(g + 1) * AT_HD], kv, ((1,), (1,)))
            m_old = m_s[g]
            m_new = jnp.maximum(m_old, jnp.max(s, axis=1, keepdims=True))
            alpha = jnp.exp(m_old - m_new)
            p = jnp.exp(s - m_new)
            l_s[g] = alpha * l_s[g] + jnp.sum(p, axis=1, keepdims=True)
            acc[g] = alpha * acc[g] + _dot(p, vv, ((1,), (0,)))
            m_s[g] = m_new

        @pl.when(ki == nk - 1)
        def _():
            for g in range(G):
                o_ref[:, g * AT_HD:(g + 1) * AT_HD] = acc[g] / l_s[g]
                lse_ref[0, :, g:g + 1] = m_s[g] + jnp.log(l_s[g])

    return pl.pallas_call(
        body, name="flash_fwd", grid=(AT_KV, S // tq, nk),
        in_specs=[pl.BlockSpec((tq, G * AT_HD), lambda h, i, j: (i, h)),
                  pl.BlockSpec((tk, AT_HD), lambda h, i, j: (j, h)),
                  pl.BlockSpec((tk, AT_HD), lambda h, i, j: (j, v_col0 + h))],
        out_specs=[pl.BlockSpec((tq, G * AT_HD), lambda h, i, j: (i, h)),
                   pl.BlockSpec((1, tq, G), lambda h, i, j: (h, i, 0))],
        out_shape=[jax.ShapeDtypeStruct((S, AT_QW), F32), jax.ShapeDtypeStruct((AT_KV, S, G), F32)],
        scratch_shapes=[pltpu.VMEM((G, tq, 1), F32), pltpu.VMEM((G, tq, 1), F32), pltpu.VMEM((G, tq, AT_HD), F32)],
        compiler_params=_cp("parallel", "parallel", "arbitrary"),
    )(q, k, v)


def flash_bwd_dq(q, k, v, o, lse, do, *, v_col0=0, tq=512, tk=512):
    S = q.shape[0]
    tq, tk = min(tq, S), min(tk, S)
    nk = S // tk
    G = AT_HEADS // AT_KV

    def body(q_ref, k_ref, v_ref, o_ref, lse_ref, do_ref, dq_ref, dl_ref, acc, dl_s):
        ki = pl.program_id(2)

        @pl.when(ki == 0)
        def _():
            acc[...] = jnp.zeros_like(acc)
            for g in range(G):
                sl = slice(g * AT_HD, (g + 1) * AT_HD)
                dl_s[g] = jnp.sum(do_ref[:, sl] * o_ref[:, sl], axis=1, keepdims=True)

        kv, vv = k_ref[...], v_ref[...]
        for g in range(G):
            sl = slice(g * AT_HD, (g + 1) * AT_HD)
            s = _dot(q_ref[:, sl], kv, ((1,), (1,)))
            p = jnp.exp(s - lse_ref[0, :, g:g + 1])
            dp = _dot(do_ref[:, sl], vv, ((1,), (1,)))
            ds = p * (dp - dl_s[g])
            acc[g] += _dot(ds, kv, ((1,), (0,)))

        @pl.when(ki == nk - 1)
        def _():
            for g in range(G):
                dq_ref[:, g * AT_HD:(g + 1) * AT_HD] = acc[g]
                dl_ref[0, :, g:g + 1] = dl_s[g]

    qspec = pl.BlockSpec((tq, G * AT_HD), lambda h, i, j: (i, h))
    kspec = pl.BlockSpec((tk, AT_HD), lambda h, i, j: (j, h))
    lspec = pl.BlockSpec((1, tq, G), lambda h, i, j: (h, i, 0))
    return pl.pallas_call(
        body, name="flash_bwd_dq", grid=(AT_KV, S // tq, nk),
        in_specs=[qspec, kspec, pl.BlockSpec((tk, AT_HD), lambda h, i, j: (j, v_col0 + h)), qspec, lspec, qspec],
        out_specs=[qspec, lspec],
        out_shape=[jax.ShapeDtypeStruct((S, AT_QW), F32), jax.ShapeDtypeStruct((AT_KV, S, G), F32)],
        scratch_shapes=[pltpu.VMEM((G, tq, AT_HD), F32), pltpu.VMEM((G, tq, 1), F32)],
        compiler_params=_cp("parallel", "parallel", "arbitrary"),
    )(q, k, v, o, lse, do)


def flash_bwd_dkv(q, k, v, lse, delta, do, *, v_col0=0, tq=512, tk=512):
    S = q.shape[0]
    tq, tk = min(tq, S), min(tk, S)
    nq = S // tq
    G = AT_HEADS // AT_KV

    def body(q_ref, k_ref, v_ref, lse_ref, dl_ref, do_ref, dk_ref, dv_ref, dk_acc, dv_acc):
        qi = pl.program_id(2)

        @pl.when(qi == 0)
        def _():
            dk_acc[...] = jnp.zeros_like(dk_acc)
            dv_acc[...] = jnp.zeros_like(dv_acc)

        kv, vv = k_ref[...], v_ref[...]
        for g in range(G):
            sl = slice(g * AT_HD, (g + 1) * AT_HD)
            qg, dog = q_ref[:, sl], do_ref[:, sl]
            s = _dot(qg, kv, ((1,), (1,)))
            p = jnp.exp(s - lse_ref[0, :, g:g + 1])
            dv_acc[...] += _dot(p, dog, ((0,), (0,)))
            dp = _dot(dog, vv, ((1,), (1,)))
            ds = p * (dp - dl_ref[0, :, g:g + 1])
            dk_acc[...] += _dot(ds, qg, ((0,), (0,)))

        @pl.when(qi == nq - 1)
        def _():
            dk_ref[...] = dk_acc[...]
            dv_ref[...] = dv_acc[...]

    qspec = pl.BlockSpec((tq, G * AT_HD), lambda h, j, i: (i, h))
    kspec = pl.BlockSpec((tk, AT_HD), lambda h, j, i: (j, h))
    lspec = pl.BlockSpec((1, tq, G), lambda h, j, i: (h, i, 0))
    return pl.pallas_call(
        body, name="flash_bwd_dkv", grid=(AT_KV, S // tk, nq),
        in_specs=[qspec, kspec, pl.BlockSpec((tk, AT_HD), lambda h, j, i: (j, v_col0 + h)), lspec, lspec, qspec],
        out_specs=[kspec, kspec],
        out_shape=[jax.ShapeDtypeStruct((S, AT_KW), F32), jax.ShapeDtypeStruct((S, AT_KW), F32)],
        scratch_shapes=[pltpu.VMEM((tk, AT_HD), F32), pltpu.VMEM((tk, AT_HD), F32)],
        compiler_params=_cp("parallel", "parallel", "arbitrary"),
    )(q, k, v, lse, delta, do)


def _t5_bucket_np(rel):
    half = REL_BUCKETS // 2
    exact = half // 2
    n = np.abs(rel)
    large = exact + (np.log(np.maximum(n, 1).astype(np.float32) / np.float32(exact))
                     / np.float32(math.log(REL_MAX_DIST / exact)) * np.float32(half - exact)).astype(np.int32)
    large = np.minimum(large, half - 1)
    return np.where(rel > 0, half, 0) + np.where(n < exact, n, large)


def _dl_tiles(Ls):
    T = min(128, Ls)
    return T, T + 2 * DL_HALF


def _dl_bucket_tables(dil, T):
    W = T + 2 * DL_HALF
    i = np.arange(T)[:, None]
    j = np.arange(W)[None, :]
    bq = _t5_bucket_np((j - DL_HALF - i) * dil)
    iw = np.arange(W)[:, None]
    jk = np.arange(T)[None, :]
    bk = _t5_bucket_np((jk + DL_HALF - iw) * dil)
    return bq.astype(np.int32), bk.astype(np.int32)


def band_fwd(q, kp, vp, bias, *, scale):
    H, dil, Ls, E = q.shape
    T, W = _dl_tiles(Ls)

    def body(q_ref, k_ref, v_ref, b_ref, o_ref, lse_ref):
        n = pl.program_id(2)
        r0 = pl.multiple_of(n * T, T)
        kw = k_ref[0, 0, pl.ds(r0, W), :]
        vw = v_ref[0, 0, pl.ds(r0, W), :]
        i = lax.broadcasted_iota(jnp.int32, (T, W), 0)
        j = lax.broadcasted_iota(jnp.int32, (T, W), 1)
        kpos = n * T + j - DL_HALF
        mask = (jnp.abs(j - DL_HALF - i) <= DL_HALF) & (kpos >= 0) & (kpos < Ls)
        s = _dot(q_ref[0, 0], kw, ((1,), (1,))) * scale + b_ref[0]
        s = jnp.where(mask, s, NEG_BIG)
        m = jnp.max(s, axis=1, keepdims=True)
        lse = m + jnp.log(jnp.sum(jnp.exp(s - m), axis=1, keepdims=True))
        p = jnp.exp(s - lse)
        o_ref[0, 0] = _dot(p, vw, ((1,), (0,)))
        lse_ref[0, 0] = lse

    return pl.pallas_call(
        body, name=f"band_fwd_{dil}", grid=(H, dil, Ls // T),
        in_specs=[pl.BlockSpec((1, 1, T, E), lambda h, d, n: (h, d, n, 0)),
                  pl.BlockSpec((1, 1, Ls + 2 * DL_HALF, E), lambda h, d, n: (h, d, 0, 0)),
                  pl.BlockSpec((1, 1, Ls + 2 * DL_HALF, E), lambda h, d, n: (h, d, 0, 0)),
                  pl.BlockSpec((1, T, W), lambda h, d, n: (h, 0, 0))],
        out_specs=[pl.BlockSpec((1, 1, T, E), lambda h, d, n: (h, d, n, 0)),
                   pl.BlockSpec((1, 1, T, 1), lambda h, d, n: (h, d, n, 0))],
        out_shape=[jax.ShapeDtypeStruct((H, dil, Ls, E), F32), jax.ShapeDtypeStruct((H, dil, Ls, 1), F32)],
        compiler_params=_cp("parallel", "parallel", "arbitrary"),
    )(q, kp, vp, bias)


def band_bwd_dq(q, kp, vp, bias, lse, dm, do, *, scale):
    H, dil, Ls, E = q.shape
    T, W = _dl_tiles(Ls)

    def body(q_ref, k_ref, v_ref, b_ref, lse_ref, dm_ref, do_ref, dq_ref, db_ref):
        d, n = pl.program_id(1), pl.program_id(2)
        r0 = pl.multiple_of(n * T, T)
        kw = k_ref[0, 0, pl.ds(r0, W), :]
        vw = v_ref[0, 0, pl.ds(r0, W), :]
        i = lax.broadcasted_iota(jnp.int32, (T, W), 0)
        j = lax.broadcasted_iota(jnp.int32, (T, W), 1)
        kpos = n * T + j - DL_HALF
        mask = (jnp.abs(j - DL_HALF - i) <= DL_HALF) & (kpos >= 0) & (kpos < Ls)
        s = _dot(q_ref[0, 0], kw, ((1,), (1,))) * scale + b_ref[0]
        p = jnp.where(mask, jnp.exp(jnp.where(mask, s, 0.0) - lse_ref[0, 0]), 0.0)
        dp = _dot(do_ref[0, 0], vw, ((1,), (1,)))
        ds = p * (dp - dm_ref[0, 0])
        dq_ref[0, 0] = _dot(ds, kw, ((1,), (0,))) * scale
        first = jnp.logical_and(d == 0, n == 0)

        @pl.when(first)
        def _():
            db_ref[0] = ds

        @pl.when(jnp.logical_not(first))
        def _():
            db_ref[0] += ds

    qspec = pl.BlockSpec((1, 1, T, E), lambda h, d, n: (h, d, n, 0))
    kspec = pl.BlockSpec((1, 1, Ls + 2 * DL_HALF, E), lambda h, d, n: (h, d, 0, 0))
    rspec = pl.BlockSpec((1, 1, T, 1), lambda h, d, n: (h, d, n, 0))
    bspec = pl.BlockSpec((1, T, W), lambda h, d, n: (h, 0, 0))
    return pl.pallas_call(
        body, name=f"band_bwd_dq_{dil}", grid=(H, dil, Ls // T),
        in_specs=[qspec, kspec, kspec, bspec, rspec, rspec, qspec],
        out_specs=[qspec, bspec],
        out_shape=[jax.ShapeDtypeStruct((H, dil, Ls, E), F32), jax.ShapeDtypeStruct((H, T, W), F32)],
        compiler_params=_cp("arbitrary", "arbitrary", "arbitrary"),
    )(q, kp, vp, bias, lse, dm, do)


def band_bwd_dkv(qp, k, v, bias_t, lsep, dmp, dop, *, scale):
    H, dil, Ls, E = k.shape
    T, W = _dl_tiles(Ls)

    def body(q_ref, k_ref, v_ref, b_ref, lse_ref, dm_ref, do_ref, dk_ref, dv_ref):
        n = pl.program_id(2)
        r0 = pl.multiple_of(n * T, T)
        qw = q_ref[0, 0, pl.ds(r0, W), :]
        dow = do_ref[0, 0, pl.ds(r0, W), :]
        lsew = lse_ref[0, 0, pl.ds(r0, W), :]
        dmw = dm_ref[0, 0, pl.ds(r0, W), :]
        iw = lax.broadcasted_iota(jnp.int32, (W, T), 0)
        j = lax.broadcasted_iota(jnp.int32, (W, T), 1)
        qpos = n * T + iw - DL_HALF
        mask = (jnp.abs(j + DL_HALF - iw) <= DL_HALF) & (qpos >= 0) & (qpos < Ls)
        s = _dot(qw, k_ref[0, 0], ((1,), (1,))) * scale + b_ref[0]
        p = jnp.where(mask, jnp.exp(jnp.where(mask, s, 0.0) - lsew), 0.0)
        dv_ref[0, 0] = _dot(p, dow, ((0,), (0,)))
        dp = _dot(dow, v_ref[0, 0], ((1,), (1,)))
        ds = p * (dp - dmw)
        dk_ref[0, 0] = _dot(ds, qw, ((0,), (0,))) * scale

    kspec = pl.BlockSpec((1, 1, T, E), lambda h, d, n: (h, d, n, 0))
    wspec = pl.BlockSpec((1, 1, Ls + 2 * DL_HALF, E), lambda h, d, n: (h, d, 0, 0))
    w1spec = pl.BlockSpec((1, 1, Ls + 2 * DL_HALF, 1), lambda h, d, n: (h, d, 0, 0))
    return pl.pallas_call(
        body, name=f"band_bwd_dkv_{dil}", grid=(H, dil, Ls // T),
        in_specs=[wspec, kspec, kspec, pl.BlockSpec((1, W, T), lambda h, d, n: (h, 0, 0)), w1spec, w1spec, wspec],
        out_specs=[kspec, kspec],
        out_shape=[jax.ShapeDtypeStruct((H, dil, Ls, E), F32), jax.ShapeDtypeStruct((H, dil, Ls, E), F32)],
        compiler_params=_cp("parallel", "parallel", "arbitrary"),
    )(qp, k, v, bias_t, lsep, dmp, dop)


def _dl_merge_fn(o0, o1, o2, l0, l1, l2):
    m = jnp.maximum(jnp.maximum(l0, l1), l2)
    e0, e1, e2 = jnp.exp(l0 - m), jnp.exp(l1 - m), jnp.exp(l2 - m)
    den = e0 + e1 + e2
    return ((e0 / den) * o0 + (e1 / den) * o1 + (e2 / den) * o2,)


def _adamw_math(w, g, m, v):
    m = ADAM_B1 * m + (1.0 - ADAM_B1) * g
    v = ADAM_B2 * v + (1.0 - ADAM_B2) * (g * g)
    m_hat = m / (1.0 - ADAM_B1 ** ADAM_STEP)
    v_hat = v / (1.0 - ADAM_B2 ** ADAM_STEP)
    delta = -ADAM_LR * (m_hat / (jnp.sqrt(v_hat) + ADAM_EPS) + ADAM_WD * w)
    return delta, m, v


def adamw_sum(parts, w, m, v, *, name, R=128):
    rows, cols = w.shape
    R = min(R, rows)
    if rows % R:
        R = rows

    def body(p_ref, w_ref, m_ref, v_ref, g_ref, d_ref, nm_ref, nv_ref):
        g = p_ref[0].astype(F32)
        for s in range(1, N_DEV):
            g = g + p_ref[s].astype(F32)
        d, nm, nv = _adamw_math(w_ref[...], g, m_ref[...], v_ref[...])
        g_ref[...] = g
        d_ref[...] = d
        nm_ref[...] = nm
        nv_ref[...] = nv

    spec = pl.BlockSpec((R, cols), lambda i: (i, 0))
    return pl.pallas_call(
        body, name=name, grid=(rows // R,),
        in_specs=[pl.BlockSpec((N_DEV, R, cols), lambda i: (0, i, 0)), spec, spec, spec],
        out_specs=[spec] * 4, out_shape=[jax.ShapeDtypeStruct((rows, cols), F32)] * 4,
        compiler_params=_cp("parallel"),
    )(parts, w, m, v)


def sum_parts(parts, *, name):
    rows, cols = parts.shape[1:]

    def body(p_ref, o_ref):
        g = p_ref[0]
        for s in range(1, N_DEV):
            g = g + p_ref[s]
        o_ref[...] = g

    return pl.pallas_call(body, name=name, out_shape=jax.ShapeDtypeStruct((rows, cols), F32))(parts)


def adamw_plain(w, g, m, v, *, name):
    def body(w_ref, g_ref, m_ref, v_ref, d_ref, nm_ref, nv_ref):
        d, nm, nv = _adamw_math(w_ref[...], g_ref[...], m_ref[...], v_ref[...])
        d_ref[...] = d
        nm_ref[...] = nm
        nv_ref[...] = nv

    return pl.pallas_call(body, name=name, out_shape=[jax.ShapeDtypeStruct(w.shape, F32)] * 3)(w, g, m, v)


def _my_pos():
    return lax.axis_index("x"), lax.axis_index("y"), lax.axis_index("c")


def _flat(px, py, pc):
    return 4 * px + 2 * py + pc


def allgather_two_level(x, *, name):
    R, C = x.shape

    def body(x_ref, out_ref, send_sems, recv_sems, local_sem):
        x_, y_, c_ = _my_pos()
        me, sibling = (x_, y_, c_), (x_, y_, 1 - c_)
        chips = [(1 - x_, y_), (x_, 1 - y_), (1 - x_, 1 - y_)]

        def rows(p):
            return out_ref.at[_flat(*p)]

        def copy(k, block, to, src=None):
            return pltpu.make_async_remote_copy(
                src_ref=rows(block) if src is None else src, dst_ref=rows(block),
                send_sem=send_sems.at[k], recv_sem=recv_sems.at[k], device_id=to, device_id_type=MESH_ID)

        mine = pltpu.make_async_copy(x_ref, rows(me), local_sem)
        mine.start()
        first = [copy(0, me, sibling, src=x_ref)]
        first += [copy(1 + j, me, (*chip, c_), src=x_ref) for j, chip in enumerate(chips)]
        for cp in first:
            cp.start()
        passed = [copy(4 + j, (*chip, c_), sibling) for j, chip in enumerate(chips)]
        for j, chip in enumerate(chips):
            copy(1 + j, (*chip, c_), me).wait_recv()
            passed[j].start()
        copy(0, sibling, me).wait_recv()
        for j, chip in enumerate(chips):
            copy(4 + j, (*chip, 1 - c_), me).wait_recv()
        for cp in first + passed:
            cp.wait_send()
        mine.wait()

    return pl.pallas_call(
        body, name=name,
        out_shape=jax.ShapeDtypeStruct((N_DEV, R, C), x.dtype),
        in_specs=[pl.BlockSpec(memory_space=pl.ANY)],
        out_specs=pl.BlockSpec(memory_space=pl.ANY),
        scratch_shapes=[pltpu.SemaphoreType.DMA((7,)), pltpu.SemaphoreType.DMA((7,)), pltpu.SemaphoreType.DMA],
    )(x)


def all_to_all(bufs, *, name):
    nb = len(bufs)

    def body(*refs):
        in_refs = refs[:nb]
        out_refs = refs[nb:2 * nb]
        send_sems, recv_sems, local_sems = refs[2 * nb:]
        x_, y_, c_ = _my_pos()
        me = _flat(x_, y_, c_)
        peers = []
        for k in range(1, N_DEV):
            fx, fy, fc = (k >> 2) & 1, (k >> 1) & 1, k & 1
            peers.append(((1 - x_) if fx else x_, (1 - y_) if fy else y_, (1 - c_) if fc else c_))
        copies = []
        for b in range(nb):
            loc = pltpu.make_async_copy(in_refs[b].at[me], out_refs[b].at[me], local_sems.at[b])
            loc.start()
            copies.append(loc)
        remote = []
        for b in range(nb):
            for k, p in enumerate(peers):
                cp = pltpu.make_async_remote_copy(
                    src_ref=in_refs[b].at[_flat(*p)], dst_ref=out_refs[b].at[me],
                    send_sem=send_sems.at[b, k], recv_sem=recv_sems.at[b, k], device_id=p, device_id_type=MESH_ID)
                cp.start()
                remote.append((b, k, p))
        for b, k, p in remote:
            pltpu.make_async_remote_copy(
                src_ref=in_refs[b].at[me], dst_ref=out_refs[b].at[_flat(*p)],
                send_sem=send_sems.at[b, k], recv_sem=recv_sems.at[b, k], device_id=p, device_id_type=MESH_ID).wait()
        for loc in copies:
            loc.wait()

    return pl.pallas_call(
        body, name=name,
        out_shape=[jax.ShapeDtypeStruct(b.shape, b.dtype) for b in bufs],
        in_specs=[pl.BlockSpec(memory_space=pl.ANY)] * nb,
        out_specs=[pl.BlockSpec(memory_space=pl.ANY)] * nb,
        scratch_shapes=[pltpu.SemaphoreType.DMA((nb, 7)), pltpu.SemaphoreType.DMA((nb, 7)), pltpu.SemaphoreType.DMA((nb,))],
    )(*bufs)


def _prenorm(tag, x, ng):
    return rowwise_fwd(f"{tag}_prenorm", _prenorm_fn, [(x, 0, False)], [(ng, False)], [(D_MODEL, MXU_DTYPE)], W=D_MODEL)[0]


def _in_out_bwd(tag, x, ng, hn, du, w_in, dx):
    dhn = matmul(du, w_in, tb=True, name=f"{tag}_dhn")
    dw_in = matmul(hn, du, ta=True, out_dtype=GRAD_WIRE_DTYPE, name=f"{tag}_dw_in")
    dx_prev, dng = rowwise_bwd(f"{tag}_prenorm_bwd", _prenorm_fn, [(x, 0, False)], [(ng, False)], [dhn],
                               W=D_MODEL, diff_rows=[0], diff_shared=[0], add=dx)
    return dx_prev, dng, dw_in


def ssd_layer_fwd(x, ng, p):
    hn = _prenorm("ssd", x, ng)
    u = matmul(hn, p["w_in"], name="ssd_in")
    act = ssd_conv_fwd(u, p["conv_w"], p["conv_b"])
    dt = rowwise_fwd("ssd_dt", _dt_fn, [(u, (SSD_DI + SSD_CONV_CH) // 128, False)], [(p["dt_bias"], False)],
                     [(128, F32)], W=128)[0]
    y0, st0 = ssd_scan_fwd(act, dt, p["alog"], reverse=False)
    y, st1 = ssd_scan_fwd(act, dt, p["alog"], reverse=True, y_prev=y0)
    g = rowwise_fwd("ssd_post", _ssd_post_fn, [(y, 0, True), (act, 0, True), (u, 0, True)],
                    [(p["dexp"], True), (p["norm_g"], True)], [(512, MXU_DTYPE)], W=512, ncb=SSD_GROUPS)[0]
    xn = matmul(g, p["w_out"], residual=x, name="ssd_out")
    return xn, dict(x=x, ng=ng, hn=hn, u=u, act=act, dt=dt, y=y, st0=st0, st1=st1, g=g)


def ssd_layer_bwd(sv, p, dx):
    u, act, dt = sv["u"], sv["act"], sv["dt"]
    S = u.shape[0]
    dg = matmul(dx, p["w_out"], tb=True, name="ssd_dg")
    dw_out = matmul(sv["g"], dx, ta=True, out_dtype=GRAD_WIRE_DTYPE, name="ssd_dw_out")
    dy, dxs_skip, dz, ddexp, dnorm = rowwise_bwd(
        "ssd_post_bwd", _ssd_post_fn, [(sv["y"], 0, True), (act, 0, True), (u, 0, True)],
        [(p["dexp"], True), (p["norm_g"], True)], [dg], W=512, ncb=SSD_GROUPS, diff_rows=[0, 1, 2], diff_shared=[0, 1])
    dxa, dB, dC, ddt, dal = ssd_scan_bwd(act, dt, p["alog"], sv["st0"], dy, dxs_skip, reverse=False)
    dxa, dB, dC, ddt, dal = ssd_scan_bwd(act, dt, p["alog"], sv["st1"], dy, dxa, reverse=True, prev=(dB, dC, ddt, dal))
    dact = jnp.concatenate([dxa, dB, dC], axis=1)
    dxbc, dconv_w, dconv_b = ssd_conv_bwd(u, p["conv_w"], p["conv_b"], dact)
    ddt_all = ddt[0] + ddt[1] + ddt[2] + ddt[3]
    ddt_raw, ddt_bias = rowwise_bwd("ssd_dt_bwd", _dt_fn, [(u, (SSD_DI + SSD_CONV_CH) // 128, False)],
                                    [(p["dt_bias"], False)], [ddt_all], W=128, diff_rows=[0], diff_shared=[0])
    du = jnp.concatenate([dz, dxbc, ddt_raw, jnp.zeros((S, SSD_IN_PAD - SSD_IN - 64), F32)], axis=1)
    dx_prev, dng, dw_in = _in_out_bwd("ssd", sv["x"], sv["ng"], sv["hn"], du, p["w_in"], dx)
    grads = dict(
        w_in=dw_in[:, :SSD_IN], w_out=dw_out, conv_w=dconv_w[:SSD_CONV], conv_b=dconv_b,
        dt_bias=ddt_bias[:, :2 * SSD_HEADS], a_log=(dal[0] + dal[1] + dal[2] + dal[3])[:, :2 * SSD_HEADS],
        d=ddexp.reshape(SSD_HEADS, SSD_HEADDIM).sum(axis=1)[None, :], norm_g=dnorm, ng=dng)
    return dx_prev, grads


def hg_layer_fwd(x, ng, p):
    hn = _prenorm("hg", x, ng)
    u = matmul(hn, p["w_in"], name="hg_in")
    lb = hg_lb_fwd(p["hgrn_lb"])
    o0, st0 = hg_scan_fwd(u, lb, reverse=False)
    o, st1 = hg_scan_fwd(u, lb, reverse=True, o_prev=o0)
    g = rowwise_fwd("hg_post", _hg_post_fn, [(o, 0, True), (u, 4 * HG_HEADS, True)], [(p["norm_g"], True)],
                    [(HG_D, MXU_DTYPE)], W=HG_D, ncb=HG_HEADS)[0]
    xn = matmul(g, p["w_out"], residual=x, name="hg_out")
    return xn, dict(x=x, ng=ng, hn=hn, u=u, lb=lb, o=o, st0=st0, st1=st1, g=g)


def hg_layer_bwd(sv, p, dx):
    u, lb = sv["u"], sv["lb"]
    dg = matmul(dx, p["w_out"], tb=True, name="hg_dg")
    dw_out = matmul(sv["g"], dx, ta=True, out_dtype=GRAD_WIRE_DTYPE, name="hg_dw_out")
    do, dgate, dnorm = rowwise_bwd("hg_post_bwd", _hg_post_fn, [(sv["o"], 0, True), (u, 4 * HG_HEADS, True)],
                                   [(p["norm_g"], True)], [dg], W=HG_D, ncb=HG_HEADS, diff_rows=[0, 1], diff_shared=[0])
    dq0, df0, dv0, dlb0 = hg_scan_bwd(u, lb, sv["st0"], do, reverse=False)
    dq, df1, dv, dlb = hg_scan_bwd(u, lb, sv["st1"], do, reverse=True, prev=(dq0, dv0, dlb0))
    du = jnp.concatenate([dq, df0, df1, dv, dgate], axis=1)
    dhgrn_lb = hg_lb_bwd(p["hgrn_lb"], dlb)
    dx_prev, dng, dw_in = _in_out_bwd("hg", sv["x"], sv["ng"], sv["hn"], du, p["w_in"], dx)
    return dx_prev, dict(w_in=dw_in, w_out=dw_out, norm_g=dnorm, hgrn_lb=dhgrn_lb, ng=dng)


def _rope_consts(S):
    ar, ac = _rope_tables(S)
    ct = np.concatenate([np.cos(ar), np.cos(ar), np.cos(ac), np.cos(ac)], axis=1).astype(np.float32)
    st = np.concatenate([-np.sin(ar), np.sin(ar), -np.sin(ac), np.sin(ac)], axis=1).astype(np.float32)
    return jnp.asarray(ct), jnp.asarray(st), jnp.asarray(_rope_swap_matrix())


def _at_qk(tag, u, col0, nheads, scale, gain, consts, cot=None):
    ct, st, perm = consts
    rows = [(u, col0, True), (ct, 0, False), (st, 0, False)]
    shared = [(gain, False), (perm, False)]
    if cot is None:
        return rowwise_fwd(f"at_{tag}", _make_qk_fn(scale), rows, shared, [(AT_HD, MXU_DTYPE)], W=AT_HD, ncb=nheads)[0]
    return rowwise_bwd(f"at_{tag}_bwd", _make_qk_fn(scale), rows, shared, [cot], W=AT_HD, ncb=nheads,
                       diff_rows=[0], diff_shared=[0])


def at_layer_fwd(x, ng, p):
    S = x.shape[0]
    hn = _prenorm("at", x, ng)
    u = matmul(hn, p["w_in"], name="at_in")
    consts = _rope_consts(S)
    qr = _at_qk("q", u, 0, AT_HEADS, AT_HD ** -0.5, p["q_g"], consts)
    kr = _at_qk("k", u, AT_HEADS, AT_KV, 1.0, p["k_g"], consts)
    vc0 = (AT_QW + AT_KW) // AT_HD
    o, lse = flash_fwd(qr, kr, u, v_col0=vc0)
    g = rowwise_fwd("at_gate", _gate_fn, [(o, 0, True), (u, (AT_QW + 2 * AT_KW) // 1024, True)], [],
                    [(1024, MXU_DTYPE)], W=1024, ncb=AT_QW // 1024)[0]
    xn = matmul(g, p["w_out"], residual=x, name="at_out")
    return xn, dict(x=x, ng=ng, hn=hn, u=u, qr=qr, kr=kr, o=o, lse=lse, g=g)


def at_layer_bwd(sv, p, dx):
    u, qr, kr = sv["u"], sv["qr"], sv["kr"]
    S = u.shape[0]
    consts = _rope_consts(S)
    vc0 = (AT_QW + AT_KW) // AT_HD
    dg = matmul(dx, p["w_out"], tb=True, name="at_dg")
    dw_out = matmul(sv["g"], dx, ta=True, out_dtype=GRAD_WIRE_DTYPE, name="at_dw_out")
    do, dgate = rowwise_bwd("at_gate_bwd", _gate_fn, [(sv["o"], 0, True), (u, (AT_QW + 2 * AT_KW) // 1024, True)], [],
                            [dg], W=1024, ncb=AT_QW // 1024, diff_rows=[0, 1], diff_shared=[])
    dqs, delta = flash_bwd_dq(qr, kr, u, sv["o"], sv["lse"], do, v_col0=vc0)
    dkr, dv = flash_bwd_dkv(qr, kr, u, sv["lse"], delta, do, v_col0=vc0)
    dq_raw, dqg = _at_qk("q", u, 0, AT_HEADS, AT_HD ** -0.5, p["q_g"], consts, cot=dqs)
    dk_raw, dkg = _at_qk("k", u, AT_HEADS, AT_KV, 1.0, p["k_g"], consts, cot=dkr)
    du = jnp.concatenate([dq_raw, dk_raw, dv, dgate], axis=1)
    dx_prev, dng, dw_in = _in_out_bwd("at", sv["x"], sv["ng"], sv["hn"], du, p["w_in"], dx)
    return dx_prev, dict(w_in=dw_in, w_out=dw_out, q_g=dqg, k_g=dkg, ng=dng)


def _to_stream(t, dil):
    S = t.shape[0]
    return t.reshape(S // dil, dil, DL_HEADS, DL_HD).transpose(2, 1, 0, 3)


def _from_stream(t):
    H, dil, Ls, E = t.shape
    return t.transpose(2, 1, 0, 3).reshape(Ls * dil, H * E)


def _stream_to_hm(t):
    H, dil, Ls, w = t.shape
    return t.transpose(0, 2, 1, 3).reshape(H * Ls * dil, w)


def _hm_to_stream(t, dil):
    w = t.shape[1]
    S = t.shape[0] // DL_HEADS
    return t.reshape(DL_HEADS, S // dil, dil, w).transpose(0, 2, 1, 3)


def _pad_l(t):
    return jnp.pad(t, ((0, 0), (0, 0), (DL_HALF, DL_HALF), (0, 0)))


OX_LSE = DL_HD
DOX_LSE, DOX_DM = DL_HD, DL_HD + 32


def _win(p_ref, c_ref, n_ref, h, T):
    return jnp.concatenate([p_ref[h, 0, T - DL_HALF:T, :], c_ref[h, 0], n_ref[h, 0, 0:DL_HALF, :]], axis=0)


def _win_specs(T, E, nb):
    return [pl.BlockSpec((DL_HEADS, 1, T, E), lambda d, n: (0, d, jnp.maximum(n - 1, 0), 0)),
            pl.BlockSpec((DL_HEADS, 1, T, E), lambda d, n: (0, d, n, 0)),
            pl.BlockSpec((DL_HEADS, 1, T, E), lambda d, n: (0, d, jnp.minimum(n + 1, nb - 1), 0))]


def _band_mask_q(n, T, W, Ls):
    i = lax.broadcasted_iota(jnp.int32, (T, W), 0)
    j = lax.broadcasted_iota(jnp.int32, (T, W), 1)
    kpos = n * T + j - DL_HALF
    return (jnp.abs(j - DL_HALF - i) <= DL_HALF) & (kpos >= 0) & (kpos < Ls)


def band_fwd(q, k, v, bias, *, scale):
    H, dil, Ls, E = q.shape
    T, W = _dl_tiles(Ls)
    nb = Ls // T

    def body(q_ref, kp_ref, kc_ref, kn_ref, vp_ref, vc_ref, vn_ref, b_ref, ox_ref):
        n = pl.program_id(1)
        mask = _band_mask_q(n, T, W, Ls)
        for h in range(H):
            kw = _win(kp_ref, kc_ref, kn_ref, h, T)
            vw = _win(vp_ref, vc_ref, vn_ref, h, T)
            s = _dot(q_ref[h, 0], kw, ((1,), (1,))) * scale + b_ref[h]
            s = jnp.where(mask, s, NEG_BIG)
            m = jnp.max(s, axis=1, keepdims=True)
            lse = m + jnp.log(jnp.sum(jnp.exp(s - m), axis=1, keepdims=True))
            p = jnp.exp(s - lse)
            ox_ref[h, 0, :, 0:E] = _dot(p, vw, ((1,), (0,)))
            ox_ref[h, 0, :, E:2 * E] = lse + jnp.zeros((T, E), F32)

    cur = pl.BlockSpec((H, 1, T, E), lambda d, n: (0, d, n, 0))
    return pl.pallas_call(
        body, name=f"band_fwd_{dil}", grid=(dil, nb),
        in_specs=[cur] + _win_specs(T, E, nb) + _win_specs(T, E, nb) + [pl.BlockSpec((H, T, W), lambda d, n: (0, 0, 0))],
        out_specs=pl.BlockSpec((H, 1, T, 2 * E), lambda d, n: (0, d, n, 0)),
        out_shape=jax.ShapeDtypeStruct((H, dil, Ls, 2 * E), F32),
        compiler_params=_cp("parallel", "parallel"),
    )(q, k, k, k, v, v, v, bias)


def band_bwd_dq(q, k, v, bias, dox, *, scale):
    H, dil, Ls, E = q.shape
    T, W = _dl_tiles(Ls)
    nb = Ls // T

    def body(q_ref, kp_ref, kc_ref, kn_ref, vp_ref, vc_ref, vn_ref, b_ref, dox_ref, dq_ref, db_ref):
        d, n = pl.program_id(0), pl.program_id(1)
        mask = _band_mask_q(n, T, W, Ls)
        first = jnp.logical_and(d == 0, n == 0)

        @pl.when(first)
        def _():
            db_ref[...] = jnp.zeros_like(db_ref)

        for h in range(H):
            kw = _win(kp_ref, kc_ref, kn_ref, h, T)
            vw = _win(vp_ref, vc_ref, vn_ref, h, T)
            dox = dox_ref[h, 0]
            do, lse, dm = dox[:, 0:E], dox[:, DOX_LSE:DOX_LSE + 1], dox[:, DOX_DM:DOX_DM + 1]
            s = _dot(q_ref[h, 0], kw, ((1,), (1,))) * scale + b_ref[h]
            p = jnp.where(mask, jnp.exp(jnp.where(mask, s, 0.0) - lse), 0.0)
            dp = _dot(do, vw, ((1,), (1,)))
            ds = p * (dp - dm)
            dq_ref[h, 0] = _dot(ds, kw, ((1,), (0,))) * scale
            db_ref[h] += ds

    cur = pl.BlockSpec((H, 1, T, E), lambda d, n: (0, d, n, 0))
    bspec = pl.BlockSpec((H, T, W), lambda d, n: (0, 0, 0))
    return pl.pallas_call(
        body, name=f"band_bwd_dq_{dil}", grid=(dil, nb),
        in_specs=[cur] + _win_specs(T, E, nb) + _win_specs(T, E, nb) + [bspec,
                  pl.BlockSpec((H, 1, T, 2 * E), lambda d, n: (0, d, n, 0))],
        out_specs=[cur, bspec],
        out_shape=[jax.ShapeDtypeStruct((H, dil, Ls, E), F32), jax.ShapeDtypeStruct((H, T, W), F32)],
        compiler_params=_cp("arbitrary", "arbitrary"),
    )(q, k, k, k, v, v, v, bias, dox)


def band_bwd_dkv(q, k, v, bias_t, dox, *, scale):
    H, dil, Ls, E = k.shape
    T, W = _dl_tiles(Ls)
    nb = Ls // T

    def body(qp_ref, qc_ref, qn_ref, k_ref, v_ref, b_ref, dp_ref, dc_ref, dn_ref, dk_ref, dv_ref):
        n = pl.program_id(1)
        iw = lax.broadcasted_iota(jnp.int32, (W, T), 0)
        j = lax.broadcasted_iota(jnp.int32, (W, T), 1)
        qpos = n * T + iw - DL_HALF
        mask = (jnp.abs(j + DL_HALF - iw) <= DL_HALF) & (qpos >= 0) & (qpos < Ls)
        for h in range(H):
            qw = _win(qp_ref, qc_ref, qn_ref, h, T)
            doxw = _win(dp_ref, dc_ref, dn_ref, h, T)
            dow, lsew, dmw = doxw[:, 0:E], doxw[:, DOX_LSE:DOX_LSE + 1], doxw[:, DOX_DM:DOX_DM + 1]
            s = _dot(qw, k_ref[h, 0], ((1,), (1,))) * scale + b_ref[h]
            p = jnp.where(mask, jnp.exp(jnp.where(mask, s, 0.0) - lsew), 0.0)
            dv_ref[h, 0] = _dot(p, dow, ((0,), (0,)))
            dp = _dot(dow, v_ref[h, 0], ((1,), (1,)))
            ds = p * (dp - dmw)
            dk_ref[h, 0] = _dot(ds, qw, ((0,), (0,))) * scale

    cur = pl.BlockSpec((H, 1, T, E), lambda d, n: (0, d, n, 0))
    return pl.pallas_call(
        body, name=f"band_bwd_dkv_{dil}", grid=(dil, nb),
        in_specs=_win_specs(T, E, nb) + [cur, cur, pl.BlockSpec((H, W, T), lambda d, n: (0, 0, 0))]
        + _win_specs(T, 2 * E, nb),
        out_specs=[cur, cur],
        out_shape=[jax.ShapeDtypeStruct((H, dil, Ls, E), F32), jax.ShapeDtypeStruct((H, dil, Ls, E), F32)],
        compiler_params=_cp("parallel", "parallel"),
    )(q, q, q, k, v, bias_t, dox, dox, dox)


def dl_merge_fwd(oxs, *, R=1024):
    rows = oxs[0].shape[0]
    R = min(R, rows)
    E = DL_HD

    def body(a_ref, b_ref, c_ref, o_ref):
        vals = [r[...] for r in (a_ref, b_ref, c_ref)]
        o_ref[...] = _dl_merge_fn(*[t[:, 0:E] for t in vals], *[t[:, OX_LSE:OX_LSE + 1] for t in vals])[0]

    spec = pl.BlockSpec((R, 2 * E), lambda i: (i, 0))
    return pl.pallas_call(
        body, name="dl_merge", grid=(rows // R,), in_specs=[spec] * 3,
        out_specs=pl.BlockSpec((R, E), lambda i: (i, 0)), out_shape=jax.ShapeDtypeStruct((rows, E), F32),
        compiler_params=_cp("parallel"),
    )(*oxs)


def dl_merge_bwd(oxs, do, *, R=1024):
    rows = oxs[0].shape[0]
    R = min(R, rows)
    E = DL_HD

    def body(a_ref, b_ref, c_ref, do_ref, da_ref, db_ref, dc_ref):
        vals = [r[...] for r in (a_ref, b_ref, c_ref)]
        os_ = [t[:, 0:E] for t in vals]
        ls_ = [t[:, OX_LSE:OX_LSE + 1] for t in vals]
        _, vjp = jax.vjp(_dl_merge_fn, *os_, *ls_)
        g = vjp((do_ref[...],))
        for k, d_ref in enumerate((da_ref, db_ref, dc_ref)):
            dm = jnp.sum(g[k] * os_[k], axis=1, keepdims=True) - g[3 + k]
            d_ref[:, 0:E] = g[k]
            d_ref[:, DOX_LSE:DOX_DM] = ls_[k] + jnp.zeros((R, DOX_DM - DOX_LSE), F32)
            d_ref[:, DOX_DM:2 * E] = dm + jnp.zeros((R, 2 * E - DOX_DM), F32)

    spec = pl.BlockSpec((R, 2 * E), lambda i: (i, 0))
    return pl.pallas_call(
        body, name="dl_merge_bwd", grid=(rows // R,), in_specs=[spec] * 3 + [pl.BlockSpec((R, E), lambda i: (i, 0))],
        out_specs=[spec] * 3, out_shape=[jax.ShapeDtypeStruct((rows, 2 * E), F32)] * 3,
        compiler_params=_cp("parallel"),
    )(*oxs, do)


def _dl_bias_tables(rel_bias, dil, T):
    W = T + 2 * DL_HALF
    bq, bk = _dl_bucket_tables(dil, T)
    idx = np.concatenate([bq.reshape(-1), bk.reshape(-1)])
    onehot_t = (np.arange(REL_BUCKETS)[:, None] == idx[None, :]).astype(np.float32)
    tab = matmul(rel_bias.T, jnp.asarray(onehot_t), exact=True, name=f"dl_bias_{dil}", tm=DL_HEADS, tk=REL_BUCKETS,
                 tn=_tile(2 * T * W, (8192, 4096, 2048, 1024, 512, 256, 128)))
    return tab[:, :T * W].reshape(DL_HEADS, T, W), tab[:, T * W:].reshape(DL_HEADS, W, T), bq


def _dl_dm_fn(do, o, dl):
    return (jnp.sum(do * o, axis=-1, keepdims=True) - dl,)


def _old_dl_layer_fwd(x, ng, p):
    S = x.shape[0]
    hn = _prenorm("dl", x, ng)
    u = matmul(hn, p["w_in"], name="dl_in")
    scale = DL_HD ** -0.5
    per_group, o_hm, lse_hm = [], [], []
    for gi, (window, dil) in enumerate(DL_PAIRS):
        base = gi * 3 * DL_W
        Ls = S // dil
        T, _ = _dl_tiles(Ls)
        bq, bk = _dl_bucket_tables(dil, T)
        qs = _to_stream(u[:, base:base + DL_W], dil).astype(MXU_DTYPE)
        ks = _to_stream(u[:, base + DL_W:base + 2 * DL_W], dil).astype(MXU_DTYPE)
        vs = _to_stream(u[:, base + 2 * DL_W:base + 3 * DL_W], dil).astype(MXU_DTYPE)
        bias = p["rel_bias"][bq].transpose(2, 0, 1)
        o_s, lse_s = band_fwd(qs, _pad_l(ks), _pad_l(vs), bias, scale=scale)
        per_group.append(dict(qs=qs, ks=ks, vs=vs, lse_s=lse_s, bq=bq, bk=bk, dil=dil))
        o_hm.append(_stream_to_hm(o_s))
        lse_hm.append(_stream_to_hm(lse_s))
    rows = [(t, 0, False) for t in o_hm] + [(t, 0, False, 1) for t in lse_hm]
    om = rowwise_fwd("dl_merge", _dl_merge_fn, rows, [], [(DL_HD, F32)], W=DL_HD)[0]
    o = om.reshape(DL_HEADS, S, DL_HD).transpose(1, 0, 2).reshape(S, DL_W)
    g = rowwise_fwd("dl_gate", _gate_fn, [(o, 0, False), (u, 9, False)], [], [(DL_W, MXU_DTYPE)], W=DL_W)[0]
    xn = matmul(g, p["w_out"], residual=x, name="dl_out")
    return xn, dict(x=x, ng=ng, hn=hn, u=u, per_group=per_group, o_hm=o_hm, lse_hm=lse_hm, o=o, g=g)


def _old_dl_layer_bwd(sv, p, dx):
    u = sv["u"]
    S = u.shape[0]
    scale = DL_HD ** -0.5
    dg = matmul(dx, p["w_out"], tb=True, name="dl_dg")
    dw_out = matmul(sv["g"], dx, ta=True, out_dtype=GRAD_WIRE_DTYPE, name="dl_dw_out")
    do, dgate = rowwise_bwd("dl_gate_bwd", _gate_fn, [(sv["o"], 0, False), (u, 9, False)], [], [dg], W=DL_W,
                            diff_rows=[0, 1], diff_shared=[])
    do_hm = do.reshape(S, DL_HEADS, DL_HD).transpose(1, 0, 2).reshape(DL_HEADS * S, DL_HD)
    rows = [(t, 0, False) for t in sv["o_hm"]] + [(t, 0, False, 1) for t in sv["lse_hm"]]
    dmerge = rowwise_bwd("dl_merge_bwd", _dl_merge_fn, rows, [], [do_hm], W=DL_HD, diff_rows=[0, 1, 2, 3, 4, 5],
                         diff_shared=[])
    parts, dbs, onehots = [], [], []
    for gi, pg in enumerate(sv["per_group"]):
        dil = pg["dil"]
        Ls = S // dil
        T, W = _dl_tiles(Ls)
        dog, dlg = dmerge[gi], dmerge[3 + gi]
        dm = rowwise_fwd(f"dl_dm_{gi}", _dl_dm_fn, [(dog, 0, False), (sv["o_hm"][gi], 0, False), (dlg, 0, False, 1)], [],
                         [(1, F32)], W=DL_HD)[0]
        do_s, dm_s = _hm_to_stream(dog, dil), _hm_to_stream(dm, dil)
        bias = p["rel_bias"][pg["bq"]].transpose(2, 0, 1)
        bias_t = p["rel_bias"][pg["bk"]].transpose(2, 0, 1)
        kp, vp = _pad_l(pg["ks"]), _pad_l(pg["vs"])
        dq_s, dbias = band_bwd_dq(pg["qs"], kp, vp, bias, pg["lse_s"], dm_s, do_s, scale=scale)
        dk_s, dv_s = band_bwd_dkv(_pad_l(pg["qs"]), pg["ks"], pg["vs"], bias_t, _pad_l(pg["lse_s"]), _pad_l(dm_s),
                                  _pad_l(do_s), scale=scale)
        parts += [_from_stream(dq_s), _from_stream(dk_s), _from_stream(dv_s)]
        dbs.append(dbias.reshape(DL_HEADS, T * W))
        onehots.append((pg["bq"].reshape(-1)[:, None] == np.arange(REL_BUCKETS)[None, :]).astype(np.float32))
    drel = matmul(jnp.concatenate(dbs, axis=1), jnp.asarray(np.concatenate(onehots, axis=0)), exact=True,
                  name="dl_drel", tm=DL_HEADS, tn=REL_BUCKETS, tk=2048)
    du = jnp.concatenate(parts + [dgate], axis=1)
    dx_prev, dng, dw_in = _in_out_bwd("dl", sv["x"], sv["ng"], sv["hn"], du, p["w_in"], dx)
    return dx_prev, dict(w_in=dw_in, w_out=dw_out, rel_bias=drel.T, ng=dng)


def dl_layer_fwd(x, ng, p):
    S = x.shape[0]
    hn = _prenorm("dl", x, ng)
    u = matmul(hn, p["w_in"], name="dl_in")
    scale = DL_HD ** -0.5
    per_group, ox_hm = [], []
    for gi, (window, dil) in enumerate(DL_PAIRS):
        base = gi * 3 * DL_W
        T, _ = _dl_tiles(S // dil)
        qs, ks, vs = [_to_stream(u[:, base + c * DL_W:base + (c + 1) * DL_W], dil).astype(MXU_DTYPE) for c in range(3)]
        bias, bias_t, bq = _dl_bias_tables(p["rel_bias"], dil, T)
        ox_s = band_fwd(qs, ks, vs, bias, scale=scale)
        per_group.append(dict(qs=qs, ks=ks, vs=vs, bias=bias, bias_t=bias_t, bq=bq, dil=dil))
        ox_hm.append(_stream_to_hm(ox_s))
    om = dl_merge_fwd(ox_hm)
    o = om.reshape(DL_HEADS, S, DL_HD).transpose(1, 0, 2).reshape(S, DL_W)
    g = rowwise_fwd("dl_gate", _gate_fn, [(o, 0, False), (u, 9, False)], [], [(DL_W, MXU_DTYPE)], W=DL_W)[0]
    xn = matmul(g, p["w_out"], residual=x, name="dl_out")
    return xn, dict(x=x, ng=ng, hn=hn, u=u, per_group=per_group, ox_hm=ox_hm, o=o, g=g)


def dl_layer_bwd(sv, p, dx):
    u = sv["u"]
    S = u.shape[0]
    scale = DL_HD ** -0.5
    dg = matmul(dx, p["w_out"], tb=True, name="dl_dg")
    dw_out = matmul(sv["g"], dx, ta=True, out_dtype=GRAD_WIRE_DTYPE, name="dl_dw_out")
    do, dgate = rowwise_bwd("dl_gate_bwd", _gate_fn, [(sv["o"], 0, False), (u, 9, False)], [], [dg], W=DL_W,
                            diff_rows=[0, 1], diff_shared=[])
    do_hm = do.reshape(S, DL_HEADS, DL_HD).transpose(1, 0, 2).reshape(DL_HEADS * S, DL_HD)
    dox_hm = dl_merge_bwd(sv["ox_hm"], do_hm)
    parts, dbs, onehots = [], [], []
    for gi, pg in enumerate(sv["per_group"]):
        dil = pg["dil"]
        T, W = _dl_tiles(S // dil)
        dox_s = _hm_to_stream(dox_hm[gi], dil)
        dq_s, dbias = band_bwd_dq(pg["qs"], pg["ks"], pg["vs"], pg["bias"], dox_s, scale=scale)
        dk_s, dv_s = band_bwd_dkv(pg["qs"], pg["ks"], pg["vs"], pg["bias_t"], dox_s, scale=scale)
        parts += [_from_stream(dq_s), _from_stream(dk_s), _from_stream(dv_s)]
        dbs.append(dbias.reshape(DL_HEADS, T * W))
        onehots.append((pg["bq"].reshape(-1)[:, None] == np.arange(REL_BUCKETS)[None, :]).astype(np.float32))
    drel = matmul(jnp.concatenate(dbs, axis=1), jnp.asarray(np.concatenate(onehots, axis=0)), exact=True,
                  name="dl_drel", tm=DL_HEADS, tn=REL_BUCKETS, tk=2048)
    du = jnp.concatenate(parts + [dgate], axis=1)
    dx_prev, dng, dw_in = _in_out_bwd("dl", sv["x"], sv["ng"], sv["hn"], du, p["w_in"], dx)
    return dx_prev, dict(w_in=dw_in, w_out=dw_out, rel_bias=drel.T, ng=dng)


WEIGHT_ORDER = ['norm_g', 'final_g', 'rel_bias', 'hgrn_lb', 'ssd_w_in', 'ssd_conv_w', 'ssd_conv_b', 'ssd_dt_bias',
                'ssd_a_log', 'ssd_d', 'ssd_norm_g', 'ssd_w_out', 'hg_w_in', 'hg_norm_g', 'hg_w_out', 'at_w_in',
                'at_q_norm_g', 'at_k_norm_g', 'at_w_out', 'dl_w_in', 'dl_w_out']
BIG_IN = ['ssd_w_in', 'hg_w_in', 'at_w_in', 'dl_w_in']
BIG_OUT = ['ssd_w_out', 'hg_w_out', 'at_w_out', 'dl_w_out']
BIG = BIG_IN + BIG_OUT
SMALL = [n for n in WEIGHT_ORDER if n not in BIG]
LANES = 128


def _pack(arrs):
    flat = jnp.concatenate([a.reshape(-1).astype(F32) for a in arrs])
    n = flat.shape[0]
    rows = -(-n // (8 * LANES)) * 8
    return jnp.pad(flat, (0, rows * LANES - n)).reshape(rows, LANES)


def _unpack(buf, shapes):
    flat = buf.reshape(-1)
    out, off = [], 0
    for shp in shapes:
        n = int(np.prod(shp)) if len(shp) else 1
        out.append(flat[off:off + n].reshape(shp))
        off += n
    return out


def kernel(x, norm_g, final_g, rel_bias, hgrn_lb, ssd_w_in, ssd_conv_w, ssd_conv_b, ssd_dt_bias, ssd_a_log, ssd_d, ssd_norm_g, ssd_w_out, hg_w_in, hg_norm_g, hg_w_out, at_w_in, at_q_norm_g, at_k_norm_g, at_w_out, dl_w_in, dl_w_out, loss_target, m_norm_g, m_final_g, m_rel_bias, m_hgrn_lb, m_ssd_w_in, m_ssd_conv_w, m_ssd_conv_b, m_ssd_dt_bias, m_ssd_a_log, m_ssd_d, m_ssd_norm_g, m_ssd_w_out, m_hg_w_in, m_hg_norm_g, m_hg_w_out, m_at_w_in, m_at_q_norm_g, m_at_k_norm_g, m_at_w_out, m_dl_w_in, m_dl_w_out, v_norm_g, v_final_g, v_rel_bias, v_hgrn_lb, v_ssd_w_in, v_ssd_conv_w, v_ssd_conv_b, v_ssd_dt_bias, v_ssd_a_log, v_ssd_d, v_ssd_norm_g, v_ssd_w_out, v_hg_w_in, v_hg_norm_g, v_hg_w_out, v_at_w_in, v_at_q_norm_g, v_at_k_norm_g, v_at_w_out, v_dl_w_in, v_dl_w_out):
    w = dict(norm_g=norm_g, final_g=final_g, rel_bias=rel_bias, hgrn_lb=hgrn_lb, ssd_w_in=ssd_w_in, ssd_conv_w=ssd_conv_w, ssd_conv_b=ssd_conv_b, ssd_dt_bias=ssd_dt_bias, ssd_a_log=ssd_a_log, ssd_d=ssd_d, ssd_norm_g=ssd_norm_g, ssd_w_out=ssd_w_out, hg_w_in=hg_w_in, hg_norm_g=hg_norm_g, hg_w_out=hg_w_out, at_w_in=at_w_in, at_q_norm_g=at_q_norm_g, at_k_norm_g=at_k_norm_g, at_w_out=at_w_out, dl_w_in=dl_w_in, dl_w_out=dl_w_out)
    m = dict(norm_g=m_norm_g, final_g=m_final_g, rel_bias=m_rel_bias, hgrn_lb=m_hgrn_lb, ssd_w_in=m_ssd_w_in, ssd_conv_w=m_ssd_conv_w, ssd_conv_b=m_ssd_conv_b, ssd_dt_bias=m_ssd_dt_bias, ssd_a_log=m_ssd_a_log, ssd_d=m_ssd_d, ssd_norm_g=m_ssd_norm_g, ssd_w_out=m_ssd_w_out, hg_w_in=m_hg_w_in, hg_norm_g=m_hg_norm_g, hg_w_out=m_hg_w_out, at_w_in=m_at_w_in, at_q_norm_g=m_at_q_norm_g, at_k_norm_g=m_at_k_norm_g, at_w_out=m_at_w_out, dl_w_in=m_dl_w_in, dl_w_out=m_dl_w_out)
    v = dict(norm_g=v_norm_g, final_g=v_final_g, rel_bias=v_rel_bias, hgrn_lb=v_hgrn_lb, ssd_w_in=v_ssd_w_in, ssd_conv_w=v_ssd_conv_w, ssd_conv_b=v_ssd_conv_b, ssd_dt_bias=v_ssd_dt_bias, ssd_a_log=v_ssd_a_log, ssd_d=v_ssd_d, ssd_norm_g=v_ssd_norm_g, ssd_w_out=v_ssd_w_out, hg_w_in=v_hg_w_in, hg_norm_g=v_hg_norm_g, hg_w_out=v_hg_w_out, at_w_in=v_at_w_in, at_q_norm_g=v_at_q_norm_g, at_k_norm_g=v_at_k_norm_g, at_w_out=v_at_w_out, dl_w_in=v_dl_w_in, dl_w_out=v_dl_w_out)
    me = 4 * lax.axis_index("x") + 2 * lax.axis_index("y") + lax.axis_index("c")
    xs = x[0]
    S = xs.shape[0]

    shard2d = {n: w[n][0] for n in BIG}
    flat = jnp.concatenate([shard2d[n].astype(MXU_DTYPE).reshape(-1, D_MODEL) for n in BIG], axis=0)
    gathered = allgather_two_level(flat, name="allgather_weights")
    full, off = {}, 0
    for n in BIG:
        r, c = shard2d[n].shape
        nr = r * c // D_MODEL
        blk = gathered[:, off:off + nr].reshape(N_DEV, r, c)
        off += nr
        full[n] = blk.transpose(1, 0, 2).reshape(r, N_DEV * c) if n in BIG_IN else blk.reshape(N_DEV * r, c)
    ncw = ssd_conv_w.shape[2]
    nhg = hg_norm_g.shape[1]
    small_shard = jnp.zeros((8, 512), F32)
    small_shard = small_shard.at[:SSD_CONV, :ncw].set(ssd_conv_w[0]).at[SSD_CONV, :nhg].set(hg_norm_g[0])
    small_all = allgather_two_level(small_shard, name="allgather_small_weights")
    conv_w_full = small_all[:, :SSD_CONV, :ncw].transpose(1, 0, 2).reshape(SSD_CONV, N_DEV * ncw)
    hg_norm_full = small_all[:, SSD_CONV, :nhg].reshape(1, N_DEV * nhg)

    p_ssd = dict(w_in=jnp.pad(full["ssd_w_in"], ((0, 0), (0, SSD_IN_PAD - SSD_IN))), w_out=full["ssd_w_out"],
                 conv_w=conv_w_full, conv_b=ssd_conv_b,
                 dt_bias=jnp.pad(ssd_dt_bias.reshape(1, 2 * SSD_HEADS), ((0, 0), (0, 128 - 2 * SSD_HEADS))),
                 alog=jnp.pad(ssd_a_log.reshape(1, 2 * SSD_HEADS), ((0, 0), (0, 128 - 2 * SSD_HEADS))),
                 dexp=jnp.repeat(ssd_d.reshape(-1), SSD_HEADDIM)[None, :], norm_g=ssd_norm_g)
    p_hg = dict(w_in=full["hg_w_in"], w_out=full["hg_w_out"], norm_g=hg_norm_full, hgrn_lb=hgrn_lb)
    p_at = dict(w_in=full["at_w_in"], w_out=full["at_w_out"], q_g=at_q_norm_g, k_g=at_k_norm_g)
    p_dl = dict(w_in=full["dl_w_in"], w_out=full["dl_w_out"], rel_bias=rel_bias)

    x1, sv0 = ssd_layer_fwd(xs, norm_g[0:1], p_ssd)
    x2, sv1 = hg_layer_fwd(x1, norm_g[1:2], p_hg)
    x3, sv2 = at_layer_fwd(x2, norm_g[2:3], p_at)
    x4, sv3 = dl_layer_fwd(x3, norm_g[3:4], p_dl)
    loss_part, dx4, dfinal = loss_head(x4, final_g[None, :], loss_target[0])
    dx3, g3 = dl_layer_bwd(sv3, p_dl, dx4)
    dx2, g2 = at_layer_bwd(sv2, p_at, dx3)
    dx1, g1 = hg_layer_bwd(sv1, p_hg, dx2)
    dx0, g0 = ssd_layer_bwd(sv0, p_ssd, dx1)

    small_full = dict(
        norm_g=jnp.concatenate([g0["ng"], g1["ng"], g2["ng"], g3["ng"]], axis=0), final_g=dfinal[0],
        rel_bias=g3["rel_bias"], hgrn_lb=g1["hgrn_lb"], ssd_conv_w=g0["conv_w"][None], ssd_conv_b=g0["conv_b"],
        ssd_dt_bias=g0["dt_bias"].reshape(1, 2, SSD_HEADS), ssd_a_log=g0["a_log"].reshape(1, 2, SSD_HEADS),
        ssd_d=g0["d"], ssd_norm_g=g0["norm_g"], hg_norm_g=g1["norm_g"], at_q_norm_g=g2["q_g"], at_k_norm_g=g2["k_g"])
    packed = _pack([loss_part[0, 0:1]] + [small_full[n] for n in SMALL])
    summed = sum_parts(allgather_two_level(packed, name="allgather_small_grads"), name="sum_small_grads")
    parts = _unpack(summed, [()] + [small_full[n].shape for n in SMALL])
    loss = parts[0]
    gsmall = dict(zip(SMALL, parts[1:]))
    gsmall["ssd_conv_w"] = lax.dynamic_slice_in_dim(gsmall["ssd_conv_w"], me * ncw, ncw, axis=2)
    gsmall["hg_norm_g"] = lax.dynamic_slice_in_dim(gsmall["hg_norm_g"], me * nhg, nhg, axis=1)
    shapes = [w[n].shape for n in SMALL]
    d_p, m_p, v_p = adamw_plain(_pack([w[n] for n in SMALL]), _pack([gsmall[n] for n in SMALL]),
                                _pack([m[n] for n in SMALL]), _pack([v[n] for n in SMALL]), name="adamw_small")
    grads = dict(gsmall)
    deltas = dict(zip(SMALL, _unpack(d_p, shapes)))
    new_m = dict(zip(SMALL, _unpack(m_p, shapes)))
    new_v = dict(zip(SMALL, _unpack(v_p, shapes)))

    gbig = dict(ssd_w_in=g0["w_in"], ssd_w_out=g0["w_out"], hg_w_in=g1["w_in"], hg_w_out=g1["w_out"],
                at_w_in=g2["w_in"], at_w_out=g2["w_out"], dl_w_in=g3["w_in"], dl_w_out=g3["w_out"])
    send = []
    for n in BIG:
        r, c = shard2d[n].shape
        g = gbig[n]
        send.append(g.reshape(r, N_DEV, c).transpose(1, 0, 2) if n in BIG_IN else g.reshape(N_DEV, r, c))
    recv = all_to_all(send, name="exchange_weight_grads")
    for n, parts8 in zip(BIG, recv):
        gs, ds, ms, vs = adamw_sum(parts8, shard2d[n], m[n][0], v[n][0], name=f"adamw_{n}")
        grads[n], deltas[n], new_m[n], new_v[n] = gs[None], ds[None], ms[None], vs[None]

    return (loss, dx0[None], *[grads[n] for n in WEIGHT_ORDER], *[deltas[n] for n in WEIGHT_ORDER],
            *[new_m[n] for n in WEIGHT_ORDER], *[new_v[n] for n in WEIGHT_ORDER])
```

```python
import functools
import math

import jax
import jax.numpy as jnp
import numpy as np
from jax import lax
from jax.experimental import pallas as pl
from jax.experimental.pallas import tpu as pltpu

F32 = jnp.float32
BF16 = jnp.bfloat16
MXU_DTYPE = jnp.bfloat16
GRAD_WIRE_DTYPE = jnp.bfloat16
HIGHEST = lax.Precision.HIGHEST
MESH_ID = pl.DeviceIdType.MESH
N_DEV = 8

D_MODEL = 1024
EPS = 1e-6
NEG_BIG = -1e30

SSD_DI = 2048
SSD_HEADDIM = 64
SSD_HEADS = 32
SSD_GROUPS = 4
SSD_HPG = 8
SSD_STATE = 128
SSD_CONV = 7
SSD_CHUNK = 128
SSD_CONV_CH = SSD_DI + 2 * SSD_GROUPS * SSD_STATE
SSD_IN = SSD_DI + SSD_CONV_CH + 2 * SSD_HEADS
SSD_IN_PAD = 5376

HG_CHUNK = 32
HG_HEADS = 8
HG_D = 128
HG_W = 1024

AT_HEADS = 16
AT_KV = 8
AT_HD = 128
AT_QW = 2048
AT_KW = 1024
GRID_W = 64
ROPE_THETA = 10000.0

DL_PAIRS = ((128, 1), (512, 4), (2048, 16))
DL_HEADS = 16
DL_HD = 64
DL_W = 1024
DL_HALF = 64
REL_BUCKETS = 32
REL_MAX_DIST = 1024

ADAM_LR = 0.001
ADAM_B1 = 0.9
ADAM_B2 = 0.999
ADAM_EPS = 1e-08
ADAM_WD = 0.01
ADAM_STEP = 10

VMEM_LIMIT = 56 * 1024 * 1024


def _cp(*sem):
    return pltpu.CompilerParams(dimension_semantics=tuple(sem), vmem_limit_bytes=VMEM_LIMIT)


def _tile(n, cands=(1024, 768, 512, 384, 256, 128)):
    for c in cands:
        if n % c == 0:
            return c
    return n


def _dot(a, b, dims):
    return lax.dot_general(a.astype(MXU_DTYPE), b.astype(MXU_DTYPE), (dims, ((), ())), preferred_element_type=F32)


def _dot_exact(a, b, dims):
    return lax.dot_general(a, b, (dims, ((), ())), precision=HIGHEST, preferred_element_type=F32)


def _silu(x):
    return x * jax.nn.sigmoid(x)


def matmul(a, b, *, name, ta=False, tb=False, residual=None, out_dtype=F32, exact=False, tm=None, tn=None, tk=None):
    M, K = (a.shape[1], a.shape[0]) if ta else a.shape
    N = b.shape[0] if tb else b.shape[1]
    tm = tm or _tile(M, (512, 256, 128))
    tn = tn or _tile(N, (1024, 768, 512, 384, 256, 128))
    tk = tk or _tile(K, (1024, 768, 512, 384, 256, 128))
    nk = K // tk
    dims = (((0,) if ta else (1,)), ((1,) if tb else (0,)))

    def body(*refs):
        if residual is None:
            a_ref, b_ref, o_ref, acc = refs
            r_ref = None
        else:
            a_ref, b_ref, r_ref, o_ref, acc = refs
        k = pl.program_id(2)

        @pl.when(k == 0)
        def _():
            acc[...] = jnp.zeros_like(acc)

        if exact:
            acc[...] += _dot_exact(a_ref[...], b_ref[...], dims)
        else:
            acc[...] += _dot(a_ref[...], b_ref[...], dims)

        @pl.when(k == nk - 1)
        def _():
            r = acc[...]
            if r_ref is not None:
                r = r + r_ref[...]
            o_ref[...] = r.astype(o_ref.dtype)

    a_spec = pl.BlockSpec((tk, tm), lambda i, j, k: (k, i)) if ta else pl.BlockSpec((tm, tk), lambda i, j, k: (i, k))
    b_spec = pl.BlockSpec((tn, tk), lambda i, j, k: (j, k)) if tb else pl.BlockSpec((tk, tn), lambda i, j, k: (k, j))
    in_specs = [a_spec, b_spec]
    args = [a, b]
    if residual is not None:
        in_specs.append(pl.BlockSpec((tm, tn), lambda i, j, k: (i, j)))
        args.append(residual)
    return pl.pallas_call(
        body, name=name, grid=(M // tm, N // tn, nk), in_specs=in_specs,
        out_specs=pl.BlockSpec((tm, tn), lambda i, j, k: (i, j)),
        out_shape=jax.ShapeDtypeStruct((M, N), out_dtype),
        scratch_shapes=[pltpu.VMEM((tm, tn), F32)],
        compiler_params=_cp("parallel", "parallel", "arbitrary"),
    )(*args)


def _row_specs(rows, shared, R, W):
    specs = []
    for arr, col0, per_j, *wd in rows:
        w = wd[0] if wd else W
        specs.append(pl.BlockSpec((R, w), (lambda j, i, c=col0: (i, c + j)) if per_j else (lambda j, i, c=col0: (i, c))))
    for arr, per_j in shared:
        specs.append(pl.BlockSpec((arr.shape[0], W), (lambda j, i: (0, j)) if per_j else (lambda j, i: (0, 0))))
    return specs


def rowwise_fwd(name, fn, rows, shared, outs, *, W, ncb=1, R=256):
    S = rows[0][0].shape[0]
    R = min(R, S)
    nr, ns = len(rows), len(shared)

    def body(*refs):
        vals = [r[...] for r in refs[:nr + ns]]
        res = fn(*vals)
        for o_ref, r in zip(refs[nr + ns:], res):
            o_ref[...] = r.astype(o_ref.dtype)

    return pl.pallas_call(
        body, name=name, grid=(ncb, S // R),
        in_specs=_row_specs(rows, shared, R, W),
        out_specs=[pl.BlockSpec((R, w), lambda j, i: (i, j)) for w, _ in outs],
        out_shape=[jax.ShapeDtypeStruct((S, ncb * w), dt) for w, dt in outs],
        compiler_params=_cp("parallel", "parallel"),
    )(*[r[0] for r in rows], *[s[0] for s in shared])


def rowwise_bwd(name, fn, rows, shared, cots, *, W, ncb=1, R=256, diff_rows, diff_shared, add=None):
    S = rows[0][0].shape[0]
    R = min(R, S)
    nr, ns, nc = len(rows), len(shared), len(cots)
    nsteps = S // R

    def body(*refs):
        ins = refs[:nr + ns]
        ct_refs = refs[nr + ns:nr + ns + nc]
        pos = nr + ns + nc
        add_ref = None
        if add is not None:
            add_ref = refs[pos]
            pos += 1
        drow_refs = refs[pos:pos + len(diff_rows)]
        dsh_refs = refs[pos + len(diff_rows):]
        j, i = pl.program_id(0), pl.program_id(1)
        vals = [r[...] for r in ins]

        def f(*dv):
            full = list(vals)
            for idx, v in zip(list(diff_rows) + [nr + s for s in diff_shared], dv):
                full[idx] = v
            return tuple(fn(*full))

        prim = [vals[idx] for idx in diff_rows] + [vals[nr + s] for s in diff_shared]
        _, vjp = jax.vjp(f, *prim)
        grads = vjp(tuple(c[...] for c in ct_refs))
        for k, d_ref in enumerate(drow_refs):
            g = grads[k]
            if k == 0 and add_ref is not None:
                g = g + add_ref[...]
            d_ref[...] = g
        for k, (d_ref, s) in enumerate(zip(dsh_refs, diff_shared)):
            g = grads[len(diff_rows) + k]
            first = (i == 0) if shared[s][1] else jnp.logical_and(i == 0, j == 0)

            @pl.when(first)
            def _(d_ref=d_ref, g=g):
                d_ref[...] = g

            @pl.when(jnp.logical_not(first))
            def _(d_ref=d_ref, g=g):
                d_ref[...] += g

    in_specs = _row_specs(rows, shared, R, W)
    wo = [c.shape[1] // ncb for c in cots]
    in_specs += [pl.BlockSpec((R, w), lambda j, i: (i, j)) for w in wo]
    args = [r[0] for r in rows] + [s[0] for s in shared] + list(cots)
    if add is not None:
        in_specs.append(pl.BlockSpec((R, W), lambda j, i: (i, j)))
        args.append(add)
    dws = [(rows[r][3] if len(rows[r]) > 3 else W) for r in diff_rows]
    out_specs = [pl.BlockSpec((R, w), lambda j, i: (i, j)) for w in dws]
    out_shape = [jax.ShapeDtypeStruct((S, ncb * w), F32) for w in dws]
    for s in diff_shared:
        arr, per_j = shared[s]
        out_specs.append(pl.BlockSpec((arr.shape[0], W), (lambda j, i: (0, j)) if per_j else (lambda j, i: (0, 0))))
        out_shape.append(jax.ShapeDtypeStruct((arr.shape[0], ncb * W if per_j else W), F32))
    return pl.pallas_call(
        body, name=name, grid=(ncb, nsteps), in_specs=in_specs, out_specs=out_specs, out_shape=out_shape,
        compiler_params=_cp("arbitrary", "arbitrary"),
    )(*args)


def _row_specs2(rows, shared, R, W, ncb):
    specs = []
    for arr, col0, per_j, *wd in rows:
        w = wd[0] if wd else W
        if per_j:
            assert col0 % ncb == 0
            specs.append(pl.BlockSpec((R, ncb * w), lambda i, c=col0 // ncb: (i, c)))
        else:
            specs.append(pl.BlockSpec((R, w), lambda i, c=col0: (i, c)))
    for arr, per_j in shared:
        specs.append(pl.BlockSpec((arr.shape[0], ncb * W if per_j else arr.shape[1]), lambda i: (0, 0)))
    return specs


def _col_block(ref, per_j, j, w):
    return ref[:, j * w:(j + 1) * w] if per_j else ref[...]


def rowwise_fwd(name, fn, rows, shared, outs, *, W, ncb=1, R=256):
    S = rows[0][0].shape[0]
    R = min(R, S)
    nr, ns = len(rows), len(shared)
    widths = [(r[3] if len(r) > 3 else W) for r in rows]
    per_j = [r[2] for r in rows] + [s[1] for s in shared]
    ws = widths + [W] * ns

    def body(*refs):
        for j in range(ncb):
            vals = [_col_block(refs[k], per_j[k], j, ws[k]) for k in range(nr + ns)]
            res = fn(*vals)
            for o_ref, r, (wo, _) in zip(refs[nr + ns:], res, outs):
                o_ref[:, j * wo:(j + 1) * wo] = r.astype(o_ref.dtype)

    return pl.pallas_call(
        body, name=name, grid=(S // R,),
        in_specs=_row_specs2(rows, shared, R, W, ncb),
        out_specs=[pl.BlockSpec((R, ncb * w), lambda i: (i, 0)) for w, _ in outs],
        out_shape=[jax.ShapeDtypeStruct((S, ncb * w), dt) for w, dt in outs],
        compiler_params=_cp("parallel"),
    )(*[r[0] for r in rows], *[s[0] for s in shared])


def rowwise_bwd(name, fn, rows, shared, cots, *, W, ncb=1, R=256, diff_rows, diff_shared, add=None):
    S = rows[0][0].shape[0]
    R = min(R, S)
    nr, ns, nc = len(rows), len(shared), len(cots)
    widths = [(r[3] if len(r) > 3 else W) for r in rows]
    per_j = [r[2] for r in rows] + [s[1] for s in shared]
    ws = widths + [W] * ns
    wo = [c.shape[1] // ncb for c in cots]
    dws = [widths[r] for r in diff_rows]

    def body(*refs):
        ins = refs[:nr + ns]
        ct_refs = refs[nr + ns:nr + ns + nc]
        pos = nr + ns + nc
        add_ref = None
        if add is not None:
            add_ref = refs[pos]
            pos += 1
        drow_refs = refs[pos:pos + len(diff_rows)]
        dsh_refs = refs[pos + len(diff_rows):]
        i = pl.program_id(0)
        tot = [None] * len(diff_shared)
        for j in range(ncb):
            vals = [_col_block(ins[k], per_j[k], j, ws[k]) for k in range(nr + ns)]

            def f(*dv):
                full = list(vals)
                for idx, v in zip(list(diff_rows) + [nr + s for s in diff_shared], dv):
                    full[idx] = v
                return tuple(fn(*full))

            prim = [vals[idx] for idx in diff_rows] + [vals[nr + s] for s in diff_shared]
            _, vjp = jax.vjp(f, *prim)
            grads = vjp(tuple(c[:, j * w:(j + 1) * w] for c, w in zip(ct_refs, wo)))
            for k, (d_ref, w) in enumerate(zip(drow_refs, dws)):
                g = grads[k]
                if k == 0 and add_ref is not None:
                    g = g + add_ref[:, j * w:(j + 1) * w]
                d_ref[:, j * w:(j + 1) * w] = g
            for k, (d_ref, s) in enumerate(zip(dsh_refs, diff_shared)):
                g = grads[len(diff_rows) + k]
                if shared[s][1]:
                    @pl.when(i == 0)
                    def _(d_ref=d_ref, g=g, j=j):
                        d_ref[:, j * W:(j + 1) * W] = g

                    @pl.when(i != 0)
                    def _(d_ref=d_ref, g=g, j=j):
                        d_ref[:, j * W:(j + 1) * W] += g
                else:
                    tot[k] = g if tot[k] is None else tot[k] + g
        for k, (d_ref, s) in enumerate(zip(dsh_refs, diff_shared)):
            if not shared[s][1]:
                @pl.when(i == 0)
                def _(d_ref=d_ref, g=tot[k]):
                    d_ref[...] = g

                @pl.when(i != 0)
                def _(d_ref=d_ref, g=tot[k]):
                    d_ref[...] += g

    in_specs = _row_specs2(rows, shared, R, W, ncb)
    in_specs += [pl.BlockSpec((R, ncb * w), lambda i: (i, 0)) for w in wo]
    args = [r[0] for r in rows] + [s[0] for s in shared] + list(cots)
    if add is not None:
        in_specs.append(pl.BlockSpec((R, ncb * dws[0]), lambda i: (i, 0)))
        args.append(add)
    out_specs = [pl.BlockSpec((R, ncb * w), lambda i: (i, 0)) for w in dws]
    out_shape = [jax.ShapeDtypeStruct((S, ncb * w), F32) for w in dws]
    for s in diff_shared:
        arr, pj = shared[s]
        shp = (arr.shape[0], ncb * W if pj else arr.shape[1])
        out_specs.append(pl.BlockSpec(shp, lambda i: (0, 0)))
        out_shape.append(jax.ShapeDtypeStruct(shp, F32))
    return pl.pallas_call(
        body, name=name, grid=(S // R,), in_specs=in_specs, out_specs=out_specs, out_shape=out_shape,
        compiler_params=_cp("arbitrary"),
    )(*args)


def _rms(x, g):
    return x * lax.rsqrt(jnp.mean(x * x, axis=-1, keepdims=True) + EPS) * g


def _prenorm_fn(x, g):
    return (_rms(x, g),)


def loss_head(x, g, tgt, *, R=256):
    S, D = x.shape
    R = min(R, S)

    def fn(xv, gv, tv):
        err = _rms(xv, gv) - tv
        return 0.5 * jnp.sum(jnp.mean(err * err, axis=-1, keepdims=True), axis=0, keepdims=True)

    def body(x_ref, g_ref, t_ref, loss_ref, dx_ref, dg_ref):
        i = pl.program_id(0)
        tv = t_ref[...]
        val, vjp = jax.vjp(lambda a, b: fn(a, b, tv), x_ref[...], g_ref[...])
        dx, dg = vjp(jnp.ones((1, 1), F32))
        dx_ref[...] = dx

        @pl.when(i == 0)
        def _():
            loss_ref[...] = jnp.zeros_like(loss_ref) + val
            dg_ref[...] = dg

        @pl.when(i != 0)
        def _():
            loss_ref[...] += val
            dg_ref[...] += dg

    return pl.pallas_call(
        body, name="loss_head", grid=(S // R,),
        in_specs=[pl.BlockSpec((R, D), lambda i: (i, 0)), pl.BlockSpec((1, D), lambda i: (0, 0)),
                  pl.BlockSpec((R, D), lambda i: (i, 0))],
        out_specs=[pl.BlockSpec((1, 128), lambda i: (0, 0)), pl.BlockSpec((R, D), lambda i: (i, 0)),
                   pl.BlockSpec((1, D), lambda i: (0, 0))],
        out_shape=[jax.ShapeDtypeStruct((1, 128), F32), jax.ShapeDtypeStruct((S, D), F32),
                   jax.ShapeDtypeStruct((1, D), F32)],
        compiler_params=_cp("arbitrary"),
    )(x, g, tgt)


@jax.custom_vjp
def _softplus(x):
    z = jnp.exp(-jnp.abs(x))
    u = 1.0 + z
    log1p = jnp.where(u == 1.0, z, jnp.log(u) * (z / jnp.where(u == 1.0, 1.0, u - 1.0)))
    return jnp.maximum(x, 0.0) + log1p


def _softplus_fwd(x):
    return _softplus(x), x


def _softplus_bwd(x, ct):
    return (ct * jax.nn.sigmoid(x),)


_softplus.defvjp(_softplus_fwd, _softplus_bwd)


def _dt_fn(raw, bias):
    return (_softplus(raw + bias),)


CONV_CB = 256
CONV_RB = 512
CONV_PAD = 8


def ssd_conv_fwd(u, conv_w, conv_b):
    S = u.shape[0]
    ncb = SSD_CONV_CH // CONV_CB
    col0 = SSD_DI // CONV_CB
    RB = min(CONV_RB, S)

    def body(x_ref, w_ref, b_ref, o_ref, pad):
        pad[0:CONV_PAD, :] = jnp.zeros((CONV_PAD, CONV_CB), F32)
        pad[S + CONV_PAD:S + 2 * CONV_PAD, :] = jnp.zeros((CONV_PAD, CONV_CB), F32)
        pad[CONV_PAD:S + CONV_PAD, :] = x_ref[...]
        w = w_ref[...]
        b = b_ref[...]
        for r in range(S // RB):
            acc = jnp.zeros((RB, CONV_CB), F32) + b
            for k in range(SSD_CONV):
                off = r * RB + CONV_PAD + k - SSD_CONV // 2
                acc = acc + pad[off:off + RB, :] * w[k:k + 1, :]
            o_ref[r * RB:(r + 1) * RB, :] = _silu(acc)

    return pl.pallas_call(
        body, name="ssd_conv_fwd", grid=(ncb,),
        in_specs=[pl.BlockSpec((S, CONV_CB), lambda j: (0, col0 + j)),
                  pl.BlockSpec((SSD_CONV, CONV_CB), lambda j: (0, j)),
                  pl.BlockSpec((1, CONV_CB), lambda j: (0, j))],
        out_specs=pl.BlockSpec((S, CONV_CB), lambda j: (0, j)),
        out_shape=jax.ShapeDtypeStruct((S, SSD_CONV_CH), F32),
        scratch_shapes=[pltpu.VMEM((S + 2 * CONV_PAD, CONV_CB), F32)],
        compiler_params=_cp("parallel"),
    )(u, conv_w, conv_b)


def ssd_conv_bwd(u, conv_w, conv_b, dact):
    S = u.shape[0]
    ncb = SSD_CONV_CH // CONV_CB
    col0 = SSD_DI // CONV_CB
    RB = min(CONV_RB, S)
    half = SSD_CONV // 2

    def body(x_ref, w_ref, b_ref, da_ref, dx_ref, dw_ref, db_ref, xpad, dpad):
        z8 = jnp.zeros((CONV_PAD, CONV_CB), F32)
        xpad[0:CONV_PAD, :] = z8
        xpad[S + CONV_PAD:S + 2 * CONV_PAD, :] = z8
        dpad[0:CONV_PAD, :] = z8
        dpad[S + CONV_PAD:S + 2 * CONV_PAD, :] = z8
        xpad[CONV_PAD:S + CONV_PAD, :] = x_ref[...]
        w = w_ref[...]
        b = b_ref[...]
        dws = [jnp.zeros((1, CONV_CB), F32) for _ in range(SSD_CONV)]
        db = jnp.zeros((1, CONV_CB), F32)
        for r in range(S // RB):
            acc = jnp.zeros((RB, CONV_CB), F32) + b
            xs = []
            for k in range(SSD_CONV):
                off = r * RB + CONV_PAD + k - half
                xk = xpad[off:off + RB, :]
                xs.append(xk)
                acc = acc + xk * w[k:k + 1, :]
            sg = jax.nn.sigmoid(acc)
            dc = da_ref[r * RB:(r + 1) * RB, :] * (sg * (1.0 + acc * (1.0 - sg)))
            dpad[r * RB + CONV_PAD:(r + 1) * RB + CONV_PAD, :] = dc
            db = db + jnp.sum(dc, axis=0, keepdims=True)
            for k in range(SSD_CONV):
                dws[k] = dws[k] + jnp.sum(xs[k] * dc, axis=0, keepdims=True)
        for r in range(S // RB):
            acc = jnp.zeros((RB, CONV_CB), F32)
            for k in range(SSD_CONV):
                off = r * RB + CONV_PAD + half - k
                acc = acc + dpad[off:off + RB, :] * w[k:k + 1, :]
            dx_ref[r * RB:(r + 1) * RB, :] = acc
        for k in range(SSD_CONV):
            dw_ref[k:k + 1, :] = dws[k]
        dw_ref[SSD_CONV:SSD_CONV + 1, :] = jnp.zeros((1, CONV_CB), F32)
        db_ref[...] = db

    return pl.pallas_call(
        body, name="ssd_conv_bwd", grid=(ncb,),
        in_specs=[pl.BlockSpec((S, CONV_CB), lambda j: (0, col0 + j)),
                  pl.BlockSpec((SSD_CONV, CONV_CB), lambda j: (0, j)),
                  pl.BlockSpec((1, CONV_CB), lambda j: (0, j)),
                  pl.BlockSpec((S, CONV_CB), lambda j: (0, j))],
        out_specs=[pl.BlockSpec((S, CONV_CB), lambda j: (0, j)),
                   pl.BlockSpec((SSD_CONV + 1, CONV_CB), lambda j: (0, j)),
                   pl.BlockSpec((1, CONV_CB), lambda j: (0, j))],
        out_shape=[jax.ShapeDtypeStruct((S, SSD_CONV_CH), F32),
                   jax.ShapeDtypeStruct((SSD_CONV + 1, SSD_CONV_CH), F32),
                   jax.ShapeDtypeStruct((1, SSD_CONV_CH), F32)],
        scratch_shapes=[pltpu.VMEM((S + 2 * CONV_PAD, CONV_CB), F32), pltpu.VMEM((S + 2 * CONV_PAD, CONV_CB), F32)],
        compiler_params=_cp("parallel"),
    )(u, conv_w, conv_b, dact)


def _ssd_chunk(states, xs, Bg, Cg, dt, alog, *, reverse):
    Q = SSD_CHUNK
    r = lax.broadcasted_iota(jnp.int32, (Q, Q), 0)
    c = lax.broadcasted_iota(jnp.int32, (Q, Q), 1)
    keep = (c >= r) if reverse else (c <= r)
    tri = jnp.where(keep, 1.0, 0.0).astype(F32)
    a = dt * (-jnp.exp(alog))
    cum = _dot_exact(tri, a, ((1,), (0,)))
    cum_t = jnp.transpose(cum)
    last = 0 if reverse else Q - 1
    CB = _dot(Cg, Bg, ((1,), (1,)))
    new_states, ys = [], []
    for h in range(SSD_HPG):
        col = h + (SSD_HPG if reverse else 0)
        cum_c = cum[:, col:col + 1]
        cum_r = cum_t[col:col + 1, :]
        dt_c = dt[:, col:col + 1]
        cum_l = cum_c[last:last + 1, :]
        diff = jnp.where(keep, cum_c - cum_r, 0.0)
        L = jnp.where(keep, jnp.exp(diff), 0.0)
        xdt = xs[h] * dt_c
        y = _dot(CB * L, xdt, ((1,), (0,)))
        y = y + _dot(Cg * jnp.exp(cum_c), states[h], ((1,), (0,)))
        ns = jnp.exp(cum_l) * states[h] + _dot(Bg * jnp.exp(cum_l - cum_c), xdt, ((0,), (0,)))
        new_states.append(ns)
        ys.append(y)
    return new_states, ys


def _ssd_group_layout(t):
    r = t.shape[0]
    g = t[:, :2 * SSD_HEADS].reshape(r, 2, SSD_GROUPS, SSD_HPG).transpose(2, 0, 1, 3).reshape(SSD_GROUPS, r, 2 * SSD_HPG)
    return jnp.pad(g, ((0, 0), (0, 0), (0, 128 - 2 * SSD_HPG)))


def _ssd_head_layout(t):
    r = t.shape[1]
    h = t[:, :, :2 * SSD_HPG].reshape(SSD_GROUPS, r, 2, SSD_HPG).transpose(1, 2, 0, 3).reshape(r, 2 * SSD_HEADS)
    return jnp.pad(h, ((0, 0), (0, 128 - 2 * SSD_HEADS)))


def ssd_scan_fwd(act, dt, alog, *, reverse, y_prev=None):
    S = act.shape[0]
    Q, N, P = SSD_CHUNK, SSD_STATE, SSD_HEADDIM
    nc = S // Q
    GW = SSD_HPG * P

    def cidx(i):
        return (nc - 1 - i) if reverse else i

    def body(*refs):
        if y_prev is None:
            x_ref, b_ref, c_ref, dt_ref, al_ref, y_ref, st_ref, state = refs
            yp_ref = None
        else:
            x_ref, b_ref, c_ref, dt_ref, al_ref, yp_ref, y_ref, st_ref, state = refs
        g, i = pl.program_id(0), pl.program_id(1)

        @pl.when(i == 0)
        def _():
            state[...] = jnp.zeros_like(state)

        states = [state[h] for h in range(SSD_HPG)]
        for h in range(SSD_HPG):
            st_ref[0, h] = states[h]
        xv = x_ref[...]
        xs = [xv[:, h * P:(h + 1) * P] for h in range(SSD_HPG)]
        ns, ys = _ssd_chunk(states, xs, b_ref[...], c_ref[...], dt_ref[0], al_ref[0], reverse=reverse)
        for h in range(SSD_HPG):
            state[h] = ns[h]
            yh = ys[h]
            if yp_ref is not None:
                yh = yh + yp_ref[:, h * P:(h + 1) * P]
            y_ref[:, h * P:(h + 1) * P] = yh

    in_specs = [pl.BlockSpec((Q, GW), lambda g, i: (cidx(i), g)),
                pl.BlockSpec((Q, N), lambda g, i: (cidx(i), SSD_DI // N + g)),
                pl.BlockSpec((Q, N), lambda g, i: (cidx(i), SSD_DI // N + SSD_GROUPS + g)),
                pl.BlockSpec((1, Q, 128), lambda g, i: (g, cidx(i), 0)),
                pl.BlockSpec((1, 1, 128), lambda g, i: (g, 0, 0))]
    args = [act, act, act, dt, alog]
    if y_prev is not None:
        in_specs.append(pl.BlockSpec((Q, GW), lambda g, i: (cidx(i), g)))
        args.append(y_prev)
    return pl.pallas_call(
        body, name=f"ssd_scan_fwd_{int(reverse)}", grid=(SSD_GROUPS, nc), in_specs=in_specs,
        out_specs=[pl.BlockSpec((Q, GW), lambda g, i: (cidx(i), g)),
                   pl.BlockSpec((1, SSD_HPG, N, P), lambda g, i: (cidx(i), g, 0, 0))],
        out_shape=[jax.ShapeDtypeStruct((S, SSD_DI), F32), jax.ShapeDtypeStruct((nc, SSD_HEADS, N, P), F32)],
        scratch_shapes=[pltpu.VMEM((SSD_HPG, N, P), F32)],
        compiler_params=_cp("arbitrary", "arbitrary"),
    )(*args)


def ssd_scan_bwd(act, dt, alog, states, dy, prev_x, *, reverse, prev=None):
    S = act.shape[0]
    Q, N, P = SSD_CHUNK, SSD_STATE, SSD_HEADDIM
    nc = S // Q
    GW = SSD_HPG * P

    def cidx(i):
        return i if reverse else (nc - 1 - i)

    def body(*refs):
        x_ref, b_ref, c_ref, dt_ref, al_ref, st_ref, dy_ref, px_ref = refs[:8]
        pos = 8
        if prev is not None:
            pb_ref, pc_ref, pdt_ref, pal_ref = refs[pos:pos + 4]
            pos += 4
        dx_ref, db_ref, dc_ref, ddt_ref, dal_ref, dstate = refs[pos:]
        g, i = pl.program_id(0), pl.program_id(1)

        @pl.when(i == 0)
        def _():
            dstate[...] = jnp.zeros_like(dstate)

        xv = x_ref[...]
        dyv = dy_ref[...]
        xs = [xv[:, h * P:(h + 1) * P] for h in range(SSD_HPG)]
        dys = [dyv[:, h * P:(h + 1) * P] for h in range(SSD_HPG)]
        states = [st_ref[0, h] for h in range(SSD_HPG)]
        dstates = [dstate[h] for h in range(SSD_HPG)]

        def f(states, xs, Bg, Cg, dtv, al):
            return _ssd_chunk(states, xs, Bg, Cg, dtv, al, reverse=reverse)

        _, vjp = jax.vjp(f, states, xs, b_ref[...], c_ref[...], dt_ref[0], al_ref[0])
        dst, dxs, dB, dC, ddt, dal = vjp((dstates, dys))
        for h in range(SSD_HPG):
            dstate[h] = dst[h]
            dx_ref[:, h * P:(h + 1) * P] = dxs[h] + px_ref[:, h * P:(h + 1) * P]
        if prev is not None:
            dB = dB + pb_ref[...]
            dC = dC + pc_ref[...]
            ddt = ddt + pdt_ref[0]
        db_ref[...] = dB
        dc_ref[...] = dC
        ddt_ref[0] = ddt

        @pl.when(i == 0)
        def _():
            dal_ref[0] = dal + (pal_ref[0] if prev is not None else 0.0)

        @pl.when(i != 0)
        def _():
            dal_ref[0] += dal

    xspec = pl.BlockSpec((Q, GW), lambda g, i: (cidx(i), g))
    nspec_b = pl.BlockSpec((Q, N), lambda g, i: (cidx(i), SSD_DI // N + g))
    nspec_c = pl.BlockSpec((Q, N), lambda g, i: (cidx(i), SSD_DI // N + SSD_GROUPS + g))
    dtspec = pl.BlockSpec((1, Q, 128), lambda g, i: (g, cidx(i), 0))
    alspec = pl.BlockSpec((1, 1, 128), lambda g, i: (g, 0, 0))
    in_specs = [xspec, nspec_b, nspec_c,
                dtspec, alspec,
                pl.BlockSpec((1, SSD_HPG, N, P), lambda g, i: (cidx(i), g, 0, 0)),
                xspec]
    gspec = pl.BlockSpec((Q, N), lambda g, i: (cidx(i), g))
    in_specs.append(xspec)
    args = [act, act, act, dt, alog, states, dy, prev_x]
    if prev is not None:
        in_specs += [gspec, gspec, dtspec, alspec]
        args += list(prev)
    outs = pl.pallas_call(
        body, name=f"ssd_scan_bwd_{int(reverse)}", grid=(SSD_GROUPS, nc), in_specs=in_specs,
        out_specs=[pl.BlockSpec((Q, GW), lambda g, i: (cidx(i), g)),
                   pl.BlockSpec((Q, N), lambda g, i: (cidx(i), g)),
                   pl.BlockSpec((Q, N), lambda g, i: (cidx(i), g)),
                   dtspec, alspec],
        out_shape=[jax.ShapeDtypeStruct((S, SSD_DI), F32), jax.ShapeDtypeStruct((S, SSD_GROUPS * N), F32),
                   jax.ShapeDtypeStruct((S, SSD_GROUPS * N), F32),
                   jax.ShapeDtypeStruct((SSD_GROUPS, S, 128), F32), jax.ShapeDtypeStruct((SSD_GROUPS, 1, 128), F32)],
        scratch_shapes=[pltpu.VMEM((SSD_HPG, N, P), F32)],
        compiler_params=_cp("arbitrary", "arbitrary"),
    )(*args)
    return outs


def _ssd_post_fn(y, xs, z, dexp, ng):
    t = (y + xs * dexp) * _silu(z)
    return (_rms(t, ng),)


def _hg_chunk(state, qraw, fraw, v, lb, *, reverse):
    C = HG_CHUNK
    r = lax.broadcasted_iota(jnp.int32, (C, C), 0)
    c = lax.broadcasted_iota(jnp.int32, (C, C), 1)
    keep = (c >= r) if reverse else (c <= r)
    tri = jnp.where(keep, 1.0, 0.0).astype(F32)
    q = _silu(qraw)
    f = lb + (1.0 - lb) * jax.nn.sigmoid(fraw)
    k = 1.0 - f
    g = jnp.log(f)
    G = _dot_exact(tri, g, ((1,), (0,)))
    ref_row = C // 2 - 1 if reverse else C // 2
    last_row = 0 if reverse else C - 1
    Gr = G[ref_row:ref_row + 1, :]
    Gl = G[last_row:last_row + 1, :]
    q_t = q * jnp.exp(G - Gr)
    k_t = k * jnp.exp(Gr - G)
    att = jnp.where(keep, _dot(q_t, k_t, ((1,), (1,))), 0.0)
    o = _dot(att, v, ((1,), (0,))) + _dot(q * jnp.exp(G), state, ((1,), (0,)))
    kd = k * jnp.exp(Gl - G)
    new_state = jnp.transpose(jnp.exp(Gl)) * state + _dot(kd, v, ((0,), (0,)))
    return new_state, o


def hg_scan_fwd(u, lb, *, reverse, o_prev=None, rows=256):
    S = u.shape[0]
    nh = HG_HEADS
    rows = min(rows, S)
    nsteps = S // rows
    ncb = rows // HG_CHUNK
    f_sec = 2 if reverse else 1

    def blk(i):
        return (nsteps - 1 - i) if reverse else i

    def body(*refs):
        if o_prev is None:
            q_ref, f_ref, v_ref, lb_ref, o_ref, st_ref, state = refs
            op_ref = None
        else:
            q_ref, f_ref, v_ref, lb_ref, op_ref, o_ref, st_ref, state = refs
        i = pl.program_id(0)

        @pl.when(i == 0)
        def _():
            state[...] = jnp.zeros_like(state)

        def chunk(cc, carry):
            ci = (ncb - 1 - cc) if reverse else cc
            sl = pl.ds(pl.multiple_of(ci * HG_CHUNK, HG_CHUNK), HG_CHUNK)
            for h in range(nh):
                hs = slice(h * HG_D, (h + 1) * HG_D)
                st = state[h]
                st_ref[ci, h] = st
                ns, o = _hg_chunk(st, q_ref[sl, hs], f_ref[sl, hs], v_ref[sl, hs], lb_ref[:, hs], reverse=reverse)
                state[h] = ns
                if op_ref is not None:
                    o = o + op_ref[sl, hs]
                o_ref[sl, hs] = o
            return carry

        lax.fori_loop(0, ncb, chunk, 0)

    rowspec = lambda sec: pl.BlockSpec((rows, HG_W), lambda i: (blk(i), sec))
    in_specs = [rowspec(0), rowspec(f_sec), rowspec(3), pl.BlockSpec((1, HG_W), lambda i: (0, 0))]
    args = [u, u, u, lb]
    if o_prev is not None:
        in_specs.append(rowspec(0))
        args.append(o_prev)
    return pl.pallas_call(
        body, name=f"hg_scan_fwd_{int(reverse)}", grid=(nsteps,), in_specs=in_specs,
        out_specs=[rowspec(0), pl.BlockSpec((ncb, nh, HG_D, HG_D), lambda i: (blk(i), 0, 0, 0))],
        out_shape=[jax.ShapeDtypeStruct((S, HG_W), F32), jax.ShapeDtypeStruct((S // HG_CHUNK, nh, HG_D, HG_D), F32)],
        scratch_shapes=[pltpu.VMEM((nh, HG_D, HG_D), F32)],
        compiler_params=_cp("arbitrary"),
    )(*args)


def hg_scan_bwd(u, lb, states, do, *, reverse, prev=None, rows=256):
    S = u.shape[0]
    nh = HG_HEADS
    rows = min(rows, S)
    nsteps = S // rows
    ncb = rows // HG_CHUNK
    f_sec = 2 if reverse else 1

    def blk(i):
        return i if reverse else (nsteps - 1 - i)

    def body(*refs):
        q_ref, f_ref, v_ref, lb_ref, st_ref, do_ref = refs[:6]
        pos = 6
        if prev is not None:
            pq_ref, pv_ref, plb_ref = refs[pos:pos + 3]
            pos += 3
        dq_ref, df_ref, dv_ref, dlb_ref, dstate = refs[pos:]
        i = pl.program_id(0)

        @pl.when(i == 0)
        def _():
            dstate[...] = jnp.zeros_like(dstate)
            dlb_ref[...] = plb_ref[...] if prev is not None else jnp.zeros_like(dlb_ref)

        def chunk(cc, carry):
            ci = cc if reverse else (ncb - 1 - cc)
            sl = pl.ds(pl.multiple_of(ci * HG_CHUNK, HG_CHUNK), HG_CHUNK)
            for h in range(nh):
                hs = slice(h * HG_D, (h + 1) * HG_D)
                _, vjp = jax.vjp(functools.partial(_hg_chunk, reverse=reverse), st_ref[ci, h],
                                 q_ref[sl, hs], f_ref[sl, hs], v_ref[sl, hs], lb_ref[:, hs])
                dst, dq, df, dv, dlb = vjp((dstate[h], do_ref[sl, hs]))
                dstate[h] = dst
                if prev is not None:
                    dq = dq + pq_ref[sl, hs]
                    dv = dv + pv_ref[sl, hs]
                dq_ref[sl, hs] = dq
                df_ref[sl, hs] = df
                dv_ref[sl, hs] = dv
                dlb_ref[:, hs] += dlb
            return carry

        lax.fori_loop(0, ncb, chunk, 0)

    rowspec = lambda sec: pl.BlockSpec((rows, HG_W), lambda i: (blk(i), sec))
    lbspec = pl.BlockSpec((1, HG_W), lambda i: (0, 0))
    in_specs = [rowspec(0), rowspec(f_sec), rowspec(3), lbspec,
                pl.BlockSpec((ncb, nh, HG_D, HG_D), lambda i: (blk(i), 0, 0, 0)), rowspec(0)]
    args = [u, u, u, lb, states, do]
    if prev is not None:
        in_specs += [rowspec(0), rowspec(0), lbspec]
        args += list(prev)
    return pl.pallas_call(
        body, name=f"hg_scan_bwd_{int(reverse)}", grid=(nsteps,), in_specs=in_specs,
        out_specs=[rowspec(0), rowspec(0), rowspec(0), lbspec],
        out_shape=[jax.ShapeDtypeStruct((S, HG_W), F32)] * 3 + [jax.ShapeDtypeStruct((1, HG_W), F32)],
        scratch_shapes=[pltpu.VMEM((nh, HG_D, HG_D), F32)],
        compiler_params=_cp("arbitrary"),
    )(*args)


def _hg_lb_fn(lbp):
    m = jnp.max(lbp, axis=0, keepdims=True)
    e = jnp.exp(lbp - m)
    sm = e / jnp.sum(e, axis=0, keepdims=True)
    return ((sm[0:1] + sm[1:2]) - sm[0:1],)


def hg_lb_fwd(lbp):
    def body(x_ref, o_ref):
        o_ref[...] = _hg_lb_fn(x_ref[...])[0]

    return pl.pallas_call(body, name="hg_lb_fwd", out_shape=jax.ShapeDtypeStruct((1, HG_W), F32))(lbp)


def hg_lb_bwd(lbp, dlb):
    def body(x_ref, d_ref, o_ref):
        _, vjp = jax.vjp(_hg_lb_fn, x_ref[...])
        o_ref[...] = vjp((d_ref[...],))[0]

    return pl.pallas_call(body, name="hg_lb_bwd", out_shape=jax.ShapeDtypeStruct(lbp.shape, F32))(lbp, dlb)


def _hg_post_fn(o, gate, ng):
    return (_rms(o, ng) * _silu(gate),)


def _gate_fn(o, gate):
    return (o * _silu(gate),)


def _rope_tables(S):
    t = np.arange(S)
    row = (t // GRID_W).astype(np.float32)
    col = (t % GRID_W).astype(np.float32)
    half = AT_HD // 4
    inv = (ROPE_THETA ** (-np.arange(0, 2 * half, 2, dtype=np.float32) / np.float32(2 * half))).astype(np.float32)
    ar = row[:, None] * inv[None, :]
    ac = col[:, None] * inv[None, :]
    return ar.astype(np.float32), ac.astype(np.float32)


def _rope_swap_matrix():
    p = np.zeros((AT_HD, AT_HD), np.float32)
    for i in range(AT_HD):
        p[(i + 32) if (i % 64) < 32 else (i - 32), i] = 1.0
    return p


@jax.custom_vjp
def _half_swap(x):
    ax = x.ndim - 1
    lane = lax.broadcasted_iota(jnp.int32, x.shape, ax)
    return jnp.where((lane & 32) == 0, pltpu.roll(x, 96, ax), pltpu.roll(x, 32, ax))


_half_swap.defvjp(lambda x: (_half_swap(x), None), lambda _, ct: (_half_swap(ct),))


def _make_qk_fn(scale):
    def fn(x, ct, st, g):
        n = _rms(x, g)
        return ((n * ct + _half_swap(n) * st) * scale,)
    return fn


def flash_fwd(q, k, v, *, v_col0=0, tq=512, tk=512):
    S = q.shape[0]
    tq, tk = min(tq, S), min(tk, S)
    nk = S // tk
    G = AT_HEADS // AT_KV

    def body(q_ref, k_ref, v_ref, o_ref, lse_ref, m_s, l_s, acc):
        ki = pl.program_id(2)

        @pl.when(ki == 0)
        def _():
            m_s[...] = jnp.full_like(m_s, -jnp.inf)
            l_s[...] = jnp.zeros_like(l_s)
            acc[...] = jnp.zeros_like(acc)

        kv, vv = k_ref[...], v_ref[...]
        for g in range(G):
            s = _dot(q_ref[:, g * AT_HD:(g + 1) * AT_HD], kv, ((1,), (1,)))
            m_old = m_s[g]
            m_new = jnp.maximum(m_old, jnp.max(s, axis=1, keepdims=True))
            alpha = jnp.exp(m_old - m_new)
            p = jnp.exp(s - m_new)
            l_s[g] = alpha * l_s[g] + jnp.sum(p, axis=1, keepdims=True)
            acc[g] = alpha * acc[g] + _dot(p, vv, ((1,), (0,)))
            m_s[g] = m_new

        @pl.when(ki == nk - 1)
        def _():
            for g in range(G):
                o_ref[:, g * AT_HD:(g + 1) * AT_HD] = acc[g] / l_s[g]
                lse_ref[0, :, g:g + 1] = m_s[g] + jnp.log(l_s[g])

    return pl.pallas_call(
        body, name="flash_fwd", grid=(AT_KV, S // tq, nk),
        in_specs=[pl.BlockSpec((tq, G * AT_HD), lambda h, i, j: (i, h)),
                  pl.BlockSpec((tk, AT_HD), lambda h, i, j: (j, h)),
                  pl.BlockSpec((tk, AT_HD), lambda h, i, j: (j, v_col0 + h))],
        out_specs=[pl.BlockSpec((tq, G * AT_HD), lambda h, i, j: (i, h)),
                   pl.BlockSpec((1, tq, G), lambda h, i, j: (h, i, 0))],
        out_shape=[jax.ShapeDtypeStruct((S, AT_QW), F32), jax.ShapeDtypeStruct((AT_KV, S, G), F32)],
        scratch_shapes=[pltpu.VMEM((G, tq, 1), F32), pltpu.VMEM((G, tq, 1), F32), pltpu.VMEM((G, tq, AT_HD), F32)],
        compiler_params=_cp("parallel", "parallel", "arbitrary"),
    )(q, k, v)


def flash_bwd_dq(q, k, v, o, lse, do, *, v_col0=0, tq=512, tk=512):
    S = q.shape[0]
    tq, tk = min(tq, S), min(tk, S)
    nk = S // tk
    G = AT_HEADS // AT_KV

    def body(q_ref, k_ref, v_ref, o_ref, lse_ref, do_ref, dq_ref, dl_ref, acc, dl_s):
        ki = pl.program_id(2)

        @pl.when(ki == 0)
        def _():
            acc[...] = jnp.zeros_like(acc)
            for g in range(G):
                sl = slice(g * AT_HD, (g + 1) * AT_HD)
                dl_s[g] = jnp.sum(do_ref[:, sl] * o_ref[:, sl], axis=1, keepdims=True)

        kv, vv = k_ref[...], v_ref[...]
        for g in range(G):
            sl = slice(g * AT_HD, (g + 1) * AT_HD)
            s = _dot(q_ref[:, sl], kv, ((1,), (1,)))
            p = jnp.exp(s - lse_ref[0, :, g:g + 1])
            dp = _dot(do_ref[:, sl], vv, ((1,), (1,)))
            ds = p * (dp - dl_s[g])
            acc[g] += _dot(ds, kv, ((1,), (0,)))

        @pl.when(ki == nk - 1)
        def _():
            for g in range(G):
                dq_ref[:, g * AT_HD:(g + 1) * AT_HD] = acc[g]
                dl_ref[0, :, g:g + 1] = dl_s[g]

    qspec = pl.BlockSpec((tq, G * AT_HD), lambda h, i, j: (i, h))
    kspec = pl.BlockSpec((tk, AT_HD), lambda h, i, j: (j, h))
    lspec = pl.BlockSpec((1, tq, G), lambda h, i, j: (h, i, 0))
    return pl.pallas_call(
        body, name="flash_bwd_dq", grid=(AT_KV, S // tq, nk),
        in_specs=[qspec, kspec, pl.BlockSpec((tk, AT_HD), lambda h, i, j: (j, v_col0 + h)), qspec, lspec, qspec],
        out_specs=[qspec, lspec],
        out_shape=[jax.ShapeDtypeStruct((S, AT_QW), F32), jax.ShapeDtypeStruct((AT_KV, S, G), F32)],
        scratch_shapes=[pltpu.VMEM((G, tq, AT_HD), F32), pltpu.VMEM((G, tq, 1), F32)],
        compiler_params=_cp("parallel", "parallel", "arbitrary"),
    )(q, k, v, o, lse, do)


def flash_bwd_dkv(q, k, v, lse, delta, do, *, v_col0=0, tq=512, tk=512):
    S = q.shape[0]
    tq, tk = min(tq, S), min(tk, S)
    nq = S // tq
    G = AT_HEADS // AT_KV

    def body(q_ref, k_ref, v_ref, lse_ref, dl_ref, do_ref, dk_ref, dv_ref, dk_acc, dv_acc):
        qi = pl.program_id(2)

        @pl.when(qi == 0)
        def _():
            dk_acc[...] = jnp.zeros_like(dk_acc)
            dv_acc[...] = jnp.zeros_like(dv_acc)

        kv, vv = k_ref[...], v_ref[...]
        for g in range(G):
            sl = slice(g * AT_HD, (g + 1) * AT_HD)
            qg, dog = q_ref[:, sl], do_ref[:, sl]
            s = _dot(qg, kv, ((1,), (1,)))
            p = jnp.exp(s - lse_ref[0, :, g:g + 1])
            dv_acc[...] += _dot(p, dog, ((0,), (0,)))
            dp = _dot(dog, vv, ((1,), (1,)))
            ds = p * (dp - dl_ref[0, :, g:g + 1])
            dk_acc[...] += _dot(ds, qg, ((0,), (0,)))

        @pl.when(qi == nq - 1)
        def _():
            dk_ref[...] = dk_acc[...]
            dv_ref[...] = dv_acc[...]

    qspec = pl.BlockSpec((tq, G * AT_HD), lambda h, j, i: (i, h))
    kspec = pl.BlockSpec((tk, AT_HD), lambda h, j, i: (j, h))
    lspec = pl.BlockSpec((1, tq, G), lambda h, j, i: (h, i, 0))
    return pl.pallas_call(
        body, name="flash_bwd_dkv", grid=(AT_KV, S // tk, nq),
        in_specs=[qspec, kspec, pl.BlockSpec((tk, AT_HD), lambda h, j, i: (j, v_col0 + h)), lspec, lspec, qspec],
        out_specs=[kspec, kspec],
        out_shape=[jax.ShapeDtypeStruct((S, AT_KW), F32), jax.ShapeDtypeStruct((S, AT_KW), F32)],
        scratch_shapes=[pltpu.VMEM((tk, AT_HD), F32), pltpu.VMEM((tk, AT_HD), F32)],
        compiler_params=_cp("parallel", "parallel", "arbitrary"),
    )(q, k, v, lse, delta, do)


def flash_fwd(q, k, v, *, v_col0=0, tq=256):
    S = q.shape[0]
    tq = min(tq, S)
    G = AT_HEADS // AT_KV

    def body(q_ref, k_ref, v_ref, o_ref, lse_ref):
        kv, vv = k_ref[...], v_ref[...]
        for g in range(G):
            sl = slice(g * AT_HD, (g + 1) * AT_HD)
            s = _dot(q_ref[:, sl], kv, ((1,), (1,)))
            m = jnp.max(s, axis=1, keepdims=True)
            p = jnp.exp(s - m)
            l = jnp.sum(p, axis=1, keepdims=True)
            o_ref[:, sl] = _dot(p, vv, ((1,), (0,))) / l
            lse_ref[0, :, g:g + 1] = m + jnp.log(l)

    return pl.pallas_call(
        body, name="flash_fwd", grid=(AT_KV, S // tq),
        in_specs=[pl.BlockSpec((tq, G * AT_HD), lambda h, i: (i, h)),
                  pl.BlockSpec((S, AT_HD), lambda h, i: (0, h)),
                  pl.BlockSpec((S, AT_HD), lambda h, i: (0, v_col0 + h))],
        out_specs=[pl.BlockSpec((tq, G * AT_HD), lambda h, i: (i, h)),
                   pl.BlockSpec((1, tq, G), lambda h, i: (h, i, 0))],
        out_shape=[jax.ShapeDtypeStruct((S, AT_QW), F32), jax.ShapeDtypeStruct((AT_KV, S, G), F32)],
        compiler_params=_cp("parallel", "arbitrary"),
    )(q, k, v)


def flash_bwd_dq(q, k, v, o, lse, do, *, v_col0=0, tq=256):
    S = q.shape[0]
    tq = min(tq, S)
    G = AT_HEADS // AT_KV

    def body(q_ref, k_ref, v_ref, o_ref, lse_ref, do_ref, dq_ref, dl_ref):
        kv, vv = k_ref[...], v_ref[...]
        for g in range(G):
            sl = slice(g * AT_HD, (g + 1) * AT_HD)
            dog = do_ref[:, sl]
            delta = jnp.sum(dog * o_ref[:, sl], axis=1, keepdims=True)
            s = _dot(q_ref[:, sl], kv, ((1,), (1,)))
            p = jnp.exp(s - lse_ref[0, :, g:g + 1])
            dp = _dot(dog, vv, ((1,), (1,)))
            ds = p * (dp - delta)
            dq_ref[:, sl] = _dot(ds, kv, ((1,), (0,)))
            dl_ref[0, :, g:g + 1] = delta

    qspec = pl.BlockSpec((tq, G * AT_HD), lambda h, i: (i, h))
    lspec = pl.BlockSpec((1, tq, G), lambda h, i: (h, i, 0))
    return pl.pallas_call(
        body, name="flash_bwd_dq", grid=(AT_KV, S // tq),
        in_specs=[qspec, pl.BlockSpec((S, AT_HD), lambda h, i: (0, h)),
                  pl.BlockSpec((S, AT_HD), lambda h, i: (0, v_col0 + h)), qspec, lspec, qspec],
        out_specs=[qspec, lspec],
        out_shape=[jax.ShapeDtypeStruct((S, AT_QW), F32), jax.ShapeDtypeStruct((AT_KV, S, G), F32)],
        compiler_params=_cp("parallel", "arbitrary"),
    )(q, k, v, o, lse, do)


def flash_bwd_dkv(q, k, v, lse, delta, do, *, v_col0=0, tk=512):
    S = q.shape[0]
    tk = min(tk, S)
    G = AT_HEADS // AT_KV

    def body(q_ref, k_ref, v_ref, lse_ref, dl_ref, do_ref, dk_ref, dv_ref):
        kv, vv = k_ref[...], v_ref[...]
        dk = jnp.zeros((tk, AT_HD), F32)
        dv = jnp.zeros((tk, AT_HD), F32)
        for g in range(G):
            sl = slice(g * AT_HD, (g + 1) * AT_HD)
            qg, dog = q_ref[:, sl], do_ref[:, sl]
            s = _dot(qg, kv, ((1,), (1,)))
            p = jnp.exp(s - lse_ref[0, :, g:g + 1])
            dv = dv + _dot(p, dog, ((0,), (0,)))
            dp = _dot(dog, vv, ((1,), (1,)))
            ds = p * (dp - dl_ref[0, :, g:g + 1])
            dk = dk + _dot(ds, qg, ((0,), (0,)))
        dk_ref[...] = dk
        dv_ref[...] = dv

    qspec = pl.BlockSpec((S, G * AT_HD), lambda h, j: (0, h))
    kspec = pl.BlockSpec((tk, AT_HD), lambda h, j: (j, h))
    lspec = pl.BlockSpec((1, S, G), lambda h, j: (h, 0, 0))
    return pl.pallas_call(
        body, name="flash_bwd_dkv", grid=(AT_KV, S // tk),
        in_specs=[qspec, kspec, pl.BlockSpec((tk, AT_HD), lambda h, j: (j, v_col0 + h)), lspec, lspec, qspec],
        out_specs=[kspec, kspec],
        out_shape=[jax.ShapeDtypeStruct((S, AT_KW), F32), jax.ShapeDtypeStruct((S, AT_KW), F32)],
        compiler_params=_cp("parallel", "arbitrary"),
    )(q, k, v, lse, delta, do)


def _t5_bucket_np(rel):
    half = REL_BUCKETS // 2
    exact = half // 2
    n = np.abs(rel)
    large = exact + (np.log(np.maximum(n, 1).astype(np.float32) / np.float32(exact))
                     / np.float32(math.log(REL_MAX_DIST / exact)) * np.float32(half - exact)).astype(np.int32)
    large = np.minimum(large, half - 1)
    return np.where(rel > 0, half, 0) + np.where(n < exact, n, large)


def _dl_tiles(Ls):
    T = min(128, Ls)
    return T, T + 2 * DL_HALF


def _dl_bucket_tables(dil, T):
    W = T + 2 * DL_HALF
    i = np.arange(T)[:, None]
    j = np.arange(W)[None, :]
    bq = _t5_bucket_np((j - DL_HALF - i) * dil)
    iw = np.arange(W)[:, None]
    jk = np.arange(T)[None, :]
    bk = _t5_bucket_np((jk + DL_HALF - iw) * dil)
    return bq.astype(np.int32), bk.astype(np.int32)


def band_fwd(q, kp, vp, bias, *, scale):
    H, dil, Ls, E = q.shape
    T, W = _dl_tiles(Ls)

    def body(q_ref, k_ref, v_ref, b_ref, o_ref, lse_ref):
        n = pl.program_id(2)
        r0 = pl.multiple_of(n * T, T)
        kw = k_ref[0, 0, pl.ds(r0, W), :]
        vw = v_ref[0, 0, pl.ds(r0, W), :]
        i = lax.broadcasted_iota(jnp.int32, (T, W), 0)
        j = lax.broadcasted_iota(jnp.int32, (T, W), 1)
        kpos = n * T + j - DL_HALF
        mask = (jnp.abs(j - DL_HALF - i) <= DL_HALF) & (kpos >= 0) & (kpos < Ls)
        s = _dot(q_ref[0, 0], kw, ((1,), (1,))) * scale + b_ref[0]
        s = jnp.where(mask, s, NEG_BIG)
        m = jnp.max(s, axis=1, keepdims=True)
        lse = m + jnp.log(jnp.sum(jnp.exp(s - m), axis=1, keepdims=True))
        p = jnp.exp(s - lse)
        o_ref[0, 0] = _dot(p, vw, ((1,), (0,)))
        lse_ref[0, 0] = lse

    return pl.pallas_call(
        body, name=f"band_fwd_{dil}", grid=(H, dil, Ls // T),
        in_specs=[pl.BlockSpec((1, 1, T, E), lambda h, d, n: (h, d, n, 0)),
                  pl.BlockSpec((1, 1, Ls + 2 * DL_HALF, E), lambda h, d, n: (h, d, 0, 0)),
                  pl.BlockSpec((1, 1, Ls + 2 * DL_HALF, E), lambda h, d, n: (h, d, 0, 0)),
                  pl.BlockSpec((1, T, W), lambda h, d, n: (h, 0, 0))],
        out_specs=[pl.BlockSpec((1, 1, T, E), lambda h, d, n: (h, d, n, 0)),
                   pl.BlockSpec((1, 1, T, 1), lambda h, d, n: (h, d, n, 0))],
        out_shape=[jax.ShapeDtypeStruct((H, dil, Ls, E), F32), jax.ShapeDtypeStruct((H, dil, Ls, 1), F32)],
        compiler_params=_cp("parallel", "parallel", "arbitrary"),
    )(q, kp, vp, bias)


def band_bwd_dq(q, kp, vp, bias, lse, dm, do, *, scale):
    H, dil, Ls, E = q.shape
    T, W = _dl_tiles(Ls)

    def body(q_ref, k_ref, v_ref, b_ref, lse_ref, dm_ref, do_ref, dq_ref, db_ref):
        d, n = pl.program_id(1), pl.program_id(2)
        r0 = pl.multiple_of(n * T, T)
        kw = k_ref[0, 0, pl.ds(r0, W), :]
        vw = v_ref[0, 0, pl.ds(r0, W), :]
        i = lax.broadcasted_iota(jnp.int32, (T, W), 0)
        j = lax.broadcasted_iota(jnp.int32, (T, W), 1)
        kpos = n * T + j - DL_HALF
        mask = (jnp.abs(j - DL_HALF - i) <= DL_HALF) & (kpos >= 0) & (kpos < Ls)
        s = _dot(q_ref[0, 0], kw, ((1,), (1,))) * scale + b_ref[0]
        p = jnp.where(mask, jnp.exp(jnp.where(mask, s, 0.0) - lse_ref[0, 0]), 0.0)
        dp = _dot(do_ref[0, 0], vw, ((1,), (1,)))
        ds = p * (dp - dm_ref[0, 0])
        dq_ref[0, 0] = _dot(ds, kw, ((1,), (0,))) * scale
        first = jnp.logical_and(d == 0, n == 0)

        @pl.when(first)
        def _():
            db_ref[0] = ds

        @pl.when(jnp.logical_not(first))
        def _():
            db_ref[0] += ds

    qspec = pl.BlockSpec((1, 1, T, E), lambda h, d, n: (h, d, n, 0))
    kspec = pl.BlockSpec((1, 1, Ls + 2 * DL_HALF, E), lambda h, d, n: (h, d, 0, 0))
    rspec = pl.BlockSpec((1, 1, T, 1), lambda h, d, n: (h, d, n, 0))
    bspec = pl.BlockSpec((1, T, W), lambda h, d, n: (h, 0, 0))
    return pl.pallas_call(
        body, name=f"band_bwd_dq_{dil}", grid=(H, dil, Ls // T),
        in_specs=[qspec, kspec, kspec, bspec, rspec, rspec, qspec],
        out_specs=[qspec, bspec],
        out_shape=[jax.ShapeDtypeStruct((H, dil, Ls, E), F32), jax.ShapeDtypeStruct((H, T, W), F32)],
        compiler_params=_cp("arbitrary", "arbitrary", "arbitrary"),
    )(q, kp, vp, bias, lse, dm, do)


def band_bwd_dkv(qp, k, v, bias_t, lsep, dmp, dop, *, scale):
    H, dil, Ls, E = k.shape
    T, W = _dl_tiles(Ls)

    def body(q_ref, k_ref, v_ref, b_ref, lse_ref, dm_ref, do_ref, dk_ref, dv_ref):
        n = pl.program_id(2)
        r0 = pl.multiple_of(n * T, T)
        qw = q_ref[0, 0, pl.ds(r0, W), :]
        dow = do_ref[0, 0, pl.ds(r0, W), :]
        lsew = lse_ref[0, 0, pl.ds(r0, W), :]
        dmw = dm_ref[0, 0, pl.ds(r0, W), :]
        iw = lax.broadcasted_iota(jnp.int32, (W, T), 0)
        j = lax.broadcasted_iota(jnp.int32, (W, T), 1)
        qpos = n * T + iw - DL_HALF
        mask = (jnp.abs(j + DL_HALF - iw) <= DL_HALF) & (qpos >= 0) & (qpos < Ls)
        s = _dot(qw, k_ref[0, 0], ((1,), (1,))) * scale + b_ref[0]
        p = jnp.where(mask, jnp.exp(jnp.where(mask, s, 0.0) - lsew), 0.0)
        dv_ref[0, 0] = _dot(p, dow, ((0,), (0,)))
        dp = _dot(dow, v_ref[0, 0], ((1,), (1,)))
        ds = p * (dp - dmw)
        dk_ref[0, 0] = _dot(ds, qw, ((0,), (0,))) * scale

    kspec = pl.BlockSpec((1, 1, T, E), lambda h, d, n: (h, d, n, 0))
    wspec = pl.BlockSpec((1, 1, Ls + 2 * DL_HALF, E), lambda h, d, n: (h, d, 0, 0))
    w1spec = pl.BlockSpec((1, 1, Ls + 2 * DL_HALF, 1), lambda h, d, n: (h, d, 0, 0))
    return pl.pallas_call(
        body, name=f"band_bwd_dkv_{dil}", grid=(H, dil, Ls // T),
        in_specs=[wspec, kspec, kspec, pl.BlockSpec((1, W, T), lambda h, d, n: (h, 0, 0)), w1spec, w1spec, wspec],
        out_specs=[kspec, kspec],
        out_shape=[jax.ShapeDtypeStruct((H, dil, Ls, E), F32), jax.ShapeDtypeStruct((H, dil, Ls, E), F32)],
        compiler_params=_cp("parallel", "parallel", "arbitrary"),
    )(qp, k, v, bias_t, lsep, dmp, dop)


def _dl_merge_fn(o0, o1, o2, l0, l1, l2):
    m = jnp.maximum(jnp.maximum(l0, l1), l2)
    e0, e1, e2 = jnp.exp(l0 - m), jnp.exp(l1 - m), jnp.exp(l2 - m)
    den = e0 + e1 + e2
    return ((e0 / den) * o0 + (e1 / den) * o1 + (e2 / den) * o2,)


def _adamw_math(w, g, m, v):
    m = ADAM_B1 * m + (1.0 - ADAM_B1) * g
    v = ADAM_B2 * v + (1.0 - ADAM_B2) * (g * g)
    m_hat = m / (1.0 - ADAM_B1 ** ADAM_STEP)
    v_hat = v / (1.0 - ADAM_B2 ** ADAM_STEP)
    delta = -ADAM_LR * (m_hat / (jnp.sqrt(v_hat) + ADAM_EPS) + ADAM_WD * w)
    return delta, m, v


def adamw_sum(parts, w, m, v, *, name, R=128):
    rows, cols = w.shape
    R = min(R, rows)
    if rows % R:
        R = rows

    def body(p_ref, w_ref, m_ref, v_ref, g_ref, d_ref, nm_ref, nv_ref):
        g = p_ref[0].astype(F32)
        for s in range(1, N_DEV):
            g = g + p_ref[s].astype(F32)
        d, nm, nv = _adamw_math(w_ref[...], g, m_ref[...], v_ref[...])
        g_ref[...] = g
        d_ref[...] = d
        nm_ref[...] = nm
        nv_ref[...] = nv

    spec = pl.BlockSpec((R, cols), lambda i: (i, 0))
    return pl.pallas_call(
        body, name=name, grid=(rows // R,),
        in_specs=[pl.BlockSpec((N_DEV, R, cols), lambda i: (0, i, 0)), spec, spec, spec],
        out_specs=[spec] * 4, out_shape=[jax.ShapeDtypeStruct((rows, cols), F32)] * 4,
        compiler_params=_cp("parallel"),
    )(parts, w, m, v)


def sum_parts(parts, *, name):
    rows, cols = parts.shape[1:]

    def body(p_ref, o_ref):
        g = p_ref[0]
        for s in range(1, N_DEV):
            g = g + p_ref[s]
        o_ref[...] = g

    return pl.pallas_call(body, name=name, out_shape=jax.ShapeDtypeStruct((rows, cols), F32))(parts)


def adamw_plain(w, g, m, v, *, name):
    def body(w_ref, g_ref, m_ref, v_ref, d_ref, nm_ref, nv_ref):
        d, nm, nv = _adamw_math(w_ref[...], g_ref[...], m_ref[...], v_ref[...])
        d_ref[...] = d
        nm_ref[...] = nm
        nv_ref[...] = nv

    return pl.pallas_call(body, name=name, out_shape=[jax.ShapeDtypeStruct(w.shape, F32)] * 3)(w, g, m, v)


def _my_pos():
    return lax.axis_index("x"), lax.axis_index("y"), lax.axis_index("c")


def _flat(px, py, pc):
    return 4 * px + 2 * py + pc


def allgather_two_level(x, *, name):
    R, C = x.shape

    def body(x_ref, out_ref, send_sems, recv_sems, local_sem):
        x_, y_, c_ = _my_pos()
        me, sibling = (x_, y_, c_), (x_, y_, 1 - c_)
        chips = [(1 - x_, y_), (x_, 1 - y_), (1 - x_, 1 - y_)]

        def rows(p):
            return out_ref.at[_flat(*p)]

        def copy(k, block, to, src=None):
            return pltpu.make_async_remote_copy(
                src_ref=rows(block) if src is None else src, dst_ref=rows(block),
                send_sem=send_sems.at[k], recv_sem=recv_sems.at[k], device_id=to, device_id_type=MESH_ID)

        mine = pltpu.make_async_copy(x_ref, rows(me), local_sem)
        mine.start()
        first = [copy(0, me, sibling, src=x_ref)]
        first += [copy(1 + j, me, (*chip, c_), src=x_ref) for j, chip in enumerate(chips)]
        for cp in first:
            cp.start()
        passed = [copy(4 + j, (*chip, c_), sibling) for j, chip in enumerate(chips)]
        for j, chip in enumerate(chips):
            copy(1 + j, (*chip, c_), me).wait_recv()
            passed[j].start()
        copy(0, sibling, me).wait_recv()
        for j, chip in enumerate(chips):
            copy(4 + j, (*chip, 1 - c_), me).wait_recv()
        for cp in first + passed:
            cp.wait_send()
        mine.wait()

    return pl.pallas_call(
        body, name=name,
        out_shape=jax.ShapeDtypeStruct((N_DEV, R, C), x.dtype),
        in_specs=[pl.BlockSpec(memory_space=pl.ANY)],
        out_specs=pl.BlockSpec(memory_space=pl.ANY),
        scratch_shapes=[pltpu.SemaphoreType.DMA((7,)), pltpu.SemaphoreType.DMA((7,)), pltpu.SemaphoreType.DMA],
    )(x)


def all_to_all(bufs, *, name):
    nb = len(bufs)

    def body(*refs):
        in_refs = refs[:nb]
        out_refs = refs[nb:2 * nb]
        send_sems, recv_sems, local_sems = refs[2 * nb:]
        x_, y_, c_ = _my_pos()
        me = _flat(x_, y_, c_)
        peers = []
        for k in range(1, N_DEV):
            fx, fy, fc = (k >> 2) & 1, (k >> 1) & 1, k & 1
            peers.append(((1 - x_) if fx else x_, (1 - y_) if fy else y_, (1 - c_) if fc else c_))
        copies = []
        for b in range(nb):
            loc = pltpu.make_async_copy(in_refs[b].at[me], out_refs[b].at[me], local_sems.at[b])
            loc.start()
            copies.append(loc)
        remote = []
        for b in range(nb):
            for k, p in enumerate(peers):
                cp = pltpu.make_async_remote_copy(
                    src_ref=in_refs[b].at[_flat(*p)], dst_ref=out_refs[b].at[me],
                    send_sem=send_sems.at[b, k], recv_sem=recv_sems.at[b, k], device_id=p, device_id_type=MESH_ID)
                cp.start()
                remote.append((b, k, p))
        for b, k, p in remote:
            pltpu.make_async_remote_copy(
                src_ref=in_refs[b].at[me], dst_ref=out_refs[b].at[_flat(*p)],
                send_sem=send_sems.at[b, k], recv_sem=recv_sems.at[b, k], device_id=p, device_id_type=MESH_ID).wait()
        for loc in copies:
            loc.wait()

    return pl.pallas_call(
        body, name=name,
        out_shape=[jax.ShapeDtypeStruct(b.shape, b.dtype) for b in bufs],
        in_specs=[pl.BlockSpec(memory_space=pl.ANY)] * nb,
        out_specs=[pl.BlockSpec(memory_space=pl.ANY)] * nb,
        scratch_shapes=[pltpu.SemaphoreType.DMA((nb, 7)), pltpu.SemaphoreType.DMA((nb, 7)), pltpu.SemaphoreType.DMA((nb,))],
    )(*bufs)


def _prenorm(tag, x, ng):
    return rowwise_fwd(f"{tag}_prenorm", _prenorm_fn, [(x, 0, False)], [(ng, False)], [(D_MODEL, MXU_DTYPE)], W=D_MODEL)[0]


def _in_out_bwd(tag, x, ng, hn, du, w_in, dx):
    dhn = matmul(du, w_in, tb=True, name=f"{tag}_dhn")
    dw_in = matmul(hn, du, ta=True, out_dtype=GRAD_WIRE_DTYPE, name=f"{tag}_dw_in")
    dx_prev, dng = rowwise_bwd(f"{tag}_prenorm_bwd", _prenorm_fn, [(x, 0, False)], [(ng, False)], [dhn],
                               W=D_MODEL, diff_rows=[0], diff_shared=[0], add=dx)
    return dx_prev, dng, dw_in


def ssd_layer_fwd(x, ng, p):
    hn = _prenorm("ssd", x, ng)
    u = matmul(hn, p["w_in"], name="ssd_in")
    act = ssd_conv_fwd(u, p["conv_w"], p["conv_b"])
    dt = rowwise_fwd("ssd_dt", _dt_fn, [(u, (SSD_DI + SSD_CONV_CH) // 128, False)], [(p["dt_bias"], False)],
                     [(128, F32)], W=128)[0]
    dt, alog = _ssd_group_layout(dt), _ssd_group_layout(p["alog"])
    y0, st0 = ssd_scan_fwd(act, dt, alog, reverse=False)
    y, st1 = ssd_scan_fwd(act, dt, alog, reverse=True, y_prev=y0)
    g = rowwise_fwd("ssd_post", _ssd_post_fn, [(y, 0, True), (act, 0, True), (u, 0, True)],
                    [(p["dexp"], True), (p["norm_g"], True)], [(512, MXU_DTYPE)], W=512, ncb=SSD_GROUPS)[0]
    xn = matmul(g, p["w_out"], residual=x, name="ssd_out")
    return xn, dict(x=x, ng=ng, hn=hn, u=u, act=act, dt=dt, alog=alog, y=y, st0=st0, st1=st1, g=g)


def ssd_layer_bwd(sv, p, dx):
    u, act, dt = sv["u"], sv["act"], sv["dt"]
    S = u.shape[0]
    dg = matmul(dx, p["w_out"], tb=True, name="ssd_dg")
    dw_out = matmul(sv["g"], dx, ta=True, out_dtype=GRAD_WIRE_DTYPE, name="ssd_dw_out")
    dy, dxs_skip, dz, ddexp, dnorm = rowwise_bwd(
        "ssd_post_bwd", _ssd_post_fn, [(sv["y"], 0, True), (act, 0, True), (u, 0, True)],
        [(p["dexp"], True), (p["norm_g"], True)], [dg], W=512, ncb=SSD_GROUPS, diff_rows=[0, 1, 2], diff_shared=[0, 1])
    dxa, dB, dC, ddt, dal = ssd_scan_bwd(act, dt, sv["alog"], sv["st0"], dy, dxs_skip, reverse=False)
    dxa, dB, dC, ddt, dal = ssd_scan_bwd(act, dt, sv["alog"], sv["st1"], dy, dxa, reverse=True, prev=(dB, dC, ddt, dal))
    dact = jnp.concatenate([dxa, dB, dC], axis=1)
    dxbc, dconv_w, dconv_b = ssd_conv_bwd(u, p["conv_w"], p["conv_b"], dact)
    ddt_all = _ssd_head_layout(ddt)
    ddt_raw, ddt_bias = rowwise_bwd("ssd_dt_bwd", _dt_fn, [(u, (SSD_DI + SSD_CONV_CH) // 128, False)],
                                    [(p["dt_bias"], False)], [ddt_all], W=128, diff_rows=[0], diff_shared=[0])
    du = jnp.concatenate([dz, dxbc, ddt_raw, jnp.zeros((S, SSD_IN_PAD - SSD_IN - 64), F32)], axis=1)
    dx_prev, dng, dw_in = _in_out_bwd("ssd", sv["x"], sv["ng"], sv["hn"], du, p["w_in"], dx)
    grads = dict(
        w_in=dw_in[:, :SSD_IN], w_out=dw_out, conv_w=dconv_w[:SSD_CONV], conv_b=dconv_b,
        dt_bias=ddt_bias[:, :2 * SSD_HEADS], a_log=_ssd_head_layout(dal)[:, :2 * SSD_HEADS],
        d=ddexp.reshape(SSD_HEADS, SSD_HEADDIM).sum(axis=1)[None, :], norm_g=dnorm, ng=dng)
    return dx_prev, grads


def hg_layer_fwd(x, ng, p):
    hn = _prenorm("hg", x, ng)
    u = matmul(hn, p["w_in"], name="hg_in")
    lb = hg_lb_fwd(p["hgrn_lb"])
    o0, st0 = hg_scan_fwd(u, lb, reverse=False)
    o, st1 = hg_scan_fwd(u, lb, reverse=True, o_prev=o0)
    g = rowwise_fwd("hg_post", _hg_post_fn, [(o, 0, True), (u, 4 * HG_HEADS, True)], [(p["norm_g"], True)],
                    [(HG_D, MXU_DTYPE)], W=HG_D, ncb=HG_HEADS)[0]
    xn = matmul(g, p["w_out"], residual=x, name="hg_out")
    return xn, dict(x=x, ng=ng, hn=hn, u=u, lb=lb, o=o, st0=st0, st1=st1, g=g)


def hg_layer_bwd(sv, p, dx):
    u, lb = sv["u"], sv["lb"]
    dg = matmul(dx, p["w_out"], tb=True, name="hg_dg")
    dw_out = matmul(sv["g"], dx, ta=True, out_dtype=GRAD_WIRE_DTYPE, name="hg_dw_out")
    do, dgate, dnorm = rowwise_bwd("hg_post_bwd", _hg_post_fn, [(sv["o"], 0, True), (u, 4 * HG_HEADS, True)],
                                   [(p["norm_g"], True)], [dg], W=HG_D, ncb=HG_HEADS, diff_rows=[0, 1], diff_shared=[0])
    dq0, df0, dv0, dlb0 = hg_scan_bwd(u, lb, sv["st0"], do, reverse=False)
    dq, df1, dv, dlb = hg_scan_bwd(u, lb, sv["st1"], do, reverse=True, prev=(dq0, dv0, dlb0))
    du = jnp.concatenate([dq, df0, df1, dv, dgate], axis=1)
    dhgrn_lb = hg_lb_bwd(p["hgrn_lb"], dlb)
    dx_prev, dng, dw_in = _in_out_bwd("hg", sv["x"], sv["ng"], sv["hn"], du, p["w_in"], dx)
    return dx_prev, dict(w_in=dw_in, w_out=dw_out, norm_g=dnorm, hgrn_lb=dhgrn_lb, ng=dng)


def _rope_consts(S):
    ar, ac = _rope_tables(S)
    ct = np.concatenate([np.cos(ar), np.cos(ar), np.cos(ac), np.cos(ac)], axis=1).astype(np.float32)
    st = np.concatenate([-np.sin(ar), np.sin(ar), -np.sin(ac), np.sin(ac)], axis=1).astype(np.float32)
    return jnp.asarray(ct), jnp.asarray(st)


def _at_qk(tag, u, col0, nheads, scale, gain, consts, cot=None):
    ct, st = consts
    rows = [(u, col0, True), (ct, 0, False), (st, 0, False)]
    shared = [(gain, False)]
    if cot is None:
        return rowwise_fwd(f"at_{tag}", _make_qk_fn(scale), rows, shared, [(AT_HD, MXU_DTYPE)], W=AT_HD, ncb=nheads)[0]
    return rowwise_bwd(f"at_{tag}_bwd", _make_qk_fn(scale), rows, shared, [cot], W=AT_HD, ncb=nheads,
                       diff_rows=[0], diff_shared=[0])


def at_layer_fwd(x, ng, p):
    S = x.shape[0]
    hn = _prenorm("at", x, ng)
    u = matmul(hn, p["w_in"], name="at_in")
    consts = _rope_consts(S)
    qr = _at_qk("q", u, 0, AT_HEADS, AT_HD ** -0.5, p["q_g"], consts)
    kr = _at_qk("k", u, AT_HEADS, AT_KV, 1.0, p["k_g"], consts)
    vc0 = (AT_QW + AT_KW) // AT_HD
    o, lse = flash_fwd(qr, kr, u, v_col0=vc0)
    g = rowwise_fwd("at_gate", _gate_fn, [(o, 0, True), (u, (AT_QW + 2 * AT_KW) // 1024, True)], [],
                    [(1024, MXU_DTYPE)], W=1024, ncb=AT_QW // 1024)[0]
    xn = matmul(g, p["w_out"], residual=x, name="at_out")
    return xn, dict(x=x, ng=ng, hn=hn, u=u, qr=qr, kr=kr, o=o, lse=lse, g=g)


def at_layer_bwd(sv, p, dx):
    u, qr, kr = sv["u"], sv["qr"], sv["kr"]
    S = u.shape[0]
    consts = _rope_consts(S)
    vc0 = (AT_QW + AT_KW) // AT_HD
    dg = matmul(dx, p["w_out"], tb=True, name="at_dg")
    dw_out = matmul(sv["g"], dx, ta=True, out_dtype=GRAD_WIRE_DTYPE, name="at_dw_out")
    do, dgate = rowwise_bwd("at_gate_bwd", _gate_fn, [(sv["o"], 0, True), (u, (AT_QW + 2 * AT_KW) // 1024, True)], [],
                            [dg], W=1024, ncb=AT_QW // 1024, diff_rows=[0, 1], diff_shared=[])
    dqs, delta = flash_bwd_dq(qr, kr, u, sv["o"], sv["lse"], do, v_col0=vc0)
    dkr, dv = flash_bwd_dkv(qr, kr, u, sv["lse"], delta, do, v_col0=vc0)
    dq_raw, dqg = _at_qk("q", u, 0, AT_HEADS, AT_HD ** -0.5, p["q_g"], consts, cot=dqs)
    dk_raw, dkg = _at_qk("k", u, AT_HEADS, AT_KV, 1.0, p["k_g"], consts, cot=dkr)
    du = jnp.concatenate([dq_raw, dk_raw, dv, dgate], axis=1)
    dx_prev, dng, dw_in = _in_out_bwd("at", sv["x"], sv["ng"], sv["hn"], du, p["w_in"], dx)
    return dx_prev, dict(w_in=dw_in, w_out=dw_out, q_g=dqg, k_g=dkg, ng=dng)


def _to_stream(t, dil):
    S = t.shape[0]
    return t.reshape(S // dil, dil, DL_HEADS, DL_HD).transpose(2, 1, 0, 3)


def _from_stream(t):
    H, dil, Ls, E = t.shape
    return t.transpose(2, 1, 0, 3).reshape(Ls * dil, H * E)


def _stream_to_hm(t):
    H, dil, Ls, w = t.shape
    return t.transpose(0, 2, 1, 3).reshape(H * Ls * dil, w)


def _hm_to_stream(t, dil):
    w = t.shape[1]
    S = t.shape[0] // DL_HEADS
    return t.reshape(DL_HEADS, S // dil, dil, w).transpose(0, 2, 1, 3)


def _pad_l(t):
    return jnp.pad(t, ((0, 0), (0, 0), (DL_HALF, DL_HALF), (0, 0)))


OX_LSE = DL_HD
DOX_LSE, DOX_DM = DL_HD, DL_HD + 32


def _win(p_ref, c_ref, n_ref, h, T):
    return jnp.concatenate([p_ref[h, 0, T - DL_HALF:T, :], c_ref[h, 0], n_ref[h, 0, 0:DL_HALF, :]], axis=0)


def _win_specs(T, E, nb):
    return [pl.BlockSpec((DL_HEADS, 1, T, E), lambda d, n: (0, d, jnp.maximum(n - 1, 0), 0)),
            pl.BlockSpec((DL_HEADS, 1, T, E), lambda d, n: (0, d, n, 0)),
            pl.BlockSpec((DL_HEADS, 1, T, E), lambda d, n: (0, d, jnp.minimum(n + 1, nb - 1), 0))]


def _band_mask_q(n, T, W, Ls):
    i = lax.broadcasted_iota(jnp.int32, (T, W), 0)
    j = lax.broadcasted_iota(jnp.int32, (T, W), 1)
    kpos = n * T + j - DL_HALF
    return (jnp.abs(j - DL_HALF - i) <= DL_HALF) & (kpos >= 0) & (kpos < Ls)


def band_fwd(q, k, v, bias, *, scale):
    H, dil, Ls, E = q.shape
    T, W = _dl_tiles(Ls)
    nb = Ls // T

    def body(q_ref, kp_ref, kc_ref, kn_ref, vp_ref, vc_ref, vn_ref, b_ref, ox_ref):
        n = pl.program_id(1)
        mask = _band_mask_q(n, T, W, Ls)
        for h in range(H):
            kw = _win(kp_ref, kc_ref, kn_ref, h, T)
            vw = _win(vp_ref, vc_ref, vn_ref, h, T)
            s = _dot(q_ref[h, 0], kw, ((1,), (1,))) * scale + b_ref[h]
            s = jnp.where(mask, s, NEG_BIG)
            m = jnp.max(s, axis=1, keepdims=True)
            lse = m + jnp.log(jnp.sum(jnp.exp(s - m), axis=1, keepdims=True))
            p = jnp.exp(s - lse)
            ox_ref[h, 0, :, 0:E] = _dot(p, vw, ((1,), (0,)))
            ox_ref[h, 0, :, E:2 * E] = lse + jnp.zeros((T, E), F32)

    cur = pl.BlockSpec((H, 1, T, E), lambda d, n: (0, d, n, 0))
    return pl.pallas_call(
        body, name=f"band_fwd_{dil}", grid=(dil, nb),
        in_specs=[cur] + _win_specs(T, E, nb) + _win_specs(T, E, nb) + [pl.BlockSpec((H, T, W), lambda d, n: (0, 0, 0))],
        out_specs=pl.BlockSpec((H, 1, T, 2 * E), lambda d, n: (0, d, n, 0)),
        out_shape=jax.ShapeDtypeStruct((H, dil, Ls, 2 * E), F32),
        compiler_params=_cp("parallel", "parallel"),
    )(q, k, k, k, v, v, v, bias)


def band_bwd_dq(q, k, v, bias, dox, *, scale):
    H, dil, Ls, E = q.shape
    T, W = _dl_tiles(Ls)
    nb = Ls // T

    def body(q_ref, kp_ref, kc_ref, kn_ref, vp_ref, vc_ref, vn_ref, b_ref, dox_ref, dq_ref, db_ref):
        d, n = pl.program_id(0), pl.program_id(1)
        mask = _band_mask_q(n, T, W, Ls)
        first = jnp.logical_and(d == 0, n == 0)

        @pl.when(first)
        def _():
            db_ref[...] = jnp.zeros_like(db_ref)

        for h in range(H):
            kw = _win(kp_ref, kc_ref, kn_ref, h, T)
            vw = _win(vp_ref, vc_ref, vn_ref, h, T)
            dox = dox_ref[h, 0]
            do, lse, dm = dox[:, 0:E], dox[:, DOX_LSE:DOX_LSE + 1], dox[:, DOX_DM:DOX_DM + 1]
            s = _dot(q_ref[h, 0], kw, ((1,), (1,))) * scale + b_ref[h]
            p = jnp.where(mask, jnp.exp(jnp.where(mask, s, 0.0) - lse), 0.0)
            dp = _dot(do, vw, ((1,), (1,)))
            ds = p * (dp - dm)
            dq_ref[h, 0] = _dot(ds, kw, ((1,), (0,))) * scale
            db_ref[h] += ds

    cur = pl.BlockSpec((H, 1, T, E), lambda d, n: (0, d, n, 0))
    bspec = pl.BlockSpec((H, T, W), lambda d, n: (0, 0, 0))
    return pl.pallas_call(
        body, name=f"band_bwd_dq_{dil}", grid=(dil, nb),
        in_specs=[cur] + _win_specs(T, E, nb) + _win_specs(T, E, nb) + [bspec,
                  pl.BlockSpec((H, 1, T, 2 * E), lambda d, n: (0, d, n, 0))],
        out_specs=[cur, bspec],
        out_shape=[jax.ShapeDtypeStruct((H, dil, Ls, E), F32), jax.ShapeDtypeStruct((H, T, W), F32)],
        compiler_params=_cp("arbitrary", "arbitrary"),
    )(q, k, k, k, v, v, v, bias, dox)


def band_bwd_dkv(q, k, v, bias_t, dox, *, scale):
    H, dil, Ls, E = k.shape
    T, W = _dl_tiles(Ls)
    nb = Ls // T

    def body(qp_ref, qc_ref, qn_ref, k_ref, v_ref, b_ref, dp_ref, dc_ref, dn_ref, dk_ref, dv_ref):
        n = pl.program_id(1)
        iw = lax.broadcasted_iota(jnp.int32, (W, T), 0)
        j = lax.broadcasted_iota(jnp.int32, (W, T), 1)
        qpos = n * T + iw - DL_HALF
        mask = (jnp.abs(j + DL_HALF - iw) <= DL_HALF) & (qpos >= 0) & (qpos < Ls)
        for h in range(H):
            qw = _win(qp_ref, qc_ref, qn_ref, h, T)
            doxw = _win(dp_ref, dc_ref, dn_ref, h, T)
            dow, lsew, dmw = doxw[:, 0:E], doxw[:, DOX_LSE:DOX_LSE + 1], doxw[:, DOX_DM:DOX_DM + 1]
            s = _dot(qw, k_ref[h, 0], ((1,), (1,))) * scale + b_ref[h]
            p = jnp.where(mask, jnp.exp(jnp.where(mask, s, 0.0) - lsew), 0.0)
            dv_ref[h, 0] = _dot(p, dow, ((0,), (0,)))
            dp = _dot(dow, v_ref[h, 0], ((1,), (1,)))
            ds = p * (dp - dmw)
            dk_ref[h, 0] = _dot(ds, qw, ((0,), (0,))) * scale

    cur = pl.BlockSpec((H, 1, T, E), lambda d, n: (0, d, n, 0))
    return pl.pallas_call(
        body, name=f"band_bwd_dkv_{dil}", grid=(dil, nb),
        in_specs=_win_specs(T, E, nb) + [cur, cur, pl.BlockSpec((H, W, T), lambda d, n: (0, 0, 0))]
        + _win_specs(T, 2 * E, nb),
        out_specs=[cur, cur],
        out_shape=[jax.ShapeDtypeStruct((H, dil, Ls, E), F32), jax.ShapeDtypeStruct((H, dil, Ls, E), F32)],
        compiler_params=_cp("parallel", "parallel"),
    )(q, q, q, k, v, bias_t, dox, dox, dox)


def dl_merge_fwd(oxs, *, R=1024):
    rows = oxs[0].shape[0]
    R = min(R, rows)
    E = DL_HD

    def body(a_ref, b_ref, c_ref, o_ref):
        vals = [r[...] for r in (a_ref, b_ref, c_ref)]
        o_ref[...] = _dl_merge_fn(*[t[:, 0:E] for t in vals], *[t[:, OX_LSE:OX_LSE + 1] for t in vals])[0]

    spec = pl.BlockSpec((R, 2 * E), lambda i: (i, 0))
    return pl.pallas_call(
        body, name="dl_merge", grid=(rows // R,), in_specs=[spec] * 3,
        out_specs=pl.BlockSpec((R, E), lambda i: (i, 0)), out_shape=jax.ShapeDtypeStruct((rows, E), F32),
        compiler_params=_cp("parallel"),
    )(*oxs)


def dl_merge_bwd(oxs, do, *, R=1024):
    rows = oxs[0].shape[0]
    R = min(R, rows)
    E = DL_HD

    def body(a_ref, b_ref, c_ref, do_ref, da_ref, db_ref, dc_ref):
        vals = [r[...] for r in (a_ref, b_ref, c_ref)]
        os_ = [t[:, 0:E] for t in vals]
        ls_ = [t[:, OX_LSE:OX_LSE + 1] for t in vals]
        _, vjp = jax.vjp(_dl_merge_fn, *os_, *ls_)
        g = vjp((do_ref[...],))
        for k, d_ref in enumerate((da_ref, db_ref, dc_ref)):
            dm = jnp.sum(g[k] * os_[k], axis=1, keepdims=True) - g[3 + k]
            d_ref[:, 0:E] = g[k]
            d_ref[:, DOX_LSE:DOX_DM] = ls_[k] + jnp.zeros((R, DOX_DM - DOX_LSE), F32)
            d_ref[:, DOX_DM:2 * E] = dm + jnp.zeros((R, 2 * E - DOX_DM), F32)

    spec = pl.BlockSpec((R, 2 * E), lambda i: (i, 0))
    return pl.pallas_call(
        body, name="dl_merge_bwd", grid=(rows // R,), in_specs=[spec] * 3 + [pl.BlockSpec((R, E), lambda i: (i, 0))],
        out_specs=[spec] * 3, out_shape=[jax.ShapeDtypeStruct((rows, 2 * E), F32)] * 3,
        compiler_params=_cp("parallel"),
    )(*oxs, do)


def _dl_bias_tables(rel_bias, dil, T):
    W = T + 2 * DL_HALF
    bq, bk = _dl_bucket_tables(dil, T)
    idx = np.concatenate([bq.reshape(-1), bk.reshape(-1)])
    onehot_t = (np.arange(REL_BUCKETS)[:, None] == idx[None, :]).astype(np.float32)
    tab = matmul(rel_bias.T, jnp.asarray(onehot_t), exact=True, name=f"dl_bias_{dil}", tm=DL_HEADS, tk=REL_BUCKETS,
                 tn=_tile(2 * T * W, (8192, 4096, 2048, 1024, 512, 256, 128)))
    return tab[:, :T * W].reshape(DL_HEADS, T, W), tab[:, T * W:].reshape(DL_HEADS, W, T), bq


def _dl_dm_fn(do, o, dl):
    return (jnp.sum(do * o, axis=-1, keepdims=True) - dl,)


def _old_dl_layer_fwd(x, ng, p):
    S = x.shape[0]
    hn = _prenorm("dl", x, ng)
    u = matmul(hn, p["w_in"], name="dl_in")
    scale = DL_HD ** -0.5
    per_group, o_hm, lse_hm = [], [], []
    for gi, (window, dil) in enumerate(DL_PAIRS):
        base = gi * 3 * DL_W
        Ls = S // dil
        T, _ = _dl_tiles(Ls)
        bq, bk = _dl_bucket_tables(dil, T)
        qs = _to_stream(u[:, base:base + DL_W], dil).astype(MXU_DTYPE)
        ks = _to_stream(u[:, base + DL_W:base + 2 * DL_W], dil).astype(MXU_DTYPE)
        vs = _to_stream(u[:, base + 2 * DL_W:base + 3 * DL_W], dil).astype(MXU_DTYPE)
        bias = p["rel_bias"][bq].transpose(2, 0, 1)
        o_s, lse_s = band_fwd(qs, _pad_l(ks), _pad_l(vs), bias, scale=scale)
        per_group.append(dict(qs=qs, ks=ks, vs=vs, lse_s=lse_s, bq=bq, bk=bk, dil=dil))
        o_hm.append(_stream_to_hm(o_s))
        lse_hm.append(_stream_to_hm(lse_s))
    rows = [(t, 0, False) for t in o_hm] + [(t, 0, False, 1) for t in lse_hm]
    om = rowwise_fwd("dl_merge", _dl_merge_fn, rows, [], [(DL_HD, F32)], W=DL_HD)[0]
    o = om.reshape(DL_HEADS, S, DL_HD).transpose(1, 0, 2).reshape(S, DL_W)
    g = rowwise_fwd("dl_gate", _gate_fn, [(o, 0, False), (u, 9, False)], [], [(DL_W, MXU_DTYPE)], W=DL_W)[0]
    xn = matmul(g, p["w_out"], residual=x, name="dl_out")
    return xn, dict(x=x, ng=ng, hn=hn, u=u, per_group=per_group, o_hm=o_hm, lse_hm=lse_hm, o=o, g=g)


def _old_dl_layer_bwd(sv, p, dx):
    u = sv["u"]
    S = u.shape[0]
    scale = DL_HD ** -0.5
    dg = matmul(dx, p["w_out"], tb=True, name="dl_dg")
    dw_out = matmul(sv["g"], dx, ta=True, out_dtype=GRAD_WIRE_DTYPE, name="dl_dw_out")
    do, dgate = rowwise_bwd("dl_gate_bwd", _gate_fn, [(sv["o"], 0, False), (u, 9, False)], [], [dg], W=DL_W,
                            diff_rows=[0, 1], diff_shared=[])
    do_hm = do.reshape(S, DL_HEADS, DL_HD).transpose(1, 0, 2).reshape(DL_HEADS * S, DL_HD)
    rows = [(t, 0, False) for t in sv["o_hm"]] + [(t, 0, False, 1) for t in sv["lse_hm"]]
    dmerge = rowwise_bwd("dl_merge_bwd", _dl_merge_fn, rows, [], [do_hm], W=DL_HD, diff_rows=[0, 1, 2, 3, 4, 5],
                         diff_shared=[])
    parts, dbs, onehots = [], [], []
    for gi, pg in enumerate(sv["per_group"]):
        dil = pg["dil"]
        Ls = S // dil
        T, W = _dl_tiles(Ls)
        dog, dlg = dmerge[gi], dmerge[3 + gi]
        dm = rowwise_fwd(f"dl_dm_{gi}", _dl_dm_fn, [(dog, 0, False), (sv["o_hm"][gi], 0, False), (dlg, 0, False, 1)], [],
                         [(1, F32)], W=DL_HD)[0]
        do_s, dm_s = _hm_to_stream(dog, dil), _hm_to_stream(dm, dil)
        bias = p["rel_bias"][pg["bq"]].transpose(2, 0, 1)
        bias_t = p["rel_bias"][pg["bk"]].transpose(2, 0, 1)
        kp, vp = _pad_l(pg["ks"]), _pad_l(pg["vs"])
        dq_s, dbias = band_bwd_dq(pg["qs"], kp, vp, bias, pg["lse_s"], dm_s, do_s, scale=scale)
        dk_s, dv_s = band_bwd_dkv(_pad_l(pg["qs"]), pg["ks"], pg["vs"], bias_t, _pad_l(pg["lse_s"]), _pad_l(dm_s),
                                  _pad_l(do_s), scale=scale)
        parts += [_from_stream(dq_s), _from_stream(dk_s), _from_stream(dv_s)]
        dbs.append(dbias.reshape(DL_HEADS, T * W))
        onehots.append((pg["bq"].reshape(-1)[:, None] == np.arange(REL_BUCKETS)[None, :]).astype(np.float32))
    drel = matmul(jnp.concatenate(dbs, axis=1), jnp.asarray(np.concatenate(onehots, axis=0)), exact=True,
                  name="dl_drel", tm=DL_HEADS, tn=REL_BUCKETS, tk=2048)
    du = jnp.concatenate(parts + [dgate], axis=1)
    dx_prev, dng, dw_in = _in_out_bwd("dl", sv["x"], sv["ng"], sv["hn"], du, p["w_in"], dx)
    return dx_prev, dict(w_in=dw_in, w_out=dw_out, rel_bias=drel.T, ng=dng)


def dl_layer_fwd(x, ng, p):
    S = x.shape[0]
    hn = _prenorm("dl", x, ng)
    u = matmul(hn, p["w_in"], name="dl_in")
    scale = DL_HD ** -0.5
    per_group, ox_hm = [], []
    for gi, (window, dil) in enumerate(DL_PAIRS):
        base = gi * 3 * DL_W
        T, _ = _dl_tiles(S // dil)
        qs, ks, vs = [_to_stream(u[:, base + c * DL_W:base + (c + 1) * DL_W], dil).astype(MXU_DTYPE) for c in range(3)]
        bias, bias_t, bq = _dl_bias_tables(p["rel_bias"], dil, T)
        ox_s = band_fwd(qs, ks, vs, bias, scale=scale)
        per_group.append(dict(qs=qs, ks=ks, vs=vs, bias=bias, bias_t=bias_t, bq=bq, dil=dil))
        ox_hm.append(_stream_to_hm(ox_s))
    om = dl_merge_fwd(ox_hm)
    o = om.reshape(DL_HEADS, S, DL_HD).transpose(1, 0, 2).reshape(S, DL_W)
    g = rowwise_fwd("dl_gate", _gate_fn, [(o, 0, False), (u, 9, False)], [], [(DL_W, MXU_DTYPE)], W=DL_W)[0]
    xn = matmul(g, p["w_out"], residual=x, name="dl_out")
    return xn, dict(x=x, ng=ng, hn=hn, u=u, per_group=per_group, ox_hm=ox_hm, o=o, g=g)


def dl_layer_bwd(sv, p, dx):
    u = sv["u"]
    S = u.shape[0]
    scale = DL_HD ** -0.5
    dg = matmul(dx, p["w_out"], tb=True, name="dl_dg")
    dw_out = matmul(sv["g"], dx, ta=True, out_dtype=GRAD_WIRE_DTYPE, name="dl_dw_out")
    do, dgate = rowwise_bwd("dl_gate_bwd", _gate_fn, [(sv["o"], 0, False), (u, 9, False)], [], [dg], W=DL_W,
                            diff_rows=[0, 1], diff_shared=[])
    do_hm = do.reshape(S, DL_HEADS, DL_HD).transpose(1, 0, 2).reshape(DL_HEADS * S, DL_HD)
    dox_hm = dl_merge_bwd(sv["ox_hm"], do_hm)
    parts, dbs, onehots = [], [], []
    for gi, pg in enumerate(sv["per_group"]):
        dil = pg["dil"]
        T, W = _dl_tiles(S // dil)
        dox_s = _hm_to_stream(dox_hm[gi], dil)
        dq_s, dbias = band_bwd_dq(pg["qs"], pg["ks"], pg["vs"], pg["bias"], dox_s, scale=scale)
        dk_s, dv_s = band_bwd_dkv(pg["qs"], pg["ks"], pg["vs"], pg["bias_t"], dox_s, scale=scale)
        parts += [_from_stream(dq_s), _from_stream(dk_s), _from_stream(dv_s)]
        dbs.append(dbias.reshape(DL_HEADS, T * W))
        onehots.append((pg["bq"].reshape(-1)[:, None] == np.arange(REL_BUCKETS)[None, :]).astype(np.float32))
    drel = matmul(jnp.concatenate(dbs, axis=1), jnp.asarray(np.concatenate(onehots, axis=0)), exact=True,
                  name="dl_drel", tm=DL_HEADS, tn=REL_BUCKETS, tk=2048)
    du = jnp.concatenate(parts + [dgate], axis=1)
    dx_prev, dng, dw_in = _in_out_bwd("dl", sv["x"], sv["ng"], sv["hn"], du, p["w_in"], dx)
    return dx_prev, dict(w_in=dw_in, w_out=dw_out, rel_bias=drel.T, ng=dng)


WEIGHT_ORDER = ['norm_g', 'final_g', 'rel_bias', 'hgrn_lb', 'ssd_w_in', 'ssd_conv_w', 'ssd_conv_b', 'ssd_dt_bias',
                'ssd_a_log', 'ssd_d', 'ssd_norm_g', 'ssd_w_out', 'hg_w_in', 'hg_norm_g', 'hg_w_out', 'at_w_in',
                'at_q_norm_g', 'at_k_norm_g', 'at_w_out', 'dl_w_in', 'dl_w_out']
BIG_IN = ['ssd_w_in', 'hg_w_in', 'at_w_in', 'dl_w_in']
BIG_OUT = ['ssd_w_out', 'hg_w_out', 'at_w_out', 'dl_w_out']
BIG = BIG_IN + BIG_OUT
SMALL = [n for n in WEIGHT_ORDER if n not in BIG]
LANES = 128


def _pack(arrs):
    flat = jnp.concatenate([a.reshape(-1).astype(F32) for a in arrs])
    n = flat.shape[0]
    rows = -(-n // (8 * LANES)) * 8
    return jnp.pad(flat, (0, rows * LANES - n)).reshape(rows, LANES)


def _unpack(buf, shapes):
    flat = buf.reshape(-1)
    out, off = [], 0
    for shp in shapes:
        n = int(np.prod(shp)) if len(shp) else 1
        out.append(flat[off:off + n].reshape(shp))
        off += n
    return out


def kernel(x, norm_g, final_g, rel_bias, hgrn_lb, ssd_w_in, ssd_conv_w, ssd_conv_b, ssd_dt_bias, ssd_a_log, ssd_d, ssd_norm_g, ssd_w_out, hg_w_in, hg_norm_g, hg_w_out, at_w_in, at_q_norm_g, at_k_norm_g, at_w_out, dl_w_in, dl_w_out, loss_target, m_norm_g, m_final_g, m_rel_bias, m_hgrn_lb, m_ssd_w_in, m_ssd_conv_w, m_ssd_conv_b, m_ssd_dt_bias, m_ssd_a_log, m_ssd_d, m_ssd_norm_g, m_ssd_w_out, m_hg_w_in, m_hg_norm_g, m_hg_w_out, m_at_w_in, m_at_q_norm_g, m_at_k_norm_g, m_at_w_out, m_dl_w_in, m_dl_w_out, v_norm_g, v_final_g, v_rel_bias, v_hgrn_lb, v_ssd_w_in, v_ssd_conv_w, v_ssd_conv_b, v_ssd_dt_bias, v_ssd_a_log, v_ssd_d, v_ssd_norm_g, v_ssd_w_out, v_hg_w_in, v_hg_norm_g, v_hg_w_out, v_at_w_in, v_at_q_norm_g, v_at_k_norm_g, v_at_w_out, v_dl_w_in, v_dl_w_out):
    w = dict(norm_g=norm_g, final_g=final_g, rel_bias=rel_bias, hgrn_lb=hgrn_lb, ssd_w_in=ssd_w_in, ssd_conv_w=ssd_conv_w, ssd_conv_b=ssd_conv_b, ssd_dt_bias=ssd_dt_bias, ssd_a_log=ssd_a_log, ssd_d=ssd_d, ssd_norm_g=ssd_norm_g, ssd_w_out=ssd_w_out, hg_w_in=hg_w_in, hg_norm_g=hg_norm_g, hg_w_out=hg_w_out, at_w_in=at_w_in, at_q_norm_g=at_q_norm_g, at_k_norm_g=at_k_norm_g, at_w_out=at_w_out, dl_w_in=dl_w_in, dl_w_out=dl_w_out)
    m = dict(norm_g=m_norm_g, final_g=m_final_g, rel_bias=m_rel_bias, hgrn_lb=m_hgrn_lb, ssd_w_in=m_ssd_w_in, ssd_conv_w=m_ssd_conv_w, ssd_conv_b=m_ssd_conv_b, ssd_dt_bias=m_ssd_dt_bias, ssd_a_log=m_ssd_a_log, ssd_d=m_ssd_d, ssd_norm_g=m_ssd_norm_g, ssd_w_out=m_ssd_w_out, hg_w_in=m_hg_w_in, hg_norm_g=m_hg_norm_g, hg_w_out=m_hg_w_out, at_w_in=m_at_w_in, at_q_norm_g=m_at_q_norm_g, at_k_norm_g=m_at_k_norm_g, at_w_out=m_at_w_out, dl_w_in=m_dl_w_in, dl_w_out=m_dl_w_out)
    v = dict(norm_g=v_norm_g, final_g=v_final_g, rel_bias=v_rel_bias, hgrn_lb=v_hgrn_lb, ssd_w_in=v_ssd_w_in, ssd_conv_w=v_ssd_conv_w, ssd_conv_b=v_ssd_conv_b, ssd_dt_bias=v_ssd_dt_bias, ssd_a_log=v_ssd_a_log, ssd_d=v_ssd_d, ssd_norm_g=v_ssd_norm_g, ssd_w_out=v_ssd_w_out, hg_w_in=v_hg_w_in, hg_norm_g=v_hg_norm_g, hg_w_out=v_hg_w_out, at_w_in=v_at_w_in, at_q_norm_g=v_at_q_norm_g, at_k_norm_g=v_at_k_norm_g, at_w_out=v_at_w_out, dl_w_in=v_dl_w_in, dl_w_out=v_dl_w_out)
    me = 4 * lax.axis_index("x") + 2 * lax.axis_index("y") + lax.axis_index("c")
    xs = x[0]
    S = xs.shape[0]

    shard2d = {n: w[n][0] for n in BIG}
    flat = jnp.concatenate([shard2d[n].astype(MXU_DTYPE).reshape(-1, D_MODEL) for n in BIG], axis=0)
    gathered = allgather_two_level(flat, name="allgather_weights")
    full, off = {}, 0
    for n in BIG:
        r, c = shard2d[n].shape
        nr = r * c // D_MODEL
        blk = gathered[:, off:off + nr].reshape(N_DEV, r, c)
        off += nr
        full[n] = blk.transpose(1, 0, 2).reshape(r, N_DEV * c) if n in BIG_IN else blk.reshape(N_DEV * r, c)
    ncw = ssd_conv_w.shape[2]
    nhg = hg_norm_g.shape[1]
    small_shard = jnp.zeros((8, 512), F32)
    small_shard = small_shard.at[:SSD_CONV, :ncw].set(ssd_conv_w[0]).at[SSD_CONV, :nhg].set(hg_norm_g[0])
    small_all = allgather_two_level(small_shard, name="allgather_small_weights")
    conv_w_full = small_all[:, :SSD_CONV, :ncw].transpose(1, 0, 2).reshape(SSD_CONV, N_DEV * ncw)
    hg_norm_full = small_all[:, SSD_CONV, :nhg].reshape(1, N_DEV * nhg)

    p_ssd = dict(w_in=jnp.pad(full["ssd_w_in"], ((0, 0), (0, SSD_IN_PAD - SSD_IN))), w_out=full["ssd_w_out"],
                 conv_w=conv_w_full, conv_b=ssd_conv_b,
                 dt_bias=jnp.pad(ssd_dt_bias.reshape(1, 2 * SSD_HEADS), ((0, 0), (0, 128 - 2 * SSD_HEADS))),
                 alog=jnp.pad(ssd_a_log.reshape(1, 2 * SSD_HEADS), ((0, 0), (0, 128 - 2 * SSD_HEADS))),
                 dexp=jnp.repeat(ssd_d.reshape(-1), SSD_HEADDIM)[None, :], norm_g=ssd_norm_g)
    p_hg = dict(w_in=full["hg_w_in"], w_out=full["hg_w_out"], norm_g=hg_norm_full, hgrn_lb=hgrn_lb)
    p_at = dict(w_in=full["at_w_in"], w_out=full["at_w_out"], q_g=at_q_norm_g, k_g=at_k_norm_g)
    p_dl = dict(w_in=full["dl_w_in"], w_out=full["dl_w_out"], rel_bias=rel_bias)

    x1, sv0 = ssd_layer_fwd(xs, norm_g[0:1], p_ssd)
    x2, sv1 = hg_layer_fwd(x1, norm_g[1:2], p_hg)
    x3, sv2 = at_layer_fwd(x2, norm_g[2:3], p_at)
    x4, sv3 = dl_layer_fwd(x3, norm_g[3:4], p_dl)
    loss_part, dx4, dfinal = loss_head(x4, final_g[None, :], loss_target[0])
    dx3, g3 = dl_layer_bwd(sv3, p_dl, dx4)
    dx2, g2 = at_layer_bwd(sv2, p_at, dx3)
    dx1, g1 = hg_layer_bwd(sv1, p_hg, dx2)
    dx0, g0 = ssd_layer_bwd(sv0, p_ssd, dx1)

    small_full = dict(
        norm_g=jnp.concatenate([g0["ng"], g1["ng"], g2["ng"], g3["ng"]], axis=0), final_g=dfinal[0],
        rel_bias=g3["rel_bias"], hgrn_lb=g1["hgrn_lb"], ssd_conv_w=g0["conv_w"][None], ssd_conv_b=g0["conv_b"],
        ssd_dt_bias=g0["dt_bias"].reshape(1, 2, SSD_HEADS), ssd_a_log=g0["a_log"].reshape(1, 2, SSD_HEADS),
        ssd_d=g0["d"], ssd_norm_g=g0["norm_g"], hg_norm_g=g1["norm_g"], at_q_norm_g=g2["q_g"], at_k_norm_g=g2["k_g"])
    packed = _pack([loss_part[0, 0:1]] + [small_full[n] for n in SMALL])
    summed = sum_parts(allgather_two_level(packed, name="allgather_small_grads"), name="sum_small_grads")
    parts = _unpack(summed, [()] + [small_full[n].shape for n in SMALL])
    loss = parts[0]
    gsmall = dict(zip(SMALL, parts[1:]))
    gsmall["ssd_conv_w"] = lax.dynamic_slice_in_dim(gsmall["ssd_conv_w"], me * ncw, ncw, axis=2)
    gsmall["hg_norm_g"] = lax.dynamic_slice_in_dim(gsmall["hg_norm_g"], me * nhg, nhg, axis=1)
    shapes = [w[n].shape for n in SMALL]
    d_p, m_p, v_p = adamw_plain(_pack([w[n] for n in SMALL]), _pack([gsmall[n] for n in SMALL]),
                                _pack([m[n] for n in SMALL]), _pack([v[n] for n in SMALL]), name="adamw_small")
    grads = dict(gsmall)
    deltas = dict(zip(SMALL, _unpack(d_p, shapes)))
    new_m = dict(zip(SMALL, _unpack(m_p, shapes)))
    new_v = dict(zip(SMALL, _unpack(v_p, shapes)))

    gbig = dict(ssd_w_in=g0["w_in"], ssd_w_out=g0["w_out"], hg_w_in=g1["w_in"], hg_w_out=g1["w_out"],
                at_w_in=g2["w_in"], at_w_out=g2["w_out"], dl_w_in=g3["w_in"], dl_w_out=g3["w_out"])
    send = []
    for n in BIG:
        r, c = shard2d[n].shape
        g = gbig[n]
        send.append(g.reshape(r, N_DEV, c).transpose(1, 0, 2) if n in BIG_IN else g.reshape(N_DEV, r, c))
    recv = all_to_all(send, name="exchange_weight_grads")
    for n, parts8 in zip(BIG, recv):
        gs, ds, ms, vs = adamw_sum(parts8, shard2d[n], m[n][0], v[n][0], name=f"adamw_{n}")
        grads[n], deltas[n], new_m[n], new_v[n] = gs[None], ds[None], ms[None], vs[None]

    return (loss, dx0[None], *[grads[n] for n in WEIGHT_ORDER], *[deltas[n] for n in WEIGHT_ORDER],
            *[new_m[n] for n in WEIGHT_ORDER], *[new_v[n] for n in WEIGHT_ORDER])
```

```python
import functools
import math

import jax
import jax.numpy as jnp
import numpy as np
from jax import lax
from jax.experimental import pallas as pl
from jax.experimental.pallas import tpu as pltpu

F32 = jnp.float32
BF16 = jnp.bfloat16
MXU_DTYPE = jnp.bfloat16
GRAD_WIRE_DTYPE = jnp.bfloat16
HIGHEST = lax.Precision.HIGHEST
MESH_ID = pl.DeviceIdType.MESH
N_DEV = 8

D_MODEL = 1024
EPS = 1e-6
NEG_BIG = -1e30

SSD_DI = 2048
SSD_HEADDIM = 64
SSD_HEADS = 32
SSD_GROUPS = 4
SSD_HPG = 8
SSD_STATE = 128
SSD_CONV = 7
SSD_CHUNK = 128
SSD_CONV_CH = SSD_DI + 2 * SSD_GROUPS * SSD_STATE
SSD_IN = SSD_DI + SSD_CONV_CH + 2 * SSD_HEADS
SSD_IN_PAD = 5376

HG_CHUNK = 32
HG_HEADS = 8
HG_D = 128
HG_W = 1024

AT_HEADS = 16
AT_KV = 8
AT_HD = 128
AT_QW = 2048
AT_KW = 1024
GRID_W = 64
ROPE_THETA = 10000.0

DL_PAIRS = ((128, 1), (512, 4), (2048, 16))
DL_HEADS = 16
DL_HD = 64
DL_W = 1024
DL_HALF = 64
REL_BUCKETS = 32
REL_MAX_DIST = 1024

ADAM_LR = 0.001
ADAM_B1 = 0.9
ADAM_B2 = 0.999
ADAM_EPS = 1e-08
ADAM_WD = 0.01
ADAM_STEP = 10

VMEM_LIMIT = 56 * 1024 * 1024


def _cp(*sem):
    return pltpu.CompilerParams(dimension_semantics=tuple(sem), vmem_limit_bytes=VMEM_LIMIT)


def _tile(n, cands=(1024, 768, 512, 384, 256, 128)):
    for c in cands:
        if n % c == 0:
            return c
    return n


def _dot(a, b, dims):
    return lax.dot_general(a.astype(MXU_DTYPE), b.astype(MXU_DTYPE), (dims, ((), ())), preferred_element_type=F32)


def _dot_exact(a, b, dims):
    return lax.dot_general(a, b, (dims, ((), ())), precision=HIGHEST, preferred_element_type=F32)


def _silu(x):
    return x * jax.nn.sigmoid(x)


def _my_pos():
    return lax.axis_index("x"), lax.axis_index("y"), lax.axis_index("c")


def _flat(px, py, pc):
    return 4 * px + 2 * py + pc


def _peers():
    x_, y_, c_ = _my_pos()
    out = []
    for k in range(1, N_DEV):
        fx, fy, fc = (k >> 2) & 1, (k >> 1) & 1, k & 1
        out.append(((1 - x_) if fx else x_, (1 - y_) if fy else y_, (1 - c_) if fc else c_))
    return out


def _comm_copies(kind, in_refs, out_refs, send_sems, recv_sems, local_sems):
    me = _flat(*_my_pos())
    local, starts, waits = [], [], []
    for b, (i_ref, o_ref) in enumerate(zip(in_refs, out_refs)):
        local.append(pltpu.make_async_copy(i_ref if kind == "ag" else i_ref.at[me], o_ref.at[me], local_sems.at[b]))
        for k, p in enumerate(_peers()):
            src = i_ref if kind == "ag" else i_ref.at[_flat(*p)]
            starts.append(pltpu.make_async_remote_copy(
                src_ref=src, dst_ref=o_ref.at[me], send_sem=send_sems.at[b, k], recv_sem=recv_sems.at[b, k],
                device_id=p, device_id_type=MESH_ID))
            waits.append(pltpu.make_async_remote_copy(
                src_ref=src, dst_ref=o_ref.at[_flat(*p)], send_sem=send_sems.at[b, k], recv_sem=recv_sems.at[b, k],
                device_id=p, device_id_type=MESH_ID))
    return local, starts, waits


def pcall(body, comm, *, name, grid, in_specs, out_specs, out_shape, scratch_shapes=(), compiler_params=None):
    single = not isinstance(out_specs, (list, tuple))
    out_specs_l = [out_specs] if single else list(out_specs)
    out_shape_l = [out_shape] if single else list(out_shape)
    if comm is None:
        return pl.pallas_call(body, name=name, grid=grid, in_specs=in_specs, out_specs=out_specs, out_shape=out_shape,
                              scratch_shapes=list(scratch_shapes), compiler_params=compiler_params)
    kind, bufs = comm
    nb, n_in, n_out, n_scr = len(bufs), len(in_specs), len(out_specs_l), len(scratch_shapes)
    c_shape = [jax.ShapeDtypeStruct(((N_DEV,) + b.shape) if kind == "ag" else b.shape, b.dtype) for b in bufs]
    anyspec = pl.BlockSpec(memory_space=pl.ANY)

    def body2(*refs):
        ins, c_ins = refs[:n_in], refs[n_in:n_in + nb]
        outs = refs[n_in + nb:n_in + nb + n_out]
        c_outs = refs[n_in + nb + n_out:n_in + 2 * nb + n_out]
        scr = refs[n_in + 2 * nb + n_out:n_in + 2 * nb + n_out + n_scr]
        send_sems, recv_sems, local_sems = refs[n_in + 2 * nb + n_out + n_scr:]
        first = last = None
        for ax, g in enumerate(grid):
            pid = pl.program_id(ax)
            first = (pid == 0) if first is None else jnp.logical_and(first, pid == 0)
            last = (pid == g - 1) if last is None else jnp.logical_and(last, pid == g - 1)

        @pl.when(first)
        def _():
            local, starts, _ = _comm_copies(kind, c_ins, c_outs, send_sems, recv_sems, local_sems)
            for cp in local + starts:
                cp.start()

        body(*ins, *outs, *scr)

        @pl.when(last)
        def _():
            local, _, waits = _comm_copies(kind, c_ins, c_outs, send_sems, recv_sems, local_sems)
            for cp in waits + local:
                cp.wait()

    call = pl.pallas_call(
        body2, name=name, grid=grid, in_specs=list(in_specs) + [anyspec] * nb,
        out_specs=out_specs_l + [anyspec] * nb, out_shape=out_shape_l + c_shape,
        scratch_shapes=list(scratch_shapes) + [pltpu.SemaphoreType.DMA((nb, N_DEV - 1)),
                                               pltpu.SemaphoreType.DMA((nb, N_DEV - 1)), pltpu.SemaphoreType.DMA((nb,))],
        compiler_params=compiler_params)

    def run(*args):
        res = call(*args, *bufs)
        own = res[:n_out]
        return (own[0] if single else list(own)), list(res[n_out:])

    return run


def comm_only(comm, *, name):
    def body(x_ref, o_ref):
        o_ref[...] = x_ref[...]

    spec = pl.BlockSpec((8, 128), lambda i: (0, 0))
    _, res = pcall(body, comm, name=name, grid=(1,), in_specs=[spec], out_specs=spec,
                   out_shape=jax.ShapeDtypeStruct((8, 128), F32))(jnp.zeros((8, 128), F32))
    return res


def matmul(a, b, *, name, ta=False, tb=False, residual=None, out_dtype=F32, exact=False, tm=None, tn=None, tk=None,
           b_cols=None):
    M, K = (a.shape[1], a.shape[0]) if ta else a.shape
    n0, N = b_cols if b_cols is not None else (0, b.shape[0] if tb else b.shape[1])
    tm = tm or _tile(M, (512, 256, 128))
    tn = tn or _tile(N, (1024, 768, 512, 384, 256, 128))
    tk = tk or _tile(K, (1024, 768, 512, 384, 256, 128))
    nk = K // tk
    dims = (((0,) if ta else (1,)), ((1,) if tb else (0,)))

    def body(*refs):
        if residual is None:
            a_ref, b_ref, o_ref, acc = refs
            r_ref = None
        else:
            a_ref, b_ref, r_ref, o_ref, acc = refs
        k = pl.program_id(2)

        @pl.when(k == 0)
        def _():
            acc[...] = jnp.zeros_like(acc)

        if exact:
            acc[...] += _dot_exact(a_ref[...], b_ref[...], dims)
        else:
            acc[...] += _dot(a_ref[...], b_ref[...], dims)

        @pl.when(k == nk - 1)
        def _():
            r = acc[...]
            if r_ref is not None:
                r = r + r_ref[...]
            o_ref[...] = r.astype(o_ref.dtype)

    a_spec = pl.BlockSpec((tk, tm), lambda i, j, k: (k, i)) if ta else pl.BlockSpec((tm, tk), lambda i, j, k: (i, k))
    assert n0 % tn == 0
    jb = n0 // tn
    b_spec = (pl.BlockSpec((tn, tk), lambda i, j, k: (j + jb, k)) if tb
              else pl.BlockSpec((tk, tn), lambda i, j, k: (k, j + jb)))
    in_specs = [a_spec, b_spec]
    args = [a, b]
    if residual is not None:
        in_specs.append(pl.BlockSpec((tm, tn), lambda i, j, k: (i, j)))
        args.append(residual)
    return pl.pallas_call(
        body, name=name, grid=(M // tm, N // tn, nk), in_specs=in_specs,
        out_specs=pl.BlockSpec((tm, tn), lambda i, j, k: (i, j)),
        out_shape=jax.ShapeDtypeStruct((M, N), out_dtype),
        scratch_shapes=[pltpu.VMEM((tm, tn), F32)],
        compiler_params=_cp("parallel", "parallel", "arbitrary"),
    )(*args)


def _row_specs(rows, shared, R, W):
    specs = []
    for arr, col0, per_j, *wd in rows:
        w = wd[0] if wd else W
        specs.append(pl.BlockSpec((R, w), (lambda j, i, c=col0: (i, c + j)) if per_j else (lambda j, i, c=col0: (i, c))))
    for arr, per_j in shared:
        specs.append(pl.BlockSpec((arr.shape[0], W), (lambda j, i: (0, j)) if per_j else (lambda j, i: (0, 0))))
    return specs


def rowwise_fwd(name, fn, rows, shared, outs, *, W, ncb=1, R=256):
    S = rows[0][0].shape[0]
    R = min(R, S)
    nr, ns = len(rows), len(shared)

    def body(*refs):
        vals = [r[...] for r in refs[:nr + ns]]
        res = fn(*vals)
        for o_ref, r in zip(refs[nr + ns:], res):
            o_ref[...] = r.astype(o_ref.dtype)

    return pl.pallas_call(
        body, name=name, grid=(ncb, S // R),
        in_specs=_row_specs(rows, shared, R, W),
        out_specs=[pl.BlockSpec((R, w), lambda j, i: (i, j)) for w, _ in outs],
        out_shape=[jax.ShapeDtypeStruct((S, ncb * w), dt) for w, dt in outs],
        compiler_params=_cp("parallel", "parallel"),
    )(*[r[0] for r in rows], *[s[0] for s in shared])


def rowwise_bwd(name, fn, rows, shared, cots, *, W, ncb=1, R=256, diff_rows, diff_shared, add=None):
    S = rows[0][0].shape[0]
    R = min(R, S)
    nr, ns, nc = len(rows), len(shared), len(cots)
    nsteps = S // R

    def body(*refs):
        ins = refs[:nr + ns]
        ct_refs = refs[nr + ns:nr + ns + nc]
        pos = nr + ns + nc
        add_ref = None
        if add is not None:
            add_ref = refs[pos]
            pos += 1
        drow_refs = refs[pos:pos + len(diff_rows)]
        dsh_refs = refs[pos + len(diff_rows):]
        j, i = pl.program_id(0), pl.program_id(1)
        vals = [r[...] for r in ins]

        def f(*dv):
            full = list(vals)
            for idx, v in zip(list(diff_rows) + [nr + s for s in diff_shared], dv):
                full[idx] = v
            return tuple(fn(*full))

        prim = [vals[idx] for idx in diff_rows] + [vals[nr + s] for s in diff_shared]
        _, vjp = jax.vjp(f, *prim)
        grads = vjp(tuple(c[...] for c in ct_refs))
        for k, d_ref in enumerate(drow_refs):
            g = grads[k]
            if k == 0 and add_ref is not None:
                g = g + add_ref[...]
            d_ref[...] = g
        for k, (d_ref, s) in enumerate(zip(dsh_refs, diff_shared)):
            g = grads[len(diff_rows) + k]
            first = (i == 0) if shared[s][1] else jnp.logical_and(i == 0, j == 0)

            @pl.when(first)
            def _(d_ref=d_ref, g=g):
                d_ref[...] = g

            @pl.when(jnp.logical_not(first))
            def _(d_ref=d_ref, g=g):
                d_ref[...] += g

    in_specs = _row_specs(rows, shared, R, W)
    wo = [c.shape[1] // ncb for c in cots]
    in_specs += [pl.BlockSpec((R, w), lambda j, i: (i, j)) for w in wo]
    args = [r[0] for r in rows] + [s[0] for s in shared] + list(cots)
    if add is not None:
        in_specs.append(pl.BlockSpec((R, W), lambda j, i: (i, j)))
        args.append(add)
    dws = [(rows[r][3] if len(rows[r]) > 3 else W) for r in diff_rows]
    out_specs = [pl.BlockSpec((R, w), lambda j, i: (i, j)) for w in dws]
    out_shape = [jax.ShapeDtypeStruct((S, ncb * w), F32) for w in dws]
    for s in diff_shared:
        arr, per_j = shared[s]
        out_specs.append(pl.BlockSpec((arr.shape[0], W), (lambda j, i: (0, j)) if per_j else (lambda j, i: (0, 0))))
        out_shape.append(jax.ShapeDtypeStruct((arr.shape[0], ncb * W if per_j else W), F32))
    return pl.pallas_call(
        body, name=name, grid=(ncb, nsteps), in_specs=in_specs, out_specs=out_specs, out_shape=out_shape,
        compiler_params=_cp("arbitrary", "arbitrary"),
    )(*args)


def _row_specs2(rows, shared, R, W, ncb):
    specs = []
    for arr, col0, per_j, *wd in rows:
        w = wd[0] if wd else W
        if per_j:
            assert col0 % ncb == 0
            specs.append(pl.BlockSpec((R, ncb * w), lambda i, c=col0 // ncb: (i, c)))
        else:
            specs.append(pl.BlockSpec((R, w), lambda i, c=col0: (i, c)))
    for arr, per_j in shared:
        specs.append(pl.BlockSpec((arr.shape[0], ncb * W if per_j else arr.shape[1]), lambda i: (0, 0)))
    return specs


def _col_block(ref, per_j, j, w):
    return ref[:, j * w:(j + 1) * w] if per_j else ref[...]


def rowwise_fwd(name, fn, rows, shared, outs, *, W, ncb=1, R=256):
    S = rows[0][0].shape[0]
    R = min(R, S)
    nr, ns = len(rows), len(shared)
    widths = [(r[3] if len(r) > 3 else W) for r in rows]
    per_j = [r[2] for r in rows] + [s[1] for s in shared]
    ws = widths + [W] * ns

    def body(*refs):
        for j in range(ncb):
            vals = [_col_block(refs[k], per_j[k], j, ws[k]) for k in range(nr + ns)]
            res = fn(*vals)
            for o_ref, r, (wo, _) in zip(refs[nr + ns:], res, outs):
                o_ref[:, j * wo:(j + 1) * wo] = r.astype(o_ref.dtype)

    return pl.pallas_call(
        body, name=name, grid=(S // R,),
        in_specs=_row_specs2(rows, shared, R, W, ncb),
        out_specs=[pl.BlockSpec((R, ncb * w), lambda i: (i, 0)) for w, _ in outs],
        out_shape=[jax.ShapeDtypeStruct((S, ncb * w), dt) for w, dt in outs],
        compiler_params=_cp("parallel"),
    )(*[r[0] for r in rows], *[s[0] for s in shared])


def rowwise_bwd(name, fn, rows, shared, cots, *, W, ncb=1, R=256, diff_rows, diff_shared, add=None):
    S = rows[0][0].shape[0]
    R = min(R, S)
    nr, ns, nc = len(rows), len(shared), len(cots)
    widths = [(r[3] if len(r) > 3 else W) for r in rows]
    per_j = [r[2] for r in rows] + [s[1] for s in shared]
    ws = widths + [W] * ns
    wo = [c.shape[1] // ncb for c in cots]
    dws = [widths[r] for r in diff_rows]

    def body(*refs):
        ins = refs[:nr + ns]
        ct_refs = refs[nr + ns:nr + ns + nc]
        pos = nr + ns + nc
        add_ref = None
        if add is not None:
            add_ref = refs[pos]
            pos += 1
        drow_refs = refs[pos:pos + len(diff_rows)]
        dsh_refs = refs[pos + len(diff_rows):]
        i = pl.program_id(0)
        tot = [None] * len(diff_shared)
        for j in range(ncb):
            vals = [_col_block(ins[k], per_j[k], j, ws[k]) for k in range(nr + ns)]

            def f(*dv):
                full = list(vals)
                for idx, v in zip(list(diff_rows) + [nr + s for s in diff_shared], dv):
                    full[idx] = v
                return tuple(fn(*full))

            prim = [vals[idx] for idx in diff_rows] + [vals[nr + s] for s in diff_shared]
            _, vjp = jax.vjp(f, *prim)
            grads = vjp(tuple(c[:, j * w:(j + 1) * w] for c, w in zip(ct_refs, wo)))
            for k, (d_ref, w) in enumerate(zip(drow_refs, dws)):
                g = grads[k]
                if k == 0 and add_ref is not None:
                    g = g + add_ref[:, j * w:(j + 1) * w]
                d_ref[:, j * w:(j + 1) * w] = g
            for k, (d_ref, s) in enumerate(zip(dsh_refs, diff_shared)):
                g = grads[len(diff_rows) + k]
                if shared[s][1]:
                    @pl.when(i == 0)
                    def _(d_ref=d_ref, g=g, j=j):
                        d_ref[:, j * W:(j + 1) * W] = g

                    @pl.when(i != 0)
                    def _(d_ref=d_ref, g=g, j=j):
                        d_ref[:, j * W:(j + 1) * W] += g
                else:
                    tot[k] = g if tot[k] is None else tot[k] + g
        for k, (d_ref, s) in enumerate(zip(dsh_refs, diff_shared)):
            if not shared[s][1]:
                @pl.when(i == 0)
                def _(d_ref=d_ref, g=tot[k]):
                    d_ref[...] = g

                @pl.when(i != 0)
                def _(d_ref=d_ref, g=tot[k]):
                    d_ref[...] += g

    in_specs = _row_specs2(rows, shared, R, W, ncb)
    in_specs += [pl.BlockSpec((R, ncb * w), lambda i: (i, 0)) for w in wo]
    args = [r[0] for r in rows] + [s[0] for s in shared] + list(cots)
    if add is not None:
        in_specs.append(pl.BlockSpec((R, ncb * dws[0]), lambda i: (i, 0)))
        args.append(add)
    out_specs = [pl.BlockSpec((R, ncb * w), lambda i: (i, 0)) for w in dws]
    out_shape = [jax.ShapeDtypeStruct((S, ncb * w), F32) for w in dws]
    for s in diff_shared:
        arr, pj = shared[s]
        shp = (arr.shape[0], ncb * W if pj else arr.shape[1])
        out_specs.append(pl.BlockSpec(shp, lambda i: (0, 0)))
        out_shape.append(jax.ShapeDtypeStruct(shp, F32))
    return pl.pallas_call(
        body, name=name, grid=(S // R,), in_specs=in_specs, out_specs=out_specs, out_shape=out_shape,
        compiler_params=_cp("arbitrary"),
    )(*args)


def _rms(x, g):
    return x * lax.rsqrt(jnp.mean(x * x, axis=-1, keepdims=True) + EPS) * g


def _prenorm_fn(x, g):
    return (_rms(x, g),)


def loss_head(x, g, tgt, *, R=256):
    S, D = x.shape
    R = min(R, S)

    def fn(xv, gv, tv):
        err = _rms(xv, gv) - tv
        return 0.5 * jnp.sum(jnp.mean(err * err, axis=-1, keepdims=True), axis=0, keepdims=True)

    def body(x_ref, g_ref, t_ref, loss_ref, dx_ref, dg_ref):
        i = pl.program_id(0)
        tv = t_ref[...]
        val, vjp = jax.vjp(lambda a, b: fn(a, b, tv), x_ref[...], g_ref[...])
        dx, dg = vjp(jnp.ones((1, 1), F32))
        dx_ref[...] = dx

        @pl.when(i == 0)
        def _():
            loss_ref[...] = jnp.zeros_like(loss_ref) + val
            dg_ref[...] = dg

        @pl.when(i != 0)
        def _():
            loss_ref[...] += val
            dg_ref[...] += dg

    return pl.pallas_call(
        body, name="loss_head", grid=(S // R,),
        in_specs=[pl.BlockSpec((R, D), lambda i: (i, 0)), pl.BlockSpec((1, D), lambda i: (0, 0)),
                  pl.BlockSpec((R, D), lambda i: (i, 0))],
        out_specs=[pl.BlockSpec((1, 128), lambda i: (0, 0)), pl.BlockSpec((R, D), lambda i: (i, 0)),
                   pl.BlockSpec((1, D), lambda i: (0, 0))],
        out_shape=[jax.ShapeDtypeStruct((1, 128), F32), jax.ShapeDtypeStruct((S, D), F32),
                   jax.ShapeDtypeStruct((1, D), F32)],
        compiler_params=_cp("arbitrary"),
    )(x, g, tgt)


@jax.custom_vjp
def _softplus(x):
    z = jnp.exp(-jnp.abs(x))
    u = 1.0 + z
    log1p = jnp.where(u == 1.0, z, jnp.log(u) * (z / jnp.where(u == 1.0, 1.0, u - 1.0)))
    return jnp.maximum(x, 0.0) + log1p


def _softplus_fwd(x):
    return _softplus(x), x


def _softplus_bwd(x, ct):
    return (ct * jax.nn.sigmoid(x),)


_softplus.defvjp(_softplus_fwd, _softplus_bwd)


def _dt_fn(raw, bias):
    return (_softplus(raw + bias),)


CONV_CB = 256
CONV_RB = 512
CONV_PAD = 8


def ssd_conv_fwd(u, conv_w, conv_b):
    S = u.shape[0]
    ncb = SSD_CONV_CH // CONV_CB
    col0 = SSD_DI // CONV_CB
    RB = min(CONV_RB, S)

    def body(x_ref, w_ref, b_ref, o_ref, pad):
        pad[0:CONV_PAD, :] = jnp.zeros((CONV_PAD, CONV_CB), F32)
        pad[S + CONV_PAD:S + 2 * CONV_PAD, :] = jnp.zeros((CONV_PAD, CONV_CB), F32)
        pad[CONV_PAD:S + CONV_PAD, :] = x_ref[...]
        w = w_ref[...]
        b = b_ref[...]
        for r in range(S // RB):
            acc = jnp.zeros((RB, CONV_CB), F32) + b
            for k in range(SSD_CONV):
                off = r * RB + CONV_PAD + k - SSD_CONV // 2
                acc = acc + pad[off:off + RB, :] * w[k:k + 1, :]
            o_ref[r * RB:(r + 1) * RB, :] = _silu(acc)

    return pl.pallas_call(
        body, name="ssd_conv_fwd", grid=(ncb,),
        in_specs=[pl.BlockSpec((S, CONV_CB), lambda j: (0, col0 + j)),
                  pl.BlockSpec((SSD_CONV, CONV_CB), lambda j: (0, j)),
                  pl.BlockSpec((1, CONV_CB), lambda j: (0, j))],
        out_specs=pl.BlockSpec((S, CONV_CB), lambda j: (0, j)),
        out_shape=jax.ShapeDtypeStruct((S, SSD_CONV_CH), F32),
        scratch_shapes=[pltpu.VMEM((S + 2 * CONV_PAD, CONV_CB), F32)],
        compiler_params=_cp("parallel"),
    )(u, conv_w, conv_b)


def ssd_conv_bwd(u, conv_w, conv_b, dact):
    S = u.shape[0]
    ncb = SSD_CONV_CH // CONV_CB
    col0 = SSD_DI // CONV_CB
    RB = min(CONV_RB, S)
    half = SSD_CONV // 2

    def body(x_ref, w_ref, b_ref, da_ref, dx_ref, dw_ref, db_ref, xpad, dpad):
        z8 = jnp.zeros((CONV_PAD, CONV_CB), F32)
        xpad[0:CONV_PAD, :] = z8
        xpad[S + CONV_PAD:S + 2 * CONV_PAD, :] = z8
        dpad[0:CONV_PAD, :] = z8
        dpad[S + CONV_PAD:S + 2 * CONV_PAD, :] = z8
        xpad[CONV_PAD:S + CONV_PAD, :] = x_ref[...]
        w = w_ref[...]
        b = b_ref[...]
        dws = [jnp.zeros((1, CONV_CB), F32) for _ in range(SSD_CONV)]
        db = jnp.zeros((1, CONV_CB), F32)
        for r in range(S // RB):
            acc = jnp.zeros((RB, CONV_CB), F32) + b
            xs = []
            for k in range(SSD_CONV):
                off = r * RB + CONV_PAD + k - half
                xk = xpad[off:off + RB, :]
                xs.append(xk)
                acc = acc + xk * w[k:k + 1, :]
            sg = jax.nn.sigmoid(acc)
            dc = da_ref[r * RB:(r + 1) * RB, :] * (sg * (1.0 + acc * (1.0 - sg)))
            dpad[r * RB + CONV_PAD:(r + 1) * RB + CONV_PAD, :] = dc
            db = db + jnp.sum(dc, axis=0, keepdims=True)
            for k in range(SSD_CONV):
                dws[k] = dws[k] + jnp.sum(xs[k] * dc, axis=0, keepdims=True)
        for r in range(S // RB):
            acc = jnp.zeros((RB, CONV_CB), F32)
            for k in range(SSD_CONV):
                off = r * RB + CONV_PAD + half - k
                acc = acc + dpad[off:off + RB, :] * w[k:k + 1, :]
            dx_ref[r * RB:(r + 1) * RB, :] = acc
        for k in range(SSD_CONV):
            dw_ref[k:k + 1, :] = dws[k]
        dw_ref[SSD_CONV:SSD_CONV + 1, :] = jnp.zeros((1, CONV_CB), F32)
        db_ref[...] = db

    return pl.pallas_call(
        body, name="ssd_conv_bwd", grid=(ncb,),
        in_specs=[pl.BlockSpec((S, CONV_CB), lambda j: (0, col0 + j)),
                  pl.BlockSpec((SSD_CONV, CONV_CB), lambda j: (0, j)),
                  pl.BlockSpec((1, CONV_CB), lambda j: (0, j)),
                  pl.BlockSpec((S, CONV_CB), lambda j: (0, j))],
        out_specs=[pl.BlockSpec((S, CONV_CB), lambda j: (0, j)),
                   pl.BlockSpec((SSD_CONV + 1, CONV_CB), lambda j: (0, j)),
                   pl.BlockSpec((1, CONV_CB), lambda j: (0, j))],
        out_shape=[jax.ShapeDtypeStruct((S, SSD_CONV_CH), F32),
                   jax.ShapeDtypeStruct((SSD_CONV + 1, SSD_CONV_CH), F32),
                   jax.ShapeDtypeStruct((1, SSD_CONV_CH), F32)],
        scratch_shapes=[pltpu.VMEM((S + 2 * CONV_PAD, CONV_CB), F32), pltpu.VMEM((S + 2 * CONV_PAD, CONV_CB), F32)],
        compiler_params=_cp("parallel"),
    )(u, conv_w, conv_b, dact)


def _ssd_chunk(states, xs, Bg, Cg, dt, alog, *, reverse):
    Q = SSD_CHUNK
    r = lax.broadcasted_iota(jnp.int32, (Q, Q), 0)
    c = lax.broadcasted_iota(jnp.int32, (Q, Q), 1)
    keep = (c >= r) if reverse else (c <= r)
    tri = jnp.where(keep, 1.0, 0.0).astype(F32)
    a = dt * (-jnp.exp(alog))
    cum = _dot_exact(tri, a, ((1,), (0,)))
    cum_t = jnp.transpose(cum)
    last = 0 if reverse else Q - 1
    CB = _dot(Cg, Bg, ((1,), (1,)))
    new_states, ys = [], []
    for h in range(SSD_HPG):
        col = h + (SSD_HPG if reverse else 0)
        cum_c = cum[:, col:col + 1]
        cum_r = cum_t[col:col + 1, :]
        dt_c = dt[:, col:col + 1]
        cum_l = cum_c[last:last + 1, :]
        diff = jnp.where(keep, cum_c - cum_r, 0.0)
        L = jnp.where(keep, jnp.exp(diff), 0.0)
        xdt = xs[h] * dt_c
        y = _dot(CB * L, xdt, ((1,), (0,)))
        y = y + _dot(Cg * jnp.exp(cum_c), states[h], ((1,), (0,)))
        ns = jnp.exp(cum_l) * states[h] + _dot(Bg * jnp.exp(cum_l - cum_c), xdt, ((0,), (0,)))
        new_states.append(ns)
        ys.append(y)
    return new_states, ys


def _ssd_group_layout(t):
    r = t.shape[0]
    g = t[:, :2 * SSD_HEADS].reshape(r, 2, SSD_GROUPS, SSD_HPG).transpose(2, 0, 1, 3).reshape(SSD_GROUPS, r, 2 * SSD_HPG)
    return jnp.pad(g, ((0, 0), (0, 0), (0, 128 - 2 * SSD_HPG)))


def _ssd_head_layout(t):
    r = t.shape[1]
    h = t[:, :, :2 * SSD_HPG].reshape(SSD_GROUPS, r, 2, SSD_HPG).transpose(1, 2, 0, 3).reshape(r, 2 * SSD_HEADS)
    return jnp.pad(h, ((0, 0), (0, 128 - 2 * SSD_HEADS)))


def ssd_scan_fwd(act, dt, alog, *, reverse, y_prev=None, comm=None):
    S = act.shape[0]
    Q, N, P = SSD_CHUNK, SSD_STATE, SSD_HEADDIM
    nc = S // Q
    GW = SSD_HPG * P

    def cidx(i):
        return (nc - 1 - i) if reverse else i

    def body(*refs):
        if y_prev is None:
            x_ref, b_ref, c_ref, dt_ref, al_ref, y_ref, st_ref, state = refs
            yp_ref = None
        else:
            x_ref, b_ref, c_ref, dt_ref, al_ref, yp_ref, y_ref, st_ref, state = refs
        g, i = pl.program_id(0), pl.program_id(1)

        @pl.when(i == 0)
        def _():
            state[...] = jnp.zeros_like(state)

        states = [state[h] for h in range(SSD_HPG)]
        for h in range(SSD_HPG):
            st_ref[0, h] = states[h]
        xv = x_ref[...]
        xs = [xv[:, h * P:(h + 1) * P] for h in range(SSD_HPG)]
        ns, ys = _ssd_chunk(states, xs, b_ref[...], c_ref[...], dt_ref[0], al_ref[0], reverse=reverse)
        for h in range(SSD_HPG):
            state[h] = ns[h]
            yh = ys[h]
            if yp_ref is not None:
                yh = yh + yp_ref[:, h * P:(h + 1) * P]
            y_ref[:, h * P:(h + 1) * P] = yh

    in_specs = [pl.BlockSpec((Q, GW), lambda g, i: (cidx(i), g)),
                pl.BlockSpec((Q, N), lambda g, i: (cidx(i), SSD_DI // N + g)),
                pl.BlockSpec((Q, N), lambda g, i: (cidx(i), SSD_DI // N + SSD_GROUPS + g)),
                pl.BlockSpec((1, Q, 128), lambda g, i: (g, cidx(i), 0)),
                pl.BlockSpec((1, 1, 128), lambda g, i: (g, 0, 0))]
    args = [act, act, act, dt, alog]
    if y_prev is not None:
        in_specs.append(pl.BlockSpec((Q, GW), lambda g, i: (cidx(i), g)))
        args.append(y_prev)
    return pcall(
        body, comm, name=f"ssd_scan_fwd_{int(reverse)}", grid=(SSD_GROUPS, nc), in_specs=in_specs,
        out_specs=[pl.BlockSpec((Q, GW), lambda g, i: (cidx(i), g)),
                   pl.BlockSpec((1, SSD_HPG, N, P), lambda g, i: (cidx(i), g, 0, 0))],
        out_shape=[jax.ShapeDtypeStruct((S, SSD_DI), F32), jax.ShapeDtypeStruct((nc, SSD_HEADS, N, P), F32)],
        scratch_shapes=[pltpu.VMEM((SSD_HPG, N, P), F32)],
        compiler_params=_cp("arbitrary", "arbitrary"),
    )(*args)


def ssd_scan_bwd(act, dt, alog, states, dy, prev_x, *, reverse, prev=None, comm=None):
    S = act.shape[0]
    Q, N, P = SSD_CHUNK, SSD_STATE, SSD_HEADDIM
    nc = S // Q
    GW = SSD_HPG * P

    def cidx(i):
        return i if reverse else (nc - 1 - i)

    def body(*refs):
        x_ref, b_ref, c_ref, dt_ref, al_ref, st_ref, dy_ref, px_ref = refs[:8]
        pos = 8
        if prev is not None:
            pb_ref, pc_ref, pdt_ref, pal_ref = refs[pos:pos + 4]
            pos += 4
        dx_ref, db_ref, dc_ref, ddt_ref, dal_ref, dstate = refs[pos:]
        g, i = pl.program_id(0), pl.program_id(1)

        @pl.when(i == 0)
        def _():
            dstate[...] = jnp.zeros_like(dstate)

        xv = x_ref[...]
        dyv = dy_ref[...]
        xs = [xv[:, h * P:(h + 1) * P] for h in range(SSD_HPG)]
        dys = [dyv[:, h * P:(h + 1) * P] for h in range(SSD_HPG)]
        states = [st_ref[0, h] for h in range(SSD_HPG)]
        dstates = [dstate[h] for h in range(SSD_HPG)]

        def f(states, xs, Bg, Cg, dtv, al):
            return _ssd_chunk(states, xs, Bg, Cg, dtv, al, reverse=reverse)

        _, vjp = jax.vjp(f, states, xs, b_ref[...], c_ref[...], dt_ref[0], al_ref[0])
        dst, dxs, dB, dC, ddt, dal = vjp((dstates, dys))
        for h in range(SSD_HPG):
            dstate[h] = dst[h]
            dx_ref[:, h * P:(h + 1) * P] = dxs[h] + px_ref[:, h * P:(h + 1) * P]
        if prev is not None:
            dB = dB + pb_ref[...]
            dC = dC + pc_ref[...]
            ddt = ddt + pdt_ref[0]
        db_ref[...] = dB
        dc_ref[...] = dC
        ddt_ref[0] = ddt

        @pl.when(i == 0)
        def _():
            dal_ref[0] = dal + (pal_ref[0] if prev is not None else 0.0)

        @pl.when(i != 0)
        def _():
            dal_ref[0] += dal

    xspec = pl.BlockSpec((Q, GW), lambda g, i: (cidx(i), g))
    nspec_b = pl.BlockSpec((Q, N), lambda g, i: (cidx(i), SSD_DI // N + g))
    nspec_c = pl.BlockSpec((Q, N), lambda g, i: (cidx(i), SSD_DI // N + SSD_GROUPS + g))
    dtspec = pl.BlockSpec((1, Q, 128), lambda g, i: (g, cidx(i), 0))
    alspec = pl.BlockSpec((1, 1, 128), lambda g, i: (g, 0, 0))
    in_specs = [xspec, nspec_b, nspec_c,
                dtspec, alspec,
                pl.BlockSpec((1, SSD_HPG, N, P), lambda g, i: (cidx(i), g, 0, 0)),
                xspec]
    gspec = pl.BlockSpec((Q, N), lambda g, i: (cidx(i), g))
    in_specs.append(xspec)
    args = [act, act, act, dt, alog, states, dy, prev_x]
    if prev is not None:
        in_specs += [gspec, gspec, dtspec, alspec]
        args += list(prev)
    outs = pl.pallas_call(
        body, name=f"ssd_scan_bwd_{int(reverse)}", grid=(SSD_GROUPS, nc), in_specs=in_specs,
        out_specs=[pl.BlockSpec((Q, GW), lambda g, i: (cidx(i), g)),
                   pl.BlockSpec((Q, N), lambda g, i: (cidx(i), g)),
                   pl.BlockSpec((Q, N), lambda g, i: (cidx(i), g)),
                   dtspec, alspec],
        out_shape=[jax.ShapeDtypeStruct((S, SSD_DI), F32), jax.ShapeDtypeStruct((S, SSD_GROUPS * N), F32),
                   jax.ShapeDtypeStruct((S, SSD_GROUPS * N), F32),
                   jax.ShapeDtypeStruct((SSD_GROUPS, S, 128), F32), jax.ShapeDtypeStruct((SSD_GROUPS, 1, 128), F32)],
        scratch_shapes=[pltpu.VMEM((SSD_HPG, N, P), F32)],
        compiler_params=_cp("arbitrary", "arbitrary"),
    )(*args)
    return outs


def _ssd_chunk(state, x, Bg, Cg, dt, alog, *, reverse):
    Q, P = SSD_CHUNK, SSD_HEADDIM
    r = lax.broadcasted_iota(jnp.int32, (Q, Q), 0)
    c = lax.broadcasted_iota(jnp.int32, (Q, Q), 1)
    keep = (c >= r) if reverse else (c <= r)
    tri = jnp.where(keep, 1.0, 0.0).astype(F32)
    a = dt * (-jnp.exp(alog))
    cum = _dot_exact(tri, a, ((1,), (0,)))
    cum_t = jnp.transpose(cum)
    last = 0 if reverse else Q - 1
    CB = _dot(Cg, Bg, ((1,), (1,)))
    yoff = _dot(Cg, state, ((1,), (0,)))
    ys, xdecs, keeps = [], [], []
    for h in range(SSD_HPG):
        col = h + (SSD_HPG if reverse else 0)
        hs = slice(h * P, (h + 1) * P)
        cum_c = cum[:, col:col + 1]
        cum_r = cum_t[col:col + 1, :]
        cum_l = cum_c[last:last + 1, :]
        L = jnp.where(keep, jnp.exp(jnp.where(keep, cum_c - cum_r, 0.0)), 0.0)
        xdt = x[:, hs] * dt[:, col:col + 1]
        ys.append(_dot(CB * L, xdt, ((1,), (0,))) + jnp.exp(cum_c) * yoff[:, hs])
        xdecs.append(xdt * jnp.exp(cum_l - cum_c))
        keeps.append(jnp.exp(cum_l) + jnp.zeros((1, P), F32))
    new_state = jnp.concatenate(keeps, axis=1) * state + _dot(Bg, jnp.concatenate(xdecs, axis=1), ((0,), (0,)))
    return new_state, jnp.concatenate(ys, axis=1)


def ssd_scan_fwd(act, dt, alog, *, reverse, y_prev=None, comm=None):
    S = act.shape[0]
    Q, N, P = SSD_CHUNK, SSD_STATE, SSD_HEADDIM
    nc = S // Q
    GW = SSD_HPG * P

    def cidx(i):
        return (nc - 1 - i) if reverse else i

    def body(*refs):
        if y_prev is None:
            x_ref, b_ref, c_ref, dt_ref, al_ref, y_ref, st_ref, state = refs
            yp_ref = None
        else:
            x_ref, b_ref, c_ref, dt_ref, al_ref, yp_ref, y_ref, st_ref, state = refs
        i = pl.program_id(1)

        @pl.when(i == 0)
        def _():
            state[...] = jnp.zeros_like(state)

        st = state[...]
        st_ref[0, 0] = st
        ns, y = _ssd_chunk(st, x_ref[...], b_ref[...], c_ref[...], dt_ref[0], al_ref[0], reverse=reverse)
        state[...] = ns
        y_ref[...] = y if yp_ref is None else y + yp_ref[...]

    xspec = pl.BlockSpec((Q, GW), lambda g, i: (cidx(i), g))
    in_specs = [xspec,
                pl.BlockSpec((Q, N), lambda g, i: (cidx(i), SSD_DI // N + g)),
                pl.BlockSpec((Q, N), lambda g, i: (cidx(i), SSD_DI // N + SSD_GROUPS + g)),
                pl.BlockSpec((1, Q, 128), lambda g, i: (g, cidx(i), 0)),
                pl.BlockSpec((1, 1, 128), lambda g, i: (g, 0, 0))]
    args = [act, act, act, dt, alog]
    if y_prev is not None:
        in_specs.append(xspec)
        args.append(y_prev)
    return pcall(
        body, comm, name=f"ssd_scan_fwd_{int(reverse)}", grid=(SSD_GROUPS, nc), in_specs=in_specs,
        out_specs=[xspec, pl.BlockSpec((1, 1, N, GW), lambda g, i: (cidx(i), g, 0, 0))],
        out_shape=[jax.ShapeDtypeStruct((S, SSD_DI), F32), jax.ShapeDtypeStruct((nc, SSD_GROUPS, N, GW), F32)],
        scratch_shapes=[pltpu.VMEM((N, GW), F32)],
        compiler_params=_cp("arbitrary", "arbitrary"),
    )(*args)


def ssd_scan_bwd(act, dt, alog, states, dy, prev_x, *, reverse, prev=None, comm=None):
    S = act.shape[0]
    Q, N, P = SSD_CHUNK, SSD_STATE, SSD_HEADDIM
    nc = S // Q
    GW = SSD_HPG * P

    def cidx(i):
        return i if reverse else (nc - 1 - i)

    def body(*refs):
        x_ref, b_ref, c_ref, dt_ref, al_ref, st_ref, dy_ref, px_ref = refs[:8]
        pos = 8
        if prev is not None:
            pb_ref, pc_ref, pdt_ref, pal_ref = refs[pos:pos + 4]
            pos += 4
        dx_ref, db_ref, dc_ref, ddt_ref, dal_ref, dstate = refs[pos:]
        i = pl.program_id(1)

        @pl.when(i == 0)
        def _():
            dstate[...] = jnp.zeros_like(dstate)

        _, vjp = jax.vjp(functools.partial(_ssd_chunk, reverse=reverse), st_ref[0, 0], x_ref[...], b_ref[...],
                         c_ref[...], dt_ref[0], al_ref[0])
        dst, dx, dB, dC, ddt, dal = vjp((dstate[...], dy_ref[...]))
        dstate[...] = dst
        dx_ref[...] = dx + px_ref[...]
        if prev is not None:
            dB = dB + pb_ref[...]
            dC = dC + pc_ref[...]
            ddt = ddt + pdt_ref[0]
        db_ref[...] = dB
        dc_ref[...] = dC
        ddt_ref[0] = ddt

        @pl.when(i == 0)
        def _():
            dal_ref[0] = dal + (pal_ref[0] if prev is not None else 0.0)

        @pl.when(i != 0)
        def _():
            dal_ref[0] += dal

    xspec = pl.BlockSpec((Q, GW), lambda g, i: (cidx(i), g))
    gspec = pl.BlockSpec((Q, N), lambda g, i: (cidx(i), g))
    dtspec = pl.BlockSpec((1, Q, 128), lambda g, i: (g, cidx(i), 0))
    alspec = pl.BlockSpec((1, 1, 128), lambda g, i: (g, 0, 0))
    in_specs = [xspec,
                pl.BlockSpec((Q, N), lambda g, i: (cidx(i), SSD_DI // N + g)),
                pl.BlockSpec((Q, N), lambda g, i: (cidx(i), SSD_DI // N + SSD_GROUPS + g)),
                dtspec, alspec, pl.BlockSpec((1, 1, N, GW), lambda g, i: (cidx(i), g, 0, 0)), xspec, xspec]
    args = [act, act, act, dt, alog, states, dy, prev_x]
    if prev is not None:
        in_specs += [gspec, gspec, dtspec, alspec]
        args += list(prev)
    return pcall(
        body, comm, name=f"ssd_scan_bwd_{int(reverse)}", grid=(SSD_GROUPS, nc), in_specs=in_specs,
        out_specs=[xspec, gspec, gspec, dtspec, alspec],
        out_shape=[jax.ShapeDtypeStruct((S, SSD_DI), F32), jax.ShapeDtypeStruct((S, SSD_GROUPS * N), F32),
                   jax.ShapeDtypeStruct((S, SSD_GROUPS * N), F32),
                   jax.ShapeDtypeStruct((SSD_GROUPS, S, 128), F32), jax.ShapeDtypeStruct((SSD_GROUPS, 1, 128), F32)],
        scratch_shapes=[pltpu.VMEM((N, GW), F32)],
        compiler_params=_cp("arbitrary", "arbitrary"),
    )(*args)


def _ssd_post_fn(y, xs, z, dexp, ng):
    t = (y + xs * dexp) * _silu(z)
    return (_rms(t, ng),)


def _hg_chunk(state, qraw, fraw, v, lb, *, reverse):
    C = HG_CHUNK
    r = lax.broadcasted_iota(jnp.int32, (C, C), 0)
    c = lax.broadcasted_iota(jnp.int32, (C, C), 1)
    keep = (c >= r) if reverse else (c <= r)
    tri = jnp.where(keep, 1.0, 0.0).astype(F32)
    q = _silu(qraw)
    f = lb + (1.0 - lb) * jax.nn.sigmoid(fraw)
    k = 1.0 - f
    g = jnp.log(f)
    G = _dot_exact(tri, g, ((1,), (0,)))
    ref_row = C // 2 - 1 if reverse else C // 2
    last_row = 0 if reverse else C - 1
    Gr = G[ref_row:ref_row + 1, :]
    Gl = G[last_row:last_row + 1, :]
    q_t = q * jnp.exp(G - Gr)
    k_t = k * jnp.exp(Gr - G)
    att = jnp.where(keep, _dot(q_t, k_t, ((1,), (1,))), 0.0)
    o = _dot(att, v, ((1,), (0,))) + _dot(q * jnp.exp(G), state, ((1,), (0,)))
    kd = k * jnp.exp(Gl - G)
    new_state = jnp.transpose(jnp.exp(Gl)) * state + _dot(kd, v, ((0,), (0,)))
    return new_state, o


def hg_scan_fwd(u, lb, *, reverse, o_prev=None, rows=256, comm=None):
    S = u.shape[0]
    nh = HG_HEADS
    rows = min(rows, S)
    nsteps = S // rows
    ncb = rows // HG_CHUNK
    f_sec = 2 if reverse else 1

    def blk(i):
        return (nsteps - 1 - i) if reverse else i

    def body(*refs):
        if o_prev is None:
            q_ref, f_ref, v_ref, lb_ref, o_ref, st_ref, state = refs
            op_ref = None
        else:
            q_ref, f_ref, v_ref, lb_ref, op_ref, o_ref, st_ref, state = refs
        i = pl.program_id(0)

        @pl.when(i == 0)
        def _():
            state[...] = jnp.zeros_like(state)

        def chunk(cc, carry):
            ci = (ncb - 1 - cc) if reverse else cc
            sl = pl.ds(pl.multiple_of(ci * HG_CHUNK, HG_CHUNK), HG_CHUNK)
            for h in range(nh):
                hs = slice(h * HG_D, (h + 1) * HG_D)
                st = state[h]
                st_ref[ci, h] = st
                ns, o = _hg_chunk(st, q_ref[sl, hs], f_ref[sl, hs], v_ref[sl, hs], lb_ref[:, hs], reverse=reverse)
                state[h] = ns
                if op_ref is not None:
                    o = o + op_ref[sl, hs]
                o_ref[sl, hs] = o
            return carry

        lax.fori_loop(0, ncb, chunk, 0)

    rowspec = lambda sec: pl.BlockSpec((rows, HG_W), lambda i: (blk(i), sec))
    in_specs = [rowspec(0), rowspec(f_sec), rowspec(3), pl.BlockSpec((1, HG_W), lambda i: (0, 0))]
    args = [u, u, u, lb]
    if o_prev is not None:
        in_specs.append(rowspec(0))
        args.append(o_prev)
    return pcall(
        body, comm, name=f"hg_scan_fwd_{int(reverse)}", grid=(nsteps,), in_specs=in_specs,
        out_specs=[rowspec(0), pl.BlockSpec((ncb, nh, HG_D, HG_D), lambda i: (blk(i), 0, 0, 0))],
        out_shape=[jax.ShapeDtypeStruct((S, HG_W), F32), jax.ShapeDtypeStruct((S // HG_CHUNK, nh, HG_D, HG_D), F32)],
        scratch_shapes=[pltpu.VMEM((nh, HG_D, HG_D), F32)],
        compiler_params=_cp("arbitrary"),
    )(*args)


def hg_scan_bwd(u, lb, states, do, *, reverse, prev=None, rows=256, comm=None):
    S = u.shape[0]
    nh = HG_HEADS
    rows = min(rows, S)
    nsteps = S // rows
    ncb = rows // HG_CHUNK
    f_sec = 2 if reverse else 1

    def blk(i):
        return i if reverse else (nsteps - 1 - i)

    def body(*refs):
        q_ref, f_ref, v_ref, lb_ref, st_ref, do_ref = refs[:6]
        pos = 6
        if prev is not None:
            pq_ref, pv_ref, plb_ref = refs[pos:pos + 3]
            pos += 3
        dq_ref, df_ref, dv_ref, dlb_ref, dstate = refs[pos:]
        i = pl.program_id(0)

        @pl.when(i == 0)
        def _():
            dstate[...] = jnp.zeros_like(dstate)
            dlb_ref[...] = plb_ref[...] if prev is not None else jnp.zeros_like(dlb_ref)

        def chunk(cc, carry):
            ci = cc if reverse else (ncb - 1 - cc)
            sl = pl.ds(pl.multiple_of(ci * HG_CHUNK, HG_CHUNK), HG_CHUNK)
            for h in range(nh):
                hs = slice(h * HG_D, (h + 1) * HG_D)
                _, vjp = jax.vjp(functools.partial(_hg_chunk, reverse=reverse), st_ref[ci, h],
                                 q_ref[sl, hs], f_ref[sl, hs], v_ref[sl, hs], lb_ref[:, hs])
                dst, dq, df, dv, dlb = vjp((dstate[h], do_ref[sl, hs]))
                dstate[h] = dst
                if prev is not None:
                    dq = dq + pq_ref[sl, hs]
                    dv = dv + pv_ref[sl, hs]
                dq_ref[sl, hs] = dq
                df_ref[sl, hs] = df
                dv_ref[sl, hs] = dv
                dlb_ref[:, hs] += dlb
            return carry

        lax.fori_loop(0, ncb, chunk, 0)

    rowspec = lambda sec: pl.BlockSpec((rows, HG_W), lambda i: (blk(i), sec))
    lbspec = pl.BlockSpec((1, HG_W), lambda i: (0, 0))
    in_specs = [rowspec(0), rowspec(f_sec), rowspec(3), lbspec,
                pl.BlockSpec((ncb, nh, HG_D, HG_D), lambda i: (blk(i), 0, 0, 0)), rowspec(0)]
    args = [u, u, u, lb, states, do]
    if prev is not None:
        in_specs += [rowspec(0), rowspec(0), lbspec]
        args += list(prev)
    return pcall(
        body, comm, name=f"hg_scan_bwd_{int(reverse)}", grid=(nsteps,), in_specs=in_specs,
        out_specs=[rowspec(0), rowspec(0), rowspec(0), lbspec],
        out_shape=[jax.ShapeDtypeStruct((S, HG_W), F32)] * 3 + [jax.ShapeDtypeStruct((1, HG_W), F32)],
        scratch_shapes=[pltpu.VMEM((nh, HG_D, HG_D), F32)],
        compiler_params=_cp("arbitrary"),
    )(*args)


def _hg_lb_fn(lbp):
    m = jnp.max(lbp, axis=0, keepdims=True)
    e = jnp.exp(lbp - m)
    sm = e / jnp.sum(e, axis=0, keepdims=True)
    return ((sm[0:1] + sm[1:2]) - sm[0:1],)


def hg_lb_fwd(lbp):
    def body(x_ref, o_ref):
        o_ref[...] = _hg_lb_fn(x_ref[...])[0]

    return pl.pallas_call(body, name="hg_lb_fwd", out_shape=jax.ShapeDtypeStruct((1, HG_W), F32))(lbp)


def hg_lb_bwd(lbp, dlb):
    def body(x_ref, d_ref, o_ref):
        _, vjp = jax.vjp(_hg_lb_fn, x_ref[...])
        o_ref[...] = vjp((d_ref[...],))[0]

    return pl.pallas_call(body, name="hg_lb_bwd", out_shape=jax.ShapeDtypeStruct(lbp.shape, F32))(lbp, dlb)


def _hg_post_fn(o, gate, ng):
    return (_rms(o, ng) * _silu(gate),)


def _gate_fn(o, gate):
    return (o * _silu(gate),)


def _rope_tables(S):
    t = np.arange(S)
    row = (t // GRID_W).astype(np.float32)
    col = (t % GRID_W).astype(np.float32)
    half = AT_HD // 4
    inv = (ROPE_THETA ** (-np.arange(0, 2 * half, 2, dtype=np.float32) / np.float32(2 * half))).astype(np.float32)
    ar = row[:, None] * inv[None, :]
    ac = col[:, None] * inv[None, :]
    return ar.astype(np.float32), ac.astype(np.float32)


def _rope_swap_matrix():
    p = np.zeros((AT_HD, AT_HD), np.float32)
    for i in range(AT_HD):
        p[(i + 32) if (i % 64) < 32 else (i - 32), i] = 1.0
    return p


@jax.custom_vjp
def _half_swap(x):
    ax = x.ndim - 1
    lane = lax.broadcasted_iota(jnp.int32, x.shape, ax)
    return jnp.where((lane & 32) == 0, pltpu.roll(x, 96, ax), pltpu.roll(x, 32, ax))


_half_swap.defvjp(lambda x: (_half_swap(x), None), lambda _, ct: (_half_swap(ct),))


def _make_qk_fn(scale):
    def fn(x, ct, st, g):
        n = _rms(x, g)
        return ((n * ct + _half_swap(n) * st) * scale,)
    return fn


def flash_fwd(q, k, v, *, v_col0=0, tq=512, tk=512):
    S = q.shape[0]
    tq, tk = min(tq, S), min(tk, S)
    nk = S // tk
    G = AT_HEADS // AT_KV

    def body(q_ref, k_ref, v_ref, o_ref, lse_ref, m_s, l_s, acc):
        ki = pl.program_id(2)

        @pl.when(ki == 0)
        def _():
            m_s[...] = jnp.full_like(m_s, -jnp.inf)
            l_s[...] = jnp.zeros_like(l_s)
            acc[...] = jnp.zeros_like(acc)

        kv, vv = k_ref[...], v_ref[...]
        for g in range(G):
            s = _dot(q_ref[:, g * AT_HD:(g + 1) * AT_HD], kv, ((1,), (1,)))
            m_old = m_s[g]
            m_new = jnp.maximum(m_old, jnp.max(s, axis=1, keepdims=True))
            alpha = jnp.exp(m_old - m_new)
            p = jnp.exp(s - m_new)
            l_s[g] = alpha * l_s[g] + jnp.sum(p, axis=1, keepdims=True)
            acc[g] = alpha * acc[g] + _dot(p, vv, ((1,), (0,)))
            m_s[g] = m_new

        @pl.when(ki == nk - 1)
        def _():
            for g in range(G):
                o_ref[:, g * AT_HD:(g + 1) * AT_HD] = acc[g] / l_s[g]
                lse_ref[0, :, g:g + 1] = m_s[g] + jnp.log(l_s[g])

    return pl.pallas_call(
        body, name="flash_fwd", grid=(AT_KV, S // tq, nk),
        in_specs=[pl.BlockSpec((tq, G * AT_HD), lambda h, i, j: (i, h)),
                  pl.BlockSpec((tk, AT_HD), lambda h, i, j: (j, h)),
                  pl.BlockSpec((tk, AT_HD), lambda h, i, j: (j, v_col0 + h))],
        out_specs=[pl.BlockSpec((tq, G * AT_HD), lambda h, i, j: (i, h)),
                   pl.BlockSpec((1, tq, G), lambda h, i, j: (h, i, 0))],
        out_shape=[jax.ShapeDtypeStruct((S, AT_QW), F32), jax.ShapeDtypeStruct((AT_KV, S, G), F32)],
        scratch_shapes=[pltpu.VMEM((G, tq, 1), F32), pltpu.VMEM((G, tq, 1), F32), pltpu.VMEM((G, tq, AT_HD), F32)],
        compiler_params=_cp("parallel", "parallel", "arbitrary"),
    )(q, k, v)


def flash_bwd_dq(q, k, v, o, lse, do, *, v_col0=0, tq=512, tk=512):
    S = q.shape[0]
    tq, tk = min(tq, S), min(tk, S)
    nk = S // tk
    G = AT_HEADS // AT_KV

    def body(q_ref, k_ref, v_ref, o_ref, lse_ref, do_ref, dq_ref, dl_ref, acc, dl_s):
        ki = pl.program_id(2)

        @pl.when(ki == 0)
        def _():
            acc[...] = jnp.zeros_like(acc)
            for g in range(G):
                sl = slice(g * AT_HD, (g + 1) * AT_HD)
                dl_s[g] = jnp.sum(do_ref[:, sl] * o_ref[:, sl], axis=1, keepdims=True)

        kv, vv = k_ref[...], v_ref[...]
        for g in range(G):
            sl = slice(g * AT_HD, (g + 1) * AT_HD)
            s = _dot(q_ref[:, sl], kv, ((1,), (1,)))
            p = jnp.exp(s - lse_ref[0, :, g:g + 1])
            dp = _dot(do_ref[:, sl], vv, ((1,), (1,)))
            ds = p * (dp - dl_s[g])
            acc[g] += _dot(ds, kv, ((1,), (0,)))

        @pl.when(ki == nk - 1)
        def _():
            for g in range(G):
                dq_ref[:, g * AT_HD:(g + 1) * AT_HD] = acc[g]
                dl_ref[0, :, g:g + 1] = dl_s[g]

    qspec = pl.BlockSpec((tq, G * AT_HD), lambda h, i, j: (i, h))
    kspec = pl.BlockSpec((tk, AT_HD), lambda h, i, j: (j, h))
    lspec = pl.BlockSpec((1, tq, G), lambda h, i, j: (h, i, 0))
    return pl.pallas_call(
        body, name="flash_bwd_dq", grid=(AT_KV, S // tq, nk),
        in_specs=[qspec, kspec, pl.BlockSpec((tk, AT_HD), lambda h, i, j: (j, v_col0 + h)), qspec, lspec, qspec],
        out_specs=[qspec, lspec],
        out_shape=[jax.ShapeDtypeStruct((S, AT_QW), F32), jax.ShapeDtypeStruct((AT_KV, S, G), F32)],
        scratch_shapes=[pltpu.VMEM((G, tq, AT_HD), F32), pltpu.VMEM((G, tq, 1), F32)],
        compiler_params=_cp("parallel", "parallel", "arbitrary"),
    )(q, k, v, o, lse, do)


def flash_bwd_dkv(q, k, v, lse, delta, do, *, v_col0=0, tq=512, tk=512):
    S = q.shape[0]
    tq, tk = min(tq, S), min(tk, S)
    nq = S // tq
    G = AT_HEADS // AT_KV

    def body(q_ref, k_ref, v_ref, lse_ref, dl_ref, do_ref, dk_ref, dv_ref, dk_acc, dv_acc):
        qi = pl.program_id(2)

        @pl.when(qi == 0)
        def _():
            dk_acc[...] = jnp.zeros_like(dk_acc)
            dv_acc[...] = jnp.zeros_like(dv_acc)

        kv, vv = k_ref[...], v_ref[...]
        for g in range(G):
            sl = slice(g * AT_HD, (g + 1) * AT_HD)
            qg, dog = q_ref[:, sl], do_ref[:, sl]
            s = _dot(qg, kv, ((1,), (1,)))
            p = jnp.exp(s - lse_ref[0, :, g:g + 1])
            dv_acc[...] += _dot(p, dog, ((0,), (0,)))
            dp = _dot(dog, vv, ((1,), (1,)))
            ds = p * (dp - dl_ref[0, :, g:g + 1])
            dk_acc[...] += _dot(ds, qg, ((0,), (0,)))

        @pl.when(qi == nq - 1)
        def _():
            dk_ref[...] = dk_acc[...]
            dv_ref[...] = dv_acc[...]

    qspec = pl.BlockSpec((tq, G * AT_HD), lambda h, j, i: (i, h))
    kspec = pl.BlockSpec((tk, AT_HD), lambda h, j, i: (j, h))
    lspec = pl.BlockSpec((1, tq, G), lambda h, j, i: (h, i, 0))
    return pl.pallas_call(
        body, name="flash_bwd_dkv", grid=(AT_KV, S // tk, nq),
        in_specs=[qspec, kspec, pl.BlockSpec((tk, AT_HD), lambda h, j, i: (j, v_col0 + h)), lspec, lspec, qspec],
        out_specs=[kspec, kspec],
        out_shape=[jax.ShapeDtypeStruct((S, AT_KW), F32), jax.ShapeDtypeStruct((S, AT_KW), F32)],
        scratch_shapes=[pltpu.VMEM((tk, AT_HD), F32), pltpu.VMEM((tk, AT_HD), F32)],
        compiler_params=_cp("parallel", "parallel", "arbitrary"),
    )(q, k, v, lse, delta, do)


def flash_fwd(q, k, v, *, v_col0=0, tq=256):
    S = q.shape[0]
    tq = min(tq, S)
    G = AT_HEADS // AT_KV

    def body(q_ref, k_ref, v_ref, o_ref, lse_ref):
        kv, vv = k_ref[...], v_ref[...]
        for g in range(G):
            sl = slice(g * AT_HD, (g + 1) * AT_HD)
            s = _dot(q_ref[:, sl], kv, ((1,), (1,)))
            m = jnp.max(s, axis=1, keepdims=True)
            p = jnp.exp(s - m)
            l = jnp.sum(p, axis=1, keepdims=True)
            o_ref[:, sl] = _dot(p, vv, ((1,), (0,))) / l
            lse_ref[0, :, g:g + 1] = m + jnp.log(l)

    return pl.pallas_call(
        body, name="flash_fwd", grid=(AT_KV, S // tq),
        in_specs=[pl.BlockSpec((tq, G * AT_HD), lambda h, i: (i, h)),
                  pl.BlockSpec((S, AT_HD), lambda h, i: (0, h)),
                  pl.BlockSpec((S, AT_HD), lambda h, i: (0, v_col0 + h))],
        out_specs=[pl.BlockSpec((tq, G * AT_HD), lambda h, i: (i, h)),
                   pl.BlockSpec((1, tq, G), lambda h, i: (h, i, 0))],
        out_shape=[jax.ShapeDtypeStruct((S, AT_QW), F32), jax.ShapeDtypeStruct((AT_KV, S, G), F32)],
        compiler_params=_cp("parallel", "arbitrary"),
    )(q, k, v)


def flash_bwd_dq(q, k, v, o, lse, do, *, v_col0=0, tq=256):
    S = q.shape[0]
    tq = min(tq, S)
    G = AT_HEADS // AT_KV

    def body(q_ref, k_ref, v_ref, o_ref, lse_ref, do_ref, dq_ref, dl_ref):
        kv, vv = k_ref[...], v_ref[...]
        for g in range(G):
            sl = slice(g * AT_HD, (g + 1) * AT_HD)
            dog = do_ref[:, sl]
            delta = jnp.sum(dog * o_ref[:, sl], axis=1, keepdims=True)
            s = _dot(q_ref[:, sl], kv, ((1,), (1,)))
            p = jnp.exp(s - lse_ref[0, :, g:g + 1])
            dp = _dot(dog, vv, ((1,), (1,)))
            ds = p * (dp - delta)
            dq_ref[:, sl] = _dot(ds, kv, ((1,), (0,)))
            dl_ref[0, :, g:g + 1] = delta

    qspec = pl.BlockSpec((tq, G * AT_HD), lambda h, i: (i, h))
    lspec = pl.BlockSpec((1, tq, G), lambda h, i: (h, i, 0))
    return pl.pallas_call(
        body, name="flash_bwd_dq", grid=(AT_KV, S // tq),
        in_specs=[qspec, pl.BlockSpec((S, AT_HD), lambda h, i: (0, h)),
                  pl.BlockSpec((S, AT_HD), lambda h, i: (0, v_col0 + h)), qspec, lspec, qspec],
        out_specs=[qspec, lspec],
        out_shape=[jax.ShapeDtypeStruct((S, AT_QW), F32), jax.ShapeDtypeStruct((AT_KV, S, G), F32)],
        compiler_params=_cp("parallel", "arbitrary"),
    )(q, k, v, o, lse, do)


def flash_bwd_dkv(q, k, v, lse, delta, do, *, v_col0=0, tk=512, comm=None):
    S = q.shape[0]
    tk = min(tk, S)
    G = AT_HEADS // AT_KV

    def body(q_ref, k_ref, v_ref, lse_ref, dl_ref, do_ref, dk_ref, dv_ref):
        kv, vv = k_ref[...], v_ref[...]
        dk = jnp.zeros((tk, AT_HD), F32)
        dv = jnp.zeros((tk, AT_HD), F32)
        for g in range(G):
            sl = slice(g * AT_HD, (g + 1) * AT_HD)
            qg, dog = q_ref[:, sl], do_ref[:, sl]
            s = _dot(qg, kv, ((1,), (1,)))
            p = jnp.exp(s - lse_ref[0, :, g:g + 1])
            dv = dv + _dot(p, dog, ((0,), (0,)))
            dp = _dot(dog, vv, ((1,), (1,)))
            ds = p * (dp - dl_ref[0, :, g:g + 1])
            dk = dk + _dot(ds, qg, ((0,), (0,)))
        dk_ref[...] = dk
        dv_ref[...] = dv

    qspec = pl.BlockSpec((S, G * AT_HD), lambda h, j: (0, h))
    kspec = pl.BlockSpec((tk, AT_HD), lambda h, j: (j, h))
    lspec = pl.BlockSpec((1, S, G), lambda h, j: (h, 0, 0))
    return pcall(
        body, comm, name="flash_bwd_dkv", grid=(AT_KV, S // tk),
        in_specs=[qspec, kspec, pl.BlockSpec((tk, AT_HD), lambda h, j: (j, v_col0 + h)), lspec, lspec, qspec],
        out_specs=[kspec, kspec],
        out_shape=[jax.ShapeDtypeStruct((S, AT_KW), F32), jax.ShapeDtypeStruct((S, AT_KW), F32)],
        compiler_params=_cp("parallel", "arbitrary"),
    )(q, k, v, lse, delta, do)


def _t5_bucket_np(rel):
    half = REL_BUCKETS // 2
    exact = half // 2
    n = np.abs(rel)
    large = exact + (np.log(np.maximum(n, 1).astype(np.float32) / np.float32(exact))
                     / np.float32(math.log(REL_MAX_DIST / exact)) * np.float32(half - exact)).astype(np.int32)
    large = np.minimum(large, half - 1)
    return np.where(rel > 0, half, 0) + np.where(n < exact, n, large)


def _dl_tiles(Ls):
    T = min(128, Ls)
    return T, T + 2 * DL_HALF


def _dl_bucket_tables(dil, T):
    W = T + 2 * DL_HALF
    i = np.arange(T)[:, None]
    j = np.arange(W)[None, :]
    bq = _t5_bucket_np((j - DL_HALF - i) * dil)
    iw = np.arange(W)[:, None]
    jk = np.arange(T)[None, :]
    bk = _t5_bucket_np((jk + DL_HALF - iw) * dil)
    return bq.astype(np.int32), bk.astype(np.int32)


def band_fwd(q, kp, vp, bias, *, scale):
    H, dil, Ls, E = q.shape
    T, W = _dl_tiles(Ls)

    def body(q_ref, k_ref, v_ref, b_ref, o_ref, lse_ref):
        n = pl.program_id(2)
        r0 = pl.multiple_of(n * T, T)
        kw = k_ref[0, 0, pl.ds(r0, W), :]
        vw = v_ref[0, 0, pl.ds(r0, W), :]
        i = lax.broadcasted_iota(jnp.int32, (T, W), 0)
        j = lax.broadcasted_iota(jnp.int32, (T, W), 1)
        kpos = n * T + j - DL_HALF
        mask = (jnp.abs(j - DL_HALF - i) <= DL_HALF) & (kpos >= 0) & (kpos < Ls)
        s = _dot(q_ref[0, 0], kw, ((1,), (1,))) * scale + b_ref[0]
        s = jnp.where(mask, s, NEG_BIG)
        m = jnp.max(s, axis=1, keepdims=True)
        lse = m + jnp.log(jnp.sum(jnp.exp(s - m), axis=1, keepdims=True))
        p = jnp.exp(s - lse)
        o_ref[0, 0] = _dot(p, vw, ((1,), (0,)))
        lse_ref[0, 0] = lse

    return pl.pallas_call(
        body, name=f"band_fwd_{dil}", grid=(H, dil, Ls // T),
        in_specs=[pl.BlockSpec((1, 1, T, E), lambda h, d, n: (h, d, n, 0)),
                  pl.BlockSpec((1, 1, Ls + 2 * DL_HALF, E), lambda h, d, n: (h, d, 0, 0)),
                  pl.BlockSpec((1, 1, Ls + 2 * DL_HALF, E), lambda h, d, n: (h, d, 0, 0)),
                  pl.BlockSpec((1, T, W), lambda h, d, n: (h, 0, 0))],
        out_specs=[pl.BlockSpec((1, 1, T, E), lambda h, d, n: (h, d, n, 0)),
                   pl.BlockSpec((1, 1, T, 1), lambda h, d, n: (h, d, n, 0))],
        out_shape=[jax.ShapeDtypeStruct((H, dil, Ls, E), F32), jax.ShapeDtypeStruct((H, dil, Ls, 1), F32)],
        compiler_params=_cp("parallel", "parallel", "arbitrary"),
    )(q, kp, vp, bias)


def band_bwd_dq(q, kp, vp, bias, lse, dm, do, *, scale):
    H, dil, Ls, E = q.shape
    T, W = _dl_tiles(Ls)

    def body(q_ref, k_ref, v_ref, b_ref, lse_ref, dm_ref, do_ref, dq_ref, db_ref):
        d, n = pl.program_id(1), pl.program_id(2)
        r0 = pl.multiple_of(n * T, T)
        kw = k_ref[0, 0, pl.ds(r0, W), :]
        vw = v_ref[0, 0, pl.ds(r0, W), :]
        i = lax.broadcasted_iota(jnp.int32, (T, W), 0)
        j = lax.broadcasted_iota(jnp.int32, (T, W), 1)
        kpos = n * T + j - DL_HALF
        mask = (jnp.abs(j - DL_HALF - i) <= DL_HALF) & (kpos >= 0) & (kpos < Ls)
        s = _dot(q_ref[0, 0], kw, ((1,), (1,))) * scale + b_ref[0]
        p = jnp.where(mask, jnp.exp(jnp.where(mask, s, 0.0) - lse_ref[0, 0]), 0.0)
        dp = _dot(do_ref[0, 0], vw, ((1,), (1,)))
        ds = p * (dp - dm_ref[0, 0])
        dq_ref[0, 0] = _dot(ds, kw, ((1,), (0,))) * scale
        first = jnp.logical_and(d == 0, n == 0)

        @pl.when(first)
        def _():
            db_ref[0] = ds

        @pl.when(jnp.logical_not(first))
        def _():
            db_ref[0] += ds

    qspec = pl.BlockSpec((1, 1, T, E), lambda h, d, n: (h, d, n, 0))
    kspec = pl.BlockSpec((1, 1, Ls + 2 * DL_HALF, E), lambda h, d, n: (h, d, 0, 0))
    rspec = pl.BlockSpec((1, 1, T, 1), lambda h, d, n: (h, d, n, 0))
    bspec = pl.BlockSpec((1, T, W), lambda h, d, n: (h, 0, 0))
    return pl.pallas_call(
        body, name=f"band_bwd_dq_{dil}", grid=(H, dil, Ls // T),
        in_specs=[qspec, kspec, kspec, bspec, rspec, rspec, qspec],
        out_specs=[qspec, bspec],
        out_shape=[jax.ShapeDtypeStruct((H, dil, Ls, E), F32), jax.ShapeDtypeStruct((H, T, W), F32)],
        compiler_params=_cp("arbitrary", "arbitrary", "arbitrary"),
    )(q, kp, vp, bias, lse, dm, do)


def band_bwd_dkv(qp, k, v, bias_t, lsep, dmp, dop, *, scale):
    H, dil, Ls, E = k.shape
    T, W = _dl_tiles(Ls)

    def body(q_ref, k_ref, v_ref, b_ref, lse_ref, dm_ref, do_ref, dk_ref, dv_ref):
        n = pl.program_id(2)
        r0 = pl.multiple_of(n * T, T)
        qw = q_ref[0, 0, pl.ds(r0, W), :]
        dow = do_ref[0, 0, pl.ds(r0, W), :]
        lsew = lse_ref[0, 0, pl.ds(r0, W), :]
        dmw = dm_ref[0, 0, pl.ds(r0, W), :]
        iw = lax.broadcasted_iota(jnp.int32, (W, T), 0)
        j = lax.broadcasted_iota(jnp.int32, (W, T), 1)
        qpos = n * T + iw - DL_HALF
        mask = (jnp.abs(j + DL_HALF - iw) <= DL_HALF) & (qpos >= 0) & (qpos < Ls)
        s = _dot(qw, k_ref[0, 0], ((1,), (1,))) * scale + b_ref[0]
        p = jnp.where(mask, jnp.exp(jnp.where(mask, s, 0.0) - lsew), 0.0)
        dv_ref[0, 0] = _dot(p, dow, ((0,), (0,)))
        dp = _dot(dow, v_ref[0, 0], ((1,), (1,)))
        ds = p * (dp - dmw)
        dk_ref[0, 0] = _dot(ds, qw, ((0,), (0,))) * scale

    kspec = pl.BlockSpec((1, 1, T, E), lambda h, d, n: (h, d, n, 0))
    wspec = pl.BlockSpec((1, 1, Ls + 2 * DL_HALF, E), lambda h, d, n: (h, d, 0, 0))
    w1spec = pl.BlockSpec((1, 1, Ls + 2 * DL_HALF, 1), lambda h, d, n: (h, d, 0, 0))
    return pl.pallas_call(
        body, name=f"band_bwd_dkv_{dil}", grid=(H, dil, Ls // T),
        in_specs=[wspec, kspec, kspec, pl.BlockSpec((1, W, T), lambda h, d, n: (h, 0, 0)), w1spec, w1spec, wspec],
        out_specs=[kspec, kspec],
        out_shape=[jax.ShapeDtypeStruct((H, dil, Ls, E), F32), jax.ShapeDtypeStruct((H, dil, Ls, E), F32)],
        compiler_params=_cp("parallel", "parallel", "arbitrary"),
    )(qp, k, v, bias_t, lsep, dmp, dop)


def _dl_merge_fn(o0, o1, o2, l0, l1, l2):
    m = jnp.maximum(jnp.maximum(l0, l1), l2)
    e0, e1, e2 = jnp.exp(l0 - m), jnp.exp(l1 - m), jnp.exp(l2 - m)
    den = e0 + e1 + e2
    return ((e0 / den) * o0 + (e1 / den) * o1 + (e2 / den) * o2,)


def _adamw_math(w, g, m, v):
    m = ADAM_B1 * m + (1.0 - ADAM_B1) * g
    v = ADAM_B2 * v + (1.0 - ADAM_B2) * (g * g)
    m_hat = m / (1.0 - ADAM_B1 ** ADAM_STEP)
    v_hat = v / (1.0 - ADAM_B2 ** ADAM_STEP)
    delta = -ADAM_LR * (m_hat / (jnp.sqrt(v_hat) + ADAM_EPS) + ADAM_WD * w)
    return delta, m, v


def adamw_sum(parts, w, m, v, *, name, R=128):
    rows, cols = w.shape
    R = min(R, rows)
    if rows % R:
        R = rows

    def body(p_ref, w_ref, m_ref, v_ref, g_ref, d_ref, nm_ref, nv_ref):
        g = p_ref[0].astype(F32)
        for s in range(1, N_DEV):
            g = g + p_ref[s].astype(F32)
        d, nm, nv = _adamw_math(w_ref[...], g, m_ref[...], v_ref[...])
        g_ref[...] = g
        d_ref[...] = d
        nm_ref[...] = nm
        nv_ref[...] = nv

    spec = pl.BlockSpec((R, cols), lambda i: (i, 0))
    return pl.pallas_call(
        body, name=name, grid=(rows // R,),
        in_specs=[pl.BlockSpec((N_DEV, R, cols), lambda i: (0, i, 0)), spec, spec, spec],
        out_specs=[spec] * 4, out_shape=[jax.ShapeDtypeStruct((rows, cols), F32)] * 4,
        compiler_params=_cp("parallel"),
    )(parts, w, m, v)


def sum_parts(parts, *, name):
    rows, cols = parts.shape[1:]

    def body(p_ref, o_ref):
        g = p_ref[0]
        for s in range(1, N_DEV):
            g = g + p_ref[s]
        o_ref[...] = g

    return pl.pallas_call(body, name=name, out_shape=jax.ShapeDtypeStruct((rows, cols), F32))(parts)


def adamw_plain(w, g, m, v, *, name):
    def body(w_ref, g_ref, m_ref, v_ref, d_ref, nm_ref, nv_ref):
        d, nm, nv = _adamw_math(w_ref[...], g_ref[...], m_ref[...], v_ref[...])
        d_ref[...] = d
        nm_ref[...] = nm
        nv_ref[...] = nv

    return pl.pallas_call(body, name=name, out_shape=[jax.ShapeDtypeStruct(w.shape, F32)] * 3)(w, g, m, v)


def _my_pos():
    return lax.axis_index("x"), lax.axis_index("y"), lax.axis_index("c")


def _flat(px, py, pc):
    return 4 * px + 2 * py + pc


def allgather_two_level(x, *, name):
    R, C = x.shape

    def body(x_ref, out_ref, send_sems, recv_sems, local_sem):
        x_, y_, c_ = _my_pos()
        me, sibling = (x_, y_, c_), (x_, y_, 1 - c_)
        chips = [(1 - x_, y_), (x_, 1 - y_), (1 - x_, 1 - y_)]

        def rows(p):
            return out_ref.at[_flat(*p)]

        def copy(k, block, to, src=None):
            return pltpu.make_async_remote_copy(
                src_ref=rows(block) if src is None else src, dst_ref=rows(block),
                send_sem=send_sems.at[k], recv_sem=recv_sems.at[k], device_id=to, device_id_type=MESH_ID)

        mine = pltpu.make_async_copy(x_ref, rows(me), local_sem)
        mine.start()
        first = [copy(0, me, sibling, src=x_ref)]
        first += [copy(1 + j, me, (*chip, c_), src=x_ref) for j, chip in enumerate(chips)]
        for cp in first:
            cp.start()
        passed = [copy(4 + j, (*chip, c_), sibling) for j, chip in enumerate(chips)]
        for j, chip in enumerate(chips):
            copy(1 + j, (*chip, c_), me).wait_recv()
            passed[j].start()
        copy(0, sibling, me).wait_recv()
        for j, chip in enumerate(chips):
            copy(4 + j, (*chip, 1 - c_), me).wait_recv()
        for cp in first + passed:
            cp.wait_send()
        mine.wait()

    return pl.pallas_call(
        body, name=name,
        out_shape=jax.ShapeDtypeStruct((N_DEV, R, C), x.dtype),
        in_specs=[pl.BlockSpec(memory_space=pl.ANY)],
        out_specs=pl.BlockSpec(memory_space=pl.ANY),
        scratch_shapes=[pltpu.SemaphoreType.DMA((7,)), pltpu.SemaphoreType.DMA((7,)), pltpu.SemaphoreType.DMA],
    )(x)


def all_to_all(bufs, *, name):
    nb = len(bufs)

    def body(*refs):
        in_refs = refs[:nb]
        out_refs = refs[nb:2 * nb]
        send_sems, recv_sems, local_sems = refs[2 * nb:]
        x_, y_, c_ = _my_pos()
        me = _flat(x_, y_, c_)
        peers = []
        for k in range(1, N_DEV):
            fx, fy, fc = (k >> 2) & 1, (k >> 1) & 1, k & 1
            peers.append(((1 - x_) if fx else x_, (1 - y_) if fy else y_, (1 - c_) if fc else c_))
        copies = []
        for b in range(nb):
            loc = pltpu.make_async_copy(in_refs[b].at[me], out_refs[b].at[me], local_sems.at[b])
            loc.start()
            copies.append(loc)
        remote = []
        for b in range(nb):
            for k, p in enumerate(peers):
                cp = pltpu.make_async_remote_copy(
                    src_ref=in_refs[b].at[_flat(*p)], dst_ref=out_refs[b].at[me],
                    send_sem=send_sems.at[b, k], recv_sem=recv_sems.at[b, k], device_id=p, device_id_type=MESH_ID)
                cp.start()
                remote.append((b, k, p))
        for b, k, p in remote:
            pltpu.make_async_remote_copy(
                src_ref=in_refs[b].at[me], dst_ref=out_refs[b].at[_flat(*p)],
                send_sem=send_sems.at[b, k], recv_sem=recv_sems.at[b, k], device_id=p, device_id_type=MESH_ID).wait()
        for loc in copies:
            loc.wait()

    return pl.pallas_call(
        body, name=name,
        out_shape=[jax.ShapeDtypeStruct(b.shape, b.dtype) for b in bufs],
        in_specs=[pl.BlockSpec(memory_space=pl.ANY)] * nb,
        out_specs=[pl.BlockSpec(memory_space=pl.ANY)] * nb,
        scratch_shapes=[pltpu.SemaphoreType.DMA((nb, 7)), pltpu.SemaphoreType.DMA((nb, 7)), pltpu.SemaphoreType.DMA((nb,))],
    )(*bufs)


def _prenorm(tag, x, ng):
    return rowwise_fwd(f"{tag}_prenorm", _prenorm_fn, [(x, 0, False)], [(ng, False)], [(D_MODEL, MXU_DTYPE)], W=D_MODEL)[0]


def _cat_mxu(parts):
    return jnp.concatenate([t.astype(MXU_DTYPE) for t in parts], axis=1)


def _in_out_bwd(tag, x, ng, hn, du, w_in, dx):
    dhn = matmul(du, w_in, tb=True, name=f"{tag}_dhn")
    dw_in = matmul(hn, du, ta=True, out_dtype=GRAD_WIRE_DTYPE, name=f"{tag}_dw_in")
    dx_prev, dng = rowwise_bwd(f"{tag}_prenorm_bwd", _prenorm_fn, [(x, 0, False)], [(ng, False)], [dhn],
                               W=D_MODEL, diff_rows=[0], diff_shared=[0], add=dx)
    return dx_prev, dng, dw_in


def _own(res, comm):
    return (res, None) if comm is None else res


def ssd_layer_fwd(x, ng, p, comm=None):
    hn = _prenorm("ssd", x, ng)
    u = matmul(hn, p["w_in"], name="ssd_in")
    act = ssd_conv_fwd(u, p["conv_w"], p["conv_b"])
    dt = rowwise_fwd("ssd_dt", _dt_fn, [(u, (SSD_DI + SSD_CONV_CH) // 128, False)], [(p["dt_bias"], False)],
                     [(128, F32)], W=128)[0]
    dt, alog = _ssd_group_layout(dt), _ssd_group_layout(p["alog"])
    (y0, st0), cres = _own(ssd_scan_fwd(act, dt, alog, reverse=False, comm=comm), comm)
    y, st1 = ssd_scan_fwd(act, dt, alog, reverse=True, y_prev=y0)
    g = rowwise_fwd("ssd_post", _ssd_post_fn, [(y, 0, True), (act, 0, True), (u, 0, True)],
                    [(p["dexp"], True), (p["norm_g"], True)], [(512, MXU_DTYPE)], W=512, ncb=SSD_GROUPS)[0]
    xn = matmul(g, p["w_out"], residual=x, name="ssd_out")
    return xn, dict(x=x, ng=ng, hn=hn, u=u, act=act, dt=dt, alog=alog, y=y, st0=st0, st1=st1, g=g), cres


def ssd_layer_bwd(sv, p, dx, comm=None):
    u, act, dt = sv["u"], sv["act"], sv["dt"]
    S = u.shape[0]
    dg = matmul(dx, p["w_out"], tb=True, name="ssd_dg")
    dw_out = matmul(sv["g"], dx, ta=True, out_dtype=GRAD_WIRE_DTYPE, name="ssd_dw_out")
    dy, dxs_skip, dz, ddexp, dnorm = rowwise_bwd(
        "ssd_post_bwd", _ssd_post_fn, [(sv["y"], 0, True), (act, 0, True), (u, 0, True)],
        [(p["dexp"], True), (p["norm_g"], True)], [dg], W=512, ncb=SSD_GROUPS, diff_rows=[0, 1, 2], diff_shared=[0, 1])
    (dxa, dB, dC, ddt, dal), cres = _own(
        ssd_scan_bwd(act, dt, sv["alog"], sv["st0"], dy, dxs_skip, reverse=False, comm=comm), comm)
    dxa, dB, dC, ddt, dal = ssd_scan_bwd(act, dt, sv["alog"], sv["st1"], dy, dxa, reverse=True, prev=(dB, dC, ddt, dal))
    dact = jnp.concatenate([dxa, dB, dC], axis=1)
    dxbc, dconv_w, dconv_b = ssd_conv_bwd(u, p["conv_w"], p["conv_b"], dact)
    ddt_all = _ssd_head_layout(ddt)
    ddt_raw, ddt_bias = rowwise_bwd("ssd_dt_bwd", _dt_fn, [(u, (SSD_DI + SSD_CONV_CH) // 128, False)],
                                    [(p["dt_bias"], False)], [ddt_all], W=128, diff_rows=[0], diff_shared=[0])
    du = _cat_mxu([dz, dxbc, ddt_raw, jnp.zeros((S, SSD_IN_PAD - SSD_IN - 64), F32)])
    dx_prev, dng, dw_in = _in_out_bwd("ssd", sv["x"], sv["ng"], sv["hn"], du, p["w_in"], dx)
    grads = dict(
        w_in=dw_in[:, :SSD_IN], w_out=dw_out, conv_w=dconv_w[:SSD_CONV], conv_b=dconv_b,
        dt_bias=ddt_bias[:, :2 * SSD_HEADS], a_log=_ssd_head_layout(dal)[:, :2 * SSD_HEADS],
        d=ddexp.reshape(SSD_HEADS, SSD_HEADDIM).sum(axis=1)[None, :], norm_g=dnorm, ng=dng)
    return dx_prev, grads, cres


def hg_layer_fwd(x, ng, p, comm0=None, comm1=None):
    hn = _prenorm("hg", x, ng)
    u = matmul(hn, p["w_in"], name="hg_in")
    lb = hg_lb_fwd(p["hgrn_lb"])
    (o0, st0), cres0 = _own(hg_scan_fwd(u, lb, reverse=False, comm=comm0), comm0)
    (o, st1), cres1 = _own(hg_scan_fwd(u, lb, reverse=True, o_prev=o0, comm=comm1), comm1)
    g = rowwise_fwd("hg_post", _hg_post_fn, [(o, 0, True), (u, 4 * HG_HEADS, True)], [(p["norm_g"], True)],
                    [(HG_D, MXU_DTYPE)], W=HG_D, ncb=HG_HEADS)[0]
    xn = matmul(g, p["w_out"], residual=x, name="hg_out")
    return xn, dict(x=x, ng=ng, hn=hn, u=u, lb=lb, o=o, st0=st0, st1=st1, g=g), cres0, cres1


def hg_layer_bwd(sv, p, dx, comm=None):
    u, lb = sv["u"], sv["lb"]
    dg = matmul(dx, p["w_out"], tb=True, name="hg_dg")
    dw_out = matmul(sv["g"], dx, ta=True, out_dtype=GRAD_WIRE_DTYPE, name="hg_dw_out")
    do, dgate, dnorm = rowwise_bwd("hg_post_bwd", _hg_post_fn, [(sv["o"], 0, True), (u, 4 * HG_HEADS, True)],
                                   [(p["norm_g"], True)], [dg], W=HG_D, ncb=HG_HEADS, diff_rows=[0, 1], diff_shared=[0])
    (dq0, df0, dv0, dlb0), cres = _own(hg_scan_bwd(u, lb, sv["st0"], do, reverse=False, comm=comm), comm)
    dq, df1, dv, dlb = hg_scan_bwd(u, lb, sv["st1"], do, reverse=True, prev=(dq0, dv0, dlb0))
    du = _cat_mxu([dq, df0, df1, dv, dgate])
    dhgrn_lb = hg_lb_bwd(p["hgrn_lb"], dlb)
    dx_prev, dng, dw_in = _in_out_bwd("hg", sv["x"], sv["ng"], sv["hn"], du, p["w_in"], dx)
    return dx_prev, dict(w_in=dw_in, w_out=dw_out, norm_g=dnorm, hgrn_lb=dhgrn_lb, ng=dng), cres


def _rope_consts(S):
    ar, ac = _rope_tables(S)
    ct = np.concatenate([np.cos(ar), np.cos(ar), np.cos(ac), np.cos(ac)], axis=1).astype(np.float32)
    st = np.concatenate([-np.sin(ar), np.sin(ar), -np.sin(ac), np.sin(ac)], axis=1).astype(np.float32)
    return jnp.asarray(ct), jnp.asarray(st)


def _at_qk(tag, u, col0, nheads, scale, gain, consts, cot=None):
    ct, st = consts
    rows = [(u, col0, True), (ct, 0, False), (st, 0, False)]
    shared = [(gain, False)]
    if cot is None:
        return rowwise_fwd(f"at_{tag}", _make_qk_fn(scale), rows, shared, [(AT_HD, MXU_DTYPE)], W=AT_HD, ncb=nheads)[0]
    return rowwise_bwd(f"at_{tag}_bwd", _make_qk_fn(scale), rows, shared, [cot], W=AT_HD, ncb=nheads,
                       diff_rows=[0], diff_shared=[0])


def at_layer_fwd(x, ng, p):
    S = x.shape[0]
    hn = _prenorm("at", x, ng)
    u = matmul(hn, p["w_in"], name="at_in")
    consts = _rope_consts(S)
    qr = _at_qk("q", u, 0, AT_HEADS, AT_HD ** -0.5, p["q_g"], consts)
    kr = _at_qk("k", u, AT_HEADS, AT_KV, 1.0, p["k_g"], consts)
    vc0 = (AT_QW + AT_KW) // AT_HD
    o, lse = flash_fwd(qr, kr, u, v_col0=vc0)
    g = rowwise_fwd("at_gate", _gate_fn, [(o, 0, True), (u, (AT_QW + 2 * AT_KW) // 1024, True)], [],
                    [(1024, MXU_DTYPE)], W=1024, ncb=AT_QW // 1024)[0]
    xn = matmul(g, p["w_out"], residual=x, name="at_out")
    return xn, dict(x=x, ng=ng, hn=hn, u=u, qr=qr, kr=kr, o=o, lse=lse, g=g)


def at_layer_bwd(sv, p, dx, comm=None):
    u, qr, kr = sv["u"], sv["qr"], sv["kr"]
    S = u.shape[0]
    consts = _rope_consts(S)
    vc0 = (AT_QW + AT_KW) // AT_HD
    dg = matmul(dx, p["w_out"], tb=True, name="at_dg")
    dw_out = matmul(sv["g"], dx, ta=True, out_dtype=GRAD_WIRE_DTYPE, name="at_dw_out")
    do, dgate = rowwise_bwd("at_gate_bwd", _gate_fn, [(sv["o"], 0, True), (u, (AT_QW + 2 * AT_KW) // 1024, True)], [],
                            [dg], W=1024, ncb=AT_QW // 1024, diff_rows=[0, 1], diff_shared=[])
    dqs, delta = flash_bwd_dq(qr, kr, u, sv["o"], sv["lse"], do, v_col0=vc0)
    (dkr, dv), cres = _own(flash_bwd_dkv(qr, kr, u, sv["lse"], delta, do, v_col0=vc0, comm=comm), comm)
    dq_raw, dqg = _at_qk("q", u, 0, AT_HEADS, AT_HD ** -0.5, p["q_g"], consts, cot=dqs)
    dk_raw, dkg = _at_qk("k", u, AT_HEADS, AT_KV, 1.0, p["k_g"], consts, cot=dkr)
    du = _cat_mxu([dq_raw, dk_raw, dv, dgate])
    dx_prev, dng, dw_in = _in_out_bwd("at", sv["x"], sv["ng"], sv["hn"], du, p["w_in"], dx)
    return dx_prev, dict(w_in=dw_in, w_out=dw_out, q_g=dqg, k_g=dkg, ng=dng), cres


def _to_stream(t, dil):
    S = t.shape[0]
    return t.reshape(S // dil, dil, DL_HEADS, DL_HD).transpose(2, 1, 0, 3)


def _from_stream(t):
    H, dil, Ls, E = t.shape
    return t.transpose(2, 1, 0, 3).reshape(Ls * dil, H * E)


def _stream_to_hm(t):
    H, dil, Ls, w = t.shape
    return t.transpose(0, 2, 1, 3).reshape(H * Ls * dil, w)


def _hm_to_stream(t, dil):
    w = t.shape[1]
    S = t.shape[0] // DL_HEADS
    return t.reshape(DL_HEADS, S // dil, dil, w).transpose(0, 2, 1, 3)


def _pad_l(t):
    return jnp.pad(t, ((0, 0), (0, 0), (DL_HALF, DL_HALF), (0, 0)))


OX_LSE = DL_HD
DOX_LSE, DOX_DM = DL_HD, DL_HD + 32


def _win(p_ref, c_ref, n_ref, h, T):
    return jnp.concatenate([p_ref[h, 0, T - DL_HALF:T, :], c_ref[h, 0], n_ref[h, 0, 0:DL_HALF, :]], axis=0)


def _win_specs(T, E, nb):
    return [pl.BlockSpec((DL_HEADS, 1, T, E), lambda d, n: (0, d, jnp.maximum(n - 1, 0), 0)),
            pl.BlockSpec((DL_HEADS, 1, T, E), lambda d, n: (0, d, n, 0)),
            pl.BlockSpec((DL_HEADS, 1, T, E), lambda d, n: (0, d, jnp.minimum(n + 1, nb - 1), 0))]


def _band_mask_q(n, T, W, Ls):
    i = lax.broadcasted_iota(jnp.int32, (T, W), 0)
    j = lax.broadcasted_iota(jnp.int32, (T, W), 1)
    kpos = n * T + j - DL_HALF
    return (jnp.abs(j - DL_HALF - i) <= DL_HALF) & (kpos >= 0) & (kpos < Ls)


def band_fwd(q, k, v, bias, *, scale):
    H, dil, Ls, E = q.shape
    T, W = _dl_tiles(Ls)
    nb = Ls // T

    def body(q_ref, kp_ref, kc_ref, kn_ref, vp_ref, vc_ref, vn_ref, b_ref, ox_ref):
        n = pl.program_id(1)
        mask = _band_mask_q(n, T, W, Ls)
        for h in range(H):
            kw = _win(kp_ref, kc_ref, kn_ref, h, T)
            vw = _win(vp_ref, vc_ref, vn_ref, h, T)
            s = _dot(q_ref[h, 0], kw, ((1,), (1,))) * scale + b_ref[h]
            s = jnp.where(mask, s, NEG_BIG)
            m = jnp.max(s, axis=1, keepdims=True)
            lse = m + jnp.log(jnp.sum(jnp.exp(s - m), axis=1, keepdims=True))
            p = jnp.exp(s - lse)
            ox_ref[h, 0, :, 0:E] = _dot(p, vw, ((1,), (0,)))
            ox_ref[h, 0, :, E:2 * E] = lse + jnp.zeros((T, E), F32)

    cur = pl.BlockSpec((H, 1, T, E), lambda d, n: (0, d, n, 0))
    return pl.pallas_call(
        body, name=f"band_fwd_{dil}", grid=(dil, nb),
        in_specs=[cur] + _win_specs(T, E, nb) + _win_specs(T, E, nb) + [pl.BlockSpec((H, T, W), lambda d, n: (0, 0, 0))],
        out_specs=pl.BlockSpec((H, 1, T, 2 * E), lambda d, n: (0, d, n, 0)),
        out_shape=jax.ShapeDtypeStruct((H, dil, Ls, 2 * E), F32),
        compiler_params=_cp("parallel", "parallel"),
    )(q, k, k, k, v, v, v, bias)


def band_bwd_dq(q, k, v, bias, dox, *, scale):
    H, dil, Ls, E = q.shape
    T, W = _dl_tiles(Ls)
    nb = Ls // T

    def body(q_ref, kp_ref, kc_ref, kn_ref, vp_ref, vc_ref, vn_ref, b_ref, dox_ref, dq_ref, db_ref):
        d, n = pl.program_id(0), pl.program_id(1)
        mask = _band_mask_q(n, T, W, Ls)
        first = jnp.logical_and(d == 0, n == 0)

        @pl.when(first)
        def _():
            db_ref[...] = jnp.zeros_like(db_ref)

        for h in range(H):
            kw = _win(kp_ref, kc_ref, kn_ref, h, T)
            vw = _win(vp_ref, vc_ref, vn_ref, h, T)
            dox = dox_ref[h, 0]
            do, lse, dm = dox[:, 0:E], dox[:, DOX_LSE:DOX_LSE + 1], dox[:, DOX_DM:DOX_DM + 1]
            s = _dot(q_ref[h, 0], kw, ((1,), (1,))) * scale + b_ref[h]
            p = jnp.where(mask, jnp.exp(jnp.where(mask, s, 0.0) - lse), 0.0)
            dp = _dot(do, vw, ((1,), (1,)))
            ds = p * (dp - dm)
            dq_ref[h, 0] = (_dot(ds, kw, ((1,), (0,))) * scale).astype(dq_ref.dtype)
            db_ref[h] += ds

    cur = pl.BlockSpec((H, 1, T, E), lambda d, n: (0, d, n, 0))
    bspec = pl.BlockSpec((H, T, W), lambda d, n: (0, 0, 0))
    return pl.pallas_call(
        body, name=f"band_bwd_dq_{dil}", grid=(dil, nb),
        in_specs=[cur] + _win_specs(T, E, nb) + _win_specs(T, E, nb) + [bspec,
                  pl.BlockSpec((H, 1, T, 2 * E), lambda d, n: (0, d, n, 0))],
        out_specs=[cur, bspec],
        out_shape=[jax.ShapeDtypeStruct((H, dil, Ls, E), MXU_DTYPE), jax.ShapeDtypeStruct((H, T, W), F32)],
        compiler_params=_cp("arbitrary", "arbitrary"),
    )(q, k, k, k, v, v, v, bias, dox)


def band_bwd_dkv(q, k, v, bias_t, dox, *, scale):
    H, dil, Ls, E = k.shape
    T, W = _dl_tiles(Ls)
    nb = Ls // T

    def body(qp_ref, qc_ref, qn_ref, k_ref, v_ref, b_ref, dp_ref, dc_ref, dn_ref, dk_ref, dv_ref):
        n = pl.program_id(1)
        iw = lax.broadcasted_iota(jnp.int32, (W, T), 0)
        j = lax.broadcasted_iota(jnp.int32, (W, T), 1)
        qpos = n * T + iw - DL_HALF
        mask = (jnp.abs(j + DL_HALF - iw) <= DL_HALF) & (qpos >= 0) & (qpos < Ls)
        for h in range(H):
            qw = _win(qp_ref, qc_ref, qn_ref, h, T)
            doxw = _win(dp_ref, dc_ref, dn_ref, h, T)
            dow, lsew, dmw = doxw[:, 0:E], doxw[:, DOX_LSE:DOX_LSE + 1], doxw[:, DOX_DM:DOX_DM + 1]
            s = _dot(qw, k_ref[h, 0], ((1,), (1,))) * scale + b_ref[h]
            p = jnp.where(mask, jnp.exp(jnp.where(mask, s, 0.0) - lsew), 0.0)
            dv_ref[h, 0] = _dot(p, dow, ((0,), (0,))).astype(dv_ref.dtype)
            dp = _dot(dow, v_ref[h, 0], ((1,), (1,)))
            ds = p * (dp - dmw)
            dk_ref[h, 0] = (_dot(ds, qw, ((0,), (0,))) * scale).astype(dk_ref.dtype)

    cur = pl.BlockSpec((H, 1, T, E), lambda d, n: (0, d, n, 0))
    return pl.pallas_call(
        body, name=f"band_bwd_dkv_{dil}", grid=(dil, nb),
        in_specs=_win_specs(T, E, nb) + [cur, cur, pl.BlockSpec((H, W, T), lambda d, n: (0, 0, 0))]
        + _win_specs(T, 2 * E, nb),
        out_specs=[cur, cur],
        out_shape=[jax.ShapeDtypeStruct((H, dil, Ls, E), MXU_DTYPE)] * 2,
        compiler_params=_cp("parallel", "parallel"),
    )(q, q, q, k, v, bias_t, dox, dox, dox)


def dl_merge_fwd(oxs, *, R=1024):
    rows = oxs[0].shape[0]
    R = min(R, rows)
    E = DL_HD

    def body(a_ref, b_ref, c_ref, o_ref):
        vals = [r[...] for r in (a_ref, b_ref, c_ref)]
        o_ref[...] = _dl_merge_fn(*[t[:, 0:E] for t in vals], *[t[:, OX_LSE:OX_LSE + 1] for t in vals])[0]

    spec = pl.BlockSpec((R, 2 * E), lambda i: (i, 0))
    return pl.pallas_call(
        body, name="dl_merge", grid=(rows // R,), in_specs=[spec] * 3,
        out_specs=pl.BlockSpec((R, E), lambda i: (i, 0)), out_shape=jax.ShapeDtypeStruct((rows, E), F32),
        compiler_params=_cp("parallel"),
    )(*oxs)


def dl_merge_bwd(oxs, do, *, R=1024):
    rows = oxs[0].shape[0]
    R = min(R, rows)
    E = DL_HD

    def body(a_ref, b_ref, c_ref, do_ref, da_ref, db_ref, dc_ref):
        vals = [r[...] for r in (a_ref, b_ref, c_ref)]
        os_ = [t[:, 0:E] for t in vals]
        ls_ = [t[:, OX_LSE:OX_LSE + 1] for t in vals]
        _, vjp = jax.vjp(_dl_merge_fn, *os_, *ls_)
        g = vjp((do_ref[...],))
        for k, d_ref in enumerate((da_ref, db_ref, dc_ref)):
            dm = jnp.sum(g[k] * os_[k], axis=1, keepdims=True) - g[3 + k]
            d_ref[:, 0:E] = g[k]
            d_ref[:, DOX_LSE:DOX_DM] = ls_[k] + jnp.zeros((R, DOX_DM - DOX_LSE), F32)
            d_ref[:, DOX_DM:2 * E] = dm + jnp.zeros((R, 2 * E - DOX_DM), F32)

    spec = pl.BlockSpec((R, 2 * E), lambda i: (i, 0))
    return pl.pallas_call(
        body, name="dl_merge_bwd", grid=(rows // R,), in_specs=[spec] * 3 + [pl.BlockSpec((R, E), lambda i: (i, 0))],
        out_specs=[spec] * 3, out_shape=[jax.ShapeDtypeStruct((rows, 2 * E), F32)] * 3,
        compiler_params=_cp("parallel"),
    )(*oxs, do)


def _dl_bias_tables(rel_bias, dil, T):
    W = T + 2 * DL_HALF
    bq, bk = _dl_bucket_tables(dil, T)
    idx = np.concatenate([bq.reshape(-1), bk.reshape(-1)])
    onehot_t = (np.arange(REL_BUCKETS)[:, None] == idx[None, :]).astype(np.float32)
    tab = matmul(rel_bias.T, jnp.asarray(onehot_t), exact=True, name=f"dl_bias_{dil}", tm=DL_HEADS, tk=REL_BUCKETS,
                 tn=_tile(2 * T * W, (8192, 4096, 2048, 1024, 512, 256, 128)))
    return tab[:, :T * W].reshape(DL_HEADS, T, W), tab[:, T * W:].reshape(DL_HEADS, W, T), bq


def _dl_dm_fn(do, o, dl):
    return (jnp.sum(do * o, axis=-1, keepdims=True) - dl,)


def _old_dl_layer_fwd(x, ng, p):
    S = x.shape[0]
    hn = _prenorm("dl", x, ng)
    u = matmul(hn, p["w_in"], name="dl_in")
    scale = DL_HD ** -0.5
    per_group, o_hm, lse_hm = [], [], []
    for gi, (window, dil) in enumerate(DL_PAIRS):
        base = gi * 3 * DL_W
        Ls = S // dil
        T, _ = _dl_tiles(Ls)
        bq, bk = _dl_bucket_tables(dil, T)
        qs = _to_stream(u[:, base:base + DL_W], dil).astype(MXU_DTYPE)
        ks = _to_stream(u[:, base + DL_W:base + 2 * DL_W], dil).astype(MXU_DTYPE)
        vs = _to_stream(u[:, base + 2 * DL_W:base + 3 * DL_W], dil).astype(MXU_DTYPE)
        bias = p["rel_bias"][bq].transpose(2, 0, 1)
        o_s, lse_s = band_fwd(qs, _pad_l(ks), _pad_l(vs), bias, scale=scale)
        per_group.append(dict(qs=qs, ks=ks, vs=vs, lse_s=lse_s, bq=bq, bk=bk, dil=dil))
        o_hm.append(_stream_to_hm(o_s))
        lse_hm.append(_stream_to_hm(lse_s))
    rows = [(t, 0, False) for t in o_hm] + [(t, 0, False, 1) for t in lse_hm]
    om = rowwise_fwd("dl_merge", _dl_merge_fn, rows, [], [(DL_HD, F32)], W=DL_HD)[0]
    o = om.reshape(DL_HEADS, S, DL_HD).transpose(1, 0, 2).reshape(S, DL_W)
    g = rowwise_fwd("dl_gate", _gate_fn, [(o, 0, False), (u, 9, False)], [], [(DL_W, MXU_DTYPE)], W=DL_W)[0]
    xn = matmul(g, p["w_out"], residual=x, name="dl_out")
    return xn, dict(x=x, ng=ng, hn=hn, u=u, per_group=per_group, o_hm=o_hm, lse_hm=lse_hm, o=o, g=g)


def _old_dl_layer_bwd(sv, p, dx):
    u = sv["u"]
    S = u.shape[0]
    scale = DL_HD ** -0.5
    dg = matmul(dx, p["w_out"], tb=True, name="dl_dg")
    dw_out = matmul(sv["g"], dx, ta=True, out_dtype=GRAD_WIRE_DTYPE, name="dl_dw_out")
    do, dgate = rowwise_bwd("dl_gate_bwd", _gate_fn, [(sv["o"], 0, False), (sv["ugate"], 0, False)], [], [dg], W=DL_W,
                            diff_rows=[0, 1], diff_shared=[])
    do_hm = do.reshape(S, DL_HEADS, DL_HD).transpose(1, 0, 2).reshape(DL_HEADS * S, DL_HD)
    rows = [(t, 0, False) for t in sv["o_hm"]] + [(t, 0, False, 1) for t in sv["lse_hm"]]
    dmerge = rowwise_bwd("dl_merge_bwd", _dl_merge_fn, rows, [], [do_hm], W=DL_HD, diff_rows=[0, 1, 2, 3, 4, 5],
                         diff_shared=[])
    parts, dbs, onehots = [], [], []
    for gi, pg in enumerate(sv["per_group"]):
        dil = pg["dil"]
        Ls = S // dil
        T, W = _dl_tiles(Ls)
        dog, dlg = dmerge[gi], dmerge[3 + gi]
        dm = rowwise_fwd(f"dl_dm_{gi}", _dl_dm_fn, [(dog, 0, False), (sv["o_hm"][gi], 0, False), (dlg, 0, False, 1)], [],
                         [(1, F32)], W=DL_HD)[0]
        do_s, dm_s = _hm_to_stream(dog, dil), _hm_to_stream(dm, dil)
        bias = p["rel_bias"][pg["bq"]].transpose(2, 0, 1)
        bias_t = p["rel_bias"][pg["bk"]].transpose(2, 0, 1)
        kp, vp = _pad_l(pg["ks"]), _pad_l(pg["vs"])
        dq_s, dbias = band_bwd_dq(pg["qs"], kp, vp, bias, pg["lse_s"], dm_s, do_s, scale=scale)
        dk_s, dv_s = band_bwd_dkv(_pad_l(pg["qs"]), pg["ks"], pg["vs"], bias_t, _pad_l(pg["lse_s"]), _pad_l(dm_s),
                                  _pad_l(do_s), scale=scale)
        parts += [_from_stream(dq_s), _from_stream(dk_s), _from_stream(dv_s)]
        dbs.append(dbias.reshape(DL_HEADS, T * W))
        onehots.append((pg["bq"].reshape(-1)[:, None] == np.arange(REL_BUCKETS)[None, :]).astype(np.float32))
    drel = matmul(jnp.concatenate(dbs, axis=1), jnp.asarray(np.concatenate(onehots, axis=0)), exact=True,
                  name="dl_drel", tm=DL_HEADS, tn=REL_BUCKETS, tk=2048)
    du = _cat_mxu(parts + [dgate])
    dx_prev, dng, dw_in = _in_out_bwd("dl", sv["x"], sv["ng"], sv["hn"], du, p["w_in"], dx)
    return dx_prev, dict(w_in=dw_in, w_out=dw_out, rel_bias=drel.T, ng=dng)


def dl_layer_fwd(x, ng, p):
    S = x.shape[0]
    hn = _prenorm("dl", x, ng)
    nqkv = 3 * len(DL_PAIRS) * DL_W
    uqkv = matmul(hn, p["w_in"], name="dl_in_qkv", b_cols=(0, nqkv), out_dtype=MXU_DTYPE)
    ugate = matmul(hn, p["w_in"], name="dl_in_gate", b_cols=(nqkv, DL_W))
    scale = DL_HD ** -0.5
    per_group, ox_hm = [], []
    for gi, (window, dil) in enumerate(DL_PAIRS):
        base = gi * 3 * DL_W
        T, _ = _dl_tiles(S // dil)
        qs, ks, vs = [_to_stream(uqkv[:, base + c * DL_W:base + (c + 1) * DL_W], dil) for c in range(3)]
        bias, bias_t, bq = _dl_bias_tables(p["rel_bias"], dil, T)
        ox_s = band_fwd(qs, ks, vs, bias, scale=scale)
        per_group.append(dict(qs=qs, ks=ks, vs=vs, bias=bias, bias_t=bias_t, bq=bq, dil=dil))
        ox_hm.append(_stream_to_hm(ox_s))
    om = dl_merge_fwd(ox_hm)
    o = om.reshape(DL_HEADS, S, DL_HD).transpose(1, 0, 2).reshape(S, DL_W)
    g = rowwise_fwd("dl_gate", _gate_fn, [(o, 0, False), (ugate, 0, False)], [], [(DL_W, MXU_DTYPE)], W=DL_W)[0]
    xn = matmul(g, p["w_out"], residual=x, name="dl_out")
    return xn, dict(x=x, ng=ng, hn=hn, ugate=ugate, per_group=per_group, ox_hm=ox_hm, o=o, g=g)


def dl_layer_bwd(sv, p, dx):
    ugate = sv["ugate"]
    S = ugate.shape[0]
    scale = DL_HD ** -0.5
    dg = matmul(dx, p["w_out"], tb=True, name="dl_dg")
    dw_out = matmul(sv["g"], dx, ta=True, out_dtype=GRAD_WIRE_DTYPE, name="dl_dw_out")
    do, dgate = rowwise_bwd("dl_gate_bwd", _gate_fn, [(sv["o"], 0, False), (sv["ugate"], 0, False)], [], [dg], W=DL_W,
                            diff_rows=[0, 1], diff_shared=[])
    do_hm = do.reshape(S, DL_HEADS, DL_HD).transpose(1, 0, 2).reshape(DL_HEADS * S, DL_HD)
    dox_hm = dl_merge_bwd(sv["ox_hm"], do_hm)
    parts, dbs, onehots = [], [], []
    for gi, pg in enumerate(sv["per_group"]):
        dil = pg["dil"]
        T, W = _dl_tiles(S // dil)
        dox_s = _hm_to_stream(dox_hm[gi], dil)
        dq_s, dbias = band_bwd_dq(pg["qs"], pg["ks"], pg["vs"], pg["bias"], dox_s, scale=scale)
        dk_s, dv_s = band_bwd_dkv(pg["qs"], pg["ks"], pg["vs"], pg["bias_t"], dox_s, scale=scale)
        parts += [_from_stream(dq_s), _from_stream(dk_s), _from_stream(dv_s)]
        dbs.append(dbias.reshape(DL_HEADS, T * W))
        onehots.append((pg["bq"].reshape(-1)[:, None] == np.arange(REL_BUCKETS)[None, :]).astype(np.float32))
    drel = matmul(jnp.concatenate(dbs, axis=1), jnp.asarray(np.concatenate(onehots, axis=0)), exact=True,
                  name="dl_drel", tm=DL_HEADS, tn=REL_BUCKETS, tk=2048)
    du = _cat_mxu(parts + [dgate])
    dx_prev, dng, dw_in = _in_out_bwd("dl", sv["x"], sv["ng"], sv["hn"], du, p["w_in"], dx)
    return dx_prev, dict(w_in=dw_in, w_out=dw_out, rel_bias=drel.T, ng=dng)


WEIGHT_ORDER = ['norm_g', 'final_g', 'rel_bias', 'hgrn_lb', 'ssd_w_in', 'ssd_conv_w', 'ssd_conv_b', 'ssd_dt_bias',
                'ssd_a_log', 'ssd_d', 'ssd_norm_g', 'ssd_w_out', 'hg_w_in', 'hg_norm_g', 'hg_w_out', 'at_w_in',
                'at_q_norm_g', 'at_k_norm_g', 'at_w_out', 'dl_w_in', 'dl_w_out']
BIG_IN = ['ssd_w_in', 'hg_w_in', 'at_w_in', 'dl_w_in']
BIG_OUT = ['ssd_w_out', 'hg_w_out', 'at_w_out', 'dl_w_out']
BIG = BIG_IN + BIG_OUT
SMALL = [n for n in WEIGHT_ORDER if n not in BIG]
LANES = 128


def _pack(arrs):
    flat = jnp.concatenate([a.reshape(-1).astype(F32) for a in arrs])
    n = flat.shape[0]
    rows = -(-n // (8 * LANES)) * 8
    return jnp.pad(flat, (0, rows * LANES - n)).reshape(rows, LANES)


def _unpack(buf, shapes):
    flat = buf.reshape(-1)
    out, off = [], 0
    for shp in shapes:
        n = int(np.prod(shp)) if len(shp) else 1
        out.append(flat[off:off + n].reshape(shp))
        off += n
    return out


def kernel(x, norm_g, final_g, rel_bias, hgrn_lb, ssd_w_in, ssd_conv_w, ssd_conv_b, ssd_dt_bias, ssd_a_log, ssd_d, ssd_norm_g, ssd_w_out, hg_w_in, hg_norm_g, hg_w_out, at_w_in, at_q_norm_g, at_k_norm_g, at_w_out, dl_w_in, dl_w_out, loss_target, m_norm_g, m_final_g, m_rel_bias, m_hgrn_lb, m_ssd_w_in, m_ssd_conv_w, m_ssd_conv_b, m_ssd_dt_bias, m_ssd_a_log, m_ssd_d, m_ssd_norm_g, m_ssd_w_out, m_hg_w_in, m_hg_norm_g, m_hg_w_out, m_at_w_in, m_at_q_norm_g, m_at_k_norm_g, m_at_w_out, m_dl_w_in, m_dl_w_out, v_norm_g, v_final_g, v_rel_bias, v_hgrn_lb, v_ssd_w_in, v_ssd_conv_w, v_ssd_conv_b, v_ssd_dt_bias, v_ssd_a_log, v_ssd_d, v_ssd_norm_g, v_ssd_w_out, v_hg_w_in, v_hg_norm_g, v_hg_w_out, v_at_w_in, v_at_q_norm_g, v_at_k_norm_g, v_at_w_out, v_dl_w_in, v_dl_w_out):
    w = dict(norm_g=norm_g, final_g=final_g, rel_bias=rel_bias, hgrn_lb=hgrn_lb, ssd_w_in=ssd_w_in, ssd_conv_w=ssd_conv_w, ssd_conv_b=ssd_conv_b, ssd_dt_bias=ssd_dt_bias, ssd_a_log=ssd_a_log, ssd_d=ssd_d, ssd_norm_g=ssd_norm_g, ssd_w_out=ssd_w_out, hg_w_in=hg_w_in, hg_norm_g=hg_norm_g, hg_w_out=hg_w_out, at_w_in=at_w_in, at_q_norm_g=at_q_norm_g, at_k_norm_g=at_k_norm_g, at_w_out=at_w_out, dl_w_in=dl_w_in, dl_w_out=dl_w_out)
    m = dict(norm_g=m_norm_g, final_g=m_final_g, rel_bias=m_rel_bias, hgrn_lb=m_hgrn_lb, ssd_w_in=m_ssd_w_in, ssd_conv_w=m_ssd_conv_w, ssd_conv_b=m_ssd_conv_b, ssd_dt_bias=m_ssd_dt_bias, ssd_a_log=m_ssd_a_log, ssd_d=m_ssd_d, ssd_norm_g=m_ssd_norm_g, ssd_w_out=m_ssd_w_out, hg_w_in=m_hg_w_in, hg_norm_g=m_hg_norm_g, hg_w_out=m_hg_w_out, at_w_in=m_at_w_in, at_q_norm_g=m_at_q_norm_g, at_k_norm_g=m_at_k_norm_g, at_w_out=m_at_w_out, dl_w_in=m_dl_w_in, dl_w_out=m_dl_w_out)
    v = dict(norm_g=v_norm_g, final_g=v_final_g, rel_bias=v_rel_bias, hgrn_lb=v_hgrn_lb, ssd_w_in=v_ssd_w_in, ssd_conv_w=v_ssd_conv_w, ssd_conv_b=v_ssd_conv_b, ssd_dt_bias=v_ssd_dt_bias, ssd_a_log=v_ssd_a_log, ssd_d=v_ssd_d, ssd_norm_g=v_ssd_norm_g, ssd_w_out=v_ssd_w_out, hg_w_in=v_hg_w_in, hg_norm_g=v_hg_norm_g, hg_w_out=v_hg_w_out, at_w_in=v_at_w_in, at_q_norm_g=v_at_q_norm_g, at_k_norm_g=v_at_k_norm_g, at_w_out=v_at_w_out, dl_w_in=v_dl_w_in, dl_w_out=v_dl_w_out)
    me = 4 * lax.axis_index("x") + 2 * lax.axis_index("y") + lax.axis_index("c")
    xs = x[0]
    S = xs.shape[0]

    shard2d = {n: w[n][0] for n in BIG}
    wire = {n: shard2d[n].astype(MXU_DTYPE) for n in BIG}

    def ag(names):
        return ("ag", [wire[n] for n in names])

    def assemble(names, blks):
        out = {}
        for n, blk in zip(names, blks):
            r, c = shard2d[n].shape
            out[n] = blk.transpose(1, 0, 2).reshape(r, N_DEV * c) if n in BIG_IN else blk.reshape(N_DEV * r, c)
        return out

    def a2a(names, gw):
        bufs = []
        for n in names:
            r, c = shard2d[n].shape
            bufs.append(gw[n].reshape(r, N_DEV, c).transpose(1, 0, 2) if n in BIG_IN else gw[n].reshape(N_DEV, r, c))
        return ("a2a", bufs)

    ssd_w, hg_w, at_w, dl_w = (["ssd_w_in", "ssd_w_out"], ["hg_w_in", "hg_w_out"], ["at_w_in", "at_w_out"],
                               ["dl_w_in", "dl_w_out"])
    full = assemble(ssd_w, comm_only(ag(ssd_w), name="allgather_ssd_weights"))
    ncw = ssd_conv_w.shape[2]
    nhg = hg_norm_g.shape[1]
    small_shard = jnp.zeros((8, 512), F32)
    small_shard = small_shard.at[:SSD_CONV, :ncw].set(ssd_conv_w[0]).at[SSD_CONV, :nhg].set(hg_norm_g[0])
    small_all = allgather_two_level(small_shard, name="allgather_small_weights")
    conv_w_full = small_all[:, :SSD_CONV, :ncw].transpose(1, 0, 2).reshape(SSD_CONV, N_DEV * ncw)
    hg_norm_full = small_all[:, SSD_CONV, :nhg].reshape(1, N_DEV * nhg)

    p_ssd = dict(w_in=jnp.pad(full["ssd_w_in"], ((0, 0), (0, SSD_IN_PAD - SSD_IN))), w_out=full["ssd_w_out"],
                 conv_w=conv_w_full, conv_b=ssd_conv_b,
                 dt_bias=jnp.pad(ssd_dt_bias.reshape(1, 2 * SSD_HEADS), ((0, 0), (0, 128 - 2 * SSD_HEADS))),
                 alog=jnp.pad(ssd_a_log.reshape(1, 2 * SSD_HEADS), ((0, 0), (0, 128 - 2 * SSD_HEADS))),
                 dexp=jnp.repeat(ssd_d.reshape(-1), SSD_HEADDIM)[None, :], norm_g=ssd_norm_g)
    x1, sv0, got = ssd_layer_fwd(xs, norm_g[0:1], p_ssd, comm=ag(hg_w))
    full.update(assemble(hg_w, got))
    p_hg = dict(w_in=full["hg_w_in"], w_out=full["hg_w_out"], norm_g=hg_norm_full, hgrn_lb=hgrn_lb)
    x2, sv1, got_at, got_dl = hg_layer_fwd(x1, norm_g[1:2], p_hg, comm0=ag(at_w), comm1=ag(dl_w))
    full.update(assemble(at_w, got_at))
    full.update(assemble(dl_w, got_dl))
    p_at = dict(w_in=full["at_w_in"], w_out=full["at_w_out"], q_g=at_q_norm_g, k_g=at_k_norm_g)
    p_dl = dict(w_in=full["dl_w_in"], w_out=full["dl_w_out"], rel_bias=rel_bias)
    x3, sv2 = at_layer_fwd(x2, norm_g[2:3], p_at)
    x4, sv3 = dl_layer_fwd(x3, norm_g[3:4], p_dl)
    loss_part, dx4, dfinal = loss_head(x4, final_g[None, :], loss_target[0])
    dx3, g3 = dl_layer_bwd(sv3, p_dl, dx4)
    dx2, g2, recv_dl = at_layer_bwd(sv2, p_at, dx3, comm=a2a(dl_w, dict(dl_w_in=g3["w_in"], dl_w_out=g3["w_out"])))
    dx1, g1, recv_at = hg_layer_bwd(sv1, p_hg, dx2, comm=a2a(at_w, dict(at_w_in=g2["w_in"], at_w_out=g2["w_out"])))
    dx0, g0, recv_hg = ssd_layer_bwd(sv0, p_ssd, dx1, comm=a2a(hg_w, dict(hg_w_in=g1["w_in"], hg_w_out=g1["w_out"])))
    recv_ssd = comm_only(a2a(ssd_w, dict(ssd_w_in=g0["w_in"], ssd_w_out=g0["w_out"])), name="exchange_ssd_grads")
    recv = dict(zip(ssd_w + hg_w + at_w + dl_w, recv_ssd + recv_hg + recv_at + recv_dl))

    small_full = dict(
        norm_g=jnp.concatenate([g0["ng"], g1["ng"], g2["ng"], g3["ng"]], axis=0), final_g=dfinal[0],
        rel_bias=g3["rel_bias"], hgrn_lb=g1["hgrn_lb"], ssd_conv_w=g0["conv_w"][None], ssd_conv_b=g0["conv_b"],
        ssd_dt_bias=g0["dt_bias"].reshape(1, 2, SSD_HEADS), ssd_a_log=g0["a_log"].reshape(1, 2, SSD_HEADS),
        ssd_d=g0["d"], ssd_norm_g=g0["norm_g"], hg_norm_g=g1["norm_g"], at_q_norm_g=g2["q_g"], at_k_norm_g=g2["k_g"])
    packed = _pack([loss_part[0, 0:1]] + [small_full[n] for n in SMALL])
    summed = sum_parts(allgather_two_level(packed, name="allgather_small_grads"), name="sum_small_grads")
    parts = _unpack(summed, [()] + [small_full[n].shape for n in SMALL])
    loss = parts[0]
    gsmall = dict(zip(SMALL, parts[1:]))
    gsmall["ssd_conv_w"] = lax.dynamic_slice_in_dim(gsmall["ssd_conv_w"], me * ncw, ncw, axis=2)
    gsmall["hg_norm_g"] = lax.dynamic_slice_in_dim(gsmall["hg_norm_g"], me * nhg, nhg, axis=1)
    shapes = [w[n].shape for n in SMALL]
    d_p, m_p, v_p = adamw_plain(_pack([w[n] for n in SMALL]), _pack([gsmall[n] for n in SMALL]),
                                _pack([m[n] for n in SMALL]), _pack([v[n] for n in SMALL]), name="adamw_small")
    grads = dict(gsmall)
    deltas = dict(zip(SMALL, _unpack(d_p, shapes)))
    new_m = dict(zip(SMALL, _unpack(m_p, shapes)))
    new_v = dict(zip(SMALL, _unpack(v_p, shapes)))

    for n in BIG:
        gs, ds, ms, vs = adamw_sum(recv[n], shard2d[n], m[n][0], v[n][0], name=f"adamw_{n}")
        grads[n], deltas[n], new_m[n], new_v[n] = gs[None], ds[None], ms[None], vs[None]

    return (loss, dx0[None], *[grads[n] for n in WEIGHT_ORDER], *[deltas[n] for n in WEIGHT_ORDER],
            *[new_m[n] for n in WEIGHT_ORDER], *[new_v[n] for n in WEIGHT_ORDER])
```

```python
import functools
import math

import jax
import jax.numpy as jnp
import numpy as np
from jax import lax
from jax.experimental import pallas as pl
from jax.experimental.pallas import tpu as pltpu

F32 = jnp.float32
BF16 = jnp.bfloat16
MXU_DTYPE = jnp.bfloat16
GRAD_WIRE_DTYPE = jnp.bfloat16
HIGHEST = lax.Precision.HIGHEST
MESH_ID = pl.DeviceIdType.MESH
N_DEV = 8

D_MODEL = 1024
EPS = 1e-6
NEG_BIG = -1e30

SSD_DI = 2048
SSD_HEADDIM = 64
SSD_HEADS = 32
SSD_GROUPS = 4
SSD_HPG = 8
SSD_STATE = 128
SSD_CONV = 7
SSD_CHUNK = 128
SSD_CONV_CH = SSD_DI + 2 * SSD_GROUPS * SSD_STATE
SSD_IN = SSD_DI + SSD_CONV_CH + 2 * SSD_HEADS
SSD_IN_PAD = 5376

HG_CHUNK = 32
HG_HEADS = 8
HG_D = 128
HG_W = 1024

AT_HEADS = 16
AT_KV = 8
AT_HD = 128
AT_QW = 2048
AT_KW = 1024
GRID_W = 64
ROPE_THETA = 10000.0

DL_PAIRS = ((128, 1), (512, 4), (2048, 16))
DL_HEADS = 16
DL_HD = 64
DL_W = 1024
DL_HALF = 64
REL_BUCKETS = 32
REL_MAX_DIST = 1024

ADAM_LR = 0.001
ADAM_B1 = 0.9
ADAM_B2 = 0.999
ADAM_EPS = 1e-08
ADAM_WD = 0.01
ADAM_STEP = 10

VMEM_LIMIT = 56 * 1024 * 1024


def _cp(*sem):
    return pltpu.CompilerParams(dimension_semantics=tuple(sem), vmem_limit_bytes=VMEM_LIMIT)


def _tile(n, cands=(1024, 768, 512, 384, 256, 128)):
    for c in cands:
        if n % c == 0:
            return c
    return n


def _dot(a, b, dims):
    return lax.dot_general(a.astype(MXU_DTYPE), b.astype(MXU_DTYPE), (dims, ((), ())), preferred_element_type=F32)


def _dot_exact(a, b, dims):
    return lax.dot_general(a, b, (dims, ((), ())), precision=HIGHEST, preferred_element_type=F32)


def _silu(x):
    return x * jax.nn.sigmoid(x)


def _my_pos():
    return lax.axis_index("x"), lax.axis_index("y"), lax.axis_index("c")


def _flat(px, py, pc):
    return 4 * px + 2 * py + pc


def _peers():
    x_, y_, c_ = _my_pos()
    out = []
    for k in range(1, N_DEV):
        fx, fy, fc = (k >> 2) & 1, (k >> 1) & 1, k & 1
        out.append(((1 - x_) if fx else x_, (1 - y_) if fy else y_, (1 - c_) if fc else c_))
    return out


def _comm_copies(kind, in_refs, out_refs, send_sems, recv_sems, local_sems):
    me = _flat(*_my_pos())
    local, starts, waits = [], [], []
    for b, (i_ref, o_ref) in enumerate(zip(in_refs, out_refs)):
        local.append(pltpu.make_async_copy(i_ref if kind == "ag" else i_ref.at[me], o_ref.at[me], local_sems.at[b]))
        for k, p in enumerate(_peers()):
            src = i_ref if kind == "ag" else i_ref.at[_flat(*p)]
            starts.append(pltpu.make_async_remote_copy(
                src_ref=src, dst_ref=o_ref.at[me], send_sem=send_sems.at[b, k], recv_sem=recv_sems.at[b, k],
                device_id=p, device_id_type=MESH_ID))
            waits.append(pltpu.make_async_remote_copy(
                src_ref=src, dst_ref=o_ref.at[_flat(*p)], send_sem=send_sems.at[b, k], recv_sem=recv_sems.at[b, k],
                device_id=p, device_id_type=MESH_ID))
    return local, starts, waits


def pcall(body, comm, *, name, grid, in_specs, out_specs, out_shape, scratch_shapes=(), compiler_params=None):
    single = not isinstance(out_specs, (list, tuple))
    out_specs_l = [out_specs] if single else list(out_specs)
    out_shape_l = [out_shape] if single else list(out_shape)
    if comm is None:
        return pl.pallas_call(body, name=name, grid=grid, in_specs=in_specs, out_specs=out_specs, out_shape=out_shape,
                              scratch_shapes=list(scratch_shapes), compiler_params=compiler_params)
    kind, bufs = comm
    nb, n_in, n_out, n_scr = len(bufs), len(in_specs), len(out_specs_l), len(scratch_shapes)
    c_shape = [jax.ShapeDtypeStruct(((N_DEV,) + b.shape) if kind == "ag" else b.shape, b.dtype) for b in bufs]
    anyspec = pl.BlockSpec(memory_space=pl.ANY)

    def body2(*refs):
        ins, c_ins = refs[:n_in], refs[n_in:n_in + nb]
        outs = refs[n_in + nb:n_in + nb + n_out]
        c_outs = refs[n_in + nb + n_out:n_in + 2 * nb + n_out]
        scr = refs[n_in + 2 * nb + n_out:n_in + 2 * nb + n_out + n_scr]
        send_sems, recv_sems, local_sems = refs[n_in + 2 * nb + n_out + n_scr:]
        first = last = None
        for ax, g in enumerate(grid):
            pid = pl.program_id(ax)
            first = (pid == 0) if first is None else jnp.logical_and(first, pid == 0)
            last = (pid == g - 1) if last is None else jnp.logical_and(last, pid == g - 1)

        @pl.when(first)
        def _():
            local, starts, _ = _comm_copies(kind, c_ins, c_outs, send_sems, recv_sems, local_sems)
            for cp in local + starts:
                cp.start()

        body(*ins, *outs, *scr)

        @pl.when(last)
        def _():
            local, _, waits = _comm_copies(kind, c_ins, c_outs, send_sems, recv_sems, local_sems)
            for cp in waits + local:
                cp.wait()

    call = pl.pallas_call(
        body2, name=name, grid=grid, in_specs=list(in_specs) + [anyspec] * nb,
        out_specs=out_specs_l + [anyspec] * nb, out_shape=out_shape_l + c_shape,
        scratch_shapes=list(scratch_shapes) + [pltpu.SemaphoreType.DMA((nb, N_DEV - 1)),
                                               pltpu.SemaphoreType.DMA((nb, N_DEV - 1)), pltpu.SemaphoreType.DMA((nb,))],
        compiler_params=compiler_params)

    def run(*args):
        res = call(*args, *bufs)
        own = res[:n_out]
        return (own[0] if single else list(own)), list(res[n_out:])

    return run


def comm_only(comm, *, name):
    def body(x_ref, o_ref):
        o_ref[...] = x_ref[...]

    spec = pl.BlockSpec((8, 128), lambda i: (0, 0))
    _, res = pcall(body, comm, name=name, grid=(1,), in_specs=[spec], out_specs=spec,
                   out_shape=jax.ShapeDtypeStruct((8, 128), F32))(jnp.zeros((8, 128), F32))
    return res


def matmul(a, b, *, name, ta=False, tb=False, residual=None, out_dtype=F32, exact=False, tm=None, tn=None, tk=None,
           b_cols=None):
    M, K = (a.shape[1], a.shape[0]) if ta else a.shape
    n0, N = b_cols if b_cols is not None else (0, b.shape[0] if tb else b.shape[1])
    tm = tm or _tile(M, (512, 256, 128))
    tn = tn or _tile(N, (1024, 768, 512, 384, 256, 128))
    tk = tk or _tile(K, (1024, 768, 512, 384, 256, 128))
    nk = K // tk
    dims = (((0,) if ta else (1,)), ((1,) if tb else (0,)))

    def body(*refs):
        if residual is None:
            a_ref, b_ref, o_ref, acc = refs
            r_ref = None
        else:
            a_ref, b_ref, r_ref, o_ref, acc = refs
        k = pl.program_id(2)

        @pl.when(k == 0)
        def _():
            acc[...] = jnp.zeros_like(acc)

        if exact:
            acc[...] += _dot_exact(a_ref[...], b_ref[...], dims)
        else:
            acc[...] += _dot(a_ref[...], b_ref[...], dims)

        @pl.when(k == nk - 1)
        def _():
            r = acc[...]
            if r_ref is not None:
                r = r + r_ref[...]
            o_ref[...] = r.astype(o_ref.dtype)

    a_spec = pl.BlockSpec((tk, tm), lambda i, j, k: (k, i)) if ta else pl.BlockSpec((tm, tk), lambda i, j, k: (i, k))
    assert n0 % tn == 0
    jb = n0 // tn
    b_spec = (pl.BlockSpec((tn, tk), lambda i, j, k: (j + jb, k)) if tb
              else pl.BlockSpec((tk, tn), lambda i, j, k: (k, j + jb)))
    in_specs = [a_spec, b_spec]
    args = [a, b]
    if residual is not None:
        in_specs.append(pl.BlockSpec((tm, tn), lambda i, j, k: (i, j)))
        args.append(residual)
    return pl.pallas_call(
        body, name=name, grid=(M // tm, N // tn, nk), in_specs=in_specs,
        out_specs=pl.BlockSpec((tm, tn), lambda i, j, k: (i, j)),
        out_shape=jax.ShapeDtypeStruct((M, N), out_dtype),
        scratch_shapes=[pltpu.VMEM((tm, tn), F32)],
        compiler_params=_cp("parallel", "parallel", "arbitrary"),
    )(*args)


def _row_specs(rows, shared, R, W):
    specs = []
    for arr, col0, per_j, *wd in rows:
        w = wd[0] if wd else W
        specs.append(pl.BlockSpec((R, w), (lambda j, i, c=col0: (i, c + j)) if per_j else (lambda j, i, c=col0: (i, c))))
    for arr, per_j in shared:
        specs.append(pl.BlockSpec((arr.shape[0], W), (lambda j, i: (0, j)) if per_j else (lambda j, i: (0, 0))))
    return specs


def rowwise_fwd(name, fn, rows, shared, outs, *, W, ncb=1, R=256):
    S = rows[0][0].shape[0]
    R = min(R, S)
    nr, ns = len(rows), len(shared)

    def body(*refs):
        vals = [r[...] for r in refs[:nr + ns]]
        res = fn(*vals)
        for o_ref, r in zip(refs[nr + ns:], res):
            o_ref[...] = r.astype(o_ref.dtype)

    return pl.pallas_call(
        body, name=name, grid=(ncb, S // R),
        in_specs=_row_specs(rows, shared, R, W),
        out_specs=[pl.BlockSpec((R, w), lambda j, i: (i, j)) for w, _ in outs],
        out_shape=[jax.ShapeDtypeStruct((S, ncb * w), dt) for w, dt in outs],
        compiler_params=_cp("parallel", "parallel"),
    )(*[r[0] for r in rows], *[s[0] for s in shared])


def rowwise_bwd(name, fn, rows, shared, cots, *, W, ncb=1, R=256, diff_rows, diff_shared, add=None):
    S = rows[0][0].shape[0]
    R = min(R, S)
    nr, ns, nc = len(rows), len(shared), len(cots)
    nsteps = S // R

    def body(*refs):
        ins = refs[:nr + ns]
        ct_refs = refs[nr + ns:nr + ns + nc]
        pos = nr + ns + nc
        add_ref = None
        if add is not None:
            add_ref = refs[pos]
            pos += 1
        drow_refs = refs[pos:pos + len(diff_rows)]
        dsh_refs = refs[pos + len(diff_rows):]
        j, i = pl.program_id(0), pl.program_id(1)
        vals = [r[...] for r in ins]

        def f(*dv):
            full = list(vals)
            for idx, v in zip(list(diff_rows) + [nr + s for s in diff_shared], dv):
                full[idx] = v
            return tuple(fn(*full))

        prim = [vals[idx] for idx in diff_rows] + [vals[nr + s] for s in diff_shared]
        _, vjp = jax.vjp(f, *prim)
        grads = vjp(tuple(c[...] for c in ct_refs))
        for k, d_ref in enumerate(drow_refs):
            g = grads[k]
            if k == 0 and add_ref is not None:
                g = g + add_ref[...]
            d_ref[...] = g
        for k, (d_ref, s) in enumerate(zip(dsh_refs, diff_shared)):
            g = grads[len(diff_rows) + k]
            first = (i == 0) if shared[s][1] else jnp.logical_and(i == 0, j == 0)

            @pl.when(first)
            def _(d_ref=d_ref, g=g):
                d_ref[...] = g

            @pl.when(jnp.logical_not(first))
            def _(d_ref=d_ref, g=g):
                d_ref[...] += g

    in_specs = _row_specs(rows, shared, R, W)
    wo = [c.shape[1] // ncb for c in cots]
    in_specs += [pl.BlockSpec((R, w), lambda j, i: (i, j)) for w in wo]
    args = [r[0] for r in rows] + [s[0] for s in shared] + list(cots)
    if add is not None:
        in_specs.append(pl.BlockSpec((R, W), lambda j, i: (i, j)))
        args.append(add)
    dws = [(rows[r][3] if len(rows[r]) > 3 else W) for r in diff_rows]
    out_specs = [pl.BlockSpec((R, w), lambda j, i: (i, j)) for w in dws]
    out_shape = [jax.ShapeDtypeStruct((S, ncb * w), F32) for w in dws]
    for s in diff_shared:
        arr, per_j = shared[s]
        out_specs.append(pl.BlockSpec((arr.shape[0], W), (lambda j, i: (0, j)) if per_j else (lambda j, i: (0, 0))))
        out_shape.append(jax.ShapeDtypeStruct((arr.shape[0], ncb * W if per_j else W), F32))
    return pl.pallas_call(
        body, name=name, grid=(ncb, nsteps), in_specs=in_specs, out_specs=out_specs, out_shape=out_shape,
        compiler_params=_cp("arbitrary", "arbitrary"),
    )(*args)


def _row_specs2(rows, shared, R, W, ncb):
    specs = []
    for arr, col0, per_j, *wd in rows:
        w = wd[0] if wd else W
        if per_j:
            assert col0 % ncb == 0
            specs.append(pl.BlockSpec((R, ncb * w), lambda i, c=col0 // ncb: (i, c)))
        else:
            specs.append(pl.BlockSpec((R, w), lambda i, c=col0: (i, c)))
    for arr, per_j in shared:
        specs.append(pl.BlockSpec((arr.shape[0], ncb * W if per_j else arr.shape[1]), lambda i: (0, 0)))
    return specs


def _col_block(ref, per_j, j, w):
    return ref[:, j * w:(j + 1) * w] if per_j else ref[...]


def rowwise_fwd(name, fn, rows, shared, outs, *, W, ncb=1, R=256):
    S = rows[0][0].shape[0]
    R = min(R, S)
    nr, ns = len(rows), len(shared)
    widths = [(r[3] if len(r) > 3 else W) for r in rows]
    per_j = [r[2] for r in rows] + [s[1] for s in shared]
    ws = widths + [W] * ns

    def body(*refs):
        for j in range(ncb):
            vals = [_col_block(refs[k], per_j[k], j, ws[k]) for k in range(nr + ns)]
            res = fn(*vals)
            for o_ref, r, (wo, _) in zip(refs[nr + ns:], res, outs):
                o_ref[:, j * wo:(j + 1) * wo] = r.astype(o_ref.dtype)

    return pl.pallas_call(
        body, name=name, grid=(S // R,),
        in_specs=_row_specs2(rows, shared, R, W, ncb),
        out_specs=[pl.BlockSpec((R, ncb * w), lambda i: (i, 0)) for w, _ in outs],
        out_shape=[jax.ShapeDtypeStruct((S, ncb * w), dt) for w, dt in outs],
        compiler_params=_cp("parallel"),
    )(*[r[0] for r in rows], *[s[0] for s in shared])


def rowwise_bwd(name, fn, rows, shared, cots, *, W, ncb=1, R=256, diff_rows, diff_shared, add=None):
    S = rows[0][0].shape[0]
    R = min(R, S)
    nr, ns, nc = len(rows), len(shared), len(cots)
    widths = [(r[3] if len(r) > 3 else W) for r in rows]
    per_j = [r[2] for r in rows] + [s[1] for s in shared]
    ws = widths + [W] * ns
    wo = [c.shape[1] // ncb for c in cots]
    dws = [widths[r] for r in diff_rows]

    def body(*refs):
        ins = refs[:nr + ns]
        ct_refs = refs[nr + ns:nr + ns + nc]
        pos = nr + ns + nc
        add_ref = None
        if add is not None:
            add_ref = refs[pos]
            pos += 1
        drow_refs = refs[pos:pos + len(diff_rows)]
        dsh_refs = refs[pos + len(diff_rows):]
        i = pl.program_id(0)
        tot = [None] * len(diff_shared)
        for j in range(ncb):
            vals = [_col_block(ins[k], per_j[k], j, ws[k]) for k in range(nr + ns)]

            def f(*dv):
                full = list(vals)
                for idx, v in zip(list(diff_rows) + [nr + s for s in diff_shared], dv):
                    full[idx] = v
                return tuple(fn(*full))

            prim = [vals[idx] for idx in diff_rows] + [vals[nr + s] for s in diff_shared]
            _, vjp = jax.vjp(f, *prim)
            grads = vjp(tuple(c[:, j * w:(j + 1) * w] for c, w in zip(ct_refs, wo)))
            for k, (d_ref, w) in enumerate(zip(drow_refs, dws)):
                g = grads[k]
                if k == 0 and add_ref is not None:
                    g = g + add_ref[:, j * w:(j + 1) * w]
                d_ref[:, j * w:(j + 1) * w] = g
            for k, (d_ref, s) in enumerate(zip(dsh_refs, diff_shared)):
                g = grads[len(diff_rows) + k]
                if shared[s][1]:
                    @pl.when(i == 0)
                    def _(d_ref=d_ref, g=g, j=j):
                        d_ref[:, j * W:(j + 1) * W] = g

                    @pl.when(i != 0)
                    def _(d_ref=d_ref, g=g, j=j):
                        d_ref[:, j * W:(j + 1) * W] += g
                else:
                    tot[k] = g if tot[k] is None else tot[k] + g
        for k, (d_ref, s) in enumerate(zip(dsh_refs, diff_shared)):
            if not shared[s][1]:
                @pl.when(i == 0)
                def _(d_ref=d_ref, g=tot[k]):
                    d_ref[...] = g

                @pl.when(i != 0)
                def _(d_ref=d_ref, g=tot[k]):
                    d_ref[...] += g

    in_specs = _row_specs2(rows, shared, R, W, ncb)
    in_specs += [pl.BlockSpec((R, ncb * w), lambda i: (i, 0)) for w in wo]
    args = [r[0] for r in rows] + [s[0] for s in shared] + list(cots)
    if add is not None:
        in_specs.append(pl.BlockSpec((R, ncb * dws[0]), lambda i: (i, 0)))
        args.append(add)
    out_specs = [pl.BlockSpec((R, ncb * w), lambda i: (i, 0)) for w in dws]
    out_shape = [jax.ShapeDtypeStruct((S, ncb * w), F32) for w in dws]
    for s in diff_shared:
        arr, pj = shared[s]
        shp = (arr.shape[0], ncb * W if pj else arr.shape[1])
        out_specs.append(pl.BlockSpec(shp, lambda i: (0, 0)))
        out_shape.append(jax.ShapeDtypeStruct(shp, F32))
    return pl.pallas_call(
        body, name=name, grid=(S // R,), in_specs=in_specs, out_specs=out_specs, out_shape=out_shape,
        compiler_params=_cp("arbitrary"),
    )(*args)


def _rms(x, g):
    return x * lax.rsqrt(jnp.mean(x * x, axis=-1, keepdims=True) + EPS) * g


def _prenorm_fn(x, g):
    return (_rms(x, g),)


def loss_head(x, g, tgt, *, R=256):
    S, D = x.shape
    R = min(R, S)

    def fn(xv, gv, tv):
        err = _rms(xv, gv) - tv
        return 0.5 * jnp.sum(jnp.mean(err * err, axis=-1, keepdims=True), axis=0, keepdims=True)

    def body(x_ref, g_ref, t_ref, loss_ref, dx_ref, dg_ref):
        i = pl.program_id(0)
        tv = t_ref[...]
        val, vjp = jax.vjp(lambda a, b: fn(a, b, tv), x_ref[...], g_ref[...])
        dx, dg = vjp(jnp.ones((1, 1), F32))
        dx_ref[...] = dx

        @pl.when(i == 0)
        def _():
            loss_ref[...] = jnp.zeros_like(loss_ref) + val
            dg_ref[...] = dg

        @pl.when(i != 0)
        def _():
            loss_ref[...] += val
            dg_ref[...] += dg

    return pl.pallas_call(
        body, name="loss_head", grid=(S // R,),
        in_specs=[pl.BlockSpec((R, D), lambda i: (i, 0)), pl.BlockSpec((1, D), lambda i: (0, 0)),
                  pl.BlockSpec((R, D), lambda i: (i, 0))],
        out_specs=[pl.BlockSpec((1, 128), lambda i: (0, 0)), pl.BlockSpec((R, D), lambda i: (i, 0)),
                   pl.BlockSpec((1, D), lambda i: (0, 0))],
        out_shape=[jax.ShapeDtypeStruct((1, 128), F32), jax.ShapeDtypeStruct((S, D), F32),
                   jax.ShapeDtypeStruct((1, D), F32)],
        compiler_params=_cp("arbitrary"),
    )(x, g, tgt)


@jax.custom_vjp
def _softplus(x):
    z = jnp.exp(-jnp.abs(x))
    u = 1.0 + z
    log1p = jnp.where(u == 1.0, z, jnp.log(u) * (z / jnp.where(u == 1.0, 1.0, u - 1.0)))
    return jnp.maximum(x, 0.0) + log1p


def _softplus_fwd(x):
    return _softplus(x), x


def _softplus_bwd(x, ct):
    return (ct * jax.nn.sigmoid(x),)


_softplus.defvjp(_softplus_fwd, _softplus_bwd)


def _dt_fn(raw, bias):
    return (_softplus(raw + bias),)


CONV_CB = 256
CONV_RB = 512
CONV_PAD = 8


def ssd_conv_fwd(u, conv_w, conv_b):
    S = u.shape[0]
    ncb = SSD_CONV_CH // CONV_CB
    col0 = SSD_DI // CONV_CB
    RB = min(CONV_RB, S)

    def body(x_ref, w_ref, b_ref, o_ref, pad):
        pad[0:CONV_PAD, :] = jnp.zeros((CONV_PAD, CONV_CB), F32)
        pad[S + CONV_PAD:S + 2 * CONV_PAD, :] = jnp.zeros((CONV_PAD, CONV_CB), F32)
        pad[CONV_PAD:S + CONV_PAD, :] = x_ref[...]
        w = w_ref[...]
        b = b_ref[...]
        for r in range(S // RB):
            acc = jnp.zeros((RB, CONV_CB), F32) + b
            for k in range(SSD_CONV):
                off = r * RB + CONV_PAD + k - SSD_CONV // 2
                acc = acc + pad[off:off + RB, :] * w[k:k + 1, :]
            o_ref[r * RB:(r + 1) * RB, :] = _silu(acc)

    return pl.pallas_call(
        body, name="ssd_conv_fwd", grid=(ncb,),
        in_specs=[pl.BlockSpec((S, CONV_CB), lambda j: (0, col0 + j)),
                  pl.BlockSpec((SSD_CONV, CONV_CB), lambda j: (0, j)),
                  pl.BlockSpec((1, CONV_CB), lambda j: (0, j))],
        out_specs=pl.BlockSpec((S, CONV_CB), lambda j: (0, j)),
        out_shape=jax.ShapeDtypeStruct((S, SSD_CONV_CH), F32),
        scratch_shapes=[pltpu.VMEM((S + 2 * CONV_PAD, CONV_CB), F32)],
        compiler_params=_cp("parallel"),
    )(u, conv_w, conv_b)


def ssd_conv_bwd(u, conv_w, conv_b, dact):
    S = u.shape[0]
    ncb = SSD_CONV_CH // CONV_CB
    col0 = SSD_DI // CONV_CB
    RB = min(CONV_RB, S)
    half = SSD_CONV // 2

    def body(x_ref, w_ref, b_ref, da_ref, dx_ref, dw_ref, db_ref, xpad, dpad):
        z8 = jnp.zeros((CONV_PAD, CONV_CB), F32)
        xpad[0:CONV_PAD, :] = z8
        xpad[S + CONV_PAD:S + 2 * CONV_PAD, :] = z8
        dpad[0:CONV_PAD, :] = z8
        dpad[S + CONV_PAD:S + 2 * CONV_PAD, :] = z8
        xpad[CONV_PAD:S + CONV_PAD, :] = x_ref[...]
        w = w_ref[...]
        b = b_ref[...]
        dws = [jnp.zeros((1, CONV_CB), F32) for _ in range(SSD_CONV)]
        db = jnp.zeros((1, CONV_CB), F32)
        for r in range(S // RB):
            acc = jnp.zeros((RB, CONV_CB), F32) + b
            xs = []
            for k in range(SSD_CONV):
                off = r * RB + CONV_PAD + k - half
                xk = xpad[off:off + RB, :]
                xs.append(xk)
                acc = acc + xk * w[k:k + 1, :]
            sg = jax.nn.sigmoid(acc)
            dc = da_ref[r * RB:(r + 1) * RB, :] * (sg * (1.0 + acc * (1.0 - sg)))
            dpad[r * RB + CONV_PAD:(r + 1) * RB + CONV_PAD, :] = dc
            db = db + jnp.sum(dc, axis=0, keepdims=True)
            for k in range(SSD_CONV):
                dws[k] = dws[k] + jnp.sum(xs[k] * dc, axis=0, keepdims=True)
        for r in range(S // RB):
            acc = jnp.zeros((RB, CONV_CB), F32)
            for k in range(SSD_CONV):
                off = r * RB + CONV_PAD + half - k
                acc = acc + dpad[off:off + RB, :] * w[k:k + 1, :]
            dx_ref[r * RB:(r + 1) * RB, :] = acc
        for k in range(SSD_CONV):
            dw_ref[k:k + 1, :] = dws[k]
        dw_ref[SSD_CONV:SSD_CONV + 1, :] = jnp.zeros((1, CONV_CB), F32)
        db_ref[...] = db

    return pl.pallas_call(
        body, name="ssd_conv_bwd", grid=(ncb,),
        in_specs=[pl.BlockSpec((S, CONV_CB), lambda j: (0, col0 + j)),
                  pl.BlockSpec((SSD_CONV, CONV_CB), lambda j: (0, j)),
                  pl.BlockSpec((1, CONV_CB), lambda j: (0, j)),
                  pl.BlockSpec((S, CONV_CB), lambda j: (0, j))],
        out_specs=[pl.BlockSpec((S, CONV_CB), lambda j: (0, j)),
                   pl.BlockSpec((SSD_CONV + 1, CONV_CB), lambda j: (0, j)),
                   pl.BlockSpec((1, CONV_CB), lambda j: (0, j))],
        out_shape=[jax.ShapeDtypeStruct((S, SSD_CONV_CH), F32),
                   jax.ShapeDtypeStruct((SSD_CONV + 1, SSD_CONV_CH), F32),
                   jax.ShapeDtypeStruct((1, SSD_CONV_CH), F32)],
        scratch_shapes=[pltpu.VMEM((S + 2 * CONV_PAD, CONV_CB), F32), pltpu.VMEM((S + 2 * CONV_PAD, CONV_CB), F32)],
        compiler_params=_cp("parallel"),
    )(u, conv_w, conv_b, dact)


def _ssd_chunk(states, xs, Bg, Cg, dt, alog, *, reverse):
    Q = SSD_CHUNK
    r = lax.broadcasted_iota(jnp.int32, (Q, Q), 0)
    c = lax.broadcasted_iota(jnp.int32, (Q, Q), 1)
    keep = (c >= r) if reverse else (c <= r)
    tri = jnp.where(keep, 1.0, 0.0).astype(F32)
    a = dt * (-jnp.exp(alog))
    cum = _dot_exact(tri, a, ((1,), (0,)))
    cum_t = jnp.transpose(cum)
    last = 0 if reverse else Q - 1
    CB = _dot(Cg, Bg, ((1,), (1,)))
    new_states, ys = [], []
    for h in range(SSD_HPG):
        col = h + (SSD_HPG if reverse else 0)
        cum_c = cum[:, col:col + 1]
        cum_r = cum_t[col:col + 1, :]
        dt_c = dt[:, col:col + 1]
        cum_l = cum_c[last:last + 1, :]
        diff = jnp.where(keep, cum_c - cum_r, 0.0)
        L = jnp.where(keep, jnp.exp(diff), 0.0)
        xdt = xs[h] * dt_c
        y = _dot(CB * L, xdt, ((1,), (0,)))
        y = y + _dot(Cg * jnp.exp(cum_c), states[h], ((1,), (0,)))
        ns = jnp.exp(cum_l) * states[h] + _dot(Bg * jnp.exp(cum_l - cum_c), xdt, ((0,), (0,)))
        new_states.append(ns)
        ys.append(y)
    return new_states, ys


def _ssd_group_layout(t):
    r = t.shape[0]
    g = t[:, :2 * SSD_HEADS].reshape(r, 2, SSD_GROUPS, SSD_HPG).transpose(2, 0, 1, 3).reshape(SSD_GROUPS, r, 2 * SSD_HPG)
    return jnp.pad(g, ((0, 0), (0, 0), (0, 128 - 2 * SSD_HPG)))


def _ssd_head_layout(t):
    r = t.shape[1]
    h = t[:, :, :2 * SSD_HPG].reshape(SSD_GROUPS, r, 2, SSD_HPG).transpose(1, 2, 0, 3).reshape(r, 2 * SSD_HEADS)
    return jnp.pad(h, ((0, 0), (0, 128 - 2 * SSD_HEADS)))


def ssd_scan_fwd(act, dt, alog, *, reverse, y_prev=None, comm=None):
    S = act.shape[0]
    Q, N, P = SSD_CHUNK, SSD_STATE, SSD_HEADDIM
    nc = S // Q
    GW = SSD_HPG * P

    def cidx(i):
        return (nc - 1 - i) if reverse else i

    def body(*refs):
        if y_prev is None:
            x_ref, b_ref, c_ref, dt_ref, al_ref, y_ref, st_ref, state = refs
            yp_ref = None
        else:
            x_ref, b_ref, c_ref, dt_ref, al_ref, yp_ref, y_ref, st_ref, state = refs
        g, i = pl.program_id(0), pl.program_id(1)

        @pl.when(i == 0)
        def _():
            state[...] = jnp.zeros_like(state)

        states = [state[h] for h in range(SSD_HPG)]
        for h in range(SSD_HPG):
            st_ref[0, h] = states[h]
        xv = x_ref[...]
        xs = [xv[:, h * P:(h + 1) * P] for h in range(SSD_HPG)]
        ns, ys = _ssd_chunk(states, xs, b_ref[...], c_ref[...], dt_ref[0], al_ref[0], reverse=reverse)
        for h in range(SSD_HPG):
            state[h] = ns[h]
            yh = ys[h]
            if yp_ref is not None:
                yh = yh + yp_ref[:, h * P:(h + 1) * P]
            y_ref[:, h * P:(h + 1) * P] = yh

    in_specs = [pl.BlockSpec((Q, GW), lambda g, i: (cidx(i), g)),
                pl.BlockSpec((Q, N), lambda g, i: (cidx(i), SSD_DI // N + g)),
                pl.BlockSpec((Q, N), lambda g, i: (cidx(i), SSD_DI // N + SSD_GROUPS + g)),
                pl.BlockSpec((1, Q, 128), lambda g, i: (g, cidx(i), 0)),
                pl.BlockSpec((1, 1, 128), lambda g, i: (g, 0, 0))]
    args = [act, act, act, dt, alog]
    if y_prev is not None:
        in_specs.append(pl.BlockSpec((Q, GW), lambda g, i: (cidx(i), g)))
        args.append(y_prev)
    return pcall(
        body, comm, name=f"ssd_scan_fwd_{int(reverse)}", grid=(SSD_GROUPS, nc), in_specs=in_specs,
        out_specs=[pl.BlockSpec((Q, GW), lambda g, i: (cidx(i), g)),
                   pl.BlockSpec((1, SSD_HPG, N, P), lambda g, i: (cidx(i), g, 0, 0))],
        out_shape=[jax.ShapeDtypeStruct((S, SSD_DI), F32), jax.ShapeDtypeStruct((nc, SSD_HEADS, N, P), F32)],
        scratch_shapes=[pltpu.VMEM((SSD_HPG, N, P), F32)],
        compiler_params=_cp("arbitrary", "arbitrary"),
    )(*args)


def ssd_scan_bwd(act, dt, alog, states, dy, prev_x, *, reverse, prev=None, comm=None):
    S = act.shape[0]
    Q, N, P = SSD_CHUNK, SSD_STATE, SSD_HEADDIM
    nc = S // Q
    GW = SSD_HPG * P

    def cidx(i):
        return i if reverse else (nc - 1 - i)

    def body(*refs):
        x_ref, b_ref, c_ref, dt_ref, al_ref, st_ref, dy_ref, px_ref = refs[:8]
        pos = 8
        if prev is not None:
            pb_ref, pc_ref, pdt_ref, pal_ref = refs[pos:pos + 4]
            pos += 4
        dx_ref, db_ref, dc_ref, ddt_ref, dal_ref, dstate = refs[pos:]
        g, i = pl.program_id(0), pl.program_id(1)

        @pl.when(i == 0)
        def _():
            dstate[...] = jnp.zeros_like(dstate)

        xv = x_ref[...]
        dyv = dy_ref[...]
        xs = [xv[:, h * P:(h + 1) * P] for h in range(SSD_HPG)]
        dys = [dyv[:, h * P:(h + 1) * P] for h in range(SSD_HPG)]
        states = [st_ref[0, h] for h in range(SSD_HPG)]
        dstates = [dstate[h] for h in range(SSD_HPG)]

        def f(states, xs, Bg, Cg, dtv, al):
            return _ssd_chunk(states, xs, Bg, Cg, dtv, al, reverse=reverse)

        _, vjp = jax.vjp(f, states, xs, b_ref[...], c_ref[...], dt_ref[0], al_ref[0])
        dst, dxs, dB, dC, ddt, dal = vjp((dstates, dys))
        for h in range(SSD_HPG):
            dstate[h] = dst[h]
            dx_ref[:, h * P:(h + 1) * P] = dxs[h] + px_ref[:, h * P:(h + 1) * P]
        if prev is not None:
            dB = dB + pb_ref[...]
            dC = dC + pc_ref[...]
            ddt = ddt + pdt_ref[0]
        db_ref[...] = dB
        dc_ref[...] = dC
        ddt_ref[0] = ddt

        @pl.when(i == 0)
        def _():
            dal_ref[0] = dal + (pal_ref[0] if prev is not None else 0.0)

        @pl.when(i != 0)
        def _():
            dal_ref[0] += dal

    xspec = pl.BlockSpec((Q, GW), lambda g, i: (cidx(i), g))
    nspec_b = pl.BlockSpec((Q, N), lambda g, i: (cidx(i), SSD_DI // N + g))
    nspec_c = pl.BlockSpec((Q, N), lambda g, i: (cidx(i), SSD_DI // N + SSD_GROUPS + g))
    dtspec = pl.BlockSpec((1, Q, 128), lambda g, i: (g, cidx(i), 0))
    alspec = pl.BlockSpec((1, 1, 128), lambda g, i: (g, 0, 0))
    in_specs = [xspec, nspec_b, nspec_c,
                dtspec, alspec,
                pl.BlockSpec((1, SSD_HPG, N, P), lambda g, i: (cidx(i), g, 0, 0)),
                xspec]
    gspec = pl.BlockSpec((Q, N), lambda g, i: (cidx(i), g))
    in_specs.append(xspec)
    args = [act, act, act, dt, alog, states, dy, prev_x]
    if prev is not None:
        in_specs += [gspec, gspec, dtspec, alspec]
        args += list(prev)
    outs = pl.pallas_call(
        body, name=f"ssd_scan_bwd_{int(reverse)}", grid=(SSD_GROUPS, nc), in_specs=in_specs,
        out_specs=[pl.BlockSpec((Q, GW), lambda g, i: (cidx(i), g)),
                   pl.BlockSpec((Q, N), lambda g, i: (cidx(i), g)),
                   pl.BlockSpec((Q, N), lambda g, i: (cidx(i), g)),
                   dtspec, alspec],
        out_shape=[jax.ShapeDtypeStruct((S, SSD_DI), F32), jax.ShapeDtypeStruct((S, SSD_GROUPS * N), F32),
                   jax.ShapeDtypeStruct((S, SSD_GROUPS * N), F32),
                   jax.ShapeDtypeStruct((SSD_GROUPS, S, 128), F32), jax.ShapeDtypeStruct((SSD_GROUPS, 1, 128), F32)],
        scratch_shapes=[pltpu.VMEM((SSD_HPG, N, P), F32)],
        compiler_params=_cp("arbitrary", "arbitrary"),
    )(*args)
    return outs


def _ssd_chunk(state, x, Bg, Cg, dt, alog, *, reverse):
    Q, P = SSD_CHUNK, SSD_HEADDIM
    r = lax.broadcasted_iota(jnp.int32, (Q, Q), 0)
    c = lax.broadcasted_iota(jnp.int32, (Q, Q), 1)
    keep = (c >= r) if reverse else (c <= r)
    a = dt * (-jnp.exp(alog))
    cum = _cumsum_rows(a, reverse)
    cum_t = jnp.transpose(cum)
    last = 0 if reverse else Q - 1
    CB = _dot(Cg, Bg, ((1,), (1,)))
    yoff = _dot(Cg, state, ((1,), (0,)))
    ys, xdecs, keeps = [], [], []
    for h in range(SSD_HPG):
        col = h + (SSD_HPG if reverse else 0)
        hs = slice(h * P, (h + 1) * P)
        cum_c = cum[:, col:col + 1]
        cum_r = cum_t[col:col + 1, :]
        cum_l = cum_c[last:last + 1, :]
        L = jnp.where(keep, jnp.exp(jnp.where(keep, cum_c - cum_r, 0.0)), 0.0)
        xdt = x[:, hs] * dt[:, col:col + 1]
        ys.append(_dot(CB * L, xdt, ((1,), (0,))) + jnp.exp(cum_c) * yoff[:, hs])
        xdecs.append(xdt * jnp.exp(cum_l - cum_c))
        keeps.append(jnp.exp(cum_l) + jnp.zeros((1, P), F32))
    new_state = jnp.concatenate(keeps, axis=1) * state + _dot(Bg, jnp.concatenate(xdecs, axis=1), ((0,), (0,)))
    return new_state, jnp.concatenate(ys, axis=1)


def ssd_scan_fwd(act, dt, alog, *, reverse, y_prev=None, comm=None):
    S = act.shape[0]
    Q, N, P = SSD_CHUNK, SSD_STATE, SSD_HEADDIM
    nc = S // Q
    GW = SSD_HPG * P

    def cidx(i):
        return (nc - 1 - i) if reverse else i

    def body(*refs):
        if y_prev is None:
            x_ref, b_ref, c_ref, dt_ref, al_ref, y_ref, st_ref, state = refs
            yp_ref = None
        else:
            x_ref, b_ref, c_ref, dt_ref, al_ref, yp_ref, y_ref, st_ref, state = refs
        i = pl.program_id(1)

        @pl.when(i == 0)
        def _():
            state[...] = jnp.zeros_like(state)

        st = state[...]
        st_ref[0, 0] = st
        ns, y = _ssd_chunk(st, x_ref[...], b_ref[...], c_ref[...], dt_ref[0], al_ref[0], reverse=reverse)
        state[...] = ns
        y_ref[...] = y if yp_ref is None else y + yp_ref[...]

    xspec = pl.BlockSpec((Q, GW), lambda g, i: (cidx(i), g))
    in_specs = [xspec,
                pl.BlockSpec((Q, N), lambda g, i: (cidx(i), SSD_DI // N + g)),
                pl.BlockSpec((Q, N), lambda g, i: (cidx(i), SSD_DI // N + SSD_GROUPS + g)),
                pl.BlockSpec((1, Q, 128), lambda g, i: (g, cidx(i), 0)),
                pl.BlockSpec((1, 1, 128), lambda g, i: (g, 0, 0))]
    args = [act, act, act, dt, alog]
    if y_prev is not None:
        in_specs.append(xspec)
        args.append(y_prev)
    return pcall(
        body, comm, name=f"ssd_scan_fwd_{int(reverse)}", grid=(SSD_GROUPS, nc), in_specs=in_specs,
        out_specs=[xspec, pl.BlockSpec((1, 1, N, GW), lambda g, i: (cidx(i), g, 0, 0))],
        out_shape=[jax.ShapeDtypeStruct((S, SSD_DI), F32), jax.ShapeDtypeStruct((nc, SSD_GROUPS, N, GW), F32)],
        scratch_shapes=[pltpu.VMEM((N, GW), F32)],
        compiler_params=_cp("arbitrary", "arbitrary"),
    )(*args)


def ssd_scan_bwd(act, dt, alog, states, dy, prev_x, *, reverse, prev=None, comm=None):
    S = act.shape[0]
    Q, N, P = SSD_CHUNK, SSD_STATE, SSD_HEADDIM
    nc = S // Q
    GW = SSD_HPG * P

    def cidx(i):
        return i if reverse else (nc - 1 - i)

    def body(*refs):
        x_ref, b_ref, c_ref, dt_ref, al_ref, st_ref, dy_ref, px_ref = refs[:8]
        pos = 8
        if prev is not None:
            pb_ref, pc_ref, pdt_ref, pal_ref = refs[pos:pos + 4]
            pos += 4
        dx_ref, db_ref, dc_ref, ddt_ref, dal_ref, dstate = refs[pos:]
        i = pl.program_id(1)

        @pl.when(i == 0)
        def _():
            dstate[...] = jnp.zeros_like(dstate)

        _, vjp = jax.vjp(functools.partial(_ssd_chunk, reverse=reverse), st_ref[0, 0], x_ref[...], b_ref[...],
                         c_ref[...], dt_ref[0], al_ref[0])
        dst, dx, dB, dC, ddt, dal = vjp((dstate[...], dy_ref[...]))
        dstate[...] = dst
        dx_ref[...] = dx + px_ref[...]
        if prev is not None:
            dB = dB + pb_ref[...]
            dC = dC + pc_ref[...]
            ddt = ddt + pdt_ref[0]
        db_ref[...] = dB
        dc_ref[...] = dC
        ddt_ref[0] = ddt

        @pl.when(i == 0)
        def _():
            dal_ref[0] = dal + (pal_ref[0] if prev is not None else 0.0)

        @pl.when(i != 0)
        def _():
            dal_ref[0] += dal

    xspec = pl.BlockSpec((Q, GW), lambda g, i: (cidx(i), g))
    gspec = pl.BlockSpec((Q, N), lambda g, i: (cidx(i), g))
    dtspec = pl.BlockSpec((1, Q, 128), lambda g, i: (g, cidx(i), 0))
    alspec = pl.BlockSpec((1, 1, 128), lambda g, i: (g, 0, 0))
    in_specs = [xspec,
                pl.BlockSpec((Q, N), lambda g, i: (cidx(i), SSD_DI // N + g)),
                pl.BlockSpec((Q, N), lambda g, i: (cidx(i), SSD_DI // N + SSD_GROUPS + g)),
                dtspec, alspec, pl.BlockSpec((1, 1, N, GW), lambda g, i: (cidx(i), g, 0, 0)), xspec, xspec]
    args = [act, act, act, dt, alog, states, dy, prev_x]
    if prev is not None:
        in_specs += [gspec, gspec, dtspec, alspec]
        args += list(prev)
    return pcall(
        body, comm, name=f"ssd_scan_bwd_{int(reverse)}", grid=(SSD_GROUPS, nc), in_specs=in_specs,
        out_specs=[xspec, gspec, gspec, dtspec, alspec],
        out_shape=[jax.ShapeDtypeStruct((S, SSD_DI), F32), jax.ShapeDtypeStruct((S, SSD_GROUPS * N), F32),
                   jax.ShapeDtypeStruct((S, SSD_GROUPS * N), F32),
                   jax.ShapeDtypeStruct((SSD_GROUPS, S, 128), F32), jax.ShapeDtypeStruct((SSD_GROUPS, 1, 128), F32)],
        scratch_shapes=[pltpu.VMEM((N, GW), F32)],
        compiler_params=_cp("arbitrary", "arbitrary"),
    )(*args)


def _ssd_post_fn(y, xs, z, dexp, ng):
    t = (y + xs * dexp) * _silu(z)
    return (_rms(t, ng),)


def _cumsum_rows_impl(x, reverse):
    n = x.shape[0]
    row = lax.broadcasted_iota(jnp.int32, x.shape, 0)
    k = 1
    while k < n:
        if reverse:
            x = x + jnp.where(row < n - k, pltpu.roll(x, n - k, 0), 0.0)
        else:
            x = x + jnp.where(row >= k, pltpu.roll(x, k, 0), 0.0)
        k *= 2
    return x


@functools.partial(jax.custom_vjp, nondiff_argnums=(1,))
def _cumsum_rows(x, reverse):
    return _cumsum_rows_impl(x, reverse)


_cumsum_rows.defvjp(lambda x, reverse: (_cumsum_rows_impl(x, reverse), None),
                    lambda reverse, _, ct: (_cumsum_rows_impl(ct, not reverse),))


def _hg_chunk(state, qraw, fraw, v, lb, *, reverse):
    C = HG_CHUNK
    r = lax.broadcasted_iota(jnp.int32, (C, C), 0)
    c = lax.broadcasted_iota(jnp.int32, (C, C), 1)
    keep = (c >= r) if reverse else (c <= r)
    q = _silu(qraw)
    f = lb + (1.0 - lb) * jax.nn.sigmoid(fraw)
    k = 1.0 - f
    g = jnp.log(f)
    G = _cumsum_rows(g, reverse)
    ref_row = C // 2 - 1 if reverse else C // 2
    last_row = 0 if reverse else C - 1
    Gr = G[ref_row:ref_row + 1, :]
    Gl = G[last_row:last_row + 1, :]
    q_t = q * jnp.exp(G - Gr)
    k_t = k * jnp.exp(Gr - G)
    att = jnp.where(keep, _dot(q_t, k_t, ((1,), (1,))), 0.0)
    o = _dot(att, v, ((1,), (0,))) + _dot(q * jnp.exp(G), state, ((1,), (0,)))
    kd = k * jnp.exp(Gl - G)
    new_state = jnp.transpose(jnp.exp(Gl)) * state + _dot(kd, v, ((0,), (0,)))
    return new_state, o


def hg_scan_fwd(u, lb, *, reverse, o_prev=None, rows=256, comm=None):
    S = u.shape[0]
    nh = HG_HEADS
    rows = min(rows, S)
    nsteps = S // rows
    ncb = rows // HG_CHUNK
    f_sec = 2 if reverse else 1

    def blk(i):
        return (nsteps - 1 - i) if reverse else i

    def body(*refs):
        if o_prev is None:
            q_ref, f_ref, v_ref, lb_ref, o_ref, st_ref, state = refs
            op_ref = None
        else:
            q_ref, f_ref, v_ref, lb_ref, op_ref, o_ref, st_ref, state = refs
        i = pl.program_id(0)

        @pl.when(i == 0)
        def _():
            state[...] = jnp.zeros_like(state)

        def chunk(cc, carry):
            ci = (ncb - 1 - cc) if reverse else cc
            sl = pl.ds(pl.multiple_of(ci * HG_CHUNK, HG_CHUNK), HG_CHUNK)
            for h in range(nh):
                hs = slice(h * HG_D, (h + 1) * HG_D)
                st = state[h]
                st_ref[ci, h] = st
                ns, o = _hg_chunk(st, q_ref[sl, hs], f_ref[sl, hs], v_ref[sl, hs], lb_ref[:, hs], reverse=reverse)
                state[h] = ns
                if op_ref is not None:
                    o = o + op_ref[sl, hs]
                o_ref[sl, hs] = o
            return carry

        lax.fori_loop(0, ncb, chunk, 0)

    rowspec = lambda sec: pl.BlockSpec((rows, HG_W), lambda i: (blk(i), sec))
    in_specs = [rowspec(0), rowspec(f_sec), rowspec(3), pl.BlockSpec((1, HG_W), lambda i: (0, 0))]
    args = [u, u, u, lb]
    if o_prev is not None:
        in_specs.append(rowspec(0))
        args.append(o_prev)
    return pcall(
        body, comm, name=f"hg_scan_fwd_{int(reverse)}", grid=(nsteps,), in_specs=in_specs,
        out_specs=[rowspec(0), pl.BlockSpec((ncb, nh, HG_D, HG_D), lambda i: (blk(i), 0, 0, 0))],
        out_shape=[jax.ShapeDtypeStruct((S, HG_W), F32), jax.ShapeDtypeStruct((S // HG_CHUNK, nh, HG_D, HG_D), F32)],
        scratch_shapes=[pltpu.VMEM((nh, HG_D, HG_D), F32)],
        compiler_params=_cp("arbitrary"),
    )(*args)


def hg_scan_bwd(u, lb, states, do, *, reverse, prev=None, rows=256, comm=None):
    S = u.shape[0]
    nh = HG_HEADS
    rows = min(rows, S)
    nsteps = S // rows
    ncb = rows // HG_CHUNK
    f_sec = 2 if reverse else 1

    def blk(i):
        return i if reverse else (nsteps - 1 - i)

    def body(*refs):
        q_ref, f_ref, v_ref, lb_ref, st_ref, do_ref = refs[:6]
        pos = 6
        if prev is not None:
            pq_ref, pv_ref, plb_ref = refs[pos:pos + 3]
            pos += 3
        dq_ref, df_ref, dv_ref, dlb_ref, dstate = refs[pos:]
        i = pl.program_id(0)

        @pl.when(i == 0)
        def _():
            dstate[...] = jnp.zeros_like(dstate)
            dlb_ref[...] = plb_ref[...] if prev is not None else jnp.zeros_like(dlb_ref)

        def chunk(cc, carry):
            ci = cc if reverse else (ncb - 1 - cc)
            sl = pl.ds(pl.multiple_of(ci * HG_CHUNK, HG_CHUNK), HG_CHUNK)
            for h in range(nh):
                hs = slice(h * HG_D, (h + 1) * HG_D)
                _, vjp = jax.vjp(functools.partial(_hg_chunk, reverse=reverse), st_ref[ci, h],
                                 q_ref[sl, hs], f_ref[sl, hs], v_ref[sl, hs], lb_ref[:, hs])
                dst, dq, df, dv, dlb = vjp((dstate[h], do_ref[sl, hs]))
                dstate[h] = dst
                if prev is not None:
                    dq = dq + pq_ref[sl, hs]
                    dv = dv + pv_ref[sl, hs]
                dq_ref[sl, hs] = dq
                df_ref[sl, hs] = df
                dv_ref[sl, hs] = dv
                dlb_ref[:, hs] += dlb
            return carry

        lax.fori_loop(0, ncb, chunk, 0)

    rowspec = lambda sec: pl.BlockSpec((rows, HG_W), lambda i: (blk(i), sec))
    lbspec = pl.BlockSpec((1, HG_W), lambda i: (0, 0))
    in_specs = [rowspec(0), rowspec(f_sec), rowspec(3), lbspec,
                pl.BlockSpec((ncb, nh, HG_D, HG_D), lambda i: (blk(i), 0, 0, 0)), rowspec(0)]
    args = [u, u, u, lb, states, do]
    if prev is not None:
        in_specs += [rowspec(0), rowspec(0), lbspec]
        args += list(prev)
    return pcall(
        body, comm, name=f"hg_scan_bwd_{int(reverse)}", grid=(nsteps,), in_specs=in_specs,
        out_specs=[rowspec(0), rowspec(0), rowspec(0), lbspec],
        out_shape=[jax.ShapeDtypeStruct((S, HG_W), F32)] * 3 + [jax.ShapeDtypeStruct((1, HG_W), F32)],
        scratch_shapes=[pltpu.VMEM((nh, HG_D, HG_D), F32)],
        compiler_params=_cp("arbitrary"),
    )(*args)


def _hg_lb_fn(lbp):
    m = jnp.max(lbp, axis=0, keepdims=True)
    e = jnp.exp(lbp - m)
    sm = e / jnp.sum(e, axis=0, keepdims=True)
    return ((sm[0:1] + sm[1:2]) - sm[0:1],)


def hg_lb_fwd(lbp):
    def body(x_ref, o_ref):
        o_ref[...] = _hg_lb_fn(x_ref[...])[0]

    return pl.pallas_call(body, name="hg_lb_fwd", out_shape=jax.ShapeDtypeStruct((1, HG_W), F32))(lbp)


def hg_lb_bwd(lbp, dlb):
    def body(x_ref, d_ref, o_ref):
        _, vjp = jax.vjp(_hg_lb_fn, x_ref[...])
        o_ref[...] = vjp((d_ref[...],))[0]

    return pl.pallas_call(body, name="hg_lb_bwd", out_shape=jax.ShapeDtypeStruct(lbp.shape, F32))(lbp, dlb)


def _hg_post_fn(o, gate, ng):
    return (_rms(o, ng) * _silu(gate),)


def _gate_fn(o, gate):
    return (o * _silu(gate),)


def _rope_tables(S):
    t = np.arange(S)
    row = (t // GRID_W).astype(np.float32)
    col = (t % GRID_W).astype(np.float32)
    half = AT_HD // 4
    inv = (ROPE_THETA ** (-np.arange(0, 2 * half, 2, dtype=np.float32) / np.float32(2 * half))).astype(np.float32)
    ar = row[:, None] * inv[None, :]
    ac = col[:, None] * inv[None, :]
    return ar.astype(np.float32), ac.astype(np.float32)


def _rope_swap_matrix():
    p = np.zeros((AT_HD, AT_HD), np.float32)
    for i in range(AT_HD):
        p[(i + 32) if (i % 64) < 32 else (i - 32), i] = 1.0
    return p


@jax.custom_vjp
def _half_swap(x):
    ax = x.ndim - 1
    lane = lax.broadcasted_iota(jnp.int32, x.shape, ax)
    return jnp.where((lane & 32) == 0, pltpu.roll(x, 96, ax), pltpu.roll(x, 32, ax))


_half_swap.defvjp(lambda x: (_half_swap(x), None), lambda _, ct: (_half_swap(ct),))


def _make_qk_fn(scale):
    def fn(x, ct, st, g):
        n = _rms(x, g)
        return ((n * ct + _half_swap(n) * st) * scale,)
    return fn


def flash_fwd(q, k, v, *, v_col0=0, tq=512, tk=512):
    S = q.shape[0]
    tq, tk = min(tq, S), min(tk, S)
    nk = S // tk
    G = AT_HEADS // AT_KV

    def body(q_ref, k_ref, v_ref, o_ref, lse_ref, m_s, l_s, acc):
        ki = pl.program_id(2)

        @pl.when(ki == 0)
        def _():
            m_s[...] = jnp.full_like(m_s, -jnp.inf)
            l_s[...] = jnp.zeros_like(l_s)
            acc[...] = jnp.zeros_like(acc)

        kv, vv = k_ref[...], v_ref[...]
        for g in range(G):
            s = _dot(q_ref[:, g * AT_HD:(g + 1) * AT_HD], kv, ((1,), (1,)))
            m_old = m_s[g]
            m_new = jnp.maximum(m_old, jnp.max(s, axis=1, keepdims=True))
            alpha = jnp.exp(m_old - m_new)
            p = jnp.exp(s - m_new)
            l_s[g] = alpha * l_s[g] + jnp.sum(p, axis=1, keepdims=True)
            acc[g] = alpha * acc[g] + _dot(p, vv, ((1,), (0,)))
            m_s[g] = m_new

        @pl.when(ki == nk - 1)
        def _():
            for g in range(G):
                o_ref[:, g * AT_HD:(g + 1) * AT_HD] = acc[g] / l_s[g]
                lse_ref[0, :, g:g + 1] = m_s[g] + jnp.log(l_s[g])

    return pl.pallas_call(
        body, name="flash_fwd", grid=(AT_KV, S // tq, nk),
        in_specs=[pl.BlockSpec((tq, G * AT_HD), lambda h, i, j: (i, h)),
                  pl.BlockSpec((tk, AT_HD), lambda h, i, j: (j, h)),
                  pl.BlockSpec((tk, AT_HD), lambda h, i, j: (j, v_col0 + h))],
        out_specs=[pl.BlockSpec((tq, G * AT_HD), lambda h, i, j: (i, h)),
                   pl.BlockSpec((1, tq, G), lambda h, i, j: (h, i, 0))],
        out_shape=[jax.ShapeDtypeStruct((S, AT_QW), F32), jax.ShapeDtypeStruct((AT_KV, S, G), F32)],
        scratch_shapes=[pltpu.VMEM((G, tq, 1), F32), pltpu.VMEM((G, tq, 1), F32), pltpu.VMEM((G, tq, AT_HD), F32)],
        compiler_params=_cp("parallel", "parallel", "arbitrary"),
    )(q, k, v)


def flash_bwd_dq(q, k, v, o, lse, do, *, v_col0=0, tq=512, tk=512):
    S = q.shape[0]
    tq, tk = min(tq, S), min(tk, S)
    nk = S // tk
    G = AT_HEADS // AT_KV

    def body(q_ref, k_ref, v_ref, o_ref, lse_ref, do_ref, dq_ref, dl_ref, acc, dl_s):
        ki = pl.program_id(2)

        @pl.when(ki == 0)
        def _():
            acc[...] = jnp.zeros_like(acc)
            for g in range(G):
                sl = slice(g * AT_HD, (g + 1) * AT_HD)
                dl_s[g] = jnp.sum(do_ref[:, sl] * o_ref[:, sl], axis=1, keepdims=True)

        kv, vv = k_ref[...], v_ref[...]
        for g in range(G):
            sl = slice(g * AT_HD, (g + 1) * AT_HD)
            s = _dot(q_ref[:, sl], kv, ((1,), (1,)))
            p = jnp.exp(s - lse_ref[0, :, g:g + 1])
            dp = _dot(do_ref[:, sl], vv, ((1,), (1,)))
            ds = p * (dp - dl_s[g])
            acc[g] += _dot(ds, kv, ((1,), (0,)))

        @pl.when(ki == nk - 1)
        def _():
            for g in range(G):
                dq_ref[:, g * AT_HD:(g + 1) * AT_HD] = acc[g]
                dl_ref[0, :, g:g + 1] = dl_s[g]

    qspec = pl.BlockSpec((tq, G * AT_HD), lambda h, i, j: (i, h))
    kspec = pl.BlockSpec((tk, AT_HD), lambda h, i, j: (j, h))
    lspec = pl.BlockSpec((1, tq, G), lambda h, i, j: (h, i, 0))
    return pl.pallas_call(
        body, name="flash_bwd_dq", grid=(AT_KV, S // tq, nk),
        in_specs=[qspec, kspec, pl.BlockSpec((tk, AT_HD), lambda h, i, j: (j, v_col0 + h)), qspec, lspec, qspec],
        out_specs=[qspec, lspec],
        out_shape=[jax.ShapeDtypeStruct((S, AT_QW), F32), jax.ShapeDtypeStruct((AT_KV, S, G), F32)],
        scratch_shapes=[pltpu.VMEM((G, tq, AT_HD), F32), pltpu.VMEM((G, tq, 1), F32)],
        compiler_params=_cp("parallel", "parallel", "arbitrary"),
    )(q, k, v, o, lse, do)


def flash_bwd_dkv(q, k, v, lse, delta, do, *, v_col0=0, tq=512, tk=512):
    S = q.shape[0]
    tq, tk = min(tq, S), min(tk, S)
    nq = S // tq
    G = AT_HEADS // AT_KV

    def body(q_ref, k_ref, v_ref, lse_ref, dl_ref, do_ref, dk_ref, dv_ref, dk_acc, dv_acc):
        qi = pl.program_id(2)

        @pl.when(qi == 0)
        def _():
            dk_acc[...] = jnp.zeros_like(dk_acc)
            dv_acc[...] = jnp.zeros_like(dv_acc)

        kv, vv = k_ref[...], v_ref[...]
        for g in range(G):
            sl = slice(g * AT_HD, (g + 1) * AT_HD)
            qg, dog = q_ref[:, sl], do_ref[:, sl]
            s = _dot(qg, kv, ((1,), (1,)))
            p = jnp.exp(s - lse_ref[0, :, g:g + 1])
            dv_acc[...] += _dot(p, dog, ((0,), (0,)))
            dp = _dot(dog, vv, ((1,), (1,)))
            ds = p * (dp - dl_ref[0, :, g:g + 1])
            dk_acc[...] += _dot(ds, qg, ((0,), (0,)))

        @pl.when(qi == nq - 1)
        def _():
            dk_ref[...] = dk_acc[...]
            dv_ref[...] = dv_acc[...]

    qspec = pl.BlockSpec((tq, G * AT_HD), lambda h, j, i: (i, h))
    kspec = pl.BlockSpec((tk, AT_HD), lambda h, j, i: (j, h))
    lspec = pl.BlockSpec((1, tq, G), lambda h, j, i: (h, i, 0))
    return pl.pallas_call(
        body, name="flash_bwd_dkv", grid=(AT_KV, S // tk, nq),
        in_specs=[qspec, kspec, pl.BlockSpec((tk, AT_HD), lambda h, j, i: (j, v_col0 + h)), lspec, lspec, qspec],
        out_specs=[kspec, kspec],
        out_shape=[jax.ShapeDtypeStruct((S, AT_KW), F32), jax.ShapeDtypeStruct((S, AT_KW), F32)],
        scratch_shapes=[pltpu.VMEM((tk, AT_HD), F32), pltpu.VMEM((tk, AT_HD), F32)],
        compiler_params=_cp("parallel", "parallel", "arbitrary"),
    )(q, k, v, lse, delta, do)


def flash_fwd(q, k, v, *, v_col0=0, tq=256):
    S = q.shape[0]
    tq = min(tq, S)
    G = AT_HEADS // AT_KV

    def body(q_ref, k_ref, v_ref, o_ref, lse_ref):
        kv, vv = k_ref[...], v_ref[...]
        for g in range(G):
            sl = slice(g * AT_HD, (g + 1) * AT_HD)
            s = _dot(q_ref[:, sl], kv, ((1,), (1,)))
            m = jnp.max(s, axis=1, keepdims=True)
            p = jnp.exp(s - m)
            l = jnp.sum(p, axis=1, keepdims=True)
            o_ref[:, sl] = _dot(p, vv, ((1,), (0,))) / l
            lse_ref[0, :, g:g + 1] = m + jnp.log(l)

    return pl.pallas_call(
        body, name="flash_fwd", grid=(AT_KV, S // tq),
        in_specs=[pl.BlockSpec((tq, G * AT_HD), lambda h, i: (i, h)),
                  pl.BlockSpec((S, AT_HD), lambda h, i: (0, h)),
                  pl.BlockSpec((S, AT_HD), lambda h, i: (0, v_col0 + h))],
        out_specs=[pl.BlockSpec((tq, G * AT_HD), lambda h, i: (i, h)),
                   pl.BlockSpec((1, tq, G), lambda h, i: (h, i, 0))],
        out_shape=[jax.ShapeDtypeStruct((S, AT_QW), F32), jax.ShapeDtypeStruct((AT_KV, S, G), F32)],
        compiler_params=_cp("parallel", "arbitrary"),
    )(q, k, v)


def flash_bwd_dq(q, k, v, o, lse, do, *, v_col0=0, tq=256):
    S = q.shape[0]
    tq = min(tq, S)
    G = AT_HEADS // AT_KV

    def body(q_ref, k_ref, v_ref, o_ref, lse_ref, do_ref, dq_ref, dl_ref):
        kv, vv = k_ref[...], v_ref[...]
        for g in range(G):
            sl = slice(g * AT_HD, (g + 1) * AT_HD)
            dog = do_ref[:, sl]
            delta = jnp.sum(dog * o_ref[:, sl], axis=1, keepdims=True)
            s = _dot(q_ref[:, sl], kv, ((1,), (1,)))
            p = jnp.exp(s - lse_ref[0, :, g:g + 1])
            dp = _dot(dog, vv, ((1,), (1,)))
            ds = p * (dp - delta)
            dq_ref[:, sl] = _dot(ds, kv, ((1,), (0,)))
            dl_ref[0, :, g:g + 1] = delta

    qspec = pl.BlockSpec((tq, G * AT_HD), lambda h, i: (i, h))
    lspec = pl.BlockSpec((1, tq, G), lambda h, i: (h, i, 0))
    return pl.pallas_call(
        body, name="flash_bwd_dq", grid=(AT_KV, S // tq),
        in_specs=[qspec, pl.BlockSpec((S, AT_HD), lambda h, i: (0, h)),
                  pl.BlockSpec((S, AT_HD), lambda h, i: (0, v_col0 + h)), qspec, lspec, qspec],
        out_specs=[qspec, lspec],
        out_shape=[jax.ShapeDtypeStruct((S, AT_QW), F32), jax.ShapeDtypeStruct((AT_KV, S, G), F32)],
        compiler_params=_cp("parallel", "arbitrary"),
    )(q, k, v, o, lse, do)


def flash_bwd_dkv(q, k, v, lse, delta, do, *, v_col0=0, tk=512, comm=None):
    S = q.shape[0]
    tk = min(tk, S)
    G = AT_HEADS // AT_KV

    def body(q_ref, k_ref, v_ref, lse_ref, dl_ref, do_ref, dk_ref, dv_ref):
        kv, vv = k_ref[...], v_ref[...]
        dk = jnp.zeros((tk, AT_HD), F32)
        dv = jnp.zeros((tk, AT_HD), F32)
        for g in range(G):
            sl = slice(g * AT_HD, (g + 1) * AT_HD)
            qg, dog = q_ref[:, sl], do_ref[:, sl]
            s = _dot(qg, kv, ((1,), (1,)))
            p = jnp.exp(s - lse_ref[0, :, g:g + 1])
            dv = dv + _dot(p, dog, ((0,), (0,)))
            dp = _dot(dog, vv, ((1,), (1,)))
            ds = p * (dp - dl_ref[0, :, g:g + 1])
            dk = dk + _dot(ds, qg, ((0,), (0,)))
        dk_ref[...] = dk
        dv_ref[...] = dv

    qspec = pl.BlockSpec((S, G * AT_HD), lambda h, j: (0, h))
    kspec = pl.BlockSpec((tk, AT_HD), lambda h, j: (j, h))
    lspec = pl.BlockSpec((1, S, G), lambda h, j: (h, 0, 0))
    return pcall(
        body, comm, name="flash_bwd_dkv", grid=(AT_KV, S // tk),
        in_specs=[qspec, kspec, pl.BlockSpec((tk, AT_HD), lambda h, j: (j, v_col0 + h)), lspec, lspec, qspec],
        out_specs=[kspec, kspec],
        out_shape=[jax.ShapeDtypeStruct((S, AT_KW), F32), jax.ShapeDtypeStruct((S, AT_KW), F32)],
        compiler_params=_cp("parallel", "arbitrary"),
    )(q, k, v, lse, delta, do)


def _t5_bucket_np(rel):
    half = REL_BUCKETS // 2
    exact = half // 2
    n = np.abs(rel)
    large = exact + (np.log(np.maximum(n, 1).astype(np.float32) / np.float32(exact))
                     / np.float32(math.log(REL_MAX_DIST / exact)) * np.float32(half - exact)).astype(np.int32)
    large = np.minimum(large, half - 1)
    return np.where(rel > 0, half, 0) + np.where(n < exact, n, large)


def _dl_tiles(Ls):
    T = min(128, Ls)
    return T, T + 2 * DL_HALF


def _dl_bucket_tables(dil, T):
    W = T + 2 * DL_HALF
    i = np.arange(T)[:, None]
    j = np.arange(W)[None, :]
    bq = _t5_bucket_np((j - DL_HALF - i) * dil)
    iw = np.arange(W)[:, None]
    jk = np.arange(T)[None, :]
    bk = _t5_bucket_np((jk + DL_HALF - iw) * dil)
    return bq.astype(np.int32), bk.astype(np.int32)


def band_fwd(q, kp, vp, bias, *, scale):
    H, dil, Ls, E = q.shape
    T, W = _dl_tiles(Ls)

    def body(q_ref, k_ref, v_ref, b_ref, o_ref, lse_ref):
        n = pl.program_id(2)
        r0 = pl.multiple_of(n * T, T)
        kw = k_ref[0, 0, pl.ds(r0, W), :]
        vw = v_ref[0, 0, pl.ds(r0, W), :]
        i = lax.broadcasted_iota(jnp.int32, (T, W), 0)
        j = lax.broadcasted_iota(jnp.int32, (T, W), 1)
        kpos = n * T + j - DL_HALF
        mask = (jnp.abs(j - DL_HALF - i) <= DL_HALF) & (kpos >= 0) & (kpos < Ls)
        s = _dot(q_ref[0, 0], kw, ((1,), (1,))) * scale + b_ref[0]
        s = jnp.where(mask, s, NEG_BIG)
        m = jnp.max(s, axis=1, keepdims=True)
        lse = m + jnp.log(jnp.sum(jnp.exp(s - m), axis=1, keepdims=True))
        p = jnp.exp(s - lse)
        o_ref[0, 0] = _dot(p, vw, ((1,), (0,)))
        lse_ref[0, 0] = lse

    return pl.pallas_call(
        body, name=f"band_fwd_{dil}", grid=(H, dil, Ls // T),
        in_specs=[pl.BlockSpec((1, 1, T, E), lambda h, d, n: (h, d, n, 0)),
                  pl.BlockSpec((1, 1, Ls + 2 * DL_HALF, E), lambda h, d, n: (h, d, 0, 0)),
                  pl.BlockSpec((1, 1, Ls + 2 * DL_HALF, E), lambda h, d, n: (h, d, 0, 0)),
                  pl.BlockSpec((1, T, W), lambda h, d, n: (h, 0, 0))],
        out_specs=[pl.BlockSpec((1, 1, T, E), lambda h, d, n: (h, d, n, 0)),
                   pl.BlockSpec((1, 1, T, 1), lambda h, d, n: (h, d, n, 0))],
        out_shape=[jax.ShapeDtypeStruct((H, dil, Ls, E), F32), jax.ShapeDtypeStruct((H, dil, Ls, 1), F32)],
        compiler_params=_cp("parallel", "parallel", "arbitrary"),
    )(q, kp, vp, bias)


def band_bwd_dq(q, kp, vp, bias, lse, dm, do, *, scale):
    H, dil, Ls, E = q.shape
    T, W = _dl_tiles(Ls)

    def body(q_ref, k_ref, v_ref, b_ref, lse_ref, dm_ref, do_ref, dq_ref, db_ref):
        d, n = pl.program_id(1), pl.program_id(2)
        r0 = pl.multiple_of(n * T, T)
        kw = k_ref[0, 0, pl.ds(r0, W), :]
        vw = v_ref[0, 0, pl.ds(r0, W), :]
        i = lax.broadcasted_iota(jnp.int32, (T, W), 0)
        j = lax.broadcasted_iota(jnp.int32, (T, W), 1)
        kpos = n * T + j - DL_HALF
        mask = (jnp.abs(j - DL_HALF - i) <= DL_HALF) & (kpos >= 0) & (kpos < Ls)
        s = _dot(q_ref[0, 0], kw, ((1,), (1,))) * scale + b_ref[0]
        p = jnp.where(mask, jnp.exp(jnp.where(mask, s, 0.0) - lse_ref[0, 0]), 0.0)
        dp = _dot(do_ref[0, 0], vw, ((1,), (1,)))
        ds = p * (dp - dm_ref[0, 0])
        dq_ref[0, 0] = _dot(ds, kw, ((1,), (0,))) * scale
        first = jnp.logical_and(d == 0, n == 0)

        @pl.when(first)
        def _():
            db_ref[0] = ds

        @pl.when(jnp.logical_not(first))
        def _():
            db_ref[0] += ds

    qspec = pl.BlockSpec((1, 1, T, E), lambda h, d, n: (h, d, n, 0))
    kspec = pl.BlockSpec((1, 1, Ls + 2 * DL_HALF, E), lambda h, d, n: (h, d, 0, 0))
    rspec = pl.BlockSpec((1, 1, T, 1), lambda h, d, n: (h, d, n, 0))
    bspec = pl.BlockSpec((1, T, W), lambda h, d, n: (h, 0, 0))
    return pl.pallas_call(
        body, name=f"band_bwd_dq_{dil}", grid=(H, dil, Ls // T),
        in_specs=[qspec, kspec, kspec, bspec, rspec, rspec, qspec],
        out_specs=[qspec, bspec],
        out_shape=[jax.ShapeDtypeStruct((H, dil, Ls, E), F32), jax.ShapeDtypeStruct((H, T, W), F32)],
        compiler_params=_cp("arbitrary", "arbitrary", "arbitrary"),
    )(q, kp, vp, bias, lse, dm, do)


def band_bwd_dkv(qp, k, v, bias_t, lsep, dmp, dop, *, scale):
    H, dil, Ls, E = k.shape
    T, W = _dl_tiles(Ls)

    def body(q_ref, k_ref, v_ref, b_ref, lse_ref, dm_ref, do_ref, dk_ref, dv_ref):
        n = pl.program_id(2)
        r0 = pl.multiple_of(n * T, T)
        qw = q_ref[0, 0, pl.ds(r0, W), :]
        dow = do_ref[0, 0, pl.ds(r0, W), :]
        lsew = lse_ref[0, 0, pl.ds(r0, W), :]
        dmw = dm_ref[0, 0, pl.ds(r0, W), :]
        iw = lax.broadcasted_iota(jnp.int32, (W, T), 0)
        j = lax.broadcasted_iota(jnp.int32, (W, T), 1)
        qpos = n * T + iw - DL_HALF
        mask = (jnp.abs(j + DL_HALF - iw) <= DL_HALF) & (qpos >= 0) & (qpos < Ls)
        s = _dot(qw, k_ref[0, 0], ((1,), (1,))) * scale + b_ref[0]
        p = jnp.where(mask, jnp.exp(jnp.where(mask, s, 0.0) - lsew), 0.0)
        dv_ref[0, 0] = _dot(p, dow, ((0,), (0,)))
        dp = _dot(dow, v_ref[0, 0], ((1,), (1,)))
        ds = p * (dp - dmw)
        dk_ref[0, 0] = _dot(ds, qw, ((0,), (0,))) * scale

    kspec = pl.BlockSpec((1, 1, T, E), lambda h, d, n: (h, d, n, 0))
    wspec = pl.BlockSpec((1, 1, Ls + 2 * DL_HALF, E), lambda h, d, n: (h, d, 0, 0))
    w1spec = pl.BlockSpec((1, 1, Ls + 2 * DL_HALF, 1), lambda h, d, n: (h, d, 0, 0))
    return pl.pallas_call(
        body, name=f"band_bwd_dkv_{dil}", grid=(H, dil, Ls // T),
        in_specs=[wspec, kspec, kspec, pl.BlockSpec((1, W, T), lambda h, d, n: (h, 0, 0)), w1spec, w1spec, wspec],
        out_specs=[kspec, kspec],
        out_shape=[jax.ShapeDtypeStruct((H, dil, Ls, E), F32), jax.ShapeDtypeStruct((H, dil, Ls, E), F32)],
        compiler_params=_cp("parallel", "parallel", "arbitrary"),
    )(qp, k, v, bias_t, lsep, dmp, dop)


def _dl_merge_fn(o0, o1, o2, l0, l1, l2):
    m = jnp.maximum(jnp.maximum(l0, l1), l2)
    e0, e1, e2 = jnp.exp(l0 - m), jnp.exp(l1 - m), jnp.exp(l2 - m)
    den = e0 + e1 + e2
    return ((e0 / den) * o0 + (e1 / den) * o1 + (e2 / den) * o2,)


def _adamw_math(w, g, m, v):
    m = ADAM_B1 * m + (1.0 - ADAM_B1) * g
    v = ADAM_B2 * v + (1.0 - ADAM_B2) * (g * g)
    m_hat = m / (1.0 - ADAM_B1 ** ADAM_STEP)
    v_hat = v / (1.0 - ADAM_B2 ** ADAM_STEP)
    delta = -ADAM_LR * (m_hat / (jnp.sqrt(v_hat) + ADAM_EPS) + ADAM_WD * w)
    return delta, m, v


def adamw_sum(parts, w, m, v, *, name, R=128):
    rows, cols = w.shape
    R = min(R, rows)
    if rows % R:
        R = rows

    def body(p_ref, w_ref, m_ref, v_ref, g_ref, d_ref, nm_ref, nv_ref):
        g = p_ref[0].astype(F32)
        for s in range(1, N_DEV):
            g = g + p_ref[s].astype(F32)
        d, nm, nv = _adamw_math(w_ref[...], g, m_ref[...], v_ref[...])
        g_ref[...] = g
        d_ref[...] = d
        nm_ref[...] = nm
        nv_ref[...] = nv

    spec = pl.BlockSpec((R, cols), lambda i: (i, 0))
    return pl.pallas_call(
        body, name=name, grid=(rows // R,),
        in_specs=[pl.BlockSpec((N_DEV, R, cols), lambda i: (0, i, 0)), spec, spec, spec],
        out_specs=[spec] * 4, out_shape=[jax.ShapeDtypeStruct((rows, cols), F32)] * 4,
        compiler_params=_cp("parallel"),
    )(parts, w, m, v)


def sum_parts(parts, *, name):
    rows, cols = parts.shape[1:]

    def body(p_ref, o_ref):
        g = p_ref[0]
        for s in range(1, N_DEV):
            g = g + p_ref[s]
        o_ref[...] = g

    return pl.pallas_call(body, name=name, out_shape=jax.ShapeDtypeStruct((rows, cols), F32))(parts)


def adamw_plain(w, g, m, v, *, name):
    def body(w_ref, g_ref, m_ref, v_ref, d_ref, nm_ref, nv_ref):
        d, nm, nv = _adamw_math(w_ref[...], g_ref[...], m_ref[...], v_ref[...])
        d_ref[...] = d
        nm_ref[...] = nm
        nv_ref[...] = nv

    return pl.pallas_call(body, name=name, out_shape=[jax.ShapeDtypeStruct(w.shape, F32)] * 3)(w, g, m, v)


def _my_pos():
    return lax.axis_index("x"), lax.axis_index("y"), lax.axis_index("c")


def _flat(px, py, pc):
    return 4 * px + 2 * py + pc


def allgather_two_level(x, *, name):
    R, C = x.shape

    def body(x_ref, out_ref, send_sems, recv_sems, local_sem):
        x_, y_, c_ = _my_pos()
        me, sibling = (x_, y_, c_), (x_, y_, 1 - c_)
        chips = [(1 - x_, y_), (x_, 1 - y_), (1 - x_, 1 - y_)]

        def rows(p):
            return out_ref.at[_flat(*p)]

        def copy(k, block, to, src=None):
            return pltpu.make_async_remote_copy(
                src_ref=rows(block) if src is None else src, dst_ref=rows(block),
                send_sem=send_sems.at[k], recv_sem=recv_sems.at[k], device_id=to, device_id_type=MESH_ID)

        mine = pltpu.make_async_copy(x_ref, rows(me), local_sem)
        mine.start()
        first = [copy(0, me, sibling, src=x_ref)]
        first += [copy(1 + j, me, (*chip, c_), src=x_ref) for j, chip in enumerate(chips)]
        for cp in first:
            cp.start()
        passed = [copy(4 + j, (*chip, c_), sibling) for j, chip in enumerate(chips)]
        for j, chip in enumerate(chips):
            copy(1 + j, (*chip, c_), me).wait_recv()
            passed[j].start()
        copy(0, sibling, me).wait_recv()
        for j, chip in enumerate(chips):
            copy(4 + j, (*chip, 1 - c_), me).wait_recv()
        for cp in first + passed:
            cp.wait_send()
        mine.wait()

    return pl.pallas_call(
        body, name=name,
        out_shape=jax.ShapeDtypeStruct((N_DEV, R, C), x.dtype),
        in_specs=[pl.BlockSpec(memory_space=pl.ANY)],
        out_specs=pl.BlockSpec(memory_space=pl.ANY),
        scratch_shapes=[pltpu.SemaphoreType.DMA((7,)), pltpu.SemaphoreType.DMA((7,)), pltpu.SemaphoreType.DMA],
    )(x)


def all_to_all(bufs, *, name):
    nb = len(bufs)

    def body(*refs):
        in_refs = refs[:nb]
        out_refs = refs[nb:2 * nb]
        send_sems, recv_sems, local_sems = refs[2 * nb:]
        x_, y_, c_ = _my_pos()
        me = _flat(x_, y_, c_)
        peers = []
        for k in range(1, N_DEV):
            fx, fy, fc = (k >> 2) & 1, (k >> 1) & 1, k & 1
            peers.append(((1 - x_) if fx else x_, (1 - y_) if fy else y_, (1 - c_) if fc else c_))
        copies = []
        for b in range(nb):
            loc = pltpu.make_async_copy(in_refs[b].at[me], out_refs[b].at[me], local_sems.at[b])
            loc.start()
            copies.append(loc)
        remote = []
        for b in range(nb):
            for k, p in enumerate(peers):
                cp = pltpu.make_async_remote_copy(
                    src_ref=in_refs[b].at[_flat(*p)], dst_ref=out_refs[b].at[me],
                    send_sem=send_sems.at[b, k], recv_sem=recv_sems.at[b, k], device_id=p, device_id_type=MESH_ID)
                cp.start()
                remote.append((b, k, p))
        for b, k, p in remote:
            pltpu.make_async_remote_copy(
                src_ref=in_refs[b].at[me], dst_ref=out_refs[b].at[_flat(*p)],
                send_sem=send_sems.at[b, k], recv_sem=recv_sems.at[b, k], device_id=p, device_id_type=MESH_ID).wait()
        for loc in copies:
            loc.wait()

    return pl.pallas_call(
        body, name=name,
        out_shape=[jax.ShapeDtypeStruct(b.shape, b.dtype) for b in bufs],
        in_specs=[pl.BlockSpec(memory_space=pl.ANY)] * nb,
        out_specs=[pl.BlockSpec(memory_space=pl.ANY)] * nb,
        scratch_shapes=[pltpu.SemaphoreType.DMA((nb, 7)), pltpu.SemaphoreType.DMA((nb, 7)), pltpu.SemaphoreType.DMA((nb,))],
    )(*bufs)


def _prenorm(tag, x, ng):
    return rowwise_fwd(f"{tag}_prenorm", _prenorm_fn, [(x, 0, False)], [(ng, False)], [(D_MODEL, MXU_DTYPE)], W=D_MODEL)[0]


def _cat_mxu(parts):
    return jnp.concatenate([t.astype(MXU_DTYPE) for t in parts], axis=1)


def _in_out_bwd(tag, x, ng, hn, du, w_in, dx):
    dhn = matmul(du, w_in, tb=True, name=f"{tag}_dhn")
    dw_in = matmul(hn, du, ta=True, out_dtype=GRAD_WIRE_DTYPE, name=f"{tag}_dw_in")
    dx_prev, dng = rowwise_bwd(f"{tag}_prenorm_bwd", _prenorm_fn, [(x, 0, False)], [(ng, False)], [dhn],
                               W=D_MODEL, diff_rows=[0], diff_shared=[0], add=dx)
    return dx_prev, dng, dw_in


def _own(res, comm):
    return (res, None) if comm is None else res


def ssd_layer_fwd(x, ng, p, comm=None):
    hn = _prenorm("ssd", x, ng)
    u = matmul(hn, p["w_in"], name="ssd_in")
    act = ssd_conv_fwd(u, p["conv_w"], p["conv_b"])
    dt = rowwise_fwd("ssd_dt", _dt_fn, [(u, (SSD_DI + SSD_CONV_CH) // 128, False)], [(p["dt_bias"], False)],
                     [(128, F32)], W=128)[0]
    dt, alog = _ssd_group_layout(dt), _ssd_group_layout(p["alog"])
    (y0, st0), cres = _own(ssd_scan_fwd(act, dt, alog, reverse=False, comm=comm), comm)
    y, st1 = ssd_scan_fwd(act, dt, alog, reverse=True, y_prev=y0)
    g = rowwise_fwd("ssd_post", _ssd_post_fn, [(y, 0, True), (act, 0, True), (u, 0, True)],
                    [(p["dexp"], True), (p["norm_g"], True)], [(512, MXU_DTYPE)], W=512, ncb=SSD_GROUPS)[0]
    xn = matmul(g, p["w_out"], residual=x, name="ssd_out")
    return xn, dict(x=x, ng=ng, hn=hn, u=u, act=act, dt=dt, alog=alog, y=y, st0=st0, st1=st1, g=g), cres


def ssd_layer_bwd(sv, p, dx, comm=None):
    u, act, dt = sv["u"], sv["act"], sv["dt"]
    S = u.shape[0]
    dg = matmul(dx, p["w_out"], tb=True, name="ssd_dg")
    dw_out = matmul(sv["g"], dx, ta=True, out_dtype=GRAD_WIRE_DTYPE, name="ssd_dw_out")
    dy, dxs_skip, dz, ddexp, dnorm = rowwise_bwd(
        "ssd_post_bwd", _ssd_post_fn, [(sv["y"], 0, True), (act, 0, True), (u, 0, True)],
        [(p["dexp"], True), (p["norm_g"], True)], [dg], W=512, ncb=SSD_GROUPS, diff_rows=[0, 1, 2], diff_shared=[0, 1])
    (dxa, dB, dC, ddt, dal), cres = _own(
        ssd_scan_bwd(act, dt, sv["alog"], sv["st0"], dy, dxs_skip, reverse=False, comm=comm), comm)
    dxa, dB, dC, ddt, dal = ssd_scan_bwd(act, dt, sv["alog"], sv["st1"], dy, dxa, reverse=True, prev=(dB, dC, ddt, dal))
    dact = jnp.concatenate([dxa, dB, dC], axis=1)
    dxbc, dconv_w, dconv_b = ssd_conv_bwd(u, p["conv_w"], p["conv_b"], dact)
    ddt_all = _ssd_head_layout(ddt)
    ddt_raw, ddt_bias = rowwise_bwd("ssd_dt_bwd", _dt_fn, [(u, (SSD_DI + SSD_CONV_CH) // 128, False)],
                                    [(p["dt_bias"], False)], [ddt_all], W=128, diff_rows=[0], diff_shared=[0])
    du = _cat_mxu([dz, dxbc, ddt_raw, jnp.zeros((S, SSD_IN_PAD - SSD_IN - 64), F32)])
    dx_prev, dng, dw_in = _in_out_bwd("ssd", sv["x"], sv["ng"], sv["hn"], du, p["w_in"], dx)
    grads = dict(
        w_in=dw_in[:, :SSD_IN], w_out=dw_out, conv_w=dconv_w[:SSD_CONV], conv_b=dconv_b,
        dt_bias=ddt_bias[:, :2 * SSD_HEADS], a_log=_ssd_head_layout(dal)[:, :2 * SSD_HEADS],
        d=ddexp.reshape(SSD_HEADS, SSD_HEADDIM).sum(axis=1)[None, :], norm_g=dnorm, ng=dng)
    return dx_prev, grads, cres


def hg_layer_fwd(x, ng, p, comm0=None, comm1=None):
    hn = _prenorm("hg", x, ng)
    u = matmul(hn, p["w_in"], name="hg_in")
    lb = hg_lb_fwd(p["hgrn_lb"])
    (o0, st0), cres0 = _own(hg_scan_fwd(u, lb, reverse=False, comm=comm0), comm0)
    (o, st1), cres1 = _own(hg_scan_fwd(u, lb, reverse=True, o_prev=o0, comm=comm1), comm1)
    g = rowwise_fwd("hg_post", _hg_post_fn, [(o, 0, True), (u, 4 * HG_HEADS, True)], [(p["norm_g"], True)],
                    [(HG_D, MXU_DTYPE)], W=HG_D, ncb=HG_HEADS)[0]
    xn = matmul(g, p["w_out"], residual=x, name="hg_out")
    return xn, dict(x=x, ng=ng, hn=hn, u=u, lb=lb, o=o, st0=st0, st1=st1, g=g), cres0, cres1


def hg_layer_bwd(sv, p, dx, comm=None):
    u, lb = sv["u"], sv["lb"]
    dg = matmul(dx, p["w_out"], tb=True, name="hg_dg")
    dw_out = matmul(sv["g"], dx, ta=True, out_dtype=GRAD_WIRE_DTYPE, name="hg_dw_out")
    do, dgate, dnorm = rowwise_bwd("hg_post_bwd", _hg_post_fn, [(sv["o"], 0, True), (u, 4 * HG_HEADS, True)],
                                   [(p["norm_g"], True)], [dg], W=HG_D, ncb=HG_HEADS, diff_rows=[0, 1], diff_shared=[0])
    (dq0, df0, dv0, dlb0), cres = _own(hg_scan_bwd(u, lb, sv["st0"], do, reverse=False, comm=comm), comm)
    dq, df1, dv, dlb = hg_scan_bwd(u, lb, sv["st1"], do, reverse=True, prev=(dq0, dv0, dlb0))
    du = _cat_mxu([dq, df0, df1, dv, dgate])
    dhgrn_lb = hg_lb_bwd(p["hgrn_lb"], dlb)
    dx_prev, dng, dw_in = _in_out_bwd("hg", sv["x"], sv["ng"], sv["hn"], du, p["w_in"], dx)
    return dx_prev, dict(w_in=dw_in, w_out=dw_out, norm_g=dnorm, hgrn_lb=dhgrn_lb, ng=dng), cres


def _rope_consts(S):
    ar, ac = _rope_tables(S)
    ct = np.concatenate([np.cos(ar), np.cos(ar), np.cos(ac), np.cos(ac)], axis=1).astype(np.float32)
    st = np.concatenate([-np.sin(ar), np.sin(ar), -np.sin(ac), np.sin(ac)], axis=1).astype(np.float32)
    return jnp.asarray(ct), jnp.asarray(st)


def _at_qk(tag, u, col0, nheads, scale, gain, consts, cot=None):
    ct, st = consts
    rows = [(u, col0, True), (ct, 0, False), (st, 0, False)]
    shared = [(gain, False)]
    if cot is None:
        return rowwise_fwd(f"at_{tag}", _make_qk_fn(scale), rows, shared, [(AT_HD, MXU_DTYPE)], W=AT_HD, ncb=nheads)[0]
    return rowwise_bwd(f"at_{tag}_bwd", _make_qk_fn(scale), rows, shared, [cot], W=AT_HD, ncb=nheads,
                       diff_rows=[0], diff_shared=[0])


def at_layer_fwd(x, ng, p):
    S = x.shape[0]
    hn = _prenorm("at", x, ng)
    u = matmul(hn, p["w_in"], name="at_in")
    consts = _rope_consts(S)
    qr = _at_qk("q", u, 0, AT_HEADS, AT_HD ** -0.5, p["q_g"], consts)
    kr = _at_qk("k", u, AT_HEADS, AT_KV, 1.0, p["k_g"], consts)
    vc0 = (AT_QW + AT_KW) // AT_HD
    o, lse = flash_fwd(qr, kr, u, v_col0=vc0)
    g = rowwise_fwd("at_gate", _gate_fn, [(o, 0, True), (u, (AT_QW + 2 * AT_KW) // 1024, True)], [],
                    [(1024, MXU_DTYPE)], W=1024, ncb=AT_QW // 1024)[0]
    xn = matmul(g, p["w_out"], residual=x, name="at_out")
    return xn, dict(x=x, ng=ng, hn=hn, u=u, qr=qr, kr=kr, o=o, lse=lse, g=g)


def at_layer_bwd(sv, p, dx, comm=None):
    u, qr, kr = sv["u"], sv["qr"], sv["kr"]
    S = u.shape[0]
    consts = _rope_consts(S)
    vc0 = (AT_QW + AT_KW) // AT_HD
    dg = matmul(dx, p["w_out"], tb=True, name="at_dg")
    dw_out = matmul(sv["g"], dx, ta=True, out_dtype=GRAD_WIRE_DTYPE, name="at_dw_out")
    do, dgate = rowwise_bwd("at_gate_bwd", _gate_fn, [(sv["o"], 0, True), (u, (AT_QW + 2 * AT_KW) // 1024, True)], [],
                            [dg], W=1024, ncb=AT_QW // 1024, diff_rows=[0, 1], diff_shared=[])
    dqs, delta = flash_bwd_dq(qr, kr, u, sv["o"], sv["lse"], do, v_col0=vc0)
    (dkr, dv), cres = _own(flash_bwd_dkv(qr, kr, u, sv["lse"], delta, do, v_col0=vc0, comm=comm), comm)
    dq_raw, dqg = _at_qk("q", u, 0, AT_HEADS, AT_HD ** -0.5, p["q_g"], consts, cot=dqs)
    dk_raw, dkg = _at_qk("k", u, AT_HEADS, AT_KV, 1.0, p["k_g"], consts, cot=dkr)
    du = _cat_mxu([dq_raw, dk_raw, dv, dgate])
    dx_prev, dng, dw_in = _in_out_bwd("at", sv["x"], sv["ng"], sv["hn"], du, p["w_in"], dx)
    return dx_prev, dict(w_in=dw_in, w_out=dw_out, q_g=dqg, k_g=dkg, ng=dng), cres


def _to_stream(t, dil):
    S = t.shape[0]
    return t.reshape(S // dil, dil, DL_HEADS, DL_HD).transpose(2, 1, 0, 3)


def _from_stream(t):
    H, dil, Ls, E = t.shape
    return t.transpose(2, 1, 0, 3).reshape(Ls * dil, H * E)


def _stream_to_hm(t):
    H, dil, Ls, w = t.shape
    return t.transpose(0, 2, 1, 3).reshape(H * Ls * dil, w)


def _hm_to_stream(t, dil):
    w = t.shape[1]
    S = t.shape[0] // DL_HEADS
    return t.reshape(DL_HEADS, S // dil, dil, w).transpose(0, 2, 1, 3)


def _pad_l(t):
    return jnp.pad(t, ((0, 0), (0, 0), (DL_HALF, DL_HALF), (0, 0)))


OX_LSE = DL_HD
DOX_LSE, DOX_DM = DL_HD, DL_HD + 32


def _win(p_ref, c_ref, n_ref, h, T):
    return jnp.concatenate([p_ref[h, 0, T - DL_HALF:T, :], c_ref[h, 0], n_ref[h, 0, 0:DL_HALF, :]], axis=0)


def _win_specs(T, E, nb):
    return [pl.BlockSpec((DL_HEADS, 1, T, E), lambda d, n: (0, d, jnp.maximum(n - 1, 0), 0)),
            pl.BlockSpec((DL_HEADS, 1, T, E), lambda d, n: (0, d, n, 0)),
            pl.BlockSpec((DL_HEADS, 1, T, E), lambda d, n: (0, d, jnp.minimum(n + 1, nb - 1), 0))]


def _band_mask_q(n, T, W, Ls):
    i = lax.broadcasted_iota(jnp.int32, (T, W), 0)
    j = lax.broadcasted_iota(jnp.int32, (T, W), 1)
    kpos = n * T + j - DL_HALF
    return (jnp.abs(j - DL_HALF - i) <= DL_HALF) & (kpos >= 0) & (kpos < Ls)


def band_fwd(q, k, v, bias, *, scale):
    H, dil, Ls, E = q.shape
    T, W = _dl_tiles(Ls)
    nb = Ls // T

    def body(q_ref, kp_ref, kc_ref, kn_ref, vp_ref, vc_ref, vn_ref, b_ref, ox_ref):
        n = pl.program_id(1)
        mask = _band_mask_q(n, T, W, Ls)
        for h in range(H):
            kw = _win(kp_ref, kc_ref, kn_ref, h, T)
            vw = _win(vp_ref, vc_ref, vn_ref, h, T)
            s = _dot(q_ref[h, 0], kw, ((1,), (1,))) * scale + b_ref[h]
            s = jnp.where(mask, s, NEG_BIG)
            m = jnp.max(s, axis=1, keepdims=True)
            lse = m + jnp.log(jnp.sum(jnp.exp(s - m), axis=1, keepdims=True))
            p = jnp.exp(s - lse)
            ox_ref[h, 0, :, 0:E] = _dot(p, vw, ((1,), (0,)))
            ox_ref[h, 0, :, E:2 * E] = lse + jnp.zeros((T, E), F32)

    cur = pl.BlockSpec((H, 1, T, E), lambda d, n: (0, d, n, 0))
    return pl.pallas_call(
        body, name=f"band_fwd_{dil}", grid=(dil, nb),
        in_specs=[cur] + _win_specs(T, E, nb) + _win_specs(T, E, nb) + [pl.BlockSpec((H, T, W), lambda d, n: (0, 0, 0))],
        out_specs=pl.BlockSpec((H, 1, T, 2 * E), lambda d, n: (0, d, n, 0)),
        out_shape=jax.ShapeDtypeStruct((H, dil, Ls, 2 * E), F32),
        compiler_params=_cp("parallel", "parallel"),
    )(q, k, k, k, v, v, v, bias)


def band_bwd_dq(q, k, v, bias, dox, *, scale):
    H, dil, Ls, E = q.shape
    T, W = _dl_tiles(Ls)
    nb = Ls // T

    def body(q_ref, kp_ref, kc_ref, kn_ref, vp_ref, vc_ref, vn_ref, b_ref, dox_ref, dq_ref, db_ref):
        d, n = pl.program_id(0), pl.program_id(1)
        mask = _band_mask_q(n, T, W, Ls)
        first = jnp.logical_and(d == 0, n == 0)

        @pl.when(first)
        def _():
            db_ref[...] = jnp.zeros_like(db_ref)

        for h in range(H):
            kw = _win(kp_ref, kc_ref, kn_ref, h, T)
            vw = _win(vp_ref, vc_ref, vn_ref, h, T)
            dox = dox_ref[h, 0]
            do, lse, dm = dox[:, 0:E], dox[:, DOX_LSE:DOX_LSE + 1], dox[:, DOX_DM:DOX_DM + 1]
            s = _dot(q_ref[h, 0], kw, ((1,), (1,))) * scale + b_ref[h]
            p = jnp.where(mask, jnp.exp(jnp.where(mask, s, 0.0) - lse), 0.0)
            dp = _dot(do, vw, ((1,), (1,)))
            ds = p * (dp - dm)
            dq_ref[h, 0] = (_dot(ds, kw, ((1,), (0,))) * scale).astype(dq_ref.dtype)
            db_ref[h] += ds

    cur = pl.BlockSpec((H, 1, T, E), lambda d, n: (0, d, n, 0))
    bspec = pl.BlockSpec((H, T, W), lambda d, n: (0, 0, 0))
    return pl.pallas_call(
        body, name=f"band_bwd_dq_{dil}", grid=(dil, nb),
        in_specs=[cur] + _win_specs(T, E, nb) + _win_specs(T, E, nb) + [bspec,
                  pl.BlockSpec((H, 1, T, 2 * E), lambda d, n: (0, d, n, 0))],
        out_specs=[cur, bspec],
        out_shape=[jax.ShapeDtypeStruct((H, dil, Ls, E), MXU_DTYPE), jax.ShapeDtypeStruct((H, T, W), F32)],
        compiler_params=_cp("arbitrary", "arbitrary"),
    )(q, k, k, k, v, v, v, bias, dox)


def band_bwd_dkv(q, k, v, bias_t, dox, *, scale):
    H, dil, Ls, E = k.shape
    T, W = _dl_tiles(Ls)
    nb = Ls // T

    def body(qp_ref, qc_ref, qn_ref, k_ref, v_ref, b_ref, dp_ref, dc_ref, dn_ref, dk_ref, dv_ref):
        n = pl.program_id(1)
        iw = lax.broadcasted_iota(jnp.int32, (W, T), 0)
        j = lax.broadcasted_iota(jnp.int32, (W, T), 1)
        qpos = n * T + iw - DL_HALF
        mask = (jnp.abs(j + DL_HALF - iw) <= DL_HALF) & (qpos >= 0) & (qpos < Ls)
        for h in range(H):
            qw = _win(qp_ref, qc_ref, qn_ref, h, T)
            doxw = _win(dp_ref, dc_ref, dn_ref, h, T)
            dow, lsew, dmw = doxw[:, 0:E], doxw[:, DOX_LSE:DOX_LSE + 1], doxw[:, DOX_DM:DOX_DM + 1]
            s = _dot(qw, k_ref[h, 0], ((1,), (1,))) * scale + b_ref[h]
            p = jnp.where(mask, jnp.exp(jnp.where(mask, s, 0.0) - lsew), 0.0)
            dv_ref[h, 0] = _dot(p, dow, ((0,), (0,))).astype(dv_ref.dtype)
            dp = _dot(dow, v_ref[h, 0], ((1,), (1,)))
            ds = p * (dp - dmw)
            dk_ref[h, 0] = (_dot(ds, qw, ((0,), (0,))) * scale).astype(dk_ref.dtype)

    cur = pl.BlockSpec((H, 1, T, E), lambda d, n: (0, d, n, 0))
    return pl.pallas_call(
        body, name=f"band_bwd_dkv_{dil}", grid=(dil, nb),
        in_specs=_win_specs(T, E, nb) + [cur, cur, pl.BlockSpec((H, W, T), lambda d, n: (0, 0, 0))]
        + _win_specs(T, 2 * E, nb),
        out_specs=[cur, cur],
        out_shape=[jax.ShapeDtypeStruct((H, dil, Ls, E), MXU_DTYPE)] * 2,
        compiler_params=_cp("parallel", "parallel"),
    )(q, q, q, k, v, bias_t, dox, dox, dox)


def dl_merge_fwd(oxs, *, R=1024):
    rows = oxs[0].shape[0]
    R = min(R, rows)
    E = DL_HD

    def body(a_ref, b_ref, c_ref, o_ref):
        vals = [r[...] for r in (a_ref, b_ref, c_ref)]
        o_ref[...] = _dl_merge_fn(*[t[:, 0:E] for t in vals], *[t[:, OX_LSE:OX_LSE + 1] for t in vals])[0]

    spec = pl.BlockSpec((R, 2 * E), lambda i: (i, 0))
    return pl.pallas_call(
        body, name="dl_merge", grid=(rows // R,), in_specs=[spec] * 3,
        out_specs=pl.BlockSpec((R, E), lambda i: (i, 0)), out_shape=jax.ShapeDtypeStruct((rows, E), F32),
        compiler_params=_cp("parallel"),
    )(*oxs)


def dl_merge_bwd(oxs, do, *, R=1024):
    rows = oxs[0].shape[0]
    R = min(R, rows)
    E = DL_HD

    def body(a_ref, b_ref, c_ref, do_ref, da_ref, db_ref, dc_ref):
        vals = [r[...] for r in (a_ref, b_ref, c_ref)]
        os_ = [t[:, 0:E] for t in vals]
        ls_ = [t[:, OX_LSE:OX_LSE + 1] for t in vals]
        _, vjp = jax.vjp(_dl_merge_fn, *os_, *ls_)
        g = vjp((do_ref[...],))
        for k, d_ref in enumerate((da_ref, db_ref, dc_ref)):
            dm = jnp.sum(g[k] * os_[k], axis=1, keepdims=True) - g[3 + k]
            d_ref[:, 0:E] = g[k]
            d_ref[:, DOX_LSE:DOX_DM] = ls_[k] + jnp.zeros((R, DOX_DM - DOX_LSE), F32)
            d_ref[:, DOX_DM:2 * E] = dm + jnp.zeros((R, 2 * E - DOX_DM), F32)

    spec = pl.BlockSpec((R, 2 * E), lambda i: (i, 0))
    return pl.pallas_call(
        body, name="dl_merge_bwd", grid=(rows // R,), in_specs=[spec] * 3 + [pl.BlockSpec((R, E), lambda i: (i, 0))],
        out_specs=[spec] * 3, out_shape=[jax.ShapeDtypeStruct((rows, 2 * E), F32)] * 3,
        compiler_params=_cp("parallel"),
    )(*oxs, do)


def _dl_bias_tables(rel_bias, dil, T):
    W = T + 2 * DL_HALF
    bq, bk = _dl_bucket_tables(dil, T)
    idx = np.concatenate([bq.reshape(-1), bk.reshape(-1)])
    onehot_t = (np.arange(REL_BUCKETS)[:, None] == idx[None, :]).astype(np.float32)
    tab = matmul(rel_bias.T, jnp.asarray(onehot_t), exact=True, name=f"dl_bias_{dil}", tm=DL_HEADS, tk=REL_BUCKETS,
                 tn=_tile(2 * T * W, (8192, 4096, 2048, 1024, 512, 256, 128)))
    return tab[:, :T * W].reshape(DL_HEADS, T, W), tab[:, T * W:].reshape(DL_HEADS, W, T), bq


def _dl_dm_fn(do, o, dl):
    return (jnp.sum(do * o, axis=-1, keepdims=True) - dl,)


def _old_dl_layer_fwd(x, ng, p):
    S = x.shape[0]
    hn = _prenorm("dl", x, ng)
    u = matmul(hn, p["w_in"], name="dl_in")
    scale = DL_HD ** -0.5
    per_group, o_hm, lse_hm = [], [], []
    for gi, (window, dil) in enumerate(DL_PAIRS):
        base = gi * 3 * DL_W
        Ls = S // dil
        T, _ = _dl_tiles(Ls)
        bq, bk = _dl_bucket_tables(dil, T)
        qs = _to_stream(u[:, base:base + DL_W], dil).astype(MXU_DTYPE)
        ks = _to_stream(u[:, base + DL_W:base + 2 * DL_W], dil).astype(MXU_DTYPE)
        vs = _to_stream(u[:, base + 2 * DL_W:base + 3 * DL_W], dil).astype(MXU_DTYPE)
        bias = p["rel_bias"][bq].transpose(2, 0, 1)
        o_s, lse_s = band_fwd(qs, _pad_l(ks), _pad_l(vs), bias, scale=scale)
        per_group.append(dict(qs=qs, ks=ks, vs=vs, lse_s=lse_s, bq=bq, bk=bk, dil=dil))
        o_hm.append(_stream_to_hm(o_s))
        lse_hm.append(_stream_to_hm(lse_s))
    rows = [(t, 0, False) for t in o_hm] + [(t, 0, False, 1) for t in lse_hm]
    om = rowwise_fwd("dl_merge", _dl_merge_fn, rows, [], [(DL_HD, F32)], W=DL_HD)[0]
    o = om.reshape(DL_HEADS, S, DL_HD).transpose(1, 0, 2).reshape(S, DL_W)
    g = rowwise_fwd("dl_gate", _gate_fn, [(o, 0, False), (u, 9, False)], [], [(DL_W, MXU_DTYPE)], W=DL_W)[0]
    xn = matmul(g, p["w_out"], residual=x, name="dl_out")
    return xn, dict(x=x, ng=ng, hn=hn, u=u, per_group=per_group, o_hm=o_hm, lse_hm=lse_hm, o=o, g=g)


def _old_dl_layer_bwd(sv, p, dx):
    u = sv["u"]
    S = u.shape[0]
    scale = DL_HD ** -0.5
    dg = matmul(dx, p["w_out"], tb=True, name="dl_dg")
    dw_out = matmul(sv["g"], dx, ta=True, out_dtype=GRAD_WIRE_DTYPE, name="dl_dw_out")
    do, dgate = rowwise_bwd("dl_gate_bwd", _gate_fn, [(sv["o"], 0, False), (sv["ugate"], 0, False)], [], [dg], W=DL_W,
                            diff_rows=[0, 1], diff_shared=[])
    do_hm = do.reshape(S, DL_HEADS, DL_HD).transpose(1, 0, 2).reshape(DL_HEADS * S, DL_HD)
    rows = [(t, 0, False) for t in sv["o_hm"]] + [(t, 0, False, 1) for t in sv["lse_hm"]]
    dmerge = rowwise_bwd("dl_merge_bwd", _dl_merge_fn, rows, [], [do_hm], W=DL_HD, diff_rows=[0, 1, 2, 3, 4, 5],
                         diff_shared=[])
    parts, dbs, onehots = [], [], []
    for gi, pg in enumerate(sv["per_group"]):
        dil = pg["dil"]
        Ls = S // dil
        T, W = _dl_tiles(Ls)
        dog, dlg = dmerge[gi], dmerge[3 + gi]
        dm = rowwise_fwd(f"dl_dm_{gi}", _dl_dm_fn, [(dog, 0, False), (sv["o_hm"][gi], 0, False), (dlg, 0, False, 1)], [],
                         [(1, F32)], W=DL_HD)[0]
        do_s, dm_s = _hm_to_stream(dog, dil), _hm_to_stream(dm, dil)
        bias = p["rel_bias"][pg["bq"]].transpose(2, 0, 1)
        bias_t = p["rel_bias"][pg["bk"]].transpose(2, 0, 1)
        kp, vp = _pad_l(pg["ks"]), _pad_l(pg["vs"])
        dq_s, dbias = band_bwd_dq(pg["qs"], kp, vp, bias, pg["lse_s"], dm_s, do_s, scale=scale)
        dk_s, dv_s = band_bwd_dkv(_pad_l(pg["qs"]), pg["ks"], pg["vs"], bias_t, _pad_l(pg["lse_s"]), _pad_l(dm_s),
                                  _pad_l(do_s), scale=scale)
        parts += [_from_stream(dq_s), _from_stream(dk_s), _from_stream(dv_s)]
        dbs.append(dbias.reshape(DL_HEADS, T * W))
        onehots.append((pg["bq"].reshape(-1)[:, None] == np.arange(REL_BUCKETS)[None, :]).astype(np.float32))
    drel = matmul(jnp.concatenate(dbs, axis=1), jnp.asarray(np.concatenate(onehots, axis=0)), exact=True,
                  name="dl_drel", tm=DL_HEADS, tn=REL_BUCKETS, tk=2048)
    du = _cat_mxu(parts + [dgate])
    dx_prev, dng, dw_in = _in_out_bwd("dl", sv["x"], sv["ng"], sv["hn"], du, p["w_in"], dx)
    return dx_prev, dict(w_in=dw_in, w_out=dw_out, rel_bias=drel.T, ng=dng)


def dl_layer_fwd(x, ng, p):
    S = x.shape[0]
    hn = _prenorm("dl", x, ng)
    nqkv = 3 * len(DL_PAIRS) * DL_W
    uqkv = matmul(hn, p["w_in"], name="dl_in_qkv", b_cols=(0, nqkv), out_dtype=MXU_DTYPE)
    ugate = matmul(hn, p["w_in"], name="dl_in_gate", b_cols=(nqkv, DL_W))
    scale = DL_HD ** -0.5
    per_group, ox_hm = [], []
    for gi, (window, dil) in enumerate(DL_PAIRS):
        base = gi * 3 * DL_W
        T, _ = _dl_tiles(S // dil)
        qs, ks, vs = [_to_stream(uqkv[:, base + c * DL_W:base + (c + 1) * DL_W], dil) for c in range(3)]
        bias, bias_t, bq = _dl_bias_tables(p["rel_bias"], dil, T)
        ox_s = band_fwd(qs, ks, vs, bias, scale=scale)
        per_group.append(dict(qs=qs, ks=ks, vs=vs, bias=bias, bias_t=bias_t, bq=bq, dil=dil))
        ox_hm.append(_stream_to_hm(ox_s))
    om = dl_merge_fwd(ox_hm)
    o = om.reshape(DL_HEADS, S, DL_HD).transpose(1, 0, 2).reshape(S, DL_W)
    g = rowwise_fwd("dl_gate", _gate_fn, [(o, 0, False), (ugate, 0, False)], [], [(DL_W, MXU_DTYPE)], W=DL_W)[0]
    xn = matmul(g, p["w_out"], residual=x, name="dl_out")
    return xn, dict(x=x, ng=ng, hn=hn, ugate=ugate, per_group=per_group, ox_hm=ox_hm, o=o, g=g)


def dl_layer_bwd(sv, p, dx):
    ugate = sv["ugate"]
    S = ugate.shape[0]
    scale = DL_HD ** -0.5
    dg = matmul(dx, p["w_out"], tb=True, name="dl_dg")
    dw_out = matmul(sv["g"], dx, ta=True, out_dtype=GRAD_WIRE_DTYPE, name="dl_dw_out")
    do, dgate = rowwise_bwd("dl_gate_bwd", _gate_fn, [(sv["o"], 0, False), (sv["ugate"], 0, False)], [], [dg], W=DL_W,
                            diff_rows=[0, 1], diff_shared=[])
    do_hm = do.reshape(S, DL_HEADS, DL_HD).transpose(1, 0, 2).reshape(DL_HEADS * S, DL_HD)
    dox_hm = dl_merge_bwd(sv["ox_hm"], do_hm)
    parts, dbs, onehots = [], [], []
    for gi, pg in enumerate(sv["per_group"]):
        dil = pg["dil"]
        T, W = _dl_tiles(S // dil)
        dox_s = _hm_to_stream(dox_hm[gi], dil)
        dq_s, dbias = band_bwd_dq(pg["qs"], pg["ks"], pg["vs"], pg["bias"], dox_s, scale=scale)
        dk_s, dv_s = band_bwd_dkv(pg["qs"], pg["ks"], pg["vs"], pg["bias_t"], dox_s, scale=scale)
        parts += [_from_stream(dq_s), _from_stream(dk_s), _from_stream(dv_s)]
        dbs.append(dbias.reshape(DL_HEADS, T * W))
        onehots.append((pg["bq"].reshape(-1)[:, None] == np.arange(REL_BUCKETS)[None, :]).astype(np.float32))
    drel = matmul(jnp.concatenate(dbs, axis=1), jnp.asarray(np.concatenate(onehots, axis=0)), exact=True,
                  name="dl_drel", tm=DL_HEADS, tn=REL_BUCKETS, tk=2048)
    du = _cat_mxu(parts + [dgate])
    dx_prev, dng, dw_in = _in_out_bwd("dl", sv["x"], sv["ng"], sv["hn"], du, p["w_in"], dx)
    return dx_prev, dict(w_in=dw_in, w_out=dw_out, rel_bias=drel.T, ng=dng)


WEIGHT_ORDER = ['norm_g', 'final_g', 'rel_bias', 'hgrn_lb', 'ssd_w_in', 'ssd_conv_w', 'ssd_conv_b', 'ssd_dt_bias',
                'ssd_a_log', 'ssd_d', 'ssd_norm_g', 'ssd_w_out', 'hg_w_in', 'hg_norm_g', 'hg_w_out', 'at_w_in',
                'at_q_norm_g', 'at_k_norm_g', 'at_w_out', 'dl_w_in', 'dl_w_out']
BIG_IN = ['ssd_w_in', 'hg_w_in', 'at_w_in', 'dl_w_in']
BIG_OUT = ['ssd_w_out', 'hg_w_out', 'at_w_out', 'dl_w_out']
BIG = BIG_IN + BIG_OUT
SMALL = [n for n in WEIGHT_ORDER if n not in BIG]
LANES = 128


def _pack(arrs):
    flat = jnp.concatenate([a.reshape(-1).astype(F32) for a in arrs])
    n = flat.shape[0]
    rows = -(-n // (8 * LANES)) * 8
    return jnp.pad(flat, (0, rows * LANES - n)).reshape(rows, LANES)


def _unpack(buf, shapes):
    flat = buf.reshape(-1)
    out, off = [], 0
    for shp in shapes:
        n = int(np.prod(shp)) if len(shp) else 1
        out.append(flat[off:off + n].reshape(shp))
        off += n
    return out


def kernel(x, norm_g, final_g, rel_bias, hgrn_lb, ssd_w_in, ssd_conv_w, ssd_conv_b, ssd_dt_bias, ssd_a_log, ssd_d, ssd_norm_g, ssd_w_out, hg_w_in, hg_norm_g, hg_w_out, at_w_in, at_q_norm_g, at_k_norm_g, at_w_out, dl_w_in, dl_w_out, loss_target, m_norm_g, m_final_g, m_rel_bias, m_hgrn_lb, m_ssd_w_in, m_ssd_conv_w, m_ssd_conv_b, m_ssd_dt_bias, m_ssd_a_log, m_ssd_d, m_ssd_norm_g, m_ssd_w_out, m_hg_w_in, m_hg_norm_g, m_hg_w_out, m_at_w_in, m_at_q_norm_g, m_at_k_norm_g, m_at_w_out, m_dl_w_in, m_dl_w_out, v_norm_g, v_final_g, v_rel_bias, v_hgrn_lb, v_ssd_w_in, v_ssd_conv_w, v_ssd_conv_b, v_ssd_dt_bias, v_ssd_a_log, v_ssd_d, v_ssd_norm_g, v_ssd_w_out, v_hg_w_in, v_hg_norm_g, v_hg_w_out, v_at_w_in, v_at_q_norm_g, v_at_k_norm_g, v_at_w_out, v_dl_w_in, v_dl_w_out):
    w = dict(norm_g=norm_g, final_g=final_g, rel_bias=rel_bias, hgrn_lb=hgrn_lb, ssd_w_in=ssd_w_in, ssd_conv_w=ssd_conv_w, ssd_conv_b=ssd_conv_b, ssd_dt_bias=ssd_dt_bias, ssd_a_log=ssd_a_log, ssd_d=ssd_d, ssd_norm_g=ssd_norm_g, ssd_w_out=ssd_w_out, hg_w_in=hg_w_in, hg_norm_g=hg_norm_g, hg_w_out=hg_w_out, at_w_in=at_w_in, at_q_norm_g=at_q_norm_g, at_k_norm_g=at_k_norm_g, at_w_out=at_w_out, dl_w_in=dl_w_in, dl_w_out=dl_w_out)
    m = dict(norm_g=m_norm_g, final_g=m_final_g, rel_bias=m_rel_bias, hgrn_lb=m_hgrn_lb, ssd_w_in=m_ssd_w_in, ssd_conv_w=m_ssd_conv_w, ssd_conv_b=m_ssd_conv_b, ssd_dt_bias=m_ssd_dt_bias, ssd_a_log=m_ssd_a_log, ssd_d=m_ssd_d, ssd_norm_g=m_ssd_norm_g, ssd_w_out=m_ssd_w_out, hg_w_in=m_hg_w_in, hg_norm_g=m_hg_norm_g, hg_w_out=m_hg_w_out, at_w_in=m_at_w_in, at_q_norm_g=m_at_q_norm_g, at_k_norm_g=m_at_k_norm_g, at_w_out=m_at_w_out, dl_w_in=m_dl_w_in, dl_w_out=m_dl_w_out)
    v = dict(norm_g=v_norm_g, final_g=v_final_g, rel_bias=v_rel_bias, hgrn_lb=v_hgrn_lb, ssd_w_in=v_ssd_w_in, ssd_conv_w=v_ssd_conv_w, ssd_conv_b=v_ssd_conv_b, ssd_dt_bias=v_ssd_dt_bias, ssd_a_log=v_ssd_a_log, ssd_d=v_ssd_d, ssd_norm_g=v_ssd_norm_g, ssd_w_out=v_ssd_w_out, hg_w_in=v_hg_w_in, hg_norm_g=v_hg_norm_g, hg_w_out=v_hg_w_out, at_w_in=v_at_w_in, at_q_norm_g=v_at_q_norm_g, at_k_norm_g=v_at_k_norm_g, at_w_out=v_at_w_out, dl_w_in=v_dl_w_in, dl_w_out=v_dl_w_out)
    me = 4 * lax.axis_index("x") + 2 * lax.axis_index("y") + lax.axis_index("c")
    xs = x[0]
    S = xs.shape[0]

    shard2d = {n: w[n][0] for n in BIG}
    wire = {n: shard2d[n].astype(MXU_DTYPE) for n in BIG}

    def ag(names):
        return ("ag", [wire[n] for n in names])

    def assemble(names, blks):
        out = {}
        for n, blk in zip(names, blks):
            r, c = shard2d[n].shape
            out[n] = blk.transpose(1, 0, 2).reshape(r, N_DEV * c) if n in BIG_IN else blk.reshape(N_DEV * r, c)
        return out

    def a2a(names, gw):
        bufs = []
        for n in names:
            r, c = shard2d[n].shape
            bufs.append(gw[n].reshape(r, N_DEV, c).transpose(1, 0, 2) if n in BIG_IN else gw[n].reshape(N_DEV, r, c))
        return ("a2a", bufs)

    ssd_w, hg_w, at_w, dl_w = (["ssd_w_in", "ssd_w_out"], ["hg_w_in", "hg_w_out"], ["at_w_in", "at_w_out"],
                               ["dl_w_in", "dl_w_out"])
    full = assemble(ssd_w, comm_only(ag(ssd_w), name="allgather_ssd_weights"))
    ncw = ssd_conv_w.shape[2]
    nhg = hg_norm_g.shape[1]
    small_shard = jnp.zeros((8, 512), F32)
    small_shard = small_shard.at[:SSD_CONV, :ncw].set(ssd_conv_w[0]).at[SSD_CONV, :nhg].set(hg_norm_g[0])
    small_all = allgather_two_level(small_shard, name="allgather_small_weights")
    conv_w_full = small_all[:, :SSD_CONV, :ncw].transpose(1, 0, 2).reshape(SSD_CONV, N_DEV * ncw)
    hg_norm_full = small_all[:, SSD_CONV, :nhg].reshape(1, N_DEV * nhg)

    p_ssd = dict(w_in=jnp.pad(full["ssd_w_in"], ((0, 0), (0, SSD_IN_PAD - SSD_IN))), w_out=full["ssd_w_out"],
                 conv_w=conv_w_full, conv_b=ssd_conv_b,
                 dt_bias=jnp.pad(ssd_dt_bias.reshape(1, 2 * SSD_HEADS), ((0, 0), (0, 128 - 2 * SSD_HEADS))),
                 alog=jnp.pad(ssd_a_log.reshape(1, 2 * SSD_HEADS), ((0, 0), (0, 128 - 2 * SSD_HEADS))),
                 dexp=jnp.repeat(ssd_d.reshape(-1), SSD_HEADDIM)[None, :], norm_g=ssd_norm_g)
    x1, sv0, got = ssd_layer_fwd(xs, norm_g[0:1], p_ssd, comm=ag(hg_w))
    full.update(assemble(hg_w, got))
    p_hg = dict(w_in=full["hg_w_in"], w_out=full["hg_w_out"], norm_g=hg_norm_full, hgrn_lb=hgrn_lb)
    x2, sv1, got_at, got_dl = hg_layer_fwd(x1, norm_g[1:2], p_hg, comm0=ag(at_w), comm1=ag(dl_w))
    full.update(assemble(at_w, got_at))
    full.update(assemble(dl_w, got_dl))
    p_at = dict(w_in=full["at_w_in"], w_out=full["at_w_out"], q_g=at_q_norm_g, k_g=at_k_norm_g)
    p_dl = dict(w_in=full["dl_w_in"], w_out=full["dl_w_out"], rel_bias=rel_bias)
    x3, sv2 = at_layer_fwd(x2, norm_g[2:3], p_at)
    x4, sv3 = dl_layer_fwd(x3, norm_g[3:4], p_dl)
    loss_part, dx4, dfinal = loss_head(x4, final_g[None, :], loss_target[0])
    dx3, g3 = dl_layer_bwd(sv3, p_dl, dx4)
    dx2, g2, recv_dl = at_layer_bwd(sv2, p_at, dx3, comm=a2a(dl_w, dict(dl_w_in=g3["w_in"], dl_w_out=g3["w_out"])))
    dx1, g1, recv_at = hg_layer_bwd(sv1, p_hg, dx2, comm=a2a(at_w, dict(at_w_in=g2["w_in"], at_w_out=g2["w_out"])))
    dx0, g0, recv_hg = ssd_layer_bwd(sv0, p_ssd, dx1, comm=a2a(hg_w, dict(hg_w_in=g1["w_in"], hg_w_out=g1["w_out"])))
    recv_ssd = comm_only(a2a(ssd_w, dict(ssd_w_in=g0["w_in"], ssd_w_out=g0["w_out"])), name="exchange_ssd_grads")
    recv = dict(zip(ssd_w + hg_w + at_w + dl_w, recv_ssd + recv_hg + recv_at + recv_dl))

    small_full = dict(
        norm_g=jnp.concatenate([g0["ng"], g1["ng"], g2["ng"], g3["ng"]], axis=0), final_g=dfinal[0],
        rel_bias=g3["rel_bias"], hgrn_lb=g1["hgrn_lb"], ssd_conv_w=g0["conv_w"][None], ssd_conv_b=g0["conv_b"],
        ssd_dt_bias=g0["dt_bias"].reshape(1, 2, SSD_HEADS), ssd_a_log=g0["a_log"].reshape(1, 2, SSD_HEADS),
        ssd_d=g0["d"], ssd_norm_g=g0["norm_g"], hg_norm_g=g1["norm_g"], at_q_norm_g=g2["q_g"], at_k_norm_g=g2["k_g"])
    packed = _pack([loss_part[0, 0:1]] + [small_full[n] for n in SMALL])
    summed = sum_parts(allgather_two_level(packed, name="allgather_small_grads"), name="sum_small_grads")
    parts = _unpack(summed, [()] + [small_full[n].shape for n in SMALL])
    loss = parts[0]
    gsmall = dict(zip(SMALL, parts[1:]))
    gsmall["ssd_conv_w"] = lax.dynamic_slice_in_dim(gsmall["ssd_conv_w"], me * ncw, ncw, axis=2)
    gsmall["hg_norm_g"] = lax.dynamic_slice_in_dim(gsmall["hg_norm_g"], me * nhg, nhg, axis=1)
    shapes = [w[n].shape for n in SMALL]
    d_p, m_p, v_p = adamw_plain(_pack([w[n] for n in SMALL]), _pack([gsmall[n] for n in SMALL]),
                                _pack([m[n] for n in SMALL]), _pack([v[n] for n in SMALL]), name="adamw_small")
    grads = dict(gsmall)
    deltas = dict(zip(SMALL, _unpack(d_p, shapes)))
    new_m = dict(zip(SMALL, _unpack(m_p, shapes)))
    new_v = dict(zip(SMALL, _unpack(v_p, shapes)))

    for n in BIG:
        gs, ds, ms, vs = adamw_sum(recv[n], shard2d[n], m[n][0], v[n][0], name=f"adamw_{n}")
        grads[n], deltas[n], new_m[n], new_v[n] = gs[None], ds[None], ms[None], vs[None]

    return (loss, dx0[None], *[grads[n] for n in WEIGHT_ORDER], *[deltas[n] for n in WEIGHT_ORDER],
            *[new_m[n] for n in WEIGHT_ORDER], *[new_v[n] for n in WEIGHT_ORDER])
```

```python
import functools
import math

import jax
import jax.numpy as jnp
import numpy as np
from jax import lax
from jax.experimental import pallas as pl
from jax.experimental.pallas import tpu as pltpu

F32 = jnp.float32
BF16 = jnp.bfloat16
MXU_DTYPE = jnp.bfloat16
GRAD_WIRE_DTYPE = jnp.bfloat16
HIGHEST = lax.Precision.HIGHEST
MESH_ID = pl.DeviceIdType.MESH
N_DEV = 8

D_MODEL = 1024
EPS = 1e-6
NEG_BIG = -1e30

SSD_DI = 2048
SSD_HEADDIM = 64
SSD_HEADS = 32
SSD_GROUPS = 4
SSD_HPG = 8
SSD_STATE = 128
SSD_CONV = 7
SSD_CHUNK = 128
SSD_CONV_CH = SSD_DI + 2 * SSD_GROUPS * SSD_STATE
SSD_IN = SSD_DI + SSD_CONV_CH + 2 * SSD_HEADS
SSD_IN_PAD = 5376

HG_CHUNK = 32
HG_HEADS = 8
HG_D = 128
HG_W = 1024

AT_HEADS = 16
AT_KV = 8
AT_HD = 128
AT_QW = 2048
AT_KW = 1024
GRID_W = 64
ROPE_THETA = 10000.0

DL_PAIRS = ((128, 1), (512, 4), (2048, 16))
DL_HEADS = 16
DL_HD = 64
DL_W = 1024
DL_HALF = 64
REL_BUCKETS = 32
REL_MAX_DIST = 1024

ADAM_LR = 0.001
ADAM_B1 = 0.9
ADAM_B2 = 0.999
ADAM_EPS = 1e-08
ADAM_WD = 0.01
ADAM_STEP = 10

VMEM_LIMIT = 56 * 1024 * 1024


def _cp(*sem):
    return pltpu.CompilerParams(dimension_semantics=tuple(sem), vmem_limit_bytes=VMEM_LIMIT)


def _tile(n, cands=(1024, 768, 512, 384, 256, 128)):
    for c in cands:
        if n % c == 0:
            return c
    return n


def _dot(a, b, dims):
    return lax.dot_general(a.astype(MXU_DTYPE), b.astype(MXU_DTYPE), (dims, ((), ())), preferred_element_type=F32)


def _dot_exact(a, b, dims):
    return lax.dot_general(a, b, (dims, ((), ())), precision=HIGHEST, preferred_element_type=F32)


def _silu(x):
    return x * jax.nn.sigmoid(x)


def _my_pos():
    return lax.axis_index("x"), lax.axis_index("y"), lax.axis_index("c")


def _flat(px, py, pc):
    return 4 * px + 2 * py + pc


def _peers():
    x_, y_, c_ = _my_pos()
    out = []
    for k in range(1, N_DEV):
        fx, fy, fc = (k >> 2) & 1, (k >> 1) & 1, k & 1
        out.append(((1 - x_) if fx else x_, (1 - y_) if fy else y_, (1 - c_) if fc else c_))
    return out


def _comm_copies(kind, in_refs, out_refs, send_sems, recv_sems, local_sems):
    me = _flat(*_my_pos())
    local, starts, waits = [], [], []
    for b, (i_ref, o_ref) in enumerate(zip(in_refs, out_refs)):
        local.append(pltpu.make_async_copy(i_ref if kind == "ag" else i_ref.at[me], o_ref.at[me], local_sems.at[b]))
        for k, p in enumerate(_peers()):
            src = i_ref if kind == "ag" else i_ref.at[_flat(*p)]
            starts.append(pltpu.make_async_remote_copy(
                src_ref=src, dst_ref=o_ref.at[me], send_sem=send_sems.at[b, k], recv_sem=recv_sems.at[b, k],
                device_id=p, device_id_type=MESH_ID))
            waits.append(pltpu.make_async_remote_copy(
                src_ref=src, dst_ref=o_ref.at[_flat(*p)], send_sem=send_sems.at[b, k], recv_sem=recv_sems.at[b, k],
                device_id=p, device_id_type=MESH_ID))
    return local, starts, waits


def pcall(body, comm, *, name, grid, in_specs, out_specs, out_shape, scratch_shapes=(), compiler_params=None):
    single = not isinstance(out_specs, (list, tuple))
    out_specs_l = [out_specs] if single else list(out_specs)
    out_shape_l = [out_shape] if single else list(out_shape)
    if comm is None:
        return pl.pallas_call(body, name=name, grid=grid, in_specs=in_specs, out_specs=out_specs, out_shape=out_shape,
                              scratch_shapes=list(scratch_shapes), compiler_params=compiler_params)
    kind, bufs = comm
    nb, n_in, n_out, n_scr = len(bufs), len(in_specs), len(out_specs_l), len(scratch_shapes)
    c_shape = [jax.ShapeDtypeStruct(((N_DEV,) + b.shape) if kind == "ag" else b.shape, b.dtype) for b in bufs]
    anyspec = pl.BlockSpec(memory_space=pl.ANY)

    def body2(*refs):
        ins, c_ins = refs[:n_in], refs[n_in:n_in + nb]
        outs = refs[n_in + nb:n_in + nb + n_out]
        c_outs = refs[n_in + nb + n_out:n_in + 2 * nb + n_out]
        scr = refs[n_in + 2 * nb + n_out:n_in + 2 * nb + n_out + n_scr]
        send_sems, recv_sems, local_sems = refs[n_in + 2 * nb + n_out + n_scr:]
        first = last = None
        for ax, g in enumerate(grid):
            pid = pl.program_id(ax)
            first = (pid == 0) if first is None else jnp.logical_and(first, pid == 0)
            last = (pid == g - 1) if last is None else jnp.logical_and(last, pid == g - 1)

        @pl.when(first)
        def _():
            local, starts, _ = _comm_copies(kind, c_ins, c_outs, send_sems, recv_sems, local_sems)
            for cp in local + starts:
                cp.start()

        body(*ins, *outs, *scr)

        @pl.when(last)
        def _():
            local, _, waits = _comm_copies(kind, c_ins, c_outs, send_sems, recv_sems, local_sems)
            for cp in waits + local:
                cp.wait()

    call = pl.pallas_call(
        body2, name=name, grid=grid, in_specs=list(in_specs) + [anyspec] * nb,
        out_specs=out_specs_l + [anyspec] * nb, out_shape=out_shape_l + c_shape,
        scratch_shapes=list(scratch_shapes) + [pltpu.SemaphoreType.DMA((nb, N_DEV - 1)),
                                               pltpu.SemaphoreType.DMA((nb, N_DEV - 1)), pltpu.SemaphoreType.DMA((nb,))],
        compiler_params=compiler_params)

    def run(*args):
        res = call(*args, *bufs)
        own = res[:n_out]
        return (own[0] if single else list(own)), list(res[n_out:])

    return run


def comm_only(comm, *, name):
    def body(x_ref, o_ref):
        o_ref[...] = x_ref[...]

    spec = pl.BlockSpec((8, 128), lambda i: (0, 0))
    _, res = pcall(body, comm, name=name, grid=(1,), in_specs=[spec], out_specs=spec,
                   out_shape=jax.ShapeDtypeStruct((8, 128), F32))(jnp.zeros((8, 128), F32))
    return res


def matmul(a, b, *, name, ta=False, tb=False, residual=None, out_dtype=F32, exact=False, tm=None, tn=None, tk=None,
           b_cols=None):
    M, K = (a.shape[1], a.shape[0]) if ta else a.shape
    n0, N = b_cols if b_cols is not None else (0, b.shape[0] if tb else b.shape[1])
    tm = tm or _tile(M, (512, 256, 128))
    tn = tn or _tile(N, (1024, 768, 512, 384, 256, 128))
    tk = tk or _tile(K, (1024, 768, 512, 384, 256, 128))
    nk = K // tk
    dims = (((0,) if ta else (1,)), ((1,) if tb else (0,)))

    def body(*refs):
        if residual is None:
            a_ref, b_ref, o_ref, acc = refs
            r_ref = None
        else:
            a_ref, b_ref, r_ref, o_ref, acc = refs
        k = pl.program_id(2)

        @pl.when(k == 0)
        def _():
            acc[...] = jnp.zeros_like(acc)

        if exact:
            acc[...] += _dot_exact(a_ref[...], b_ref[...], dims)
        else:
            acc[...] += _dot(a_ref[...], b_ref[...], dims)

        @pl.when(k == nk - 1)
        def _():
            r = acc[...]
            if r_ref is not None:
                r = r + r_ref[...]
            o_ref[...] = r.astype(o_ref.dtype)

    a_spec = pl.BlockSpec((tk, tm), lambda i, j, k: (k, i)) if ta else pl.BlockSpec((tm, tk), lambda i, j, k: (i, k))
    assert n0 % tn == 0
    jb = n0 // tn
    b_spec = (pl.BlockSpec((tn, tk), lambda i, j, k: (j + jb, k)) if tb
              else pl.BlockSpec((tk, tn), lambda i, j, k: (k, j + jb)))
    in_specs = [a_spec, b_spec]
    args = [a, b]
    if residual is not None:
        in_specs.append(pl.BlockSpec((tm, tn), lambda i, j, k: (i, j)))
        args.append(residual)
    return pl.pallas_call(
        body, name=name, grid=(M // tm, N // tn, nk), in_specs=in_specs,
        out_specs=pl.BlockSpec((tm, tn), lambda i, j, k: (i, j)),
        out_shape=jax.ShapeDtypeStruct((M, N), out_dtype),
        scratch_shapes=[pltpu.VMEM((tm, tn), F32)],
        compiler_params=_cp("parallel", "parallel", "arbitrary"),
    )(*args)


def _row_specs(rows, shared, R, W):
    specs = []
    for arr, col0, per_j, *wd in rows:
        w = wd[0] if wd else W
        specs.append(pl.BlockSpec((R, w), (lambda j, i, c=col0: (i, c + j)) if per_j else (lambda j, i, c=col0: (i, c))))
    for arr, per_j in shared:
        specs.append(pl.BlockSpec((arr.shape[0], W), (lambda j, i: (0, j)) if per_j else (lambda j, i: (0, 0))))
    return specs


def rowwise_fwd(name, fn, rows, shared, outs, *, W, ncb=1, R=256):
    S = rows[0][0].shape[0]
    R = min(R, S)
    nr, ns = len(rows), len(shared)

    def body(*refs):
        vals = [r[...] for r in refs[:nr + ns]]
        res = fn(*vals)
        for o_ref, r in zip(refs[nr + ns:], res):
            o_ref[...] = r.astype(o_ref.dtype)

    return pl.pallas_call(
        body, name=name, grid=(ncb, S // R),
        in_specs=_row_specs(rows, shared, R, W),
        out_specs=[pl.BlockSpec((R, w), lambda j, i: (i, j)) for w, _ in outs],
        out_shape=[jax.ShapeDtypeStruct((S, ncb * w), dt) for w, dt in outs],
        compiler_params=_cp("parallel", "parallel"),
    )(*[r[0] for r in rows], *[s[0] for s in shared])


def rowwise_bwd(name, fn, rows, shared, cots, *, W, ncb=1, R=256, diff_rows, diff_shared, add=None):
    S = rows[0][0].shape[0]
    R = min(R, S)
    nr, ns, nc = len(rows), len(shared), len(cots)
    nsteps = S // R

    def body(*refs):
        ins = refs[:nr + ns]
        ct_refs = refs[nr + ns:nr + ns + nc]
        pos = nr + ns + nc
        add_ref = None
        if add is not None:
            add_ref = refs[pos]
            pos += 1
        drow_refs = refs[pos:pos + len(diff_rows)]
        dsh_refs = refs[pos + len(diff_rows):]
        j, i = pl.program_id(0), pl.program_id(1)
        vals = [r[...] for r in ins]

        def f(*dv):
            full = list(vals)
            for idx, v in zip(list(diff_rows) + [nr + s for s in diff_shared], dv):
                full[idx] = v
            return tuple(fn(*full))

        prim = [vals[idx] for idx in diff_rows] + [vals[nr + s] for s in diff_shared]
        _, vjp = jax.vjp(f, *prim)
        grads = vjp(tuple(c[...] for c in ct_refs))
        for k, d_ref in enumerate(drow_refs):
            g = grads[k]
            if k == 0 and add_ref is not None:
                g = g + add_ref[...]
            d_ref[...] = g
        for k, (d_ref, s) in enumerate(zip(dsh_refs, diff_shared)):
            g = grads[len(diff_rows) + k]
            first = (i == 0) if shared[s][1] else jnp.logical_and(i == 0, j == 0)

            @pl.when(first)
            def _(d_ref=d_ref, g=g):
                d_ref[...] = g

            @pl.when(jnp.logical_not(first))
            def _(d_ref=d_ref, g=g):
                d_ref[...] += g

    in_specs = _row_specs(rows, shared, R, W)
    wo = [c.shape[1] // ncb for c in cots]
    in_specs += [pl.BlockSpec((R, w), lambda j, i: (i, j)) for w in wo]
    args = [r[0] for r in rows] + [s[0] for s in shared] + list(cots)
    if add is not None:
        in_specs.append(pl.BlockSpec((R, W), lambda j, i: (i, j)))
        args.append(add)
    dws = [(rows[r][3] if len(rows[r]) > 3 else W) for r in diff_rows]
    out_specs = [pl.BlockSpec((R, w), lambda j, i: (i, j)) for w in dws]
    out_shape = [jax.ShapeDtypeStruct((S, ncb * w), F32) for w in dws]
    for s in diff_shared:
        arr, per_j = shared[s]
        out_specs.append(pl.BlockSpec((arr.shape[0], W), (lambda j, i: (0, j)) if per_j else (lambda j, i: (0, 0))))
        out_shape.append(jax.ShapeDtypeStruct((arr.shape[0], ncb * W if per_j else W), F32))
    return pl.pallas_call(
        body, name=name, grid=(ncb, nsteps), in_specs=in_specs, out_specs=out_specs, out_shape=out_shape,
        compiler_params=_cp("arbitrary", "arbitrary"),
    )(*args)


def _row_specs2(rows, shared, R, W, ncb):
    specs = []
    for arr, col0, per_j, *wd in rows:
        w = wd[0] if wd else W
        if per_j:
            assert col0 % ncb == 0
            specs.append(pl.BlockSpec((R, ncb * w), lambda i, c=col0 // ncb: (i, c)))
        else:
            specs.append(pl.BlockSpec((R, w), lambda i, c=col0: (i, c)))
    for arr, per_j in shared:
        specs.append(pl.BlockSpec((arr.shape[0], ncb * W if per_j else arr.shape[1]), lambda i: (0, 0)))
    return specs


def _col_block(ref, per_j, j, w):
    return ref[:, j * w:(j + 1) * w] if per_j else ref[...]


def rowwise_fwd(name, fn, rows, shared, outs, *, W, ncb=1, R=256):
    S = rows[0][0].shape[0]
    R = min(R, S)
    nr, ns = len(rows), len(shared)
    widths = [(r[3] if len(r) > 3 else W) for r in rows]
    per_j = [r[2] for r in rows] + [s[1] for s in shared]
    ws = widths + [W] * ns

    def body(*refs):
        for j in range(ncb):
            vals = [_col_block(refs[k], per_j[k], j, ws[k]) for k in range(nr + ns)]
            res = fn(*vals)
            for o_ref, r, (wo, _) in zip(refs[nr + ns:], res, outs):
                o_ref[:, j * wo:(j + 1) * wo] = r.astype(o_ref.dtype)

    return pl.pallas_call(
        body, name=name, grid=(S // R,),
        in_specs=_row_specs2(rows, shared, R, W, ncb),
        out_specs=[pl.BlockSpec((R, ncb * w), lambda i: (i, 0)) for w, _ in outs],
        out_shape=[jax.ShapeDtypeStruct((S, ncb * w), dt) for w, dt in outs],
        compiler_params=_cp("parallel"),
    )(*[r[0] for r in rows], *[s[0] for s in shared])


def rowwise_bwd(name, fn, rows, shared, cots, *, W, ncb=1, R=256, diff_rows, diff_shared, add=None):
    S = rows[0][0].shape[0]
    R = min(R, S)
    nr, ns, nc = len(rows), len(shared), len(cots)
    widths = [(r[3] if len(r) > 3 else W) for r in rows]
    per_j = [r[2] for r in rows] + [s[1] for s in shared]
    ws = widths + [W] * ns
    wo = [c.shape[1] // ncb for c in cots]
    dws = [widths[r] for r in diff_rows]

    def body(*refs):
        ins = refs[:nr + ns]
        ct_refs = refs[nr + ns:nr + ns + nc]
        pos = nr + ns + nc
        add_ref = None
        if add is not None:
            add_ref = refs[pos]
            pos += 1
        drow_refs = refs[pos:pos + len(diff_rows)]
        dsh_refs = refs[pos + len(diff_rows):]
        i = pl.program_id(0)
        tot = [None] * len(diff_shared)
        for j in range(ncb):
            vals = [_col_block(ins[k], per_j[k], j, ws[k]) for k in range(nr + ns)]

            def f(*dv):
                full = list(vals)
                for idx, v in zip(list(diff_rows) + [nr + s for s in diff_shared], dv):
                    full[idx] = v
                return tuple(fn(*full))

            prim = [vals[idx] for idx in diff_rows] + [vals[nr + s] for s in diff_shared]
            _, vjp = jax.vjp(f, *prim)
            grads = vjp(tuple(c[:, j * w:(j + 1) * w] for c, w in zip(ct_refs, wo)))
            for k, (d_ref, w) in enumerate(zip(drow_refs, dws)):
                g = grads[k]
                if k == 0 and add_ref is not None:
                    g = g + add_ref[:, j * w:(j + 1) * w]
                d_ref[:, j * w:(j + 1) * w] = g
            for k, (d_ref, s) in enumerate(zip(dsh_refs, diff_shared)):
                g = grads[len(diff_rows) + k]
                if shared[s][1]:
                    @pl.when(i == 0)
                    def _(d_ref=d_ref, g=g, j=j):
                        d_ref[:, j * W:(j + 1) * W] = g

                    @pl.when(i != 0)
                    def _(d_ref=d_ref, g=g, j=j):
                        d_ref[:, j * W:(j + 1) * W] += g
                else:
                    tot[k] = g if tot[k] is None else tot[k] + g
        for k, (d_ref, s) in enumerate(zip(dsh_refs, diff_shared)):
            if not shared[s][1]:
                @pl.when(i == 0)
                def _(d_ref=d_ref, g=tot[k]):
                    d_ref[...] = g

                @pl.when(i != 0)
                def _(d_ref=d_ref, g=tot[k]):
                    d_ref[...] += g

    in_specs = _row_specs2(rows, shared, R, W, ncb)
    in_specs += [pl.BlockSpec((R, ncb * w), lambda i: (i, 0)) for w in wo]
    args = [r[0] for r in rows] + [s[0] for s in shared] + list(cots)
    if add is not None:
        in_specs.append(pl.BlockSpec((R, ncb * dws[0]), lambda i: (i, 0)))
        args.append(add)
    out_specs = [pl.BlockSpec((R, ncb * w), lambda i: (i, 0)) for w in dws]
    out_shape = [jax.ShapeDtypeStruct((S, ncb * w), F32) for w in dws]
    for s in diff_shared:
        arr, pj = shared[s]
        shp = (arr.shape[0], ncb * W if pj else arr.shape[1])
        out_specs.append(pl.BlockSpec(shp, lambda i: (0, 0)))
        out_shape.append(jax.ShapeDtypeStruct(shp, F32))
    return pl.pallas_call(
        body, name=name, grid=(S // R,), in_specs=in_specs, out_specs=out_specs, out_shape=out_shape,
        compiler_params=_cp("arbitrary"),
    )(*args)


def _rms(x, g):
    return x * lax.rsqrt(jnp.mean(x * x, axis=-1, keepdims=True) + EPS) * g


def _prenorm_fn(x, g):
    return (_rms(x, g),)


def loss_head(x, g, tgt, *, R=256):
    S, D = x.shape
    R = min(R, S)

    def fn(xv, gv, tv):
        err = _rms(xv, gv) - tv
        return 0.5 * jnp.sum(jnp.mean(err * err, axis=-1, keepdims=True), axis=0, keepdims=True)

    def body(x_ref, g_ref, t_ref, loss_ref, dx_ref, dg_ref):
        i = pl.program_id(0)
        tv = t_ref[...]
        val, vjp = jax.vjp(lambda a, b: fn(a, b, tv), x_ref[...], g_ref[...])
        dx, dg = vjp(jnp.ones((1, 1), F32))
        dx_ref[...] = dx

        @pl.when(i == 0)
        def _():
            loss_ref[...] = jnp.zeros_like(loss_ref) + val
            dg_ref[...] = dg

        @pl.when(i != 0)
        def _():
            loss_ref[...] += val
            dg_ref[...] += dg

    return pl.pallas_call(
        body, name="loss_head", grid=(S // R,),
        in_specs=[pl.BlockSpec((R, D), lambda i: (i, 0)), pl.BlockSpec((1, D), lambda i: (0, 0)),
                  pl.BlockSpec((R, D), lambda i: (i, 0))],
        out_specs=[pl.BlockSpec((1, 128), lambda i: (0, 0)), pl.BlockSpec((R, D), lambda i: (i, 0)),
                   pl.BlockSpec((1, D), lambda i: (0, 0))],
        out_shape=[jax.ShapeDtypeStruct((1, 128), F32), jax.ShapeDtypeStruct((S, D), F32),
                   jax.ShapeDtypeStruct((1, D), F32)],
        compiler_params=_cp("arbitrary"),
    )(x, g, tgt)


@jax.custom_vjp
def _softplus(x):
    z = jnp.exp(-jnp.abs(x))
    u = 1.0 + z
    log1p = jnp.where(u == 1.0, z, jnp.log(u) * (z / jnp.where(u == 1.0, 1.0, u - 1.0)))
    return jnp.maximum(x, 0.0) + log1p


def _softplus_fwd(x):
    return _softplus(x), x


def _softplus_bwd(x, ct):
    return (ct * jax.nn.sigmoid(x),)


_softplus.defvjp(_softplus_fwd, _softplus_bwd)


def _dt_fn(raw, bias):
    return (_softplus(raw + bias),)


CONV_CB = 256
CONV_RB = 512
CONV_PAD = 8


def ssd_conv_fwd(u, conv_w, conv_b):
    S = u.shape[0]
    ncb = SSD_CONV_CH // CONV_CB
    col0 = SSD_DI // CONV_CB
    RB = min(CONV_RB, S)

    def body(x_ref, w_ref, b_ref, o_ref, pad):
        pad[0:CONV_PAD, :] = jnp.zeros((CONV_PAD, CONV_CB), F32)
        pad[S + CONV_PAD:S + 2 * CONV_PAD, :] = jnp.zeros((CONV_PAD, CONV_CB), F32)
        pad[CONV_PAD:S + CONV_PAD, :] = x_ref[...]
        w = w_ref[...]
        b = b_ref[...]
        for r in range(S // RB):
            acc = jnp.zeros((RB, CONV_CB), F32) + b
            for k in range(SSD_CONV):
                off = r * RB + CONV_PAD + k - SSD_CONV // 2
                acc = acc + pad[off:off + RB, :] * w[k:k + 1, :]
            o_ref[r * RB:(r + 1) * RB, :] = _silu(acc)

    return pl.pallas_call(
        body, name="ssd_conv_fwd", grid=(ncb,),
        in_specs=[pl.BlockSpec((S, CONV_CB), lambda j: (0, col0 + j)),
                  pl.BlockSpec((SSD_CONV, CONV_CB), lambda j: (0, j)),
                  pl.BlockSpec((1, CONV_CB), lambda j: (0, j))],
        out_specs=pl.BlockSpec((S, CONV_CB), lambda j: (0, j)),
        out_shape=jax.ShapeDtypeStruct((S, SSD_CONV_CH), F32),
        scratch_shapes=[pltpu.VMEM((S + 2 * CONV_PAD, CONV_CB), F32)],
        compiler_params=_cp("parallel"),
    )(u, conv_w, conv_b)


def ssd_conv_bwd(u, conv_w, conv_b, dact):
    S = u.shape[0]
    ncb = SSD_CONV_CH // CONV_CB
    col0 = SSD_DI // CONV_CB
    RB = min(CONV_RB, S)
    half = SSD_CONV // 2

    def body(x_ref, w_ref, b_ref, da_ref, dx_ref, dw_ref, db_ref, xpad, dpad):
        z8 = jnp.zeros((CONV_PAD, CONV_CB), F32)
        xpad[0:CONV_PAD, :] = z8
        xpad[S + CONV_PAD:S + 2 * CONV_PAD, :] = z8
        dpad[0:CONV_PAD, :] = z8
        dpad[S + CONV_PAD:S + 2 * CONV_PAD, :] = z8
        xpad[CONV_PAD:S + CONV_PAD, :] = x_ref[...]
        w = w_ref[...]
        b = b_ref[...]
        dws = [jnp.zeros((1, CONV_CB), F32) for _ in range(SSD_CONV)]
        db = jnp.zeros((1, CONV_CB), F32)
        for r in range(S // RB):
            acc = jnp.zeros((RB, CONV_CB), F32) + b
            xs = []
            for k in range(SSD_CONV):
                off = r * RB + CONV_PAD + k - half
                xk = xpad[off:off + RB, :]
                xs.append(xk)
                acc = acc + xk * w[k:k + 1, :]
            sg = jax.nn.sigmoid(acc)
            dc = da_ref[r * RB:(r + 1) * RB, :] * (sg * (1.0 + acc * (1.0 - sg)))
            dpad[r * RB + CONV_PAD:(r + 1) * RB + CONV_PAD, :] = dc
            db = db + jnp.sum(dc, axis=0, keepdims=True)
            for k in range(SSD_CONV):
                dws[k] = dws[k] + jnp.sum(xs[k] * dc, axis=0, keepdims=True)
        for r in range(S // RB):
            acc = jnp.zeros((RB, CONV_CB), F32)
            for k in range(SSD_CONV):
                off = r * RB + CONV_PAD + half - k
                acc = acc + dpad[off:off + RB, :] * w[k:k + 1, :]
            dx_ref[r * RB:(r + 1) * RB, :] = acc
        for k in range(SSD_CONV):
            dw_ref[k:k + 1, :] = dws[k]
        dw_ref[SSD_CONV:SSD_CONV + 1, :] = jnp.zeros((1, CONV_CB), F32)
        db_ref[...] = db

    return pl.pallas_call(
        body, name="ssd_conv_bwd", grid=(ncb,),
        in_specs=[pl.BlockSpec((S, CONV_CB), lambda j: (0, col0 + j)),
                  pl.BlockSpec((SSD_CONV, CONV_CB), lambda j: (0, j)),
                  pl.BlockSpec((1, CONV_CB), lambda j: (0, j)),
                  pl.BlockSpec((S, CONV_CB), lambda j: (0, j))],
        out_specs=[pl.BlockSpec((S, CONV_CB), lambda j: (0, j)),
                   pl.BlockSpec((SSD_CONV + 1, CONV_CB), lambda j: (0, j)),
                   pl.BlockSpec((1, CONV_CB), lambda j: (0, j))],
        out_shape=[jax.ShapeDtypeStruct((S, SSD_CONV_CH), F32),
                   jax.ShapeDtypeStruct((SSD_CONV + 1, SSD_CONV_CH), F32),
                   jax.ShapeDtypeStruct((1, SSD_CONV_CH), F32)],
        scratch_shapes=[pltpu.VMEM((S + 2 * CONV_PAD, CONV_CB), F32), pltpu.VMEM((S + 2 * CONV_PAD, CONV_CB), F32)],
        compiler_params=_cp("parallel"),
    )(u, conv_w, conv_b, dact)


def _ssd_chunk(states, xs, Bg, Cg, dt, alog, *, reverse):
    Q = SSD_CHUNK
    r = lax.broadcasted_iota(jnp.int32, (Q, Q), 0)
    c = lax.broadcasted_iota(jnp.int32, (Q, Q), 1)
    keep = (c >= r) if reverse else (c <= r)
    tri = jnp.where(keep, 1.0, 0.0).astype(F32)
    a = dt * (-jnp.exp(alog))
    cum = _dot_exact(tri, a, ((1,), (0,)))
    cum_t = jnp.transpose(cum)
    last = 0 if reverse else Q - 1
    CB = _dot(Cg, Bg, ((1,), (1,)))
    new_states, ys = [], []
    for h in range(SSD_HPG):
        col = h + (SSD_HPG if reverse else 0)
        cum_c = cum[:, col:col + 1]
        cum_r = cum_t[col:col + 1, :]
        dt_c = dt[:, col:col + 1]
        cum_l = cum_c[last:last + 1, :]
        diff = jnp.where(keep, cum_c - cum_r, 0.0)
        L = jnp.where(keep, jnp.exp(diff), 0.0)
        xdt = xs[h] * dt_c
        y = _dot(CB * L, xdt, ((1,), (0,)))
        y = y + _dot(Cg * jnp.exp(cum_c), states[h], ((1,), (0,)))
        ns = jnp.exp(cum_l) * states[h] + _dot(Bg * jnp.exp(cum_l - cum_c), xdt, ((0,), (0,)))
        new_states.append(ns)
        ys.append(y)
    return new_states, ys


def _ssd_group_layout(t):
    r = t.shape[0]
    g = t[:, :2 * SSD_HEADS].reshape(r, 2, SSD_GROUPS, SSD_HPG).transpose(2, 0, 1, 3).reshape(SSD_GROUPS, r, 2 * SSD_HPG)
    return jnp.pad(g, ((0, 0), (0, 0), (0, 128 - 2 * SSD_HPG)))


def _ssd_head_layout(t):
    r = t.shape[1]
    h = t[:, :, :2 * SSD_HPG].reshape(SSD_GROUPS, r, 2, SSD_HPG).transpose(1, 2, 0, 3).reshape(r, 2 * SSD_HEADS)
    return jnp.pad(h, ((0, 0), (0, 128 - 2 * SSD_HEADS)))


def ssd_scan_fwd(act, dt, alog, *, reverse, y_prev=None, comm=None):
    S = act.shape[0]
    Q, N, P = SSD_CHUNK, SSD_STATE, SSD_HEADDIM
    nc = S // Q
    GW = SSD_HPG * P

    def cidx(i):
        return (nc - 1 - i) if reverse else i

    def body(*refs):
        if y_prev is None:
            x_ref, b_ref, c_ref, dt_ref, al_ref, y_ref, st_ref, state = refs
            yp_ref = None
        else:
            x_ref, b_ref, c_ref, dt_ref, al_ref, yp_ref, y_ref, st_ref, state = refs
        g, i = pl.program_id(0), pl.program_id(1)

        @pl.when(i == 0)
        def _():
            state[...] = jnp.zeros_like(state)

        states = [state[h] for h in range(SSD_HPG)]
        for h in range(SSD_HPG):
            st_ref[0, h] = states[h]
        xv = x_ref[...]
        xs = [xv[:, h * P:(h + 1) * P] for h in range(SSD_HPG)]
        ns, ys = _ssd_chunk(states, xs, b_ref[...], c_ref[...], dt_ref[0], al_ref[0], reverse=reverse)
        for h in range(SSD_HPG):
            state[h] = ns[h]
            yh = ys[h]
            if yp_ref is not None:
                yh = yh + yp_ref[:, h * P:(h + 1) * P]
            y_ref[:, h * P:(h + 1) * P] = yh

    in_specs = [pl.BlockSpec((Q, GW), lambda g, i: (cidx(i), g)),
                pl.BlockSpec((Q, N), lambda g, i: (cidx(i), SSD_DI // N + g)),
                pl.BlockSpec((Q, N), lambda g, i: (cidx(i), SSD_DI // N + SSD_GROUPS + g)),
                pl.BlockSpec((1, Q, 128), lambda g, i: (g, cidx(i), 0)),
                pl.BlockSpec((1, 1, 128), lambda g, i: (g, 0, 0))]
    args = [act, act, act, dt, alog]
    if y_prev is not None:
        in_specs.append(pl.BlockSpec((Q, GW), lambda g, i: (cidx(i), g)))
        args.append(y_prev)
    return pcall(
        body, comm, name=f"ssd_scan_fwd_{int(reverse)}", grid=(SSD_GROUPS, nc), in_specs=in_specs,
        out_specs=[pl.BlockSpec((Q, GW), lambda g, i: (cidx(i), g)),
                   pl.BlockSpec((1, SSD_HPG, N, P), lambda g, i: (cidx(i), g, 0, 0))],
        out_shape=[jax.ShapeDtypeStruct((S, SSD_DI), F32), jax.ShapeDtypeStruct((nc, SSD_HEADS, N, P), F32)],
        scratch_shapes=[pltpu.VMEM((SSD_HPG, N, P), F32)],
        compiler_params=_cp("arbitrary", "arbitrary"),
    )(*args)


def ssd_scan_bwd(act, dt, alog, states, dy, prev_x, *, reverse, prev=None, comm=None):
    S = act.shape[0]
    Q, N, P = SSD_CHUNK, SSD_STATE, SSD_HEADDIM
    nc = S // Q
    GW = SSD_HPG * P

    def cidx(i):
        return i if reverse else (nc - 1 - i)

    def body(*refs):
        x_ref, b_ref, c_ref, dt_ref, al_ref, st_ref, dy_ref, px_ref = refs[:8]
        pos = 8
        if prev is not None:
            pb_ref, pc_ref, pdt_ref, pal_ref = refs[pos:pos + 4]
            pos += 4
        dx_ref, db_ref, dc_ref, ddt_ref, dal_ref, dstate = refs[pos:]
        g, i = pl.program_id(0), pl.program_id(1)

        @pl.when(i == 0)
        def _():
            dstate[...] = jnp.zeros_like(dstate)

        xv = x_ref[...]
        dyv = dy_ref[...]
        xs = [xv[:, h * P:(h + 1) * P] for h in range(SSD_HPG)]
        dys = [dyv[:, h * P:(h + 1) * P] for h in range(SSD_HPG)]
        states = [st_ref[0, h] for h in range(SSD_HPG)]
        dstates = [dstate[h] for h in range(SSD_HPG)]

        def f(states, xs, Bg, Cg, dtv, al):
            return _ssd_chunk(states, xs, Bg, Cg, dtv, al, reverse=reverse)

        _, vjp = jax.vjp(f, states, xs, b_ref[...], c_ref[...], dt_ref[0], al_ref[0])
        dst, dxs, dB, dC, ddt, dal = vjp((dstates, dys))
        for h in range(SSD_HPG):
            dstate[h] = dst[h]
            dx_ref[:, h * P:(h + 1) * P] = dxs[h] + px_ref[:, h * P:(h + 1) * P]
        if prev is not None:
            dB = dB + pb_ref[...]
            dC = dC + pc_ref[...]
            ddt = ddt + pdt_ref[0]
        db_ref[...] = dB
        dc_ref[...] = dC
        ddt_ref[0] = ddt

        @pl.when(i == 0)
        def _():
            dal_ref[0] = dal + (pal_ref[0] if prev is not None else 0.0)

        @pl.when(i != 0)
        def _():
            dal_ref[0] += dal

    xspec = pl.BlockSpec((Q, GW), lambda g, i: (cidx(i), g))
    nspec_b = pl.BlockSpec((Q, N), lambda g, i: (cidx(i), SSD_DI // N + g))
    nspec_c = pl.BlockSpec((Q, N), lambda g, i: (cidx(i), SSD_DI // N + SSD_GROUPS + g))
    dtspec = pl.BlockSpec((1, Q, 128), lambda g, i: (g, cidx(i), 0))
    alspec = pl.BlockSpec((1, 1, 128), lambda g, i: (g, 0, 0))
    in_specs = [xspec, nspec_b, nspec_c,
                dtspec, alspec,
                pl.BlockSpec((1, SSD_HPG, N, P), lambda g, i: (cidx(i), g, 0, 0)),
                xspec]
    gspec = pl.BlockSpec((Q, N), lambda g, i: (cidx(i), g))
    in_specs.append(xspec)
    args = [act, act, act, dt, alog, states, dy, prev_x]
    if prev is not None:
        in_specs += [gspec, gspec, dtspec, alspec]
        args += list(prev)
    outs = pl.pallas_call(
        body, name=f"ssd_scan_bwd_{int(reverse)}", grid=(SSD_GROUPS, nc), in_specs=in_specs,
        out_specs=[pl.BlockSpec((Q, GW), lambda g, i: (cidx(i), g)),
                   pl.BlockSpec((Q, N), lambda g, i: (cidx(i), g)),
                   pl.BlockSpec((Q, N), lambda g, i: (cidx(i), g)),
                   dtspec, alspec],
        out_shape=[jax.ShapeDtypeStruct((S, SSD_DI), F32), jax.ShapeDtypeStruct((S, SSD_GROUPS * N), F32),
                   jax.ShapeDtypeStruct((S, SSD_GROUPS * N), F32),
                   jax.ShapeDtypeStruct((SSD_GROUPS, S, 128), F32), jax.ShapeDtypeStruct((SSD_GROUPS, 1, 128), F32)],
        scratch_shapes=[pltpu.VMEM((SSD_HPG, N, P), F32)],
        compiler_params=_cp("arbitrary", "arbitrary"),
    )(*args)
    return outs


def _ssd_chunk(state, x, Bg, Cg, dt, alog, *, reverse):
    Q, P = SSD_CHUNK, SSD_HEADDIM
    r = lax.broadcasted_iota(jnp.int32, (Q, Q), 0)
    c = lax.broadcasted_iota(jnp.int32, (Q, Q), 1)
    keep = (c >= r) if reverse else (c <= r)
    a = dt * (-jnp.exp(alog))
    cum = _cumsum_rows(a, reverse)
    cum_t = jnp.transpose(cum)
    last = 0 if reverse else Q - 1
    CB = _dot(Cg, Bg, ((1,), (1,)))
    yoff = _dot(Cg, state, ((1,), (0,)))
    ys, xdecs, keeps = [], [], []
    for h in range(SSD_HPG):
        col = h + (SSD_HPG if reverse else 0)
        hs = slice(h * P, (h + 1) * P)
        cum_c = cum[:, col:col + 1]
        cum_r = cum_t[col:col + 1, :]
        cum_l = cum_c[last:last + 1, :]
        L = jnp.where(keep, jnp.exp(jnp.where(keep, cum_c - cum_r, 0.0)), 0.0)
        xdt = x[:, hs] * dt[:, col:col + 1]
        ys.append(_dot(CB * L, xdt, ((1,), (0,))) + jnp.exp(cum_c) * yoff[:, hs])
        xdecs.append(xdt * jnp.exp(cum_l - cum_c))
        keeps.append(jnp.exp(cum_l) + jnp.zeros((1, P), F32))
    new_state = jnp.concatenate(keeps, axis=1) * state + _dot(Bg, jnp.concatenate(xdecs, axis=1), ((0,), (0,)))
    return new_state, jnp.concatenate(ys, axis=1)


def ssd_scan_fwd(act, dt, alog, *, reverse, y_prev=None, comm=None):
    S = act.shape[0]
    Q, N, P = SSD_CHUNK, SSD_STATE, SSD_HEADDIM
    nc = S // Q
    GW = SSD_HPG * P

    def cidx(i):
        return (nc - 1 - i) if reverse else i

    def body(*refs):
        if y_prev is None:
            x_ref, b_ref, c_ref, dt_ref, al_ref, y_ref, st_ref, state = refs
            yp_ref = None
        else:
            x_ref, b_ref, c_ref, dt_ref, al_ref, yp_ref, y_ref, st_ref, state = refs
        i = pl.program_id(1)

        @pl.when(i == 0)
        def _():
            state[...] = jnp.zeros_like(state)

        st = state[...]
        st_ref[0, 0] = st
        ns, y = _ssd_chunk(st, x_ref[...], b_ref[...], c_ref[...], dt_ref[0], al_ref[0], reverse=reverse)
        state[...] = ns
        y_ref[...] = y if yp_ref is None else y + yp_ref[...]

    xspec = pl.BlockSpec((Q, GW), lambda g, i: (cidx(i), g))
    in_specs = [xspec,
                pl.BlockSpec((Q, N), lambda g, i: (cidx(i), SSD_DI // N + g)),
                pl.BlockSpec((Q, N), lambda g, i: (cidx(i), SSD_DI // N + SSD_GROUPS + g)),
                pl.BlockSpec((1, Q, 128), lambda g, i: (g, cidx(i), 0)),
                pl.BlockSpec((1, 1, 128), lambda g, i: (g, 0, 0))]
    args = [act, act, act, dt, alog]
    if y_prev is not None:
        in_specs.append(xspec)
        args.append(y_prev)
    return pcall(
        body, comm, name=f"ssd_scan_fwd_{int(reverse)}", grid=(SSD_GROUPS, nc), in_specs=in_specs,
        out_specs=[xspec, pl.BlockSpec((1, 1, N, GW), lambda g, i: (cidx(i), g, 0, 0))],
        out_shape=[jax.ShapeDtypeStruct((S, SSD_DI), F32), jax.ShapeDtypeStruct((nc, SSD_GROUPS, N, GW), F32)],
        scratch_shapes=[pltpu.VMEM((N, GW), F32)],
        compiler_params=_cp("arbitrary", "arbitrary"),
    )(*args)


def ssd_scan_bwd(act, dt, alog, states, dy, prev_x, *, reverse, prev=None, comm=None):
    S = act.shape[0]
    Q, N, P = SSD_CHUNK, SSD_STATE, SSD_HEADDIM
    nc = S // Q
    GW = SSD_HPG * P

    def cidx(i):
        return i if reverse else (nc - 1 - i)

    def body(*refs):
        x_ref, b_ref, c_ref, dt_ref, al_ref, st_ref, dy_ref, px_ref = refs[:8]
        pos = 8
        if prev is not None:
            pb_ref, pc_ref, pdt_ref, pal_ref = refs[pos:pos + 4]
            pos += 4
        dx_ref, db_ref, dc_ref, ddt_ref, dal_ref, dstate = refs[pos:]
        i = pl.program_id(1)

        @pl.when(i == 0)
        def _():
            dstate[...] = jnp.zeros_like(dstate)

        _, vjp = jax.vjp(functools.partial(_ssd_chunk, reverse=reverse), st_ref[0, 0], x_ref[...], b_ref[...],
                         c_ref[...], dt_ref[0], al_ref[0])
        dst, dx, dB, dC, ddt, dal = vjp((dstate[...], dy_ref[...]))
        dstate[...] = dst
        dx_ref[...] = dx + px_ref[...]
        if prev is not None:
            dB = dB + pb_ref[...]
            dC = dC + pc_ref[...]
            ddt = ddt + pdt_ref[0]
        db_ref[...] = dB
        dc_ref[...] = dC
        ddt_ref[0] = ddt

        @pl.when(i == 0)
        def _():
            dal_ref[0] = dal + (pal_ref[0] if prev is not None else 0.0)

        @pl.when(i != 0)
        def _():
            dal_ref[0] += dal

    xspec = pl.BlockSpec((Q, GW), lambda g, i: (cidx(i), g))
    gspec = pl.BlockSpec((Q, N), lambda g, i: (cidx(i), g))
    dtspec = pl.BlockSpec((1, Q, 128), lambda g, i: (g, cidx(i), 0))
    alspec = pl.BlockSpec((1, 1, 128), lambda g, i: (g, 0, 0))
    in_specs = [xspec,
                pl.BlockSpec((Q, N), lambda g, i: (cidx(i), SSD_DI // N + g)),
                pl.BlockSpec((Q, N), lambda g, i: (cidx(i), SSD_DI // N + SSD_GROUPS + g)),
                dtspec, alspec, pl.BlockSpec((1, 1, N, GW), lambda g, i: (cidx(i), g, 0, 0)), xspec, xspec]
    args = [act, act, act, dt, alog, states, dy, prev_x]
    if prev is not None:
        in_specs += [gspec, gspec, dtspec, alspec]
        args += list(prev)
    return pcall(
        body, comm, name=f"ssd_scan_bwd_{int(reverse)}", grid=(SSD_GROUPS, nc), in_specs=in_specs,
        out_specs=[xspec, gspec, gspec, dtspec, alspec],
        out_shape=[jax.ShapeDtypeStruct((S, SSD_DI), F32), jax.ShapeDtypeStruct((S, SSD_GROUPS * N), F32),
                   jax.ShapeDtypeStruct((S, SSD_GROUPS * N), F32),
                   jax.ShapeDtypeStruct((SSD_GROUPS, S, 128), F32), jax.ShapeDtypeStruct((SSD_GROUPS, 1, 128), F32)],
        scratch_shapes=[pltpu.VMEM((N, GW), F32)],
        compiler_params=_cp("arbitrary", "arbitrary"),
    )(*args)


def _ssd_chunk(state, x, Bg, Cg, dt, alog, dtr, alr, *, reverse):
    Q, P = SSD_CHUNK, SSD_HEADDIM
    r = lax.broadcasted_iota(jnp.int32, (Q, Q), 0)
    c = lax.broadcasted_iota(jnp.int32, (Q, Q), 1)
    keep = (c >= r) if reverse else (c <= r)
    cum_t = jnp.transpose(_cumsum_rows(dt * (-jnp.exp(alog)), reverse))
    cum = _cumsum_rows(dtr * (-jnp.exp(alr)), reverse)
    last = 0 if reverse else Q - 1
    cum_l = cum[last:last + 1, :]
    CB = _dot(Cg, Bg, ((1,), (1,)))
    yoff = _dot(Cg, state, ((1,), (0,))) * jnp.exp(cum)
    xdt = x * dtr
    ys = []
    for h in range(SSD_HPG):
        col = h + (SSD_HPG if reverse else 0)
        hs = slice(h * P, (h + 1) * P)
        cum_q = jnp.concatenate([cum[:, hs]] * (Q // P), axis=1)
        L = jnp.where(keep, jnp.exp(jnp.where(keep, cum_q - cum_t[col:col + 1, :], 0.0)), 0.0)
        ys.append(_dot(CB * L, xdt[:, hs], ((1,), (0,))))
    new_state = jnp.exp(cum_l) * state + _dot(Bg, xdt * jnp.exp(cum_l - cum), ((0,), (0,)))
    return new_state, jnp.concatenate(ys, axis=1) + yoff


def ssd_scan_fwd(act, dt, alog, dtr, alr, *, reverse, y_prev=None, comm=None):
    S = act.shape[0]
    Q, N, P = SSD_CHUNK, SSD_STATE, SSD_HEADDIM
    nc = S // Q
    GW = SSD_HPG * P

    def cidx(i):
        return (nc - 1 - i) if reverse else i

    def body(*refs):
        if y_prev is None:
            x_ref, b_ref, c_ref, dt_ref, al_ref, dtr_ref, alr_ref, y_ref, st_ref, state = refs
            yp_ref = None
        else:
            x_ref, b_ref, c_ref, dt_ref, al_ref, dtr_ref, alr_ref, yp_ref, y_ref, st_ref, state = refs
        i = pl.program_id(1)

        @pl.when(i == 0)
        def _():
            state[...] = jnp.zeros_like(state)

        st = state[...]
        st_ref[0, 0] = st
        ns, y = _ssd_chunk(st, x_ref[...], b_ref[...], c_ref[...], dt_ref[0], al_ref[0], dtr_ref[...], alr_ref[...],
                           reverse=reverse)
        state[...] = ns
        y_ref[...] = y if yp_ref is None else y + yp_ref[...]

    xspec = pl.BlockSpec((Q, GW), lambda g, i: (cidx(i), g))
    in_specs = [xspec,
                pl.BlockSpec((Q, N), lambda g, i: (cidx(i), SSD_DI // N + g)),
                pl.BlockSpec((Q, N), lambda g, i: (cidx(i), SSD_DI // N + SSD_GROUPS + g)),
                pl.BlockSpec((1, Q, 128), lambda g, i: (g, cidx(i), 0)),
                pl.BlockSpec((1, 1, 128), lambda g, i: (g, 0, 0)),
                xspec, pl.BlockSpec((1, GW), lambda g, i: (0, g))]
    args = [act, act, act, dt, alog, dtr, alr]
    if y_prev is not None:
        in_specs.append(xspec)
        args.append(y_prev)
    return pcall(
        body, comm, name=f"ssd_scan_fwd_{int(reverse)}", grid=(SSD_GROUPS, nc), in_specs=in_specs,
        out_specs=[xspec, pl.BlockSpec((1, 1, N, GW), lambda g, i: (cidx(i), g, 0, 0))],
        out_shape=[jax.ShapeDtypeStruct((S, SSD_DI), F32), jax.ShapeDtypeStruct((nc, SSD_GROUPS, N, GW), F32)],
        scratch_shapes=[pltpu.VMEM((N, GW), F32)],
        compiler_params=_cp("arbitrary", "arbitrary"),
    )(*args)


def ssd_scan_bwd(act, dt, alog, dtr, alr, states, dy, prev_x, *, reverse, prev=None, comm=None):
    S = act.shape[0]
    Q, N, P = SSD_CHUNK, SSD_STATE, SSD_HEADDIM
    nc = S // Q
    GW = SSD_HPG * P

    def cidx(i):
        return i if reverse else (nc - 1 - i)

    def body(*refs):
        x_ref, b_ref, c_ref, dt_ref, al_ref, dtr_ref, alr_ref, st_ref, dy_ref, px_ref = refs[:10]
        pos = 10
        if prev is not None:
            pb_ref, pc_ref, pdt_ref, pal_ref = refs[pos:pos + 4]
            pos += 4
        dx_ref, db_ref, dc_ref, ddt_ref, dal_ref, ddtr_ref, dalr_ref, dstate = refs[pos:]
        i = pl.program_id(1)

        @pl.when(i == 0)
        def _():
            dstate[...] = jnp.zeros_like(dstate)

        _, vjp = jax.vjp(functools.partial(_ssd_chunk, reverse=reverse), st_ref[0, 0], x_ref[...], b_ref[...],
                         c_ref[...], dt_ref[0], al_ref[0], dtr_ref[...], alr_ref[...])
        dst, dx, dB, dC, ddt, dal, ddtr, dalr = vjp((dstate[...], dy_ref[...]))
        dstate[...] = dst
        dx_ref[...] = dx + px_ref[...]
        if prev is not None:
            dB = dB + pb_ref[...]
            dC = dC + pc_ref[...]
            ddt = ddt + pdt_ref[0]
        db_ref[...] = dB
        dc_ref[...] = dC
        ddt_ref[0] = ddt
        ddtr_ref[...] = ddtr

        @pl.when(i == 0)
        def _():
            dal_ref[0] = dal + (pal_ref[0] if prev is not None else 0.0)
            dalr_ref[...] = dalr

        @pl.when(i != 0)
        def _():
            dal_ref[0] += dal
            dalr_ref[...] += dalr

    xspec = pl.BlockSpec((Q, GW), lambda g, i: (cidx(i), g))
    gspec = pl.BlockSpec((Q, N), lambda g, i: (cidx(i), g))
    dtspec = pl.BlockSpec((1, Q, 128), lambda g, i: (g, cidx(i), 0))
    alspec = pl.BlockSpec((1, 1, 128), lambda g, i: (g, 0, 0))
    alrspec = pl.BlockSpec((1, GW), lambda g, i: (0, g))
    in_specs = [xspec,
                pl.BlockSpec((Q, N), lambda g, i: (cidx(i), SSD_DI // N + g)),
                pl.BlockSpec((Q, N), lambda g, i: (cidx(i), SSD_DI // N + SSD_GROUPS + g)),
                dtspec, alspec, xspec, alrspec,
                pl.BlockSpec((1, 1, N, GW), lambda g, i: (cidx(i), g, 0, 0)), xspec, xspec]
    args = [act, act, act, dt, alog, dtr, alr, states, dy, prev_x]
    if prev is not None:
        in_specs += [gspec, gspec, dtspec, alspec]
        args += list(prev)
    return pcall(
        body, comm, name=f"ssd_scan_bwd_{int(reverse)}", grid=(SSD_GROUPS, nc), in_specs=in_specs,
        out_specs=[xspec, gspec, gspec, dtspec, alspec, xspec, alrspec],
        out_shape=[jax.ShapeDtypeStruct((S, SSD_DI), F32), jax.ShapeDtypeStruct((S, SSD_GROUPS * N), F32),
                   jax.ShapeDtypeStruct((S, SSD_GROUPS * N), F32),
                   jax.ShapeDtypeStruct((SSD_GROUPS, S, 128), F32), jax.ShapeDtypeStruct((SSD_GROUPS, 1, 128), F32),
                   jax.ShapeDtypeStruct((S, SSD_DI), F32), jax.ShapeDtypeStruct((1, SSD_DI), F32)],
        scratch_shapes=[pltpu.VMEM((N, GW), F32)],
        compiler_params=_cp("arbitrary", "arbitrary"),
    )(*args)


def _ssd_post_fn(y, xs, z, dexp, ng):
    t = (y + xs * dexp) * _silu(z)
    return (_rms(t, ng),)


def _cumsum_rows_impl(x, reverse):
    n = x.shape[0]
    row = lax.broadcasted_iota(jnp.int32, x.shape, 0)
    k = 1
    while k < n:
        if reverse:
            x = x + jnp.where(row < n - k, pltpu.roll(x, n - k, 0), 0.0)
        else:
            x = x + jnp.where(row >= k, pltpu.roll(x, k, 0), 0.0)
        k *= 2
    return x


@functools.partial(jax.custom_vjp, nondiff_argnums=(1,))
def _cumsum_rows(x, reverse):
    return _cumsum_rows_impl(x, reverse)


_cumsum_rows.defvjp(lambda x, reverse: (_cumsum_rows_impl(x, reverse), None),
                    lambda reverse, _, ct: (_cumsum_rows_impl(ct, not reverse),))


def _hg_chunk(state, qraw, fraw, v, lb, *, reverse):
    C = HG_CHUNK
    r = lax.broadcasted_iota(jnp.int32, (C, C), 0)
    c = lax.broadcasted_iota(jnp.int32, (C, C), 1)
    keep = (c >= r) if reverse else (c <= r)
    q = _silu(qraw)
    f = lb + (1.0 - lb) * jax.nn.sigmoid(fraw)
    k = 1.0 - f
    g = jnp.log(f)
    G = _cumsum_rows(g, reverse)
    ref_row = C // 2 - 1 if reverse else C // 2
    last_row = 0 if reverse else C - 1
    Gr = G[ref_row:ref_row + 1, :]
    Gl = G[last_row:last_row + 1, :]
    q_t = q * jnp.exp(G - Gr)
    k_t = k * jnp.exp(Gr - G)
    att = jnp.where(keep, _dot(q_t, k_t, ((1,), (1,))), 0.0)
    o = _dot(att, v, ((1,), (0,))) + _dot(q * jnp.exp(G), state, ((1,), (0,)))
    kd = k * jnp.exp(Gl - G)
    new_state = jnp.transpose(jnp.exp(Gl)) * state + _dot(kd, v, ((0,), (0,)))
    return new_state, o


def hg_scan_fwd(u, lb, *, reverse, o_prev=None, rows=256, comm=None):
    S = u.shape[0]
    nh = HG_HEADS
    rows = min(rows, S)
    nsteps = S // rows
    ncb = rows // HG_CHUNK
    f_sec = 2 if reverse else 1

    def blk(i):
        return (nsteps - 1 - i) if reverse else i

    def body(*refs):
        if o_prev is None:
            q_ref, f_ref, v_ref, lb_ref, o_ref, st_ref, state = refs
            op_ref = None
        else:
            q_ref, f_ref, v_ref, lb_ref, op_ref, o_ref, st_ref, state = refs
        i = pl.program_id(0)

        @pl.when(i == 0)
        def _():
            state[...] = jnp.zeros_like(state)

        def chunk(cc, carry):
            ci = (ncb - 1 - cc) if reverse else cc
            sl = pl.ds(pl.multiple_of(ci * HG_CHUNK, HG_CHUNK), HG_CHUNK)
            for h in range(nh):
                hs = slice(h * HG_D, (h + 1) * HG_D)
                st = state[h]
                st_ref[ci, h] = st
                ns, o = _hg_chunk(st, q_ref[sl, hs], f_ref[sl, hs], v_ref[sl, hs], lb_ref[:, hs], reverse=reverse)
                state[h] = ns
                if op_ref is not None:
                    o = o + op_ref[sl, hs]
                o_ref[sl, hs] = o
            return carry

        lax.fori_loop(0, ncb, chunk, 0)

    rowspec = lambda sec: pl.BlockSpec((rows, HG_W), lambda i: (blk(i), sec))
    in_specs = [rowspec(0), rowspec(f_sec), rowspec(3), pl.BlockSpec((1, HG_W), lambda i: (0, 0))]
    args = [u, u, u, lb]
    if o_prev is not None:
        in_specs.append(rowspec(0))
        args.append(o_prev)
    return pcall(
        body, comm, name=f"hg_scan_fwd_{int(reverse)}", grid=(nsteps,), in_specs=in_specs,
        out_specs=[rowspec(0), pl.BlockSpec((ncb, nh, HG_D, HG_D), lambda i: (blk(i), 0, 0, 0))],
        out_shape=[jax.ShapeDtypeStruct((S, HG_W), F32), jax.ShapeDtypeStruct((S // HG_CHUNK, nh, HG_D, HG_D), F32)],
        scratch_shapes=[pltpu.VMEM((nh, HG_D, HG_D), F32)],
        compiler_params=_cp("arbitrary"),
    )(*args)


def hg_scan_bwd(u, lb, states, do, *, reverse, prev=None, rows=256, comm=None):
    S = u.shape[0]
    nh = HG_HEADS
    rows = min(rows, S)
    nsteps = S // rows
    ncb = rows // HG_CHUNK
    f_sec = 2 if reverse else 1

    def blk(i):
        return i if reverse else (nsteps - 1 - i)

    def body(*refs):
        q_ref, f_ref, v_ref, lb_ref, st_ref, do_ref = refs[:6]
        pos = 6
        if prev is not None:
            pq_ref, pv_ref, plb_ref = refs[pos:pos + 3]
            pos += 3
        dq_ref, df_ref, dv_ref, dlb_ref, dstate = refs[pos:]
        i = pl.program_id(0)

        @pl.when(i == 0)
        def _():
            dstate[...] = jnp.zeros_like(dstate)
            dlb_ref[...] = plb_ref[...] if prev is not None else jnp.zeros_like(dlb_ref)

        def chunk(cc, carry):
            ci = cc if reverse else (ncb - 1 - cc)
            sl = pl.ds(pl.multiple_of(ci * HG_CHUNK, HG_CHUNK), HG_CHUNK)
            for h in range(nh):
                hs = slice(h * HG_D, (h + 1) * HG_D)
                _, vjp = jax.vjp(functools.partial(_hg_chunk, reverse=reverse), st_ref[ci, h],
                                 q_ref[sl, hs], f_ref[sl, hs], v_ref[sl, hs], lb_ref[:, hs])
                dst, dq, df, dv, dlb = vjp((dstate[h], do_ref[sl, hs]))
                dstate[h] = dst
                if prev is not None:
                    dq = dq + pq_ref[sl, hs]
                    dv = dv + pv_ref[sl, hs]
                dq_ref[sl, hs] = dq
                df_ref[sl, hs] = df
                dv_ref[sl, hs] = dv
                dlb_ref[:, hs] += dlb
            return carry

        lax.fori_loop(0, ncb, chunk, 0)

    rowspec = lambda sec: pl.BlockSpec((rows, HG_W), lambda i: (blk(i), sec))
    lbspec = pl.BlockSpec((1, HG_W), lambda i: (0, 0))
    in_specs = [rowspec(0), rowspec(f_sec), rowspec(3), lbspec,
                pl.BlockSpec((ncb, nh, HG_D, HG_D), lambda i: (blk(i), 0, 0, 0)), rowspec(0)]
    args = [u, u, u, lb, states, do]
    if prev is not None:
        in_specs += [rowspec(0), rowspec(0), lbspec]
        args += list(prev)
    return pcall(
        body, comm, name=f"hg_scan_bwd_{int(reverse)}", grid=(nsteps,), in_specs=in_specs,
        out_specs=[rowspec(0), rowspec(0), rowspec(0), lbspec],
        out_shape=[jax.ShapeDtypeStruct((S, HG_W), F32)] * 3 + [jax.ShapeDtypeStruct((1, HG_W), F32)],
        scratch_shapes=[pltpu.VMEM((nh, HG_D, HG_D), F32)],
        compiler_params=_cp("arbitrary"),
    )(*args)


def _hg_lb_fn(lbp):
    m = jnp.max(lbp, axis=0, keepdims=True)
    e = jnp.exp(lbp - m)
    sm = e / jnp.sum(e, axis=0, keepdims=True)
    return ((sm[0:1] + sm[1:2]) - sm[0:1],)


def hg_lb_fwd(lbp):
    def body(x_ref, o_ref):
        o_ref[...] = _hg_lb_fn(x_ref[...])[0]

    return pl.pallas_call(body, name="hg_lb_fwd", out_shape=jax.ShapeDtypeStruct((1, HG_W), F32))(lbp)


def hg_lb_bwd(lbp, dlb):
    def body(x_ref, d_ref, o_ref):
        _, vjp = jax.vjp(_hg_lb_fn, x_ref[...])
        o_ref[...] = vjp((d_ref[...],))[0]

    return pl.pallas_call(body, name="hg_lb_bwd", out_shape=jax.ShapeDtypeStruct(lbp.shape, F32))(lbp, dlb)


def _hg_post_fn(o, gate, ng):
    return (_rms(o, ng) * _silu(gate),)


def _gate_fn(o, gate):
    return (o * _silu(gate),)


def _rope_tables(S):
    t = np.arange(S)
    row = (t // GRID_W).astype(np.float32)
    col = (t % GRID_W).astype(np.float32)
    half = AT_HD // 4
    inv = (ROPE_THETA ** (-np.arange(0, 2 * half, 2, dtype=np.float32) / np.float32(2 * half))).astype(np.float32)
    ar = row[:, None] * inv[None, :]
    ac = col[:, None] * inv[None, :]
    return ar.astype(np.float32), ac.astype(np.float32)


def _rope_swap_matrix():
    p = np.zeros((AT_HD, AT_HD), np.float32)
    for i in range(AT_HD):
        p[(i + 32) if (i % 64) < 32 else (i - 32), i] = 1.0
    return p


@jax.custom_vjp
def _half_swap(x):
    ax = x.ndim - 1
    lane = lax.broadcasted_iota(jnp.int32, x.shape, ax)
    return jnp.where((lane & 32) == 0, pltpu.roll(x, 96, ax), pltpu.roll(x, 32, ax))


_half_swap.defvjp(lambda x: (_half_swap(x), None), lambda _, ct: (_half_swap(ct),))


def _make_qk_fn(scale):
    def fn(x, ct, st, g):
        n = _rms(x, g)
        return ((n * ct + _half_swap(n) * st) * scale,)
    return fn


def flash_fwd(q, k, v, *, v_col0=0, tq=512, tk=512):
    S = q.shape[0]
    tq, tk = min(tq, S), min(tk, S)
    nk = S // tk
    G = AT_HEADS // AT_KV

    def body(q_ref, k_ref, v_ref, o_ref, lse_ref, m_s, l_s, acc):
        ki = pl.program_id(2)

        @pl.when(ki == 0)
        def _():
            m_s[...] = jnp.full_like(m_s, -jnp.inf)
            l_s[...] = jnp.zeros_like(l_s)
            acc[...] = jnp.zeros_like(acc)

        kv, vv = k_ref[...], v_ref[...]
        for g in range(G):
            s = _dot(q_ref[:, g * AT_HD:(g + 1) * AT_HD], kv, ((1,), (1,)))
            m_old = m_s[g]
            m_new = jnp.maximum(m_old, jnp.max(s, axis=1, keepdims=True))
            alpha = jnp.exp(m_old - m_new)
            p = jnp.exp(s - m_new)
            l_s[g] = alpha * l_s[g] + jnp.sum(p, axis=1, keepdims=True)
            acc[g] = alpha * acc[g] + _dot(p, vv, ((1,), (0,)))
            m_s[g] = m_new

        @pl.when(ki == nk - 1)
        def _():
            for g in range(G):
                o_ref[:, g * AT_HD:(g + 1) * AT_HD] = acc[g] / l_s[g]
                lse_ref[0, :, g:g + 1] = m_s[g] + jnp.log(l_s[g])

    return pl.pallas_call(
        body, name="flash_fwd", grid=(AT_KV, S // tq, nk),
        in_specs=[pl.BlockSpec((tq, G * AT_HD), lambda h, i, j: (i, h)),
                  pl.BlockSpec((tk, AT_HD), lambda h, i, j: (j, h)),
                  pl.BlockSpec((tk, AT_HD), lambda h, i, j: (j, v_col0 + h))],
        out_specs=[pl.BlockSpec((tq, G * AT_HD), lambda h, i, j: (i, h)),
                   pl.BlockSpec((1, tq, G), lambda h, i, j: (h, i, 0))],
        out_shape=[jax.ShapeDtypeStruct((S, AT_QW), F32), jax.ShapeDtypeStruct((AT_KV, S, G), F32)],
        scratch_shapes=[pltpu.VMEM((G, tq, 1), F32), pltpu.VMEM((G, tq, 1), F32), pltpu.VMEM((G, tq, AT_HD), F32)],
        compiler_params=_cp("parallel", "parallel", "arbitrary"),
    )(q, k, v)


def flash_bwd_dq(q, k, v, o, lse, do, *, v_col0=0, tq=512, tk=512):
    S = q.shape[0]
    tq, tk = min(tq, S), min(tk, S)
    nk = S // tk
    G = AT_HEADS // AT_KV

    def body(q_ref, k_ref, v_ref, o_ref, lse_ref, do_ref, dq_ref, dl_ref, acc, dl_s):
        ki = pl.program_id(2)

        @pl.when(ki == 0)
        def _():
            acc[...] = jnp.zeros_like(acc)
            for g in range(G):
                sl = slice(g * AT_HD, (g + 1) * AT_HD)
                dl_s[g] = jnp.sum(do_ref[:, sl] * o_ref[:, sl], axis=1, keepdims=True)

        kv, vv = k_ref[...], v_ref[...]
        for g in range(G):
            sl = slice(g * AT_HD, (g + 1) * AT_HD)
            s = _dot(q_ref[:, sl], kv, ((1,), (1,)))
            p = jnp.exp(s - lse_ref[0, :, g:g + 1])
            dp = _dot(do_ref[:, sl], vv, ((1,), (1,)))
            ds = p * (dp - dl_s[g])
            acc[g] += _dot(ds, kv, ((1,), (0,)))

        @pl.when(ki == nk - 1)
        def _():
            for g in range(G):
                dq_ref[:, g * AT_HD:(g + 1) * AT_HD] = acc[g]
                dl_ref[0, :, g:g + 1] = dl_s[g]

    qspec = pl.BlockSpec((tq, G * AT_HD), lambda h, i, j: (i, h))
    kspec = pl.BlockSpec((tk, AT_HD), lambda h, i, j: (j, h))
    lspec = pl.BlockSpec((1, tq, G), lambda h, i, j: (h, i, 0))
    return pl.pallas_call(
        body, name="flash_bwd_dq", grid=(AT_KV, S // tq, nk),
        in_specs=[qspec, kspec, pl.BlockSpec((tk, AT_HD), lambda h, i, j: (j, v_col0 + h)), qspec, lspec, qspec],
        out_specs=[qspec, lspec],
        out_shape=[jax.ShapeDtypeStruct((S, AT_QW), F32), jax.ShapeDtypeStruct((AT_KV, S, G), F32)],
        scratch_shapes=[pltpu.VMEM((G, tq, AT_HD), F32), pltpu.VMEM((G, tq, 1), F32)],
        compiler_params=_cp("parallel", "parallel", "arbitrary"),
    )(q, k, v, o, lse, do)


def flash_bwd_dkv(q, k, v, lse, delta, do, *, v_col0=0, tq=512, tk=512):
    S = q.shape[0]
    tq, tk = min(tq, S), min(tk, S)
    nq = S // tq
    G = AT_HEADS // AT_KV

    def body(q_ref, k_ref, v_ref, lse_ref, dl_ref, do_ref, dk_ref, dv_ref, dk_acc, dv_acc):
        qi = pl.program_id(2)

        @pl.when(qi == 0)
        def _():
            dk_acc[...] = jnp.zeros_like(dk_acc)
            dv_acc[...] = jnp.zeros_like(dv_acc)

        kv, vv = k_ref[...], v_ref[...]
        for g in range(G):
            sl = slice(g * AT_HD, (g + 1) * AT_HD)
            qg, dog = q_ref[:, sl], do_ref[:, sl]
            s = _dot(qg, kv, ((1,), (1,)))
            p = jnp.exp(s - lse_ref[0, :, g:g + 1])
            dv_acc[...] += _dot(p, dog, ((0,), (0,)))
            dp = _dot(dog, vv, ((1,), (1,)))
            ds = p * (dp - dl_ref[0, :, g:g + 1])
            dk_acc[...] += _dot(ds, qg, ((0,), (0,)))

        @pl.when(qi == nq - 1)
        def _():
            dk_ref[...] = dk_acc[...]
            dv_ref[...] = dv_acc[...]

    qspec = pl.BlockSpec((tq, G * AT_HD), lambda h, j, i: (i, h))
    kspec = pl.BlockSpec((tk, AT_HD), lambda h, j, i: (j, h))
    lspec = pl.BlockSpec((1, tq, G), lambda h, j, i: (h, i, 0))
    return pl.pallas_call(
        body, name="flash_bwd_dkv", grid=(AT_KV, S // tk, nq),
        in_specs=[qspec, kspec, pl.BlockSpec((tk, AT_HD), lambda h, j, i: (j, v_col0 + h)), lspec, lspec, qspec],
        out_specs=[kspec, kspec],
        out_shape=[jax.ShapeDtypeStruct((S, AT_KW), F32), jax.ShapeDtypeStruct((S, AT_KW), F32)],
        scratch_shapes=[pltpu.VMEM((tk, AT_HD), F32), pltpu.VMEM((tk, AT_HD), F32)],
        compiler_params=_cp("parallel", "parallel", "arbitrary"),
    )(q, k, v, lse, delta, do)


def flash_fwd(q, k, v, *, v_col0=0, tq=256, comm=None):
    S = q.shape[0]
    tq = min(tq, S)
    G = AT_HEADS // AT_KV

    def body(q_ref, k_ref, v_ref, o_ref, lse_ref):
        kv, vv = k_ref[...], v_ref[...]
        for g in range(G):
            sl = slice(g * AT_HD, (g + 1) * AT_HD)
            s = _dot(q_ref[:, sl], kv, ((1,), (1,)))
            m = jnp.max(s, axis=1, keepdims=True)
            p = jnp.exp(s - m)
            l = jnp.sum(p, axis=1, keepdims=True)
            o_ref[:, sl] = _dot(p, vv, ((1,), (0,))) / l
            lse_ref[0, :, g:g + 1] = m + jnp.log(l)

    return pcall(
        body, comm, name="flash_fwd", grid=(AT_KV, S // tq),
        in_specs=[pl.BlockSpec((tq, G * AT_HD), lambda h, i: (i, h)),
                  pl.BlockSpec((S, AT_HD), lambda h, i: (0, h)),
                  pl.BlockSpec((S, AT_HD), lambda h, i: (0, v_col0 + h))],
        out_specs=[pl.BlockSpec((tq, G * AT_HD), lambda h, i: (i, h)),
                   pl.BlockSpec((1, tq, G), lambda h, i: (h, i, 0))],
        out_shape=[jax.ShapeDtypeStruct((S, AT_QW), F32), jax.ShapeDtypeStruct((AT_KV, S, G), F32)],
        compiler_params=_cp("parallel", "arbitrary"),
    )(q, k, v)


def flash_bwd_dq(q, k, v, o, lse, do, *, v_col0=0, tq=256):
    S = q.shape[0]
    tq = min(tq, S)
    G = AT_HEADS // AT_KV

    def body(q_ref, k_ref, v_ref, o_ref, lse_ref, do_ref, dq_ref, dl_ref):
        kv, vv = k_ref[...], v_ref[...]
        for g in range(G):
            sl = slice(g * AT_HD, (g + 1) * AT_HD)
            dog = do_ref[:, sl]
            delta = jnp.sum(dog * o_ref[:, sl], axis=1, keepdims=True)
            s = _dot(q_ref[:, sl], kv, ((1,), (1,)))
            p = jnp.exp(s - lse_ref[0, :, g:g + 1])
            dp = _dot(dog, vv, ((1,), (1,)))
            ds = p * (dp - delta)
            dq_ref[:, sl] = _dot(ds, kv, ((1,), (0,)))
            dl_ref[0, :, g:g + 1] = delta

    qspec = pl.BlockSpec((tq, G * AT_HD), lambda h, i: (i, h))
    lspec = pl.BlockSpec((1, tq, G), lambda h, i: (h, i, 0))
    return pl.pallas_call(
        body, name="flash_bwd_dq", grid=(AT_KV, S // tq),
        in_specs=[qspec, pl.BlockSpec((S, AT_HD), lambda h, i: (0, h)),
                  pl.BlockSpec((S, AT_HD), lambda h, i: (0, v_col0 + h)), qspec, lspec, qspec],
        out_specs=[qspec, lspec],
        out_shape=[jax.ShapeDtypeStruct((S, AT_QW), F32), jax.ShapeDtypeStruct((AT_KV, S, G), F32)],
        compiler_params=_cp("parallel", "arbitrary"),
    )(q, k, v, o, lse, do)


def flash_bwd_dkv(q, k, v, lse, delta, do, *, v_col0=0, tk=512, comm=None):
    S = q.shape[0]
    tk = min(tk, S)
    G = AT_HEADS // AT_KV

    def body(q_ref, k_ref, v_ref, lse_ref, dl_ref, do_ref, dk_ref, dv_ref):
        kv, vv = k_ref[...], v_ref[...]
        dk = jnp.zeros((tk, AT_HD), F32)
        dv = jnp.zeros((tk, AT_HD), F32)
        for g in range(G):
            sl = slice(g * AT_HD, (g + 1) * AT_HD)
            qg, dog = q_ref[:, sl], do_ref[:, sl]
            s = _dot(qg, kv, ((1,), (1,)))
            p = jnp.exp(s - lse_ref[0, :, g:g + 1])
            dv = dv + _dot(p, dog, ((0,), (0,)))
            dp = _dot(dog, vv, ((1,), (1,)))
            ds = p * (dp - dl_ref[0, :, g:g + 1])
            dk = dk + _dot(ds, qg, ((0,), (0,)))
        dk_ref[...] = dk
        dv_ref[...] = dv

    qspec = pl.BlockSpec((S, G * AT_HD), lambda h, j: (0, h))
    kspec = pl.BlockSpec((tk, AT_HD), lambda h, j: (j, h))
    lspec = pl.BlockSpec((1, S, G), lambda h, j: (h, 0, 0))
    return pcall(
        body, comm, name="flash_bwd_dkv", grid=(AT_KV, S // tk),
        in_specs=[qspec, kspec, pl.BlockSpec((tk, AT_HD), lambda h, j: (j, v_col0 + h)), lspec, lspec, qspec],
        out_specs=[kspec, kspec],
        out_shape=[jax.ShapeDtypeStruct((S, AT_KW), F32), jax.ShapeDtypeStruct((S, AT_KW), F32)],
        compiler_params=_cp("parallel", "arbitrary"),
    )(q, k, v, lse, delta, do)


def _t5_bucket_np(rel):
    half = REL_BUCKETS // 2
    exact = half // 2
    n = np.abs(rel)
    large = exact + (np.log(np.maximum(n, 1).astype(np.float32) / np.float32(exact))
                     / np.float32(math.log(REL_MAX_DIST / exact)) * np.float32(half - exact)).astype(np.int32)
    large = np.minimum(large, half - 1)
    return np.where(rel > 0, half, 0) + np.where(n < exact, n, large)


def _dl_tiles(Ls):
    T = min(128, Ls)
    return T, T + 2 * DL_HALF


def _dl_bucket_tables(dil, T):
    W = T + 2 * DL_HALF
    i = np.arange(T)[:, None]
    j = np.arange(W)[None, :]
    bq = _t5_bucket_np((j - DL_HALF - i) * dil)
    iw = np.arange(W)[:, None]
    jk = np.arange(T)[None, :]
    bk = _t5_bucket_np((jk + DL_HALF - iw) * dil)
    return bq.astype(np.int32), bk.astype(np.int32)


def band_fwd(q, kp, vp, bias, *, scale):
    H, dil, Ls, E = q.shape
    T, W = _dl_tiles(Ls)

    def body(q_ref, k_ref, v_ref, b_ref, o_ref, lse_ref):
        n = pl.program_id(2)
        r0 = pl.multiple_of(n * T, T)
        kw = k_ref[0, 0, pl.ds(r0, W), :]
        vw = v_ref[0, 0, pl.ds(r0, W), :]
        i = lax.broadcasted_iota(jnp.int32, (T, W), 0)
        j = lax.broadcasted_iota(jnp.int32, (T, W), 1)
        kpos = n * T + j - DL_HALF
        mask = (jnp.abs(j - DL_HALF - i) <= DL_HALF) & (kpos >= 0) & (kpos < Ls)
        s = _dot(q_ref[0, 0], kw, ((1,), (1,))) * scale + b_ref[0]
        s = jnp.where(mask, s, NEG_BIG)
        m = jnp.max(s, axis=1, keepdims=True)
        lse = m + jnp.log(jnp.sum(jnp.exp(s - m), axis=1, keepdims=True))
        p = jnp.exp(s - lse)
        o_ref[0, 0] = _dot(p, vw, ((1,), (0,)))
        lse_ref[0, 0] = lse

    return pl.pallas_call(
        body, name=f"band_fwd_{dil}", grid=(H, dil, Ls // T),
        in_specs=[pl.BlockSpec((1, 1, T, E), lambda h, d, n: (h, d, n, 0)),
                  pl.BlockSpec((1, 1, Ls + 2 * DL_HALF, E), lambda h, d, n: (h, d, 0, 0)),
                  pl.BlockSpec((1, 1, Ls + 2 * DL_HALF, E), lambda h, d, n: (h, d, 0, 0)),
                  pl.BlockSpec((1, T, W), lambda h, d, n: (h, 0, 0))],
        out_specs=[pl.BlockSpec((1, 1, T, E), lambda h, d, n: (h, d, n, 0)),
                   pl.BlockSpec((1, 1, T, 1), lambda h, d, n: (h, d, n, 0))],
        out_shape=[jax.ShapeDtypeStruct((H, dil, Ls, E), F32), jax.ShapeDtypeStruct((H, dil, Ls, 1), F32)],
        compiler_params=_cp("parallel", "parallel", "arbitrary"),
    )(q, kp, vp, bias)


def band_bwd_dq(q, kp, vp, bias, lse, dm, do, *, scale):
    H, dil, Ls, E = q.shape
    T, W = _dl_tiles(Ls)

    def body(q_ref, k_ref, v_ref, b_ref, lse_ref, dm_ref, do_ref, dq_ref, db_ref):
        d, n = pl.program_id(1), pl.program_id(2)
        r0 = pl.multiple_of(n * T, T)
        kw = k_ref[0, 0, pl.ds(r0, W), :]
        vw = v_ref[0, 0, pl.ds(r0, W), :]
        i = lax.broadcasted_iota(jnp.int32, (T, W), 0)
        j = lax.broadcasted_iota(jnp.int32, (T, W), 1)
        kpos = n * T + j - DL_HALF
        mask = (jnp.abs(j - DL_HALF - i) <= DL_HALF) & (kpos >= 0) & (kpos < Ls)
        s = _dot(q_ref[0, 0], kw, ((1,), (1,))) * scale + b_ref[0]
        p = jnp.where(mask, jnp.exp(jnp.where(mask, s, 0.0) - lse_ref[0, 0]), 0.0)
        dp = _dot(do_ref[0, 0], vw, ((1,), (1,)))
        ds = p * (dp - dm_ref[0, 0])
        dq_ref[0, 0] = _dot(ds, kw, ((1,), (0,))) * scale
        first = jnp.logical_and(d == 0, n == 0)

        @pl.when(first)
        def _():
            db_ref[0] = ds

        @pl.when(jnp.logical_not(first))
        def _():
            db_ref[0] += ds

    qspec = pl.BlockSpec((1, 1, T, E), lambda h, d, n: (h, d, n, 0))
    kspec = pl.BlockSpec((1, 1, Ls + 2 * DL_HALF, E), lambda h, d, n: (h, d, 0, 0))
    rspec = pl.BlockSpec((1, 1, T, 1), lambda h, d, n: (h, d, n, 0))
    bspec = pl.BlockSpec((1, T, W), lambda h, d, n: (h, 0, 0))
    return pl.pallas_call(
        body, name=f"band_bwd_dq_{dil}", grid=(H, dil, Ls // T),
        in_specs=[qspec, kspec, kspec, bspec, rspec, rspec, qspec],
        out_specs=[qspec, bspec],
        out_shape=[jax.ShapeDtypeStruct((H, dil, Ls, E), F32), jax.ShapeDtypeStruct((H, T, W), F32)],
        compiler_params=_cp("arbitrary", "arbitrary", "arbitrary"),
    )(q, kp, vp, bias, lse, dm, do)


def band_bwd_dkv(qp, k, v, bias_t, lsep, dmp, dop, *, scale):
    H, dil, Ls, E = k.shape
    T, W = _dl_tiles(Ls)

    def body(q_ref, k_ref, v_ref, b_ref, lse_ref, dm_ref, do_ref, dk_ref, dv_ref):
        n = pl.program_id(2)
        r0 = pl.multiple_of(n * T, T)
        qw = q_ref[0, 0, pl.ds(r0, W), :]
        dow = do_ref[0, 0, pl.ds(r0, W), :]
        lsew = lse_ref[0, 0, pl.ds(r0, W), :]
        dmw = dm_ref[0, 0, pl.ds(r0, W), :]
        iw = lax.broadcasted_iota(jnp.int32, (W, T), 0)
        j = lax.broadcasted_iota(jnp.int32, (W, T), 1)
        qpos = n * T + iw - DL_HALF
        mask = (jnp.abs(j + DL_HALF - iw) <= DL_HALF) & (qpos >= 0) & (qpos < Ls)
        s = _dot(qw, k_ref[0, 0], ((1,), (1,))) * scale + b_ref[0]
        p = jnp.where(mask, jnp.exp(jnp.where(mask, s, 0.0) - lsew), 0.0)
        dv_ref[0, 0] = _dot(p, dow, ((0,), (0,)))
        dp = _dot(dow, v_ref[0, 0], ((1,), (1,)))
        ds = p * (dp - dmw)
        dk_ref[0, 0] = _dot(ds, qw, ((0,), (0,))) * scale

    kspec = pl.BlockSpec((1, 1, T, E), lambda h, d, n: (h, d, n, 0))
    wspec = pl.BlockSpec((1, 1, Ls + 2 * DL_HALF, E), lambda h, d, n: (h, d, 0, 0))
    w1spec = pl.BlockSpec((1, 1, Ls + 2 * DL_HALF, 1), lambda h, d, n: (h, d, 0, 0))
    return pl.pallas_call(
        body, name=f"band_bwd_dkv_{dil}", grid=(H, dil, Ls // T),
        in_specs=[wspec, kspec, kspec, pl.BlockSpec((1, W, T), lambda h, d, n: (h, 0, 0)), w1spec, w1spec, wspec],
        out_specs=[kspec, kspec],
        out_shape=[jax.ShapeDtypeStruct((H, dil, Ls, E), F32), jax.ShapeDtypeStruct((H, dil, Ls, E), F32)],
        compiler_params=_cp("parallel", "parallel", "arbitrary"),
    )(qp, k, v, bias_t, lsep, dmp, dop)


def _dl_merge_fn(o0, o1, o2, l0, l1, l2):
    m = jnp.maximum(jnp.maximum(l0, l1), l2)
    e0, e1, e2 = jnp.exp(l0 - m), jnp.exp(l1 - m), jnp.exp(l2 - m)
    den = e0 + e1 + e2
    return ((e0 / den) * o0 + (e1 / den) * o1 + (e2 / den) * o2,)


def _adamw_math(w, g, m, v):
    m = ADAM_B1 * m + (1.0 - ADAM_B1) * g
    v = ADAM_B2 * v + (1.0 - ADAM_B2) * (g * g)
    m_hat = m / (1.0 - ADAM_B1 ** ADAM_STEP)
    v_hat = v / (1.0 - ADAM_B2 ** ADAM_STEP)
    delta = -ADAM_LR * (m_hat / (jnp.sqrt(v_hat) + ADAM_EPS) + ADAM_WD * w)
    return delta, m, v


def adamw_sum(parts, w, m, v, *, name, R=128):
    rows, cols = w.shape
    R = min(R, rows)
    if rows % R:
        R = rows

    def body(p_ref, w_ref, m_ref, v_ref, g_ref, d_ref, nm_ref, nv_ref):
        g = p_ref[0].astype(F32)
        for s in range(1, N_DEV):
            g = g + p_ref[s].astype(F32)
        d, nm, nv = _adamw_math(w_ref[...], g, m_ref[...], v_ref[...])
        g_ref[...] = g
        d_ref[...] = d
        nm_ref[...] = nm
        nv_ref[...] = nv

    spec = pl.BlockSpec((R, cols), lambda i: (i, 0))
    return pl.pallas_call(
        body, name=name, grid=(rows // R,),
        in_specs=[pl.BlockSpec((N_DEV, R, cols), lambda i: (0, i, 0)), spec, spec, spec],
        out_specs=[spec] * 4, out_shape=[jax.ShapeDtypeStruct((rows, cols), F32)] * 4,
        compiler_params=_cp("parallel"),
    )(parts, w, m, v)


def sum_parts(parts, *, name):
    rows, cols = parts.shape[1:]

    def body(p_ref, o_ref):
        g = p_ref[0]
        for s in range(1, N_DEV):
            g = g + p_ref[s]
        o_ref[...] = g

    return pl.pallas_call(body, name=name, out_shape=jax.ShapeDtypeStruct((rows, cols), F32))(parts)


def adamw_plain(w, g, m, v, *, name):
    def body(w_ref, g_ref, m_ref, v_ref, d_ref, nm_ref, nv_ref):
        d, nm, nv = _adamw_math(w_ref[...], g_ref[...], m_ref[...], v_ref[...])
        d_ref[...] = d
        nm_ref[...] = nm
        nv_ref[...] = nv

    return pl.pallas_call(body, name=name, out_shape=[jax.ShapeDtypeStruct(w.shape, F32)] * 3)(w, g, m, v)


def _my_pos():
    return lax.axis_index("x"), lax.axis_index("y"), lax.axis_index("c")


def _flat(px, py, pc):
    return 4 * px + 2 * py + pc


def allgather_two_level(x, *, name):
    R, C = x.shape

    def body(x_ref, out_ref, send_sems, recv_sems, local_sem):
        x_, y_, c_ = _my_pos()
        me, sibling = (x_, y_, c_), (x_, y_, 1 - c_)
        chips = [(1 - x_, y_), (x_, 1 - y_), (1 - x_, 1 - y_)]

        def rows(p):
            return out_ref.at[_flat(*p)]

        def copy(k, block, to, src=None):
            return pltpu.make_async_remote_copy(
                src_ref=rows(block) if src is None else src, dst_ref=rows(block),
                send_sem=send_sems.at[k], recv_sem=recv_sems.at[k], device_id=to, device_id_type=MESH_ID)

        mine = pltpu.make_async_copy(x_ref, rows(me), local_sem)
        mine.start()
        first = [copy(0, me, sibling, src=x_ref)]
        first += [copy(1 + j, me, (*chip, c_), src=x_ref) for j, chip in enumerate(chips)]
        for cp in first:
            cp.start()
        passed = [copy(4 + j, (*chip, c_), sibling) for j, chip in enumerate(chips)]
        for j, chip in enumerate(chips):
            copy(1 + j, (*chip, c_), me).wait_recv()
            passed[j].start()
        copy(0, sibling, me).wait_recv()
        for j, chip in enumerate(chips):
            copy(4 + j, (*chip, 1 - c_), me).wait_recv()
        for cp in first + passed:
            cp.wait_send()
        mine.wait()

    return pl.pallas_call(
        body, name=name,
        out_shape=jax.ShapeDtypeStruct((N_DEV, R, C), x.dtype),
        in_specs=[pl.BlockSpec(memory_space=pl.ANY)],
        out_specs=pl.BlockSpec(memory_space=pl.ANY),
        scratch_shapes=[pltpu.SemaphoreType.DMA((7,)), pltpu.SemaphoreType.DMA((7,)), pltpu.SemaphoreType.DMA],
    )(x)


def allgather_two_level_multi(xs, *, name):
    nb = len(xs)

    def body(*refs):
        x_refs, out_refs = refs[:nb], refs[nb:2 * nb]
        send_sems, recv_sems, local_sems = refs[2 * nb:]
        x_, y_, c_ = _my_pos()
        me, sibling = (x_, y_, c_), (x_, y_, 1 - c_)
        chips = [(1 - x_, y_), (x_, 1 - y_), (1 - x_, 1 - y_)]

        def copy(b, k, block, to, own=False):
            rows = out_refs[b].at[_flat(*block)]
            return pltpu.make_async_remote_copy(
                src_ref=x_refs[b] if own else rows, dst_ref=rows,
                send_sem=send_sems.at[b, k], recv_sem=recv_sems.at[b, k], device_id=to, device_id_type=MESH_ID)

        mine = [pltpu.make_async_copy(x_refs[b], out_refs[b].at[_flat(*me)], local_sems.at[b]) for b in range(nb)]
        first = []
        for b in range(nb):
            first.append(copy(b, 0, me, sibling, own=True))
            first += [copy(b, 1 + j, me, (*chip, c_), own=True) for j, chip in enumerate(chips)]
        for cp in mine + first:
            cp.start()
        passed = []
        for j, chip in enumerate(chips):
            for b in range(nb):
                copy(b, 1 + j, (*chip, c_), me).wait_recv()
                fwd = copy(b, 4 + j, (*chip, c_), sibling)
                fwd.start()
                passed.append(fwd)
        for b in range(nb):
            copy(b, 0, sibling, me).wait_recv()
            for j, chip in enumerate(chips):
                copy(b, 4 + j, (*chip, 1 - c_), me).wait_recv()
        for cp in first + passed:
            cp.wait_send()
        for cp in mine:
            cp.wait()

    anyspec = pl.BlockSpec(memory_space=pl.ANY)
    return pl.pallas_call(
        body, name=name,
        out_shape=[jax.ShapeDtypeStruct((N_DEV,) + x.shape, x.dtype) for x in xs],
        in_specs=[anyspec] * nb, out_specs=[anyspec] * nb,
        scratch_shapes=[pltpu.SemaphoreType.DMA((nb, 7)), pltpu.SemaphoreType.DMA((nb, 7)), pltpu.SemaphoreType.DMA((nb,))],
    )(*xs)


def all_to_all(bufs, *, name):
    nb = len(bufs)

    def body(*refs):
        in_refs = refs[:nb]
        out_refs = refs[nb:2 * nb]
        send_sems, recv_sems, local_sems = refs[2 * nb:]
        x_, y_, c_ = _my_pos()
        me = _flat(x_, y_, c_)
        peers = []
        for k in range(1, N_DEV):
            fx, fy, fc = (k >> 2) & 1, (k >> 1) & 1, k & 1
            peers.append(((1 - x_) if fx else x_, (1 - y_) if fy else y_, (1 - c_) if fc else c_))
        copies = []
        for b in range(nb):
            loc = pltpu.make_async_copy(in_refs[b].at[me], out_refs[b].at[me], local_sems.at[b])
            loc.start()
            copies.append(loc)
        remote = []
        for b in range(nb):
            for k, p in enumerate(peers):
                cp = pltpu.make_async_remote_copy(
                    src_ref=in_refs[b].at[_flat(*p)], dst_ref=out_refs[b].at[me],
                    send_sem=send_sems.at[b, k], recv_sem=recv_sems.at[b, k], device_id=p, device_id_type=MESH_ID)
                cp.start()
                remote.append((b, k, p))
        for b, k, p in remote:
            pltpu.make_async_remote_copy(
                src_ref=in_refs[b].at[me], dst_ref=out_refs[b].at[_flat(*p)],
                send_sem=send_sems.at[b, k], recv_sem=recv_sems.at[b, k], device_id=p, device_id_type=MESH_ID).wait()
        for loc in copies:
            loc.wait()

    return pl.pallas_call(
        body, name=name,
        out_shape=[jax.ShapeDtypeStruct(b.shape, b.dtype) for b in bufs],
        in_specs=[pl.BlockSpec(memory_space=pl.ANY)] * nb,
        out_specs=[pl.BlockSpec(memory_space=pl.ANY)] * nb,
        scratch_shapes=[pltpu.SemaphoreType.DMA((nb, 7)), pltpu.SemaphoreType.DMA((nb, 7)), pltpu.SemaphoreType.DMA((nb,))],
    )(*bufs)


def _prenorm(tag, x, ng):
    return rowwise_fwd(f"{tag}_prenorm", _prenorm_fn, [(x, 0, False)], [(ng, False)], [(D_MODEL, MXU_DTYPE)], W=D_MODEL)[0]


def _cat_mxu(parts):
    return jnp.concatenate([t.astype(MXU_DTYPE) for t in parts], axis=1)


def _in_out_bwd(tag, x, ng, hn, du, w_in, dx):
    dhn = matmul(du, w_in, tb=True, name=f"{tag}_dhn")
    dw_in = matmul(hn, du, ta=True, out_dtype=GRAD_WIRE_DTYPE, name=f"{tag}_dw_in")
    dx_prev, dng = rowwise_bwd(f"{tag}_prenorm_bwd", _prenorm_fn, [(x, 0, False)], [(ng, False)], [dhn],
                               W=D_MODEL, diff_rows=[0], diff_shared=[0], add=dx)
    return dx_prev, dng, dw_in


def _own(res, comm):
    return (res, None) if comm is None else res


def ssd_layer_fwd(x, ng, p, comm=None):
    hn = _prenorm("ssd", x, ng)
    u = matmul(hn, p["w_in"], name="ssd_in")
    act = ssd_conv_fwd(u, p["conv_w"], p["conv_b"])
    dt = rowwise_fwd("ssd_dt", _dt_fn, [(u, (SSD_DI + SSD_CONV_CH) // 128, False)], [(p["dt_bias"], False)],
                     [(128, F32)], W=128)[0]
    H = SSD_HEADS
    dtr = [jnp.repeat(dt[:, d * H:(d + 1) * H], SSD_HEADDIM, axis=1) for d in (0, 1)]
    alr = [jnp.repeat(p["alog"][:, d * H:(d + 1) * H], SSD_HEADDIM, axis=1) for d in (0, 1)]
    dt, alog = _ssd_group_layout(dt), _ssd_group_layout(p["alog"])
    (y0, st0), cres = _own(ssd_scan_fwd(act, dt, alog, dtr[0], alr[0], reverse=False, comm=comm), comm)
    y, st1 = ssd_scan_fwd(act, dt, alog, dtr[1], alr[1], reverse=True, y_prev=y0)
    g = rowwise_fwd("ssd_post", _ssd_post_fn, [(y, 0, True), (act, 0, True), (u, 0, True)],
                    [(p["dexp"], True), (p["norm_g"], True)], [(512, MXU_DTYPE)], W=512, ncb=SSD_GROUPS)[0]
    xn = matmul(g, p["w_out"], residual=x, name="ssd_out")
    return xn, dict(x=x, ng=ng, hn=hn, u=u, act=act, dt=dt, alog=alog, dtr=dtr, alr=alr, y=y, st0=st0, st1=st1,
                    g=g), cres


def ssd_layer_bwd(sv, p, dx, comm=None):
    u, act, dt = sv["u"], sv["act"], sv["dt"]
    S = u.shape[0]
    dg = matmul(dx, p["w_out"], tb=True, name="ssd_dg")
    dw_out = matmul(sv["g"], dx, ta=True, out_dtype=GRAD_WIRE_DTYPE, name="ssd_dw_out")
    dy, dxs_skip, dz, ddexp, dnorm = rowwise_bwd(
        "ssd_post_bwd", _ssd_post_fn, [(sv["y"], 0, True), (act, 0, True), (u, 0, True)],
        [(p["dexp"], True), (p["norm_g"], True)], [dg], W=512, ncb=SSD_GROUPS, diff_rows=[0, 1, 2], diff_shared=[0, 1])
    dtr, alr = sv["dtr"], sv["alr"]
    (dxa, dB, dC, ddt, dal, ddtr0, dalr0), cres = _own(
        ssd_scan_bwd(act, dt, sv["alog"], dtr[0], alr[0], sv["st0"], dy, dxs_skip, reverse=False, comm=comm), comm)
    dxa, dB, dC, ddt, dal, ddtr1, dalr1 = ssd_scan_bwd(act, dt, sv["alog"], dtr[1], alr[1], sv["st1"], dy, dxa,
                                                       reverse=True, prev=(dB, dC, ddt, dal))
    dact = jnp.concatenate([dxa, dB, dC], axis=1)
    dxbc, dconv_w, dconv_b = ssd_conv_bwd(u, p["conv_w"], p["conv_b"], dact)
    fold = jnp.asarray(np.repeat(np.eye(SSD_HEADS, dtype=np.float32), SSD_HEADDIM, axis=0))
    folded = [matmul(t, fold, exact=True, name=f"ssd_ddt_fold_{d}", tn=SSD_HEADS) for d, t in enumerate((ddtr0, ddtr1))]
    ddt_all = _ssd_head_layout(ddt) + jnp.pad(jnp.concatenate(folded, axis=1), ((0, 0), (0, 128 - 2 * SSD_HEADS)))
    dal_rep = jnp.concatenate([t.reshape(SSD_HEADS, SSD_HEADDIM).sum(axis=1) for t in (dalr0, dalr1)])[None, :]
    ddt_raw, ddt_bias = rowwise_bwd("ssd_dt_bwd", _dt_fn, [(u, (SSD_DI + SSD_CONV_CH) // 128, False)],
                                    [(p["dt_bias"], False)], [ddt_all], W=128, diff_rows=[0], diff_shared=[0])
    du = _cat_mxu([dz, dxbc, ddt_raw, jnp.zeros((S, SSD_IN_PAD - SSD_IN - 64), F32)])
    dx_prev, dng, dw_in = _in_out_bwd("ssd", sv["x"], sv["ng"], sv["hn"], du, p["w_in"], dx)
    grads = dict(
        w_in=dw_in[:, :SSD_IN], w_out=dw_out, conv_w=dconv_w[:SSD_CONV], conv_b=dconv_b,
        dt_bias=ddt_bias[:, :2 * SSD_HEADS], a_log=_ssd_head_layout(dal)[:, :2 * SSD_HEADS] + dal_rep,
        d=ddexp.reshape(SSD_HEADS, SSD_HEADDIM).sum(axis=1)[None, :], norm_g=dnorm, ng=dng)
    return dx_prev, grads, cres


def hg_layer_fwd(x, ng, p, comm0=None, comm1=None):
    hn = _prenorm("hg", x, ng)
    u = matmul(hn, p["w_in"], name="hg_in")
    lb = hg_lb_fwd(p["hgrn_lb"])
    (o0, st0), cres0 = _own(hg_scan_fwd(u, lb, reverse=False, comm=comm0), comm0)
    (o, st1), cres1 = _own(hg_scan_fwd(u, lb, reverse=True, o_prev=o0, comm=comm1), comm1)
    g = rowwise_fwd("hg_post", _hg_post_fn, [(o, 0, True), (u, 4 * HG_HEADS, True)], [(p["norm_g"], True)],
                    [(HG_D, MXU_DTYPE)], W=HG_D, ncb=HG_HEADS)[0]
    xn = matmul(g, p["w_out"], residual=x, name="hg_out")
    return xn, dict(x=x, ng=ng, hn=hn, u=u, lb=lb, o=o, st0=st0, st1=st1, g=g), cres0, cres1


def hg_layer_bwd(sv, p, dx, comm=None):
    u, lb = sv["u"], sv["lb"]
    dg = matmul(dx, p["w_out"], tb=True, name="hg_dg")
    dw_out = matmul(sv["g"], dx, ta=True, out_dtype=GRAD_WIRE_DTYPE, name="hg_dw_out")
    do, dgate, dnorm = rowwise_bwd("hg_post_bwd", _hg_post_fn, [(sv["o"], 0, True), (u, 4 * HG_HEADS, True)],
                                   [(p["norm_g"], True)], [dg], W=HG_D, ncb=HG_HEADS, diff_rows=[0, 1], diff_shared=[0])
    (dq0, df0, dv0, dlb0), cres = _own(hg_scan_bwd(u, lb, sv["st0"], do, reverse=False, comm=comm), comm)
    dq, df1, dv, dlb = hg_scan_bwd(u, lb, sv["st1"], do, reverse=True, prev=(dq0, dv0, dlb0))
    du = _cat_mxu([dq, df0, df1, dv, dgate])
    dhgrn_lb = hg_lb_bwd(p["hgrn_lb"], dlb)
    dx_prev, dng, dw_in = _in_out_bwd("hg", sv["x"], sv["ng"], sv["hn"], du, p["w_in"], dx)
    return dx_prev, dict(w_in=dw_in, w_out=dw_out, norm_g=dnorm, hgrn_lb=dhgrn_lb, ng=dng), cres


def _rope_consts(S):
    ar, ac = _rope_tables(S)
    ct = np.concatenate([np.cos(ar), np.cos(ar), np.cos(ac), np.cos(ac)], axis=1).astype(np.float32)
    st = np.concatenate([-np.sin(ar), np.sin(ar), -np.sin(ac), np.sin(ac)], axis=1).astype(np.float32)
    return jnp.asarray(ct), jnp.asarray(st)


def _at_qk(tag, u, col0, nheads, scale, gain, consts, cot=None):
    ct, st = consts
    rows = [(u, col0, True), (ct, 0, False), (st, 0, False)]
    shared = [(gain, False)]
    if cot is None:
        return rowwise_fwd(f"at_{tag}", _make_qk_fn(scale), rows, shared, [(AT_HD, MXU_DTYPE)], W=AT_HD, ncb=nheads)[0]
    return rowwise_bwd(f"at_{tag}_bwd", _make_qk_fn(scale), rows, shared, [cot], W=AT_HD, ncb=nheads,
                       diff_rows=[0], diff_shared=[0])


def at_layer_fwd(x, ng, p, comm=None):
    S = x.shape[0]
    hn = _prenorm("at", x, ng)
    u = matmul(hn, p["w_in"], name="at_in")
    consts = _rope_consts(S)
    qr = _at_qk("q", u, 0, AT_HEADS, AT_HD ** -0.5, p["q_g"], consts)
    kr = _at_qk("k", u, AT_HEADS, AT_KV, 1.0, p["k_g"], consts)
    vc0 = (AT_QW + AT_KW) // AT_HD
    (o, lse), cres = _own(flash_fwd(qr, kr, u, v_col0=vc0, comm=comm), comm)
    g = rowwise_fwd("at_gate", _gate_fn, [(o, 0, True), (u, (AT_QW + 2 * AT_KW) // 1024, True)], [],
                    [(1024, MXU_DTYPE)], W=1024, ncb=AT_QW // 1024)[0]
    xn = matmul(g, p["w_out"], residual=x, name="at_out")
    return xn, dict(x=x, ng=ng, hn=hn, u=u, qr=qr, kr=kr, o=o, lse=lse, g=g), cres


def at_layer_bwd(sv, p, dx, comm=None):
    u, qr, kr = sv["u"], sv["qr"], sv["kr"]
    S = u.shape[0]
    consts = _rope_consts(S)
    vc0 = (AT_QW + AT_KW) // AT_HD
    dg = matmul(dx, p["w_out"], tb=True, name="at_dg")
    dw_out = matmul(sv["g"], dx, ta=True, out_dtype=GRAD_WIRE_DTYPE, name="at_dw_out")
    do, dgate = rowwise_bwd("at_gate_bwd", _gate_fn, [(sv["o"], 0, True), (u, (AT_QW + 2 * AT_KW) // 1024, True)], [],
                            [dg], W=1024, ncb=AT_QW // 1024, diff_rows=[0, 1], diff_shared=[])
    dqs, delta = flash_bwd_dq(qr, kr, u, sv["o"], sv["lse"], do, v_col0=vc0)
    (dkr, dv), cres = _own(flash_bwd_dkv(qr, kr, u, sv["lse"], delta, do, v_col0=vc0, comm=comm), comm)
    dq_raw, dqg = _at_qk("q", u, 0, AT_HEADS, AT_HD ** -0.5, p["q_g"], consts, cot=dqs)
    dk_raw, dkg = _at_qk("k", u, AT_HEADS, AT_KV, 1.0, p["k_g"], consts, cot=dkr)
    du = _cat_mxu([dq_raw, dk_raw, dv, dgate])
    dx_prev, dng, dw_in = _in_out_bwd("at", sv["x"], sv["ng"], sv["hn"], du, p["w_in"], dx)
    return dx_prev, dict(w_in=dw_in, w_out=dw_out, q_g=dqg, k_g=dkg, ng=dng), cres


def _to_stream(t, dil):
    S = t.shape[0]
    return t.reshape(S // dil, dil, DL_HEADS, DL_HD).transpose(2, 1, 0, 3)


def _from_stream(t):
    H, dil, Ls, E = t.shape
    return t.transpose(2, 1, 0, 3).reshape(Ls * dil, H * E)


def _stream_to_hm(t):
    H, dil, Ls, w = t.shape
    return t.transpose(0, 2, 1, 3).reshape(H * Ls * dil, w)


def _hm_to_stream(t, dil):
    w = t.shape[1]
    S = t.shape[0] // DL_HEADS
    return t.reshape(DL_HEADS, S // dil, dil, w).transpose(0, 2, 1, 3)


def _pad_l(t):
    return jnp.pad(t, ((0, 0), (0, 0), (DL_HALF, DL_HALF), (0, 0)))


OX_LSE = DL_HD
DOX_LSE, DOX_DM = DL_HD, DL_HD + 32


def _win(p_ref, c_ref, n_ref, h, T):
    return jnp.concatenate([p_ref[h, 0, T - DL_HALF:T, :], c_ref[h, 0], n_ref[h, 0, 0:DL_HALF, :]], axis=0)


def _win_specs(T, E, nb):
    return [pl.BlockSpec((DL_HEADS, 1, T, E), lambda d, n: (0, d, jnp.maximum(n - 1, 0), 0)),
            pl.BlockSpec((DL_HEADS, 1, T, E), lambda d, n: (0, d, n, 0)),
            pl.BlockSpec((DL_HEADS, 1, T, E), lambda d, n: (0, d, jnp.minimum(n + 1, nb - 1), 0))]


def _band_mask_q(n, T, W, Ls):
    i = lax.broadcasted_iota(jnp.int32, (T, W), 0)
    j = lax.broadcasted_iota(jnp.int32, (T, W), 1)
    kpos = n * T + j - DL_HALF
    return (jnp.abs(j - DL_HALF - i) <= DL_HALF) & (kpos >= 0) & (kpos < Ls)


def band_fwd(q, k, v, bias, *, scale):
    H, dil, Ls, E = q.shape
    T, W = _dl_tiles(Ls)
    nb = Ls // T

    def body(q_ref, kp_ref, kc_ref, kn_ref, vp_ref, vc_ref, vn_ref, b_ref, ox_ref):
        n = pl.program_id(1)
        mask = _band_mask_q(n, T, W, Ls)
        for h in range(H):
            kw = _win(kp_ref, kc_ref, kn_ref, h, T)
            vw = _win(vp_ref, vc_ref, vn_ref, h, T)
            s = _dot(q_ref[h, 0], kw, ((1,), (1,))) * scale + b_ref[h]
            s = jnp.where(mask, s, NEG_BIG)
            m = jnp.max(s, axis=1, keepdims=True)
            lse = m + jnp.log(jnp.sum(jnp.exp(s - m), axis=1, keepdims=True))
            p = jnp.exp(s - lse)
            ox_ref[h, 0, :, 0:E] = _dot(p, vw, ((1,), (0,)))
            ox_ref[h, 0, :, E:2 * E] = lse + jnp.zeros((T, E), F32)

    cur = pl.BlockSpec((H, 1, T, E), lambda d, n: (0, d, n, 0))
    return pl.pallas_call(
        body, name=f"band_fwd_{dil}", grid=(dil, nb),
        in_specs=[cur] + _win_specs(T, E, nb) + _win_specs(T, E, nb) + [pl.BlockSpec((H, T, W), lambda d, n: (0, 0, 0))],
        out_specs=pl.BlockSpec((H, 1, T, 2 * E), lambda d, n: (0, d, n, 0)),
        out_shape=jax.ShapeDtypeStruct((H, dil, Ls, 2 * E), F32),
        compiler_params=_cp("parallel", "parallel"),
    )(q, k, k, k, v, v, v, bias)


def band_bwd_dq(q, k, v, bias, dox, *, scale):
    H, dil, Ls, E = q.shape
    T, W = _dl_tiles(Ls)
    nb = Ls // T

    def body(q_ref, kp_ref, kc_ref, kn_ref, vp_ref, vc_ref, vn_ref, b_ref, dox_ref, dq_ref, db_ref):
        d, n = pl.program_id(0), pl.program_id(1)
        mask = _band_mask_q(n, T, W, Ls)
        first = jnp.logical_and(d == 0, n == 0)

        @pl.when(first)
        def _():
            db_ref[...] = jnp.zeros_like(db_ref)

        for h in range(H):
            kw = _win(kp_ref, kc_ref, kn_ref, h, T)
            vw = _win(vp_ref, vc_ref, vn_ref, h, T)
            dox = dox_ref[h, 0]
            do, lse, dm = dox[:, 0:E], dox[:, DOX_LSE:DOX_LSE + 1], dox[:, DOX_DM:DOX_DM + 1]
            s = _dot(q_ref[h, 0], kw, ((1,), (1,))) * scale + b_ref[h]
            p = jnp.where(mask, jnp.exp(jnp.where(mask, s, 0.0) - lse), 0.0)
            dp = _dot(do, vw, ((1,), (1,)))
            ds = p * (dp - dm)
            dq_ref[h, 0] = (_dot(ds, kw, ((1,), (0,))) * scale).astype(dq_ref.dtype)
            db_ref[h] += ds

    cur = pl.BlockSpec((H, 1, T, E), lambda d, n: (0, d, n, 0))
    bspec = pl.BlockSpec((H, T, W), lambda d, n: (0, 0, 0))
    return pl.pallas_call(
        body, name=f"band_bwd_dq_{dil}", grid=(dil, nb),
        in_specs=[cur] + _win_specs(T, E, nb) + _win_specs(T, E, nb) + [bspec,
                  pl.BlockSpec((H, 1, T, 2 * E), lambda d, n: (0, d, n, 0))],
        out_specs=[cur, bspec],
        out_shape=[jax.ShapeDtypeStruct((H, dil, Ls, E), MXU_DTYPE), jax.ShapeDtypeStruct((H, T, W), F32)],
        compiler_params=_cp("arbitrary", "arbitrary"),
    )(q, k, k, k, v, v, v, bias, dox)


def band_bwd_dkv(q, k, v, bias_t, dox, *, scale):
    H, dil, Ls, E = k.shape
    T, W = _dl_tiles(Ls)
    nb = Ls // T

    def body(qp_ref, qc_ref, qn_ref, k_ref, v_ref, b_ref, dp_ref, dc_ref, dn_ref, dk_ref, dv_ref):
        n = pl.program_id(1)
        iw = lax.broadcasted_iota(jnp.int32, (W, T), 0)
        j = lax.broadcasted_iota(jnp.int32, (W, T), 1)
        qpos = n * T + iw - DL_HALF
        mask = (jnp.abs(j + DL_HALF - iw) <= DL_HALF) & (qpos >= 0) & (qpos < Ls)
        for h in range(H):
            qw = _win(qp_ref, qc_ref, qn_ref, h, T)
            doxw = _win(dp_ref, dc_ref, dn_ref, h, T)
            dow, lsew, dmw = doxw[:, 0:E], doxw[:, DOX_LSE:DOX_LSE + 1], doxw[:, DOX_DM:DOX_DM + 1]
            s = _dot(qw, k_ref[h, 0], ((1,), (1,))) * scale + b_ref[h]
            p = jnp.where(mask, jnp.exp(jnp.where(mask, s, 0.0) - lsew), 0.0)
            dv_ref[h, 0] = _dot(p, dow, ((0,), (0,))).astype(dv_ref.dtype)
            dp = _dot(dow, v_ref[h, 0], ((1,), (1,)))
            ds = p * (dp - dmw)
            dk_ref[h, 0] = (_dot(ds, qw, ((0,), (0,))) * scale).astype(dk_ref.dtype)

    cur = pl.BlockSpec((H, 1, T, E), lambda d, n: (0, d, n, 0))
    return pl.pallas_call(
        body, name=f"band_bwd_dkv_{dil}", grid=(dil, nb),
        in_specs=_win_specs(T, E, nb) + [cur, cur, pl.BlockSpec((H, W, T), lambda d, n: (0, 0, 0))]
        + _win_specs(T, 2 * E, nb),
        out_specs=[cur, cur],
        out_shape=[jax.ShapeDtypeStruct((H, dil, Ls, E), MXU_DTYPE)] * 2,
        compiler_params=_cp("parallel", "parallel"),
    )(q, q, q, k, v, bias_t, dox, dox, dox)


def dl_merge_fwd(oxs, *, R=1024):
    rows = oxs[0].shape[0]
    R = min(R, rows)
    E = DL_HD

    def body(a_ref, b_ref, c_ref, o_ref):
        vals = [r[...] for r in (a_ref, b_ref, c_ref)]
        o_ref[...] = _dl_merge_fn(*[t[:, 0:E] for t in vals], *[t[:, OX_LSE:OX_LSE + 1] for t in vals])[0]

    spec = pl.BlockSpec((R, 2 * E), lambda i: (i, 0))
    return pl.pallas_call(
        body, name="dl_merge", grid=(rows // R,), in_specs=[spec] * 3,
        out_specs=pl.BlockSpec((R, E), lambda i: (i, 0)), out_shape=jax.ShapeDtypeStruct((rows, E), F32),
        compiler_params=_cp("parallel"),
    )(*oxs)


def dl_merge_bwd(oxs, do, *, R=1024):
    rows = oxs[0].shape[0]
    R = min(R, rows)
    E = DL_HD

    def body(a_ref, b_ref, c_ref, do_ref, da_ref, db_ref, dc_ref):
        vals = [r[...] for r in (a_ref, b_ref, c_ref)]
        os_ = [t[:, 0:E] for t in vals]
        ls_ = [t[:, OX_LSE:OX_LSE + 1] for t in vals]
        _, vjp = jax.vjp(_dl_merge_fn, *os_, *ls_)
        g = vjp((do_ref[...],))
        for k, d_ref in enumerate((da_ref, db_ref, dc_ref)):
            dm = jnp.sum(g[k] * os_[k], axis=1, keepdims=True) - g[3 + k]
            d_ref[:, 0:E] = g[k]
            d_ref[:, DOX_LSE:DOX_DM] = ls_[k] + jnp.zeros((R, DOX_DM - DOX_LSE), F32)
            d_ref[:, DOX_DM:2 * E] = dm + jnp.zeros((R, 2 * E - DOX_DM), F32)

    spec = pl.BlockSpec((R, 2 * E), lambda i: (i, 0))
    return pl.pallas_call(
        body, name="dl_merge_bwd", grid=(rows // R,), in_specs=[spec] * 3 + [pl.BlockSpec((R, E), lambda i: (i, 0))],
        out_specs=[spec] * 3, out_shape=[jax.ShapeDtypeStruct((rows, 2 * E), F32)] * 3,
        compiler_params=_cp("parallel"),
    )(*oxs, do)


def _dl_bias_tables(rel_bias, dil, T):
    W = T + 2 * DL_HALF
    bq, bk = _dl_bucket_tables(dil, T)
    idx = np.concatenate([bq.reshape(-1), bk.reshape(-1)])
    onehot_t = (np.arange(REL_BUCKETS)[:, None] == idx[None, :]).astype(np.float32)
    tab = matmul(rel_bias.T, jnp.asarray(onehot_t), exact=True, name=f"dl_bias_{dil}", tm=DL_HEADS, tk=REL_BUCKETS,
                 tn=_tile(2 * T * W, (8192, 4096, 2048, 1024, 512, 256, 128)))
    return tab[:, :T * W].reshape(DL_HEADS, T, W), tab[:, T * W:].reshape(DL_HEADS, W, T), bq


def _dl_dm_fn(do, o, dl):
    return (jnp.sum(do * o, axis=-1, keepdims=True) - dl,)


def _old_dl_layer_fwd(x, ng, p):
    S = x.shape[0]
    hn = _prenorm("dl", x, ng)
    u = matmul(hn, p["w_in"], name="dl_in")
    scale = DL_HD ** -0.5
    per_group, o_hm, lse_hm = [], [], []
    for gi, (window, dil) in enumerate(DL_PAIRS):
        base = gi * 3 * DL_W
        Ls = S // dil
        T, _ = _dl_tiles(Ls)
        bq, bk = _dl_bucket_tables(dil, T)
        qs = _to_stream(u[:, base:base + DL_W], dil).astype(MXU_DTYPE)
        ks = _to_stream(u[:, base + DL_W:base + 2 * DL_W], dil).astype(MXU_DTYPE)
        vs = _to_stream(u[:, base + 2 * DL_W:base + 3 * DL_W], dil).astype(MXU_DTYPE)
        bias = p["rel_bias"][bq].transpose(2, 0, 1)
        o_s, lse_s = band_fwd(qs, _pad_l(ks), _pad_l(vs), bias, scale=scale)
        per_group.append(dict(qs=qs, ks=ks, vs=vs, lse_s=lse_s, bq=bq, bk=bk, dil=dil))
        o_hm.append(_stream_to_hm(o_s))
        lse_hm.append(_stream_to_hm(lse_s))
    rows = [(t, 0, False) for t in o_hm] + [(t, 0, False, 1) for t in lse_hm]
    om = rowwise_fwd("dl_merge", _dl_merge_fn, rows, [], [(DL_HD, F32)], W=DL_HD)[0]
    o = om.reshape(DL_HEADS, S, DL_HD).transpose(1, 0, 2).reshape(S, DL_W)
    g = rowwise_fwd("dl_gate", _gate_fn, [(o, 0, False), (u, 9, False)], [], [(DL_W, MXU_DTYPE)], W=DL_W)[0]
    xn = matmul(g, p["w_out"], residual=x, name="dl_out")
    return xn, dict(x=x, ng=ng, hn=hn, u=u, per_group=per_group, o_hm=o_hm, lse_hm=lse_hm, o=o, g=g)


def _old_dl_layer_bwd(sv, p, dx):
    u = sv["u"]
    S = u.shape[0]
    scale = DL_HD ** -0.5
    dg = matmul(dx, p["w_out"], tb=True, name="dl_dg")
    dw_out = matmul(sv["g"], dx, ta=True, out_dtype=GRAD_WIRE_DTYPE, name="dl_dw_out")
    do, dgate = rowwise_bwd("dl_gate_bwd", _gate_fn, [(sv["o"], 0, False), (sv["ugate"], 0, False)], [], [dg], W=DL_W,
                            diff_rows=[0, 1], diff_shared=[])
    do_hm = do.reshape(S, DL_HEADS, DL_HD).transpose(1, 0, 2).reshape(DL_HEADS * S, DL_HD)
    rows = [(t, 0, False) for t in sv["o_hm"]] + [(t, 0, False, 1) for t in sv["lse_hm"]]
    dmerge = rowwise_bwd("dl_merge_bwd", _dl_merge_fn, rows, [], [do_hm], W=DL_HD, diff_rows=[0, 1, 2, 3, 4, 5],
                         diff_shared=[])
    parts, dbs, onehots = [], [], []
    for gi, pg in enumerate(sv["per_group"]):
        dil = pg["dil"]
        Ls = S // dil
        T, W = _dl_tiles(Ls)
        dog, dlg = dmerge[gi], dmerge[3 + gi]
        dm = rowwise_fwd(f"dl_dm_{gi}", _dl_dm_fn, [(dog, 0, False), (sv["o_hm"][gi], 0, False), (dlg, 0, False, 1)], [],
                         [(1, F32)], W=DL_HD)[0]
        do_s, dm_s = _hm_to_stream(dog, dil), _hm_to_stream(dm, dil)
        bias = p["rel_bias"][pg["bq"]].transpose(2, 0, 1)
        bias_t = p["rel_bias"][pg["bk"]].transpose(2, 0, 1)
        kp, vp = _pad_l(pg["ks"]), _pad_l(pg["vs"])
        dq_s, dbias = band_bwd_dq(pg["qs"], kp, vp, bias, pg["lse_s"], dm_s, do_s, scale=scale)
        dk_s, dv_s = band_bwd_dkv(_pad_l(pg["qs"]), pg["ks"], pg["vs"], bias_t, _pad_l(pg["lse_s"]), _pad_l(dm_s),
                                  _pad_l(do_s), scale=scale)
        parts += [_from_stream(dq_s), _from_stream(dk_s), _from_stream(dv_s)]
        dbs.append(dbias.reshape(DL_HEADS, T * W))
        onehots.append((pg["bq"].reshape(-1)[:, None] == np.arange(REL_BUCKETS)[None, :]).astype(np.float32))
    drel = matmul(jnp.concatenate(dbs, axis=1), jnp.asarray(np.concatenate(onehots, axis=0)), exact=True,
                  name="dl_drel", tm=DL_HEADS, tn=REL_BUCKETS, tk=2048)
    du = _cat_mxu(parts + [dgate])
    dx_prev, dng, dw_in = _in_out_bwd("dl", sv["x"], sv["ng"], sv["hn"], du, p["w_in"], dx)
    return dx_prev, dict(w_in=dw_in, w_out=dw_out, rel_bias=drel.T, ng=dng)


def dl_layer_fwd(x, ng, p):
    S = x.shape[0]
    hn = _prenorm("dl", x, ng)
    nqkv = 3 * len(DL_PAIRS) * DL_W
    uqkv = matmul(hn, p["w_in"], name="dl_in_qkv", b_cols=(0, nqkv), out_dtype=MXU_DTYPE)
    ugate = matmul(hn, p["w_in"], name="dl_in_gate", b_cols=(nqkv, DL_W))
    scale = DL_HD ** -0.5
    per_group, ox_hm = [], []
    for gi, (window, dil) in enumerate(DL_PAIRS):
        base = gi * 3 * DL_W
        T, _ = _dl_tiles(S // dil)
        qs, ks, vs = [_to_stream(uqkv[:, base + c * DL_W:base + (c + 1) * DL_W], dil) for c in range(3)]
        bias, bias_t, bq = _dl_bias_tables(p["rel_bias"], dil, T)
        ox_s = band_fwd(qs, ks, vs, bias, scale=scale)
        per_group.append(dict(qs=qs, ks=ks, vs=vs, bias=bias, bias_t=bias_t, bq=bq, dil=dil))
        ox_hm.append(_stream_to_hm(ox_s))
    om = dl_merge_fwd(ox_hm)
    o = om.reshape(DL_HEADS, S, DL_HD).transpose(1, 0, 2).reshape(S, DL_W)
    g = rowwise_fwd("dl_gate", _gate_fn, [(o, 0, False), (ugate, 0, False)], [], [(DL_W, MXU_DTYPE)], W=DL_W)[0]
    xn = matmul(g, p["w_out"], residual=x, name="dl_out")
    return xn, dict(x=x, ng=ng, hn=hn, ugate=ugate, per_group=per_group, ox_hm=ox_hm, o=o, g=g)


def dl_layer_bwd(sv, p, dx):
    ugate = sv["ugate"]
    S = ugate.shape[0]
    scale = DL_HD ** -0.5
    dg = matmul(dx, p["w_out"], tb=True, name="dl_dg")
    dw_out = matmul(sv["g"], dx, ta=True, out_dtype=GRAD_WIRE_DTYPE, name="dl_dw_out")
    do, dgate = rowwise_bwd("dl_gate_bwd", _gate_fn, [(sv["o"], 0, False), (sv["ugate"], 0, False)], [], [dg], W=DL_W,
                            diff_rows=[0, 1], diff_shared=[])
    do_hm = do.reshape(S, DL_HEADS, DL_HD).transpose(1, 0, 2).reshape(DL_HEADS * S, DL_HD)
    dox_hm = dl_merge_bwd(sv["ox_hm"], do_hm)
    parts, dbs, onehots = [], [], []
    for gi, pg in enumerate(sv["per_group"]):
        dil = pg["dil"]
        T, W = _dl_tiles(S // dil)
        dox_s = _hm_to_stream(dox_hm[gi], dil)
        dq_s, dbias = band_bwd_dq(pg["qs"], pg["ks"], pg["vs"], pg["bias"], dox_s, scale=scale)
        dk_s, dv_s = band_bwd_dkv(pg["qs"], pg["ks"], pg["vs"], pg["bias_t"], dox_s, scale=scale)
        parts += [_from_stream(dq_s), _from_stream(dk_s), _from_stream(dv_s)]
        dbs.append(dbias.reshape(DL_HEADS, T * W))
        onehots.append((pg["bq"].reshape(-1)[:, None] == np.arange(REL_BUCKETS)[None, :]).astype(np.float32))
    drel = matmul(jnp.concatenate(dbs, axis=1), jnp.asarray(np.concatenate(onehots, axis=0)), exact=True,
                  name="dl_drel", tm=DL_HEADS, tn=REL_BUCKETS, tk=2048)
    du = _cat_mxu(parts + [dgate])
    dx_prev, dng, dw_in = _in_out_bwd("dl", sv["x"], sv["ng"], sv["hn"], du, p["w_in"], dx)
    return dx_prev, dict(w_in=dw_in, w_out=dw_out, rel_bias=drel.T, ng=dng)


WEIGHT_ORDER = ['norm_g', 'final_g', 'rel_bias', 'hgrn_lb', 'ssd_w_in', 'ssd_conv_w', 'ssd_conv_b', 'ssd_dt_bias',
                'ssd_a_log', 'ssd_d', 'ssd_norm_g', 'ssd_w_out', 'hg_w_in', 'hg_norm_g', 'hg_w_out', 'at_w_in',
                'at_q_norm_g', 'at_k_norm_g', 'at_w_out', 'dl_w_in', 'dl_w_out']
BIG_IN = ['ssd_w_in', 'hg_w_in', 'at_w_in', 'dl_w_in']
BIG_OUT = ['ssd_w_out', 'hg_w_out', 'at_w_out', 'dl_w_out']
BIG = BIG_IN + BIG_OUT
SMALL = [n for n in WEIGHT_ORDER if n not in BIG]
LANES = 128


def _pack(arrs):
    flat = jnp.concatenate([a.reshape(-1).astype(F32) for a in arrs])
    n = flat.shape[0]
    rows = -(-n // (8 * LANES)) * 8
    return jnp.pad(flat, (0, rows * LANES - n)).reshape(rows, LANES)


def _unpack(buf, shapes):
    flat = buf.reshape(-1)
    out, off = [], 0
    for shp in shapes:
        n = int(np.prod(shp)) if len(shp) else 1
        out.append(flat[off:off + n].reshape(shp))
        off += n
    return out


def kernel(x, norm_g, final_g, rel_bias, hgrn_lb, ssd_w_in, ssd_conv_w, ssd_conv_b, ssd_dt_bias, ssd_a_log, ssd_d, ssd_norm_g, ssd_w_out, hg_w_in, hg_norm_g, hg_w_out, at_w_in, at_q_norm_g, at_k_norm_g, at_w_out, dl_w_in, dl_w_out, loss_target, m_norm_g, m_final_g, m_rel_bias, m_hgrn_lb, m_ssd_w_in, m_ssd_conv_w, m_ssd_conv_b, m_ssd_dt_bias, m_ssd_a_log, m_ssd_d, m_ssd_norm_g, m_ssd_w_out, m_hg_w_in, m_hg_norm_g, m_hg_w_out, m_at_w_in, m_at_q_norm_g, m_at_k_norm_g, m_at_w_out, m_dl_w_in, m_dl_w_out, v_norm_g, v_final_g, v_rel_bias, v_hgrn_lb, v_ssd_w_in, v_ssd_conv_w, v_ssd_conv_b, v_ssd_dt_bias, v_ssd_a_log, v_ssd_d, v_ssd_norm_g, v_ssd_w_out, v_hg_w_in, v_hg_norm_g, v_hg_w_out, v_at_w_in, v_at_q_norm_g, v_at_k_norm_g, v_at_w_out, v_dl_w_in, v_dl_w_out):
    w = dict(norm_g=norm_g, final_g=final_g, rel_bias=rel_bias, hgrn_lb=hgrn_lb, ssd_w_in=ssd_w_in, ssd_conv_w=ssd_conv_w, ssd_conv_b=ssd_conv_b, ssd_dt_bias=ssd_dt_bias, ssd_a_log=ssd_a_log, ssd_d=ssd_d, ssd_norm_g=ssd_norm_g, ssd_w_out=ssd_w_out, hg_w_in=hg_w_in, hg_norm_g=hg_norm_g, hg_w_out=hg_w_out, at_w_in=at_w_in, at_q_norm_g=at_q_norm_g, at_k_norm_g=at_k_norm_g, at_w_out=at_w_out, dl_w_in=dl_w_in, dl_w_out=dl_w_out)
    m = dict(norm_g=m_norm_g, final_g=m_final_g, rel_bias=m_rel_bias, hgrn_lb=m_hgrn_lb, ssd_w_in=m_ssd_w_in, ssd_conv_w=m_ssd_conv_w, ssd_conv_b=m_ssd_conv_b, ssd_dt_bias=m_ssd_dt_bias, ssd_a_log=m_ssd_a_log, ssd_d=m_ssd_d, ssd_norm_g=m_ssd_norm_g, ssd_w_out=m_ssd_w_out, hg_w_in=m_hg_w_in, hg_norm_g=m_hg_norm_g, hg_w_out=m_hg_w_out, at_w_in=m_at_w_in, at_q_norm_g=m_at_q_norm_g, at_k_norm_g=m_at_k_norm_g, at_w_out=m_at_w_out, dl_w_in=m_dl_w_in, dl_w_out=m_dl_w_out)
    v = dict(norm_g=v_norm_g, final_g=v_final_g, rel_bias=v_rel_bias, hgrn_lb=v_hgrn_lb, ssd_w_in=v_ssd_w_in, ssd_conv_w=v_ssd_conv_w, ssd_conv_b=v_ssd_conv_b, ssd_dt_bias=v_ssd_dt_bias, ssd_a_log=v_ssd_a_log, ssd_d=v_ssd_d, ssd_norm_g=v_ssd_norm_g, ssd_w_out=v_ssd_w_out, hg_w_in=v_hg_w_in, hg_norm_g=v_hg_norm_g, hg_w_out=v_hg_w_out, at_w_in=v_at_w_in, at_q_norm_g=v_at_q_norm_g, at_k_norm_g=v_at_k_norm_g, at_w_out=v_at_w_out, dl_w_in=v_dl_w_in, dl_w_out=v_dl_w_out)
    me = 4 * lax.axis_index("x") + 2 * lax.axis_index("y") + lax.axis_index("c")
    xs = x[0]
    S = xs.shape[0]

    shard2d = {n: w[n][0] for n in BIG}
    wire = {n: shard2d[n].astype(MXU_DTYPE) for n in BIG}

    def ag(names):
        return ("ag", [wire[n] for n in names])

    def assemble(names, blks):
        out = {}
        for n, blk in zip(names, blks):
            r, c = shard2d[n].shape
            out[n] = blk.transpose(1, 0, 2).reshape(r, N_DEV * c) if n in BIG_IN else blk.reshape(N_DEV * r, c)
        return out

    def a2a(names, gw):
        bufs = []
        for n in names:
            r, c = shard2d[n].shape
            bufs.append(gw[n].reshape(r, N_DEV, c).transpose(1, 0, 2) if n in BIG_IN else gw[n].reshape(N_DEV, r, c))
        return ("a2a", bufs)

    ssd_w, hg_w, at_w, dl_w = (["ssd_w_in", "ssd_w_out"], ["hg_w_in", "hg_w_out"], ["at_w_in", "at_w_out"],
                               ["dl_w_in", "dl_w_out"])
    full = assemble(ssd_w, allgather_two_level_multi([wire[n] for n in ssd_w], name="allgather_ssd_weights"))
    ncw = ssd_conv_w.shape[2]
    nhg = hg_norm_g.shape[1]
    small_shard = jnp.zeros((8, 512), F32)
    small_shard = small_shard.at[:SSD_CONV, :ncw].set(ssd_conv_w[0]).at[SSD_CONV, :nhg].set(hg_norm_g[0])
    small_all = allgather_two_level(small_shard, name="allgather_small_weights")
    conv_w_full = small_all[:, :SSD_CONV, :ncw].transpose(1, 0, 2).reshape(SSD_CONV, N_DEV * ncw)
    hg_norm_full = small_all[:, SSD_CONV, :nhg].reshape(1, N_DEV * nhg)

    p_ssd = dict(w_in=jnp.pad(full["ssd_w_in"], ((0, 0), (0, SSD_IN_PAD - SSD_IN))), w_out=full["ssd_w_out"],
                 conv_w=conv_w_full, conv_b=ssd_conv_b,
                 dt_bias=jnp.pad(ssd_dt_bias.reshape(1, 2 * SSD_HEADS), ((0, 0), (0, 128 - 2 * SSD_HEADS))),
                 alog=jnp.pad(ssd_a_log.reshape(1, 2 * SSD_HEADS), ((0, 0), (0, 128 - 2 * SSD_HEADS))),
                 dexp=jnp.repeat(ssd_d.reshape(-1), SSD_HEADDIM)[None, :], norm_g=ssd_norm_g)
    x1, sv0, got = ssd_layer_fwd(xs, norm_g[0:1], p_ssd, comm=ag(hg_w))
    full.update(assemble(hg_w, got))
    p_hg = dict(w_in=full["hg_w_in"], w_out=full["hg_w_out"], norm_g=hg_norm_full, hgrn_lb=hgrn_lb)
    x2, sv1, got_at, _ = hg_layer_fwd(x1, norm_g[1:2], p_hg, comm0=ag(at_w))
    full.update(assemble(at_w, got_at))
    p_at = dict(w_in=full["at_w_in"], w_out=full["at_w_out"], q_g=at_q_norm_g, k_g=at_k_norm_g)
    x3, sv2, got_dl = at_layer_fwd(x2, norm_g[2:3], p_at, comm=ag(dl_w))
    full.update(assemble(dl_w, got_dl))
    p_dl = dict(w_in=full["dl_w_in"], w_out=full["dl_w_out"], rel_bias=rel_bias)
    x4, sv3 = dl_layer_fwd(x3, norm_g[3:4], p_dl)
    loss_part, dx4, dfinal = loss_head(x4, final_g[None, :], loss_target[0])
    dx3, g3 = dl_layer_bwd(sv3, p_dl, dx4)
    dx2, g2, recv_dl = at_layer_bwd(sv2, p_at, dx3, comm=a2a(dl_w, dict(dl_w_in=g3["w_in"], dl_w_out=g3["w_out"])))
    dx1, g1, recv_at = hg_layer_bwd(sv1, p_hg, dx2, comm=a2a(at_w, dict(at_w_in=g2["w_in"], at_w_out=g2["w_out"])))
    dx0, g0, recv_hg = ssd_layer_bwd(sv0, p_ssd, dx1, comm=a2a(hg_w, dict(hg_w_in=g1["w_in"], hg_w_out=g1["w_out"])))
    recv_ssd = comm_only(a2a(ssd_w, dict(ssd_w_in=g0["w_in"], ssd_w_out=g0["w_out"])), name="exchange_ssd_grads")
    recv = dict(zip(ssd_w + hg_w + at_w + dl_w, recv_ssd + recv_hg + recv_at + recv_dl))

    small_full = dict(
        norm_g=jnp.concatenate([g0["ng"], g1["ng"], g2["ng"], g3["ng"]], axis=0), final_g=dfinal[0],
        rel_bias=g3["rel_bias"], hgrn_lb=g1["hgrn_lb"], ssd_conv_w=g0["conv_w"][None], ssd_conv_b=g0["conv_b"],
        ssd_dt_bias=g0["dt_bias"].reshape(1, 2, SSD_HEADS), ssd_a_log=g0["a_log"].reshape(1, 2, SSD_HEADS),
        ssd_d=g0["d"], ssd_norm_g=g0["norm_g"], hg_norm_g=g1["norm_g"], at_q_norm_g=g2["q_g"], at_k_norm_g=g2["k_g"])
    packed = _pack([loss_part[0, 0:1]] + [small_full[n] for n in SMALL])
    summed = sum_parts(allgather_two_level(packed, name="allgather_small_grads"), name="sum_small_grads")
    parts = _unpack(summed, [()] + [small_full[n].shape for n in SMALL])
    loss = parts[0]
    gsmall = dict(zip(SMALL, parts[1:]))
    gsmall["ssd_conv_w"] = lax.dynamic_slice_in_dim(gsmall["ssd_conv_w"], me * ncw, ncw, axis=2)
    gsmall["hg_norm_g"] = lax.dynamic_slice_in_dim(gsmall["hg_norm_g"], me * nhg, nhg, axis=1)
    shapes = [w[n].shape for n in SMALL]
    d_p, m_p, v_p = adamw_plain(_pack([w[n] for n in SMALL]), _pack([gsmall[n] for n in SMALL]),
                                _pack([m[n] for n in SMALL]), _pack([v[n] for n in SMALL]), name="adamw_small")
    grads = dict(gsmall)
    deltas = dict(zip(SMALL, _unpack(d_p, shapes)))
    new_m = dict(zip(SMALL, _unpack(m_p, shapes)))
    new_v = dict(zip(SMALL, _unpack(v_p, shapes)))

    for n in BIG:
        gs, ds, ms, vs = adamw_sum(recv[n], shard2d[n], m[n][0], v[n][0], name=f"adamw_{n}")
        grads[n], deltas[n], new_m[n], new_v[n] = gs[None], ds[None], ms[None], vs[None]

    return (loss, dx0[None], *[grads[n] for n in WEIGHT_ORDER], *[deltas[n] for n in WEIGHT_ORDER],
            *[new_m[n] for n in WEIGHT_ORDER], *[new_v[n] for n in WEIGHT_ORDER])
```

```python
import functools
import math

import jax
import jax.numpy as jnp
import numpy as np
from jax import lax
from jax.experimental import pallas as pl
from jax.experimental.pallas import tpu as pltpu

F32 = jnp.float32
BF16 = jnp.bfloat16
MXU_DTYPE = jnp.bfloat16
GRAD_WIRE_DTYPE = jnp.bfloat16
HIGHEST = lax.Precision.HIGHEST
MESH_ID = pl.DeviceIdType.MESH
N_DEV = 8

D_MODEL = 1024
EPS = 1e-6
NEG_BIG = -1e30

SSD_DI = 2048
SSD_HEADDIM = 64
SSD_HEADS = 32
SSD_GROUPS = 4
SSD_HPG = 8
SSD_STATE = 128
SSD_CONV = 7
SSD_CHUNK = 128
SSD_CONV_CH = SSD_DI + 2 * SSD_GROUPS * SSD_STATE
SSD_IN = SSD_DI + SSD_CONV_CH + 2 * SSD_HEADS
SSD_IN_PAD = 5376

HG_CHUNK = 32
HG_HEADS = 8
HG_D = 128
HG_W = 1024

AT_HEADS = 16
AT_KV = 8
AT_HD = 128
AT_QW = 2048
AT_KW = 1024
GRID_W = 64
ROPE_THETA = 10000.0

DL_PAIRS = ((128, 1), (512, 4), (2048, 16))
DL_HEADS = 16
DL_HD = 64
DL_W = 1024
DL_HALF = 64
REL_BUCKETS = 32
REL_MAX_DIST = 1024

ADAM_LR = 0.001
ADAM_B1 = 0.9
ADAM_B2 = 0.999
ADAM_EPS = 1e-08
ADAM_WD = 0.01
ADAM_STEP = 10

VMEM_LIMIT = 56 * 1024 * 1024


def _cp(*sem):
    return pltpu.CompilerParams(dimension_semantics=tuple(sem), vmem_limit_bytes=VMEM_LIMIT)


def _tile(n, cands=(1024, 768, 512, 384, 256, 128)):
    for c in cands:
        if n % c == 0:
            return c
    return n


def _dot(a, b, dims):
    return lax.dot_general(a.astype(MXU_DTYPE), b.astype(MXU_DTYPE), (dims, ((), ())), preferred_element_type=F32)


def _dot_exact(a, b, dims):
    return lax.dot_general(a, b, (dims, ((), ())), precision=HIGHEST, preferred_element_type=F32)


def _silu(x):
    return x * jax.nn.sigmoid(x)


def _my_pos():
    return lax.axis_index("x"), lax.axis_index("y"), lax.axis_index("c")


def _flat(px, py, pc):
    return 4 * px + 2 * py + pc


def _peers():
    x_, y_, c_ = _my_pos()
    out = []
    for k in range(1, N_DEV):
        fx, fy, fc = (k >> 2) & 1, (k >> 1) & 1, k & 1
        out.append(((1 - x_) if fx else x_, (1 - y_) if fy else y_, (1 - c_) if fc else c_))
    return out


def _comm_copies(kind, in_refs, out_refs, send_sems, recv_sems, local_sems):
    me = _flat(*_my_pos())
    local, starts, waits = [], [], []
    for b, (i_ref, o_ref) in enumerate(zip(in_refs, out_refs)):
        local.append(pltpu.make_async_copy(i_ref if kind == "ag" else i_ref.at[me], o_ref.at[me], local_sems.at[b]))
        for k, p in enumerate(_peers()):
            src = i_ref if kind == "ag" else i_ref.at[_flat(*p)]
            starts.append(pltpu.make_async_remote_copy(
                src_ref=src, dst_ref=o_ref.at[me], send_sem=send_sems.at[b, k], recv_sem=recv_sems.at[b, k],
                device_id=p, device_id_type=MESH_ID))
            waits.append(pltpu.make_async_remote_copy(
                src_ref=src, dst_ref=o_ref.at[_flat(*p)], send_sem=send_sems.at[b, k], recv_sem=recv_sems.at[b, k],
                device_id=p, device_id_type=MESH_ID))
    return local, starts, waits


def pcall(body, comm, *, name, grid, in_specs, out_specs, out_shape, scratch_shapes=(), compiler_params=None):
    single = not isinstance(out_specs, (list, tuple))
    out_specs_l = [out_specs] if single else list(out_specs)
    out_shape_l = [out_shape] if single else list(out_shape)
    if comm is None:
        return pl.pallas_call(body, name=name, grid=grid, in_specs=in_specs, out_specs=out_specs, out_shape=out_shape,
                              scratch_shapes=list(scratch_shapes), compiler_params=compiler_params)
    kind, bufs = comm
    nb, n_in, n_out, n_scr = len(bufs), len(in_specs), len(out_specs_l), len(scratch_shapes)
    c_shape = [jax.ShapeDtypeStruct(((N_DEV,) + b.shape) if kind == "ag" else b.shape, b.dtype) for b in bufs]
    anyspec = pl.BlockSpec(memory_space=pl.ANY)

    def body2(*refs):
        ins, c_ins = refs[:n_in], refs[n_in:n_in + nb]
        outs = refs[n_in + nb:n_in + nb + n_out]
        c_outs = refs[n_in + nb + n_out:n_in + 2 * nb + n_out]
        scr = refs[n_in + 2 * nb + n_out:n_in + 2 * nb + n_out + n_scr]
        send_sems, recv_sems, local_sems = refs[n_in + 2 * nb + n_out + n_scr:]
        first = last = None
        for ax, g in enumerate(grid):
            pid = pl.program_id(ax)
            first = (pid == 0) if first is None else jnp.logical_and(first, pid == 0)
            last = (pid == g - 1) if last is None else jnp.logical_and(last, pid == g - 1)

        @pl.when(first)
        def _():
            local, starts, _ = _comm_copies(kind, c_ins, c_outs, send_sems, recv_sems, local_sems)
            for cp in local + starts:
                cp.start()

        body(*ins, *outs, *scr)

        @pl.when(last)
        def _():
            local, _, waits = _comm_copies(kind, c_ins, c_outs, send_sems, recv_sems, local_sems)
            for cp in waits + local:
                cp.wait()

    call = pl.pallas_call(
        body2, name=name, grid=grid, in_specs=list(in_specs) + [anyspec] * nb,
        out_specs=out_specs_l + [anyspec] * nb, out_shape=out_shape_l + c_shape,
        scratch_shapes=list(scratch_shapes) + [pltpu.SemaphoreType.DMA((nb, N_DEV - 1)),
                                               pltpu.SemaphoreType.DMA((nb, N_DEV - 1)), pltpu.SemaphoreType.DMA((nb,))],
        compiler_params=compiler_params)

    def run(*args):
        res = call(*args, *bufs)
        own = res[:n_out]
        return (own[0] if single else list(own)), list(res[n_out:])

    return run


def comm_only(comm, *, name):
    def body(x_ref, o_ref):
        o_ref[...] = x_ref[...]

    spec = pl.BlockSpec((8, 128), lambda i: (0, 0))
    _, res = pcall(body, comm, name=name, grid=(1,), in_specs=[spec], out_specs=spec,
                   out_shape=jax.ShapeDtypeStruct((8, 128), F32))(jnp.zeros((8, 128), F32))
    return res


def matmul(a, b, *, name, ta=False, tb=False, residual=None, out_dtype=F32, exact=False, tm=None, tn=None, tk=None,
           b_cols=None, comm=None):
    M, K = (a.shape[1], a.shape[0]) if ta else a.shape
    n0, N = b_cols if b_cols is not None else (0, b.shape[0] if tb else b.shape[1])
    tm = tm or _tile(M, (1024, 512, 256, 128))
    tn = tn or _tile(N, (1024, 768, 512, 384, 256, 128))
    tk = tk or _tile(K, (1024, 768, 512, 384, 256, 128))
    nk = K // tk
    dims = (((0,) if ta else (1,)), ((1,) if tb else (0,)))

    def body(*refs):
        if residual is None:
            a_ref, b_ref, o_ref, acc = refs
            r_ref = None
        else:
            a_ref, b_ref, r_ref, o_ref, acc = refs
        k = pl.program_id(2)

        @pl.when(k == 0)
        def _():
            acc[...] = jnp.zeros_like(acc)

        if exact:
            acc[...] += _dot_exact(a_ref[...], b_ref[...], dims)
        else:
            acc[...] += _dot(a_ref[...], b_ref[...], dims)

        @pl.when(k == nk - 1)
        def _():
            r = acc[...]
            if r_ref is not None:
                r = r + r_ref[...]
            o_ref[...] = r.astype(o_ref.dtype)

    a_spec = pl.BlockSpec((tk, tm), lambda i, j, k: (k, i)) if ta else pl.BlockSpec((tm, tk), lambda i, j, k: (i, k))
    assert n0 % tn == 0
    jb = n0 // tn
    b_spec = (pl.BlockSpec((tn, tk), lambda i, j, k: (j + jb, k)) if tb
              else pl.BlockSpec((tk, tn), lambda i, j, k: (k, j + jb)))
    in_specs = [a_spec, b_spec]
    args = [a, b]
    if residual is not None:
        in_specs.append(pl.BlockSpec((tm, tn), lambda i, j, k: (i, j)))
        args.append(residual)
    return pcall(
        body, comm, name=name, grid=(M // tm, N // tn, nk), in_specs=in_specs,
        out_specs=pl.BlockSpec((tm, tn), lambda i, j, k: (i, j)),
        out_shape=jax.ShapeDtypeStruct((M, N), out_dtype),
        scratch_shapes=[pltpu.VMEM((tm, tn), F32)],
        compiler_params=_cp("parallel", "parallel", "arbitrary"),
    )(*args)


def _row_specs(rows, shared, R, W):
    specs = []
    for arr, col0, per_j, *wd in rows:
        w = wd[0] if wd else W
        specs.append(pl.BlockSpec((R, w), (lambda j, i, c=col0: (i, c + j)) if per_j else (lambda j, i, c=col0: (i, c))))
    for arr, per_j in shared:
        specs.append(pl.BlockSpec((arr.shape[0], W), (lambda j, i: (0, j)) if per_j else (lambda j, i: (0, 0))))
    return specs


def rowwise_fwd(name, fn, rows, shared, outs, *, W, ncb=1, R=256):
    S = rows[0][0].shape[0]
    R = min(R, S)
    nr, ns = len(rows), len(shared)

    def body(*refs):
        vals = [r[...] for r in refs[:nr + ns]]
        res = fn(*vals)
        for o_ref, r in zip(refs[nr + ns:], res):
            o_ref[...] = r.astype(o_ref.dtype)

    return pl.pallas_call(
        body, name=name, grid=(ncb, S // R),
        in_specs=_row_specs(rows, shared, R, W),
        out_specs=[pl.BlockSpec((R, w), lambda j, i: (i, j)) for w, _ in outs],
        out_shape=[jax.ShapeDtypeStruct((S, ncb * w), dt) for w, dt in outs],
        compiler_params=_cp("parallel", "parallel"),
    )(*[r[0] for r in rows], *[s[0] for s in shared])


def rowwise_bwd(name, fn, rows, shared, cots, *, W, ncb=1, R=256, diff_rows, diff_shared, add=None):
    S = rows[0][0].shape[0]
    R = min(R, S)
    nr, ns, nc = len(rows), len(shared), len(cots)
    nsteps = S // R

    def body(*refs):
        ins = refs[:nr + ns]
        ct_refs = refs[nr + ns:nr + ns + nc]
        pos = nr + ns + nc
        add_ref = None
        if add is not None:
            add_ref = refs[pos]
            pos += 1
        drow_refs = refs[pos:pos + len(diff_rows)]
        dsh_refs = refs[pos + len(diff_rows):]
        j, i = pl.program_id(0), pl.program_id(1)
        vals = [r[...] for r in ins]

        def f(*dv):
            full = list(vals)
            for idx, v in zip(list(diff_rows) + [nr + s for s in diff_shared], dv):
                full[idx] = v
            return tuple(fn(*full))

        prim = [vals[idx] for idx in diff_rows] + [vals[nr + s] for s in diff_shared]
        _, vjp = jax.vjp(f, *prim)
        grads = vjp(tuple(c[...] for c in ct_refs))
        for k, d_ref in enumerate(drow_refs):
            g = grads[k]
            if k == 0 and add_ref is not None:
                g = g + add_ref[...]
            d_ref[...] = g
        for k, (d_ref, s) in enumerate(zip(dsh_refs, diff_shared)):
            g = grads[len(diff_rows) + k]
            first = (i == 0) if shared[s][1] else jnp.logical_and(i == 0, j == 0)

            @pl.when(first)
            def _(d_ref=d_ref, g=g):
                d_ref[...] = g

            @pl.when(jnp.logical_not(first))
            def _(d_ref=d_ref, g=g):
                d_ref[...] += g

    in_specs = _row_specs(rows, shared, R, W)
    wo = [c.shape[1] // ncb for c in cots]
    in_specs += [pl.BlockSpec((R, w), lambda j, i: (i, j)) for w in wo]
    args = [r[0] for r in rows] + [s[0] for s in shared] + list(cots)
    if add is not None:
        in_specs.append(pl.BlockSpec((R, W), lambda j, i: (i, j)))
        args.append(add)
    dws = [(rows[r][3] if len(rows[r]) > 3 else W) for r in diff_rows]
    out_specs = [pl.BlockSpec((R, w), lambda j, i: (i, j)) for w in dws]
    out_shape = [jax.ShapeDtypeStruct((S, ncb * w), F32) for w in dws]
    for s in diff_shared:
        arr, per_j = shared[s]
        out_specs.append(pl.BlockSpec((arr.shape[0], W), (lambda j, i: (0, j)) if per_j else (lambda j, i: (0, 0))))
        out_shape.append(jax.ShapeDtypeStruct((arr.shape[0], ncb * W if per_j else W), F32))
    return pl.pallas_call(
        body, name=name, grid=(ncb, nsteps), in_specs=in_specs, out_specs=out_specs, out_shape=out_shape,
        compiler_params=_cp("arbitrary", "arbitrary"),
    )(*args)


def _row_specs2(rows, shared, R, W, ncb):
    specs = []
    for arr, col0, per_j, *wd in rows:
        w = wd[0] if wd else W
        if per_j:
            assert col0 % ncb == 0
            specs.append(pl.BlockSpec((R, ncb * w), lambda i, c=col0 // ncb: (i, c)))
        else:
            specs.append(pl.BlockSpec((R, w), lambda i, c=col0: (i, c)))
    for arr, per_j in shared:
        specs.append(pl.BlockSpec((arr.shape[0], ncb * W if per_j else arr.shape[1]), lambda i: (0, 0)))
    return specs


def _col_block(ref, per_j, j, w):
    return ref[:, j * w:(j + 1) * w] if per_j else ref[...]


def rowwise_fwd(name, fn, rows, shared, outs, *, W, ncb=1, R=256):
    S = rows[0][0].shape[0]
    R = min(R, S)
    nr, ns = len(rows), len(shared)
    widths = [(r[3] if len(r) > 3 else W) for r in rows]
    per_j = [r[2] for r in rows] + [s[1] for s in shared]
    ws = widths + [W] * ns

    def body(*refs):
        for j in range(ncb):
            vals = [_col_block(refs[k], per_j[k], j, ws[k]) for k in range(nr + ns)]
            res = fn(*vals)
            for o_ref, r, (wo, _) in zip(refs[nr + ns:], res, outs):
                o_ref[:, j * wo:(j + 1) * wo] = r.astype(o_ref.dtype)

    return pl.pallas_call(
        body, name=name, grid=(S // R,),
        in_specs=_row_specs2(rows, shared, R, W, ncb),
        out_specs=[pl.BlockSpec((R, ncb * w), lambda i: (i, 0)) for w, _ in outs],
        out_shape=[jax.ShapeDtypeStruct((S, ncb * w), dt) for w, dt in outs],
        compiler_params=_cp("parallel"),
    )(*[r[0] for r in rows], *[s[0] for s in shared])


def rowwise_bwd(name, fn, rows, shared, cots, *, W, ncb=1, R=256, diff_rows, diff_shared, add=None):
    S = rows[0][0].shape[0]
    R = min(R, S)
    nr, ns, nc = len(rows), len(shared), len(cots)
    widths = [(r[3] if len(r) > 3 else W) for r in rows]
    per_j = [r[2] for r in rows] + [s[1] for s in shared]
    ws = widths + [W] * ns
    wo = [c.shape[1] // ncb for c in cots]
    dws = [widths[r] for r in diff_rows]

    def body(*refs):
        ins = refs[:nr + ns]
        ct_refs = refs[nr + ns:nr + ns + nc]
        pos = nr + ns + nc
        add_ref = None
        if add is not None:
            add_ref = refs[pos]
            pos += 1
        drow_refs = refs[pos:pos + len(diff_rows)]
        dsh_refs = refs[pos + len(diff_rows):]
        i = pl.program_id(0)
        tot = [None] * len(diff_shared)
        for j in range(ncb):
            vals = [_col_block(ins[k], per_j[k], j, ws[k]) for k in range(nr + ns)]

            def f(*dv):
                full = list(vals)
                for idx, v in zip(list(diff_rows) + [nr + s for s in diff_shared], dv):
                    full[idx] = v
                return tuple(fn(*full))

            prim = [vals[idx] for idx in diff_rows] + [vals[nr + s] for s in diff_shared]
            _, vjp = jax.vjp(f, *prim)
            grads = vjp(tuple(c[:, j * w:(j + 1) * w] for c, w in zip(ct_refs, wo)))
            for k, (d_ref, w) in enumerate(zip(drow_refs, dws)):
                g = grads[k]
                if k == 0 and add_ref is not None:
                    g = g + add_ref[:, j * w:(j + 1) * w]
                d_ref[:, j * w:(j + 1) * w] = g
            for k, (d_ref, s) in enumerate(zip(dsh_refs, diff_shared)):
                g = grads[len(diff_rows) + k]
                if shared[s][1]:
                    @pl.when(i == 0)
                    def _(d_ref=d_ref, g=g, j=j):
                        d_ref[:, j * W:(j + 1) * W] = g

                    @pl.when(i != 0)
                    def _(d_ref=d_ref, g=g, j=j):
                        d_ref[:, j * W:(j + 1) * W] += g
                else:
                    tot[k] = g if tot[k] is None else tot[k] + g
        for k, (d_ref, s) in enumerate(zip(dsh_refs, diff_shared)):
            if not shared[s][1]:
                @pl.when(i == 0)
                def _(d_ref=d_ref, g=tot[k]):
                    d_ref[...] = g

                @pl.when(i != 0)
                def _(d_ref=d_ref, g=tot[k]):
                    d_ref[...] += g

    in_specs = _row_specs2(rows, shared, R, W, ncb)
    in_specs += [pl.BlockSpec((R, ncb * w), lambda i: (i, 0)) for w in wo]
    args = [r[0] for r in rows] + [s[0] for s in shared] + list(cots)
    if add is not None:
        in_specs.append(pl.BlockSpec((R, ncb * dws[0]), lambda i: (i, 0)))
        args.append(add)
    out_specs = [pl.BlockSpec((R, ncb * w), lambda i: (i, 0)) for w in dws]
    out_shape = [jax.ShapeDtypeStruct((S, ncb * w), F32) for w in dws]
    for s in diff_shared:
        arr, pj = shared[s]
        shp = (arr.shape[0], ncb * W if pj else arr.shape[1])
        out_specs.append(pl.BlockSpec(shp, lambda i: (0, 0)))
        out_shape.append(jax.ShapeDtypeStruct(shp, F32))
    return pl.pallas_call(
        body, name=name, grid=(S // R,), in_specs=in_specs, out_specs=out_specs, out_shape=out_shape,
        compiler_params=_cp("arbitrary"),
    )(*args)


def _rms(x, g):
    return x * lax.rsqrt(jnp.mean(x * x, axis=-1, keepdims=True) + EPS) * g


def _prenorm_fn(x, g):
    return (_rms(x, g),)


def loss_head(x, g, tgt, *, R=256):
    S, D = x.shape
    R = min(R, S)

    def fn(xv, gv, tv):
        err = _rms(xv, gv) - tv
        return 0.5 * jnp.sum(jnp.mean(err * err, axis=-1, keepdims=True), axis=0, keepdims=True)

    def body(x_ref, g_ref, t_ref, loss_ref, dx_ref, dg_ref):
        i = pl.program_id(0)
        tv = t_ref[...]
        val, vjp = jax.vjp(lambda a, b: fn(a, b, tv), x_ref[...], g_ref[...])
        dx, dg = vjp(jnp.ones((1, 1), F32))
        dx_ref[...] = dx

        @pl.when(i == 0)
        def _():
            loss_ref[...] = jnp.zeros_like(loss_ref) + val
            dg_ref[...] = dg

        @pl.when(i != 0)
        def _():
            loss_ref[...] += val
            dg_ref[...] += dg

    return pl.pallas_call(
        body, name="loss_head", grid=(S // R,),
        in_specs=[pl.BlockSpec((R, D), lambda i: (i, 0)), pl.BlockSpec((1, D), lambda i: (0, 0)),
                  pl.BlockSpec((R, D), lambda i: (i, 0))],
        out_specs=[pl.BlockSpec((1, 128), lambda i: (0, 0)), pl.BlockSpec((R, D), lambda i: (i, 0)),
                   pl.BlockSpec((1, D), lambda i: (0, 0))],
        out_shape=[jax.ShapeDtypeStruct((1, 128), F32), jax.ShapeDtypeStruct((S, D), F32),
                   jax.ShapeDtypeStruct((1, D), F32)],
        compiler_params=_cp("arbitrary"),
    )(x, g, tgt)


@jax.custom_vjp
def _softplus(x):
    z = jnp.exp(-jnp.abs(x))
    u = 1.0 + z
    log1p = jnp.where(u == 1.0, z, jnp.log(u) * (z / jnp.where(u == 1.0, 1.0, u - 1.0)))
    return jnp.maximum(x, 0.0) + log1p


def _softplus_fwd(x):
    return _softplus(x), x


def _softplus_bwd(x, ct):
    return (ct * jax.nn.sigmoid(x),)


_softplus.defvjp(_softplus_fwd, _softplus_bwd)


def _dt_fn(raw, bias):
    return (_softplus(raw + bias),)


CONV_CB = 256
CONV_RB = 512
CONV_PAD = 8


def ssd_conv_fwd(u, conv_w, conv_b):
    S = u.shape[0]
    ncb = SSD_CONV_CH // CONV_CB
    col0 = SSD_DI // CONV_CB
    RB = min(CONV_RB, S)

    def body(x_ref, w_ref, b_ref, o_ref, pad):
        pad[0:CONV_PAD, :] = jnp.zeros((CONV_PAD, CONV_CB), F32)
        pad[S + CONV_PAD:S + 2 * CONV_PAD, :] = jnp.zeros((CONV_PAD, CONV_CB), F32)
        pad[CONV_PAD:S + CONV_PAD, :] = x_ref[...]
        w = w_ref[...]
        b = b_ref[...]
        for r in range(S // RB):
            acc = jnp.zeros((RB, CONV_CB), F32) + b
            for k in range(SSD_CONV):
                off = r * RB + CONV_PAD + k - SSD_CONV // 2
                acc = acc + pad[off:off + RB, :] * w[k:k + 1, :]
            o_ref[r * RB:(r + 1) * RB, :] = _silu(acc)

    return pl.pallas_call(
        body, name="ssd_conv_fwd", grid=(ncb,),
        in_specs=[pl.BlockSpec((S, CONV_CB), lambda j: (0, col0 + j)),
                  pl.BlockSpec((SSD_CONV, CONV_CB), lambda j: (0, j)),
                  pl.BlockSpec((1, CONV_CB), lambda j: (0, j))],
        out_specs=pl.BlockSpec((S, CONV_CB), lambda j: (0, j)),
        out_shape=jax.ShapeDtypeStruct((S, SSD_CONV_CH), F32),
        scratch_shapes=[pltpu.VMEM((S + 2 * CONV_PAD, CONV_CB), F32)],
        compiler_params=_cp("parallel"),
    )(u, conv_w, conv_b)


def ssd_conv_bwd(u, conv_w, conv_b, dact):
    S = u.shape[0]
    ncb = SSD_CONV_CH // CONV_CB
    col0 = SSD_DI // CONV_CB
    RB = min(CONV_RB, S)
    half = SSD_CONV // 2

    def body(x_ref, w_ref, b_ref, da_ref, dx_ref, dw_ref, db_ref, xpad, dpad):
        z8 = jnp.zeros((CONV_PAD, CONV_CB), F32)
        xpad[0:CONV_PAD, :] = z8
        xpad[S + CONV_PAD:S + 2 * CONV_PAD, :] = z8
        dpad[0:CONV_PAD, :] = z8
        dpad[S + CONV_PAD:S + 2 * CONV_PAD, :] = z8
        xpad[CONV_PAD:S + CONV_PAD, :] = x_ref[...]
        w = w_ref[...]
        b = b_ref[...]
        dws = [jnp.zeros((1, CONV_CB), F32) for _ in range(SSD_CONV)]
        db = jnp.zeros((1, CONV_CB), F32)
        for r in range(S // RB):
            acc = jnp.zeros((RB, CONV_CB), F32) + b
            xs = []
            for k in range(SSD_CONV):
                off = r * RB + CONV_PAD + k - half
                xk = xpad[off:off + RB, :]
                xs.append(xk)
                acc = acc + xk * w[k:k + 1, :]
            sg = jax.nn.sigmoid(acc)
            dc = da_ref[r * RB:(r + 1) * RB, :] * (sg * (1.0 + acc * (1.0 - sg)))
            dpad[r * RB + CONV_PAD:(r + 1) * RB + CONV_PAD, :] = dc
            db = db + jnp.sum(dc, axis=0, keepdims=True)
            for k in range(SSD_CONV):
                dws[k] = dws[k] + jnp.sum(xs[k] * dc, axis=0, keepdims=True)
        for r in range(S // RB):
            acc = jnp.zeros((RB, CONV_CB), F32)
            for k in range(SSD_CONV):
                off = r * RB + CONV_PAD + half - k
                acc = acc + dpad[off:off + RB, :] * w[k:k + 1, :]
            dx_ref[r * RB:(r + 1) * RB, :] = acc
        for k in range(SSD_CONV):
            dw_ref[k:k + 1, :] = dws[k]
        dw_ref[SSD_CONV:SSD_CONV + 1, :] = jnp.zeros((1, CONV_CB), F32)
        db_ref[...] = db

    return pl.pallas_call(
        body, name="ssd_conv_bwd", grid=(ncb,),
        in_specs=[pl.BlockSpec((S, CONV_CB), lambda j: (0, col0 + j)),
                  pl.BlockSpec((SSD_CONV, CONV_CB), lambda j: (0, j)),
                  pl.BlockSpec((1, CONV_CB), lambda j: (0, j)),
                  pl.BlockSpec((S, CONV_CB), lambda j: (0, j))],
        out_specs=[pl.BlockSpec((S, CONV_CB), lambda j: (0, j)),
                   pl.BlockSpec((SSD_CONV + 1, CONV_CB), lambda j: (0, j)),
                   pl.BlockSpec((1, CONV_CB), lambda j: (0, j))],
        out_shape=[jax.ShapeDtypeStruct((S, SSD_CONV_CH), F32),
                   jax.ShapeDtypeStruct((SSD_CONV + 1, SSD_CONV_CH), F32),
                   jax.ShapeDtypeStruct((1, SSD_CONV_CH), F32)],
        scratch_shapes=[pltpu.VMEM((S + 2 * CONV_PAD, CONV_CB), F32), pltpu.VMEM((S + 2 * CONV_PAD, CONV_CB), F32)],
        compiler_params=_cp("parallel"),
    )(u, conv_w, conv_b, dact)


def _ssd_chunk(states, xs, Bg, Cg, dt, alog, *, reverse):
    Q = SSD_CHUNK
    r = lax.broadcasted_iota(jnp.int32, (Q, Q), 0)
    c = lax.broadcasted_iota(jnp.int32, (Q, Q), 1)
    keep = (c >= r) if reverse else (c <= r)
    tri = jnp.where(keep, 1.0, 0.0).astype(F32)
    a = dt * (-jnp.exp(alog))
    cum = _dot_exact(tri, a, ((1,), (0,)))
    cum_t = jnp.transpose(cum)
    last = 0 if reverse else Q - 1
    CB = _dot(Cg, Bg, ((1,), (1,)))
    new_states, ys = [], []
    for h in range(SSD_HPG):
        col = h + (SSD_HPG if reverse else 0)
        cum_c = cum[:, col:col + 1]
        cum_r = cum_t[col:col + 1, :]
        dt_c = dt[:, col:col + 1]
        cum_l = cum_c[last:last + 1, :]
        diff = jnp.where(keep, cum_c - cum_r, 0.0)
        L = jnp.where(keep, jnp.exp(diff), 0.0)
        xdt = xs[h] * dt_c
        y = _dot(CB * L, xdt, ((1,), (0,)))
        y = y + _dot(Cg * jnp.exp(cum_c), states[h], ((1,), (0,)))
        ns = jnp.exp(cum_l) * states[h] + _dot(Bg * jnp.exp(cum_l - cum_c), xdt, ((0,), (0,)))
        new_states.append(ns)
        ys.append(y)
    return new_states, ys


def _ssd_group_layout(t):
    r = t.shape[0]
    g = t[:, :2 * SSD_HEADS].reshape(r, 2, SSD_GROUPS, SSD_HPG).transpose(2, 0, 1, 3).reshape(SSD_GROUPS, r, 2 * SSD_HPG)
    return jnp.pad(g, ((0, 0), (0, 0), (0, 128 - 2 * SSD_HPG)))


def _ssd_head_layout(t):
    r = t.shape[1]
    h = t[:, :, :2 * SSD_HPG].reshape(SSD_GROUPS, r, 2, SSD_HPG).transpose(1, 2, 0, 3).reshape(r, 2 * SSD_HEADS)
    return jnp.pad(h, ((0, 0), (0, 128 - 2 * SSD_HEADS)))


def ssd_scan_fwd(act, dt, alog, *, reverse, y_prev=None, comm=None):
    S = act.shape[0]
    Q, N, P = SSD_CHUNK, SSD_STATE, SSD_HEADDIM
    nc = S // Q
    GW = SSD_HPG * P

    def cidx(i):
        return (nc - 1 - i) if reverse else i

    def body(*refs):
        if y_prev is None:
            x_ref, b_ref, c_ref, dt_ref, al_ref, y_ref, st_ref, state = refs
            yp_ref = None
        else:
            x_ref, b_ref, c_ref, dt_ref, al_ref, yp_ref, y_ref, st_ref, state = refs
        g, i = pl.program_id(0), pl.program_id(1)

        @pl.when(i == 0)
        def _():
            state[...] = jnp.zeros_like(state)

        states = [state[h] for h in range(SSD_HPG)]
        for h in range(SSD_HPG):
            st_ref[0, h] = states[h]
        xv = x_ref[...]
        xs = [xv[:, h * P:(h + 1) * P] for h in range(SSD_HPG)]
        ns, ys = _ssd_chunk(states, xs, b_ref[...], c_ref[...], dt_ref[0], al_ref[0], reverse=reverse)
        for h in range(SSD_HPG):
            state[h] = ns[h]
            yh = ys[h]
            if yp_ref is not None:
                yh = yh + yp_ref[:, h * P:(h + 1) * P]
            y_ref[:, h * P:(h + 1) * P] = yh

    in_specs = [pl.BlockSpec((Q, GW), lambda g, i: (cidx(i), g)),
                pl.BlockSpec((Q, N), lambda g, i: (cidx(i), SSD_DI // N + g)),
                pl.BlockSpec((Q, N), lambda g, i: (cidx(i), SSD_DI // N + SSD_GROUPS + g)),
                pl.BlockSpec((1, Q, 128), lambda g, i: (g, cidx(i), 0)),
                pl.BlockSpec((1, 1, 128), lambda g, i: (g, 0, 0))]
    args = [act, act, act, dt, alog]
    if y_prev is not None:
        in_specs.append(pl.BlockSpec((Q, GW), lambda g, i: (cidx(i), g)))
        args.append(y_prev)
    return pcall(
        body, comm, name=f"ssd_scan_fwd_{int(reverse)}", grid=(SSD_GROUPS, nc), in_specs=in_specs,
        out_specs=[pl.BlockSpec((Q, GW), lambda g, i: (cidx(i), g)),
                   pl.BlockSpec((1, SSD_HPG, N, P), lambda g, i: (cidx(i), g, 0, 0))],
        out_shape=[jax.ShapeDtypeStruct((S, SSD_DI), F32), jax.ShapeDtypeStruct((nc, SSD_HEADS, N, P), F32)],
        scratch_shapes=[pltpu.VMEM((SSD_HPG, N, P), F32)],
        compiler_params=_cp("arbitrary", "arbitrary"),
    )(*args)


def ssd_scan_bwd(act, dt, alog, states, dy, prev_x, *, reverse, prev=None, comm=None):
    S = act.shape[0]
    Q, N, P = SSD_CHUNK, SSD_STATE, SSD_HEADDIM
    nc = S // Q
    GW = SSD_HPG * P

    def cidx(i):
        return i if reverse else (nc - 1 - i)

    def body(*refs):
        x_ref, b_ref, c_ref, dt_ref, al_ref, st_ref, dy_ref, px_ref = refs[:8]
        pos = 8
        if prev is not None:
            pb_ref, pc_ref, pdt_ref, pal_ref = refs[pos:pos + 4]
            pos += 4
        dx_ref, db_ref, dc_ref, ddt_ref, dal_ref, dstate = refs[pos:]
        g, i = pl.program_id(0), pl.program_id(1)

        @pl.when(i == 0)
        def _():
            dstate[...] = jnp.zeros_like(dstate)

        xv = x_ref[...]
        dyv = dy_ref[...]
        xs = [xv[:, h * P:(h + 1) * P] for h in range(SSD_HPG)]
        dys = [dyv[:, h * P:(h + 1) * P] for h in range(SSD_HPG)]
        states = [st_ref[0, h] for h in range(SSD_HPG)]
        dstates = [dstate[h] for h in range(SSD_HPG)]

        def f(states, xs, Bg, Cg, dtv, al):
            return _ssd_chunk(states, xs, Bg, Cg, dtv, al, reverse=reverse)

        _, vjp = jax.vjp(f, states, xs, b_ref[...], c_ref[...], dt_ref[0], al_ref[0])
        dst, dxs, dB, dC, ddt, dal = vjp((dstates, dys))
        for h in range(SSD_HPG):
            dstate[h] = dst[h]
            dx_ref[:, h * P:(h + 1) * P] = dxs[h] + px_ref[:, h * P:(h + 1) * P]
        if prev is not None:
            dB = dB + pb_ref[...]
            dC = dC + pc_ref[...]
            ddt = ddt + pdt_ref[0]
        db_ref[...] = dB
        dc_ref[...] = dC
        ddt_ref[0] = ddt

        @pl.when(i == 0)
        def _():
            dal_ref[0] = dal + (pal_ref[0] if prev is not None else 0.0)

        @pl.when(i != 0)
        def _():
            dal_ref[0] += dal

    xspec = pl.BlockSpec((Q, GW), lambda g, i: (cidx(i), g))
    nspec_b = pl.BlockSpec((Q, N), lambda g, i: (cidx(i), SSD_DI // N + g))
    nspec_c = pl.BlockSpec((Q, N), lambda g, i: (cidx(i), SSD_DI // N + SSD_GROUPS + g))
    dtspec = pl.BlockSpec((1, Q, 128), lambda g, i: (g, cidx(i), 0))
    alspec = pl.BlockSpec((1, 1, 128), lambda g, i: (g, 0, 0))
    in_specs = [xspec, nspec_b, nspec_c,
                dtspec, alspec,
                pl.BlockSpec((1, SSD_HPG, N, P), lambda g, i: (cidx(i), g, 0, 0)),
                xspec]
    gspec = pl.BlockSpec((Q, N), lambda g, i: (cidx(i), g))
    in_specs.append(xspec)
    args = [act, act, act, dt, alog, states, dy, prev_x]
    if prev is not None:
        in_specs += [gspec, gspec, dtspec, alspec]
        args += list(prev)
    outs = pl.pallas_call(
        body, name=f"ssd_scan_bwd_{int(reverse)}", grid=(SSD_GROUPS, nc), in_specs=in_specs,
        out_specs=[pl.BlockSpec((Q, GW), lambda g, i: (cidx(i), g)),
                   pl.BlockSpec((Q, N), lambda g, i: (cidx(i), g)),
                   pl.BlockSpec((Q, N), lambda g, i: (cidx(i), g)),
                   dtspec, alspec],
        out_shape=[jax.ShapeDtypeStruct((S, SSD_DI), F32), jax.ShapeDtypeStruct((S, SSD_GROUPS * N), F32),
                   jax.ShapeDtypeStruct((S, SSD_GROUPS * N), F32),
                   jax.ShapeDtypeStruct((SSD_GROUPS, S, 128), F32), jax.ShapeDtypeStruct((SSD_GROUPS, 1, 128), F32)],
        scratch_shapes=[pltpu.VMEM((SSD_HPG, N, P), F32)],
        compiler_params=_cp("arbitrary", "arbitrary"),
    )(*args)
    return outs


def _ssd_chunk(state, x, Bg, Cg, dt, alog, *, reverse):
    Q, P = SSD_CHUNK, SSD_HEADDIM
    r = lax.broadcasted_iota(jnp.int32, (Q, Q), 0)
    c = lax.broadcasted_iota(jnp.int32, (Q, Q), 1)
    keep = (c >= r) if reverse else (c <= r)
    a = dt * (-jnp.exp(alog))
    cum = _cumsum_rows(a, reverse)
    cum_t = jnp.transpose(cum)
    last = 0 if reverse else Q - 1
    CB = _dot(Cg, Bg, ((1,), (1,)))
    yoff = _dot(Cg, state, ((1,), (0,)))
    ys, xdecs, keeps = [], [], []
    for h in range(SSD_HPG):
        col = h + (SSD_HPG if reverse else 0)
        hs = slice(h * P, (h + 1) * P)
        cum_c = cum[:, col:col + 1]
        cum_r = cum_t[col:col + 1, :]
        cum_l = cum_c[last:last + 1, :]
        L = jnp.where(keep, jnp.exp(jnp.where(keep, cum_c - cum_r, 0.0)), 0.0)
        xdt = x[:, hs] * dt[:, col:col + 1]
        ys.append(_dot(CB * L, xdt, ((1,), (0,))) + jnp.exp(cum_c) * yoff[:, hs])
        xdecs.append(xdt * jnp.exp(cum_l - cum_c))
        keeps.append(jnp.exp(cum_l) + jnp.zeros((1, P), F32))
    new_state = jnp.concatenate(keeps, axis=1) * state + _dot(Bg, jnp.concatenate(xdecs, axis=1), ((0,), (0,)))
    return new_state, jnp.concatenate(ys, axis=1)


def ssd_scan_fwd(act, dt, alog, *, reverse, y_prev=None, comm=None):
    S = act.shape[0]
    Q, N, P = SSD_CHUNK, SSD_STATE, SSD_HEADDIM
    nc = S // Q
    GW = SSD_HPG * P

    def cidx(i):
        return (nc - 1 - i) if reverse else i

    def body(*refs):
        if y_prev is None:
            x_ref, b_ref, c_ref, dt_ref, al_ref, y_ref, st_ref, state = refs
            yp_ref = None
        else:
            x_ref, b_ref, c_ref, dt_ref, al_ref, yp_ref, y_ref, st_ref, state = refs
        i = pl.program_id(1)

        @pl.when(i == 0)
        def _():
            state[...] = jnp.zeros_like(state)

        st = state[...]
        st_ref[0, 0] = st
        ns, y = _ssd_chunk(st, x_ref[...], b_ref[...], c_ref[...], dt_ref[0], al_ref[0], reverse=reverse)
        state[...] = ns
        y_ref[...] = y if yp_ref is None else y + yp_ref[...]

    xspec = pl.BlockSpec((Q, GW), lambda g, i: (cidx(i), g))
    in_specs = [xspec,
                pl.BlockSpec((Q, N), lambda g, i: (cidx(i), SSD_DI // N + g)),
                pl.BlockSpec((Q, N), lambda g, i: (cidx(i), SSD_DI // N + SSD_GROUPS + g)),
                pl.BlockSpec((1, Q, 128), lambda g, i: (g, cidx(i), 0)),
                pl.BlockSpec((1, 1, 128), lambda g, i: (g, 0, 0))]
    args = [act, act, act, dt, alog]
    if y_prev is not None:
        in_specs.append(xspec)
        args.append(y_prev)
    return pcall(
        body, comm, name=f"ssd_scan_fwd_{int(reverse)}", grid=(SSD_GROUPS, nc), in_specs=in_specs,
        out_specs=[xspec, pl.BlockSpec((1, 1, N, GW), lambda g, i: (cidx(i), g, 0, 0))],
        out_shape=[jax.ShapeDtypeStruct((S, SSD_DI), F32), jax.ShapeDtypeStruct((nc, SSD_GROUPS, N, GW), F32)],
        scratch_shapes=[pltpu.VMEM((N, GW), F32)],
        compiler_params=_cp("arbitrary", "arbitrary"),
    )(*args)


def ssd_scan_bwd(act, dt, alog, states, dy, prev_x, *, reverse, prev=None, comm=None):
    S = act.shape[0]
    Q, N, P = SSD_CHUNK, SSD_STATE, SSD_HEADDIM
    nc = S // Q
    GW = SSD_HPG * P

    def cidx(i):
        return i if reverse else (nc - 1 - i)

    def body(*refs):
        x_ref, b_ref, c_ref, dt_ref, al_ref, st_ref, dy_ref, px_ref = refs[:8]
        pos = 8
        if prev is not None:
            pb_ref, pc_ref, pdt_ref, pal_ref = refs[pos:pos + 4]
            pos += 4
        dx_ref, db_ref, dc_ref, ddt_ref, dal_ref, dstate = refs[pos:]
        i = pl.program_id(1)

        @pl.when(i == 0)
        def _():
            dstate[...] = jnp.zeros_like(dstate)

        _, vjp = jax.vjp(functools.partial(_ssd_chunk, reverse=reverse), st_ref[0, 0], x_ref[...], b_ref[...],
                         c_ref[...], dt_ref[0], al_ref[0])
        dst, dx, dB, dC, ddt, dal = vjp((dstate[...], dy_ref[...]))
        dstate[...] = dst
        dx_ref[...] = dx + px_ref[...]
        if prev is not None:
            dB = dB + pb_ref[...]
            dC = dC + pc_ref[...]
            ddt = ddt + pdt_ref[0]
        db_ref[...] = dB
        dc_ref[...] = dC
        ddt_ref[0] = ddt

        @pl.when(i == 0)
        def _():
            dal_ref[0] = dal + (pal_ref[0] if prev is not None else 0.0)

        @pl.when(i != 0)
        def _():
            dal_ref[0] += dal

    xspec = pl.BlockSpec((Q, GW), lambda g, i: (cidx(i), g))
    gspec = pl.BlockSpec((Q, N), lambda g, i: (cidx(i), g))
    dtspec = pl.BlockSpec((1, Q, 128), lambda g, i: (g, cidx(i), 0))
    alspec = pl.BlockSpec((1, 1, 128), lambda g, i: (g, 0, 0))
    in_specs = [xspec,
                pl.BlockSpec((Q, N), lambda g, i: (cidx(i), SSD_DI // N + g)),
                pl.BlockSpec((Q, N), lambda g, i: (cidx(i), SSD_DI // N + SSD_GROUPS + g)),
                dtspec, alspec, pl.BlockSpec((1, 1, N, GW), lambda g, i: (cidx(i), g, 0, 0)), xspec, xspec]
    args = [act, act, act, dt, alog, states, dy, prev_x]
    if prev is not None:
        in_specs += [gspec, gspec, dtspec, alspec]
        args += list(prev)
    return pcall(
        body, comm, name=f"ssd_scan_bwd_{int(reverse)}", grid=(SSD_GROUPS, nc), in_specs=in_specs,
        out_specs=[xspec, gspec, gspec, dtspec, alspec],
        out_shape=[jax.ShapeDtypeStruct((S, SSD_DI), F32), jax.ShapeDtypeStruct((S, SSD_GROUPS * N), F32),
                   jax.ShapeDtypeStruct((S, SSD_GROUPS * N), F32),
                   jax.ShapeDtypeStruct((SSD_GROUPS, S, 128), F32), jax.ShapeDtypeStruct((SSD_GROUPS, 1, 128), F32)],
        scratch_shapes=[pltpu.VMEM((N, GW), F32)],
        compiler_params=_cp("arbitrary", "arbitrary"),
    )(*args)


def _ssd_chunk(state, x, Bg, Cg, dt, alog, dtr, alr, *, reverse):
    Q, P = SSD_CHUNK, SSD_HEADDIM
    r = lax.broadcasted_iota(jnp.int32, (Q, Q), 0)
    c = lax.broadcasted_iota(jnp.int32, (Q, Q), 1)
    keep = (c >= r) if reverse else (c <= r)
    cum_t = jnp.transpose(_cumsum_rows(dt * (-jnp.exp(alog)), reverse))
    cum = _cumsum_rows(dtr * (-jnp.exp(alr)), reverse)
    last = 0 if reverse else Q - 1
    cum_l = cum[last:last + 1, :]
    CB = _dot(Cg, Bg, ((1,), (1,)))
    yoff = _dot(Cg, state, ((1,), (0,))) * jnp.exp(cum)
    xdt = x * dtr
    ys = []
    for h in range(SSD_HPG):
        col = h + (SSD_HPG if reverse else 0)
        hs = slice(h * P, (h + 1) * P)
        cum_q = jnp.concatenate([cum[:, hs]] * (Q // P), axis=1)
        L = jnp.where(keep, jnp.exp(jnp.where(keep, cum_q - cum_t[col:col + 1, :], 0.0)), 0.0)
        ys.append(_dot(CB * L, xdt[:, hs], ((1,), (0,))))
    new_state = jnp.exp(cum_l) * state + _dot(Bg, xdt * jnp.exp(cum_l - cum), ((0,), (0,)))
    return new_state, jnp.concatenate(ys, axis=1) + yoff


def ssd_scan_fwd(act, dt, alog, dtr, alr, *, reverse, y_prev=None, comm=None):
    S = act.shape[0]
    Q, N, P = SSD_CHUNK, SSD_STATE, SSD_HEADDIM
    nc = S // Q
    GW = SSD_HPG * P

    def cidx(i):
        return (nc - 1 - i) if reverse else i

    def body(*refs):
        if y_prev is None:
            x_ref, b_ref, c_ref, dt_ref, al_ref, dtr_ref, alr_ref, y_ref, st_ref, state = refs
            yp_ref = None
        else:
            x_ref, b_ref, c_ref, dt_ref, al_ref, dtr_ref, alr_ref, yp_ref, y_ref, st_ref, state = refs
        i = pl.program_id(1)

        @pl.when(i == 0)
        def _():
            state[...] = jnp.zeros_like(state)

        st = state[...]
        st_ref[0, 0] = st
        ns, y = _ssd_chunk(st, x_ref[...], b_ref[...], c_ref[...], dt_ref[0], al_ref[0], dtr_ref[...], alr_ref[...],
                           reverse=reverse)
        state[...] = ns
        y_ref[...] = y if yp_ref is None else y + yp_ref[...]

    xspec = pl.BlockSpec((Q, GW), lambda g, i: (cidx(i), g))
    in_specs = [xspec,
                pl.BlockSpec((Q, N), lambda g, i: (cidx(i), SSD_DI // N + g)),
                pl.BlockSpec((Q, N), lambda g, i: (cidx(i), SSD_DI // N + SSD_GROUPS + g)),
                pl.BlockSpec((1, Q, 128), lambda g, i: (g, cidx(i), 0)),
                pl.BlockSpec((1, 1, 128), lambda g, i: (g, 0, 0)),
                xspec, pl.BlockSpec((1, GW), lambda g, i: (0, g))]
    args = [act, act, act, dt, alog, dtr, alr]
    if y_prev is not None:
        in_specs.append(xspec)
        args.append(y_prev)
    return pcall(
        body, comm, name=f"ssd_scan_fwd_{int(reverse)}", grid=(SSD_GROUPS, nc), in_specs=in_specs,
        out_specs=[xspec, pl.BlockSpec((1, 1, N, GW), lambda g, i: (cidx(i), g, 0, 0))],
        out_shape=[jax.ShapeDtypeStruct((S, SSD_DI), F32), jax.ShapeDtypeStruct((nc, SSD_GROUPS, N, GW), F32)],
        scratch_shapes=[pltpu.VMEM((N, GW), F32)],
        compiler_params=_cp("arbitrary", "arbitrary"),
    )(*args)


def ssd_scan_bwd(act, dt, alog, dtr, alr, states, dy, prev_x, *, reverse, prev=None, comm=None):
    S = act.shape[0]
    Q, N, P = SSD_CHUNK, SSD_STATE, SSD_HEADDIM
    nc = S // Q
    GW = SSD_HPG * P

    def cidx(i):
        return i if reverse else (nc - 1 - i)

    def body(*refs):
        x_ref, b_ref, c_ref, dt_ref, al_ref, dtr_ref, alr_ref, st_ref, dy_ref, px_ref = refs[:10]
        pos = 10
        if prev is not None:
            pb_ref, pc_ref, pdt_ref, pal_ref = refs[pos:pos + 4]
            pos += 4
        dx_ref, db_ref, dc_ref, ddt_ref, dal_ref, ddtr_ref, dalr_ref, dstate = refs[pos:]
        i = pl.program_id(1)

        @pl.when(i == 0)
        def _():
            dstate[...] = jnp.zeros_like(dstate)

        _, vjp = jax.vjp(functools.partial(_ssd_chunk, reverse=reverse), st_ref[0, 0], x_ref[...], b_ref[...],
                         c_ref[...], dt_ref[0], al_ref[0], dtr_ref[...], alr_ref[...])
        dst, dx, dB, dC, ddt, dal, ddtr, dalr = vjp((dstate[...], dy_ref[...]))
        dstate[...] = dst
        dx_ref[...] = dx + px_ref[...]
        if prev is not None:
            dB = dB + pb_ref[...]
            dC = dC + pc_ref[...]
            ddt = ddt + pdt_ref[0]
        db_ref[...] = dB
        dc_ref[...] = dC
        ddt_ref[0] = ddt
        ddtr_ref[...] = ddtr

        @pl.when(i == 0)
        def _():
            dal_ref[0] = dal + (pal_ref[0] if prev is not None else 0.0)
            dalr_ref[...] = dalr

        @pl.when(i != 0)
        def _():
            dal_ref[0] += dal
            dalr_ref[...] += dalr

    xspec = pl.BlockSpec((Q, GW), lambda g, i: (cidx(i), g))
    gspec = pl.BlockSpec((Q, N), lambda g, i: (cidx(i), g))
    dtspec = pl.BlockSpec((1, Q, 128), lambda g, i: (g, cidx(i), 0))
    alspec = pl.BlockSpec((1, 1, 128), lambda g, i: (g, 0, 0))
    alrspec = pl.BlockSpec((1, GW), lambda g, i: (0, g))
    in_specs = [xspec,
                pl.BlockSpec((Q, N), lambda g, i: (cidx(i), SSD_DI // N + g)),
                pl.BlockSpec((Q, N), lambda g, i: (cidx(i), SSD_DI // N + SSD_GROUPS + g)),
                dtspec, alspec, xspec, alrspec,
                pl.BlockSpec((1, 1, N, GW), lambda g, i: (cidx(i), g, 0, 0)), xspec, xspec]
    args = [act, act, act, dt, alog, dtr, alr, states, dy, prev_x]
    if prev is not None:
        in_specs += [gspec, gspec, dtspec, alspec]
        args += list(prev)
    return pcall(
        body, comm, name=f"ssd_scan_bwd_{int(reverse)}", grid=(SSD_GROUPS, nc), in_specs=in_specs,
        out_specs=[xspec, gspec, gspec, dtspec, alspec, xspec, alrspec],
        out_shape=[jax.ShapeDtypeStruct((S, SSD_DI), F32), jax.ShapeDtypeStruct((S, SSD_GROUPS * N), F32),
                   jax.ShapeDtypeStruct((S, SSD_GROUPS * N), F32),
                   jax.ShapeDtypeStruct((SSD_GROUPS, S, 128), F32), jax.ShapeDtypeStruct((SSD_GROUPS, 1, 128), F32),
                   jax.ShapeDtypeStruct((S, SSD_DI), F32), jax.ShapeDtypeStruct((1, SSD_DI), F32)],
        scratch_shapes=[pltpu.VMEM((N, GW), F32)],
        compiler_params=_cp("arbitrary", "arbitrary"),
    )(*args)


def _ssd_post_fn(y, xs, z, dexp, ng):
    t = (y + xs * dexp) * _silu(z)
    return (_rms(t, ng),)


def _cumsum_rows_impl(x, reverse):
    n = x.shape[0]
    row = lax.broadcasted_iota(jnp.int32, x.shape, 0)
    k = 1
    while k < n:
        if reverse:
            x = x + jnp.where(row < n - k, pltpu.roll(x, n - k, 0), 0.0)
        else:
            x = x + jnp.where(row >= k, pltpu.roll(x, k, 0), 0.0)
        k *= 2
    return x


@functools.partial(jax.custom_vjp, nondiff_argnums=(1,))
def _cumsum_rows(x, reverse):
    return _cumsum_rows_impl(x, reverse)


_cumsum_rows.defvjp(lambda x, reverse: (_cumsum_rows_impl(x, reverse), None),
                    lambda reverse, _, ct: (_cumsum_rows_impl(ct, not reverse),))


def _hg_chunk(state, qraw, fraw, v, lb, *, reverse):
    C = HG_CHUNK
    r = lax.broadcasted_iota(jnp.int32, (C, C), 0)
    c = lax.broadcasted_iota(jnp.int32, (C, C), 1)
    keep = (c >= r) if reverse else (c <= r)
    q = _silu(qraw)
    f = lb + (1.0 - lb) * jax.nn.sigmoid(fraw)
    k = 1.0 - f
    g = jnp.log(f)
    G = _cumsum_rows(g, reverse)
    ref_row = C // 2 - 1 if reverse else C // 2
    last_row = 0 if reverse else C - 1
    Gr = G[ref_row:ref_row + 1, :]
    Gl = G[last_row:last_row + 1, :]
    q_t = q * jnp.exp(G - Gr)
    k_t = k * jnp.exp(Gr - G)
    att = jnp.where(keep, _dot(q_t, k_t, ((1,), (1,))), 0.0)
    o = _dot(att, v, ((1,), (0,))) + _dot(q * jnp.exp(G), state, ((1,), (0,)))
    kd = k * jnp.exp(Gl - G)
    new_state = jnp.transpose(jnp.exp(Gl)) * state + _dot(kd, v, ((0,), (0,)))
    return new_state, o


def hg_scan_fwd(u, lb, *, reverse, o_prev=None, rows=256, comm=None):
    S = u.shape[0]
    nh = HG_HEADS
    rows = min(rows, S)
    nsteps = S // rows
    ncb = rows // HG_CHUNK
    f_sec = 2 if reverse else 1

    def blk(i):
        return (nsteps - 1 - i) if reverse else i

    def body(*refs):
        if o_prev is None:
            q_ref, f_ref, v_ref, lb_ref, o_ref, st_ref, state = refs
            op_ref = None
        else:
            q_ref, f_ref, v_ref, lb_ref, op_ref, o_ref, st_ref, state = refs
        i = pl.program_id(0)

        @pl.when(i == 0)
        def _():
            state[...] = jnp.zeros_like(state)

        def chunk(cc, carry):
            ci = (ncb - 1 - cc) if reverse else cc
            sl = pl.ds(pl.multiple_of(ci * HG_CHUNK, HG_CHUNK), HG_CHUNK)
            for h in range(nh):
                hs = slice(h * HG_D, (h + 1) * HG_D)
                st = state[h]
                st_ref[ci, h] = st
                ns, o = _hg_chunk(st, q_ref[sl, hs], f_ref[sl, hs], v_ref[sl, hs], lb_ref[:, hs], reverse=reverse)
                state[h] = ns
                if op_ref is not None:
                    o = o + op_ref[sl, hs]
                o_ref[sl, hs] = o
            return carry

        lax.fori_loop(0, ncb, chunk, 0)

    rowspec = lambda sec: pl.BlockSpec((rows, HG_W), lambda i: (blk(i), sec))
    in_specs = [rowspec(0), rowspec(f_sec), rowspec(3), pl.BlockSpec((1, HG_W), lambda i: (0, 0))]
    args = [u, u, u, lb]
    if o_prev is not None:
        in_specs.append(rowspec(0))
        args.append(o_prev)
    return pcall(
        body, comm, name=f"hg_scan_fwd_{int(reverse)}", grid=(nsteps,), in_specs=in_specs,
        out_specs=[rowspec(0), pl.BlockSpec((ncb, nh, HG_D, HG_D), lambda i: (blk(i), 0, 0, 0))],
        out_shape=[jax.ShapeDtypeStruct((S, HG_W), F32), jax.ShapeDtypeStruct((S // HG_CHUNK, nh, HG_D, HG_D), F32)],
        scratch_shapes=[pltpu.VMEM((nh, HG_D, HG_D), F32)],
        compiler_params=_cp("arbitrary"),
    )(*args)


def hg_scan_bwd(u, lb, states, do, *, reverse, prev=None, rows=256, comm=None):
    S = u.shape[0]
    nh = HG_HEADS
    rows = min(rows, S)
    nsteps = S // rows
    ncb = rows // HG_CHUNK
    f_sec = 2 if reverse else 1

    def blk(i):
        return i if reverse else (nsteps - 1 - i)

    def body(*refs):
        q_ref, f_ref, v_ref, lb_ref, st_ref, do_ref = refs[:6]
        pos = 6
        if prev is not None:
            pq_ref, pv_ref, plb_ref = refs[pos:pos + 3]
            pos += 3
        dq_ref, df_ref, dv_ref, dlb_ref, dstate = refs[pos:]
        i = pl.program_id(0)

        @pl.when(i == 0)
        def _():
            dstate[...] = jnp.zeros_like(dstate)
            dlb_ref[...] = plb_ref[...] if prev is not None else jnp.zeros_like(dlb_ref)

        def chunk(cc, carry):
            ci = cc if reverse else (ncb - 1 - cc)
            sl = pl.ds(pl.multiple_of(ci * HG_CHUNK, HG_CHUNK), HG_CHUNK)
            for h in range(nh):
                hs = slice(h * HG_D, (h + 1) * HG_D)
                _, vjp = jax.vjp(functools.partial(_hg_chunk, reverse=reverse), st_ref[ci, h],
                                 q_ref[sl, hs], f_ref[sl, hs], v_ref[sl, hs], lb_ref[:, hs])
                dst, dq, df, dv, dlb = vjp((dstate[h], do_ref[sl, hs]))
                dstate[h] = dst
                if prev is not None:
                    dq = dq + pq_ref[sl, hs]
                    dv = dv + pv_ref[sl, hs]
                dq_ref[sl, hs] = dq
                df_ref[sl, hs] = df
                dv_ref[sl, hs] = dv
                dlb_ref[:, hs] += dlb
            return carry

        lax.fori_loop(0, ncb, chunk, 0)

    rowspec = lambda sec: pl.BlockSpec((rows, HG_W), lambda i: (blk(i), sec))
    lbspec = pl.BlockSpec((1, HG_W), lambda i: (0, 0))
    in_specs = [rowspec(0), rowspec(f_sec), rowspec(3), lbspec,
                pl.BlockSpec((ncb, nh, HG_D, HG_D), lambda i: (blk(i), 0, 0, 0)), rowspec(0)]
    args = [u, u, u, lb, states, do]
    if prev is not None:
        in_specs += [rowspec(0), rowspec(0), lbspec]
        args += list(prev)
    return pcall(
        body, comm, name=f"hg_scan_bwd_{int(reverse)}", grid=(nsteps,), in_specs=in_specs,
        out_specs=[rowspec(0), rowspec(0), rowspec(0), lbspec],
        out_shape=[jax.ShapeDtypeStruct((S, HG_W), F32)] * 3 + [jax.ShapeDtypeStruct((1, HG_W), F32)],
        scratch_shapes=[pltpu.VMEM((nh, HG_D, HG_D), F32)],
        compiler_params=_cp("arbitrary"),
    )(*args)


def _hg_lb_fn(lbp):
    m = jnp.max(lbp, axis=0, keepdims=True)
    e = jnp.exp(lbp - m)
    sm = e / jnp.sum(e, axis=0, keepdims=True)
    return ((sm[0:1] + sm[1:2]) - sm[0:1],)


def hg_lb_fwd(lbp):
    def body(x_ref, o_ref):
        o_ref[...] = _hg_lb_fn(x_ref[...])[0]

    return pl.pallas_call(body, name="hg_lb_fwd", out_shape=jax.ShapeDtypeStruct((1, HG_W), F32))(lbp)


def hg_lb_bwd(lbp, dlb):
    def body(x_ref, d_ref, o_ref):
        _, vjp = jax.vjp(_hg_lb_fn, x_ref[...])
        o_ref[...] = vjp((d_ref[...],))[0]

    return pl.pallas_call(body, name="hg_lb_bwd", out_shape=jax.ShapeDtypeStruct(lbp.shape, F32))(lbp, dlb)


def _hg_post_fn(o, gate, ng):
    return (_rms(o, ng) * _silu(gate),)


def _gate_fn(o, gate):
    return (o * _silu(gate),)


def _rope_tables(S):
    t = np.arange(S)
    row = (t // GRID_W).astype(np.float32)
    col = (t % GRID_W).astype(np.float32)
    half = AT_HD // 4
    inv = (ROPE_THETA ** (-np.arange(0, 2 * half, 2, dtype=np.float32) / np.float32(2 * half))).astype(np.float32)
    ar = row[:, None] * inv[None, :]
    ac = col[:, None] * inv[None, :]
    return ar.astype(np.float32), ac.astype(np.float32)


def _rope_swap_matrix():
    p = np.zeros((AT_HD, AT_HD), np.float32)
    for i in range(AT_HD):
        p[(i + 32) if (i % 64) < 32 else (i - 32), i] = 1.0
    return p


@jax.custom_vjp
def _half_swap(x):
    ax = x.ndim - 1
    lane = lax.broadcasted_iota(jnp.int32, x.shape, ax)
    return jnp.where((lane & 32) == 0, pltpu.roll(x, 96, ax), pltpu.roll(x, 32, ax))


_half_swap.defvjp(lambda x: (_half_swap(x), None), lambda _, ct: (_half_swap(ct),))


def _make_qk_fn(scale):
    def fn(x, ct, st, g):
        n = _rms(x, g)
        return ((n * ct + _half_swap(n) * st) * scale,)
    return fn


def flash_fwd(q, k, v, *, v_col0=0, tq=512, tk=512):
    S = q.shape[0]
    tq, tk = min(tq, S), min(tk, S)
    nk = S // tk
    G = AT_HEADS // AT_KV

    def body(q_ref, k_ref, v_ref, o_ref, lse_ref, m_s, l_s, acc):
        ki = pl.program_id(2)

        @pl.when(ki == 0)
        def _():
            m_s[...] = jnp.full_like(m_s, -jnp.inf)
            l_s[...] = jnp.zeros_like(l_s)
            acc[...] = jnp.zeros_like(acc)

        kv, vv = k_ref[...], v_ref[...]
        for g in range(G):
            s = _dot(q_ref[:, g * AT_HD:(g + 1) * AT_HD], kv, ((1,), (1,)))
            m_old = m_s[g]
            m_new = jnp.maximum(m_old, jnp.max(s, axis=1, keepdims=True))
            alpha = jnp.exp(m_old - m_new)
            p = jnp.exp(s - m_new)
            l_s[g] = alpha * l_s[g] + jnp.sum(p, axis=1, keepdims=True)
            acc[g] = alpha * acc[g] + _dot(p, vv, ((1,), (0,)))
            m_s[g] = m_new

        @pl.when(ki == nk - 1)
        def _():
            for g in range(G):
                o_ref[:, g * AT_HD:(g + 1) * AT_HD] = acc[g] / l_s[g]
                lse_ref[0, :, g:g + 1] = m_s[g] + jnp.log(l_s[g])

    return pl.pallas_call(
        body, name="flash_fwd", grid=(AT_KV, S // tq, nk),
        in_specs=[pl.BlockSpec((tq, G * AT_HD), lambda h, i, j: (i, h)),
                  pl.BlockSpec((tk, AT_HD), lambda h, i, j: (j, h)),
                  pl.BlockSpec((tk, AT_HD), lambda h, i, j: (j, v_col0 + h))],
        out_specs=[pl.BlockSpec((tq, G * AT_HD), lambda h, i, j: (i, h)),
                   pl.BlockSpec((1, tq, G), lambda h, i, j: (h, i, 0))],
        out_shape=[jax.ShapeDtypeStruct((S, AT_QW), F32), jax.ShapeDtypeStruct((AT_KV, S, G), F32)],
        scratch_shapes=[pltpu.VMEM((G, tq, 1), F32), pltpu.VMEM((G, tq, 1), F32), pltpu.VMEM((G, tq, AT_HD), F32)],
        compiler_params=_cp("parallel", "parallel", "arbitrary"),
    )(q, k, v)


def flash_bwd_dq(q, k, v, o, lse, do, *, v_col0=0, tq=512, tk=512):
    S = q.shape[0]
    tq, tk = min(tq, S), min(tk, S)
    nk = S // tk
    G = AT_HEADS // AT_KV

    def body(q_ref, k_ref, v_ref, o_ref, lse_ref, do_ref, dq_ref, dl_ref, acc, dl_s):
        ki = pl.program_id(2)

        @pl.when(ki == 0)
        def _():
            acc[...] = jnp.zeros_like(acc)
            for g in range(G):
                sl = slice(g * AT_HD, (g + 1) * AT_HD)
                dl_s[g] = jnp.sum(do_ref[:, sl] * o_ref[:, sl], axis=1, keepdims=True)

        kv, vv = k_ref[...], v_ref[...]
        for g in range(G):
            sl = slice(g * AT_HD, (g + 1) * AT_HD)
            s = _dot(q_ref[:, sl], kv, ((1,), (1,)))
            p = jnp.exp(s - lse_ref[0, :, g:g + 1])
            dp = _dot(do_ref[:, sl], vv, ((1,), (1,)))
            ds = p * (dp - dl_s[g])
            acc[g] += _dot(ds, kv, ((1,), (0,)))

        @pl.when(ki == nk - 1)
        def _():
            for g in range(G):
                dq_ref[:, g * AT_HD:(g + 1) * AT_HD] = acc[g]
                dl_ref[0, :, g:g + 1] = dl_s[g]

    qspec = pl.BlockSpec((tq, G * AT_HD), lambda h, i, j: (i, h))
    kspec = pl.BlockSpec((tk, AT_HD), lambda h, i, j: (j, h))
    lspec = pl.BlockSpec((1, tq, G), lambda h, i, j: (h, i, 0))
    return pl.pallas_call(
        body, name="flash_bwd_dq", grid=(AT_KV, S // tq, nk),
        in_specs=[qspec, kspec, pl.BlockSpec((tk, AT_HD), lambda h, i, j: (j, v_col0 + h)), qspec, lspec, qspec],
        out_specs=[qspec, lspec],
        out_shape=[jax.ShapeDtypeStruct((S, AT_QW), F32), jax.ShapeDtypeStruct((AT_KV, S, G), F32)],
        scratch_shapes=[pltpu.VMEM((G, tq, AT_HD), F32), pltpu.VMEM((G, tq, 1), F32)],
        compiler_params=_cp("parallel", "parallel", "arbitrary"),
    )(q, k, v, o, lse, do)


def flash_bwd_dkv(q, k, v, lse, delta, do, *, v_col0=0, tq=512, tk=512):
    S = q.shape[0]
    tq, tk = min(tq, S), min(tk, S)
    nq = S // tq
    G = AT_HEADS // AT_KV

    def body(q_ref, k_ref, v_ref, lse_ref, dl_ref, do_ref, dk_ref, dv_ref, dk_acc, dv_acc):
        qi = pl.program_id(2)

        @pl.when(qi == 0)
        def _():
            dk_acc[...] = jnp.zeros_like(dk_acc)
            dv_acc[...] = jnp.zeros_like(dv_acc)

        kv, vv = k_ref[...], v_ref[...]
        for g in range(G):
            sl = slice(g * AT_HD, (g + 1) * AT_HD)
            qg, dog = q_ref[:, sl], do_ref[:, sl]
            s = _dot(qg, kv, ((1,), (1,)))
            p = jnp.exp(s - lse_ref[0, :, g:g + 1])
            dv_acc[...] += _dot(p, dog, ((0,), (0,)))
            dp = _dot(dog, vv, ((1,), (1,)))
            ds = p * (dp - dl_ref[0, :, g:g + 1])
            dk_acc[...] += _dot(ds, qg, ((0,), (0,)))

        @pl.when(qi == nq - 1)
        def _():
            dk_ref[...] = dk_acc[...]
            dv_ref[...] = dv_acc[...]

    qspec = pl.BlockSpec((tq, G * AT_HD), lambda h, j, i: (i, h))
    kspec = pl.BlockSpec((tk, AT_HD), lambda h, j, i: (j, h))
    lspec = pl.BlockSpec((1, tq, G), lambda h, j, i: (h, i, 0))
    return pl.pallas_call(
        body, name="flash_bwd_dkv", grid=(AT_KV, S // tk, nq),
        in_specs=[qspec, kspec, pl.BlockSpec((tk, AT_HD), lambda h, j, i: (j, v_col0 + h)), lspec, lspec, qspec],
        out_specs=[kspec, kspec],
        out_shape=[jax.ShapeDtypeStruct((S, AT_KW), F32), jax.ShapeDtypeStruct((S, AT_KW), F32)],
        scratch_shapes=[pltpu.VMEM((tk, AT_HD), F32), pltpu.VMEM((tk, AT_HD), F32)],
        compiler_params=_cp("parallel", "parallel", "arbitrary"),
    )(q, k, v, lse, delta, do)


def flash_fwd(q, k, v, *, v_col0=0, tq=256, comm=None):
    S = q.shape[0]
    tq = min(tq, S)
    G = AT_HEADS // AT_KV

    def body(q_ref, k_ref, v_ref, o_ref, lse_ref):
        kv, vv = k_ref[...], v_ref[...]
        for g in range(G):
            sl = slice(g * AT_HD, (g + 1) * AT_HD)
            s = _dot(q_ref[:, sl], kv, ((1,), (1,)))
            m = jnp.max(s, axis=1, keepdims=True)
            p = jnp.exp(s - m)
            l = jnp.sum(p, axis=1, keepdims=True)
            o_ref[:, sl] = _dot(p, vv, ((1,), (0,))) / l
            lse_ref[0, :, g:g + 1] = m + jnp.log(l)

    return pcall(
        body, comm, name="flash_fwd", grid=(AT_KV, S // tq),
        in_specs=[pl.BlockSpec((tq, G * AT_HD), lambda h, i: (i, h)),
                  pl.BlockSpec((S, AT_HD), lambda h, i: (0, h)),
                  pl.BlockSpec((S, AT_HD), lambda h, i: (0, v_col0 + h))],
        out_specs=[pl.BlockSpec((tq, G * AT_HD), lambda h, i: (i, h)),
                   pl.BlockSpec((1, tq, G), lambda h, i: (h, i, 0))],
        out_shape=[jax.ShapeDtypeStruct((S, AT_QW), F32), jax.ShapeDtypeStruct((AT_KV, S, G), F32)],
        compiler_params=_cp("parallel", "arbitrary"),
    )(q, k, v)


def flash_bwd_dq(q, k, v, o, lse, do, *, v_col0=0, tq=256):
    S = q.shape[0]
    tq = min(tq, S)
    G = AT_HEADS // AT_KV

    def body(q_ref, k_ref, v_ref, o_ref, lse_ref, do_ref, dq_ref, dl_ref):
        kv, vv = k_ref[...], v_ref[...]
        for g in range(G):
            sl = slice(g * AT_HD, (g + 1) * AT_HD)
            dog = do_ref[:, sl]
            delta = jnp.sum(dog * o_ref[:, sl], axis=1, keepdims=True)
            s = _dot(q_ref[:, sl], kv, ((1,), (1,)))
            p = jnp.exp(s - lse_ref[0, :, g:g + 1])
            dp = _dot(dog, vv, ((1,), (1,)))
            ds = p * (dp - delta)
            dq_ref[:, sl] = _dot(ds, kv, ((1,), (0,)))
            dl_ref[0, :, g:g + 1] = delta

    qspec = pl.BlockSpec((tq, G * AT_HD), lambda h, i: (i, h))
    lspec = pl.BlockSpec((1, tq, G), lambda h, i: (h, i, 0))
    return pl.pallas_call(
        body, name="flash_bwd_dq", grid=(AT_KV, S // tq),
        in_specs=[qspec, pl.BlockSpec((S, AT_HD), lambda h, i: (0, h)),
                  pl.BlockSpec((S, AT_HD), lambda h, i: (0, v_col0 + h)), qspec, lspec, qspec],
        out_specs=[qspec, lspec],
        out_shape=[jax.ShapeDtypeStruct((S, AT_QW), F32), jax.ShapeDtypeStruct((AT_KV, S, G), F32)],
        compiler_params=_cp("parallel", "arbitrary"),
    )(q, k, v, o, lse, do)


def flash_bwd_dkv(q, k, v, lse, delta, do, *, v_col0=0, tk=512, comm=None):
    S = q.shape[0]
    tk = min(tk, S)
    G = AT_HEADS // AT_KV

    def body(q_ref, k_ref, v_ref, lse_ref, dl_ref, do_ref, dk_ref, dv_ref):
        kv, vv = k_ref[...], v_ref[...]
        dk = jnp.zeros((tk, AT_HD), F32)
        dv = jnp.zeros((tk, AT_HD), F32)
        for g in range(G):
            sl = slice(g * AT_HD, (g + 1) * AT_HD)
            qg, dog = q_ref[:, sl], do_ref[:, sl]
            s = _dot(qg, kv, ((1,), (1,)))
            p = jnp.exp(s - lse_ref[0, :, g:g + 1])
            dv = dv + _dot(p, dog, ((0,), (0,)))
            dp = _dot(dog, vv, ((1,), (1,)))
            ds = p * (dp - dl_ref[0, :, g:g + 1])
            dk = dk + _dot(ds, qg, ((0,), (0,)))
        dk_ref[...] = dk
        dv_ref[...] = dv

    qspec = pl.BlockSpec((S, G * AT_HD), lambda h, j: (0, h))
    kspec = pl.BlockSpec((tk, AT_HD), lambda h, j: (j, h))
    lspec = pl.BlockSpec((1, S, G), lambda h, j: (h, 0, 0))
    return pcall(
        body, comm, name="flash_bwd_dkv", grid=(AT_KV, S // tk),
        in_specs=[qspec, kspec, pl.BlockSpec((tk, AT_HD), lambda h, j: (j, v_col0 + h)), lspec, lspec, qspec],
        out_specs=[kspec, kspec],
        out_shape=[jax.ShapeDtypeStruct((S, AT_KW), F32), jax.ShapeDtypeStruct((S, AT_KW), F32)],
        compiler_params=_cp("parallel", "arbitrary"),
    )(q, k, v, lse, delta, do)


def _t5_bucket_np(rel):
    half = REL_BUCKETS // 2
    exact = half // 2
    n = np.abs(rel)
    large = exact + (np.log(np.maximum(n, 1).astype(np.float32) / np.float32(exact))
                     / np.float32(math.log(REL_MAX_DIST / exact)) * np.float32(half - exact)).astype(np.int32)
    large = np.minimum(large, half - 1)
    return np.where(rel > 0, half, 0) + np.where(n < exact, n, large)


def _dl_tiles(Ls):
    T = min(128, Ls)
    return T, T + 2 * DL_HALF


def _dl_bucket_tables(dil, T):
    W = T + 2 * DL_HALF
    i = np.arange(T)[:, None]
    j = np.arange(W)[None, :]
    bq = _t5_bucket_np((j - DL_HALF - i) * dil)
    iw = np.arange(W)[:, None]
    jk = np.arange(T)[None, :]
    bk = _t5_bucket_np((jk + DL_HALF - iw) * dil)
    return bq.astype(np.int32), bk.astype(np.int32)


def band_fwd(q, kp, vp, bias, *, scale):
    H, dil, Ls, E = q.shape
    T, W = _dl_tiles(Ls)

    def body(q_ref, k_ref, v_ref, b_ref, o_ref, lse_ref):
        n = pl.program_id(2)
        r0 = pl.multiple_of(n * T, T)
        kw = k_ref[0, 0, pl.ds(r0, W), :]
        vw = v_ref[0, 0, pl.ds(r0, W), :]
        i = lax.broadcasted_iota(jnp.int32, (T, W), 0)
        j = lax.broadcasted_iota(jnp.int32, (T, W), 1)
        kpos = n * T + j - DL_HALF
        mask = (jnp.abs(j - DL_HALF - i) <= DL_HALF) & (kpos >= 0) & (kpos < Ls)
        s = _dot(q_ref[0, 0], kw, ((1,), (1,))) * scale + b_ref[0]
        s = jnp.where(mask, s, NEG_BIG)
        m = jnp.max(s, axis=1, keepdims=True)
        lse = m + jnp.log(jnp.sum(jnp.exp(s - m), axis=1, keepdims=True))
        p = jnp.exp(s - lse)
        o_ref[0, 0] = _dot(p, vw, ((1,), (0,)))
        lse_ref[0, 0] = lse

    return pl.pallas_call(
        body, name=f"band_fwd_{dil}", grid=(H, dil, Ls // T),
        in_specs=[pl.BlockSpec((1, 1, T, E), lambda h, d, n: (h, d, n, 0)),
                  pl.BlockSpec((1, 1, Ls + 2 * DL_HALF, E), lambda h, d, n: (h, d, 0, 0)),
                  pl.BlockSpec((1, 1, Ls + 2 * DL_HALF, E), lambda h, d, n: (h, d, 0, 0)),
                  pl.BlockSpec((1, T, W), lambda h, d, n: (h, 0, 0))],
        out_specs=[pl.BlockSpec((1, 1, T, E), lambda h, d, n: (h, d, n, 0)),
                   pl.BlockSpec((1, 1, T, 1), lambda h, d, n: (h, d, n, 0))],
        out_shape=[jax.ShapeDtypeStruct((H, dil, Ls, E), F32), jax.ShapeDtypeStruct((H, dil, Ls, 1), F32)],
        compiler_params=_cp("parallel", "parallel", "arbitrary"),
    )(q, kp, vp, bias)


def band_bwd_dq(q, kp, vp, bias, lse, dm, do, *, scale):
    H, dil, Ls, E = q.shape
    T, W = _dl_tiles(Ls)

    def body(q_ref, k_ref, v_ref, b_ref, lse_ref, dm_ref, do_ref, dq_ref, db_ref):
        d, n = pl.program_id(1), pl.program_id(2)
        r0 = pl.multiple_of(n * T, T)
        kw = k_ref[0, 0, pl.ds(r0, W), :]
        vw = v_ref[0, 0, pl.ds(r0, W), :]
        i = lax.broadcasted_iota(jnp.int32, (T, W), 0)
        j = lax.broadcasted_iota(jnp.int32, (T, W), 1)
        kpos = n * T + j - DL_HALF
        mask = (jnp.abs(j - DL_HALF - i) <= DL_HALF) & (kpos >= 0) & (kpos < Ls)
        s = _dot(q_ref[0, 0], kw, ((1,), (1,))) * scale + b_ref[0]
        p = jnp.where(mask, jnp.exp(jnp.where(mask, s, 0.0) - lse_ref[0, 0]), 0.0)
        dp = _dot(do_ref[0, 0], vw, ((1,), (1,)))
        ds = p * (dp - dm_ref[0, 0])
        dq_ref[0, 0] = _dot(ds, kw, ((1,), (0,))) * scale
        first = jnp.logical_and(d == 0, n == 0)

        @pl.when(first)
        def _():
            db_ref[0] = ds

        @pl.when(jnp.logical_not(first))
        def _():
            db_ref[0] += ds

    qspec = pl.BlockSpec((1, 1, T, E), lambda h, d, n: (h, d, n, 0))
    kspec = pl.BlockSpec((1, 1, Ls + 2 * DL_HALF, E), lambda h, d, n: (h, d, 0, 0))
    rspec = pl.BlockSpec((1, 1, T, 1), lambda h, d, n: (h, d, n, 0))
    bspec = pl.BlockSpec((1, T, W), lambda h, d, n: (h, 0, 0))
    return pl.pallas_call(
        body, name=f"band_bwd_dq_{dil}", grid=(H, dil, Ls // T),
        in_specs=[qspec, kspec, kspec, bspec, rspec, rspec, qspec],
        out_specs=[qspec, bspec],
        out_shape=[jax.ShapeDtypeStruct((H, dil, Ls, E), F32), jax.ShapeDtypeStruct((H, T, W), F32)],
        compiler_params=_cp("arbitrary", "arbitrary", "arbitrary"),
    )(q, kp, vp, bias, lse, dm, do)


def band_bwd_dkv(qp, k, v, bias_t, lsep, dmp, dop, *, scale):
    H, dil, Ls, E = k.shape
    T, W = _dl_tiles(Ls)

    def body(q_ref, k_ref, v_ref, b_ref, lse_ref, dm_ref, do_ref, dk_ref, dv_ref):
        n = pl.program_id(2)
        r0 = pl.multiple_of(n * T, T)
        qw = q_ref[0, 0, pl.ds(r0, W), :]
        dow = do_ref[0, 0, pl.ds(r0, W), :]
        lsew = lse_ref[0, 0, pl.ds(r0, W), :]
        dmw = dm_ref[0, 0, pl.ds(r0, W), :]
        iw = lax.broadcasted_iota(jnp.int32, (W, T), 0)
        j = lax.broadcasted_iota(jnp.int32, (W, T), 1)
        qpos = n * T + iw - DL_HALF
        mask = (jnp.abs(j + DL_HALF - iw) <= DL_HALF) & (qpos >= 0) & (qpos < Ls)
        s = _dot(qw, k_ref[0, 0], ((1,), (1,))) * scale + b_ref[0]
        p = jnp.where(mask, jnp.exp(jnp.where(mask, s, 0.0) - lsew), 0.0)
        dv_ref[0, 0] = _dot(p, dow, ((0,), (0,)))
        dp = _dot(dow, v_ref[0, 0], ((1,), (1,)))
        ds = p * (dp - dmw)
        dk_ref[0, 0] = _dot(ds, qw, ((0,), (0,))) * scale

    kspec = pl.BlockSpec((1, 1, T, E), lambda h, d, n: (h, d, n, 0))
    wspec = pl.BlockSpec((1, 1, Ls + 2 * DL_HALF, E), lambda h, d, n: (h, d, 0, 0))
    w1spec = pl.BlockSpec((1, 1, Ls + 2 * DL_HALF, 1), lambda h, d, n: (h, d, 0, 0))
    return pl.pallas_call(
        body, name=f"band_bwd_dkv_{dil}", grid=(H, dil, Ls // T),
        in_specs=[wspec, kspec, kspec, pl.BlockSpec((1, W, T), lambda h, d, n: (h, 0, 0)), w1spec, w1spec, wspec],
        out_specs=[kspec, kspec],
        out_shape=[jax.ShapeDtypeStruct((H, dil, Ls, E), F32), jax.ShapeDtypeStruct((H, dil, Ls, E), F32)],
        compiler_params=_cp("parallel", "parallel", "arbitrary"),
    )(qp, k, v, bias_t, lsep, dmp, dop)


def _dl_merge_fn(o0, o1, o2, l0, l1, l2):
    m = jnp.maximum(jnp.maximum(l0, l1), l2)
    e0, e1, e2 = jnp.exp(l0 - m), jnp.exp(l1 - m), jnp.exp(l2 - m)
    den = e0 + e1 + e2
    return ((e0 / den) * o0 + (e1 / den) * o1 + (e2 / den) * o2,)


def _adamw_math(w, g, m, v):
    m = ADAM_B1 * m + (1.0 - ADAM_B1) * g
    v = ADAM_B2 * v + (1.0 - ADAM_B2) * (g * g)
    m_hat = m / (1.0 - ADAM_B1 ** ADAM_STEP)
    v_hat = v / (1.0 - ADAM_B2 ** ADAM_STEP)
    delta = -ADAM_LR * (m_hat / (jnp.sqrt(v_hat) + ADAM_EPS) + ADAM_WD * w)
    return delta, m, v


def adamw_sum(parts, w, m, v, *, name, R=128):
    rows, cols = w.shape
    R = min(R, rows)
    if rows % R:
        R = rows

    def body(p_ref, w_ref, m_ref, v_ref, g_ref, d_ref, nm_ref, nv_ref):
        g = p_ref[0].astype(F32)
        for s in range(1, N_DEV):
            g = g + p_ref[s].astype(F32)
        d, nm, nv = _adamw_math(w_ref[...], g, m_ref[...], v_ref[...])
        g_ref[...] = g
        d_ref[...] = d
        nm_ref[...] = nm
        nv_ref[...] = nv

    spec = pl.BlockSpec((R, cols), lambda i: (i, 0))
    return pl.pallas_call(
        body, name=name, grid=(rows // R,),
        in_specs=[pl.BlockSpec((N_DEV, R, cols), lambda i: (0, i, 0)), spec, spec, spec],
        out_specs=[spec] * 4, out_shape=[jax.ShapeDtypeStruct((rows, cols), F32)] * 4,
        compiler_params=_cp("parallel"),
    )(parts, w, m, v)


def sum_parts(parts, *, name):
    rows, cols = parts.shape[1:]

    def body(p_ref, o_ref):
        g = p_ref[0]
        for s in range(1, N_DEV):
            g = g + p_ref[s]
        o_ref[...] = g

    return pl.pallas_call(body, name=name, out_shape=jax.ShapeDtypeStruct((rows, cols), F32))(parts)


def adamw_plain(w, g, m, v, *, name):
    def body(w_ref, g_ref, m_ref, v_ref, d_ref, nm_ref, nv_ref):
        d, nm, nv = _adamw_math(w_ref[...], g_ref[...], m_ref[...], v_ref[...])
        d_ref[...] = d
        nm_ref[...] = nm
        nv_ref[...] = nv

    return pl.pallas_call(body, name=name, out_shape=[jax.ShapeDtypeStruct(w.shape, F32)] * 3)(w, g, m, v)


def _my_pos():
    return lax.axis_index("x"), lax.axis_index("y"), lax.axis_index("c")


def _flat(px, py, pc):
    return 4 * px + 2 * py + pc


def allgather_two_level(x, *, name):
    R, C = x.shape

    def body(x_ref, out_ref, send_sems, recv_sems, local_sem):
        x_, y_, c_ = _my_pos()
        me, sibling = (x_, y_, c_), (x_, y_, 1 - c_)
        chips = [(1 - x_, y_), (x_, 1 - y_), (1 - x_, 1 - y_)]

        def rows(p):
            return out_ref.at[_flat(*p)]

        def copy(k, block, to, src=None):
            return pltpu.make_async_remote_copy(
                src_ref=rows(block) if src is None else src, dst_ref=rows(block),
                send_sem=send_sems.at[k], recv_sem=recv_sems.at[k], device_id=to, device_id_type=MESH_ID)

        mine = pltpu.make_async_copy(x_ref, rows(me), local_sem)
        mine.start()
        first = [copy(0, me, sibling, src=x_ref)]
        first += [copy(1 + j, me, (*chip, c_), src=x_ref) for j, chip in enumerate(chips)]
        for cp in first:
            cp.start()
        passed = [copy(4 + j, (*chip, c_), sibling) for j, chip in enumerate(chips)]
        for j, chip in enumerate(chips):
            copy(1 + j, (*chip, c_), me).wait_recv()
            passed[j].start()
        copy(0, sibling, me).wait_recv()
        for j, chip in enumerate(chips):
            copy(4 + j, (*chip, 1 - c_), me).wait_recv()
        for cp in first + passed:
            cp.wait_send()
        mine.wait()

    return pl.pallas_call(
        body, name=name,
        out_shape=jax.ShapeDtypeStruct((N_DEV, R, C), x.dtype),
        in_specs=[pl.BlockSpec(memory_space=pl.ANY)],
        out_specs=pl.BlockSpec(memory_space=pl.ANY),
        scratch_shapes=[pltpu.SemaphoreType.DMA((7,)), pltpu.SemaphoreType.DMA((7,)), pltpu.SemaphoreType.DMA],
    )(x)


def allgather_two_level_multi(xs, *, name):
    nb = len(xs)

    def body(*refs):
        x_refs, out_refs = refs[:nb], refs[nb:2 * nb]
        send_sems, recv_sems, local_sems = refs[2 * nb:]
        x_, y_, c_ = _my_pos()
        me, sibling = (x_, y_, c_), (x_, y_, 1 - c_)
        chips = [(1 - x_, y_), (x_, 1 - y_), (1 - x_, 1 - y_)]

        def copy(b, k, block, to, own=False):
            rows = out_refs[b].at[_flat(*block)]
            return pltpu.make_async_remote_copy(
                src_ref=x_refs[b] if own else rows, dst_ref=rows,
                send_sem=send_sems.at[b, k], recv_sem=recv_sems.at[b, k], device_id=to, device_id_type=MESH_ID)

        mine = [pltpu.make_async_copy(x_refs[b], out_refs[b].at[_flat(*me)], local_sems.at[b]) for b in range(nb)]
        first = []
        for b in range(nb):
            first.append(copy(b, 0, me, sibling, own=True))
            first += [copy(b, 1 + j, me, (*chip, c_), own=True) for j, chip in enumerate(chips)]
        for cp in mine + first:
            cp.start()
        passed = []
        for j, chip in enumerate(chips):
            for b in range(nb):
                copy(b, 1 + j, (*chip, c_), me).wait_recv()
                fwd = copy(b, 4 + j, (*chip, c_), sibling)
                fwd.start()
                passed.append(fwd)
        for b in range(nb):
            copy(b, 0, sibling, me).wait_recv()
            for j, chip in enumerate(chips):
                copy(b, 4 + j, (*chip, 1 - c_), me).wait_recv()
        for cp in first + passed:
            cp.wait_send()
        for cp in mine:
            cp.wait()

    anyspec = pl.BlockSpec(memory_space=pl.ANY)
    return pl.pallas_call(
        body, name=name,
        out_shape=[jax.ShapeDtypeStruct((N_DEV,) + x.shape, x.dtype) for x in xs],
        in_specs=[anyspec] * nb, out_specs=[anyspec] * nb,
        scratch_shapes=[pltpu.SemaphoreType.DMA((nb, 7)), pltpu.SemaphoreType.DMA((nb, 7)), pltpu.SemaphoreType.DMA((nb,))],
    )(*xs)


def all_to_all(bufs, *, name):
    nb = len(bufs)

    def body(*refs):
        in_refs = refs[:nb]
        out_refs = refs[nb:2 * nb]
        send_sems, recv_sems, local_sems = refs[2 * nb:]
        x_, y_, c_ = _my_pos()
        me = _flat(x_, y_, c_)
        peers = []
        for k in range(1, N_DEV):
            fx, fy, fc = (k >> 2) & 1, (k >> 1) & 1, k & 1
            peers.append(((1 - x_) if fx else x_, (1 - y_) if fy else y_, (1 - c_) if fc else c_))
        copies = []
        for b in range(nb):
            loc = pltpu.make_async_copy(in_refs[b].at[me], out_refs[b].at[me], local_sems.at[b])
            loc.start()
            copies.append(loc)
        remote = []
        for b in range(nb):
            for k, p in enumerate(peers):
                cp = pltpu.make_async_remote_copy(
                    src_ref=in_refs[b].at[_flat(*p)], dst_ref=out_refs[b].at[me],
                    send_sem=send_sems.at[b, k], recv_sem=recv_sems.at[b, k], device_id=p, device_id_type=MESH_ID)
                cp.start()
                remote.append((b, k, p))
        for b, k, p in remote:
            pltpu.make_async_remote_copy(
                src_ref=in_refs[b].at[me], dst_ref=out_refs[b].at[_flat(*p)],
                send_sem=send_sems.at[b, k], recv_sem=recv_sems.at[b, k], device_id=p, device_id_type=MESH_ID).wait()
        for loc in copies:
            loc.wait()

    return pl.pallas_call(
        body, name=name,
        out_shape=[jax.ShapeDtypeStruct(b.shape, b.dtype) for b in bufs],
        in_specs=[pl.BlockSpec(memory_space=pl.ANY)] * nb,
        out_specs=[pl.BlockSpec(memory_space=pl.ANY)] * nb,
        scratch_shapes=[pltpu.SemaphoreType.DMA((nb, 7)), pltpu.SemaphoreType.DMA((nb, 7)), pltpu.SemaphoreType.DMA((nb,))],
    )(*bufs)


def _prenorm(tag, x, ng):
    return rowwise_fwd(f"{tag}_prenorm", _prenorm_fn, [(x, 0, False)], [(ng, False)], [(D_MODEL, MXU_DTYPE)], W=D_MODEL)[0]


def _cat_mxu(parts):
    return jnp.concatenate([t.astype(MXU_DTYPE) for t in parts], axis=1)


def _in_out_bwd(tag, x, ng, hn, du, w_in, dx, tail_comm=None):
    dw_in = matmul(hn, du, ta=True, out_dtype=GRAD_WIRE_DTYPE, name=f"{tag}_dw_in")
    comm = None if tail_comm is None else tail_comm(dw_in)
    dhn, cres = _own(matmul(du, w_in, tb=True, name=f"{tag}_dhn", comm=comm), comm)
    dx_prev, dng = rowwise_bwd(f"{tag}_prenorm_bwd", _prenorm_fn, [(x, 0, False)], [(ng, False)], [dhn],
                               W=D_MODEL, diff_rows=[0], diff_shared=[0], add=dx)
    return (dx_prev, dng, dw_in) if tail_comm is None else (dx_prev, dng, dw_in, cres)


def _own(res, comm):
    return (res, None) if comm is None else res


def ssd_layer_fwd(x, ng, p, comm=None, comm1=None):
    hn = _prenorm("ssd", x, ng)
    u = matmul(hn, p["w_in"], name="ssd_in")
    act = ssd_conv_fwd(u, p["conv_w"], p["conv_b"])
    dt = rowwise_fwd("ssd_dt", _dt_fn, [(u, (SSD_DI + SSD_CONV_CH) // 128, False)], [(p["dt_bias"], False)],
                     [(128, F32)], W=128)[0]
    H = SSD_HEADS
    dtr = [jnp.repeat(dt[:, d * H:(d + 1) * H], SSD_HEADDIM, axis=1) for d in (0, 1)]
    alr = [jnp.repeat(p["alog"][:, d * H:(d + 1) * H], SSD_HEADDIM, axis=1) for d in (0, 1)]
    dt, alog = _ssd_group_layout(dt), _ssd_group_layout(p["alog"])
    (y0, st0), cres = _own(ssd_scan_fwd(act, dt, alog, dtr[0], alr[0], reverse=False, comm=comm), comm)
    (y, st1), cres1 = _own(ssd_scan_fwd(act, dt, alog, dtr[1], alr[1], reverse=True, y_prev=y0, comm=comm1), comm1)
    g = rowwise_fwd("ssd_post", _ssd_post_fn, [(y, 0, True), (act, 0, True), (u, 0, True)],
                    [(p["dexp"], True), (p["norm_g"], True)], [(512, MXU_DTYPE)], W=512, ncb=SSD_GROUPS)[0]
    xn = matmul(g, p["w_out"], residual=x, name="ssd_out")
    return xn, dict(x=x, ng=ng, hn=hn, u=u, act=act, dt=dt, alog=alog, dtr=dtr, alr=alr, y=y, st0=st0, st1=st1,
                    g=g), cres, cres1


def ssd_layer_bwd(sv, p, dx, comm=None, tail_comm=None):
    u, act, dt = sv["u"], sv["act"], sv["dt"]
    S = u.shape[0]
    dg = matmul(dx, p["w_out"], tb=True, name="ssd_dg")
    dw_out = matmul(sv["g"], dx, ta=True, out_dtype=GRAD_WIRE_DTYPE, name="ssd_dw_out")
    dy, dxs_skip, dz, ddexp, dnorm = rowwise_bwd(
        "ssd_post_bwd", _ssd_post_fn, [(sv["y"], 0, True), (act, 0, True), (u, 0, True)],
        [(p["dexp"], True), (p["norm_g"], True)], [dg], W=512, ncb=SSD_GROUPS, diff_rows=[0, 1, 2], diff_shared=[0, 1])
    dtr, alr = sv["dtr"], sv["alr"]
    (dxa, dB, dC, ddt, dal, ddtr0, dalr0), cres = _own(
        ssd_scan_bwd(act, dt, sv["alog"], dtr[0], alr[0], sv["st0"], dy, dxs_skip, reverse=False, comm=comm), comm)
    dxa, dB, dC, ddt, dal, ddtr1, dalr1 = ssd_scan_bwd(act, dt, sv["alog"], dtr[1], alr[1], sv["st1"], dy, dxa,
                                                       reverse=True, prev=(dB, dC, ddt, dal))
    dact = jnp.concatenate([dxa, dB, dC], axis=1)
    dxbc, dconv_w, dconv_b = ssd_conv_bwd(u, p["conv_w"], p["conv_b"], dact)
    fold = jnp.asarray(np.repeat(np.eye(SSD_HEADS, dtype=np.float32), SSD_HEADDIM, axis=0))
    folded = [matmul(t, fold, exact=True, name=f"ssd_ddt_fold_{d}", tn=SSD_HEADS) for d, t in enumerate((ddtr0, ddtr1))]
    ddt_all = _ssd_head_layout(ddt) + jnp.pad(jnp.concatenate(folded, axis=1), ((0, 0), (0, 128 - 2 * SSD_HEADS)))
    dal_rep = jnp.concatenate([t.reshape(SSD_HEADS, SSD_HEADDIM).sum(axis=1) for t in (dalr0, dalr1)])[None, :]
    ddt_raw, ddt_bias = rowwise_bwd("ssd_dt_bwd", _dt_fn, [(u, (SSD_DI + SSD_CONV_CH) // 128, False)],
                                    [(p["dt_bias"], False)], [ddt_all], W=128, diff_rows=[0], diff_shared=[0])
    du = _cat_mxu([dz, dxbc, ddt_raw, jnp.zeros((S, SSD_IN_PAD - SSD_IN - 64), F32)])
    res = _in_out_bwd("ssd", sv["x"], sv["ng"], sv["hn"], du, p["w_in"], dx,
                      tail_comm=None if tail_comm is None else (lambda g_in: tail_comm(g_in[:, :SSD_IN], dw_out)))
    dx_prev, dng, dw_in = res[:3]
    tail = res[3] if tail_comm is not None else None
    grads = dict(
        w_in=dw_in[:, :SSD_IN], w_out=dw_out, conv_w=dconv_w[:SSD_CONV], conv_b=dconv_b,
        dt_bias=ddt_bias[:, :2 * SSD_HEADS], a_log=_ssd_head_layout(dal)[:, :2 * SSD_HEADS] + dal_rep,
        d=ddexp.reshape(SSD_HEADS, SSD_HEADDIM).sum(axis=1)[None, :], norm_g=dnorm, ng=dng)
    return dx_prev, grads, cres, tail


def hg_layer_fwd(x, ng, p, comm0=None, comm1=None):
    hn = _prenorm("hg", x, ng)
    u = matmul(hn, p["w_in"], name="hg_in")
    lb = hg_lb_fwd(p["hgrn_lb"])
    (o0, st0), cres0 = _own(hg_scan_fwd(u, lb, reverse=False, comm=comm0), comm0)
    (o, st1), cres1 = _own(hg_scan_fwd(u, lb, reverse=True, o_prev=o0, comm=comm1), comm1)
    g = rowwise_fwd("hg_post", _hg_post_fn, [(o, 0, True), (u, 4 * HG_HEADS, True)], [(p["norm_g"], True)],
                    [(HG_D, MXU_DTYPE)], W=HG_D, ncb=HG_HEADS)[0]
    xn = matmul(g, p["w_out"], residual=x, name="hg_out")
    return xn, dict(x=x, ng=ng, hn=hn, u=u, lb=lb, o=o, st0=st0, st1=st1, g=g), cres0, cres1


def hg_layer_bwd(sv, p, dx, comm=None):
    u, lb = sv["u"], sv["lb"]
    dg = matmul(dx, p["w_out"], tb=True, name="hg_dg")
    dw_out = matmul(sv["g"], dx, ta=True, out_dtype=GRAD_WIRE_DTYPE, name="hg_dw_out")
    do, dgate, dnorm = rowwise_bwd("hg_post_bwd", _hg_post_fn, [(sv["o"], 0, True), (u, 4 * HG_HEADS, True)],
                                   [(p["norm_g"], True)], [dg], W=HG_D, ncb=HG_HEADS, diff_rows=[0, 1], diff_shared=[0])
    (dq0, df0, dv0, dlb0), cres = _own(hg_scan_bwd(u, lb, sv["st0"], do, reverse=False, comm=comm), comm)
    dq, df1, dv, dlb = hg_scan_bwd(u, lb, sv["st1"], do, reverse=True, prev=(dq0, dv0, dlb0))
    du = _cat_mxu([dq, df0, df1, dv, dgate])
    dhgrn_lb = hg_lb_bwd(p["hgrn_lb"], dlb)
    dx_prev, dng, dw_in = _in_out_bwd("hg", sv["x"], sv["ng"], sv["hn"], du, p["w_in"], dx)
    return dx_prev, dict(w_in=dw_in, w_out=dw_out, norm_g=dnorm, hgrn_lb=dhgrn_lb, ng=dng), cres


def _rope_consts(S):
    ar, ac = _rope_tables(S)
    ct = np.concatenate([np.cos(ar), np.cos(ar), np.cos(ac), np.cos(ac)], axis=1).astype(np.float32)
    st = np.concatenate([-np.sin(ar), np.sin(ar), -np.sin(ac), np.sin(ac)], axis=1).astype(np.float32)
    return jnp.asarray(ct), jnp.asarray(st)


def _at_qk(tag, u, col0, nheads, scale, gain, consts, cot=None):
    ct, st = consts
    rows = [(u, col0, True), (ct, 0, False), (st, 0, False)]
    shared = [(gain, False)]
    if cot is None:
        return rowwise_fwd(f"at_{tag}", _make_qk_fn(scale), rows, shared, [(AT_HD, MXU_DTYPE)], W=AT_HD, ncb=nheads)[0]
    return rowwise_bwd(f"at_{tag}_bwd", _make_qk_fn(scale), rows, shared, [cot], W=AT_HD, ncb=nheads,
                       diff_rows=[0], diff_shared=[0])


def at_layer_fwd(x, ng, p, comm=None):
    S = x.shape[0]
    hn = _prenorm("at", x, ng)
    u = matmul(hn, p["w_in"], name="at_in")
    consts = _rope_consts(S)
    qr = _at_qk("q", u, 0, AT_HEADS, AT_HD ** -0.5, p["q_g"], consts)
    kr = _at_qk("k", u, AT_HEADS, AT_KV, 1.0, p["k_g"], consts)
    vc0 = (AT_QW + AT_KW) // AT_HD
    (o, lse), cres = _own(flash_fwd(qr, kr, u, v_col0=vc0, comm=comm), comm)
    g = rowwise_fwd("at_gate", _gate_fn, [(o, 0, True), (u, (AT_QW + 2 * AT_KW) // 1024, True)], [],
                    [(1024, MXU_DTYPE)], W=1024, ncb=AT_QW // 1024)[0]
    xn = matmul(g, p["w_out"], residual=x, name="at_out")
    return xn, dict(x=x, ng=ng, hn=hn, u=u, qr=qr, kr=kr, o=o, lse=lse, g=g), cres


def at_layer_bwd(sv, p, dx, comm=None):
    u, qr, kr = sv["u"], sv["qr"], sv["kr"]
    S = u.shape[0]
    consts = _rope_consts(S)
    vc0 = (AT_QW + AT_KW) // AT_HD
    dg = matmul(dx, p["w_out"], tb=True, name="at_dg")
    dw_out = matmul(sv["g"], dx, ta=True, out_dtype=GRAD_WIRE_DTYPE, name="at_dw_out")
    do, dgate = rowwise_bwd("at_gate_bwd", _gate_fn, [(sv["o"], 0, True), (u, (AT_QW + 2 * AT_KW) // 1024, True)], [],
                            [dg], W=1024, ncb=AT_QW // 1024, diff_rows=[0, 1], diff_shared=[])
    dqs, delta = flash_bwd_dq(qr, kr, u, sv["o"], sv["lse"], do, v_col0=vc0)
    (dkr, dv), cres = _own(flash_bwd_dkv(qr, kr, u, sv["lse"], delta, do, v_col0=vc0, comm=comm), comm)
    dq_raw, dqg = _at_qk("q", u, 0, AT_HEADS, AT_HD ** -0.5, p["q_g"], consts, cot=dqs)
    dk_raw, dkg = _at_qk("k", u, AT_HEADS, AT_KV, 1.0, p["k_g"], consts, cot=dkr)
    du = _cat_mxu([dq_raw, dk_raw, dv, dgate])
    dx_prev, dng, dw_in = _in_out_bwd("at", sv["x"], sv["ng"], sv["hn"], du, p["w_in"], dx)
    return dx_prev, dict(w_in=dw_in, w_out=dw_out, q_g=dqg, k_g=dkg, ng=dng), cres


def _to_stream(t, dil):
    S = t.shape[0]
    return t.reshape(S // dil, dil, DL_HEADS, DL_HD).transpose(2, 1, 0, 3)


def _from_stream(t):
    H, dil, Ls, E = t.shape
    return t.transpose(2, 1, 0, 3).reshape(Ls * dil, H * E)


def _stream_to_hm(t):
    H, dil, Ls, w = t.shape
    return t.transpose(0, 2, 1, 3).reshape(H * Ls * dil, w)


def _hm_to_stream(t, dil):
    w = t.shape[1]
    S = t.shape[0] // DL_HEADS
    return t.reshape(DL_HEADS, S // dil, dil, w).transpose(0, 2, 1, 3)


def _pad_l(t):
    return jnp.pad(t, ((0, 0), (0, 0), (DL_HALF, DL_HALF), (0, 0)))


OX_LSE = DL_HD
DOX_LSE, DOX_DM = DL_HD, DL_HD + 32


def _win(p_ref, c_ref, n_ref, h, T):
    return jnp.concatenate([p_ref[h, 0, T - DL_HALF:T, :], c_ref[h, 0], n_ref[h, 0, 0:DL_HALF, :]], axis=0)


def _win_specs(T, E, nb):
    return [pl.BlockSpec((DL_HEADS, 1, T, E), lambda d, n: (0, d, jnp.maximum(n - 1, 0), 0)),
            pl.BlockSpec((DL_HEADS, 1, T, E), lambda d, n: (0, d, n, 0)),
            pl.BlockSpec((DL_HEADS, 1, T, E), lambda d, n: (0, d, jnp.minimum(n + 1, nb - 1), 0))]


def _band_mask_q(n, T, W, Ls):
    i = lax.broadcasted_iota(jnp.int32, (T, W), 0)
    j = lax.broadcasted_iota(jnp.int32, (T, W), 1)
    kpos = n * T + j - DL_HALF
    return (jnp.abs(j - DL_HALF - i) <= DL_HALF) & (kpos >= 0) & (kpos < Ls)


def band_fwd(q, k, v, bias, *, scale):
    H, dil, Ls, E = q.shape
    T, W = _dl_tiles(Ls)
    nb = Ls // T

    def body(q_ref, kp_ref, kc_ref, kn_ref, vp_ref, vc_ref, vn_ref, b_ref, ox_ref):
        n = pl.program_id(1)
        mask = _band_mask_q(n, T, W, Ls)
        for h in range(H):
            kw = _win(kp_ref, kc_ref, kn_ref, h, T)
            vw = _win(vp_ref, vc_ref, vn_ref, h, T)
            s = _dot(q_ref[h, 0], kw, ((1,), (1,))) * scale + b_ref[h]
            s = jnp.where(mask, s, NEG_BIG)
            m = jnp.max(s, axis=1, keepdims=True)
            lse = m + jnp.log(jnp.sum(jnp.exp(s - m), axis=1, keepdims=True))
            p = jnp.exp(s - lse)
            ox_ref[h, 0, :, 0:E] = _dot(p, vw, ((1,), (0,)))
            ox_ref[h, 0, :, E:2 * E] = lse + jnp.zeros((T, E), F32)

    cur = pl.BlockSpec((H, 1, T, E), lambda d, n: (0, d, n, 0))
    return pl.pallas_call(
        body, name=f"band_fwd_{dil}", grid=(dil, nb),
        in_specs=[cur] + _win_specs(T, E, nb) + _win_specs(T, E, nb) + [pl.BlockSpec((H, T, W), lambda d, n: (0, 0, 0))],
        out_specs=pl.BlockSpec((H, 1, T, 2 * E), lambda d, n: (0, d, n, 0)),
        out_shape=jax.ShapeDtypeStruct((H, dil, Ls, 2 * E), F32),
        compiler_params=_cp("parallel", "parallel"),
    )(q, k, k, k, v, v, v, bias)


def band_bwd_dq(q, k, v, bias, dox, *, scale):
    H, dil, Ls, E = q.shape
    T, W = _dl_tiles(Ls)
    nb = Ls // T

    def body(q_ref, kp_ref, kc_ref, kn_ref, vp_ref, vc_ref, vn_ref, b_ref, dox_ref, dq_ref, db_ref):
        d, n = pl.program_id(0), pl.program_id(1)
        mask = _band_mask_q(n, T, W, Ls)
        first = jnp.logical_and(d == 0, n == 0)

        @pl.when(first)
        def _():
            db_ref[...] = jnp.zeros_like(db_ref)

        for h in range(H):
            kw = _win(kp_ref, kc_ref, kn_ref, h, T)
            vw = _win(vp_ref, vc_ref, vn_ref, h, T)
            dox = dox_ref[h, 0]
            do, lse, dm = dox[:, 0:E], dox[:, DOX_LSE:DOX_LSE + 1], dox[:, DOX_DM:DOX_DM + 1]
            s = _dot(q_ref[h, 0], kw, ((1,), (1,))) * scale + b_ref[h]
            p = jnp.where(mask, jnp.exp(jnp.where(mask, s, 0.0) - lse), 0.0)
            dp = _dot(do, vw, ((1,), (1,)))
            ds = p * (dp - dm)
            dq_ref[h, 0] = (_dot(ds, kw, ((1,), (0,))) * scale).astype(dq_ref.dtype)
            db_ref[h] += ds

    cur = pl.BlockSpec((H, 1, T, E), lambda d, n: (0, d, n, 0))
    bspec = pl.BlockSpec((H, T, W), lambda d, n: (0, 0, 0))
    return pl.pallas_call(
        body, name=f"band_bwd_dq_{dil}", grid=(dil, nb),
        in_specs=[cur] + _win_specs(T, E, nb) + _win_specs(T, E, nb) + [bspec,
                  pl.BlockSpec((H, 1, T, 2 * E), lambda d, n: (0, d, n, 0))],
        out_specs=[cur, bspec],
        out_shape=[jax.ShapeDtypeStruct((H, dil, Ls, E), MXU_DTYPE), jax.ShapeDtypeStruct((H, T, W), F32)],
        compiler_params=_cp("arbitrary", "arbitrary"),
    )(q, k, k, k, v, v, v, bias, dox)


def band_bwd_dkv(q, k, v, bias_t, dox, *, scale):
    H, dil, Ls, E = k.shape
    T, W = _dl_tiles(Ls)
    nb = Ls // T

    def body(qp_ref, qc_ref, qn_ref, k_ref, v_ref, b_ref, dp_ref, dc_ref, dn_ref, dk_ref, dv_ref):
        n = pl.program_id(1)
        iw = lax.broadcasted_iota(jnp.int32, (W, T), 0)
        j = lax.broadcasted_iota(jnp.int32, (W, T), 1)
        qpos = n * T + iw - DL_HALF
        mask = (jnp.abs(j + DL_HALF - iw) <= DL_HALF) & (qpos >= 0) & (qpos < Ls)
        for h in range(H):
            qw = _win(qp_ref, qc_ref, qn_ref, h, T)
            doxw = _win(dp_ref, dc_ref, dn_ref, h, T)
            dow, lsew, dmw = doxw[:, 0:E], doxw[:, DOX_LSE:DOX_LSE + 1], doxw[:, DOX_DM:DOX_DM + 1]
            s = _dot(qw, k_ref[h, 0], ((1,), (1,))) * scale + b_ref[h]
            p = jnp.where(mask, jnp.exp(jnp.where(mask, s, 0.0) - lsew), 0.0)
            dv_ref[h, 0] = _dot(p, dow, ((0,), (0,))).astype(dv_ref.dtype)
            dp = _dot(dow, v_ref[h, 0], ((1,), (1,)))
            ds = p * (dp - dmw)
            dk_ref[h, 0] = (_dot(ds, qw, ((0,), (0,))) * scale).astype(dk_ref.dtype)

    cur = pl.BlockSpec((H, 1, T, E), lambda d, n: (0, d, n, 0))
    return pl.pallas_call(
        body, name=f"band_bwd_dkv_{dil}", grid=(dil, nb),
        in_specs=_win_specs(T, E, nb) + [cur, cur, pl.BlockSpec((H, W, T), lambda d, n: (0, 0, 0))]
        + _win_specs(T, 2 * E, nb),
        out_specs=[cur, cur],
        out_shape=[jax.ShapeDtypeStruct((H, dil, Ls, E), MXU_DTYPE)] * 2,
        compiler_params=_cp("parallel", "parallel"),
    )(q, q, q, k, v, bias_t, dox, dox, dox)


def dl_merge_fwd(oxs, *, R=1024):
    rows = oxs[0].shape[0]
    R = min(R, rows)
    E = DL_HD

    def body(a_ref, b_ref, c_ref, o_ref):
        vals = [r[...] for r in (a_ref, b_ref, c_ref)]
        o_ref[...] = _dl_merge_fn(*[t[:, 0:E] for t in vals], *[t[:, OX_LSE:OX_LSE + 1] for t in vals])[0]

    spec = pl.BlockSpec((R, 2 * E), lambda i: (i, 0))
    return pl.pallas_call(
        body, name="dl_merge", grid=(rows // R,), in_specs=[spec] * 3,
        out_specs=pl.BlockSpec((R, E), lambda i: (i, 0)), out_shape=jax.ShapeDtypeStruct((rows, E), F32),
        compiler_params=_cp("parallel"),
    )(*oxs)


def dl_merge_bwd(oxs, do, *, R=1024):
    rows = oxs[0].shape[0]
    R = min(R, rows)
    E = DL_HD

    def body(a_ref, b_ref, c_ref, do_ref, da_ref, db_ref, dc_ref):
        vals = [r[...] for r in (a_ref, b_ref, c_ref)]
        os_ = [t[:, 0:E] for t in vals]
        ls_ = [t[:, OX_LSE:OX_LSE + 1] for t in vals]
        _, vjp = jax.vjp(_dl_merge_fn, *os_, *ls_)
        g = vjp((do_ref[...],))
        for k, d_ref in enumerate((da_ref, db_ref, dc_ref)):
            dm = jnp.sum(g[k] * os_[k], axis=1, keepdims=True) - g[3 + k]
            d_ref[:, 0:E] = g[k]
            d_ref[:, DOX_LSE:DOX_DM] = ls_[k] + jnp.zeros((R, DOX_DM - DOX_LSE), F32)
            d_ref[:, DOX_DM:2 * E] = dm + jnp.zeros((R, 2 * E - DOX_DM), F32)

    spec = pl.BlockSpec((R, 2 * E), lambda i: (i, 0))
    return pl.pallas_call(
        body, name="dl_merge_bwd", grid=(rows // R,), in_specs=[spec] * 3 + [pl.BlockSpec((R, E), lambda i: (i, 0))],
        out_specs=[spec] * 3, out_shape=[jax.ShapeDtypeStruct((rows, 2 * E), F32)] * 3,
        compiler_params=_cp("parallel"),
    )(*oxs, do)


def _dl_bias_tables(rel_bias, dil, T):
    W = T + 2 * DL_HALF
    bq, bk = _dl_bucket_tables(dil, T)
    idx = np.concatenate([bq.reshape(-1), bk.reshape(-1)])
    onehot_t = (np.arange(REL_BUCKETS)[:, None] == idx[None, :]).astype(np.float32)
    tab = matmul(rel_bias.T, jnp.asarray(onehot_t), exact=True, name=f"dl_bias_{dil}", tm=DL_HEADS, tk=REL_BUCKETS,
                 tn=_tile(2 * T * W, (8192, 4096, 2048, 1024, 512, 256, 128)))
    return tab[:, :T * W].reshape(DL_HEADS, T, W), tab[:, T * W:].reshape(DL_HEADS, W, T), bq


def _dl_dm_fn(do, o, dl):
    return (jnp.sum(do * o, axis=-1, keepdims=True) - dl,)


def _old_dl_layer_fwd(x, ng, p):
    S = x.shape[0]
    hn = _prenorm("dl", x, ng)
    u = matmul(hn, p["w_in"], name="dl_in")
    scale = DL_HD ** -0.5
    per_group, o_hm, lse_hm = [], [], []
    for gi, (window, dil) in enumerate(DL_PAIRS):
        base = gi * 3 * DL_W
        Ls = S // dil
        T, _ = _dl_tiles(Ls)
        bq, bk = _dl_bucket_tables(dil, T)
        qs = _to_stream(u[:, base:base + DL_W], dil).astype(MXU_DTYPE)
        ks = _to_stream(u[:, base + DL_W:base + 2 * DL_W], dil).astype(MXU_DTYPE)
        vs = _to_stream(u[:, base + 2 * DL_W:base + 3 * DL_W], dil).astype(MXU_DTYPE)
        bias = p["rel_bias"][bq].transpose(2, 0, 1)
        o_s, lse_s = band_fwd(qs, _pad_l(ks), _pad_l(vs), bias, scale=scale)
        per_group.append(dict(qs=qs, ks=ks, vs=vs, lse_s=lse_s, bq=bq, bk=bk, dil=dil))
        o_hm.append(_stream_to_hm(o_s))
        lse_hm.append(_stream_to_hm(lse_s))
    rows = [(t, 0, False) for t in o_hm] + [(t, 0, False, 1) for t in lse_hm]
    om = rowwise_fwd("dl_merge", _dl_merge_fn, rows, [], [(DL_HD, F32)], W=DL_HD)[0]
    o = om.reshape(DL_HEADS, S, DL_HD).transpose(1, 0, 2).reshape(S, DL_W)
    g = rowwise_fwd("dl_gate", _gate_fn, [(o, 0, False), (u, 9, False)], [], [(DL_W, MXU_DTYPE)], W=DL_W)[0]
    xn = matmul(g, p["w_out"], residual=x, name="dl_out")
    return xn, dict(x=x, ng=ng, hn=hn, u=u, per_group=per_group, o_hm=o_hm, lse_hm=lse_hm, o=o, g=g)


def _old_dl_layer_bwd(sv, p, dx):
    u = sv["u"]
    S = u.shape[0]
    scale = DL_HD ** -0.5
    dg = matmul(dx, p["w_out"], tb=True, name="dl_dg")
    dw_out = matmul(sv["g"], dx, ta=True, out_dtype=GRAD_WIRE_DTYPE, name="dl_dw_out")
    do, dgate = rowwise_bwd("dl_gate_bwd", _gate_fn, [(sv["o"], 0, False), (sv["ugate"], 0, False)], [], [dg], W=DL_W,
                            diff_rows=[0, 1], diff_shared=[])
    do_hm = do.reshape(S, DL_HEADS, DL_HD).transpose(1, 0, 2).reshape(DL_HEADS * S, DL_HD)
    rows = [(t, 0, False) for t in sv["o_hm"]] + [(t, 0, False, 1) for t in sv["lse_hm"]]
    dmerge = rowwise_bwd("dl_merge_bwd", _dl_merge_fn, rows, [], [do_hm], W=DL_HD, diff_rows=[0, 1, 2, 3, 4, 5],
                         diff_shared=[])
    parts, dbs, onehots = [], [], []
    for gi, pg in enumerate(sv["per_group"]):
        dil = pg["dil"]
        Ls = S // dil
        T, W = _dl_tiles(Ls)
        dog, dlg = dmerge[gi], dmerge[3 + gi]
        dm = rowwise_fwd(f"dl_dm_{gi}", _dl_dm_fn, [(dog, 0, False), (sv["o_hm"][gi], 0, False), (dlg, 0, False, 1)], [],
                         [(1, F32)], W=DL_HD)[0]
        do_s, dm_s = _hm_to_stream(dog, dil), _hm_to_stream(dm, dil)
        bias = p["rel_bias"][pg["bq"]].transpose(2, 0, 1)
        bias_t = p["rel_bias"][pg["bk"]].transpose(2, 0, 1)
        kp, vp = _pad_l(pg["ks"]), _pad_l(pg["vs"])
        dq_s, dbias = band_bwd_dq(pg["qs"], kp, vp, bias, pg["lse_s"], dm_s, do_s, scale=scale)
        dk_s, dv_s = band_bwd_dkv(_pad_l(pg["qs"]), pg["ks"], pg["vs"], bias_t, _pad_l(pg["lse_s"]), _pad_l(dm_s),
                                  _pad_l(do_s), scale=scale)
        parts += [_from_stream(dq_s), _from_stream(dk_s), _from_stream(dv_s)]
        dbs.append(dbias.reshape(DL_HEADS, T * W))
        onehots.append((pg["bq"].reshape(-1)[:, None] == np.arange(REL_BUCKETS)[None, :]).astype(np.float32))
    drel = matmul(jnp.concatenate(dbs, axis=1), jnp.asarray(np.concatenate(onehots, axis=0)), exact=True,
                  name="dl_drel", tm=DL_HEADS, tn=REL_BUCKETS, tk=2048)
    du = _cat_mxu(parts + [dgate])
    dx_prev, dng, dw_in = _in_out_bwd("dl", sv["x"], sv["ng"], sv["hn"], du, p["w_in"], dx)
    return dx_prev, dict(w_in=dw_in, w_out=dw_out, rel_bias=drel.T, ng=dng)


def dl_layer_fwd(x, ng, p):
    S = x.shape[0]
    hn = _prenorm("dl", x, ng)
    nqkv = 3 * len(DL_PAIRS) * DL_W
    uqkv = matmul(hn, p["w_in"], name="dl_in_qkv", b_cols=(0, nqkv), out_dtype=MXU_DTYPE)
    ugate = matmul(hn, p["w_in"], name="dl_in_gate", b_cols=(nqkv, DL_W))
    scale = DL_HD ** -0.5
    per_group, ox_hm = [], []
    for gi, (window, dil) in enumerate(DL_PAIRS):
        base = gi * 3 * DL_W
        T, _ = _dl_tiles(S // dil)
        qs, ks, vs = [_to_stream(uqkv[:, base + c * DL_W:base + (c + 1) * DL_W], dil) for c in range(3)]
        bias, bias_t, bq = _dl_bias_tables(p["rel_bias"], dil, T)
        ox_s = band_fwd(qs, ks, vs, bias, scale=scale)
        per_group.append(dict(qs=qs, ks=ks, vs=vs, bias=bias, bias_t=bias_t, bq=bq, dil=dil))
        ox_hm.append(_stream_to_hm(ox_s))
    om = dl_merge_fwd(ox_hm)
    o = om.reshape(DL_HEADS, S, DL_HD).transpose(1, 0, 2).reshape(S, DL_W)
    g = rowwise_fwd("dl_gate", _gate_fn, [(o, 0, False), (ugate, 0, False)], [], [(DL_W, MXU_DTYPE)], W=DL_W)[0]
    xn = matmul(g, p["w_out"], residual=x, name="dl_out")
    return xn, dict(x=x, ng=ng, hn=hn, ugate=ugate, per_group=per_group, ox_hm=ox_hm, o=o, g=g)


def dl_layer_bwd(sv, p, dx):
    ugate = sv["ugate"]
    S = ugate.shape[0]
    scale = DL_HD ** -0.5
    dg = matmul(dx, p["w_out"], tb=True, name="dl_dg")
    dw_out = matmul(sv["g"], dx, ta=True, out_dtype=GRAD_WIRE_DTYPE, name="dl_dw_out")
    do, dgate = rowwise_bwd("dl_gate_bwd", _gate_fn, [(sv["o"], 0, False), (sv["ugate"], 0, False)], [], [dg], W=DL_W,
                            diff_rows=[0, 1], diff_shared=[])
    do_hm = do.reshape(S, DL_HEADS, DL_HD).transpose(1, 0, 2).reshape(DL_HEADS * S, DL_HD)
    dox_hm = dl_merge_bwd(sv["ox_hm"], do_hm)
    parts, dbs, onehots = [], [], []
    for gi, pg in enumerate(sv["per_group"]):
        dil = pg["dil"]
        T, W = _dl_tiles(S // dil)
        dox_s = _hm_to_stream(dox_hm[gi], dil)
        dq_s, dbias = band_bwd_dq(pg["qs"], pg["ks"], pg["vs"], pg["bias"], dox_s, scale=scale)
        dk_s, dv_s = band_bwd_dkv(pg["qs"], pg["ks"], pg["vs"], pg["bias_t"], dox_s, scale=scale)
        parts += [_from_stream(dq_s), _from_stream(dk_s), _from_stream(dv_s)]
        dbs.append(dbias.reshape(DL_HEADS, T * W))
        onehots.append((pg["bq"].reshape(-1)[:, None] == np.arange(REL_BUCKETS)[None, :]).astype(np.float32))
    drel = matmul(jnp.concatenate(dbs, axis=1), jnp.asarray(np.concatenate(onehots, axis=0)), exact=True,
                  name="dl_drel", tm=DL_HEADS, tn=REL_BUCKETS, tk=2048)
    du = _cat_mxu(parts + [dgate])
    dx_prev, dng, dw_in = _in_out_bwd("dl", sv["x"], sv["ng"], sv["hn"], du, p["w_in"], dx)
    return dx_prev, dict(w_in=dw_in, w_out=dw_out, rel_bias=drel.T, ng=dng)


WEIGHT_ORDER = ['norm_g', 'final_g', 'rel_bias', 'hgrn_lb', 'ssd_w_in', 'ssd_conv_w', 'ssd_conv_b', 'ssd_dt_bias',
                'ssd_a_log', 'ssd_d', 'ssd_norm_g', 'ssd_w_out', 'hg_w_in', 'hg_norm_g', 'hg_w_out', 'at_w_in',
                'at_q_norm_g', 'at_k_norm_g', 'at_w_out', 'dl_w_in', 'dl_w_out']
BIG_IN = ['ssd_w_in', 'hg_w_in', 'at_w_in', 'dl_w_in']
BIG_OUT = ['ssd_w_out', 'hg_w_out', 'at_w_out', 'dl_w_out']
BIG = BIG_IN + BIG_OUT
SMALL = [n for n in WEIGHT_ORDER if n not in BIG]
LANES = 128


def _pack(arrs):
    flat = jnp.concatenate([a.reshape(-1).astype(F32) for a in arrs])
    n = flat.shape[0]
    rows = -(-n // (8 * LANES)) * 8
    return jnp.pad(flat, (0, rows * LANES - n)).reshape(rows, LANES)


def _unpack(buf, shapes):
    flat = buf.reshape(-1)
    out, off = [], 0
    for shp in shapes:
        n = int(np.prod(shp)) if len(shp) else 1
        out.append(flat[off:off + n].reshape(shp))
        off += n
    return out


def kernel(x, norm_g, final_g, rel_bias, hgrn_lb, ssd_w_in, ssd_conv_w, ssd_conv_b, ssd_dt_bias, ssd_a_log, ssd_d, ssd_norm_g, ssd_w_out, hg_w_in, hg_norm_g, hg_w_out, at_w_in, at_q_norm_g, at_k_norm_g, at_w_out, dl_w_in, dl_w_out, loss_target, m_norm_g, m_final_g, m_rel_bias, m_hgrn_lb, m_ssd_w_in, m_ssd_conv_w, m_ssd_conv_b, m_ssd_dt_bias, m_ssd_a_log, m_ssd_d, m_ssd_norm_g, m_ssd_w_out, m_hg_w_in, m_hg_norm_g, m_hg_w_out, m_at_w_in, m_at_q_norm_g, m_at_k_norm_g, m_at_w_out, m_dl_w_in, m_dl_w_out, v_norm_g, v_final_g, v_rel_bias, v_hgrn_lb, v_ssd_w_in, v_ssd_conv_w, v_ssd_conv_b, v_ssd_dt_bias, v_ssd_a_log, v_ssd_d, v_ssd_norm_g, v_ssd_w_out, v_hg_w_in, v_hg_norm_g, v_hg_w_out, v_at_w_in, v_at_q_norm_g, v_at_k_norm_g, v_at_w_out, v_dl_w_in, v_dl_w_out):
    w = dict(norm_g=norm_g, final_g=final_g, rel_bias=rel_bias, hgrn_lb=hgrn_lb, ssd_w_in=ssd_w_in, ssd_conv_w=ssd_conv_w, ssd_conv_b=ssd_conv_b, ssd_dt_bias=ssd_dt_bias, ssd_a_log=ssd_a_log, ssd_d=ssd_d, ssd_norm_g=ssd_norm_g, ssd_w_out=ssd_w_out, hg_w_in=hg_w_in, hg_norm_g=hg_norm_g, hg_w_out=hg_w_out, at_w_in=at_w_in, at_q_norm_g=at_q_norm_g, at_k_norm_g=at_k_norm_g, at_w_out=at_w_out, dl_w_in=dl_w_in, dl_w_out=dl_w_out)
    m = dict(norm_g=m_norm_g, final_g=m_final_g, rel_bias=m_rel_bias, hgrn_lb=m_hgrn_lb, ssd_w_in=m_ssd_w_in, ssd_conv_w=m_ssd_conv_w, ssd_conv_b=m_ssd_conv_b, ssd_dt_bias=m_ssd_dt_bias, ssd_a_log=m_ssd_a_log, ssd_d=m_ssd_d, ssd_norm_g=m_ssd_norm_g, ssd_w_out=m_ssd_w_out, hg_w_in=m_hg_w_in, hg_norm_g=m_hg_norm_g, hg_w_out=m_hg_w_out, at_w_in=m_at_w_in, at_q_norm_g=m_at_q_norm_g, at_k_norm_g=m_at_k_norm_g, at_w_out=m_at_w_out, dl_w_in=m_dl_w_in, dl_w_out=m_dl_w_out)
    v = dict(norm_g=v_norm_g, final_g=v_final_g, rel_bias=v_rel_bias, hgrn_lb=v_hgrn_lb, ssd_w_in=v_ssd_w_in, ssd_conv_w=v_ssd_conv_w, ssd_conv_b=v_ssd_conv_b, ssd_dt_bias=v_ssd_dt_bias, ssd_a_log=v_ssd_a_log, ssd_d=v_ssd_d, ssd_norm_g=v_ssd_norm_g, ssd_w_out=v_ssd_w_out, hg_w_in=v_hg_w_in, hg_norm_g=v_hg_norm_g, hg_w_out=v_hg_w_out, at_w_in=v_at_w_in, at_q_norm_g=v_at_q_norm_g, at_k_norm_g=v_at_k_norm_g, at_w_out=v_at_w_out, dl_w_in=v_dl_w_in, dl_w_out=v_dl_w_out)
    me = 4 * lax.axis_index("x") + 2 * lax.axis_index("y") + lax.axis_index("c")
    xs = x[0]
    S = xs.shape[0]

    shard2d = {n: w[n][0] for n in BIG}
    wire = {n: shard2d[n].astype(MXU_DTYPE) for n in BIG}

    def ag(names):
        return ("ag", [wire[n] for n in names])

    def assemble(names, blks):
        out = {}
        for n, blk in zip(names, blks):
            r, c = shard2d[n].shape
            out[n] = blk.transpose(1, 0, 2).reshape(r, N_DEV * c) if n in BIG_IN else blk.reshape(N_DEV * r, c)
        return out

    def a2a(names, gw):
        bufs = []
        for n in names:
            r, c = shard2d[n].shape
            bufs.append(gw[n].reshape(r, N_DEV, c).transpose(1, 0, 2) if n in BIG_IN else gw[n].reshape(N_DEV, r, c))
        return ("a2a", bufs)

    ssd_w, hg_w, at_w, dl_w = (["ssd_w_in", "ssd_w_out"], ["hg_w_in", "hg_w_out"], ["at_w_in", "at_w_out"],
                               ["dl_w_in", "dl_w_out"])
    full = assemble(ssd_w, allgather_two_level_multi([wire[n] for n in ssd_w], name="allgather_ssd_weights"))
    ncw = ssd_conv_w.shape[2]
    nhg = hg_norm_g.shape[1]
    small_shard = jnp.zeros((8, 512), F32)
    small_shard = small_shard.at[:SSD_CONV, :ncw].set(ssd_conv_w[0]).at[SSD_CONV, :nhg].set(hg_norm_g[0])
    small_all = allgather_two_level(small_shard, name="allgather_small_weights")
    conv_w_full = small_all[:, :SSD_CONV, :ncw].transpose(1, 0, 2).reshape(SSD_CONV, N_DEV * ncw)
    hg_norm_full = small_all[:, SSD_CONV, :nhg].reshape(1, N_DEV * nhg)

    p_ssd = dict(w_in=jnp.pad(full["ssd_w_in"], ((0, 0), (0, SSD_IN_PAD - SSD_IN))), w_out=full["ssd_w_out"],
                 conv_w=conv_w_full, conv_b=ssd_conv_b,
                 dt_bias=jnp.pad(ssd_dt_bias.reshape(1, 2 * SSD_HEADS), ((0, 0), (0, 128 - 2 * SSD_HEADS))),
                 alog=jnp.pad(ssd_a_log.reshape(1, 2 * SSD_HEADS), ((0, 0), (0, 128 - 2 * SSD_HEADS))),
                 dexp=jnp.repeat(ssd_d.reshape(-1), SSD_HEADDIM)[None, :], norm_g=ssd_norm_g)
    x1, sv0, got, got_at_in = ssd_layer_fwd(xs, norm_g[0:1], p_ssd, comm=ag(hg_w), comm1=ag(["at_w_in"]))
    full.update(assemble(hg_w, got))
    p_hg = dict(w_in=full["hg_w_in"], w_out=full["hg_w_out"], norm_g=hg_norm_full, hgrn_lb=hgrn_lb)
    x2, sv1, got_at_out, _ = hg_layer_fwd(x1, norm_g[1:2], p_hg, comm0=ag(["at_w_out"]))
    full.update(assemble(at_w, got_at_in + got_at_out))
    p_at = dict(w_in=full["at_w_in"], w_out=full["at_w_out"], q_g=at_q_norm_g, k_g=at_k_norm_g)
    x3, sv2, got_dl = at_layer_fwd(x2, norm_g[2:3], p_at, comm=ag(dl_w))
    full.update(assemble(dl_w, got_dl))
    p_dl = dict(w_in=full["dl_w_in"], w_out=full["dl_w_out"], rel_bias=rel_bias)
    x4, sv3 = dl_layer_fwd(x3, norm_g[3:4], p_dl)
    loss_part, dx4, dfinal = loss_head(x4, final_g[None, :], loss_target[0])
    dx3, g3 = dl_layer_bwd(sv3, p_dl, dx4)
    dx2, g2, recv_dl = at_layer_bwd(sv2, p_at, dx3, comm=a2a(dl_w, dict(dl_w_in=g3["w_in"], dl_w_out=g3["w_out"])))
    dx1, g1, recv_at = hg_layer_bwd(sv1, p_hg, dx2, comm=a2a(at_w, dict(at_w_in=g2["w_in"], at_w_out=g2["w_out"])))
    dx0, g0, recv_hg, recv_ssd = ssd_layer_bwd(
        sv0, p_ssd, dx1, comm=a2a(hg_w, dict(hg_w_in=g1["w_in"], hg_w_out=g1["w_out"])),
        tail_comm=lambda g_in, g_out: a2a(ssd_w, dict(ssd_w_in=g_in, ssd_w_out=g_out)))
    recv = dict(zip(ssd_w + hg_w + at_w + dl_w, recv_ssd + recv_hg + recv_at + recv_dl))

    small_full = dict(
        norm_g=jnp.concatenate([g0["ng"], g1["ng"], g2["ng"], g3["ng"]], axis=0), final_g=dfinal[0],
        rel_bias=g3["rel_bias"], hgrn_lb=g1["hgrn_lb"], ssd_conv_w=g0["conv_w"][None], ssd_conv_b=g0["conv_b"],
        ssd_dt_bias=g0["dt_bias"].reshape(1, 2, SSD_HEADS), ssd_a_log=g0["a_log"].reshape(1, 2, SSD_HEADS),
        ssd_d=g0["d"], ssd_norm_g=g0["norm_g"], hg_norm_g=g1["norm_g"], at_q_norm_g=g2["q_g"], at_k_norm_g=g2["k_g"])
    packed = _pack([loss_part[0, 0:1]] + [small_full[n] for n in SMALL])
    summed = sum_parts(allgather_two_level(packed, name="allgather_small_grads"), name="sum_small_grads")
    parts = _unpack(summed, [()] + [small_full[n].shape for n in SMALL])
    loss = parts[0]
    gsmall = dict(zip(SMALL, parts[1:]))
    gsmall["ssd_conv_w"] = lax.dynamic_slice_in_dim(gsmall["ssd_conv_w"], me * ncw, ncw, axis=2)
    gsmall["hg_norm_g"] = lax.dynamic_slice_in_dim(gsmall["hg_norm_g"], me * nhg, nhg, axis=1)
    shapes = [w[n].shape for n in SMALL]
    d_p, m_p, v_p = adamw_plain(_pack([w[n] for n in SMALL]), _pack([gsmall[n] for n in SMALL]),
                                _pack([m[n] for n in SMALL]), _pack([v[n] for n in SMALL]), name="adamw_small")
    grads = dict(gsmall)
    deltas = dict(zip(SMALL, _unpack(d_p, shapes)))
    new_m = dict(zip(SMALL, _unpack(m_p, shapes)))
    new_v = dict(zip(SMALL, _unpack(v_p, shapes)))

    for n in BIG:
        gs, ds, ms, vs = adamw_sum(recv[n], shard2d[n], m[n][0], v[n][0], name=f"adamw_{n}")
        grads[n], deltas[n], new_m[n], new_v[n] = gs[None], ds[None], ms[None], vs[None]

    return (loss, dx0[None], *[grads[n] for n in WEIGHT_ORDER], *[deltas[n] for n in WEIGHT_ORDER],
            *[new_m[n] for n in WEIGHT_ORDER], *[new_v[n] for n in WEIGHT_ORDER])
```

```python
import functools
import math

import jax
import jax.numpy as jnp
import numpy as np
from jax import lax
from jax.experimental import pallas as pl
from jax.experimental.pallas import tpu as pltpu

F32 = jnp.float32
MXU_DTYPE = jnp.bfloat16
GRAD_WIRE_DTYPE = jnp.bfloat16
HIGHEST = lax.Precision.HIGHEST
MESH_ID = pl.DeviceIdType.MESH
N_DEV = 8

D_MODEL = 1024
EPS = 1e-6
NEG_BIG = -1e30

SSD_DI = 2048
SSD_HEADDIM = 64
SSD_HEADS = 32
SSD_GROUPS = 4
SSD_HPG = 8
SSD_STATE = 128
SSD_CONV = 7
SSD_CHUNK = 128
SSD_CONV_CH = SSD_DI + 2 * SSD_GROUPS * SSD_STATE
SSD_IN = SSD_DI + SSD_CONV_CH + 2 * SSD_HEADS
SSD_IN_PAD = 5376

HG_CHUNK = 32
HG_HEADS = 8
HG_D = 128
HG_W = 1024

AT_HEADS = 16
AT_KV = 8
AT_HD = 128
AT_QW = 2048
AT_KW = 1024
GRID_W = 64
ROPE_THETA = 10000.0

DL_PAIRS = ((128, 1), (512, 4), (2048, 16))
DL_HEADS = 16
DL_HD = 64
DL_W = 1024
DL_HALF = 64
REL_BUCKETS = 32
REL_MAX_DIST = 1024

ADAM_LR = 0.001
ADAM_B1 = 0.9
ADAM_B2 = 0.999
ADAM_EPS = 1e-08
ADAM_WD = 0.01
ADAM_STEP = 10

VMEM_LIMIT = 56 * 1024 * 1024


def _cp(*sem):
    return pltpu.CompilerParams(dimension_semantics=tuple(sem), vmem_limit_bytes=VMEM_LIMIT)


def _tile(n, cands=(1024, 768, 512, 384, 256, 128)):
    for c in cands:
        if n % c == 0:
            return c
    return n


def _dot(a, b, dims):
    return lax.dot_general(a.astype(MXU_DTYPE), b.astype(MXU_DTYPE), (dims, ((), ())), preferred_element_type=F32)


def _dot_exact(a, b, dims):
    return lax.dot_general(a, b, (dims, ((), ())), precision=HIGHEST, preferred_element_type=F32)


def _silu(x):
    return x * jax.nn.sigmoid(x)


def _my_pos():
    return lax.axis_index("x"), lax.axis_index("y"), lax.axis_index("c")


def _flat(px, py, pc):
    return 4 * px + 2 * py + pc


def _peers():
    x_, y_, c_ = _my_pos()
    out = []
    for k in range(1, N_DEV):
        fx, fy, fc = (k >> 2) & 1, (k >> 1) & 1, k & 1
        out.append(((1 - x_) if fx else x_, (1 - y_) if fy else y_, (1 - c_) if fc else c_))
    return out


def _comm_copies(kind, in_refs, out_refs, send_sems, recv_sems, local_sems):
    me = _flat(*_my_pos())
    local, starts, waits = [], [], []
    for b, (i_ref, o_ref) in enumerate(zip(in_refs, out_refs)):
        local.append(pltpu.make_async_copy(i_ref if kind == "ag" else i_ref.at[me], o_ref.at[me], local_sems.at[b]))
        for k, p in enumerate(_peers()):
            src = i_ref if kind == "ag" else i_ref.at[_flat(*p)]
            starts.append(pltpu.make_async_remote_copy(
                src_ref=src, dst_ref=o_ref.at[me], send_sem=send_sems.at[b, k], recv_sem=recv_sems.at[b, k],
                device_id=p, device_id_type=MESH_ID))
            waits.append(pltpu.make_async_remote_copy(
                src_ref=src, dst_ref=o_ref.at[_flat(*p)], send_sem=send_sems.at[b, k], recv_sem=recv_sems.at[b, k],
                device_id=p, device_id_type=MESH_ID))
    return local, starts, waits


def pcall(body, comm, *, name, grid, in_specs, out_specs, out_shape, scratch_shapes=(), compiler_params=None):
    single = not isinstance(out_specs, (list, tuple))
    out_specs_l = [out_specs] if single else list(out_specs)
    out_shape_l = [out_shape] if single else list(out_shape)
    if comm is None:
        return pl.pallas_call(body, name=name, grid=grid, in_specs=in_specs, out_specs=out_specs, out_shape=out_shape,
                              scratch_shapes=list(scratch_shapes), compiler_params=compiler_params)
    kind, bufs = comm
    nb, n_in, n_out, n_scr = len(bufs), len(in_specs), len(out_specs_l), len(scratch_shapes)
    c_shape = [jax.ShapeDtypeStruct(((N_DEV,) + b.shape) if kind == "ag" else b.shape, b.dtype) for b in bufs]
    anyspec = pl.BlockSpec(memory_space=pl.ANY)

    def body2(*refs):
        ins, c_ins = refs[:n_in], refs[n_in:n_in + nb]
        outs = refs[n_in + nb:n_in + nb + n_out]
        c_outs = refs[n_in + nb + n_out:n_in + 2 * nb + n_out]
        scr = refs[n_in + 2 * nb + n_out:n_in + 2 * nb + n_out + n_scr]
        send_sems, recv_sems, local_sems = refs[n_in + 2 * nb + n_out + n_scr:]
        first = last = None
        for ax, g in enumerate(grid):
            pid = pl.program_id(ax)
            first = (pid == 0) if first is None else jnp.logical_and(first, pid == 0)
            last = (pid == g - 1) if last is None else jnp.logical_and(last, pid == g - 1)

        @pl.when(first)
        def _():
            local, starts, _ = _comm_copies(kind, c_ins, c_outs, send_sems, recv_sems, local_sems)
            for cp in local + starts:
                cp.start()

        body(*ins, *outs, *scr)

        @pl.when(last)
        def _():
            local, _, waits = _comm_copies(kind, c_ins, c_outs, send_sems, recv_sems, local_sems)
            for cp in waits + local:
                cp.wait()

    call = pl.pallas_call(
        body2, name=name, grid=grid, in_specs=list(in_specs) + [anyspec] * nb,
        out_specs=out_specs_l + [anyspec] * nb, out_shape=out_shape_l + c_shape,
        scratch_shapes=list(scratch_shapes) + [pltpu.SemaphoreType.DMA((nb, N_DEV - 1)),
                                               pltpu.SemaphoreType.DMA((nb, N_DEV - 1)), pltpu.SemaphoreType.DMA((nb,))],
        compiler_params=compiler_params)

    def run(*args):
        res = call(*args, *bufs)
        own = res[:n_out]
        return (own[0] if single else list(own)), list(res[n_out:])

    return run


def matmul(a, b, *, name, ta=False, tb=False, residual=None, out_dtype=F32, exact=False, tm=None, tn=None, tk=None,
           b_cols=None, comm=None):
    M, K = (a.shape[1], a.shape[0]) if ta else a.shape
    n0, N = b_cols if b_cols is not None else (0, b.shape[0] if tb else b.shape[1])
    tm = tm or _tile(M, (1024, 512, 256, 128))
    tn = tn or _tile(N, (1024, 768, 512, 384, 256, 128))
    tk = tk or _tile(K, (2048, 1024, 768, 512, 384, 256, 128))
    nk = K // tk
    dims = (((0,) if ta else (1,)), ((1,) if tb else (0,)))

    def body(*refs):
        if residual is None:
            a_ref, b_ref, o_ref, acc = refs
            r_ref = None
        else:
            a_ref, b_ref, r_ref, o_ref, acc = refs
        k = pl.program_id(2)

        @pl.when(k == 0)
        def _():
            acc[...] = jnp.zeros_like(acc)

        if exact:
            acc[...] += _dot_exact(a_ref[...], b_ref[...], dims)
        else:
            acc[...] += _dot(a_ref[...], b_ref[...], dims)

        @pl.when(k == nk - 1)
        def _():
            r = acc[...]
            if r_ref is not None:
                r = r + r_ref[...]
            o_ref[...] = r.astype(o_ref.dtype)

    a_spec = pl.BlockSpec((tk, tm), lambda i, j, k: (k, i)) if ta else pl.BlockSpec((tm, tk), lambda i, j, k: (i, k))
    assert n0 % tn == 0
    jb = n0 // tn
    b_spec = (pl.BlockSpec((tn, tk), lambda i, j, k: (j + jb, k)) if tb
              else pl.BlockSpec((tk, tn), lambda i, j, k: (k, j + jb)))
    in_specs = [a_spec, b_spec]
    args = [a, b]
    if residual is not None:
        in_specs.append(pl.BlockSpec((tm, tn), lambda i, j, k: (i, j)))
        args.append(residual)
    return pcall(
        body, comm, name=name, grid=(M // tm, N // tn, nk), in_specs=in_specs,
        out_specs=pl.BlockSpec((tm, tn), lambda i, j, k: (i, j)),
        out_shape=jax.ShapeDtypeStruct((M, N), out_dtype),
        scratch_shapes=[pltpu.VMEM((tm, tn), F32)],
        compiler_params=_cp("parallel", "parallel", "arbitrary"),
    )(*args)


def _row_specs2(rows, shared, R, W, ncb):
    specs = []
    for arr, col0, per_j, *wd in rows:
        w = wd[0] if wd else W
        if per_j:
            assert col0 % ncb == 0
            specs.append(pl.BlockSpec((R, ncb * w), lambda i, c=col0 // ncb: (i, c)))
        else:
            specs.append(pl.BlockSpec((R, w), lambda i, c=col0: (i, c)))
    for arr, per_j in shared:
        specs.append(pl.BlockSpec((arr.shape[0], ncb * W if per_j else arr.shape[1]), lambda i: (0, 0)))
    return specs


def _col_block(ref, per_j, j, w):
    return ref[:, j * w:(j + 1) * w] if per_j else ref[...]


def rowwise_fwd(name, fn, rows, shared, outs, *, W, ncb=1, R=256):
    S = rows[0][0].shape[0]
    R = min(R, S)
    nr, ns = len(rows), len(shared)
    widths = [(r[3] if len(r) > 3 else W) for r in rows]
    per_j = [r[2] for r in rows] + [s[1] for s in shared]
    ws = widths + [W] * ns

    def body(*refs):
        for j in range(ncb):
            vals = [_col_block(refs[k], per_j[k], j, ws[k]) for k in range(nr + ns)]
            res = fn(*vals)
            for o_ref, r, (wo, _) in zip(refs[nr + ns:], res, outs):
                o_ref[:, j * wo:(j + 1) * wo] = r.astype(o_ref.dtype)

    return pl.pallas_call(
        body, name=name, grid=(S // R,),
        in_specs=_row_specs2(rows, shared, R, W, ncb),
        out_specs=[pl.BlockSpec((R, ncb * w), lambda i: (i, 0)) for w, _ in outs],
        out_shape=[jax.ShapeDtypeStruct((S, ncb * w), dt) for w, dt in outs],
        compiler_params=_cp("parallel"),
    )(*[r[0] for r in rows], *[s[0] for s in shared])


def rowwise_bwd(name, fn, rows, shared, cots, *, W, ncb=1, R=256, diff_rows, diff_shared, add=None):
    S = rows[0][0].shape[0]
    R = min(R, S)
    nr, ns, nc = len(rows), len(shared), len(cots)
    widths = [(r[3] if len(r) > 3 else W) for r in rows]
    per_j = [r[2] for r in rows] + [s[1] for s in shared]
    ws = widths + [W] * ns
    wo = [c.shape[1] // ncb for c in cots]
    dws = [widths[r] for r in diff_rows]

    def body(*refs):
        ins = refs[:nr + ns]
        ct_refs = refs[nr + ns:nr + ns + nc]
        pos = nr + ns + nc
        add_ref = None
        if add is not None:
            add_ref = refs[pos]
            pos += 1
        drow_refs = refs[pos:pos + len(diff_rows)]
        dsh_refs = refs[pos + len(diff_rows):]
        i = pl.program_id(0)
        tot = [None] * len(diff_shared)
        for j in range(ncb):
            vals = [_col_block(ins[k], per_j[k], j, ws[k]) for k in range(nr + ns)]

            def f(*dv):
                full = list(vals)
                for idx, v in zip(list(diff_rows) + [nr + s for s in diff_shared], dv):
                    full[idx] = v
                return tuple(fn(*full))

            prim = [vals[idx] for idx in diff_rows] + [vals[nr + s] for s in diff_shared]
            _, vjp = jax.vjp(f, *prim)
            grads = vjp(tuple(c[:, j * w:(j + 1) * w] for c, w in zip(ct_refs, wo)))
            for k, (d_ref, w) in enumerate(zip(drow_refs, dws)):
                g = grads[k]
                if k == 0 and add_ref is not None:
                    g = g + add_ref[:, j * w:(j + 1) * w]
                d_ref[:, j * w:(j + 1) * w] = g
            for k, (d_ref, s) in enumerate(zip(dsh_refs, diff_shared)):
                g = grads[len(diff_rows) + k]
                if shared[s][1]:
                    @pl.when(i == 0)
                    def _(d_ref=d_ref, g=g, j=j):
                        d_ref[:, j * W:(j + 1) * W] = g

                    @pl.when(i != 0)
                    def _(d_ref=d_ref, g=g, j=j):
                        d_ref[:, j * W:(j + 1) * W] += g
                else:
                    tot[k] = g if tot[k] is None else tot[k] + g
        for k, (d_ref, s) in enumerate(zip(dsh_refs, diff_shared)):
            if not shared[s][1]:
                @pl.when(i == 0)
                def _(d_ref=d_ref, g=tot[k]):
                    d_ref[...] = g

                @pl.when(i != 0)
                def _(d_ref=d_ref, g=tot[k]):
                    d_ref[...] += g

    in_specs = _row_specs2(rows, shared, R, W, ncb)
    in_specs += [pl.BlockSpec((R, ncb * w), lambda i: (i, 0)) for w in wo]
    args = [r[0] for r in rows] + [s[0] for s in shared] + list(cots)
    if add is not None:
        in_specs.append(pl.BlockSpec((R, ncb * dws[0]), lambda i: (i, 0)))
        args.append(add)
    out_specs = [pl.BlockSpec((R, ncb * w), lambda i: (i, 0)) for w in dws]
    out_shape = [jax.ShapeDtypeStruct((S, ncb * w), F32) for w in dws]
    for s in diff_shared:
        arr, pj = shared[s]
        shp = (arr.shape[0], ncb * W if pj else arr.shape[1])
        out_specs.append(pl.BlockSpec(shp, lambda i: (0, 0)))
        out_shape.append(jax.ShapeDtypeStruct(shp, F32))
    return pl.pallas_call(
        body, name=name, grid=(S // R,), in_specs=in_specs, out_specs=out_specs, out_shape=out_shape,
        compiler_params=_cp("arbitrary"),
    )(*args)


def _rms(x, g):
    return x * lax.rsqrt(jnp.mean(x * x, axis=-1, keepdims=True) + EPS) * g


def _prenorm_fn(x, g):
    return (_rms(x, g),)


def loss_head(x, g, tgt, *, R=256):
    S, D = x.shape
    R = min(R, S)

    def fn(xv, gv, tv):
        err = _rms(xv, gv) - tv
        return 0.5 * jnp.sum(jnp.mean(err * err, axis=-1, keepdims=True), axis=0, keepdims=True)

    def body(x_ref, g_ref, t_ref, loss_ref, dx_ref, dg_ref):
        i = pl.program_id(0)
        tv = t_ref[...]
        val, vjp = jax.vjp(lambda a, b: fn(a, b, tv), x_ref[...], g_ref[...])
        dx, dg = vjp(jnp.ones((1, 1), F32))
        dx_ref[...] = dx

        @pl.when(i == 0)
        def _():
            loss_ref[...] = jnp.zeros_like(loss_ref) + val
            dg_ref[...] = dg

        @pl.when(i != 0)
        def _():
            loss_ref[...] += val
            dg_ref[...] += dg

    return pl.pallas_call(
        body, name="loss_head", grid=(S // R,),
        in_specs=[pl.BlockSpec((R, D), lambda i: (i, 0)), pl.BlockSpec((1, D), lambda i: (0, 0)),
                  pl.BlockSpec((R, D), lambda i: (i, 0))],
        out_specs=[pl.BlockSpec((1, 128), lambda i: (0, 0)), pl.BlockSpec((R, D), lambda i: (i, 0)),
                   pl.BlockSpec((1, D), lambda i: (0, 0))],
        out_shape=[jax.ShapeDtypeStruct((1, 128), F32), jax.ShapeDtypeStruct((S, D), F32),
                   jax.ShapeDtypeStruct((1, D), F32)],
        compiler_params=_cp("arbitrary"),
    )(x, g, tgt)


@jax.custom_vjp
def _softplus(x):
    z = jnp.exp(-jnp.abs(x))
    u = 1.0 + z
    log1p = jnp.where(u == 1.0, z, jnp.log(u) * (z / jnp.where(u == 1.0, 1.0, u - 1.0)))
    return jnp.maximum(x, 0.0) + log1p


def _softplus_fwd(x):
    return _softplus(x), x


def _softplus_bwd(x, ct):
    return (ct * jax.nn.sigmoid(x),)


_softplus.defvjp(_softplus_fwd, _softplus_bwd)


def _dt_fn(raw, bias):
    return (_softplus(raw + bias),)


CONV_CB = 256
CONV_RB = 512
CONV_PAD = 8


def ssd_conv_fwd(u, conv_w, conv_b):
    S = u.shape[0]
    ncb = SSD_CONV_CH // CONV_CB
    col0 = SSD_DI // CONV_CB
    RB = min(CONV_RB, S)

    def body(x_ref, w_ref, b_ref, o_ref, pad):
        pad[0:CONV_PAD, :] = jnp.zeros((CONV_PAD, CONV_CB), F32)
        pad[S + CONV_PAD:S + 2 * CONV_PAD, :] = jnp.zeros((CONV_PAD, CONV_CB), F32)
        pad[CONV_PAD:S + CONV_PAD, :] = x_ref[...]
        w = w_ref[...]
        b = b_ref[...]
        for r in range(S // RB):
            acc = jnp.zeros((RB, CONV_CB), F32) + b
            for k in range(SSD_CONV):
                off = r * RB + CONV_PAD + k - SSD_CONV // 2
                acc = acc + pad[off:off + RB, :] * w[k:k + 1, :]
            o_ref[r * RB:(r + 1) * RB, :] = _silu(acc)

    return pl.pallas_call(
        body, name="ssd_conv_fwd", grid=(ncb,),
        in_specs=[pl.BlockSpec((S, CONV_CB), lambda j: (0, col0 + j)),
                  pl.BlockSpec((SSD_CONV, CONV_CB), lambda j: (0, j)),
                  pl.BlockSpec((1, CONV_CB), lambda j: (0, j))],
        out_specs=pl.BlockSpec((S, CONV_CB), lambda j: (0, j)),
        out_shape=jax.ShapeDtypeStruct((S, SSD_CONV_CH), F32),
        scratch_shapes=[pltpu.VMEM((S + 2 * CONV_PAD, CONV_CB), F32)],
        compiler_params=_cp("parallel"),
    )(u, conv_w, conv_b)


def ssd_conv_bwd(u, conv_w, conv_b, dact):
    S = u.shape[0]
    ncb = SSD_CONV_CH // CONV_CB
    col0 = SSD_DI // CONV_CB
    RB = min(CONV_RB, S)
    half = SSD_CONV // 2

    def body(x_ref, w_ref, b_ref, da_ref, dx_ref, dw_ref, db_ref, xpad, dpad):
        z8 = jnp.zeros((CONV_PAD, CONV_CB), F32)
        xpad[0:CONV_PAD, :] = z8
        xpad[S + CONV_PAD:S + 2 * CONV_PAD, :] = z8
        dpad[0:CONV_PAD, :] = z8
        dpad[S + CONV_PAD:S + 2 * CONV_PAD, :] = z8
        xpad[CONV_PAD:S + CONV_PAD, :] = x_ref[...]
        w = w_ref[...]
        b = b_ref[...]
        dws = [jnp.zeros((1, CONV_CB), F32) for _ in range(SSD_CONV)]
        db = jnp.zeros((1, CONV_CB), F32)
        for r in range(S // RB):
            acc = jnp.zeros((RB, CONV_CB), F32) + b
            xs = []
            for k in range(SSD_CONV):
                off = r * RB + CONV_PAD + k - half
                xk = xpad[off:off + RB, :]
                xs.append(xk)
                acc = acc + xk * w[k:k + 1, :]
            sg = jax.nn.sigmoid(acc)
            dc = da_ref[r * RB:(r + 1) * RB, :] * (sg * (1.0 + acc * (1.0 - sg)))
            dpad[r * RB + CONV_PAD:(r + 1) * RB + CONV_PAD, :] = dc
            db = db + jnp.sum(dc, axis=0, keepdims=True)
            for k in range(SSD_CONV):
                dws[k] = dws[k] + jnp.sum(xs[k] * dc, axis=0, keepdims=True)
        for r in range(S // RB):
            acc = jnp.zeros((RB, CONV_CB), F32)
            for k in range(SSD_CONV):
                off = r * RB + CONV_PAD + half - k
                acc = acc + dpad[off:off + RB, :] * w[k:k + 1, :]
            dx_ref[r * RB:(r + 1) * RB, :] = acc
        for k in range(SSD_CONV):
            dw_ref[k:k + 1, :] = dws[k]
        dw_ref[SSD_CONV:SSD_CONV + 1, :] = jnp.zeros((1, CONV_CB), F32)
        db_ref[...] = db

    return pl.pallas_call(
        body, name="ssd_conv_bwd", grid=(ncb,),
        in_specs=[pl.BlockSpec((S, CONV_CB), lambda j: (0, col0 + j)),
                  pl.BlockSpec((SSD_CONV, CONV_CB), lambda j: (0, j)),
                  pl.BlockSpec((1, CONV_CB), lambda j: (0, j)),
                  pl.BlockSpec((S, CONV_CB), lambda j: (0, j))],
        out_specs=[pl.BlockSpec((S, CONV_CB), lambda j: (0, j)),
                   pl.BlockSpec((SSD_CONV + 1, CONV_CB), lambda j: (0, j)),
                   pl.BlockSpec((1, CONV_CB), lambda j: (0, j))],
        out_shape=[jax.ShapeDtypeStruct((S, SSD_CONV_CH), F32),
                   jax.ShapeDtypeStruct((SSD_CONV + 1, SSD_CONV_CH), F32),
                   jax.ShapeDtypeStruct((1, SSD_CONV_CH), F32)],
        scratch_shapes=[pltpu.VMEM((S + 2 * CONV_PAD, CONV_CB), F32), pltpu.VMEM((S + 2 * CONV_PAD, CONV_CB), F32)],
        compiler_params=_cp("parallel"),
    )(u, conv_w, conv_b, dact)


def _ssd_group_layout(t):
    r = t.shape[0]
    g = t[:, :2 * SSD_HEADS].reshape(r, 2, SSD_GROUPS, SSD_HPG).transpose(2, 0, 1, 3).reshape(SSD_GROUPS, r, 2 * SSD_HPG)
    return jnp.pad(g, ((0, 0), (0, 0), (0, 128 - 2 * SSD_HPG)))


def _ssd_head_layout(t):
    r = t.shape[1]
    h = t[:, :, :2 * SSD_HPG].reshape(SSD_GROUPS, r, 2, SSD_HPG).transpose(1, 2, 0, 3).reshape(r, 2 * SSD_HEADS)
    return jnp.pad(h, ((0, 0), (0, 128 - 2 * SSD_HEADS)))


def _ssd_chunk(state, x, Bg, Cg, dt, alog, dtr, alr, *, reverse):
    Q, P = SSD_CHUNK, SSD_HEADDIM
    r = lax.broadcasted_iota(jnp.int32, (Q, Q), 0)
    c = lax.broadcasted_iota(jnp.int32, (Q, Q), 1)
    keep = (c >= r) if reverse else (c <= r)
    cum_t = jnp.transpose(_cumsum_rows(dt * (-jnp.exp(alog)), reverse))
    cum = _cumsum_rows(dtr * (-jnp.exp(alr)), reverse)
    last = 0 if reverse else Q - 1
    cum_l = cum[last:last + 1, :]
    CB = _dot(Cg, Bg, ((1,), (1,)))
    yoff = _dot(Cg, state, ((1,), (0,))) * jnp.exp(cum)
    xdt = x * dtr
    ys = []
    for h in range(SSD_HPG):
        col = h + (SSD_HPG if reverse else 0)
        hs = slice(h * P, (h + 1) * P)
        cum_q = jnp.concatenate([cum[:, hs]] * (Q // P), axis=1)
        L = jnp.where(keep, jnp.exp(jnp.where(keep, cum_q - cum_t[col:col + 1, :], 0.0)), 0.0)
        ys.append(_dot(CB * L, xdt[:, hs], ((1,), (0,))))
    new_state = jnp.exp(cum_l) * state + _dot(Bg, xdt * jnp.exp(cum_l - cum), ((0,), (0,)))
    return new_state, jnp.concatenate(ys, axis=1) + yoff


def ssd_scan_fwd(act, dt, alog, dtr, alr, *, reverse, y_prev=None, comm=None):
    S = act.shape[0]
    Q, N, P = SSD_CHUNK, SSD_STATE, SSD_HEADDIM
    nc = S // Q
    GW = SSD_HPG * P

    def cidx(i):
        return (nc - 1 - i) if reverse else i

    def body(*refs):
        if y_prev is None:
            x_ref, b_ref, c_ref, dt_ref, al_ref, dtr_ref, alr_ref, y_ref, st_ref, state = refs
            yp_ref = None
        else:
            x_ref, b_ref, c_ref, dt_ref, al_ref, dtr_ref, alr_ref, yp_ref, y_ref, st_ref, state = refs
        i = pl.program_id(1)

        @pl.when(i == 0)
        def _():
            state[...] = jnp.zeros_like(state)

        st = state[...]
        st_ref[0, 0] = st
        ns, y = _ssd_chunk(st, x_ref[...], b_ref[...], c_ref[...], dt_ref[0], al_ref[0], dtr_ref[...], alr_ref[...],
                           reverse=reverse)
        state[...] = ns
        y_ref[...] = y if yp_ref is None else y + yp_ref[...]

    xspec = pl.BlockSpec((Q, GW), lambda g, i: (cidx(i), g))
    in_specs = [xspec,
                pl.BlockSpec((Q, N), lambda g, i: (cidx(i), SSD_DI // N + g)),
                pl.BlockSpec((Q, N), lambda g, i: (cidx(i), SSD_DI // N + SSD_GROUPS + g)),
                pl.BlockSpec((1, Q, 128), lambda g, i: (g, cidx(i), 0)),
                pl.BlockSpec((1, 1, 128), lambda g, i: (g, 0, 0)),
                xspec, pl.BlockSpec((1, GW), lambda g, i: (0, g))]
    args = [act, act, act, dt, alog, dtr, alr]
    if y_prev is not None:
        in_specs.append(xspec)
        args.append(y_prev)
    return pcall(
        body, comm, name=f"ssd_scan_fwd_{int(reverse)}", grid=(SSD_GROUPS, nc), in_specs=in_specs,
        out_specs=[xspec, pl.BlockSpec((1, 1, N, GW), lambda g, i: (cidx(i), g, 0, 0))],
        out_shape=[jax.ShapeDtypeStruct((S, SSD_DI), F32), jax.ShapeDtypeStruct((nc, SSD_GROUPS, N, GW), F32)],
        scratch_shapes=[pltpu.VMEM((N, GW), F32)],
        compiler_params=_cp("arbitrary", "arbitrary"),
    )(*args)


def ssd_scan_bwd(act, dt, alog, dtr, alr, states, dy, prev_x, *, reverse, prev=None, comm=None):
    S = act.shape[0]
    Q, N, P = SSD_CHUNK, SSD_STATE, SSD_HEADDIM
    nc = S // Q
    GW = SSD_HPG * P

    def cidx(i):
        return i if reverse else (nc - 1 - i)

    def body(*refs):
        x_ref, b_ref, c_ref, dt_ref, al_ref, dtr_ref, alr_ref, st_ref, dy_ref, px_ref = refs[:10]
        pos = 10
        if prev is not None:
            pb_ref, pc_ref, pdt_ref, pal_ref = refs[pos:pos + 4]
            pos += 4
        dx_ref, db_ref, dc_ref, ddt_ref, dal_ref, ddtr_ref, dalr_ref, dstate = refs[pos:]
        i = pl.program_id(1)

        @pl.when(i == 0)
        def _():
            dstate[...] = jnp.zeros_like(dstate)

        _, vjp = jax.vjp(functools.partial(_ssd_chunk, reverse=reverse), st_ref[0, 0], x_ref[...], b_ref[...],
                         c_ref[...], dt_ref[0], al_ref[0], dtr_ref[...], alr_ref[...])
        dst, dx, dB, dC, ddt, dal, ddtr, dalr = vjp((dstate[...], dy_ref[...]))
        dstate[...] = dst
        dx_ref[...] = dx + px_ref[...]
        if prev is not None:
            dB = dB + pb_ref[...]
            dC = dC + pc_ref[...]
            ddt = ddt + pdt_ref[0]
        db_ref[...] = dB
        dc_ref[...] = dC
        ddt_ref[0] = ddt
        ddtr_ref[...] = ddtr

        @pl.when(i == 0)
        def _():
            dal_ref[0] = dal + (pal_ref[0] if prev is not None else 0.0)
            dalr_ref[...] = dalr

        @pl.when(i != 0)
        def _():
            dal_ref[0] += dal
            dalr_ref[...] += dalr

    xspec = pl.BlockSpec((Q, GW), lambda g, i: (cidx(i), g))
    gspec = pl.BlockSpec((Q, N), lambda g, i: (cidx(i), g))
    dtspec = pl.BlockSpec((1, Q, 128), lambda g, i: (g, cidx(i), 0))
    alspec = pl.BlockSpec((1, 1, 128), lambda g, i: (g, 0, 0))
    alrspec = pl.BlockSpec((1, GW), lambda g, i: (0, g))
    in_specs = [xspec,
                pl.BlockSpec((Q, N), lambda g, i: (cidx(i), SSD_DI // N + g)),
                pl.BlockSpec((Q, N), lambda g, i: (cidx(i), SSD_DI // N + SSD_GROUPS + g)),
                dtspec, alspec, xspec, alrspec,
                pl.BlockSpec((1, 1, N, GW), lambda g, i: (cidx(i), g, 0, 0)), xspec, xspec]
    args = [act, act, act, dt, alog, dtr, alr, states, dy, prev_x]
    if prev is not None:
        in_specs += [gspec, gspec, dtspec, alspec]
        args += list(prev)
    return pcall(
        body, comm, name=f"ssd_scan_bwd_{int(reverse)}", grid=(SSD_GROUPS, nc), in_specs=in_specs,
        out_specs=[xspec, gspec, gspec, dtspec, alspec, xspec, alrspec],
        out_shape=[jax.ShapeDtypeStruct((S, SSD_DI), F32), jax.ShapeDtypeStruct((S, SSD_GROUPS * N), F32),
                   jax.ShapeDtypeStruct((S, SSD_GROUPS * N), F32),
                   jax.ShapeDtypeStruct((SSD_GROUPS, S, 128), F32), jax.ShapeDtypeStruct((SSD_GROUPS, 1, 128), F32),
                   jax.ShapeDtypeStruct((S, SSD_DI), F32), jax.ShapeDtypeStruct((1, SSD_DI), F32)],
        scratch_shapes=[pltpu.VMEM((N, GW), F32)],
        compiler_params=_cp("arbitrary", "arbitrary"),
    )(*args)


def _ssd_post_fn(y, xs, z, dexp, ng):
    t = (y + xs * dexp) * _silu(z)
    return (_rms(t, ng),)


def _cumsum_rows_impl(x, reverse):
    n = x.shape[0]
    row = lax.broadcasted_iota(jnp.int32, x.shape, 0)
    k = 1
    while k < n:
        if reverse:
            x = x + jnp.where(row < n - k, pltpu.roll(x, n - k, 0), 0.0)
        else:
            x = x + jnp.where(row >= k, pltpu.roll(x, k, 0), 0.0)
        k *= 2
    return x


@functools.partial(jax.custom_vjp, nondiff_argnums=(1,))
def _cumsum_rows(x, reverse):
    return _cumsum_rows_impl(x, reverse)


_cumsum_rows.defvjp(lambda x, reverse: (_cumsum_rows_impl(x, reverse), None),
                    lambda reverse, _, ct: (_cumsum_rows_impl(ct, not reverse),))


def _hg_chunk(state, qraw, fraw, v, lb, *, reverse):
    C = HG_CHUNK
    r = lax.broadcasted_iota(jnp.int32, (C, C), 0)
    c = lax.broadcasted_iota(jnp.int32, (C, C), 1)
    keep = (c >= r) if reverse else (c <= r)
    q = _silu(qraw)
    f = lb + (1.0 - lb) * jax.nn.sigmoid(fraw)
    k = 1.0 - f
    g = jnp.log(f)
    G = _cumsum_rows(g, reverse)
    ref_row = C // 2 - 1 if reverse else C // 2
    last_row = 0 if reverse else C - 1
    Gr = G[ref_row:ref_row + 1, :]
    Gl = G[last_row:last_row + 1, :]
    q_t = q * jnp.exp(G - Gr)
    k_t = k * jnp.exp(Gr - G)
    att = jnp.where(keep, _dot(q_t, k_t, ((1,), (1,))), 0.0)
    o = _dot(att, v, ((1,), (0,))) + _dot(q * jnp.exp(G), state, ((1,), (0,)))
    kd = k * jnp.exp(Gl - G)
    new_state = jnp.transpose(jnp.exp(Gl)) * state + _dot(kd, v, ((0,), (0,)))
    return new_state, o


def hg_scan_fwd(u, lb, *, reverse, o_prev=None, rows=256, comm=None):
    S = u.shape[0]
    nh = HG_HEADS
    rows = min(rows, S)
    nsteps = S // rows
    ncb = rows // HG_CHUNK
    f_sec = 2 if reverse else 1

    def blk(i):
        return (nsteps - 1 - i) if reverse else i

    def body(*refs):
        if o_prev is None:
            q_ref, f_ref, v_ref, lb_ref, o_ref, st_ref, state = refs
            op_ref = None
        else:
            q_ref, f_ref, v_ref, lb_ref, op_ref, o_ref, st_ref, state = refs
        i = pl.program_id(0)

        @pl.when(i == 0)
        def _():
            state[...] = jnp.zeros_like(state)

        def chunk(cc, carry):
            ci = (ncb - 1 - cc) if reverse else cc
            sl = pl.ds(pl.multiple_of(ci * HG_CHUNK, HG_CHUNK), HG_CHUNK)
            for h in range(nh):
                hs = slice(h * HG_D, (h + 1) * HG_D)
                st = state[h]
                st_ref[ci, h] = st
                ns, o = _hg_chunk(st, q_ref[sl, hs], f_ref[sl, hs], v_ref[sl, hs], lb_ref[:, hs], reverse=reverse)
                state[h] = ns
                if op_ref is not None:
                    o = o + op_ref[sl, hs]
                o_ref[sl, hs] = o
            return carry

        lax.fori_loop(0, ncb, chunk, 0)

    rowspec = lambda sec: pl.BlockSpec((rows, HG_W), lambda i: (blk(i), sec))
    in_specs = [rowspec(0), rowspec(f_sec), rowspec(3), pl.BlockSpec((1, HG_W), lambda i: (0, 0))]
    args = [u, u, u, lb]
    if o_prev is not None:
        in_specs.append(rowspec(0))
        args.append(o_prev)
    return pcall(
        body, comm, name=f"hg_scan_fwd_{int(reverse)}", grid=(nsteps,), in_specs=in_specs,
        out_specs=[rowspec(0), pl.BlockSpec((ncb, nh, HG_D, HG_D), lambda i: (blk(i), 0, 0, 0))],
        out_shape=[jax.ShapeDtypeStruct((S, HG_W), F32), jax.ShapeDtypeStruct((S // HG_CHUNK, nh, HG_D, HG_D), F32)],
        scratch_shapes=[pltpu.VMEM((nh, HG_D, HG_D), F32)],
        compiler_params=_cp("arbitrary"),
    )(*args)


def hg_scan_bwd(u, lb, states, do, *, reverse, prev=None, rows=256, comm=None):
    S = u.shape[0]
    nh = HG_HEADS
    rows = min(rows, S)
    nsteps = S // rows
    ncb = rows // HG_CHUNK
    f_sec = 2 if reverse else 1

    def blk(i):
        return i if reverse else (nsteps - 1 - i)

    def body(*refs):
        q_ref, f_ref, v_ref, lb_ref, st_ref, do_ref = refs[:6]
        pos = 6
        if prev is not None:
            pq_ref, pv_ref, plb_ref = refs[pos:pos + 3]
            pos += 3
        dq_ref, df_ref, dv_ref, dlb_ref, dstate = refs[pos:]
        i = pl.program_id(0)

        @pl.when(i == 0)
        def _():
            dstate[...] = jnp.zeros_like(dstate)
            dlb_ref[...] = plb_ref[...] if prev is not None else jnp.zeros_like(dlb_ref)

        def chunk(cc, carry):
            ci = cc if reverse else (ncb - 1 - cc)
            sl = pl.ds(pl.multiple_of(ci * HG_CHUNK, HG_CHUNK), HG_CHUNK)
            for h in range(nh):
                hs = slice(h * HG_D, (h + 1) * HG_D)
                _, vjp = jax.vjp(functools.partial(_hg_chunk, reverse=reverse), st_ref[ci, h],
                                 q_ref[sl, hs], f_ref[sl, hs], v_ref[sl, hs], lb_ref[:, hs])
                dst, dq, df, dv, dlb = vjp((dstate[h], do_ref[sl, hs]))
                dstate[h] = dst
                if prev is not None:
                    dq = dq + pq_ref[sl, hs]
                    dv = dv + pv_ref[sl, hs]
                dq_ref[sl, hs] = dq
                df_ref[sl, hs] = df
                dv_ref[sl, hs] = dv
                dlb_ref[:, hs] += dlb
            return carry

        lax.fori_loop(0, ncb, chunk, 0)

    rowspec = lambda sec: pl.BlockSpec((rows, HG_W), lambda i: (blk(i), sec))
    lbspec = pl.BlockSpec((1, HG_W), lambda i: (0, 0))
    in_specs = [rowspec(0), rowspec(f_sec), rowspec(3), lbspec,
                pl.BlockSpec((ncb, nh, HG_D, HG_D), lambda i: (blk(i), 0, 0, 0)), rowspec(0)]
    args = [u, u, u, lb, states, do]
    if prev is not None:
        in_specs += [rowspec(0), rowspec(0), lbspec]
        args += list(prev)
    return pcall(
        body, comm, name=f"hg_scan_bwd_{int(reverse)}", grid=(nsteps,), in_specs=in_specs,
        out_specs=[rowspec(0), rowspec(0), rowspec(0), lbspec],
        out_shape=[jax.ShapeDtypeStruct((S, HG_W), F32)] * 3 + [jax.ShapeDtypeStruct((1, HG_W), F32)],
        scratch_shapes=[pltpu.VMEM((nh, HG_D, HG_D), F32)],
        compiler_params=_cp("arbitrary"),
    )(*args)


def _hg_lb_fn(lbp):
    m = jnp.max(lbp, axis=0, keepdims=True)
    e = jnp.exp(lbp - m)
    sm = e / jnp.sum(e, axis=0, keepdims=True)
    return ((sm[0:1] + sm[1:2]) - sm[0:1],)


def hg_lb_fwd(lbp):
    def body(x_ref, o_ref):
        o_ref[...] = _hg_lb_fn(x_ref[...])[0]

    return pl.pallas_call(body, name="hg_lb_fwd", out_shape=jax.ShapeDtypeStruct((1, HG_W), F32))(lbp)


def hg_lb_bwd(lbp, dlb):
    def body(x_ref, d_ref, o_ref):
        _, vjp = jax.vjp(_hg_lb_fn, x_ref[...])
        o_ref[...] = vjp((d_ref[...],))[0]

    return pl.pallas_call(body, name="hg_lb_bwd", out_shape=jax.ShapeDtypeStruct(lbp.shape, F32))(lbp, dlb)


def _hg_post_fn(o, gate, ng):
    return (_rms(o, ng) * _silu(gate),)


def _gate_fn(o, gate):
    return (o * _silu(gate),)


def _rope_tables(S):
    t = np.arange(S)
    row = (t // GRID_W).astype(np.float32)
    col = (t % GRID_W).astype(np.float32)
    half = AT_HD // 4
    inv = (ROPE_THETA ** (-np.arange(0, 2 * half, 2, dtype=np.float32) / np.float32(2 * half))).astype(np.float32)
    ar = row[:, None] * inv[None, :]
    ac = col[:, None] * inv[None, :]
    return ar.astype(np.float32), ac.astype(np.float32)


@jax.custom_vjp
def _half_swap(x):
    ax = x.ndim - 1
    lane = lax.broadcasted_iota(jnp.int32, x.shape, ax)
    return jnp.where((lane & 32) == 0, pltpu.roll(x, 96, ax), pltpu.roll(x, 32, ax))


_half_swap.defvjp(lambda x: (_half_swap(x), None), lambda _, ct: (_half_swap(ct),))


def _make_qk_fn(scale):
    def fn(x, ct, st, g):
        n = _rms(x, g)
        return ((n * ct + _half_swap(n) * st) * scale,)
    return fn


def flash_fwd(q, k, v, *, v_col0=0, tq=256, comm=None):
    S = q.shape[0]
    tq = min(tq, S)
    G = AT_HEADS // AT_KV

    def body(q_ref, k_ref, v_ref, o_ref, lse_ref):
        kv, vv = k_ref[...], v_ref[...]
        for g in range(G):
            sl = slice(g * AT_HD, (g + 1) * AT_HD)
            s = _dot(q_ref[:, sl], kv, ((1,), (1,)))
            m = jnp.max(s, axis=1, keepdims=True)
            p = jnp.exp(s - m)
            l = jnp.sum(p, axis=1, keepdims=True)
            o_ref[:, sl] = _dot(p, vv, ((1,), (0,))) / l
            lse_ref[0, :, g:g + 1] = m + jnp.log(l)

    return pcall(
        body, comm, name="flash_fwd", grid=(AT_KV, S // tq),
        in_specs=[pl.BlockSpec((tq, G * AT_HD), lambda h, i: (i, h)),
                  pl.BlockSpec((S, AT_HD), lambda h, i: (0, h)),
                  pl.BlockSpec((S, AT_HD), lambda h, i: (0, v_col0 + h))],
        out_specs=[pl.BlockSpec((tq, G * AT_HD), lambda h, i: (i, h)),
                   pl.BlockSpec((1, tq, G), lambda h, i: (h, i, 0))],
        out_shape=[jax.ShapeDtypeStruct((S, AT_QW), F32), jax.ShapeDtypeStruct((AT_KV, S, G), F32)],
        compiler_params=_cp("parallel", "arbitrary"),
    )(q, k, v)


def flash_bwd_dq(q, k, v, o, lse, do, *, v_col0=0, tq=256):
    S = q.shape[0]
    tq = min(tq, S)
    G = AT_HEADS // AT_KV

    def body(q_ref, k_ref, v_ref, o_ref, lse_ref, do_ref, dq_ref, dl_ref):
        kv, vv = k_ref[...], v_ref[...]
        for g in range(G):
            sl = slice(g * AT_HD, (g + 1) * AT_HD)
            dog = do_ref[:, sl]
            delta = jnp.sum(dog * o_ref[:, sl], axis=1, keepdims=True)
            s = _dot(q_ref[:, sl], kv, ((1,), (1,)))
            p = jnp.exp(s - lse_ref[0, :, g:g + 1])
            dp = _dot(dog, vv, ((1,), (1,)))
            ds = p * (dp - delta)
            dq_ref[:, sl] = _dot(ds, kv, ((1,), (0,)))
            dl_ref[0, :, g:g + 1] = delta

    qspec = pl.BlockSpec((tq, G * AT_HD), lambda h, i: (i, h))
    lspec = pl.BlockSpec((1, tq, G), lambda h, i: (h, i, 0))
    return pl.pallas_call(
        body, name="flash_bwd_dq", grid=(AT_KV, S // tq),
        in_specs=[qspec, pl.BlockSpec((S, AT_HD), lambda h, i: (0, h)),
                  pl.BlockSpec((S, AT_HD), lambda h, i: (0, v_col0 + h)), qspec, lspec, qspec],
        out_specs=[qspec, lspec],
        out_shape=[jax.ShapeDtypeStruct((S, AT_QW), F32), jax.ShapeDtypeStruct((AT_KV, S, G), F32)],
        compiler_params=_cp("parallel", "arbitrary"),
    )(q, k, v, o, lse, do)


def flash_bwd_dkv(q, k, v, lse, delta, do, *, v_col0=0, tk=512, comm=None):
    S = q.shape[0]
    tk = min(tk, S)
    G = AT_HEADS // AT_KV

    def body(q_ref, k_ref, v_ref, lse_ref, dl_ref, do_ref, dk_ref, dv_ref):
        kv, vv = k_ref[...], v_ref[...]
        dk = jnp.zeros((tk, AT_HD), F32)
        dv = jnp.zeros((tk, AT_HD), F32)
        for g in range(G):
            sl = slice(g * AT_HD, (g + 1) * AT_HD)
            qg, dog = q_ref[:, sl], do_ref[:, sl]
            s = _dot(qg, kv, ((1,), (1,)))
            p = jnp.exp(s - lse_ref[0, :, g:g + 1])
            dv = dv + _dot(p, dog, ((0,), (0,)))
            dp = _dot(dog, vv, ((1,), (1,)))
            ds = p * (dp - dl_ref[0, :, g:g + 1])
            dk = dk + _dot(ds, qg, ((0,), (0,)))
        dk_ref[...] = dk
        dv_ref[...] = dv

    qspec = pl.BlockSpec((S, G * AT_HD), lambda h, j: (0, h))
    kspec = pl.BlockSpec((tk, AT_HD), lambda h, j: (j, h))
    lspec = pl.BlockSpec((1, S, G), lambda h, j: (h, 0, 0))
    return pcall(
        body, comm, name="flash_bwd_dkv", grid=(AT_KV, S // tk),
        in_specs=[qspec, kspec, pl.BlockSpec((tk, AT_HD), lambda h, j: (j, v_col0 + h)), lspec, lspec, qspec],
        out_specs=[kspec, kspec],
        out_shape=[jax.ShapeDtypeStruct((S, AT_KW), F32), jax.ShapeDtypeStruct((S, AT_KW), F32)],
        compiler_params=_cp("parallel", "arbitrary"),
    )(q, k, v, lse, delta, do)


def _t5_bucket_np(rel):
    half = REL_BUCKETS // 2
    exact = half // 2
    n = np.abs(rel)
    large = exact + (np.log(np.maximum(n, 1).astype(np.float32) / np.float32(exact))
                     / np.float32(math.log(REL_MAX_DIST / exact)) * np.float32(half - exact)).astype(np.int32)
    large = np.minimum(large, half - 1)
    return np.where(rel > 0, half, 0) + np.where(n < exact, n, large)


def _dl_tiles(Ls):
    T = min(128, Ls)
    return T, T + 2 * DL_HALF


def _dl_bucket_tables(dil, T):
    W = T + 2 * DL_HALF
    i = np.arange(T)[:, None]
    j = np.arange(W)[None, :]
    bq = _t5_bucket_np((j - DL_HALF - i) * dil)
    iw = np.arange(W)[:, None]
    jk = np.arange(T)[None, :]
    bk = _t5_bucket_np((jk + DL_HALF - iw) * dil)
    return bq.astype(np.int32), bk.astype(np.int32)


def _dl_merge_fn(o0, o1, o2, l0, l1, l2):
    m = jnp.maximum(jnp.maximum(l0, l1), l2)
    e0, e1, e2 = jnp.exp(l0 - m), jnp.exp(l1 - m), jnp.exp(l2 - m)
    den = e0 + e1 + e2
    return ((e0 / den) * o0 + (e1 / den) * o1 + (e2 / den) * o2,)


def _adamw_math(w, g, m, v):
    m = ADAM_B1 * m + (1.0 - ADAM_B1) * g
    v = ADAM_B2 * v + (1.0 - ADAM_B2) * (g * g)
    m_hat = m / (1.0 - ADAM_B1 ** ADAM_STEP)
    v_hat = v / (1.0 - ADAM_B2 ** ADAM_STEP)
    delta = -ADAM_LR * (m_hat / (jnp.sqrt(v_hat) + ADAM_EPS) + ADAM_WD * w)
    return delta, m, v


def adamw_sum(parts, w, m, v, *, name, R=128):
    rows, cols = w.shape
    R = min(R, rows)
    if rows % R:
        R = rows

    def body(p_ref, w_ref, m_ref, v_ref, g_ref, d_ref, nm_ref, nv_ref):
        g = p_ref[0].astype(F32)
        for s in range(1, N_DEV):
            g = g + p_ref[s].astype(F32)
        d, nm, nv = _adamw_math(w_ref[...], g, m_ref[...], v_ref[...])
        g_ref[...] = g
        d_ref[...] = d
        nm_ref[...] = nm
        nv_ref[...] = nv

    spec = pl.BlockSpec((R, cols), lambda i: (i, 0))
    return pl.pallas_call(
        body, name=name, grid=(rows // R,),
        in_specs=[pl.BlockSpec((N_DEV, R, cols), lambda i: (0, i, 0)), spec, spec, spec],
        out_specs=[spec] * 4, out_shape=[jax.ShapeDtypeStruct((rows, cols), F32)] * 4,
        compiler_params=_cp("parallel"),
    )(parts, w, m, v)


def sum_parts(parts, *, name):
    rows, cols = parts.shape[1:]

    def body(p_ref, o_ref):
        g = p_ref[0]
        for s in range(1, N_DEV):
            g = g + p_ref[s]
        o_ref[...] = g

    return pl.pallas_call(body, name=name, out_shape=jax.ShapeDtypeStruct((rows, cols), F32))(parts)


def adamw_plain(w, g, m, v, *, name):
    def body(w_ref, g_ref, m_ref, v_ref, d_ref, nm_ref, nv_ref):
        d, nm, nv = _adamw_math(w_ref[...], g_ref[...], m_ref[...], v_ref[...])
        d_ref[...] = d
        nm_ref[...] = nm
        nv_ref[...] = nv

    return pl.pallas_call(body, name=name, out_shape=[jax.ShapeDtypeStruct(w.shape, F32)] * 3)(w, g, m, v)


def allgather_two_level(x, *, name):
    R, C = x.shape

    def body(x_ref, out_ref, send_sems, recv_sems, local_sem):
        x_, y_, c_ = _my_pos()
        me, sibling = (x_, y_, c_), (x_, y_, 1 - c_)
        chips = [(1 - x_, y_), (x_, 1 - y_), (1 - x_, 1 - y_)]

        def rows(p):
            return out_ref.at[_flat(*p)]

        def copy(k, block, to, src=None):
            return pltpu.make_async_remote_copy(
                src_ref=rows(block) if src is None else src, dst_ref=rows(block),
                send_sem=send_sems.at[k], recv_sem=recv_sems.at[k], device_id=to, device_id_type=MESH_ID)

        mine = pltpu.make_async_copy(x_ref, rows(me), local_sem)
        mine.start()
        first = [copy(0, me, sibling, src=x_ref)]
        first += [copy(1 + j, me, (*chip, c_), src=x_ref) for j, chip in enumerate(chips)]
        for cp in first:
            cp.start()
        passed = [copy(4 + j, (*chip, c_), sibling) for j, chip in enumerate(chips)]
        for j, chip in enumerate(chips):
            copy(1 + j, (*chip, c_), me).wait_recv()
            passed[j].start()
        copy(0, sibling, me).wait_recv()
        for j, chip in enumerate(chips):
            copy(4 + j, (*chip, 1 - c_), me).wait_recv()
        for cp in first + passed:
            cp.wait_send()
        mine.wait()

    return pl.pallas_call(
        body, name=name,
        out_shape=jax.ShapeDtypeStruct((N_DEV, R, C), x.dtype),
        in_specs=[pl.BlockSpec(memory_space=pl.ANY)],
        out_specs=pl.BlockSpec(memory_space=pl.ANY),
        scratch_shapes=[pltpu.SemaphoreType.DMA((7,)), pltpu.SemaphoreType.DMA((7,)), pltpu.SemaphoreType.DMA],
    )(x)


def allgather_two_level_multi(xs, *, name):
    nb = len(xs)

    def body(*refs):
        x_refs, out_refs = refs[:nb], refs[nb:2 * nb]
        send_sems, recv_sems, local_sems = refs[2 * nb:]
        x_, y_, c_ = _my_pos()
        me, sibling = (x_, y_, c_), (x_, y_, 1 - c_)
        chips = [(1 - x_, y_), (x_, 1 - y_), (1 - x_, 1 - y_)]

        def copy(b, k, block, to, own=False):
            rows = out_refs[b].at[_flat(*block)]
            return pltpu.make_async_remote_copy(
                src_ref=x_refs[b] if own else rows, dst_ref=rows,
                send_sem=send_sems.at[b, k], recv_sem=recv_sems.at[b, k], device_id=to, device_id_type=MESH_ID)

        mine = [pltpu.make_async_copy(x_refs[b], out_refs[b].at[_flat(*me)], local_sems.at[b]) for b in range(nb)]
        first = []
        for b in range(nb):
            first.append(copy(b, 0, me, sibling, own=True))
            first += [copy(b, 1 + j, me, (*chip, c_), own=True) for j, chip in enumerate(chips)]
        for cp in mine + first:
            cp.start()
        passed = []
        for j, chip in enumerate(chips):
            for b in range(nb):
                copy(b, 1 + j, (*chip, c_), me).wait_recv()
                fwd = copy(b, 4 + j, (*chip, c_), sibling)
                fwd.start()
                passed.append(fwd)
        for b in range(nb):
            copy(b, 0, sibling, me).wait_recv()
            for j, chip in enumerate(chips):
                copy(b, 4 + j, (*chip, 1 - c_), me).wait_recv()
        for cp in first + passed:
            cp.wait_send()
        for cp in mine:
            cp.wait()

    anyspec = pl.BlockSpec(memory_space=pl.ANY)
    return pl.pallas_call(
        body, name=name,
        out_shape=[jax.ShapeDtypeStruct((N_DEV,) + x.shape, x.dtype) for x in xs],
        in_specs=[anyspec] * nb, out_specs=[anyspec] * nb,
        scratch_shapes=[pltpu.SemaphoreType.DMA((nb, 7)), pltpu.SemaphoreType.DMA((nb, 7)), pltpu.SemaphoreType.DMA((nb,))],
    )(*xs)


def _prenorm(tag, x, ng):
    return rowwise_fwd(f"{tag}_prenorm", _prenorm_fn, [(x, 0, False)], [(ng, False)], [(D_MODEL, MXU_DTYPE)], W=D_MODEL)[0]


def _cat_mxu(parts):
    return jnp.concatenate([t.astype(MXU_DTYPE) for t in parts], axis=1)


def _in_out_bwd(tag, x, ng, hn, du, w_in, dx, tail_comm=None):
    dw_in = matmul(hn, du, ta=True, out_dtype=GRAD_WIRE_DTYPE, name=f"{tag}_dw_in")
    comm = None if tail_comm is None else tail_comm(dw_in)
    dhn, cres = _own(matmul(du, w_in, tb=True, name=f"{tag}_dhn", comm=comm), comm)
    dx_prev, dng = rowwise_bwd(f"{tag}_prenorm_bwd", _prenorm_fn, [(x, 0, False)], [(ng, False)], [dhn],
                               W=D_MODEL, diff_rows=[0], diff_shared=[0], add=dx)
    return (dx_prev, dng, dw_in) if tail_comm is None else (dx_prev, dng, dw_in, cres)


def _own(res, comm):
    return (res, None) if comm is None else res


def ssd_layer_fwd(x, ng, p, comm=None, comm1=None):
    hn = _prenorm("ssd", x, ng)
    u = matmul(hn, p["w_in"], name="ssd_in")
    act = ssd_conv_fwd(u, p["conv_w"], p["conv_b"])
    dt = rowwise_fwd("ssd_dt", _dt_fn, [(u, (SSD_DI + SSD_CONV_CH) // 128, False)], [(p["dt_bias"], False)],
                     [(128, F32)], W=128)[0]
    H = SSD_HEADS
    dtr = [jnp.repeat(dt[:, d * H:(d + 1) * H], SSD_HEADDIM, axis=1) for d in (0, 1)]
    alr = [jnp.repeat(p["alog"][:, d * H:(d + 1) * H], SSD_HEADDIM, axis=1) for d in (0, 1)]
    dt, alog = _ssd_group_layout(dt), _ssd_group_layout(p["alog"])
    (y0, st0), cres = _own(ssd_scan_fwd(act, dt, alog, dtr[0], alr[0], reverse=False, comm=comm), comm)
    (y, st1), cres1 = _own(ssd_scan_fwd(act, dt, alog, dtr[1], alr[1], reverse=True, y_prev=y0, comm=comm1), comm1)
    g = rowwise_fwd("ssd_post", _ssd_post_fn, [(y, 0, True), (act, 0, True), (u, 0, True)],
                    [(p["dexp"], True), (p["norm_g"], True)], [(512, MXU_DTYPE)], W=512, ncb=SSD_GROUPS)[0]
    xn = matmul(g, p["w_out"], residual=x, name="ssd_out")
    return xn, dict(x=x, ng=ng, hn=hn, u=u, act=act, dt=dt, alog=alog, dtr=dtr, alr=alr, y=y, st0=st0, st1=st1,
                    g=g), cres, cres1


def ssd_layer_bwd(sv, p, dx, comm=None, tail_comm=None):
    u, act, dt = sv["u"], sv["act"], sv["dt"]
    S = u.shape[0]
    dg = matmul(dx, p["w_out"], tb=True, name="ssd_dg")
    dw_out = matmul(sv["g"], dx, ta=True, out_dtype=GRAD_WIRE_DTYPE, name="ssd_dw_out")
    dy, dxs_skip, dz, ddexp, dnorm = rowwise_bwd(
        "ssd_post_bwd", _ssd_post_fn, [(sv["y"], 0, True), (act, 0, True), (u, 0, True)],
        [(p["dexp"], True), (p["norm_g"], True)], [dg], W=512, ncb=SSD_GROUPS, diff_rows=[0, 1, 2], diff_shared=[0, 1])
    dtr, alr = sv["dtr"], sv["alr"]
    (dxa, dB, dC, ddt, dal, ddtr0, dalr0), cres = _own(
        ssd_scan_bwd(act, dt, sv["alog"], dtr[0], alr[0], sv["st0"], dy, dxs_skip, reverse=False, comm=comm), comm)
    comm1 = None if tail_comm is None else tail_comm(None, dw_out)
    (dxa, dB, dC, ddt, dal, ddtr1, dalr1), tail_out = _own(
        ssd_scan_bwd(act, dt, sv["alog"], dtr[1], alr[1], sv["st1"], dy, dxa, reverse=True, prev=(dB, dC, ddt, dal),
                     comm=comm1), comm1)
    dact = jnp.concatenate([dxa, dB, dC], axis=1)
    dxbc, dconv_w, dconv_b = ssd_conv_bwd(u, p["conv_w"], p["conv_b"], dact)
    fold = jnp.asarray(np.repeat(np.eye(SSD_HEADS, dtype=np.float32), SSD_HEADDIM, axis=0))
    folded = [matmul(t, fold, exact=True, name=f"ssd_ddt_fold_{d}", tn=SSD_HEADS) for d, t in enumerate((ddtr0, ddtr1))]
    ddt_all = _ssd_head_layout(ddt) + jnp.pad(jnp.concatenate(folded, axis=1), ((0, 0), (0, 128 - 2 * SSD_HEADS)))
    dal_rep = jnp.concatenate([t.reshape(SSD_HEADS, SSD_HEADDIM).sum(axis=1) for t in (dalr0, dalr1)])[None, :]
    ddt_raw, ddt_bias = rowwise_bwd("ssd_dt_bwd", _dt_fn, [(u, (SSD_DI + SSD_CONV_CH) // 128, False)],
                                    [(p["dt_bias"], False)], [ddt_all], W=128, diff_rows=[0], diff_shared=[0])
    du = _cat_mxu([dz, dxbc, ddt_raw, jnp.zeros((S, SSD_IN_PAD - SSD_IN - 64), F32)])
    res = _in_out_bwd("ssd", sv["x"], sv["ng"], sv["hn"], du, p["w_in"], dx,
                      tail_comm=None if tail_comm is None else (lambda g_in: tail_comm(g_in[:, :SSD_IN], None)))
    dx_prev, dng, dw_in = res[:3]
    tail = res[3] + tail_out if tail_comm is not None else None
    grads = dict(
        w_in=dw_in[:, :SSD_IN], w_out=dw_out, conv_w=dconv_w[:SSD_CONV], conv_b=dconv_b,
        dt_bias=ddt_bias[:, :2 * SSD_HEADS], a_log=_ssd_head_layout(dal)[:, :2 * SSD_HEADS] + dal_rep,
        d=ddexp.reshape(SSD_HEADS, SSD_HEADDIM).sum(axis=1)[None, :], norm_g=dnorm, ng=dng)
    return dx_prev, grads, cres, tail


def hg_layer_fwd(x, ng, p, comm0=None, comm1=None):
    hn = _prenorm("hg", x, ng)
    u = matmul(hn, p["w_in"], name="hg_in")
    lb = hg_lb_fwd(p["hgrn_lb"])
    (o0, st0), cres0 = _own(hg_scan_fwd(u, lb, reverse=False, comm=comm0), comm0)
    (o, st1), cres1 = _own(hg_scan_fwd(u, lb, reverse=True, o_prev=o0, comm=comm1), comm1)
    g = rowwise_fwd("hg_post", _hg_post_fn, [(o, 0, True), (u, 4 * HG_HEADS, True)], [(p["norm_g"], True)],
                    [(HG_D, MXU_DTYPE)], W=HG_D, ncb=HG_HEADS)[0]
    xn = matmul(g, p["w_out"], residual=x, name="hg_out")
    return xn, dict(x=x, ng=ng, hn=hn, u=u, lb=lb, o=o, st0=st0, st1=st1, g=g), cres0, cres1


def hg_layer_bwd(sv, p, dx, comm=None):
    u, lb = sv["u"], sv["lb"]
    dg = matmul(dx, p["w_out"], tb=True, name="hg_dg")
    dw_out = matmul(sv["g"], dx, ta=True, out_dtype=GRAD_WIRE_DTYPE, name="hg_dw_out")
    do, dgate, dnorm = rowwise_bwd("hg_post_bwd", _hg_post_fn, [(sv["o"], 0, True), (u, 4 * HG_HEADS, True)],
                                   [(p["norm_g"], True)], [dg], W=HG_D, ncb=HG_HEADS, diff_rows=[0, 1], diff_shared=[0])
    (dq0, df0, dv0, dlb0), cres = _own(hg_scan_bwd(u, lb, sv["st0"], do, reverse=False, comm=comm), comm)
    dq, df1, dv, dlb = hg_scan_bwd(u, lb, sv["st1"], do, reverse=True, prev=(dq0, dv0, dlb0))
    du = _cat_mxu([dq, df0, df1, dv, dgate])
    dhgrn_lb = hg_lb_bwd(p["hgrn_lb"], dlb)
    dx_prev, dng, dw_in = _in_out_bwd("hg", sv["x"], sv["ng"], sv["hn"], du, p["w_in"], dx)
    return dx_prev, dict(w_in=dw_in, w_out=dw_out, norm_g=dnorm, hgrn_lb=dhgrn_lb, ng=dng), cres


def _rope_consts(S):
    ar, ac = _rope_tables(S)
    ct = np.concatenate([np.cos(ar), np.cos(ar), np.cos(ac), np.cos(ac)], axis=1).astype(np.float32)
    st = np.concatenate([-np.sin(ar), np.sin(ar), -np.sin(ac), np.sin(ac)], axis=1).astype(np.float32)
    return jnp.asarray(ct), jnp.asarray(st)


def _at_qk(tag, u, col0, nheads, scale, gain, consts, cot=None):
    ct, st = consts
    rows = [(u, col0, True), (ct, 0, False), (st, 0, False)]
    shared = [(gain, False)]
    if cot is None:
        return rowwise_fwd(f"at_{tag}", _make_qk_fn(scale), rows, shared, [(AT_HD, MXU_DTYPE)], W=AT_HD, ncb=nheads)[0]
    return rowwise_bwd(f"at_{tag}_bwd", _make_qk_fn(scale), rows, shared, [cot], W=AT_HD, ncb=nheads,
                       diff_rows=[0], diff_shared=[0])


def at_layer_fwd(x, ng, p, comm=None):
    S = x.shape[0]
    hn = _prenorm("at", x, ng)
    u = matmul(hn, p["w_in"], name="at_in")
    consts = _rope_consts(S)
    qr = _at_qk("q", u, 0, AT_HEADS, AT_HD ** -0.5, p["q_g"], consts)
    kr = _at_qk("k", u, AT_HEADS, AT_KV, 1.0, p["k_g"], consts)
    vc0 = (AT_QW + AT_KW) // AT_HD
    (o, lse), cres = _own(flash_fwd(qr, kr, u, v_col0=vc0, comm=comm), comm)
    g = rowwise_fwd("at_gate", _gate_fn, [(o, 0, True), (u, (AT_QW + 2 * AT_KW) // 1024, True)], [],
                    [(1024, MXU_DTYPE)], W=1024, ncb=AT_QW // 1024)[0]
    xn = matmul(g, p["w_out"], residual=x, name="at_out")
    return xn, dict(x=x, ng=ng, hn=hn, u=u, qr=qr, kr=kr, o=o, lse=lse, g=g), cres


def at_layer_bwd(sv, p, dx, comm=None):
    u, qr, kr = sv["u"], sv["qr"], sv["kr"]
    S = u.shape[0]
    consts = _rope_consts(S)
    vc0 = (AT_QW + AT_KW) // AT_HD
    dg = matmul(dx, p["w_out"], tb=True, name="at_dg")
    dw_out = matmul(sv["g"], dx, ta=True, out_dtype=GRAD_WIRE_DTYPE, name="at_dw_out")
    do, dgate = rowwise_bwd("at_gate_bwd", _gate_fn, [(sv["o"], 0, True), (u, (AT_QW + 2 * AT_KW) // 1024, True)], [],
                            [dg], W=1024, ncb=AT_QW // 1024, diff_rows=[0, 1], diff_shared=[])
    dqs, delta = flash_bwd_dq(qr, kr, u, sv["o"], sv["lse"], do, v_col0=vc0)
    (dkr, dv), cres = _own(flash_bwd_dkv(qr, kr, u, sv["lse"], delta, do, v_col0=vc0, comm=comm), comm)
    dq_raw, dqg = _at_qk("q", u, 0, AT_HEADS, AT_HD ** -0.5, p["q_g"], consts, cot=dqs)
    dk_raw, dkg = _at_qk("k", u, AT_HEADS, AT_KV, 1.0, p["k_g"], consts, cot=dkr)
    du = _cat_mxu([dq_raw, dk_raw, dv, dgate])
    dx_prev, dng, dw_in = _in_out_bwd("at", sv["x"], sv["ng"], sv["hn"], du, p["w_in"], dx)
    return dx_prev, dict(w_in=dw_in, w_out=dw_out, q_g=dqg, k_g=dkg, ng=dng), cres


def _to_stream(t, dil):
    S = t.shape[0]
    return t.reshape(S // dil, dil, DL_HEADS, DL_HD).transpose(2, 1, 0, 3)


def _from_stream(t):
    H, dil, Ls, E = t.shape
    return t.transpose(2, 1, 0, 3).reshape(Ls * dil, H * E)


def _stream_to_hm(t):
    H, dil, Ls, w = t.shape
    return t.transpose(0, 2, 1, 3).reshape(H * Ls * dil, w)


def _hm_to_stream(t, dil):
    w = t.shape[1]
    S = t.shape[0] // DL_HEADS
    return t.reshape(DL_HEADS, S // dil, dil, w).transpose(0, 2, 1, 3)


OX_LSE = DL_HD
DOX_LSE, DOX_DM = DL_HD, DL_HD + 32


def _win(p_ref, c_ref, n_ref, h, T):
    return jnp.concatenate([p_ref[h, 0, T - DL_HALF:T, :], c_ref[h, 0], n_ref[h, 0, 0:DL_HALF, :]], axis=0)


def _win_specs(T, E, nb):
    return [pl.BlockSpec((DL_HEADS, 1, T, E), lambda d, n: (0, d, jnp.maximum(n - 1, 0), 0)),
            pl.BlockSpec((DL_HEADS, 1, T, E), lambda d, n: (0, d, n, 0)),
            pl.BlockSpec((DL_HEADS, 1, T, E), lambda d, n: (0, d, jnp.minimum(n + 1, nb - 1), 0))]


def _band_mask_q(n, T, W, Ls):
    i = lax.broadcasted_iota(jnp.int32, (T, W), 0)
    j = lax.broadcasted_iota(jnp.int32, (T, W), 1)
    kpos = n * T + j - DL_HALF
    return (jnp.abs(j - DL_HALF - i) <= DL_HALF) & (kpos >= 0) & (kpos < Ls)


def band_fwd(q, k, v, bias, *, scale):
    H, dil, Ls, E = q.shape
    T, W = _dl_tiles(Ls)
    nb = Ls // T

    def body(q_ref, kp_ref, kc_ref, kn_ref, vp_ref, vc_ref, vn_ref, b_ref, ox_ref):
        n = pl.program_id(1)
        mask = _band_mask_q(n, T, W, Ls)
        for h in range(H):
            kw = _win(kp_ref, kc_ref, kn_ref, h, T)
            vw = _win(vp_ref, vc_ref, vn_ref, h, T)
            s = _dot(q_ref[h, 0], kw, ((1,), (1,))) * scale + b_ref[h]
            s = jnp.where(mask, s, NEG_BIG)
            m = jnp.max(s, axis=1, keepdims=True)
            lse = m + jnp.log(jnp.sum(jnp.exp(s - m), axis=1, keepdims=True))
            p = jnp.exp(s - lse)
            ox_ref[h, 0, :, 0:E] = _dot(p, vw, ((1,), (0,)))
            ox_ref[h, 0, :, E:2 * E] = lse + jnp.zeros((T, E), F32)

    cur = pl.BlockSpec((H, 1, T, E), lambda d, n: (0, d, n, 0))
    return pl.pallas_call(
        body, name=f"band_fwd_{dil}", grid=(dil, nb),
        in_specs=[cur] + _win_specs(T, E, nb) + _win_specs(T, E, nb) + [pl.BlockSpec((H, T, W), lambda d, n: (0, 0, 0))],
        out_specs=pl.BlockSpec((H, 1, T, 2 * E), lambda d, n: (0, d, n, 0)),
        out_shape=jax.ShapeDtypeStruct((H, dil, Ls, 2 * E), F32),
        compiler_params=_cp("parallel", "parallel"),
    )(q, k, k, k, v, v, v, bias)


def band_bwd_dq(q, k, v, bias, dox, *, scale):
    H, dil, Ls, E = q.shape
    T, W = _dl_tiles(Ls)
    nb = Ls // T

    def body(q_ref, kp_ref, kc_ref, kn_ref, vp_ref, vc_ref, vn_ref, b_ref, dox_ref, dq_ref, db_ref):
        d, n = pl.program_id(0), pl.program_id(1)
        mask = _band_mask_q(n, T, W, Ls)
        first = jnp.logical_and(d == 0, n == 0)

        @pl.when(first)
        def _():
            db_ref[...] = jnp.zeros_like(db_ref)

        for h in range(H):
            kw = _win(kp_ref, kc_ref, kn_ref, h, T)
            vw = _win(vp_ref, vc_ref, vn_ref, h, T)
            dox = dox_ref[h, 0]
            do, lse, dm = dox[:, 0:E], dox[:, DOX_LSE:DOX_LSE + 1], dox[:, DOX_DM:DOX_DM + 1]
            s = _dot(q_ref[h, 0], kw, ((1,), (1,))) * scale + b_ref[h]
            p = jnp.where(mask, jnp.exp(jnp.where(mask, s, 0.0) - lse), 0.0)
            dp = _dot(do, vw, ((1,), (1,)))
            ds = p * (dp - dm)
            dq_ref[h, 0] = (_dot(ds, kw, ((1,), (0,))) * scale).astype(dq_ref.dtype)
            db_ref[h] += ds

    cur = pl.BlockSpec((H, 1, T, E), lambda d, n: (0, d, n, 0))
    bspec = pl.BlockSpec((H, T, W), lambda d, n: (0, 0, 0))
    return pl.pallas_call(
        body, name=f"band_bwd_dq_{dil}", grid=(dil, nb),
        in_specs=[cur] + _win_specs(T, E, nb) + _win_specs(T, E, nb) + [bspec,
                  pl.BlockSpec((H, 1, T, 2 * E), lambda d, n: (0, d, n, 0))],
        out_specs=[cur, bspec],
        out_shape=[jax.ShapeDtypeStruct((H, dil, Ls, E), MXU_DTYPE), jax.ShapeDtypeStruct((H, T, W), F32)],
        compiler_params=_cp("arbitrary", "arbitrary"),
    )(q, k, k, k, v, v, v, bias, dox)


def band_bwd_dkv(q, k, v, bias_t, dox, *, scale):
    H, dil, Ls, E = k.shape
    T, W = _dl_tiles(Ls)
    nb = Ls // T

    def body(qp_ref, qc_ref, qn_ref, k_ref, v_ref, b_ref, dp_ref, dc_ref, dn_ref, dk_ref, dv_ref):
        n = pl.program_id(1)
        iw = lax.broadcasted_iota(jnp.int32, (W, T), 0)
        j = lax.broadcasted_iota(jnp.int32, (W, T), 1)
        qpos = n * T + iw - DL_HALF
        mask = (jnp.abs(j + DL_HALF - iw) <= DL_HALF) & (qpos >= 0) & (qpos < Ls)
        for h in range(H):
            qw = _win(qp_ref, qc_ref, qn_ref, h, T)
            doxw = _win(dp_ref, dc_ref, dn_ref, h, T)
            dow, lsew, dmw = doxw[:, 0:E], doxw[:, DOX_LSE:DOX_LSE + 1], doxw[:, DOX_DM:DOX_DM + 1]
            s = _dot(qw, k_ref[h, 0], ((1,), (1,))) * scale + b_ref[h]
            p = jnp.where(mask, jnp.exp(jnp.where(mask, s, 0.0) - lsew), 0.0)
            dv_ref[h, 0] = _dot(p, dow, ((0,), (0,))).astype(dv_ref.dtype)
            dp = _dot(dow, v_ref[h, 0], ((1,), (1,)))
            ds = p * (dp - dmw)
            dk_ref[h, 0] = (_dot(ds, qw, ((0,), (0,))) * scale).astype(dk_ref.dtype)

    cur = pl.BlockSpec((H, 1, T, E), lambda d, n: (0, d, n, 0))
    return pl.pallas_call(
        body, name=f"band_bwd_dkv_{dil}", grid=(dil, nb),
        in_specs=_win_specs(T, E, nb) + [cur, cur, pl.BlockSpec((H, W, T), lambda d, n: (0, 0, 0))]
        + _win_specs(T, 2 * E, nb),
        out_specs=[cur, cur],
        out_shape=[jax.ShapeDtypeStruct((H, dil, Ls, E), MXU_DTYPE)] * 2,
        compiler_params=_cp("parallel", "parallel"),
    )(q, q, q, k, v, bias_t, dox, dox, dox)


def dl_merge_fwd(oxs, *, R=1024):
    rows = oxs[0].shape[0]
    R = min(R, rows)
    E = DL_HD

    def body(a_ref, b_ref, c_ref, o_ref):
        vals = [r[...] for r in (a_ref, b_ref, c_ref)]
        o_ref[...] = _dl_merge_fn(*[t[:, 0:E] for t in vals], *[t[:, OX_LSE:OX_LSE + 1] for t in vals])[0]

    spec = pl.BlockSpec((R, 2 * E), lambda i: (i, 0))
    return pl.pallas_call(
        body, name="dl_merge", grid=(rows // R,), in_specs=[spec] * 3,
        out_specs=pl.BlockSpec((R, E), lambda i: (i, 0)), out_shape=jax.ShapeDtypeStruct((rows, E), F32),
        compiler_params=_cp("parallel"),
    )(*oxs)


def dl_merge_bwd(oxs, do, *, R=1024):
    rows = oxs[0].shape[0]
    R = min(R, rows)
    E = DL_HD

    def body(a_ref, b_ref, c_ref, do_ref, da_ref, db_ref, dc_ref):
        vals = [r[...] for r in (a_ref, b_ref, c_ref)]
        os_ = [t[:, 0:E] for t in vals]
        ls_ = [t[:, OX_LSE:OX_LSE + 1] for t in vals]
        _, vjp = jax.vjp(_dl_merge_fn, *os_, *ls_)
        g = vjp((do_ref[...],))
        for k, d_ref in enumerate((da_ref, db_ref, dc_ref)):
            dm = jnp.sum(g[k] * os_[k], axis=1, keepdims=True) - g[3 + k]
            d_ref[:, 0:E] = g[k]
            d_ref[:, DOX_LSE:DOX_DM] = ls_[k] + jnp.zeros((R, DOX_DM - DOX_LSE), F32)
            d_ref[:, DOX_DM:2 * E] = dm + jnp.zeros((R, 2 * E - DOX_DM), F32)

    spec = pl.BlockSpec((R, 2 * E), lambda i: (i, 0))
    return pl.pallas_call(
        body, name="dl_merge_bwd", grid=(rows // R,), in_specs=[spec] * 3 + [pl.BlockSpec((R, E), lambda i: (i, 0))],
        out_specs=[spec] * 3, out_shape=[jax.ShapeDtypeStruct((rows, 2 * E), F32)] * 3,
        compiler_params=_cp("parallel"),
    )(*oxs, do)


def _dl_bias_tables(rel_bias, dil, T):
    W = T + 2 * DL_HALF
    bq, bk = _dl_bucket_tables(dil, T)
    idx = np.concatenate([bq.reshape(-1), bk.reshape(-1)])
    onehot_t = (np.arange(REL_BUCKETS)[:, None] == idx[None, :]).astype(np.float32)
    tab = matmul(rel_bias.T, jnp.asarray(onehot_t), exact=True, name=f"dl_bias_{dil}", tm=DL_HEADS, tk=REL_BUCKETS,
                 tn=_tile(2 * T * W, (8192, 4096, 2048, 1024, 512, 256, 128)))
    return tab[:, :T * W].reshape(DL_HEADS, T, W), tab[:, T * W:].reshape(DL_HEADS, W, T), bq


def dl_layer_fwd(x, ng, p):
    S = x.shape[0]
    hn = _prenorm("dl", x, ng)
    nqkv = 3 * len(DL_PAIRS) * DL_W
    uqkv = matmul(hn, p["w_in"], name="dl_in_qkv", b_cols=(0, nqkv), out_dtype=MXU_DTYPE)
    ugate = matmul(hn, p["w_in"], name="dl_in_gate", b_cols=(nqkv, DL_W))
    scale = DL_HD ** -0.5
    per_group, ox_hm = [], []
    for gi, (window, dil) in enumerate(DL_PAIRS):
        base = gi * 3 * DL_W
        T, _ = _dl_tiles(S // dil)
        qs, ks, vs = [_to_stream(uqkv[:, base + c * DL_W:base + (c + 1) * DL_W], dil) for c in range(3)]
        bias, bias_t, bq = _dl_bias_tables(p["rel_bias"], dil, T)
        ox_s = band_fwd(qs, ks, vs, bias, scale=scale)
        per_group.append(dict(qs=qs, ks=ks, vs=vs, bias=bias, bias_t=bias_t, bq=bq, dil=dil))
        ox_hm.append(_stream_to_hm(ox_s))
    om = dl_merge_fwd(ox_hm)
    o = om.reshape(DL_HEADS, S, DL_HD).transpose(1, 0, 2).reshape(S, DL_W)
    g = rowwise_fwd("dl_gate", _gate_fn, [(o, 0, False), (ugate, 0, False)], [], [(DL_W, MXU_DTYPE)], W=DL_W)[0]
    xn = matmul(g, p["w_out"], residual=x, name="dl_out")
    return xn, dict(x=x, ng=ng, hn=hn, ugate=ugate, per_group=per_group, ox_hm=ox_hm, o=o, g=g)


def dl_layer_bwd(sv, p, dx):
    ugate = sv["ugate"]
    S = ugate.shape[0]
    scale = DL_HD ** -0.5
    dg = matmul(dx, p["w_out"], tb=True, name="dl_dg")
    dw_out = matmul(sv["g"], dx, ta=True, out_dtype=GRAD_WIRE_DTYPE, name="dl_dw_out")
    do, dgate = rowwise_bwd("dl_gate_bwd", _gate_fn, [(sv["o"], 0, False), (sv["ugate"], 0, False)], [], [dg], W=DL_W,
                            diff_rows=[0, 1], diff_shared=[])
    do_hm = do.reshape(S, DL_HEADS, DL_HD).transpose(1, 0, 2).reshape(DL_HEADS * S, DL_HD)
    dox_hm = dl_merge_bwd(sv["ox_hm"], do_hm)
    parts, dbs, onehots = [], [], []
    for gi, pg in enumerate(sv["per_group"]):
        dil = pg["dil"]
        T, W = _dl_tiles(S // dil)
        dox_s = _hm_to_stream(dox_hm[gi], dil)
        dq_s, dbias = band_bwd_dq(pg["qs"], pg["ks"], pg["vs"], pg["bias"], dox_s, scale=scale)
        dk_s, dv_s = band_bwd_dkv(pg["qs"], pg["ks"], pg["vs"], pg["bias_t"], dox_s, scale=scale)
        parts += [_from_stream(dq_s), _from_stream(dk_s), _from_stream(dv_s)]
        dbs.append(dbias.reshape(DL_HEADS, T * W))
        onehots.append((pg["bq"].reshape(-1)[:, None] == np.arange(REL_BUCKETS)[None, :]).astype(np.float32))
    drel = matmul(jnp.concatenate(dbs, axis=1), jnp.asarray(np.concatenate(onehots, axis=0)), exact=True,
                  name="dl_drel", tm=DL_HEADS, tn=REL_BUCKETS, tk=2048)
    du = _cat_mxu(parts + [dgate])
    dx_prev, dng, dw_in = _in_out_bwd("dl", sv["x"], sv["ng"], sv["hn"], du, p["w_in"], dx)
    return dx_prev, dict(w_in=dw_in, w_out=dw_out, rel_bias=drel.T, ng=dng)


WEIGHT_ORDER = ['norm_g', 'final_g', 'rel_bias', 'hgrn_lb', 'ssd_w_in', 'ssd_conv_w', 'ssd_conv_b', 'ssd_dt_bias',
                'ssd_a_log', 'ssd_d', 'ssd_norm_g', 'ssd_w_out', 'hg_w_in', 'hg_norm_g', 'hg_w_out', 'at_w_in',
                'at_q_norm_g', 'at_k_norm_g', 'at_w_out', 'dl_w_in', 'dl_w_out']
BIG_IN = ['ssd_w_in', 'hg_w_in', 'at_w_in', 'dl_w_in']
BIG_OUT = ['ssd_w_out', 'hg_w_out', 'at_w_out', 'dl_w_out']
BIG = BIG_IN + BIG_OUT
SMALL = [n for n in WEIGHT_ORDER if n not in BIG]
LANES = 128


def _pack(arrs):
    flat = jnp.concatenate([a.reshape(-1).astype(F32) for a in arrs])
    n = flat.shape[0]
    rows = -(-n // (8 * LANES)) * 8
    return jnp.pad(flat, (0, rows * LANES - n)).reshape(rows, LANES)


def _unpack(buf, shapes):
    flat = buf.reshape(-1)
    out, off = [], 0
    for shp in shapes:
        n = int(np.prod(shp)) if len(shp) else 1
        out.append(flat[off:off + n].reshape(shp))
        off += n
    return out


def kernel(x, norm_g, final_g, rel_bias, hgrn_lb, ssd_w_in, ssd_conv_w, ssd_conv_b, ssd_dt_bias, ssd_a_log, ssd_d, ssd_norm_g, ssd_w_out, hg_w_in, hg_norm_g, hg_w_out, at_w_in, at_q_norm_g, at_k_norm_g, at_w_out, dl_w_in, dl_w_out, loss_target, m_norm_g, m_final_g, m_rel_bias, m_hgrn_lb, m_ssd_w_in, m_ssd_conv_w, m_ssd_conv_b, m_ssd_dt_bias, m_ssd_a_log, m_ssd_d, m_ssd_norm_g, m_ssd_w_out, m_hg_w_in, m_hg_norm_g, m_hg_w_out, m_at_w_in, m_at_q_norm_g, m_at_k_norm_g, m_at_w_out, m_dl_w_in, m_dl_w_out, v_norm_g, v_final_g, v_rel_bias, v_hgrn_lb, v_ssd_w_in, v_ssd_conv_w, v_ssd_conv_b, v_ssd_dt_bias, v_ssd_a_log, v_ssd_d, v_ssd_norm_g, v_ssd_w_out, v_hg_w_in, v_hg_norm_g, v_hg_w_out, v_at_w_in, v_at_q_norm_g, v_at_k_norm_g, v_at_w_out, v_dl_w_in, v_dl_w_out):
    w = dict(norm_g=norm_g, final_g=final_g, rel_bias=rel_bias, hgrn_lb=hgrn_lb, ssd_w_in=ssd_w_in, ssd_conv_w=ssd_conv_w, ssd_conv_b=ssd_conv_b, ssd_dt_bias=ssd_dt_bias, ssd_a_log=ssd_a_log, ssd_d=ssd_d, ssd_norm_g=ssd_norm_g, ssd_w_out=ssd_w_out, hg_w_in=hg_w_in, hg_norm_g=hg_norm_g, hg_w_out=hg_w_out, at_w_in=at_w_in, at_q_norm_g=at_q_norm_g, at_k_norm_g=at_k_norm_g, at_w_out=at_w_out, dl_w_in=dl_w_in, dl_w_out=dl_w_out)
    m = dict(norm_g=m_norm_g, final_g=m_final_g, rel_bias=m_rel_bias, hgrn_lb=m_hgrn_lb, ssd_w_in=m_ssd_w_in, ssd_conv_w=m_ssd_conv_w, ssd_conv_b=m_ssd_conv_b, ssd_dt_bias=m_ssd_dt_bias, ssd_a_log=m_ssd_a_log, ssd_d=m_ssd_d, ssd_norm_g=m_ssd_norm_g, ssd_w_out=m_ssd_w_out, hg_w_in=m_hg_w_in, hg_norm_g=m_hg_norm_g, hg_w_out=m_hg_w_out, at_w_in=m_at_w_in, at_q_norm_g=m_at_q_norm_g, at_k_norm_g=m_at_k_norm_g, at_w_out=m_at_w_out, dl_w_in=m_dl_w_in, dl_w_out=m_dl_w_out)
    v = dict(norm_g=v_norm_g, final_g=v_final_g, rel_bias=v_rel_bias, hgrn_lb=v_hgrn_lb, ssd_w_in=v_ssd_w_in, ssd_conv_w=v_ssd_conv_w, ssd_conv_b=v_ssd_conv_b, ssd_dt_bias=v_ssd_dt_bias, ssd_a_log=v_ssd_a_log, ssd_d=v_ssd_d, ssd_norm_g=v_ssd_norm_g, ssd_w_out=v_ssd_w_out, hg_w_in=v_hg_w_in, hg_norm_g=v_hg_norm_g, hg_w_out=v_hg_w_out, at_w_in=v_at_w_in, at_q_norm_g=v_at_q_norm_g, at_k_norm_g=v_at_k_norm_g, at_w_out=v_at_w_out, dl_w_in=v_dl_w_in, dl_w_out=v_dl_w_out)
    me = 4 * lax.axis_index("x") + 2 * lax.axis_index("y") + lax.axis_index("c")
    xs = x[0]
    S = xs.shape[0]

    shard2d = {n: w[n][0] for n in BIG}
    wire = {n: shard2d[n].astype(MXU_DTYPE) for n in BIG}

    def ag(names):
        return ("ag", [wire[n] for n in names])

    def assemble(names, blks):
        out = {}
        for n, blk in zip(names, blks):
            r, c = shard2d[n].shape
            out[n] = blk.transpose(1, 0, 2).reshape(r, N_DEV * c) if n in BIG_IN else blk.reshape(N_DEV * r, c)
        return out

    def a2a(names, gw):
        bufs = []
        for n in names:
            r, c = shard2d[n].shape
            bufs.append(gw[n].reshape(r, N_DEV, c).transpose(1, 0, 2) if n in BIG_IN else gw[n].reshape(N_DEV, r, c))
        return ("a2a", bufs)

    ssd_w, hg_w, at_w, dl_w = (["ssd_w_in", "ssd_w_out"], ["hg_w_in", "hg_w_out"], ["at_w_in", "at_w_out"],
                               ["dl_w_in", "dl_w_out"])
    full = assemble(ssd_w, allgather_two_level_multi([wire[n] for n in ssd_w], name="allgather_ssd_weights"))
    ncw = ssd_conv_w.shape[2]
    nhg = hg_norm_g.shape[1]
    small_shard = jnp.zeros((8, 512), F32)
    small_shard = small_shard.at[:SSD_CONV, :ncw].set(ssd_conv_w[0]).at[SSD_CONV, :nhg].set(hg_norm_g[0])
    small_all = allgather_two_level(small_shard, name="allgather_small_weights")
    conv_w_full = small_all[:, :SSD_CONV, :ncw].transpose(1, 0, 2).reshape(SSD_CONV, N_DEV * ncw)
    hg_norm_full = small_all[:, SSD_CONV, :nhg].reshape(1, N_DEV * nhg)

    p_ssd = dict(w_in=jnp.pad(full["ssd_w_in"], ((0, 0), (0, SSD_IN_PAD - SSD_IN))), w_out=full["ssd_w_out"],
                 conv_w=conv_w_full, conv_b=ssd_conv_b,
                 dt_bias=jnp.pad(ssd_dt_bias.reshape(1, 2 * SSD_HEADS), ((0, 0), (0, 128 - 2 * SSD_HEADS))),
                 alog=jnp.pad(ssd_a_log.reshape(1, 2 * SSD_HEADS), ((0, 0), (0, 128 - 2 * SSD_HEADS))),
                 dexp=jnp.repeat(ssd_d.reshape(-1), SSD_HEADDIM)[None, :], norm_g=ssd_norm_g)
    x1, sv0, got, got_at_in = ssd_layer_fwd(xs, norm_g[0:1], p_ssd, comm=ag(hg_w), comm1=ag(["at_w_in"]))
    full.update(assemble(hg_w, got))
    p_hg = dict(w_in=full["hg_w_in"], w_out=full["hg_w_out"], norm_g=hg_norm_full, hgrn_lb=hgrn_lb)
    x2, sv1, got_at_out, _ = hg_layer_fwd(x1, norm_g[1:2], p_hg, comm0=ag(["at_w_out"]))
    full.update(assemble(at_w, got_at_in + got_at_out))
    p_at = dict(w_in=full["at_w_in"], w_out=full["at_w_out"], q_g=at_q_norm_g, k_g=at_k_norm_g)
    x3, sv2, got_dl = at_layer_fwd(x2, norm_g[2:3], p_at, comm=ag(dl_w))
    full.update(assemble(dl_w, got_dl))
    p_dl = dict(w_in=full["dl_w_in"], w_out=full["dl_w_out"], rel_bias=rel_bias)
    x4, sv3 = dl_layer_fwd(x3, norm_g[3:4], p_dl)
    loss_part, dx4, dfinal = loss_head(x4, final_g[None, :], loss_target[0])
    dx3, g3 = dl_layer_bwd(sv3, p_dl, dx4)
    dx2, g2, recv_dl = at_layer_bwd(sv2, p_at, dx3, comm=a2a(dl_w, dict(dl_w_in=g3["w_in"], dl_w_out=g3["w_out"])))
    dx1, g1, recv_at = hg_layer_bwd(sv1, p_hg, dx2, comm=a2a(at_w, dict(at_w_in=g2["w_in"], at_w_out=g2["w_out"])))
    dx0, g0, recv_hg, recv_ssd = ssd_layer_bwd(
        sv0, p_ssd, dx1, comm=a2a(hg_w, dict(hg_w_in=g1["w_in"], hg_w_out=g1["w_out"])),
        tail_comm=lambda g_in, g_out: (a2a(["ssd_w_in"], dict(ssd_w_in=g_in)) if g_out is None
                                       else a2a(["ssd_w_out"], dict(ssd_w_out=g_out))))
    recv = dict(zip(ssd_w + hg_w + at_w + dl_w, recv_ssd + recv_hg + recv_at + recv_dl))

    small_full = dict(
        norm_g=jnp.concatenate([g0["ng"], g1["ng"], g2["ng"], g3["ng"]], axis=0), final_g=dfinal[0],
        rel_bias=g3["rel_bias"], hgrn_lb=g1["hgrn_lb"], ssd_conv_w=g0["conv_w"][None], ssd_conv_b=g0["conv_b"],
        ssd_dt_bias=g0["dt_bias"].reshape(1, 2, SSD_HEADS), ssd_a_log=g0["a_log"].reshape(1, 2, SSD_HEADS),
        ssd_d=g0["d"], ssd_norm_g=g0["norm_g"], hg_norm_g=g1["norm_g"], at_q_norm_g=g2["q_g"], at_k_norm_g=g2["k_g"])
    packed = _pack([loss_part[0, 0:1]] + [small_full[n] for n in SMALL])
    summed = sum_parts(allgather_two_level(packed, name="allgather_small_grads"), name="sum_small_grads")
    parts = _unpack(summed, [()] + [small_full[n].shape for n in SMALL])
    loss = parts[0]
    gsmall = dict(zip(SMALL, parts[1:]))
    gsmall["ssd_conv_w"] = lax.dynamic_slice_in_dim(gsmall["ssd_conv_w"], me * ncw, ncw, axis=2)
    gsmall["hg_norm_g"] = lax.dynamic_slice_in_dim(gsmall["hg_norm_g"], me * nhg, nhg, axis=1)
    shapes = [w[n].shape for n in SMALL]
    d_p, m_p, v_p = adamw_plain(_pack([w[n] for n in SMALL]), _pack([gsmall[n] for n in SMALL]),
                                _pack([m[n] for n in SMALL]), _pack([v[n] for n in SMALL]), name="adamw_small")
    grads = dict(gsmall)
    deltas = dict(zip(SMALL, _unpack(d_p, shapes)))
    new_m = dict(zip(SMALL, _unpack(m_p, shapes)))
    new_v = dict(zip(SMALL, _unpack(v_p, shapes)))

    for n in BIG:
        gs, ds, ms, vs = adamw_sum(recv[n], shard2d[n], m[n][0], v[n][0], name=f"adamw_{n}")
        grads[n], deltas[n], new_m[n], new_v[n] = gs[None], ds[None], ms[None], vs[None]

    return (loss, dx0[None], *[grads[n] for n in WEIGHT_ORDER], *[deltas[n] for n in WEIGHT_ORDER],
            *[new_m[n] for n in WEIGHT_ORDER], *[new_v[n] for n in WEIGHT_ORDER])
```

```python
import functools
import math

import jax
import jax.numpy as jnp
import numpy as np
from jax import lax
from jax.experimental import pallas as pl
from jax.experimental.pallas import tpu as pltpu

F32 = jnp.float32
MXU_DTYPE = jnp.bfloat16
GRAD_WIRE_DTYPE = jnp.bfloat16
HIGHEST = lax.Precision.HIGHEST
MESH_ID = pl.DeviceIdType.MESH
N_DEV = 8

D_MODEL = 1024
EPS = 1e-6
NEG_BIG = -1e30

SSD_DI = 2048
SSD_HEADDIM = 64
SSD_HEADS = 32
SSD_GROUPS = 4
SSD_HPG = 8
SSD_STATE = 128
SSD_CONV = 7
SSD_CHUNK = 128
SSD_CONV_CH = SSD_DI + 2 * SSD_GROUPS * SSD_STATE
SSD_IN = SSD_DI + SSD_CONV_CH + 2 * SSD_HEADS
SSD_IN_PAD = 5376

HG_CHUNK = 32
HG_HEADS = 8
HG_D = 128
HG_W = 1024

AT_HEADS = 16
AT_KV = 8
AT_HD = 128
AT_QW = 2048
AT_KW = 1024
GRID_W = 64
ROPE_THETA = 10000.0

DL_PAIRS = ((128, 1), (512, 4), (2048, 16))
DL_HEADS = 16
DL_HD = 64
DL_W = 1024
DL_HALF = 64
REL_BUCKETS = 32
REL_MAX_DIST = 1024

ADAM_LR = 0.001
ADAM_B1 = 0.9
ADAM_B2 = 0.999
ADAM_EPS = 1e-08
ADAM_WD = 0.01
ADAM_STEP = 10

VMEM_LIMIT = 56 * 1024 * 1024


def _cp(*sem):
    return pltpu.CompilerParams(dimension_semantics=tuple(sem), vmem_limit_bytes=VMEM_LIMIT)


def _tile(n, cands=(1024, 768, 512, 384, 256, 128)):
    for c in cands:
        if n % c == 0:
            return c
    return n


def _dot(a, b, dims):
    return lax.dot_general(a.astype(MXU_DTYPE), b.astype(MXU_DTYPE), (dims, ((), ())), preferred_element_type=F32)


def _dot_exact(a, b, dims):
    return lax.dot_general(a, b, (dims, ((), ())), precision=HIGHEST, preferred_element_type=F32)


def _silu(x):
    return x * jax.nn.sigmoid(x)


def _my_pos():
    return lax.axis_index("x"), lax.axis_index("y"), lax.axis_index("c")


def _flat(px, py, pc):
    return 4 * px + 2 * py + pc


def _peers():
    x_, y_, c_ = _my_pos()
    out = []
    for k in range(1, N_DEV):
        fx, fy, fc = (k >> 2) & 1, (k >> 1) & 1, k & 1
        out.append(((1 - x_) if fx else x_, (1 - y_) if fy else y_, (1 - c_) if fc else c_))
    return out


def _comm_copies(kind, in_refs, out_refs, send_sems, recv_sems, local_sems):
    me = _flat(*_my_pos())
    local, starts, waits = [], [], []
    for b, (i_ref, o_ref) in enumerate(zip(in_refs, out_refs)):
        local.append(pltpu.make_async_copy(i_ref if kind == "ag" else i_ref.at[me], o_ref.at[me], local_sems.at[b]))
        for k, p in enumerate(_peers()):
            src = i_ref if kind == "ag" else i_ref.at[_flat(*p)]
            starts.append(pltpu.make_async_remote_copy(
                src_ref=src, dst_ref=o_ref.at[me], send_sem=send_sems.at[b, k], recv_sem=recv_sems.at[b, k],
                device_id=p, device_id_type=MESH_ID))
            waits.append(pltpu.make_async_remote_copy(
                src_ref=src, dst_ref=o_ref.at[_flat(*p)], send_sem=send_sems.at[b, k], recv_sem=recv_sems.at[b, k],
                device_id=p, device_id_type=MESH_ID))
    return local, starts, waits


def pcall(body, comm, *, name, grid, in_specs, out_specs, out_shape, scratch_shapes=(), compiler_params=None):
    single = not isinstance(out_specs, (list, tuple))
    out_specs_l = [out_specs] if single else list(out_specs)
    out_shape_l = [out_shape] if single else list(out_shape)
    if comm is None:
        return pl.pallas_call(body, name=name, grid=grid, in_specs=in_specs, out_specs=out_specs, out_shape=out_shape,
                              scratch_shapes=list(scratch_shapes), compiler_params=compiler_params)
    kind, bufs = comm
    nb, n_in, n_out, n_scr = len(bufs), len(in_specs), len(out_specs_l), len(scratch_shapes)
    c_shape = [jax.ShapeDtypeStruct(((N_DEV,) + b.shape) if kind == "ag" else b.shape, b.dtype) for b in bufs]
    anyspec = pl.BlockSpec(memory_space=pl.ANY)

    def body2(*refs):
        ins, c_ins = refs[:n_in], refs[n_in:n_in + nb]
        outs = refs[n_in + nb:n_in + nb + n_out]
        c_outs = refs[n_in + nb + n_out:n_in + 2 * nb + n_out]
        scr = refs[n_in + 2 * nb + n_out:n_in + 2 * nb + n_out + n_scr]
        send_sems, recv_sems, local_sems = refs[n_in + 2 * nb + n_out + n_scr:]
        first = last = None
        for ax, g in enumerate(grid):
            pid = pl.program_id(ax)
            first = (pid == 0) if first is None else jnp.logical_and(first, pid == 0)
            last = (pid == g - 1) if last is None else jnp.logical_and(last, pid == g - 1)

        @pl.when(first)
        def _():
            local, starts, _ = _comm_copies(kind, c_ins, c_outs, send_sems, recv_sems, local_sems)
            for cp in local + starts:
                cp.start()

        body(*ins, *outs, *scr)

        @pl.when(last)
        def _():
            local, _, waits = _comm_copies(kind, c_ins, c_outs, send_sems, recv_sems, local_sems)
            for cp in waits + local:
                cp.wait()

    call = pl.pallas_call(
        body2, name=name, grid=grid, in_specs=list(in_specs) + [anyspec] * nb,
        out_specs=out_specs_l + [anyspec] * nb, out_shape=out_shape_l + c_shape,
        scratch_shapes=list(scratch_shapes) + [pltpu.SemaphoreType.DMA((nb, N_DEV - 1)),
                                               pltpu.SemaphoreType.DMA((nb, N_DEV - 1)), pltpu.SemaphoreType.DMA((nb,))],
        compiler_params=compiler_params)

    def run(*args):
        res = call(*args, *bufs)
        own = res[:n_out]
        return (own[0] if single else list(own)), list(res[n_out:])

    return run


def matmul(a, b, *, name, ta=False, tb=False, residual=None, out_dtype=F32, exact=False, tm=None, tn=None, tk=None,
           b_cols=None, comm=None):
    M, K = (a.shape[1], a.shape[0]) if ta else a.shape
    n0, N = b_cols if b_cols is not None else (0, b.shape[0] if tb else b.shape[1])
    tm = tm or _tile(M, (1024, 512, 256, 128))
    tn = tn or _tile(N, (1024, 768, 512, 384, 256, 128))
    tk = tk or _tile(K, (2048, 1024, 768, 512, 384, 256, 128))
    nk = K // tk
    dims = (((0,) if ta else (1,)), ((1,) if tb else (0,)))

    def body(*refs):
        if residual is None:
            a_ref, b_ref, o_ref, acc = refs
            r_ref = None
        else:
            a_ref, b_ref, r_ref, o_ref, acc = refs
        k = pl.program_id(2)

        @pl.when(k == 0)
        def _():
            acc[...] = jnp.zeros_like(acc)

        if exact:
            acc[...] += _dot_exact(a_ref[...], b_ref[...], dims)
        else:
            acc[...] += _dot(a_ref[...], b_ref[...], dims)

        @pl.when(k == nk - 1)
        def _():
            r = acc[...]
            if r_ref is not None:
                r = r + r_ref[...]
            o_ref[...] = r.astype(o_ref.dtype)

    a_spec = pl.BlockSpec((tk, tm), lambda i, j, k: (k, i)) if ta else pl.BlockSpec((tm, tk), lambda i, j, k: (i, k))
    assert n0 % tn == 0
    jb = n0 // tn
    b_spec = (pl.BlockSpec((tn, tk), lambda i, j, k: (j + jb, k)) if tb
              else pl.BlockSpec((tk, tn), lambda i, j, k: (k, j + jb)))
    in_specs = [a_spec, b_spec]
    args = [a, b]
    if residual is not None:
        in_specs.append(pl.BlockSpec((tm, tn), lambda i, j, k: (i, j)))
        args.append(residual)
    return pcall(
        body, comm, name=name, grid=(M // tm, N // tn, nk), in_specs=in_specs,
        out_specs=pl.BlockSpec((tm, tn), lambda i, j, k: (i, j)),
        out_shape=jax.ShapeDtypeStruct((M, N), out_dtype),
        scratch_shapes=[pltpu.VMEM((tm, tn), F32)],
        compiler_params=_cp("parallel", "parallel", "arbitrary"),
    )(*args)


def _row_specs2(rows, shared, R, W, ncb):
    specs = []
    for arr, col0, per_j, *wd in rows:
        w = wd[0] if wd else W
        if per_j:
            assert col0 % ncb == 0
            specs.append(pl.BlockSpec((R, ncb * w), lambda i, c=col0 // ncb: (i, c)))
        else:
            specs.append(pl.BlockSpec((R, w), lambda i, c=col0: (i, c)))
    for arr, per_j in shared:
        specs.append(pl.BlockSpec((arr.shape[0], ncb * W if per_j else arr.shape[1]), lambda i: (0, 0)))
    return specs


def _col_block(ref, per_j, j, w):
    return ref[:, j * w:(j + 1) * w] if per_j else ref[...]


def rowwise_fwd(name, fn, rows, shared, outs, *, W, ncb=1, R=256):
    S = rows[0][0].shape[0]
    R = min(R, S)
    nr, ns = len(rows), len(shared)
    widths = [(r[3] if len(r) > 3 else W) for r in rows]
    per_j = [r[2] for r in rows] + [s[1] for s in shared]
    ws = widths + [W] * ns

    def body(*refs):
        for j in range(ncb):
            vals = [_col_block(refs[k], per_j[k], j, ws[k]) for k in range(nr + ns)]
            res = fn(*vals)
            for o_ref, r, (wo, _) in zip(refs[nr + ns:], res, outs):
                o_ref[:, j * wo:(j + 1) * wo] = r.astype(o_ref.dtype)

    return pl.pallas_call(
        body, name=name, grid=(S // R,),
        in_specs=_row_specs2(rows, shared, R, W, ncb),
        out_specs=[pl.BlockSpec((R, ncb * w), lambda i: (i, 0)) for w, _ in outs],
        out_shape=[jax.ShapeDtypeStruct((S, ncb * w), dt) for w, dt in outs],
        compiler_params=_cp("parallel"),
    )(*[r[0] for r in rows], *[s[0] for s in shared])


def rowwise_bwd(name, fn, rows, shared, cots, *, W, ncb=1, R=256, diff_rows, diff_shared, add=None):
    S = rows[0][0].shape[0]
    R = min(R, S)
    nr, ns, nc = len(rows), len(shared), len(cots)
    widths = [(r[3] if len(r) > 3 else W) for r in rows]
    per_j = [r[2] for r in rows] + [s[1] for s in shared]
    ws = widths + [W] * ns
    wo = [c.shape[1] // ncb for c in cots]
    dws = [widths[r] for r in diff_rows]

    def body(*refs):
        ins = refs[:nr + ns]
        ct_refs = refs[nr + ns:nr + ns + nc]
        pos = nr + ns + nc
        add_ref = None
        if add is not None:
            add_ref = refs[pos]
            pos += 1
        drow_refs = refs[pos:pos + len(diff_rows)]
        dsh_refs = refs[pos + len(diff_rows):]
        i = pl.program_id(0)
        tot = [None] * len(diff_shared)
        for j in range(ncb):
            vals = [_col_block(ins[k], per_j[k], j, ws[k]) for k in range(nr + ns)]

            def f(*dv):
                full = list(vals)
                for idx, v in zip(list(diff_rows) + [nr + s for s in diff_shared], dv):
                    full[idx] = v
                return tuple(fn(*full))

            prim = [vals[idx] for idx in diff_rows] + [vals[nr + s] for s in diff_shared]
            _, vjp = jax.vjp(f, *prim)
            grads = vjp(tuple(c[:, j * w:(j + 1) * w] for c, w in zip(ct_refs, wo)))
            for k, (d_ref, w) in enumerate(zip(drow_refs, dws)):
                g = grads[k]
                if k == 0 and add_ref is not None:
                    g = g + add_ref[:, j * w:(j + 1) * w]
                d_ref[:, j * w:(j + 1) * w] = g
            for k, (d_ref, s) in enumerate(zip(dsh_refs, diff_shared)):
                g = grads[len(diff_rows) + k]
                if shared[s][1]:
                    @pl.when(i == 0)
                    def _(d_ref=d_ref, g=g, j=j):
                        d_ref[:, j * W:(j + 1) * W] = g

                    @pl.when(i != 0)
                    def _(d_ref=d_ref, g=g, j=j):
                        d_ref[:, j * W:(j + 1) * W] += g
                else:
                    tot[k] = g if tot[k] is None else tot[k] + g
        for k, (d_ref, s) in enumerate(zip(dsh_refs, diff_shared)):
            if not shared[s][1]:
                @pl.when(i == 0)
                def _(d_ref=d_ref, g=tot[k]):
                    d_ref[...] = g

                @pl.when(i != 0)
                def _(d_ref=d_ref, g=tot[k]):
                    d_ref[...] += g

    in_specs = _row_specs2(rows, shared, R, W, ncb)
    in_specs += [pl.BlockSpec((R, ncb * w), lambda i: (i, 0)) for w in wo]
    args = [r[0] for r in rows] + [s[0] for s in shared] + list(cots)
    if add is not None:
        in_specs.append(pl.BlockSpec((R, ncb * dws[0]), lambda i: (i, 0)))
        args.append(add)
    out_specs = [pl.BlockSpec((R, ncb * w), lambda i: (i, 0)) for w in dws]
    out_shape = [jax.ShapeDtypeStruct((S, ncb * w), F32) for w in dws]
    for s in diff_shared:
        arr, pj = shared[s]
        shp = (arr.shape[0], ncb * W if pj else arr.shape[1])
        out_specs.append(pl.BlockSpec(shp, lambda i: (0, 0)))
        out_shape.append(jax.ShapeDtypeStruct(shp, F32))
    return pl.pallas_call(
        body, name=name, grid=(S // R,), in_specs=in_specs, out_specs=out_specs, out_shape=out_shape,
        compiler_params=_cp("arbitrary"),
    )(*args)


def _rms(x, g):
    return x * lax.rsqrt(jnp.mean(x * x, axis=-1, keepdims=True) + EPS) * g


def _prenorm_fn(x, g):
    return (_rms(x, g),)


def loss_head(x, g, tgt, *, R=256):
    S, D = x.shape
    R = min(R, S)

    def fn(xv, gv, tv):
        err = _rms(xv, gv) - tv
        return 0.5 * jnp.sum(jnp.mean(err * err, axis=-1, keepdims=True), axis=0, keepdims=True)

    def body(x_ref, g_ref, t_ref, loss_ref, dx_ref, dg_ref):
        i = pl.program_id(0)
        tv = t_ref[...]
        val, vjp = jax.vjp(lambda a, b: fn(a, b, tv), x_ref[...], g_ref[...])
        dx, dg = vjp(jnp.ones((1, 1), F32))
        dx_ref[...] = dx

        @pl.when(i == 0)
        def _():
            loss_ref[...] = jnp.zeros_like(loss_ref) + val
            dg_ref[...] = dg

        @pl.when(i != 0)
        def _():
            loss_ref[...] += val
            dg_ref[...] += dg

    return pl.pallas_call(
        body, name="loss_head", grid=(S // R,),
        in_specs=[pl.BlockSpec((R, D), lambda i: (i, 0)), pl.BlockSpec((1, D), lambda i: (0, 0)),
                  pl.BlockSpec((R, D), lambda i: (i, 0))],
        out_specs=[pl.BlockSpec((1, 128), lambda i: (0, 0)), pl.BlockSpec((R, D), lambda i: (i, 0)),
                   pl.BlockSpec((1, D), lambda i: (0, 0))],
        out_shape=[jax.ShapeDtypeStruct((1, 128), F32), jax.ShapeDtypeStruct((S, D), F32),
                   jax.ShapeDtypeStruct((1, D), F32)],
        compiler_params=_cp("arbitrary"),
    )(x, g, tgt)


@jax.custom_vjp
def _softplus(x):
    z = jnp.exp(-jnp.abs(x))
    u = 1.0 + z
    log1p = jnp.where(u == 1.0, z, jnp.log(u) * (z / jnp.where(u == 1.0, 1.0, u - 1.0)))
    return jnp.maximum(x, 0.0) + log1p


def _softplus_fwd(x):
    return _softplus(x), x


def _softplus_bwd(x, ct):
    return (ct * jax.nn.sigmoid(x),)


_softplus.defvjp(_softplus_fwd, _softplus_bwd)


def _dt_fn(raw, bias):
    return (_softplus(raw + bias),)


CONV_CB = 128
CONV_RB = 256
CONV_PAD = 8


def ssd_conv_fwd(u, conv_w, conv_b):
    S = u.shape[0]
    ncb = SSD_CONV_CH // CONV_CB
    col0 = SSD_DI // CONV_CB
    RB = min(CONV_RB, S)

    def body(x_ref, w_ref, b_ref, o_ref, pad):
        pad[0:CONV_PAD, :] = jnp.zeros((CONV_PAD, CONV_CB), F32)
        pad[S + CONV_PAD:S + 2 * CONV_PAD, :] = jnp.zeros((CONV_PAD, CONV_CB), F32)
        pad[CONV_PAD:S + CONV_PAD, :] = x_ref[...]
        w = w_ref[...]
        b = b_ref[...]
        for r in range(S // RB):
            acc = jnp.zeros((RB, CONV_CB), F32) + b
            for k in range(SSD_CONV):
                off = r * RB + CONV_PAD + k - SSD_CONV // 2
                acc = acc + pad[off:off + RB, :] * w[k:k + 1, :]
            o_ref[r * RB:(r + 1) * RB, :] = _silu(acc)

    return pl.pallas_call(
        body, name="ssd_conv_fwd", grid=(ncb,),
        in_specs=[pl.BlockSpec((S, CONV_CB), lambda j: (0, col0 + j)),
                  pl.BlockSpec((SSD_CONV, CONV_CB), lambda j: (0, j)),
                  pl.BlockSpec((1, CONV_CB), lambda j: (0, j))],
        out_specs=pl.BlockSpec((S, CONV_CB), lambda j: (0, j)),
        out_shape=jax.ShapeDtypeStruct((S, SSD_CONV_CH), F32),
        scratch_shapes=[pltpu.VMEM((S + 2 * CONV_PAD, CONV_CB), F32)],
        compiler_params=_cp("parallel"),
    )(u, conv_w, conv_b)


def ssd_conv_bwd(u, conv_w, conv_b, dact):
    S = u.shape[0]
    ncb = SSD_CONV_CH // CONV_CB
    col0 = SSD_DI // CONV_CB
    RB = min(CONV_RB, S)
    half = SSD_CONV // 2

    def body(x_ref, w_ref, b_ref, da_ref, dx_ref, dw_ref, db_ref, xpad, dpad):
        z8 = jnp.zeros((CONV_PAD, CONV_CB), F32)
        xpad[0:CONV_PAD, :] = z8
        xpad[S + CONV_PAD:S + 2 * CONV_PAD, :] = z8
        dpad[0:CONV_PAD, :] = z8
        dpad[S + CONV_PAD:S + 2 * CONV_PAD, :] = z8
        xpad[CONV_PAD:S + CONV_PAD, :] = x_ref[...]
        w = w_ref[...]
        b = b_ref[...]
        dws = [jnp.zeros((1, CONV_CB), F32) for _ in range(SSD_CONV)]
        db = jnp.zeros((1, CONV_CB), F32)
        for r in range(S // RB):
            acc = jnp.zeros((RB, CONV_CB), F32) + b
            xs = []
            for k in range(SSD_CONV):
                off = r * RB + CONV_PAD + k - half
                xk = xpad[off:off + RB, :]
                xs.append(xk)
                acc = acc + xk * w[k:k + 1, :]
            sg = jax.nn.sigmoid(acc)
            dc = da_ref[r * RB:(r + 1) * RB, :] * (sg * (1.0 + acc * (1.0 - sg)))
            dpad[r * RB + CONV_PAD:(r + 1) * RB + CONV_PAD, :] = dc
            db = db + jnp.sum(dc, axis=0, keepdims=True)
            for k in range(SSD_CONV):
                dws[k] = dws[k] + jnp.sum(xs[k] * dc, axis=0, keepdims=True)
        for r in range(S // RB):
            acc = jnp.zeros((RB, CONV_CB), F32)
            for k in range(SSD_CONV):
                off = r * RB + CONV_PAD + half - k
                acc = acc + dpad[off:off + RB, :] * w[k:k + 1, :]
            dx_ref[r * RB:(r + 1) * RB, :] = acc
        for k in range(SSD_CONV):
            dw_ref[k:k + 1, :] = dws[k]
        dw_ref[SSD_CONV:SSD_CONV + 1, :] = jnp.zeros((1, CONV_CB), F32)
        db_ref[...] = db

    return pl.pallas_call(
        body, name="ssd_conv_bwd", grid=(ncb,),
        in_specs=[pl.BlockSpec((S, CONV_CB), lambda j: (0, col0 + j)),
                  pl.BlockSpec((SSD_CONV, CONV_CB), lambda j: (0, j)),
                  pl.BlockSpec((1, CONV_CB), lambda j: (0, j)),
                  pl.BlockSpec((S, CONV_CB), lambda j: (0, j))],
        out_specs=[pl.BlockSpec((S, CONV_CB), lambda j: (0, j)),
                   pl.BlockSpec((SSD_CONV + 1, CONV_CB), lambda j: (0, j)),
                   pl.BlockSpec((1, CONV_CB), lambda j: (0, j))],
        out_shape=[jax.ShapeDtypeStruct((S, SSD_CONV_CH), F32),
                   jax.ShapeDtypeStruct((SSD_CONV + 1, SSD_CONV_CH), F32),
                   jax.ShapeDtypeStruct((1, SSD_CONV_CH), F32)],
        scratch_shapes=[pltpu.VMEM((S + 2 * CONV_PAD, CONV_CB), F32), pltpu.VMEM((S + 2 * CONV_PAD, CONV_CB), F32)],
        compiler_params=_cp("parallel"),
    )(u, conv_w, conv_b, dact)


def _ssd_group_layout(t):
    r = t.shape[0]
    g = t[:, :2 * SSD_HEADS].reshape(r, 2, SSD_GROUPS, SSD_HPG).transpose(2, 0, 1, 3).reshape(SSD_GROUPS, r, 2 * SSD_HPG)
    return jnp.pad(g, ((0, 0), (0, 0), (0, 128 - 2 * SSD_HPG)))


def _ssd_head_layout(t):
    r = t.shape[1]
    h = t[:, :, :2 * SSD_HPG].reshape(SSD_GROUPS, r, 2, SSD_HPG).transpose(1, 2, 0, 3).reshape(r, 2 * SSD_HEADS)
    return jnp.pad(h, ((0, 0), (0, 128 - 2 * SSD_HEADS)))


def _ssd_chunk(state, x, Bg, Cg, dt, alog, dtr, alr, *, reverse):
    Q, P = SSD_CHUNK, SSD_HEADDIM
    r = lax.broadcasted_iota(jnp.int32, (Q, Q), 0)
    c = lax.broadcasted_iota(jnp.int32, (Q, Q), 1)
    keep = (c >= r) if reverse else (c <= r)
    cum_t = jnp.transpose(_cumsum_rows(dt * (-jnp.exp(alog)), reverse))
    cum = _cumsum_rows(dtr * (-jnp.exp(alr)), reverse)
    last = 0 if reverse else Q - 1
    cum_l = cum[last:last + 1, :]
    CB = _dot(Cg, Bg, ((1,), (1,)))
    yoff = _dot(Cg, state, ((1,), (0,))) * jnp.exp(cum)
    xdt = x * dtr
    ys = []
    for h in range(SSD_HPG):
        col = h + (SSD_HPG if reverse else 0)
        hs = slice(h * P, (h + 1) * P)
        cum_q = jnp.concatenate([cum[:, hs]] * (Q // P), axis=1)
        L = jnp.where(keep, jnp.exp(jnp.where(keep, cum_q - cum_t[col:col + 1, :], 0.0)), 0.0)
        ys.append(_dot(CB * L, xdt[:, hs], ((1,), (0,))))
    new_state = jnp.exp(cum_l) * state + _dot(Bg, xdt * jnp.exp(cum_l - cum), ((0,), (0,)))
    return new_state, jnp.concatenate(ys, axis=1) + yoff


def ssd_scan_fwd(act, dt, alog, dtr, alr, *, reverse, y_prev=None, comm=None):
    S = act.shape[0]
    Q, N, P = SSD_CHUNK, SSD_STATE, SSD_HEADDIM
    nc = S // Q
    GW = SSD_HPG * P

    def cidx(i):
        return (nc - 1 - i) if reverse else i

    def body(*refs):
        if y_prev is None:
            x_ref, b_ref, c_ref, dt_ref, al_ref, dtr_ref, alr_ref, y_ref, st_ref, state = refs
            yp_ref = None
        else:
            x_ref, b_ref, c_ref, dt_ref, al_ref, dtr_ref, alr_ref, yp_ref, y_ref, st_ref, state = refs
        i = pl.program_id(1)

        @pl.when(i == 0)
        def _():
            state[...] = jnp.zeros_like(state)

        st = state[...]
        st_ref[0, 0] = st
        ns, y = _ssd_chunk(st, x_ref[...], b_ref[...], c_ref[...], dt_ref[0], al_ref[0], dtr_ref[...], alr_ref[...],
                           reverse=reverse)
        state[...] = ns
        y_ref[...] = y if yp_ref is None else y + yp_ref[...]

    xspec = pl.BlockSpec((Q, GW), lambda g, i: (cidx(i), g))
    in_specs = [xspec,
                pl.BlockSpec((Q, N), lambda g, i: (cidx(i), SSD_DI // N + g)),
                pl.BlockSpec((Q, N), lambda g, i: (cidx(i), SSD_DI // N + SSD_GROUPS + g)),
                pl.BlockSpec((1, Q, 128), lambda g, i: (g, cidx(i), 0)),
                pl.BlockSpec((1, 1, 128), lambda g, i: (g, 0, 0)),
                xspec, pl.BlockSpec((1, GW), lambda g, i: (0, g))]
    args = [act, act, act, dt, alog, dtr, alr]
    if y_prev is not None:
        in_specs.append(xspec)
        args.append(y_prev)
    return pcall(
        body, comm, name=f"ssd_scan_fwd_{int(reverse)}", grid=(SSD_GROUPS, nc), in_specs=in_specs,
        out_specs=[xspec, pl.BlockSpec((1, 1, N, GW), lambda g, i: (cidx(i), g, 0, 0))],
        out_shape=[jax.ShapeDtypeStruct((S, SSD_DI), F32), jax.ShapeDtypeStruct((nc, SSD_GROUPS, N, GW), F32)],
        scratch_shapes=[pltpu.VMEM((N, GW), F32)],
        compiler_params=_cp("arbitrary", "arbitrary"),
    )(*args)


def ssd_scan_bwd(act, dt, alog, dtr, alr, states, dy, prev_x, *, reverse, prev=None, comm=None):
    S = act.shape[0]
    Q, N, P = SSD_CHUNK, SSD_STATE, SSD_HEADDIM
    nc = S // Q
    GW = SSD_HPG * P

    def cidx(i):
        return i if reverse else (nc - 1 - i)

    def body(*refs):
        x_ref, b_ref, c_ref, dt_ref, al_ref, dtr_ref, alr_ref, st_ref, dy_ref, px_ref = refs[:10]
        pos = 10
        if prev is not None:
            pb_ref, pc_ref, pdt_ref, pal_ref = refs[pos:pos + 4]
            pos += 4
        dx_ref, db_ref, dc_ref, ddt_ref, dal_ref, ddtr_ref, dalr_ref, dstate = refs[pos:]
        i = pl.program_id(1)

        @pl.when(i == 0)
        def _():
            dstate[...] = jnp.zeros_like(dstate)

        _, vjp = jax.vjp(functools.partial(_ssd_chunk, reverse=reverse), st_ref[0, 0], x_ref[...], b_ref[...],
                         c_ref[...], dt_ref[0], al_ref[0], dtr_ref[...], alr_ref[...])
        dst, dx, dB, dC, ddt, dal, ddtr, dalr = vjp((dstate[...], dy_ref[...]))
        dstate[...] = dst
        dx_ref[...] = dx + px_ref[...]
        if prev is not None:
            dB = dB + pb_ref[...]
            dC = dC + pc_ref[...]
            ddt = ddt + pdt_ref[0]
        db_ref[...] = dB
        dc_ref[...] = dC
        ddt_ref[0] = ddt
        ddtr_ref[...] = ddtr

        @pl.when(i == 0)
        def _():
            dal_ref[0] = dal + (pal_ref[0] if prev is not None else 0.0)
            dalr_ref[...] = dalr

        @pl.when(i != 0)
        def _():
            dal_ref[0] += dal
            dalr_ref[...] += dalr

    xspec = pl.BlockSpec((Q, GW), lambda g, i: (cidx(i), g))
    gspec = pl.BlockSpec((Q, N), lambda g, i: (cidx(i), g))
    dtspec = pl.BlockSpec((1, Q, 128), lambda g, i: (g, cidx(i), 0))
    alspec = pl.BlockSpec((1, 1, 128), lambda g, i: (g, 0, 0))
    alrspec = pl.BlockSpec((1, GW), lambda g, i: (0, g))
    in_specs = [xspec,
                pl.BlockSpec((Q, N), lambda g, i: (cidx(i), SSD_DI // N + g)),
                pl.BlockSpec((Q, N), lambda g, i: (cidx(i), SSD_DI // N + SSD_GROUPS + g)),
                dtspec, alspec, xspec, alrspec,
                pl.BlockSpec((1, 1, N, GW), lambda g, i: (cidx(i), g, 0, 0)), xspec, xspec]
    args = [act, act, act, dt, alog, dtr, alr, states, dy, prev_x]
    if prev is not None:
        in_specs += [gspec, gspec, dtspec, alspec]
        args += list(prev)
    return pcall(
        body, comm, name=f"ssd_scan_bwd_{int(reverse)}", grid=(SSD_GROUPS, nc), in_specs=in_specs,
        out_specs=[xspec, gspec, gspec, dtspec, alspec, xspec, alrspec],
        out_shape=[jax.ShapeDtypeStruct((S, SSD_DI), F32), jax.ShapeDtypeStruct((S, SSD_GROUPS * N), F32),
                   jax.ShapeDtypeStruct((S, SSD_GROUPS * N), F32),
                   jax.ShapeDtypeStruct((SSD_GROUPS, S, 128), F32), jax.ShapeDtypeStruct((SSD_GROUPS, 1, 128), F32),
                   jax.ShapeDtypeStruct((S, SSD_DI), F32), jax.ShapeDtypeStruct((1, SSD_DI), F32)],
        scratch_shapes=[pltpu.VMEM((N, GW), F32)],
        compiler_params=_cp("arbitrary", "arbitrary"),
    )(*args)


def _ssd_post_fn(y, xs, z, dexp, ng):
    t = (y + xs * dexp) * _silu(z)
    return (_rms(t, ng),)


def _cumsum_rows_impl(x, reverse):
    n = x.shape[0]
    row = lax.broadcasted_iota(jnp.int32, x.shape, 0)
    k = 1
    while k < n:
        if reverse:
            x = x + jnp.where(row < n - k, pltpu.roll(x, n - k, 0), 0.0)
        else:
            x = x + jnp.where(row >= k, pltpu.roll(x, k, 0), 0.0)
        k *= 2
    return x


@functools.partial(jax.custom_vjp, nondiff_argnums=(1,))
def _cumsum_rows(x, reverse):
    return _cumsum_rows_impl(x, reverse)


_cumsum_rows.defvjp(lambda x, reverse: (_cumsum_rows_impl(x, reverse), None),
                    lambda reverse, _, ct: (_cumsum_rows_impl(ct, not reverse),))


def _hg_chunk(state, qraw, fraw, v, lb, *, reverse):
    C = HG_CHUNK
    r = lax.broadcasted_iota(jnp.int32, (C, C), 0)
    c = lax.broadcasted_iota(jnp.int32, (C, C), 1)
    keep = (c >= r) if reverse else (c <= r)
    q = _silu(qraw)
    f = lb + (1.0 - lb) * jax.nn.sigmoid(fraw)
    k = 1.0 - f
    g = jnp.log(f)
    G = _cumsum_rows(g, reverse)
    ref_row = C // 2 - 1 if reverse else C // 2
    last_row = 0 if reverse else C - 1
    Gr = G[ref_row:ref_row + 1, :]
    Gl = G[last_row:last_row + 1, :]
    q_t = q * jnp.exp(G - Gr)
    k_t = k * jnp.exp(Gr - G)
    att = jnp.where(keep, _dot(q_t, k_t, ((1,), (1,))), 0.0)
    o = _dot(att, v, ((1,), (0,))) + _dot(q * jnp.exp(G), state, ((1,), (0,)))
    kd = k * jnp.exp(Gl - G)
    new_state = jnp.transpose(jnp.exp(Gl)) * state + _dot(kd, v, ((0,), (0,)))
    return new_state, o


def hg_scan_fwd(u, lb, *, reverse, o_prev=None, rows=256, comm=None):
    S = u.shape[0]
    nh = HG_HEADS
    rows = min(rows, S)
    nsteps = S // rows
    ncb = rows // HG_CHUNK
    f_sec = 2 if reverse else 1

    def blk(i):
        return (nsteps - 1 - i) if reverse else i

    def body(*refs):
        if o_prev is None:
            q_ref, f_ref, v_ref, lb_ref, o_ref, st_ref, state = refs
            op_ref = None
        else:
            q_ref, f_ref, v_ref, lb_ref, op_ref, o_ref, st_ref, state = refs
        i = pl.program_id(0)

        @pl.when(i == 0)
        def _():
            state[...] = jnp.zeros_like(state)

        def chunk(cc, carry):
            ci = (ncb - 1 - cc) if reverse else cc
            sl = pl.ds(pl.multiple_of(ci * HG_CHUNK, HG_CHUNK), HG_CHUNK)
            for h in range(nh):
                hs = slice(h * HG_D, (h + 1) * HG_D)
                st = state[h]
                st_ref[ci, h] = st
                ns, o = _hg_chunk(st, q_ref[sl, hs], f_ref[sl, hs], v_ref[sl, hs], lb_ref[:, hs], reverse=reverse)
                state[h] = ns
                if op_ref is not None:
                    o = o + op_ref[sl, hs]
                o_ref[sl, hs] = o
            return carry

        lax.fori_loop(0, ncb, chunk, 0)

    rowspec = lambda sec: pl.BlockSpec((rows, HG_W), lambda i: (blk(i), sec))
    in_specs = [rowspec(0), rowspec(f_sec), rowspec(3), pl.BlockSpec((1, HG_W), lambda i: (0, 0))]
    args = [u, u, u, lb]
    if o_prev is not None:
        in_specs.append(rowspec(0))
        args.append(o_prev)
    return pcall(
        body, comm, name=f"hg_scan_fwd_{int(reverse)}", grid=(nsteps,), in_specs=in_specs,
        out_specs=[rowspec(0), pl.BlockSpec((ncb, nh, HG_D, HG_D), lambda i: (blk(i), 0, 0, 0))],
        out_shape=[jax.ShapeDtypeStruct((S, HG_W), F32), jax.ShapeDtypeStruct((S // HG_CHUNK, nh, HG_D, HG_D), F32)],
        scratch_shapes=[pltpu.VMEM((nh, HG_D, HG_D), F32)],
        compiler_params=_cp("arbitrary"),
    )(*args)


def hg_scan_bwd(u, lb, states, do, *, reverse, prev=None, rows=256, comm=None):
    S = u.shape[0]
    nh = HG_HEADS
    rows = min(rows, S)
    nsteps = S // rows
    ncb = rows // HG_CHUNK
    f_sec = 2 if reverse else 1

    def blk(i):
        return i if reverse else (nsteps - 1 - i)

    def body(*refs):
        q_ref, f_ref, v_ref, lb_ref, st_ref, do_ref = refs[:6]
        pos = 6
        if prev is not None:
            pq_ref, pv_ref, plb_ref = refs[pos:pos + 3]
            pos += 3
        dq_ref, df_ref, dv_ref, dlb_ref, dstate = refs[pos:]
        i = pl.program_id(0)

        @pl.when(i == 0)
        def _():
            dstate[...] = jnp.zeros_like(dstate)
            dlb_ref[...] = plb_ref[...] if prev is not None else jnp.zeros_like(dlb_ref)

        def chunk(cc, carry):
            ci = cc if reverse else (ncb - 1 - cc)
            sl = pl.ds(pl.multiple_of(ci * HG_CHUNK, HG_CHUNK), HG_CHUNK)
            for h in range(nh):
                hs = slice(h * HG_D, (h + 1) * HG_D)
                _, vjp = jax.vjp(functools.partial(_hg_chunk, reverse=reverse), st_ref[ci, h],
                                 q_ref[sl, hs], f_ref[sl, hs], v_ref[sl, hs], lb_ref[:, hs])
                dst, dq, df, dv, dlb = vjp((dstate[h], do_ref[sl, hs]))
                dstate[h] = dst
                if prev is not None:
                    dq = dq + pq_ref[sl, hs]
                    dv = dv + pv_ref[sl, hs]
                dq_ref[sl, hs] = dq
                df_ref[sl, hs] = df
                dv_ref[sl, hs] = dv
                dlb_ref[:, hs] += dlb
            return carry

        lax.fori_loop(0, ncb, chunk, 0)

    rowspec = lambda sec: pl.BlockSpec((rows, HG_W), lambda i: (blk(i), sec))
    lbspec = pl.BlockSpec((1, HG_W), lambda i: (0, 0))
    in_specs = [rowspec(0), rowspec(f_sec), rowspec(3), lbspec,
                pl.BlockSpec((ncb, nh, HG_D, HG_D), lambda i: (blk(i), 0, 0, 0)), rowspec(0)]
    args = [u, u, u, lb, states, do]
    if prev is not None:
        in_specs += [rowspec(0), rowspec(0), lbspec]
        args += list(prev)
    return pcall(
        body, comm, name=f"hg_scan_bwd_{int(reverse)}", grid=(nsteps,), in_specs=in_specs,
        out_specs=[rowspec(0), rowspec(0), rowspec(0), lbspec],
        out_shape=[jax.ShapeDtypeStruct((S, HG_W), F32)] * 3 + [jax.ShapeDtypeStruct((1, HG_W), F32)],
        scratch_shapes=[pltpu.VMEM((nh, HG_D, HG_D), F32)],
        compiler_params=_cp("arbitrary"),
    )(*args)


def _hg_lb_fn(lbp):
    m = jnp.max(lbp, axis=0, keepdims=True)
    e = jnp.exp(lbp - m)
    sm = e / jnp.sum(e, axis=0, keepdims=True)
    return ((sm[0:1] + sm[1:2]) - sm[0:1],)


def hg_lb_fwd(lbp):
    def body(x_ref, o_ref):
        o_ref[...] = _hg_lb_fn(x_ref[...])[0]

    return pl.pallas_call(body, name="hg_lb_fwd", out_shape=jax.ShapeDtypeStruct((1, HG_W), F32))(lbp)


def hg_lb_bwd(lbp, dlb):
    def body(x_ref, d_ref, o_ref):
        _, vjp = jax.vjp(_hg_lb_fn, x_ref[...])
        o_ref[...] = vjp((d_ref[...],))[0]

    return pl.pallas_call(body, name="hg_lb_bwd", out_shape=jax.ShapeDtypeStruct(lbp.shape, F32))(lbp, dlb)


def _hg_post_fn(o, gate, ng):
    return (_rms(o, ng) * _silu(gate),)


def _gate_fn(o, gate):
    return (o * _silu(gate),)


def _rope_tables(S):
    t = np.arange(S)
    row = (t // GRID_W).astype(np.float32)
    col = (t % GRID_W).astype(np.float32)
    half = AT_HD // 4
    inv = (ROPE_THETA ** (-np.arange(0, 2 * half, 2, dtype=np.float32) / np.float32(2 * half))).astype(np.float32)
    ar = row[:, None] * inv[None, :]
    ac = col[:, None] * inv[None, :]
    return ar.astype(np.float32), ac.astype(np.float32)


@jax.custom_vjp
def _half_swap(x):
    ax = x.ndim - 1
    lane = lax.broadcasted_iota(jnp.int32, x.shape, ax)
    return jnp.where((lane & 32) == 0, pltpu.roll(x, 96, ax), pltpu.roll(x, 32, ax))


_half_swap.defvjp(lambda x: (_half_swap(x), None), lambda _, ct: (_half_swap(ct),))


def _make_qk_fn(scale):
    def fn(x, ct, st, g):
        n = _rms(x, g)
        return ((n * ct + _half_swap(n) * st) * scale,)
    return fn


def flash_fwd(q, k, v, *, v_col0=0, tq=256, comm=None):
    S = q.shape[0]
    tq = min(tq, S)
    G = AT_HEADS // AT_KV

    def body(q_ref, k_ref, v_ref, o_ref, lse_ref):
        kv, vv = k_ref[...], v_ref[...]
        for g in range(G):
            sl = slice(g * AT_HD, (g + 1) * AT_HD)
            s = _dot(q_ref[:, sl], kv, ((1,), (1,)))
            m = jnp.max(s, axis=1, keepdims=True)
            p = jnp.exp(s - m)
            l = jnp.sum(p, axis=1, keepdims=True)
            o_ref[:, sl] = _dot(p, vv, ((1,), (0,))) / l
            lse_ref[0, :, g:g + 1] = m + jnp.log(l)

    return pcall(
        body, comm, name="flash_fwd", grid=(AT_KV, S // tq),
        in_specs=[pl.BlockSpec((tq, G * AT_HD), lambda h, i: (i, h)),
                  pl.BlockSpec((S, AT_HD), lambda h, i: (0, h)),
                  pl.BlockSpec((S, AT_HD), lambda h, i: (0, v_col0 + h))],
        out_specs=[pl.BlockSpec((tq, G * AT_HD), lambda h, i: (i, h)),
                   pl.BlockSpec((1, tq, G), lambda h, i: (h, i, 0))],
        out_shape=[jax.ShapeDtypeStruct((S, AT_QW), F32), jax.ShapeDtypeStruct((AT_KV, S, G), F32)],
        compiler_params=_cp("parallel", "arbitrary"),
    )(q, k, v)


def flash_bwd_dq(q, k, v, o, lse, do, *, v_col0=0, tq=256):
    S = q.shape[0]
    tq = min(tq, S)
    G = AT_HEADS // AT_KV

    def body(q_ref, k_ref, v_ref, o_ref, lse_ref, do_ref, dq_ref, dl_ref):
        kv, vv = k_ref[...], v_ref[...]
        for g in range(G):
            sl = slice(g * AT_HD, (g + 1) * AT_HD)
            dog = do_ref[:, sl]
            delta = jnp.sum(dog * o_ref[:, sl], axis=1, keepdims=True)
            s = _dot(q_ref[:, sl], kv, ((1,), (1,)))
            p = jnp.exp(s - lse_ref[0, :, g:g + 1])
            dp = _dot(dog, vv, ((1,), (1,)))
            ds = p * (dp - delta)
            dq_ref[:, sl] = _dot(ds, kv, ((1,), (0,)))
            dl_ref[0, :, g:g + 1] = delta

    qspec = pl.BlockSpec((tq, G * AT_HD), lambda h, i: (i, h))
    lspec = pl.BlockSpec((1, tq, G), lambda h, i: (h, i, 0))
    return pl.pallas_call(
        body, name="flash_bwd_dq", grid=(AT_KV, S // tq),
        in_specs=[qspec, pl.BlockSpec((S, AT_HD), lambda h, i: (0, h)),
                  pl.BlockSpec((S, AT_HD), lambda h, i: (0, v_col0 + h)), qspec, lspec, qspec],
        out_specs=[qspec, lspec],
        out_shape=[jax.ShapeDtypeStruct((S, AT_QW), F32), jax.ShapeDtypeStruct((AT_KV, S, G), F32)],
        compiler_params=_cp("parallel", "arbitrary"),
    )(q, k, v, o, lse, do)


def flash_bwd_dkv(q, k, v, lse, delta, do, *, v_col0=0, tk=512, comm=None):
    S = q.shape[0]
    tk = min(tk, S)
    G = AT_HEADS // AT_KV

    def body(q_ref, k_ref, v_ref, lse_ref, dl_ref, do_ref, dk_ref, dv_ref):
        kv, vv = k_ref[...], v_ref[...]
        dk = jnp.zeros((tk, AT_HD), F32)
        dv = jnp.zeros((tk, AT_HD), F32)
        for g in range(G):
            sl = slice(g * AT_HD, (g + 1) * AT_HD)
            qg, dog = q_ref[:, sl], do_ref[:, sl]
            s = _dot(qg, kv, ((1,), (1,)))
            p = jnp.exp(s - lse_ref[0, :, g:g + 1])
            dv = dv + _dot(p, dog, ((0,), (0,)))
            dp = _dot(dog, vv, ((1,), (1,)))
            ds = p * (dp - dl_ref[0, :, g:g + 1])
            dk = dk + _dot(ds, qg, ((0,), (0,)))
        dk_ref[...] = dk
        dv_ref[...] = dv

    qspec = pl.BlockSpec((S, G * AT_HD), lambda h, j: (0, h))
    kspec = pl.BlockSpec((tk, AT_HD), lambda h, j: (j, h))
    lspec = pl.BlockSpec((1, S, G), lambda h, j: (h, 0, 0))
    return pcall(
        body, comm, name="flash_bwd_dkv", grid=(AT_KV, S // tk),
        in_specs=[qspec, kspec, pl.BlockSpec((tk, AT_HD), lambda h, j: (j, v_col0 + h)), lspec, lspec, qspec],
        out_specs=[kspec, kspec],
        out_shape=[jax.ShapeDtypeStruct((S, AT_KW), F32), jax.ShapeDtypeStruct((S, AT_KW), F32)],
        compiler_params=_cp("parallel", "arbitrary"),
    )(q, k, v, lse, delta, do)


def _t5_bucket_np(rel):
    half = REL_BUCKETS // 2
    exact = half // 2
    n = np.abs(rel)
    large = exact + (np.log(np.maximum(n, 1).astype(np.float32) / np.float32(exact))
                     / np.float32(math.log(REL_MAX_DIST / exact)) * np.float32(half - exact)).astype(np.int32)
    large = np.minimum(large, half - 1)
    return np.where(rel > 0, half, 0) + np.where(n < exact, n, large)


def _dl_tiles(Ls):
    T = min(128, Ls)
    return T, T + 2 * DL_HALF


def _dl_bucket_tables(dil, T):
    W = T + 2 * DL_HALF
    i = np.arange(T)[:, None]
    j = np.arange(W)[None, :]
    bq = _t5_bucket_np((j - DL_HALF - i) * dil)
    iw = np.arange(W)[:, None]
    jk = np.arange(T)[None, :]
    bk = _t5_bucket_np((jk + DL_HALF - iw) * dil)
    return bq.astype(np.int32), bk.astype(np.int32)


def _dl_merge_fn(o0, o1, o2, l0, l1, l2):
    m = jnp.maximum(jnp.maximum(l0, l1), l2)
    e0, e1, e2 = jnp.exp(l0 - m), jnp.exp(l1 - m), jnp.exp(l2 - m)
    den = e0 + e1 + e2
    return ((e0 / den) * o0 + (e1 / den) * o1 + (e2 / den) * o2,)


def _adamw_math(w, g, m, v):
    m = ADAM_B1 * m + (1.0 - ADAM_B1) * g
    v = ADAM_B2 * v + (1.0 - ADAM_B2) * (g * g)
    m_hat = m / (1.0 - ADAM_B1 ** ADAM_STEP)
    v_hat = v / (1.0 - ADAM_B2 ** ADAM_STEP)
    delta = -ADAM_LR * (m_hat / (jnp.sqrt(v_hat) + ADAM_EPS) + ADAM_WD * w)
    return delta, m, v


def adamw_sum(parts, w, m, v, *, name, R=128):
    rows, cols = w.shape
    R = min(R, rows)
    if rows % R:
        R = rows

    def body(p_ref, w_ref, m_ref, v_ref, g_ref, d_ref, nm_ref, nv_ref):
        g = p_ref[0].astype(F32)
        for s in range(1, N_DEV):
            g = g + p_ref[s].astype(F32)
        d, nm, nv = _adamw_math(w_ref[...], g, m_ref[...], v_ref[...])
        g_ref[...] = g
        d_ref[...] = d
        nm_ref[...] = nm
        nv_ref[...] = nv

    spec = pl.BlockSpec((R, cols), lambda i: (i, 0))
    return pl.pallas_call(
        body, name=name, grid=(rows // R,),
        in_specs=[pl.BlockSpec((N_DEV, R, cols), lambda i: (0, i, 0)), spec, spec, spec],
        out_specs=[spec] * 4, out_shape=[jax.ShapeDtypeStruct((rows, cols), F32)] * 4,
        compiler_params=_cp("parallel"),
    )(parts, w, m, v)


def sum_parts(parts, *, name):
    rows, cols = parts.shape[1:]

    def body(p_ref, o_ref):
        g = p_ref[0]
        for s in range(1, N_DEV):
            g = g + p_ref[s]
        o_ref[...] = g

    return pl.pallas_call(body, name=name, out_shape=jax.ShapeDtypeStruct((rows, cols), F32))(parts)


def adamw_plain(w, g, m, v, *, name):
    def body(w_ref, g_ref, m_ref, v_ref, d_ref, nm_ref, nv_ref):
        d, nm, nv = _adamw_math(w_ref[...], g_ref[...], m_ref[...], v_ref[...])
        d_ref[...] = d
        nm_ref[...] = nm
        nv_ref[...] = nv

    return pl.pallas_call(body, name=name, out_shape=[jax.ShapeDtypeStruct(w.shape, F32)] * 3)(w, g, m, v)


def allgather_two_level(x, *, name):
    R, C = x.shape

    def body(x_ref, out_ref, send_sems, recv_sems, local_sem):
        x_, y_, c_ = _my_pos()
        me, sibling = (x_, y_, c_), (x_, y_, 1 - c_)
        chips = [(1 - x_, y_), (x_, 1 - y_), (1 - x_, 1 - y_)]

        def rows(p):
            return out_ref.at[_flat(*p)]

        def copy(k, block, to, src=None):
            return pltpu.make_async_remote_copy(
                src_ref=rows(block) if src is None else src, dst_ref=rows(block),
                send_sem=send_sems.at[k], recv_sem=recv_sems.at[k], device_id=to, device_id_type=MESH_ID)

        mine = pltpu.make_async_copy(x_ref, rows(me), local_sem)
        mine.start()
        first = [copy(0, me, sibling, src=x_ref)]
        first += [copy(1 + j, me, (*chip, c_), src=x_ref) for j, chip in enumerate(chips)]
        for cp in first:
            cp.start()
        passed = [copy(4 + j, (*chip, c_), sibling) for j, chip in enumerate(chips)]
        for j, chip in enumerate(chips):
            copy(1 + j, (*chip, c_), me).wait_recv()
            passed[j].start()
        copy(0, sibling, me).wait_recv()
        for j, chip in enumerate(chips):
            copy(4 + j, (*chip, 1 - c_), me).wait_recv()
        for cp in first + passed:
            cp.wait_send()
        mine.wait()

    return pl.pallas_call(
        body, name=name,
        out_shape=jax.ShapeDtypeStruct((N_DEV, R, C), x.dtype),
        in_specs=[pl.BlockSpec(memory_space=pl.ANY)],
        out_specs=pl.BlockSpec(memory_space=pl.ANY),
        scratch_shapes=[pltpu.SemaphoreType.DMA((7,)), pltpu.SemaphoreType.DMA((7,)), pltpu.SemaphoreType.DMA],
    )(x)


def allgather_two_level_multi(xs, *, name):
    nb = len(xs)

    def body(*refs):
        x_refs, out_refs = refs[:nb], refs[nb:2 * nb]
        send_sems, recv_sems, local_sems = refs[2 * nb:]
        x_, y_, c_ = _my_pos()
        me, sibling = (x_, y_, c_), (x_, y_, 1 - c_)
        chips = [(1 - x_, y_), (x_, 1 - y_), (1 - x_, 1 - y_)]

        def copy(b, k, block, to, own=False):
            rows = out_refs[b].at[_flat(*block)]
            return pltpu.make_async_remote_copy(
                src_ref=x_refs[b] if own else rows, dst_ref=rows,
                send_sem=send_sems.at[b, k], recv_sem=recv_sems.at[b, k], device_id=to, device_id_type=MESH_ID)

        mine = [pltpu.make_async_copy(x_refs[b], out_refs[b].at[_flat(*me)], local_sems.at[b]) for b in range(nb)]
        first = []
        for b in range(nb):
            first.append(copy(b, 0, me, sibling, own=True))
            first += [copy(b, 1 + j, me, (*chip, c_), own=True) for j, chip in enumerate(chips)]
        for cp in mine + first:
            cp.start()
        passed = []
        for j, chip in enumerate(chips):
            for b in range(nb):
                copy(b, 1 + j, (*chip, c_), me).wait_recv()
                fwd = copy(b, 4 + j, (*chip, c_), sibling)
                fwd.start()
                passed.append(fwd)
        for b in range(nb):
            copy(b, 0, sibling, me).wait_recv()
            for j, chip in enumerate(chips):
                copy(b, 4 + j, (*chip, 1 - c_), me).wait_recv()
        for cp in first + passed:
            cp.wait_send()
        for cp in mine:
            cp.wait()

    anyspec = pl.BlockSpec(memory_space=pl.ANY)
    return pl.pallas_call(
        body, name=name,
        out_shape=[jax.ShapeDtypeStruct((N_DEV,) + x.shape, x.dtype) for x in xs],
        in_specs=[anyspec] * nb, out_specs=[anyspec] * nb,
        scratch_shapes=[pltpu.SemaphoreType.DMA((nb, 7)), pltpu.SemaphoreType.DMA((nb, 7)), pltpu.SemaphoreType.DMA((nb,))],
    )(*xs)


def _prenorm(tag, x, ng):
    return rowwise_fwd(f"{tag}_prenorm", _prenorm_fn, [(x, 0, False)], [(ng, False)], [(D_MODEL, MXU_DTYPE)], W=D_MODEL)[0]


def _cat_mxu(parts):
    return jnp.concatenate([t.astype(MXU_DTYPE) for t in parts], axis=1)


def _in_out_bwd(tag, x, ng, hn, du, w_in, dx, tail_comm=None):
    dw_in = matmul(hn, du, ta=True, out_dtype=GRAD_WIRE_DTYPE, name=f"{tag}_dw_in")
    comm = None if tail_comm is None else tail_comm(dw_in)
    dhn, cres = _own(matmul(du, w_in, tb=True, name=f"{tag}_dhn", comm=comm), comm)
    dx_prev, dng = rowwise_bwd(f"{tag}_prenorm_bwd", _prenorm_fn, [(x, 0, False)], [(ng, False)], [dhn],
                               W=D_MODEL, diff_rows=[0], diff_shared=[0], add=dx)
    return (dx_prev, dng, dw_in) if tail_comm is None else (dx_prev, dng, dw_in, cres)


def _own(res, comm):
    return (res, None) if comm is None else res


def ssd_layer_fwd(x, ng, p, comm=None, comm1=None):
    hn = _prenorm("ssd", x, ng)
    u = matmul(hn, p["w_in"], name="ssd_in")
    act = ssd_conv_fwd(u, p["conv_w"], p["conv_b"])
    dt = rowwise_fwd("ssd_dt", _dt_fn, [(u, (SSD_DI + SSD_CONV_CH) // 128, False)], [(p["dt_bias"], False)],
                     [(128, F32)], W=128)[0]
    H = SSD_HEADS
    dtr = [jnp.repeat(dt[:, d * H:(d + 1) * H], SSD_HEADDIM, axis=1) for d in (0, 1)]
    alr = [jnp.repeat(p["alog"][:, d * H:(d + 1) * H], SSD_HEADDIM, axis=1) for d in (0, 1)]
    dt, alog = _ssd_group_layout(dt), _ssd_group_layout(p["alog"])
    (y0, st0), cres = _own(ssd_scan_fwd(act, dt, alog, dtr[0], alr[0], reverse=False, comm=comm), comm)
    (y, st1), cres1 = _own(ssd_scan_fwd(act, dt, alog, dtr[1], alr[1], reverse=True, y_prev=y0, comm=comm1), comm1)
    g = rowwise_fwd("ssd_post", _ssd_post_fn, [(y, 0, True), (act, 0, True), (u, 0, True)],
                    [(p["dexp"], True), (p["norm_g"], True)], [(512, MXU_DTYPE)], W=512, ncb=SSD_GROUPS)[0]
    xn = matmul(g, p["w_out"], residual=x, name="ssd_out")
    return xn, dict(x=x, ng=ng, hn=hn, u=u, act=act, dt=dt, alog=alog, dtr=dtr, alr=alr, y=y, st0=st0, st1=st1,
                    g=g), cres, cres1


def ssd_layer_bwd(sv, p, dx, comm=None, tail_comm=None):
    u, act, dt = sv["u"], sv["act"], sv["dt"]
    S = u.shape[0]
    dg = matmul(dx, p["w_out"], tb=True, name="ssd_dg")
    dw_out = matmul(sv["g"], dx, ta=True, out_dtype=GRAD_WIRE_DTYPE, name="ssd_dw_out")
    dy, dxs_skip, dz, ddexp, dnorm = rowwise_bwd(
        "ssd_post_bwd", _ssd_post_fn, [(sv["y"], 0, True), (act, 0, True), (u, 0, True)],
        [(p["dexp"], True), (p["norm_g"], True)], [dg], W=512, ncb=SSD_GROUPS, diff_rows=[0, 1, 2], diff_shared=[0, 1])
    dtr, alr = sv["dtr"], sv["alr"]
    (dxa, dB, dC, ddt, dal, ddtr0, dalr0), cres = _own(
        ssd_scan_bwd(act, dt, sv["alog"], dtr[0], alr[0], sv["st0"], dy, dxs_skip, reverse=False, comm=comm), comm)
    comm1 = None if tail_comm is None else tail_comm(None, dw_out)
    (dxa, dB, dC, ddt, dal, ddtr1, dalr1), tail_out = _own(
        ssd_scan_bwd(act, dt, sv["alog"], dtr[1], alr[1], sv["st1"], dy, dxa, reverse=True, prev=(dB, dC, ddt, dal),
                     comm=comm1), comm1)
    dact = jnp.concatenate([dxa, dB, dC], axis=1)
    dxbc, dconv_w, dconv_b = ssd_conv_bwd(u, p["conv_w"], p["conv_b"], dact)
    fold = jnp.asarray(np.repeat(np.eye(SSD_HEADS, dtype=np.float32), SSD_HEADDIM, axis=0))
    folded = [matmul(t, fold, exact=True, name=f"ssd_ddt_fold_{d}", tn=SSD_HEADS) for d, t in enumerate((ddtr0, ddtr1))]
    ddt_all = _ssd_head_layout(ddt) + jnp.pad(jnp.concatenate(folded, axis=1), ((0, 0), (0, 128 - 2 * SSD_HEADS)))
    dal_rep = jnp.concatenate([t.reshape(SSD_HEADS, SSD_HEADDIM).sum(axis=1) for t in (dalr0, dalr1)])[None, :]
    ddt_raw, ddt_bias = rowwise_bwd("ssd_dt_bwd", _dt_fn, [(u, (SSD_DI + SSD_CONV_CH) // 128, False)],
                                    [(p["dt_bias"], False)], [ddt_all], W=128, diff_rows=[0], diff_shared=[0])
    du = _cat_mxu([dz, dxbc, ddt_raw, jnp.zeros((S, SSD_IN_PAD - SSD_IN - 64), F32)])
    res = _in_out_bwd("ssd", sv["x"], sv["ng"], sv["hn"], du, p["w_in"], dx,
                      tail_comm=None if tail_comm is None else (lambda g_in: tail_comm(g_in[:, :SSD_IN], None)))
    dx_prev, dng, dw_in = res[:3]
    tail = res[3] + tail_out if tail_comm is not None else None
    grads = dict(
        w_in=dw_in[:, :SSD_IN], w_out=dw_out, conv_w=dconv_w[:SSD_CONV], conv_b=dconv_b,
        dt_bias=ddt_bias[:, :2 * SSD_HEADS], a_log=_ssd_head_layout(dal)[:, :2 * SSD_HEADS] + dal_rep,
        d=ddexp.reshape(SSD_HEADS, SSD_HEADDIM).sum(axis=1)[None, :], norm_g=dnorm, ng=dng)
    return dx_prev, grads, cres, tail


def hg_layer_fwd(x, ng, p, comm0=None, comm1=None):
    hn = _prenorm("hg", x, ng)
    u = matmul(hn, p["w_in"], name="hg_in")
    lb = hg_lb_fwd(p["hgrn_lb"])
    (o0, st0), cres0 = _own(hg_scan_fwd(u, lb, reverse=False, comm=comm0), comm0)
    (o, st1), cres1 = _own(hg_scan_fwd(u, lb, reverse=True, o_prev=o0, comm=comm1), comm1)
    g = rowwise_fwd("hg_post", _hg_post_fn, [(o, 0, True), (u, 4 * HG_HEADS, True)], [(p["norm_g"], True)],
                    [(HG_D, MXU_DTYPE)], W=HG_D, ncb=HG_HEADS)[0]
    xn = matmul(g, p["w_out"], residual=x, name="hg_out")
    return xn, dict(x=x, ng=ng, hn=hn, u=u, lb=lb, o=o, st0=st0, st1=st1, g=g), cres0, cres1


def hg_layer_bwd(sv, p, dx, comm=None):
    u, lb = sv["u"], sv["lb"]
    dg = matmul(dx, p["w_out"], tb=True, name="hg_dg")
    dw_out = matmul(sv["g"], dx, ta=True, out_dtype=GRAD_WIRE_DTYPE, name="hg_dw_out")
    do, dgate, dnorm = rowwise_bwd("hg_post_bwd", _hg_post_fn, [(sv["o"], 0, True), (u, 4 * HG_HEADS, True)],
                                   [(p["norm_g"], True)], [dg], W=HG_D, ncb=HG_HEADS, diff_rows=[0, 1], diff_shared=[0])
    (dq0, df0, dv0, dlb0), cres = _own(hg_scan_bwd(u, lb, sv["st0"], do, reverse=False, comm=comm), comm)
    dq, df1, dv, dlb = hg_scan_bwd(u, lb, sv["st1"], do, reverse=True, prev=(dq0, dv0, dlb0))
    du = _cat_mxu([dq, df0, df1, dv, dgate])
    dhgrn_lb = hg_lb_bwd(p["hgrn_lb"], dlb)
    dx_prev, dng, dw_in = _in_out_bwd("hg", sv["x"], sv["ng"], sv["hn"], du, p["w_in"], dx)
    return dx_prev, dict(w_in=dw_in, w_out=dw_out, norm_g=dnorm, hgrn_lb=dhgrn_lb, ng=dng), cres


def _rope_consts(S):
    ar, ac = _rope_tables(S)
    ct = np.concatenate([np.cos(ar), np.cos(ar), np.cos(ac), np.cos(ac)], axis=1).astype(np.float32)
    st = np.concatenate([-np.sin(ar), np.sin(ar), -np.sin(ac), np.sin(ac)], axis=1).astype(np.float32)
    return jnp.asarray(ct), jnp.asarray(st)


def _at_qk(tag, u, col0, nheads, scale, gain, consts, cot=None):
    ct, st = consts
    rows = [(u, col0, True), (ct, 0, False), (st, 0, False)]
    shared = [(gain, False)]
    if cot is None:
        return rowwise_fwd(f"at_{tag}", _make_qk_fn(scale), rows, shared, [(AT_HD, MXU_DTYPE)], W=AT_HD, ncb=nheads)[0]
    return rowwise_bwd(f"at_{tag}_bwd", _make_qk_fn(scale), rows, shared, [cot], W=AT_HD, ncb=nheads,
                       diff_rows=[0], diff_shared=[0])


def at_layer_fwd(x, ng, p, comm=None):
    S = x.shape[0]
    hn = _prenorm("at", x, ng)
    u = matmul(hn, p["w_in"], name="at_in")
    consts = _rope_consts(S)
    qr = _at_qk("q", u, 0, AT_HEADS, AT_HD ** -0.5, p["q_g"], consts)
    kr = _at_qk("k", u, AT_HEADS, AT_KV, 1.0, p["k_g"], consts)
    vc0 = (AT_QW + AT_KW) // AT_HD
    (o, lse), cres = _own(flash_fwd(qr, kr, u, v_col0=vc0, comm=comm), comm)
    g = rowwise_fwd("at_gate", _gate_fn, [(o, 0, True), (u, (AT_QW + 2 * AT_KW) // 1024, True)], [],
                    [(1024, MXU_DTYPE)], W=1024, ncb=AT_QW // 1024)[0]
    xn = matmul(g, p["w_out"], residual=x, name="at_out")
    return xn, dict(x=x, ng=ng, hn=hn, u=u, qr=qr, kr=kr, o=o, lse=lse, g=g), cres


def at_layer_bwd(sv, p, dx, comm=None):
    u, qr, kr = sv["u"], sv["qr"], sv["kr"]
    S = u.shape[0]
    consts = _rope_consts(S)
    vc0 = (AT_QW + AT_KW) // AT_HD
    dg = matmul(dx, p["w_out"], tb=True, name="at_dg")
    dw_out = matmul(sv["g"], dx, ta=True, out_dtype=GRAD_WIRE_DTYPE, name="at_dw_out")
    do, dgate = rowwise_bwd("at_gate_bwd", _gate_fn, [(sv["o"], 0, True), (u, (AT_QW + 2 * AT_KW) // 1024, True)], [],
                            [dg], W=1024, ncb=AT_QW // 1024, diff_rows=[0, 1], diff_shared=[])
    dqs, delta = flash_bwd_dq(qr, kr, u, sv["o"], sv["lse"], do, v_col0=vc0)
    (dkr, dv), cres = _own(flash_bwd_dkv(qr, kr, u, sv["lse"], delta, do, v_col0=vc0, comm=comm), comm)
    dq_raw, dqg = _at_qk("q", u, 0, AT_HEADS, AT_HD ** -0.5, p["q_g"], consts, cot=dqs)
    dk_raw, dkg = _at_qk("k", u, AT_HEADS, AT_KV, 1.0, p["k_g"], consts, cot=dkr)
    du = _cat_mxu([dq_raw, dk_raw, dv, dgate])
    dx_prev, dng, dw_in = _in_out_bwd("at", sv["x"], sv["ng"], sv["hn"], du, p["w_in"], dx)
    return dx_prev, dict(w_in=dw_in, w_out=dw_out, q_g=dqg, k_g=dkg, ng=dng), cres


def _to_stream(t, dil):
    S = t.shape[0]
    return t.reshape(S // dil, dil, DL_HEADS, DL_HD).transpose(2, 1, 0, 3)


def _from_stream(t):
    H, dil, Ls, E = t.shape
    return t.transpose(2, 1, 0, 3).reshape(Ls * dil, H * E)


def _stream_to_hm(t):
    H, dil, Ls, w = t.shape
    return t.transpose(0, 2, 1, 3).reshape(H * Ls * dil, w)


def _hm_to_stream(t, dil):
    w = t.shape[1]
    S = t.shape[0] // DL_HEADS
    return t.reshape(DL_HEADS, S // dil, dil, w).transpose(0, 2, 1, 3)


OX_LSE = DL_HD
DOX_LSE, DOX_DM = DL_HD, DL_HD + 32


def _win(p_ref, c_ref, n_ref, h, T):
    return jnp.concatenate([p_ref[h, 0, T - DL_HALF:T, :], c_ref[h, 0], n_ref[h, 0, 0:DL_HALF, :]], axis=0)


def _win_specs(T, E, nb):
    return [pl.BlockSpec((DL_HEADS, 1, T, E), lambda d, n: (0, d, jnp.maximum(n - 1, 0), 0)),
            pl.BlockSpec((DL_HEADS, 1, T, E), lambda d, n: (0, d, n, 0)),
            pl.BlockSpec((DL_HEADS, 1, T, E), lambda d, n: (0, d, jnp.minimum(n + 1, nb - 1), 0))]


def _band_mask_q(n, T, W, Ls):
    i = lax.broadcasted_iota(jnp.int32, (T, W), 0)
    j = lax.broadcasted_iota(jnp.int32, (T, W), 1)
    kpos = n * T + j - DL_HALF
    return (jnp.abs(j - DL_HALF - i) <= DL_HALF) & (kpos >= 0) & (kpos < Ls)


def band_fwd(q, k, v, bias, *, scale):
    H, dil, Ls, E = q.shape
    T, W = _dl_tiles(Ls)
    nb = Ls // T

    def body(q_ref, kp_ref, kc_ref, kn_ref, vp_ref, vc_ref, vn_ref, b_ref, ox_ref):
        n = pl.program_id(1)
        mask = _band_mask_q(n, T, W, Ls)
        for h in range(H):
            kw = _win(kp_ref, kc_ref, kn_ref, h, T)
            vw = _win(vp_ref, vc_ref, vn_ref, h, T)
            s = _dot(q_ref[h, 0], kw, ((1,), (1,))) * scale + b_ref[h]
            s = jnp.where(mask, s, NEG_BIG)
            m = jnp.max(s, axis=1, keepdims=True)
            lse = m + jnp.log(jnp.sum(jnp.exp(s - m), axis=1, keepdims=True))
            p = jnp.exp(s - lse)
            ox_ref[h, 0, :, 0:E] = _dot(p, vw, ((1,), (0,)))
            ox_ref[h, 0, :, E:2 * E] = lse + jnp.zeros((T, E), F32)

    cur = pl.BlockSpec((H, 1, T, E), lambda d, n: (0, d, n, 0))
    return pl.pallas_call(
        body, name=f"band_fwd_{dil}", grid=(dil, nb),
        in_specs=[cur] + _win_specs(T, E, nb) + _win_specs(T, E, nb) + [pl.BlockSpec((H, T, W), lambda d, n: (0, 0, 0))],
        out_specs=pl.BlockSpec((H, 1, T, 2 * E), lambda d, n: (0, d, n, 0)),
        out_shape=jax.ShapeDtypeStruct((H, dil, Ls, 2 * E), F32),
        compiler_params=_cp("parallel", "parallel"),
    )(q, k, k, k, v, v, v, bias)


def band_bwd_dq(q, k, v, bias, dox, *, scale):
    H, dil, Ls, E = q.shape
    T, W = _dl_tiles(Ls)
    nb = Ls // T

    def body(q_ref, kp_ref, kc_ref, kn_ref, vp_ref, vc_ref, vn_ref, b_ref, dox_ref, dq_ref, db_ref):
        d, n = pl.program_id(0), pl.program_id(1)
        mask = _band_mask_q(n, T, W, Ls)
        first = jnp.logical_and(d == 0, n == 0)

        @pl.when(first)
        def _():
            db_ref[...] = jnp.zeros_like(db_ref)

        for h in range(H):
            kw = _win(kp_ref, kc_ref, kn_ref, h, T)
            vw = _win(vp_ref, vc_ref, vn_ref, h, T)
            dox = dox_ref[h, 0]
            do, lse, dm = dox[:, 0:E], dox[:, DOX_LSE:DOX_LSE + 1], dox[:, DOX_DM:DOX_DM + 1]
            s = _dot(q_ref[h, 0], kw, ((1,), (1,))) * scale + b_ref[h]
            p = jnp.where(mask, jnp.exp(jnp.where(mask, s, 0.0) - lse), 0.0)
            dp = _dot(do, vw, ((1,), (1,)))
            ds = p * (dp - dm)
            dq_ref[h, 0] = (_dot(ds, kw, ((1,), (0,))) * scale).astype(dq_ref.dtype)
            db_ref[h] += ds

    cur = pl.BlockSpec((H, 1, T, E), lambda d, n: (0, d, n, 0))
    bspec = pl.BlockSpec((H, T, W), lambda d, n: (0, 0, 0))
    return pl.pallas_call(
        body, name=f"band_bwd_dq_{dil}", grid=(dil, nb),
        in_specs=[cur] + _win_specs(T, E, nb) + _win_specs(T, E, nb) + [bspec,
                  pl.BlockSpec((H, 1, T, 2 * E), lambda d, n: (0, d, n, 0))],
        out_specs=[cur, bspec],
        out_shape=[jax.ShapeDtypeStruct((H, dil, Ls, E), MXU_DTYPE), jax.ShapeDtypeStruct((H, T, W), F32)],
        compiler_params=_cp("arbitrary", "arbitrary"),
    )(q, k, k, k, v, v, v, bias, dox)


def band_bwd_dkv(q, k, v, bias_t, dox, *, scale):
    H, dil, Ls, E = k.shape
    T, W = _dl_tiles(Ls)
    nb = Ls // T

    def body(qp_ref, qc_ref, qn_ref, k_ref, v_ref, b_ref, dp_ref, dc_ref, dn_ref, dk_ref, dv_ref):
        n = pl.program_id(1)
        iw = lax.broadcasted_iota(jnp.int32, (W, T), 0)
        j = lax.broadcasted_iota(jnp.int32, (W, T), 1)
        qpos = n * T + iw - DL_HALF
        mask = (jnp.abs(j + DL_HALF - iw) <= DL_HALF) & (qpos >= 0) & (qpos < Ls)
        for h in range(H):
            qw = _win(qp_ref, qc_ref, qn_ref, h, T)
            doxw = _win(dp_ref, dc_ref, dn_ref, h, T)
            dow, lsew, dmw = doxw[:, 0:E], doxw[:, DOX_LSE:DOX_LSE + 1], doxw[:, DOX_DM:DOX_DM + 1]
            s = _dot(qw, k_ref[h, 0], ((1,), (1,))) * scale + b_ref[h]
            p = jnp.where(mask, jnp.exp(jnp.where(mask, s, 0.0) - lsew), 0.0)
            dv_ref[h, 0] = _dot(p, dow, ((0,), (0,))).astype(dv_ref.dtype)
            dp = _dot(dow, v_ref[h, 0], ((1,), (1,)))
            ds = p * (dp - dmw)
            dk_ref[h, 0] = (_dot(ds, qw, ((0,), (0,))) * scale).astype(dk_ref.dtype)

    cur = pl.BlockSpec((H, 1, T, E), lambda d, n: (0, d, n, 0))
    return pl.pallas_call(
        body, name=f"band_bwd_dkv_{dil}", grid=(dil, nb),
        in_specs=_win_specs(T, E, nb) + [cur, cur, pl.BlockSpec((H, W, T), lambda d, n: (0, 0, 0))]
        + _win_specs(T, 2 * E, nb),
        out_specs=[cur, cur],
        out_shape=[jax.ShapeDtypeStruct((H, dil, Ls, E), MXU_DTYPE)] * 2,
        compiler_params=_cp("parallel", "parallel"),
    )(q, q, q, k, v, bias_t, dox, dox, dox)


def dl_merge_fwd(oxs, *, R=1024):
    rows = oxs[0].shape[0]
    R = min(R, rows)
    E = DL_HD

    def body(a_ref, b_ref, c_ref, o_ref):
        vals = [r[...] for r in (a_ref, b_ref, c_ref)]
        o_ref[...] = _dl_merge_fn(*[t[:, 0:E] for t in vals], *[t[:, OX_LSE:OX_LSE + 1] for t in vals])[0]

    spec = pl.BlockSpec((R, 2 * E), lambda i: (i, 0))
    return pl.pallas_call(
        body, name="dl_merge", grid=(rows // R,), in_specs=[spec] * 3,
        out_specs=pl.BlockSpec((R, E), lambda i: (i, 0)), out_shape=jax.ShapeDtypeStruct((rows, E), F32),
        compiler_params=_cp("parallel"),
    )(*oxs)


def dl_merge_bwd(oxs, do, *, R=1024):
    rows = oxs[0].shape[0]
    R = min(R, rows)
    E = DL_HD

    def body(a_ref, b_ref, c_ref, do_ref, da_ref, db_ref, dc_ref):
        vals = [r[...] for r in (a_ref, b_ref, c_ref)]
        os_ = [t[:, 0:E] for t in vals]
        ls_ = [t[:, OX_LSE:OX_LSE + 1] for t in vals]
        _, vjp = jax.vjp(_dl_merge_fn, *os_, *ls_)
        g = vjp((do_ref[...],))
        for k, d_ref in enumerate((da_ref, db_ref, dc_ref)):
            dm = jnp.sum(g[k] * os_[k], axis=1, keepdims=True) - g[3 + k]
            d_ref[:, 0:E] = g[k]
            d_ref[:, DOX_LSE:DOX_DM] = ls_[k] + jnp.zeros((R, DOX_DM - DOX_LSE), F32)
            d_ref[:, DOX_DM:2 * E] = dm + jnp.zeros((R, 2 * E - DOX_DM), F32)

    spec = pl.BlockSpec((R, 2 * E), lambda i: (i, 0))
    return pl.pallas_call(
        body, name="dl_merge_bwd", grid=(rows // R,), in_specs=[spec] * 3 + [pl.BlockSpec((R, E), lambda i: (i, 0))],
        out_specs=[spec] * 3, out_shape=[jax.ShapeDtypeStruct((rows, 2 * E), F32)] * 3,
        compiler_params=_cp("parallel"),
    )(*oxs, do)


def _dl_bias_tables(rel_bias, dil, T):
    W = T + 2 * DL_HALF
    bq, bk = _dl_bucket_tables(dil, T)
    idx = np.concatenate([bq.reshape(-1), bk.reshape(-1)])
    onehot_t = (np.arange(REL_BUCKETS)[:, None] == idx[None, :]).astype(np.float32)
    tab = matmul(rel_bias.T, jnp.asarray(onehot_t), exact=True, name=f"dl_bias_{dil}", tm=DL_HEADS, tk=REL_BUCKETS,
                 tn=_tile(2 * T * W, (8192, 4096, 2048, 1024, 512, 256, 128)))
    return tab[:, :T * W].reshape(DL_HEADS, T, W), tab[:, T * W:].reshape(DL_HEADS, W, T), bq


def dl_layer_fwd(x, ng, p):
    S = x.shape[0]
    hn = _prenorm("dl", x, ng)
    nqkv = 3 * len(DL_PAIRS) * DL_W
    uqkv = matmul(hn, p["w_in"], name="dl_in_qkv", b_cols=(0, nqkv), out_dtype=MXU_DTYPE)
    ugate = matmul(hn, p["w_in"], name="dl_in_gate", b_cols=(nqkv, DL_W))
    scale = DL_HD ** -0.5
    per_group, ox_hm = [], []
    for gi, (window, dil) in enumerate(DL_PAIRS):
        base = gi * 3 * DL_W
        T, _ = _dl_tiles(S // dil)
        qs, ks, vs = [_to_stream(uqkv[:, base + c * DL_W:base + (c + 1) * DL_W], dil) for c in range(3)]
        bias, bias_t, bq = _dl_bias_tables(p["rel_bias"], dil, T)
        ox_s = band_fwd(qs, ks, vs, bias, scale=scale)
        per_group.append(dict(qs=qs, ks=ks, vs=vs, bias=bias, bias_t=bias_t, bq=bq, dil=dil))
        ox_hm.append(_stream_to_hm(ox_s))
    om = dl_merge_fwd(ox_hm)
    o = om.reshape(DL_HEADS, S, DL_HD).transpose(1, 0, 2).reshape(S, DL_W)
    g = rowwise_fwd("dl_gate", _gate_fn, [(o, 0, False), (ugate, 0, False)], [], [(DL_W, MXU_DTYPE)], W=DL_W)[0]
    xn = matmul(g, p["w_out"], residual=x, name="dl_out")
    return xn, dict(x=x, ng=ng, hn=hn, ugate=ugate, per_group=per_group, ox_hm=ox_hm, o=o, g=g)


def dl_layer_bwd(sv, p, dx):
    ugate = sv["ugate"]
    S = ugate.shape[0]
    scale = DL_HD ** -0.5
    dg = matmul(dx, p["w_out"], tb=True, name="dl_dg")
    dw_out = matmul(sv["g"], dx, ta=True, out_dtype=GRAD_WIRE_DTYPE, name="dl_dw_out")
    do, dgate = rowwise_bwd("dl_gate_bwd", _gate_fn, [(sv["o"], 0, False), (sv["ugate"], 0, False)], [], [dg], W=DL_W,
                            diff_rows=[0, 1], diff_shared=[])
    do_hm = do.reshape(S, DL_HEADS, DL_HD).transpose(1, 0, 2).reshape(DL_HEADS * S, DL_HD)
    dox_hm = dl_merge_bwd(sv["ox_hm"], do_hm)
    parts, dbs, onehots = [], [], []
    for gi, pg in enumerate(sv["per_group"]):
        dil = pg["dil"]
        T, W = _dl_tiles(S // dil)
        dox_s = _hm_to_stream(dox_hm[gi], dil)
        dq_s, dbias = band_bwd_dq(pg["qs"], pg["ks"], pg["vs"], pg["bias"], dox_s, scale=scale)
        dk_s, dv_s = band_bwd_dkv(pg["qs"], pg["ks"], pg["vs"], pg["bias_t"], dox_s, scale=scale)
        parts += [_from_stream(dq_s), _from_stream(dk_s), _from_stream(dv_s)]
        dbs.append(dbias.reshape(DL_HEADS, T * W))
        onehots.append((pg["bq"].reshape(-1)[:, None] == np.arange(REL_BUCKETS)[None, :]).astype(np.float32))
    drel = matmul(jnp.concatenate(dbs, axis=1), jnp.asarray(np.concatenate(onehots, axis=0)), exact=True,
                  name="dl_drel", tm=DL_HEADS, tn=REL_BUCKETS, tk=2048)
    du = _cat_mxu(parts + [dgate])
    dx_prev, dng, dw_in = _in_out_bwd("dl", sv["x"], sv["ng"], sv["hn"], du, p["w_in"], dx)
    return dx_prev, dict(w_in=dw_in, w_out=dw_out, rel_bias=drel.T, ng=dng)


WEIGHT_ORDER = ['norm_g', 'final_g', 'rel_bias', 'hgrn_lb', 'ssd_w_in', 'ssd_conv_w', 'ssd_conv_b', 'ssd_dt_bias',
                'ssd_a_log', 'ssd_d', 'ssd_norm_g', 'ssd_w_out', 'hg_w_in', 'hg_norm_g', 'hg_w_out', 'at_w_in',
                'at_q_norm_g', 'at_k_norm_g', 'at_w_out', 'dl_w_in', 'dl_w_out']
BIG_IN = ['ssd_w_in', 'hg_w_in', 'at_w_in', 'dl_w_in']
BIG_OUT = ['ssd_w_out', 'hg_w_out', 'at_w_out', 'dl_w_out']
BIG = BIG_IN + BIG_OUT
SMALL = [n for n in WEIGHT_ORDER if n not in BIG]
LANES = 128


def _pack(arrs):
    flat = jnp.concatenate([a.reshape(-1).astype(F32) for a in arrs])
    n = flat.shape[0]
    rows = -(-n // (8 * LANES)) * 8
    return jnp.pad(flat, (0, rows * LANES - n)).reshape(rows, LANES)


def _unpack(buf, shapes):
    flat = buf.reshape(-1)
    out, off = [], 0
    for shp in shapes:
        n = int(np.prod(shp)) if len(shp) else 1
        out.append(flat[off:off + n].reshape(shp))
        off += n
    return out


def kernel(x, norm_g, final_g, rel_bias, hgrn_lb, ssd_w_in, ssd_conv_w, ssd_conv_b, ssd_dt_bias, ssd_a_log, ssd_d, ssd_norm_g, ssd_w_out, hg_w_in, hg_norm_g, hg_w_out, at_w_in, at_q_norm_g, at_k_norm_g, at_w_out, dl_w_in, dl_w_out, loss_target, m_norm_g, m_final_g, m_rel_bias, m_hgrn_lb, m_ssd_w_in, m_ssd_conv_w, m_ssd_conv_b, m_ssd_dt_bias, m_ssd_a_log, m_ssd_d, m_ssd_norm_g, m_ssd_w_out, m_hg_w_in, m_hg_norm_g, m_hg_w_out, m_at_w_in, m_at_q_norm_g, m_at_k_norm_g, m_at_w_out, m_dl_w_in, m_dl_w_out, v_norm_g, v_final_g, v_rel_bias, v_hgrn_lb, v_ssd_w_in, v_ssd_conv_w, v_ssd_conv_b, v_ssd_dt_bias, v_ssd_a_log, v_ssd_d, v_ssd_norm_g, v_ssd_w_out, v_hg_w_in, v_hg_norm_g, v_hg_w_out, v_at_w_in, v_at_q_norm_g, v_at_k_norm_g, v_at_w_out, v_dl_w_in, v_dl_w_out):
    w = dict(norm_g=norm_g, final_g=final_g, rel_bias=rel_bias, hgrn_lb=hgrn_lb, ssd_w_in=ssd_w_in, ssd_conv_w=ssd_conv_w, ssd_conv_b=ssd_conv_b, ssd_dt_bias=ssd_dt_bias, ssd_a_log=ssd_a_log, ssd_d=ssd_d, ssd_norm_g=ssd_norm_g, ssd_w_out=ssd_w_out, hg_w_in=hg_w_in, hg_norm_g=hg_norm_g, hg_w_out=hg_w_out, at_w_in=at_w_in, at_q_norm_g=at_q_norm_g, at_k_norm_g=at_k_norm_g, at_w_out=at_w_out, dl_w_in=dl_w_in, dl_w_out=dl_w_out)
    m = dict(norm_g=m_norm_g, final_g=m_final_g, rel_bias=m_rel_bias, hgrn_lb=m_hgrn_lb, ssd_w_in=m_ssd_w_in, ssd_conv_w=m_ssd_conv_w, ssd_conv_b=m_ssd_conv_b, ssd_dt_bias=m_ssd_dt_bias, ssd_a_log=m_ssd_a_log, ssd_d=m_ssd_d, ssd_norm_g=m_ssd_norm_g, ssd_w_out=m_ssd_w_out, hg_w_in=m_hg_w_in, hg_norm_g=m_hg_norm_g, hg_w_out=m_hg_w_out, at_w_in=m_at_w_in, at_q_norm_g=m_at_q_norm_g, at_k_norm_g=m_at_k_norm_g, at_w_out=m_at_w_out, dl_w_in=m_dl_w_in, dl_w_out=m_dl_w_out)
    v = dict(norm_g=v_norm_g, final_g=v_final_g, rel_bias=v_rel_bias, hgrn_lb=v_hgrn_lb, ssd_w_in=v_ssd_w_in, ssd_conv_w=v_ssd_conv_w, ssd_conv_b=v_ssd_conv_b, ssd_dt_bias=v_ssd_dt_bias, ssd_a_log=v_ssd_a_log, ssd_d=v_ssd_d, ssd_norm_g=v_ssd_norm_g, ssd_w_out=v_ssd_w_out, hg_w_in=v_hg_w_in, hg_norm_g=v_hg_norm_g, hg_w_out=v_hg_w_out, at_w_in=v_at_w_in, at_q_norm_g=v_at_q_norm_g, at_k_norm_g=v_at_k_norm_g, at_w_out=v_at_w_out, dl_w_in=v_dl_w_in, dl_w_out=v_dl_w_out)
    me = 4 * lax.axis_index("x") + 2 * lax.axis_index("y") + lax.axis_index("c")
    xs = x[0]
    S = xs.shape[0]

    shard2d = {n: w[n][0] for n in BIG}
    wire = {n: shard2d[n].astype(MXU_DTYPE) for n in BIG}

    def ag(names):
        return ("ag", [wire[n] for n in names])

    def assemble(names, blks):
        out = {}
        for n, blk in zip(names, blks):
            r, c = shard2d[n].shape
            out[n] = blk.transpose(1, 0, 2).reshape(r, N_DEV * c) if n in BIG_IN else blk.reshape(N_DEV * r, c)
        return out

    def a2a(names, gw):
        bufs = []
        for n in names:
            r, c = shard2d[n].shape
            bufs.append(gw[n].reshape(r, N_DEV, c).transpose(1, 0, 2) if n in BIG_IN else gw[n].reshape(N_DEV, r, c))
        return ("a2a", bufs)

    ssd_w, hg_w, at_w, dl_w = (["ssd_w_in", "ssd_w_out"], ["hg_w_in", "hg_w_out"], ["at_w_in", "at_w_out"],
                               ["dl_w_in", "dl_w_out"])
    full = assemble(ssd_w, allgather_two_level_multi([wire[n] for n in ssd_w], name="allgather_ssd_weights"))
    ncw = ssd_conv_w.shape[2]
    nhg = hg_norm_g.shape[1]
    small_shard = jnp.zeros((8, 512), F32)
    small_shard = small_shard.at[:SSD_CONV, :ncw].set(ssd_conv_w[0]).at[SSD_CONV, :nhg].set(hg_norm_g[0])
    small_all = allgather_two_level(small_shard, name="allgather_small_weights")
    conv_w_full = small_all[:, :SSD_CONV, :ncw].transpose(1, 0, 2).reshape(SSD_CONV, N_DEV * ncw)
    hg_norm_full = small_all[:, SSD_CONV, :nhg].reshape(1, N_DEV * nhg)

    p_ssd = dict(w_in=jnp.pad(full["ssd_w_in"], ((0, 0), (0, SSD_IN_PAD - SSD_IN))), w_out=full["ssd_w_out"],
                 conv_w=conv_w_full, conv_b=ssd_conv_b,
                 dt_bias=jnp.pad(ssd_dt_bias.reshape(1, 2 * SSD_HEADS), ((0, 0), (0, 128 - 2 * SSD_HEADS))),
                 alog=jnp.pad(ssd_a_log.reshape(1, 2 * SSD_HEADS), ((0, 0), (0, 128 - 2 * SSD_HEADS))),
                 dexp=jnp.repeat(ssd_d.reshape(-1), SSD_HEADDIM)[None, :], norm_g=ssd_norm_g)
    x1, sv0, got, got_at_in = ssd_layer_fwd(xs, norm_g[0:1], p_ssd, comm=ag(hg_w), comm1=ag(["at_w_in"]))
    full.update(assemble(hg_w, got))
    p_hg = dict(w_in=full["hg_w_in"], w_out=full["hg_w_out"], norm_g=hg_norm_full, hgrn_lb=hgrn_lb)
    x2, sv1, got_at_out, _ = hg_layer_fwd(x1, norm_g[1:2], p_hg, comm0=ag(["at_w_out"]))
    full.update(assemble(at_w, got_at_in + got_at_out))
    p_at = dict(w_in=full["at_w_in"], w_out=full["at_w_out"], q_g=at_q_norm_g, k_g=at_k_norm_g)
    x3, sv2, got_dl = at_layer_fwd(x2, norm_g[2:3], p_at, comm=ag(dl_w))
    full.update(assemble(dl_w, got_dl))
    p_dl = dict(w_in=full["dl_w_in"], w_out=full["dl_w_out"], rel_bias=rel_bias)
    x4, sv3 = dl_layer_fwd(x3, norm_g[3:4], p_dl)
    loss_part, dx4, dfinal = loss_head(x4, final_g[None, :], loss_target[0])
    dx3, g3 = dl_layer_bwd(sv3, p_dl, dx4)
    dx2, g2, recv_dl = at_layer_bwd(sv2, p_at, dx3, comm=a2a(dl_w, dict(dl_w_in=g3["w_in"], dl_w_out=g3["w_out"])))
    dx1, g1, recv_at = hg_layer_bwd(sv1, p_hg, dx2, comm=a2a(at_w, dict(at_w_in=g2["w_in"], at_w_out=g2["w_out"])))
    dx0, g0, recv_hg, recv_ssd = ssd_layer_bwd(
        sv0, p_ssd, dx1, comm=a2a(hg_w, dict(hg_w_in=g1["w_in"], hg_w_out=g1["w_out"])),
        tail_comm=lambda g_in, g_out: (a2a(["ssd_w_in"], dict(ssd_w_in=g_in)) if g_out is None
                                       else a2a(["ssd_w_out"], dict(ssd_w_out=g_out))))
    recv = dict(zip(ssd_w + hg_w + at_w + dl_w, recv_ssd + recv_hg + recv_at + recv_dl))

    small_full = dict(
        norm_g=jnp.concatenate([g0["ng"], g1["ng"], g2["ng"], g3["ng"]], axis=0), final_g=dfinal[0],
        rel_bias=g3["rel_bias"], hgrn_lb=g1["hgrn_lb"], ssd_conv_w=g0["conv_w"][None], ssd_conv_b=g0["conv_b"],
        ssd_dt_bias=g0["dt_bias"].reshape(1, 2, SSD_HEADS), ssd_a_log=g0["a_log"].reshape(1, 2, SSD_HEADS),
        ssd_d=g0["d"], ssd_norm_g=g0["norm_g"], hg_norm_g=g1["norm_g"], at_q_norm_g=g2["q_g"], at_k_norm_g=g2["k_g"])
    packed = _pack([loss_part[0, 0:1]] + [small_full[n] for n in SMALL])
    summed = sum_parts(allgather_two_level(packed, name="allgather_small_grads"), name="sum_small_grads")
    parts = _unpack(summed, [()] + [small_full[n].shape for n in SMALL])
    loss = parts[0]
    gsmall = dict(zip(SMALL, parts[1:]))
    gsmall["ssd_conv_w"] = lax.dynamic_slice_in_dim(gsmall["ssd_conv_w"], me * ncw, ncw, axis=2)
    gsmall["hg_norm_g"] = lax.dynamic_slice_in_dim(gsmall["hg_norm_g"], me * nhg, nhg, axis=1)
    shapes = [w[n].shape for n in SMALL]
    d_p, m_p, v_p = adamw_plain(_pack([w[n] for n in SMALL]), _pack([gsmall[n] for n in SMALL]),
                                _pack([m[n] for n in SMALL]), _pack([v[n] for n in SMALL]), name="adamw_small")
    grads = dict(gsmall)
    deltas = dict(zip(SMALL, _unpack(d_p, shapes)))
    new_m = dict(zip(SMALL, _unpack(m_p, shapes)))
    new_v = dict(zip(SMALL, _unpack(v_p, shapes)))

    for n in BIG:
        gs, ds, ms, vs = adamw_sum(recv[n], shard2d[n], m[n][0], v[n][0], name=f"adamw_{n}")
        grads[n], deltas[n], new_m[n], new_v[n] = gs[None], ds[None], ms[None], vs[None]

    return (loss, dx0[None], *[grads[n] for n in WEIGHT_ORDER], *[deltas[n] for n in WEIGHT_ORDER],
            *[new_m[n] for n in WEIGHT_ORDER], *[new_v[n] for n in WEIGHT_ORDER])
```

```python
import functools
import math

import jax
import jax.numpy as jnp
import numpy as np
from jax import lax
from jax.experimental import pallas as pl
from jax.experimental.pallas import tpu as pltpu

F32 = jnp.float32
MXU_DTYPE = jnp.bfloat16
GRAD_WIRE_DTYPE = jnp.bfloat16
HIGHEST = lax.Precision.HIGHEST
MESH_ID = pl.DeviceIdType.MESH
N_DEV = 8

D_MODEL = 1024
EPS = 1e-6
NEG_BIG = -1e30

SSD_DI = 2048
SSD_HEADDIM = 64
SSD_HEADS = 32
SSD_GROUPS = 4
SSD_HPG = 8
SSD_STATE = 128
SSD_CONV = 7
SSD_CHUNK = 128
SSD_CONV_CH = SSD_DI + 2 * SSD_GROUPS * SSD_STATE
SSD_IN = SSD_DI + SSD_CONV_CH + 2 * SSD_HEADS
SSD_IN_PAD = 5376

HG_CHUNK = 32
HG_HEADS = 8
HG_D = 128
HG_W = 1024

AT_HEADS = 16
AT_KV = 8
AT_HD = 128
AT_QW = 2048
AT_KW = 1024
GRID_W = 64
ROPE_THETA = 10000.0

DL_PAIRS = ((128, 1), (512, 4), (2048, 16))
DL_HEADS = 16
DL_HD = 64
DL_W = 1024
DL_HALF = 64
REL_BUCKETS = 32
REL_MAX_DIST = 1024

ADAM_LR = 0.001
ADAM_B1 = 0.9
ADAM_B2 = 0.999
ADAM_EPS = 1e-08
ADAM_WD = 0.01
ADAM_STEP = 10

VMEM_LIMIT = 56 * 1024 * 1024


def _cp(*sem):
    return pltpu.CompilerParams(dimension_semantics=tuple(sem), vmem_limit_bytes=VMEM_LIMIT)


def _tile(n, cands=(1024, 768, 512, 384, 256, 128)):
    for c in cands:
        if n % c == 0:
            return c
    return n


def _dot(a, b, dims):
    return lax.dot_general(a.astype(MXU_DTYPE), b.astype(MXU_DTYPE), (dims, ((), ())), preferred_element_type=F32)


def _dot_exact(a, b, dims):
    return lax.dot_general(a, b, (dims, ((), ())), precision=HIGHEST, preferred_element_type=F32)


def _silu(x):
    return x * jax.nn.sigmoid(x)


def _my_pos():
    return lax.axis_index("x"), lax.axis_index("y"), lax.axis_index("c")


def _flat(px, py, pc):
    return 4 * px + 2 * py + pc


def _peers():
    x_, y_, c_ = _my_pos()
    out = []
    for k in range(1, N_DEV):
        fx, fy, fc = (k >> 2) & 1, (k >> 1) & 1, k & 1
        out.append(((1 - x_) if fx else x_, (1 - y_) if fy else y_, (1 - c_) if fc else c_))
    return out


def _comm_copies(kind, in_refs, out_refs, send_sems, recv_sems, local_sems):
    me = _flat(*_my_pos())
    local, starts, waits = [], [], []
    for b, (i_ref, o_ref) in enumerate(zip(in_refs, out_refs)):
        local.append(pltpu.make_async_copy(i_ref if kind == "ag" else i_ref.at[me], o_ref.at[me], local_sems.at[b]))
        for k, p in enumerate(_peers()):
            src = i_ref if kind == "ag" else i_ref.at[_flat(*p)]
            starts.append(pltpu.make_async_remote_copy(
                src_ref=src, dst_ref=o_ref.at[me], send_sem=send_sems.at[b, k], recv_sem=recv_sems.at[b, k],
                device_id=p, device_id_type=MESH_ID))
            waits.append(pltpu.make_async_remote_copy(
                src_ref=src, dst_ref=o_ref.at[_flat(*p)], send_sem=send_sems.at[b, k], recv_sem=recv_sems.at[b, k],
                device_id=p, device_id_type=MESH_ID))
    return local, starts, waits


def pcall(body, comm, *, name, grid, in_specs, out_specs, out_shape, scratch_shapes=(), compiler_params=None):
    single = not isinstance(out_specs, (list, tuple))
    out_specs_l = [out_specs] if single else list(out_specs)
    out_shape_l = [out_shape] if single else list(out_shape)
    if comm is None:
        return pl.pallas_call(body, name=name, grid=grid, in_specs=in_specs, out_specs=out_specs, out_shape=out_shape,
                              scratch_shapes=list(scratch_shapes), compiler_params=compiler_params)
    kind, bufs = comm
    nb, n_in, n_out, n_scr = len(bufs), len(in_specs), len(out_specs_l), len(scratch_shapes)
    c_shape = [jax.ShapeDtypeStruct(((N_DEV,) + b.shape) if kind == "ag" else b.shape, b.dtype) for b in bufs]
    anyspec = pl.BlockSpec(memory_space=pl.ANY)

    def body2(*refs):
        ins, c_ins = refs[:n_in], refs[n_in:n_in + nb]
        outs = refs[n_in + nb:n_in + nb + n_out]
        c_outs = refs[n_in + nb + n_out:n_in + 2 * nb + n_out]
        scr = refs[n_in + 2 * nb + n_out:n_in + 2 * nb + n_out + n_scr]
        send_sems, recv_sems, local_sems = refs[n_in + 2 * nb + n_out + n_scr:]
        first = last = None
        for ax, g in enumerate(grid):
            pid = pl.program_id(ax)
            first = (pid == 0) if first is None else jnp.logical_and(first, pid == 0)
            last = (pid == g - 1) if last is None else jnp.logical_and(last, pid == g - 1)

        @pl.when(first)
        def _():
            local, starts, _ = _comm_copies(kind, c_ins, c_outs, send_sems, recv_sems, local_sems)
            for cp in local + starts:
                cp.start()

        body(*ins, *outs, *scr)

        @pl.when(last)
        def _():
            local, _, waits = _comm_copies(kind, c_ins, c_outs, send_sems, recv_sems, local_sems)
            for cp in waits + local:
                cp.wait()

    call = pl.pallas_call(
        body2, name=name, grid=grid, in_specs=list(in_specs) + [anyspec] * nb,
        out_specs=out_specs_l + [anyspec] * nb, out_shape=out_shape_l + c_shape,
        scratch_shapes=list(scratch_shapes) + [pltpu.SemaphoreType.DMA((nb, N_DEV - 1)),
                                               pltpu.SemaphoreType.DMA((nb, N_DEV - 1)), pltpu.SemaphoreType.DMA((nb,))],
        compiler_params=compiler_params)

    def run(*args):
        res = call(*args, *bufs)
        own = res[:n_out]
        return (own[0] if single else list(own)), list(res[n_out:])

    return run


def matmul(a, b, *, name, ta=False, tb=False, residual=None, out_dtype=F32, exact=False, tm=None, tn=None, tk=None,
           b_cols=None, comm=None):
    M, K = (a.shape[1], a.shape[0]) if ta else a.shape
    n0, N = b_cols if b_cols is not None else (0, b.shape[0] if tb else b.shape[1])
    tm = tm or _tile(M, (1024, 512, 256, 128))
    tn = tn or _tile(N, (1024, 768, 512, 384, 256, 128))
    tk = tk or _tile(K, (2048, 1024, 768, 512, 384, 256, 128))
    nk = K // tk
    dims = (((0,) if ta else (1,)), ((1,) if tb else (0,)))

    def body(*refs):
        if residual is None:
            a_ref, b_ref, o_ref, acc = refs
            r_ref = None
        else:
            a_ref, b_ref, r_ref, o_ref, acc = refs
        k = pl.program_id(2)

        @pl.when(k == 0)
        def _():
            acc[...] = jnp.zeros_like(acc)

        if exact:
            acc[...] += _dot_exact(a_ref[...], b_ref[...], dims)
        else:
            acc[...] += _dot(a_ref[...], b_ref[...], dims)

        @pl.when(k == nk - 1)
        def _():
            r = acc[...]
            if r_ref is not None:
                r = r + r_ref[...]
            o_ref[...] = r.astype(o_ref.dtype)

    a_spec = pl.BlockSpec((tk, tm), lambda i, j, k: (k, i)) if ta else pl.BlockSpec((tm, tk), lambda i, j, k: (i, k))
    assert n0 % tn == 0
    jb = n0 // tn
    b_spec = (pl.BlockSpec((tn, tk), lambda i, j, k: (j + jb, k)) if tb
              else pl.BlockSpec((tk, tn), lambda i, j, k: (k, j + jb)))
    in_specs = [a_spec, b_spec]
    args = [a, b]
    if residual is not None:
        in_specs.append(pl.BlockSpec((tm, tn), lambda i, j, k: (i, j)))
        args.append(residual)
    return pcall(
        body, comm, name=name, grid=(M // tm, N // tn, nk), in_specs=in_specs,
        out_specs=pl.BlockSpec((tm, tn), lambda i, j, k: (i, j)),
        out_shape=jax.ShapeDtypeStruct((M, N), out_dtype),
        scratch_shapes=[pltpu.VMEM((tm, tn), F32)],
        compiler_params=_cp("parallel", "parallel", "arbitrary"),
    )(*args)


def _row_specs2(rows, shared, R, W, ncb):
    specs = []
    for arr, col0, per_j, *wd in rows:
        w = wd[0] if wd else W
        if per_j:
            assert col0 % ncb == 0
            specs.append(pl.BlockSpec((R, ncb * w), lambda i, c=col0 // ncb: (i, c)))
        else:
            specs.append(pl.BlockSpec((R, w), lambda i, c=col0: (i, c)))
    for arr, per_j in shared:
        specs.append(pl.BlockSpec((arr.shape[0], ncb * W if per_j else arr.shape[1]), lambda i: (0, 0)))
    return specs


def _col_block(ref, per_j, j, w):
    return ref[:, j * w:(j + 1) * w] if per_j else ref[...]


def rowwise_fwd(name, fn, rows, shared, outs, *, W, ncb=1, R=256):
    S = rows[0][0].shape[0]
    R = min(R, S)
    nr, ns = len(rows), len(shared)
    widths = [(r[3] if len(r) > 3 else W) for r in rows]
    per_j = [r[2] for r in rows] + [s[1] for s in shared]
    ws = widths + [W] * ns

    def body(*refs):
        for j in range(ncb):
            vals = [_col_block(refs[k], per_j[k], j, ws[k]) for k in range(nr + ns)]
            res = fn(*vals)
            for o_ref, r, (wo, _) in zip(refs[nr + ns:], res, outs):
                o_ref[:, j * wo:(j + 1) * wo] = r.astype(o_ref.dtype)

    return pl.pallas_call(
        body, name=name, grid=(S // R,),
        in_specs=_row_specs2(rows, shared, R, W, ncb),
        out_specs=[pl.BlockSpec((R, ncb * w), lambda i: (i, 0)) for w, _ in outs],
        out_shape=[jax.ShapeDtypeStruct((S, ncb * w), dt) for w, dt in outs],
        compiler_params=_cp("parallel"),
    )(*[r[0] for r in rows], *[s[0] for s in shared])


def rowwise_bwd(name, fn, rows, shared, cots, *, W, ncb=1, R=256, diff_rows, diff_shared, add=None):
    S = rows[0][0].shape[0]
    R = min(R, S)
    nr, ns, nc = len(rows), len(shared), len(cots)
    widths = [(r[3] if len(r) > 3 else W) for r in rows]
    per_j = [r[2] for r in rows] + [s[1] for s in shared]
    ws = widths + [W] * ns
    wo = [c.shape[1] // ncb for c in cots]
    dws = [widths[r] for r in diff_rows]

    def body(*refs):
        ins = refs[:nr + ns]
        ct_refs = refs[nr + ns:nr + ns + nc]
        pos = nr + ns + nc
        add_ref = None
        if add is not None:
            add_ref = refs[pos]
            pos += 1
        drow_refs = refs[pos:pos + len(diff_rows)]
        dsh_refs = refs[pos + len(diff_rows):]
        i = pl.program_id(0)
        tot = [None] * len(diff_shared)
        for j in range(ncb):
            vals = [_col_block(ins[k], per_j[k], j, ws[k]) for k in range(nr + ns)]

            def f(*dv):
                full = list(vals)
                for idx, v in zip(list(diff_rows) + [nr + s for s in diff_shared], dv):
                    full[idx] = v
                return tuple(fn(*full))

            prim = [vals[idx] for idx in diff_rows] + [vals[nr + s] for s in diff_shared]
            _, vjp = jax.vjp(f, *prim)
            grads = vjp(tuple(c[:, j * w:(j + 1) * w] for c, w in zip(ct_refs, wo)))
            for k, (d_ref, w) in enumerate(zip(drow_refs, dws)):
                g = grads[k]
                if k == 0 and add_ref is not None:
                    g = g + add_ref[:, j * w:(j + 1) * w]
                d_ref[:, j * w:(j + 1) * w] = g
            for k, (d_ref, s) in enumerate(zip(dsh_refs, diff_shared)):
                g = grads[len(diff_rows) + k]
                if shared[s][1]:
                    @pl.when(i == 0)
                    def _(d_ref=d_ref, g=g, j=j):
                        d_ref[:, j * W:(j + 1) * W] = g

                    @pl.when(i != 0)
                    def _(d_ref=d_ref, g=g, j=j):
                        d_ref[:, j * W:(j + 1) * W] += g
                else:
                    tot[k] = g if tot[k] is None else tot[k] + g
        for k, (d_ref, s) in enumerate(zip(dsh_refs, diff_shared)):
            if not shared[s][1]:
                @pl.when(i == 0)
                def _(d_ref=d_ref, g=tot[k]):
                    d_ref[...] = g

                @pl.when(i != 0)
                def _(d_ref=d_ref, g=tot[k]):
                    d_ref[...] += g

    in_specs = _row_specs2(rows, shared, R, W, ncb)
    in_specs += [pl.BlockSpec((R, ncb * w), lambda i: (i, 0)) for w in wo]
    args = [r[0] for r in rows] + [s[0] for s in shared] + list(cots)
    if add is not None:
        in_specs.append(pl.BlockSpec((R, ncb * dws[0]), lambda i: (i, 0)))
        args.append(add)
    out_specs = [pl.BlockSpec((R, ncb * w), lambda i: (i, 0)) for w in dws]
    out_shape = [jax.ShapeDtypeStruct((S, ncb * w), F32) for w in dws]
    for s in diff_shared:
        arr, pj = shared[s]
        shp = (arr.shape[0], ncb * W if pj else arr.shape[1])
        out_specs.append(pl.BlockSpec(shp, lambda i: (0, 0)))
        out_shape.append(jax.ShapeDtypeStruct(shp, F32))
    return pl.pallas_call(
        body, name=name, grid=(S // R,), in_specs=in_specs, out_specs=out_specs, out_shape=out_shape,
        compiler_params=_cp("arbitrary"),
    )(*args)


def _rms(x, g):
    return x * lax.rsqrt(jnp.mean(x * x, axis=-1, keepdims=True) + EPS) * g


def _prenorm_fn(x, g):
    return (_rms(x, g),)


def loss_head(x, g, tgt, *, R=256):
    S, D = x.shape
    R = min(R, S)

    def fn(xv, gv, tv):
        err = _rms(xv, gv) - tv
        return 0.5 * jnp.sum(jnp.mean(err * err, axis=-1, keepdims=True), axis=0, keepdims=True)

    def body(x_ref, g_ref, t_ref, loss_ref, dx_ref, dg_ref):
        i = pl.program_id(0)
        tv = t_ref[...]
        val, vjp = jax.vjp(lambda a, b: fn(a, b, tv), x_ref[...], g_ref[...])
        dx, dg = vjp(jnp.ones((1, 1), F32))
        dx_ref[...] = dx

        @pl.when(i == 0)
        def _():
            loss_ref[...] = jnp.zeros_like(loss_ref) + val
            dg_ref[...] = dg

        @pl.when(i != 0)
        def _():
            loss_ref[...] += val
            dg_ref[...] += dg

    return pl.pallas_call(
        body, name="loss_head", grid=(S // R,),
        in_specs=[pl.BlockSpec((R, D), lambda i: (i, 0)), pl.BlockSpec((1, D), lambda i: (0, 0)),
                  pl.BlockSpec((R, D), lambda i: (i, 0))],
        out_specs=[pl.BlockSpec((1, 128), lambda i: (0, 0)), pl.BlockSpec((R, D), lambda i: (i, 0)),
                   pl.BlockSpec((1, D), lambda i: (0, 0))],
        out_shape=[jax.ShapeDtypeStruct((1, 128), F32), jax.ShapeDtypeStruct((S, D), F32),
                   jax.ShapeDtypeStruct((1, D), F32)],
        compiler_params=_cp("arbitrary"),
    )(x, g, tgt)


@jax.custom_vjp
def _softplus(x):
    z = jnp.exp(-jnp.abs(x))
    u = 1.0 + z
    log1p = jnp.where(u == 1.0, z, jnp.log(u) * (z / jnp.where(u == 1.0, 1.0, u - 1.0)))
    return jnp.maximum(x, 0.0) + log1p


def _softplus_fwd(x):
    return _softplus(x), x


def _softplus_bwd(x, ct):
    return (ct * jax.nn.sigmoid(x),)


_softplus.defvjp(_softplus_fwd, _softplus_bwd)


def _dt_fn(raw, bias):
    return (_softplus(raw + bias),)


CONV_CB = 128
CONV_RB = 256
CONV_PAD = 8


def ssd_conv_fwd(u, conv_w, conv_b):
    S = u.shape[0]
    ncb = SSD_CONV_CH // CONV_CB
    col0 = SSD_DI // CONV_CB
    RB = min(CONV_RB, S)

    def body(x_ref, w_ref, b_ref, o_ref, pad):
        pad[0:CONV_PAD, :] = jnp.zeros((CONV_PAD, CONV_CB), F32)
        pad[S + CONV_PAD:S + 2 * CONV_PAD, :] = jnp.zeros((CONV_PAD, CONV_CB), F32)
        pad[CONV_PAD:S + CONV_PAD, :] = x_ref[...]
        w = w_ref[...]
        b = b_ref[...]
        for r in range(S // RB):
            acc = jnp.zeros((RB, CONV_CB), F32) + b
            for k in range(SSD_CONV):
                off = r * RB + CONV_PAD + k - SSD_CONV // 2
                acc = acc + pad[off:off + RB, :] * w[k:k + 1, :]
            o_ref[r * RB:(r + 1) * RB, :] = _silu(acc)

    return pl.pallas_call(
        body, name="ssd_conv_fwd", grid=(ncb,),
        in_specs=[pl.BlockSpec((S, CONV_CB), lambda j: (0, col0 + j)),
                  pl.BlockSpec((SSD_CONV, CONV_CB), lambda j: (0, j)),
                  pl.BlockSpec((1, CONV_CB), lambda j: (0, j))],
        out_specs=pl.BlockSpec((S, CONV_CB), lambda j: (0, j)),
        out_shape=jax.ShapeDtypeStruct((S, SSD_CONV_CH), F32),
        scratch_shapes=[pltpu.VMEM((S + 2 * CONV_PAD, CONV_CB), F32)],
        compiler_params=_cp("parallel"),
    )(u, conv_w, conv_b)


def ssd_conv_bwd(u, conv_w, conv_b, dact):
    S = u.shape[0]
    ncb = SSD_CONV_CH // CONV_CB
    col0 = SSD_DI // CONV_CB
    RB = min(CONV_RB, S)
    half = SSD_CONV // 2

    def body(x_ref, w_ref, b_ref, da_ref, dx_ref, dw_ref, db_ref, xpad, dpad):
        z8 = jnp.zeros((CONV_PAD, CONV_CB), F32)
        xpad[0:CONV_PAD, :] = z8
        xpad[S + CONV_PAD:S + 2 * CONV_PAD, :] = z8
        dpad[0:CONV_PAD, :] = z8
        dpad[S + CONV_PAD:S + 2 * CONV_PAD, :] = z8
        xpad[CONV_PAD:S + CONV_PAD, :] = x_ref[...]
        w = w_ref[...]
        b = b_ref[...]
        dws = [jnp.zeros((1, CONV_CB), F32) for _ in range(SSD_CONV)]
        db = jnp.zeros((1, CONV_CB), F32)
        for r in range(S // RB):
            acc = jnp.zeros((RB, CONV_CB), F32) + b
            xs = []
            for k in range(SSD_CONV):
                off = r * RB + CONV_PAD + k - half
                xk = xpad[off:off + RB, :]
                xs.append(xk)
                acc = acc + xk * w[k:k + 1, :]
            sg = jax.nn.sigmoid(acc)
            dc = da_ref[r * RB:(r + 1) * RB, :] * (sg * (1.0 + acc * (1.0 - sg)))
            dpad[r * RB + CONV_PAD:(r + 1) * RB + CONV_PAD, :] = dc
            db = db + jnp.sum(dc, axis=0, keepdims=True)
            for k in range(SSD_CONV):
                dws[k] = dws[k] + jnp.sum(xs[k] * dc, axis=0, keepdims=True)
        for r in range(S // RB):
            acc = jnp.zeros((RB, CONV_CB), F32)
            for k in range(SSD_CONV):
                off = r * RB + CONV_PAD + half - k
                acc = acc + dpad[off:off + RB, :] * w[k:k + 1, :]
            dx_ref[r * RB:(r + 1) * RB, :] = acc
        for k in range(SSD_CONV):
            dw_ref[k:k + 1, :] = dws[k]
        dw_ref[SSD_CONV:SSD_CONV + 1, :] = jnp.zeros((1, CONV_CB), F32)
        db_ref[...] = db

    return pl.pallas_call(
        body, name="ssd_conv_bwd", grid=(ncb,),
        in_specs=[pl.BlockSpec((S, CONV_CB), lambda j: (0, col0 + j)),
                  pl.BlockSpec((SSD_CONV, CONV_CB), lambda j: (0, j)),
                  pl.BlockSpec((1, CONV_CB), lambda j: (0, j)),
                  pl.BlockSpec((S, CONV_CB), lambda j: (0, j))],
        out_specs=[pl.BlockSpec((S, CONV_CB), lambda j: (0, j)),
                   pl.BlockSpec((SSD_CONV + 1, CONV_CB), lambda j: (0, j)),
                   pl.BlockSpec((1, CONV_CB), lambda j: (0, j))],
        out_shape=[jax.ShapeDtypeStruct((S, SSD_CONV_CH), F32),
                   jax.ShapeDtypeStruct((SSD_CONV + 1, SSD_CONV_CH), F32),
                   jax.ShapeDtypeStruct((1, SSD_CONV_CH), F32)],
        scratch_shapes=[pltpu.VMEM((S + 2 * CONV_PAD, CONV_CB), F32), pltpu.VMEM((S + 2 * CONV_PAD, CONV_CB), F32)],
        compiler_params=_cp("parallel"),
    )(u, conv_w, conv_b, dact)


def _ssd_group_layout(t):
    r = t.shape[0]
    g = t[:, :2 * SSD_HEADS].reshape(r, 2, SSD_GROUPS, SSD_HPG).transpose(2, 0, 1, 3).reshape(SSD_GROUPS, r, 2 * SSD_HPG)
    return jnp.pad(g, ((0, 0), (0, 0), (0, 128 - 2 * SSD_HPG)))


def _ssd_head_layout(t):
    r = t.shape[1]
    h = t[:, :, :2 * SSD_HPG].reshape(SSD_GROUPS, r, 2, SSD_HPG).transpose(1, 2, 0, 3).reshape(r, 2 * SSD_HEADS)
    return jnp.pad(h, ((0, 0), (0, 128 - 2 * SSD_HEADS)))


def _ssd_chunk(state, x, Bg, Cg, dt, alog, dtr, alr, *, reverse):
    Q, P = SSD_CHUNK, SSD_HEADDIM
    r = lax.broadcasted_iota(jnp.int32, (Q, Q), 0)
    c = lax.broadcasted_iota(jnp.int32, (Q, Q), 1)
    keep = (c >= r) if reverse else (c <= r)
    cum_t = jnp.transpose(_cumsum_rows(dt * (-jnp.exp(alog)), reverse))
    cum = _cumsum_rows(dtr * (-jnp.exp(alr)), reverse)
    last = 0 if reverse else Q - 1
    cum_l = cum[last:last + 1, :]
    CB = _dot(Cg, Bg, ((1,), (1,)))
    yoff = _dot(Cg, state, ((1,), (0,))) * jnp.exp(cum)
    xdt = x * dtr
    ys = []
    for h in range(SSD_HPG):
        col = h + (SSD_HPG if reverse else 0)
        hs = slice(h * P, (h + 1) * P)
        cum_q = jnp.concatenate([cum[:, hs]] * (Q // P), axis=1)
        L = jnp.where(keep, jnp.exp(jnp.where(keep, cum_q - cum_t[col:col + 1, :], 0.0)), 0.0)
        ys.append(_dot(CB * L, xdt[:, hs], ((1,), (0,))))
    new_state = jnp.exp(cum_l) * state + _dot(Bg, xdt * jnp.exp(cum_l - cum), ((0,), (0,)))
    return new_state, jnp.concatenate(ys, axis=1) + yoff


def ssd_scan_fwd(act, dt, alog, dtr, alr, *, reverse, y_prev=None, comm=None):
    S = act.shape[0]
    Q, N, P = SSD_CHUNK, SSD_STATE, SSD_HEADDIM
    nc = S // Q
    GW = SSD_HPG * P

    def cidx(i):
        return (nc - 1 - i) if reverse else i

    def body(*refs):
        if y_prev is None:
            x_ref, b_ref, c_ref, dt_ref, al_ref, dtr_ref, alr_ref, y_ref, st_ref, state = refs
            yp_ref = None
        else:
            x_ref, b_ref, c_ref, dt_ref, al_ref, dtr_ref, alr_ref, yp_ref, y_ref, st_ref, state = refs
        i = pl.program_id(1)

        @pl.when(i == 0)
        def _():
            state[...] = jnp.zeros_like(state)

        st = state[...]
        st_ref[0, 0] = st
        ns, y = _ssd_chunk(st, x_ref[...], b_ref[...], c_ref[...], dt_ref[0], al_ref[0], dtr_ref[...], alr_ref[...],
                           reverse=reverse)
        state[...] = ns
        y_ref[...] = y if yp_ref is None else y + yp_ref[...]

    xspec = pl.BlockSpec((Q, GW), lambda g, i: (cidx(i), g))
    in_specs = [xspec,
                pl.BlockSpec((Q, N), lambda g, i: (cidx(i), SSD_DI // N + g)),
                pl.BlockSpec((Q, N), lambda g, i: (cidx(i), SSD_DI // N + SSD_GROUPS + g)),
                pl.BlockSpec((1, Q, 128), lambda g, i: (g, cidx(i), 0)),
                pl.BlockSpec((1, 1, 128), lambda g, i: (g, 0, 0)),
                xspec, pl.BlockSpec((1, GW), lambda g, i: (0, g))]
    args = [act, act, act, dt, alog, dtr, alr]
    if y_prev is not None:
        in_specs.append(xspec)
        args.append(y_prev)
    return pcall(
        body, comm, name=f"ssd_scan_fwd_{int(reverse)}", grid=(SSD_GROUPS, nc), in_specs=in_specs,
        out_specs=[xspec, pl.BlockSpec((1, 1, N, GW), lambda g, i: (cidx(i), g, 0, 0))],
        out_shape=[jax.ShapeDtypeStruct((S, SSD_DI), F32), jax.ShapeDtypeStruct((nc, SSD_GROUPS, N, GW), F32)],
        scratch_shapes=[pltpu.VMEM((N, GW), F32)],
        compiler_params=_cp("arbitrary", "arbitrary"),
    )(*args)


def ssd_scan_bwd(act, dt, alog, dtr, alr, states, dy, prev_x, *, reverse, prev=None, comm=None):
    S = act.shape[0]
    Q, N, P = SSD_CHUNK, SSD_STATE, SSD_HEADDIM
    nc = S // Q
    GW = SSD_HPG * P

    def cidx(i):
        return i if reverse else (nc - 1 - i)

    def body(*refs):
        x_ref, b_ref, c_ref, dt_ref, al_ref, dtr_ref, alr_ref, st_ref, dy_ref, px_ref = refs[:10]
        pos = 10
        if prev is not None:
            pb_ref, pc_ref, pdt_ref, pal_ref = refs[pos:pos + 4]
            pos += 4
        dx_ref, db_ref, dc_ref, ddt_ref, dal_ref, ddtr_ref, dalr_ref, dstate = refs[pos:]
        i = pl.program_id(1)

        @pl.when(i == 0)
        def _():
            dstate[...] = jnp.zeros_like(dstate)

        _, vjp = jax.vjp(functools.partial(_ssd_chunk, reverse=reverse), st_ref[0, 0], x_ref[...], b_ref[...],
                         c_ref[...], dt_ref[0], al_ref[0], dtr_ref[...], alr_ref[...])
        dst, dx, dB, dC, ddt, dal, ddtr, dalr = vjp((dstate[...], dy_ref[...]))
        dstate[...] = dst
        dx_ref[...] = dx + px_ref[...]
        if prev is not None:
            dB = dB + pb_ref[...]
            dC = dC + pc_ref[...]
            ddt = ddt + pdt_ref[0]
        db_ref[...] = dB
        dc_ref[...] = dC
        ddt_ref[0] = ddt
        ddtr_ref[...] = ddtr

        @pl.when(i == 0)
        def _():
            dal_ref[0] = dal + (pal_ref[0] if prev is not None else 0.0)
            dalr_ref[...] = dalr

        @pl.when(i != 0)
        def _():
            dal_ref[0] += dal
            dalr_ref[...] += dalr

    xspec = pl.BlockSpec((Q, GW), lambda g, i: (cidx(i), g))
    gspec = pl.BlockSpec((Q, N), lambda g, i: (cidx(i), g))
    dtspec = pl.BlockSpec((1, Q, 128), lambda g, i: (g, cidx(i), 0))
    alspec = pl.BlockSpec((1, 1, 128), lambda g, i: (g, 0, 0))
    alrspec = pl.BlockSpec((1, GW), lambda g, i: (0, g))
    in_specs = [xspec,
                pl.BlockSpec((Q, N), lambda g, i: (cidx(i), SSD_DI // N + g)),
                pl.BlockSpec((Q, N), lambda g, i: (cidx(i), SSD_DI // N + SSD_GROUPS + g)),
                dtspec, alspec, xspec, alrspec,
                pl.BlockSpec((1, 1, N, GW), lambda g, i: (cidx(i), g, 0, 0)), xspec, xspec]
    args = [act, act, act, dt, alog, dtr, alr, states, dy, prev_x]
    if prev is not None:
        in_specs += [gspec, gspec, dtspec, alspec]
        args += list(prev)
    return pcall(
        body, comm, name=f"ssd_scan_bwd_{int(reverse)}", grid=(SSD_GROUPS, nc), in_specs=in_specs,
        out_specs=[xspec, gspec, gspec, dtspec, alspec, xspec, alrspec],
        out_shape=[jax.ShapeDtypeStruct((S, SSD_DI), F32), jax.ShapeDtypeStruct((S, SSD_GROUPS * N), F32),
                   jax.ShapeDtypeStruct((S, SSD_GROUPS * N), F32),
                   jax.ShapeDtypeStruct((SSD_GROUPS, S, 128), F32), jax.ShapeDtypeStruct((SSD_GROUPS, 1, 128), F32),
                   jax.ShapeDtypeStruct((S, SSD_DI), F32), jax.ShapeDtypeStruct((1, SSD_DI), F32)],
        scratch_shapes=[pltpu.VMEM((N, GW), F32)],
        compiler_params=_cp("arbitrary", "arbitrary"),
    )(*args)


def _ssd_post_fn(y, xs, z, dexp, ng):
    t = (y + xs * dexp) * _silu(z)
    return (_rms(t, ng),)


def _cumsum_rows_impl(x, reverse):
    n = x.shape[0]
    row = lax.broadcasted_iota(jnp.int32, x.shape, 0)
    k = 1
    while k < n:
        if reverse:
            x = x + jnp.where(row < n - k, pltpu.roll(x, n - k, 0), 0.0)
        else:
            x = x + jnp.where(row >= k, pltpu.roll(x, k, 0), 0.0)
        k *= 2
    return x


@functools.partial(jax.custom_vjp, nondiff_argnums=(1,))
def _cumsum_rows(x, reverse):
    return _cumsum_rows_impl(x, reverse)


_cumsum_rows.defvjp(lambda x, reverse: (_cumsum_rows_impl(x, reverse), None),
                    lambda reverse, _, ct: (_cumsum_rows_impl(ct, not reverse),))


def _hg_chunk(state, qraw, fraw, v, lb, *, reverse):
    C = HG_CHUNK
    r = lax.broadcasted_iota(jnp.int32, (C, C), 0)
    c = lax.broadcasted_iota(jnp.int32, (C, C), 1)
    keep = (c >= r) if reverse else (c <= r)
    q = _silu(qraw)
    f = lb + (1.0 - lb) * jax.nn.sigmoid(fraw)
    k = 1.0 - f
    g = jnp.log(f)
    G = _cumsum_rows(g, reverse)
    ref_row = C // 2 - 1 if reverse else C // 2
    last_row = 0 if reverse else C - 1
    Gr = G[ref_row:ref_row + 1, :]
    Gl = G[last_row:last_row + 1, :]
    q_t = q * jnp.exp(G - Gr)
    k_t = k * jnp.exp(Gr - G)
    att = jnp.where(keep, _dot(q_t, k_t, ((1,), (1,))), 0.0)
    o = _dot(att, v, ((1,), (0,))) + _dot(q * jnp.exp(G), state, ((1,), (0,)))
    kd = k * jnp.exp(Gl - G)
    new_state = jnp.transpose(jnp.exp(Gl)) * state + _dot(kd, v, ((0,), (0,)))
    return new_state, o


def hg_scan_fwd(u, lb, *, reverse, o_prev=None, rows=256, comm=None):
    S = u.shape[0]
    nh = HG_HEADS
    rows = min(rows, S)
    nsteps = S // rows
    ncb = rows // HG_CHUNK
    f_sec = 2 if reverse else 1

    def blk(i):
        return (nsteps - 1 - i) if reverse else i

    def body(*refs):
        if o_prev is None:
            q_ref, f_ref, v_ref, lb_ref, o_ref, st_ref, state = refs
            op_ref = None
        else:
            q_ref, f_ref, v_ref, lb_ref, op_ref, o_ref, st_ref, state = refs
        i = pl.program_id(0)

        @pl.when(i == 0)
        def _():
            state[...] = jnp.zeros_like(state)

        def chunk(cc, carry):
            ci = (ncb - 1 - cc) if reverse else cc
            sl = pl.ds(pl.multiple_of(ci * HG_CHUNK, HG_CHUNK), HG_CHUNK)
            for h in range(nh):
                hs = slice(h * HG_D, (h + 1) * HG_D)
                st = state[h]
                st_ref[ci, h] = st
                ns, o = _hg_chunk(st, q_ref[sl, hs], f_ref[sl, hs], v_ref[sl, hs], lb_ref[:, hs], reverse=reverse)
                state[h] = ns
                if op_ref is not None:
                    o = o + op_ref[sl, hs]
                o_ref[sl, hs] = o
            return carry

        lax.fori_loop(0, ncb, chunk, 0)

    rowspec = lambda sec: pl.BlockSpec((rows, HG_W), lambda i: (blk(i), sec))
    in_specs = [rowspec(0), rowspec(f_sec), rowspec(3), pl.BlockSpec((1, HG_W), lambda i: (0, 0))]
    args = [u, u, u, lb]
    if o_prev is not None:
        in_specs.append(rowspec(0))
        args.append(o_prev)
    return pcall(
        body, comm, name=f"hg_scan_fwd_{int(reverse)}", grid=(nsteps,), in_specs=in_specs,
        out_specs=[rowspec(0), pl.BlockSpec((ncb, nh, HG_D, HG_D), lambda i: (blk(i), 0, 0, 0))],
        out_shape=[jax.ShapeDtypeStruct((S, HG_W), F32), jax.ShapeDtypeStruct((S // HG_CHUNK, nh, HG_D, HG_D), F32)],
        scratch_shapes=[pltpu.VMEM((nh, HG_D, HG_D), F32)],
        compiler_params=_cp("arbitrary"),
    )(*args)


def hg_scan_bwd(u, lb, states, do, *, reverse, prev=None, rows=256, comm=None):
    S = u.shape[0]
    nh = HG_HEADS
    rows = min(rows, S)
    nsteps = S // rows
    ncb = rows // HG_CHUNK
    f_sec = 2 if reverse else 1

    def blk(i):
        return i if reverse else (nsteps - 1 - i)

    def body(*refs):
        q_ref, f_ref, v_ref, lb_ref, st_ref, do_ref = refs[:6]
        pos = 6
        if prev is not None:
            pq_ref, pv_ref, plb_ref = refs[pos:pos + 3]
            pos += 3
        dq_ref, df_ref, dv_ref, dlb_ref, dstate = refs[pos:]
        i = pl.program_id(0)

        @pl.when(i == 0)
        def _():
            dstate[...] = jnp.zeros_like(dstate)
            dlb_ref[...] = plb_ref[...] if prev is not None else jnp.zeros_like(dlb_ref)

        def chunk(cc, carry):
            ci = cc if reverse else (ncb - 1 - cc)
            sl = pl.ds(pl.multiple_of(ci * HG_CHUNK, HG_CHUNK), HG_CHUNK)
            for h in range(nh):
                hs = slice(h * HG_D, (h + 1) * HG_D)
                _, vjp = jax.vjp(functools.partial(_hg_chunk, reverse=reverse), st_ref[ci, h],
                                 q_ref[sl, hs], f_ref[sl, hs], v_ref[sl, hs], lb_ref[:, hs])
                dst, dq, df, dv, dlb = vjp((dstate[h], do_ref[sl, hs]))
                dstate[h] = dst
                if prev is not None:
                    dq = dq + pq_ref[sl, hs]
                    dv = dv + pv_ref[sl, hs]
                dq_ref[sl, hs] = dq
                df_ref[sl, hs] = df
                dv_ref[sl, hs] = dv
                dlb_ref[:, hs] += dlb
            return carry

        lax.fori_loop(0, ncb, chunk, 0)

    rowspec = lambda sec: pl.BlockSpec((rows, HG_W), lambda i: (blk(i), sec))
    lbspec = pl.BlockSpec((1, HG_W), lambda i: (0, 0))
    in_specs = [rowspec(0), rowspec(f_sec), rowspec(3), lbspec,
                pl.BlockSpec((ncb, nh, HG_D, HG_D), lambda i: (blk(i), 0, 0, 0)), rowspec(0)]
    args = [u, u, u, lb, states, do]
    if prev is not None:
        in_specs += [rowspec(0), rowspec(0), lbspec]
        args += list(prev)
    return pcall(
        body, comm, name=f"hg_scan_bwd_{int(reverse)}", grid=(nsteps,), in_specs=in_specs,
        out_specs=[rowspec(0), rowspec(0), rowspec(0), lbspec],
        out_shape=[jax.ShapeDtypeStruct((S, HG_W), F32)] * 3 + [jax.ShapeDtypeStruct((1, HG_W), F32)],
        scratch_shapes=[pltpu.VMEM((nh, HG_D, HG_D), F32)],
        compiler_params=_cp("arbitrary"),
    )(*args)


def _hg_lb_fn(lbp):
    m = jnp.max(lbp, axis=0, keepdims=True)
    e = jnp.exp(lbp - m)
    sm = e / jnp.sum(e, axis=0, keepdims=True)
    return ((sm[0:1] + sm[1:2]) - sm[0:1],)


def hg_lb_fwd(lbp):
    def body(x_ref, o_ref):
        o_ref[...] = _hg_lb_fn(x_ref[...])[0]

    return pl.pallas_call(body, name="hg_lb_fwd", out_shape=jax.ShapeDtypeStruct((1, HG_W), F32))(lbp)


def hg_lb_bwd(lbp, dlb):
    def body(x_ref, d_ref, o_ref):
        _, vjp = jax.vjp(_hg_lb_fn, x_ref[...])
        o_ref[...] = vjp((d_ref[...],))[0]

    return pl.pallas_call(body, name="hg_lb_bwd", out_shape=jax.ShapeDtypeStruct(lbp.shape, F32))(lbp, dlb)


def _hg_post_fn(o, gate, ng):
    return (_rms(o, ng) * _silu(gate),)


def _gate_fn(o, gate):
    return (o * _silu(gate),)


def _rope_tables(S):
    t = np.arange(S)
    row = (t // GRID_W).astype(np.float32)
    col = (t % GRID_W).astype(np.float32)
    half = AT_HD // 4
    inv = (ROPE_THETA ** (-np.arange(0, 2 * half, 2, dtype=np.float32) / np.float32(2 * half))).astype(np.float32)
    ar = row[:, None] * inv[None, :]
    ac = col[:, None] * inv[None, :]
    return ar.astype(np.float32), ac.astype(np.float32)


@jax.custom_vjp
def _half_swap(x):
    ax = x.ndim - 1
    lane = lax.broadcasted_iota(jnp.int32, x.shape, ax)
    return jnp.where((lane & 32) == 0, pltpu.roll(x, 96, ax), pltpu.roll(x, 32, ax))


_half_swap.defvjp(lambda x: (_half_swap(x), None), lambda _, ct: (_half_swap(ct),))


def _make_qk_fn(scale):
    def fn(x, ct, st, g):
        n = _rms(x, g)
        return ((n * ct + _half_swap(n) * st) * scale,)
    return fn


def flash_fwd(q, k, v, *, v_col0=0, tq=256, comm=None):
    S = q.shape[0]
    tq = min(tq, S)
    G = AT_HEADS // AT_KV

    def body(q_ref, k_ref, v_ref, o_ref, lse_ref):
        kv, vv = k_ref[...], v_ref[...]
        for g in range(G):
            sl = slice(g * AT_HD, (g + 1) * AT_HD)
            s = _dot(q_ref[:, sl], kv, ((1,), (1,)))
            m = jnp.max(s, axis=1, keepdims=True)
            p = jnp.exp(s - m)
            l = jnp.sum(p, axis=1, keepdims=True)
            o_ref[:, sl] = _dot(p, vv, ((1,), (0,))) / l
            lse_ref[0, :, g:g + 1] = m + jnp.log(l)

    return pcall(
        body, comm, name="flash_fwd", grid=(AT_KV, S // tq),
        in_specs=[pl.BlockSpec((tq, G * AT_HD), lambda h, i: (i, h)),
                  pl.BlockSpec((S, AT_HD), lambda h, i: (0, h)),
                  pl.BlockSpec((S, AT_HD), lambda h, i: (0, v_col0 + h))],
        out_specs=[pl.BlockSpec((tq, G * AT_HD), lambda h, i: (i, h)),
                   pl.BlockSpec((1, tq, G), lambda h, i: (h, i, 0))],
        out_shape=[jax.ShapeDtypeStruct((S, AT_QW), F32), jax.ShapeDtypeStruct((AT_KV, S, G), F32)],
        compiler_params=_cp("parallel", "arbitrary"),
    )(q, k, v)


def flash_bwd_dq(q, k, v, o, lse, do, *, v_col0=0, tq=256):
    S = q.shape[0]
    tq = min(tq, S)
    G = AT_HEADS // AT_KV

    def body(q_ref, k_ref, v_ref, o_ref, lse_ref, do_ref, dq_ref, dl_ref):
        kv, vv = k_ref[...], v_ref[...]
        for g in range(G):
            sl = slice(g * AT_HD, (g + 1) * AT_HD)
            dog = do_ref[:, sl]
            delta = jnp.sum(dog * o_ref[:, sl], axis=1, keepdims=True)
            s = _dot(q_ref[:, sl], kv, ((1,), (1,)))
            p = jnp.exp(s - lse_ref[0, :, g:g + 1])
            dp = _dot(dog, vv, ((1,), (1,)))
            ds = p * (dp - delta)
            dq_ref[:, sl] = _dot(ds, kv, ((1,), (0,)))
            dl_ref[0, :, g:g + 1] = delta

    qspec = pl.BlockSpec((tq, G * AT_HD), lambda h, i: (i, h))
    lspec = pl.BlockSpec((1, tq, G), lambda h, i: (h, i, 0))
    return pl.pallas_call(
        body, name="flash_bwd_dq", grid=(AT_KV, S // tq),
        in_specs=[qspec, pl.BlockSpec((S, AT_HD), lambda h, i: (0, h)),
                  pl.BlockSpec((S, AT_HD), lambda h, i: (0, v_col0 + h)), qspec, lspec, qspec],
        out_specs=[qspec, lspec],
        out_shape=[jax.ShapeDtypeStruct((S, AT_QW), F32), jax.ShapeDtypeStruct((AT_KV, S, G), F32)],
        compiler_params=_cp("parallel", "arbitrary"),
    )(q, k, v, o, lse, do)


def flash_bwd_dkv(q, k, v, lse, delta, do, *, v_col0=0, tk=512, comm=None):
    S = q.shape[0]
    tk = min(tk, S)
    G = AT_HEADS // AT_KV

    def body(q_ref, k_ref, v_ref, lse_ref, dl_ref, do_ref, dk_ref, dv_ref):
        kv, vv = k_ref[...], v_ref[...]
        dk = jnp.zeros((tk, AT_HD), F32)
        dv = jnp.zeros((tk, AT_HD), F32)
        for g in range(G):
            sl = slice(g * AT_HD, (g + 1) * AT_HD)
            qg, dog = q_ref[:, sl], do_ref[:, sl]
            s = _dot(qg, kv, ((1,), (1,)))
            p = jnp.exp(s - lse_ref[0, :, g:g + 1])
            dv = dv + _dot(p, dog, ((0,), (0,)))
            dp = _dot(dog, vv, ((1,), (1,)))
            ds = p * (dp - dl_ref[0, :, g:g + 1])
            dk = dk + _dot(ds, qg, ((0,), (0,)))
        dk_ref[...] = dk
        dv_ref[...] = dv

    qspec = pl.BlockSpec((S, G * AT_HD), lambda h, j: (0, h))
    kspec = pl.BlockSpec((tk, AT_HD), lambda h, j: (j, h))
    lspec = pl.BlockSpec((1, S, G), lambda h, j: (h, 0, 0))
    return pcall(
        body, comm, name="flash_bwd_dkv", grid=(AT_KV, S // tk),
        in_specs=[qspec, kspec, pl.BlockSpec((tk, AT_HD), lambda h, j: (j, v_col0 + h)), lspec, lspec, qspec],
        out_specs=[kspec, kspec],
        out_shape=[jax.ShapeDtypeStruct((S, AT_KW), F32), jax.ShapeDtypeStruct((S, AT_KW), F32)],
        compiler_params=_cp("parallel", "arbitrary"),
    )(q, k, v, lse, delta, do)


def _t5_bucket_np(rel):
    half = REL_BUCKETS // 2
    exact = half // 2
    n = np.abs(rel)
    large = exact + (np.log(np.maximum(n, 1).astype(np.float32) / np.float32(exact))
                     / np.float32(math.log(REL_MAX_DIST / exact)) * np.float32(half - exact)).astype(np.int32)
    large = np.minimum(large, half - 1)
    return np.where(rel > 0, half, 0) + np.where(n < exact, n, large)


DL_TB = 256


def _dl_tiles(Ls, T=128):
    T = min(T, Ls)
    return T, T + 2 * DL_HALF


def _dl_bucket_tables(dil, T):
    W = T + 2 * DL_HALF
    i = np.arange(T)[:, None]
    j = np.arange(W)[None, :]
    bq = _t5_bucket_np((j - DL_HALF - i) * dil)
    iw = np.arange(W)[:, None]
    jk = np.arange(T)[None, :]
    bk = _t5_bucket_np((jk + DL_HALF - iw) * dil)
    return bq.astype(np.int32), bk.astype(np.int32)


def _dl_merge_fn(o0, o1, o2, l0, l1, l2):
    m = jnp.maximum(jnp.maximum(l0, l1), l2)
    e0, e1, e2 = jnp.exp(l0 - m), jnp.exp(l1 - m), jnp.exp(l2 - m)
    den = e0 + e1 + e2
    return ((e0 / den) * o0 + (e1 / den) * o1 + (e2 / den) * o2,)


def _adamw_math(w, g, m, v):
    m = ADAM_B1 * m + (1.0 - ADAM_B1) * g
    v = ADAM_B2 * v + (1.0 - ADAM_B2) * (g * g)
    m_hat = m / (1.0 - ADAM_B1 ** ADAM_STEP)
    v_hat = v / (1.0 - ADAM_B2 ** ADAM_STEP)
    delta = -ADAM_LR * (m_hat / (jnp.sqrt(v_hat) + ADAM_EPS) + ADAM_WD * w)
    return delta, m, v


def adamw_sum(parts, w, m, v, *, name, R=128):
    rows, cols = w.shape
    R = min(R, rows)
    if rows % R:
        R = rows

    def body(p_ref, w_ref, m_ref, v_ref, g_ref, d_ref, nm_ref, nv_ref):
        g = p_ref[0].astype(F32)
        for s in range(1, N_DEV):
            g = g + p_ref[s].astype(F32)
        d, nm, nv = _adamw_math(w_ref[...], g, m_ref[...], v_ref[...])
        g_ref[...] = g
        d_ref[...] = d
        nm_ref[...] = nm
        nv_ref[...] = nv

    spec = pl.BlockSpec((R, cols), lambda i: (i, 0))
    return pl.pallas_call(
        body, name=name, grid=(rows // R,),
        in_specs=[pl.BlockSpec((N_DEV, R, cols), lambda i: (0, i, 0)), spec, spec, spec],
        out_specs=[spec] * 4, out_shape=[jax.ShapeDtypeStruct((rows, cols), F32)] * 4,
        compiler_params=_cp("parallel"),
    )(parts, w, m, v)


def sum_parts(parts, *, name):
    rows, cols = parts.shape[1:]

    def body(p_ref, o_ref):
        g = p_ref[0]
        for s in range(1, N_DEV):
            g = g + p_ref[s]
        o_ref[...] = g

    return pl.pallas_call(body, name=name, out_shape=jax.ShapeDtypeStruct((rows, cols), F32))(parts)


def adamw_plain(w, g, m, v, *, name):
    def body(w_ref, g_ref, m_ref, v_ref, d_ref, nm_ref, nv_ref):
        d, nm, nv = _adamw_math(w_ref[...], g_ref[...], m_ref[...], v_ref[...])
        d_ref[...] = d
        nm_ref[...] = nm
        nv_ref[...] = nv

    return pl.pallas_call(body, name=name, out_shape=[jax.ShapeDtypeStruct(w.shape, F32)] * 3)(w, g, m, v)


def allgather_two_level(x, *, name):
    R, C = x.shape

    def body(x_ref, out_ref, send_sems, recv_sems, local_sem):
        x_, y_, c_ = _my_pos()
        me, sibling = (x_, y_, c_), (x_, y_, 1 - c_)
        chips = [(1 - x_, y_), (x_, 1 - y_), (1 - x_, 1 - y_)]

        def rows(p):
            return out_ref.at[_flat(*p)]

        def copy(k, block, to, src=None):
            return pltpu.make_async_remote_copy(
                src_ref=rows(block) if src is None else src, dst_ref=rows(block),
                send_sem=send_sems.at[k], recv_sem=recv_sems.at[k], device_id=to, device_id_type=MESH_ID)

        mine = pltpu.make_async_copy(x_ref, rows(me), local_sem)
        mine.start()
        first = [copy(0, me, sibling, src=x_ref)]
        first += [copy(1 + j, me, (*chip, c_), src=x_ref) for j, chip in enumerate(chips)]
        for cp in first:
            cp.start()
        passed = [copy(4 + j, (*chip, c_), sibling) for j, chip in enumerate(chips)]
        for j, chip in enumerate(chips):
            copy(1 + j, (*chip, c_), me).wait_recv()
            passed[j].start()
        copy(0, sibling, me).wait_recv()
        for j, chip in enumerate(chips):
            copy(4 + j, (*chip, 1 - c_), me).wait_recv()
        for cp in first + passed:
            cp.wait_send()
        mine.wait()

    return pl.pallas_call(
        body, name=name,
        out_shape=jax.ShapeDtypeStruct((N_DEV, R, C), x.dtype),
        in_specs=[pl.BlockSpec(memory_space=pl.ANY)],
        out_specs=pl.BlockSpec(memory_space=pl.ANY),
        scratch_shapes=[pltpu.SemaphoreType.DMA((7,)), pltpu.SemaphoreType.DMA((7,)), pltpu.SemaphoreType.DMA],
    )(x)


def allgather_two_level_multi(xs, *, name):
    nb = len(xs)

    def body(*refs):
        x_refs, out_refs = refs[:nb], refs[nb:2 * nb]
        send_sems, recv_sems, local_sems = refs[2 * nb:]
        x_, y_, c_ = _my_pos()
        me, sibling = (x_, y_, c_), (x_, y_, 1 - c_)
        chips = [(1 - x_, y_), (x_, 1 - y_), (1 - x_, 1 - y_)]

        def copy(b, k, block, to, own=False):
            rows = out_refs[b].at[_flat(*block)]
            return pltpu.make_async_remote_copy(
                src_ref=x_refs[b] if own else rows, dst_ref=rows,
                send_sem=send_sems.at[b, k], recv_sem=recv_sems.at[b, k], device_id=to, device_id_type=MESH_ID)

        mine = [pltpu.make_async_copy(x_refs[b], out_refs[b].at[_flat(*me)], local_sems.at[b]) for b in range(nb)]
        first = []
        for b in range(nb):
            first.append(copy(b, 0, me, sibling, own=True))
            first += [copy(b, 1 + j, me, (*chip, c_), own=True) for j, chip in enumerate(chips)]
        for cp in mine + first:
            cp.start()
        passed = []
        for j, chip in enumerate(chips):
            for b in range(nb):
                copy(b, 1 + j, (*chip, c_), me).wait_recv()
                fwd = copy(b, 4 + j, (*chip, c_), sibling)
                fwd.start()
                passed.append(fwd)
        for b in range(nb):
            copy(b, 0, sibling, me).wait_recv()
            for j, chip in enumerate(chips):
                copy(b, 4 + j, (*chip, 1 - c_), me).wait_recv()
        for cp in first + passed:
            cp.wait_send()
        for cp in mine:
            cp.wait()

    anyspec = pl.BlockSpec(memory_space=pl.ANY)
    return pl.pallas_call(
        body, name=name,
        out_shape=[jax.ShapeDtypeStruct((N_DEV,) + x.shape, x.dtype) for x in xs],
        in_specs=[anyspec] * nb, out_specs=[anyspec] * nb,
        scratch_shapes=[pltpu.SemaphoreType.DMA((nb, 7)), pltpu.SemaphoreType.DMA((nb, 7)), pltpu.SemaphoreType.DMA((nb,))],
    )(*xs)


def _prenorm(tag, x, ng):
    return rowwise_fwd(f"{tag}_prenorm", _prenorm_fn, [(x, 0, False)], [(ng, False)], [(D_MODEL, MXU_DTYPE)], W=D_MODEL)[0]


def _cat_mxu(parts):
    return jnp.concatenate([t.astype(MXU_DTYPE) for t in parts], axis=1)


def _in_out_bwd(tag, x, ng, hn, du, w_in, dx, tail_comm=None):
    dw_in = matmul(hn, du, ta=True, out_dtype=GRAD_WIRE_DTYPE, name=f"{tag}_dw_in")
    comm = None if tail_comm is None else tail_comm(dw_in)
    dhn, cres = _own(matmul(du, w_in, tb=True, name=f"{tag}_dhn", comm=comm), comm)
    dx_prev, dng = rowwise_bwd(f"{tag}_prenorm_bwd", _prenorm_fn, [(x, 0, False)], [(ng, False)], [dhn],
                               W=D_MODEL, diff_rows=[0], diff_shared=[0], add=dx)
    return (dx_prev, dng, dw_in) if tail_comm is None else (dx_prev, dng, dw_in, cres)


def _own(res, comm):
    return (res, None) if comm is None else res


def ssd_layer_fwd(x, ng, p, comm=None, comm1=None):
    hn = _prenorm("ssd", x, ng)
    u = matmul(hn, p["w_in"], name="ssd_in")
    act = ssd_conv_fwd(u, p["conv_w"], p["conv_b"])
    dt = rowwise_fwd("ssd_dt", _dt_fn, [(u, (SSD_DI + SSD_CONV_CH) // 128, False)], [(p["dt_bias"], False)],
                     [(128, F32)], W=128)[0]
    H = SSD_HEADS
    dtr = [jnp.repeat(dt[:, d * H:(d + 1) * H], SSD_HEADDIM, axis=1) for d in (0, 1)]
    alr = [jnp.repeat(p["alog"][:, d * H:(d + 1) * H], SSD_HEADDIM, axis=1) for d in (0, 1)]
    dt, alog = _ssd_group_layout(dt), _ssd_group_layout(p["alog"])
    (y0, st0), cres = _own(ssd_scan_fwd(act, dt, alog, dtr[0], alr[0], reverse=False, comm=comm), comm)
    (y, st1), cres1 = _own(ssd_scan_fwd(act, dt, alog, dtr[1], alr[1], reverse=True, y_prev=y0, comm=comm1), comm1)
    g = rowwise_fwd("ssd_post", _ssd_post_fn, [(y, 0, True), (act, 0, True), (u, 0, True)],
                    [(p["dexp"], True), (p["norm_g"], True)], [(512, MXU_DTYPE)], W=512, ncb=SSD_GROUPS)[0]
    xn = matmul(g, p["w_out"], residual=x, name="ssd_out")
    return xn, dict(x=x, ng=ng, hn=hn, u=u, act=act, dt=dt, alog=alog, dtr=dtr, alr=alr, y=y, st0=st0, st1=st1,
                    g=g), cres, cres1


def ssd_layer_bwd(sv, p, dx, comm=None, tail_comm=None):
    u, act, dt = sv["u"], sv["act"], sv["dt"]
    S = u.shape[0]
    dg = matmul(dx, p["w_out"], tb=True, name="ssd_dg")
    dw_out = matmul(sv["g"], dx, ta=True, out_dtype=GRAD_WIRE_DTYPE, name="ssd_dw_out")
    dy, dxs_skip, dz, ddexp, dnorm = rowwise_bwd(
        "ssd_post_bwd", _ssd_post_fn, [(sv["y"], 0, True), (act, 0, True), (u, 0, True)],
        [(p["dexp"], True), (p["norm_g"], True)], [dg], W=512, ncb=SSD_GROUPS, diff_rows=[0, 1, 2], diff_shared=[0, 1])
    dtr, alr = sv["dtr"], sv["alr"]
    (dxa, dB, dC, ddt, dal, ddtr0, dalr0), cres = _own(
        ssd_scan_bwd(act, dt, sv["alog"], dtr[0], alr[0], sv["st0"], dy, dxs_skip, reverse=False, comm=comm), comm)
    comm1 = None if tail_comm is None else tail_comm(None, dw_out)
    (dxa, dB, dC, ddt, dal, ddtr1, dalr1), tail_out = _own(
        ssd_scan_bwd(act, dt, sv["alog"], dtr[1], alr[1], sv["st1"], dy, dxa, reverse=True, prev=(dB, dC, ddt, dal),
                     comm=comm1), comm1)
    dact = jnp.concatenate([dxa, dB, dC], axis=1)
    dxbc, dconv_w, dconv_b = ssd_conv_bwd(u, p["conv_w"], p["conv_b"], dact)
    fold = jnp.asarray(np.repeat(np.eye(SSD_HEADS, dtype=np.float32), SSD_HEADDIM, axis=0))
    folded = [matmul(t, fold, exact=True, name=f"ssd_ddt_fold_{d}", tn=SSD_HEADS) for d, t in enumerate((ddtr0, ddtr1))]
    ddt_all = _ssd_head_layout(ddt) + jnp.pad(jnp.concatenate(folded, axis=1), ((0, 0), (0, 128 - 2 * SSD_HEADS)))
    dal_rep = jnp.concatenate([t.reshape(SSD_HEADS, SSD_HEADDIM).sum(axis=1) for t in (dalr0, dalr1)])[None, :]
    ddt_raw, ddt_bias = rowwise_bwd("ssd_dt_bwd", _dt_fn, [(u, (SSD_DI + SSD_CONV_CH) // 128, False)],
                                    [(p["dt_bias"], False)], [ddt_all], W=128, diff_rows=[0], diff_shared=[0])
    du = _cat_mxu([dz, dxbc, ddt_raw, jnp.zeros((S, SSD_IN_PAD - SSD_IN - 64), F32)])
    res = _in_out_bwd("ssd", sv["x"], sv["ng"], sv["hn"], du, p["w_in"], dx,
                      tail_comm=None if tail_comm is None else (lambda g_in: tail_comm(g_in[:, :SSD_IN], None)))
    dx_prev, dng, dw_in = res[:3]
    tail = res[3] + tail_out if tail_comm is not None else None
    grads = dict(
        w_in=dw_in[:, :SSD_IN], w_out=dw_out, conv_w=dconv_w[:SSD_CONV], conv_b=dconv_b,
        dt_bias=ddt_bias[:, :2 * SSD_HEADS], a_log=_ssd_head_layout(dal)[:, :2 * SSD_HEADS] + dal_rep,
        d=ddexp.reshape(SSD_HEADS, SSD_HEADDIM).sum(axis=1)[None, :], norm_g=dnorm, ng=dng)
    return dx_prev, grads, cres, tail


def hg_layer_fwd(x, ng, p, comm0=None, comm1=None):
    hn = _prenorm("hg", x, ng)
    u = matmul(hn, p["w_in"], name="hg_in")
    lb = hg_lb_fwd(p["hgrn_lb"])
    (o0, st0), cres0 = _own(hg_scan_fwd(u, lb, reverse=False, comm=comm0), comm0)
    (o, st1), cres1 = _own(hg_scan_fwd(u, lb, reverse=True, o_prev=o0, comm=comm1), comm1)
    g = rowwise_fwd("hg_post", _hg_post_fn, [(o, 0, True), (u, 4 * HG_HEADS, True)], [(p["norm_g"], True)],
                    [(HG_D, MXU_DTYPE)], W=HG_D, ncb=HG_HEADS)[0]
    xn = matmul(g, p["w_out"], residual=x, name="hg_out")
    return xn, dict(x=x, ng=ng, hn=hn, u=u, lb=lb, o=o, st0=st0, st1=st1, g=g), cres0, cres1


def hg_layer_bwd(sv, p, dx, comm=None):
    u, lb = sv["u"], sv["lb"]
    dg = matmul(dx, p["w_out"], tb=True, name="hg_dg")
    dw_out = matmul(sv["g"], dx, ta=True, out_dtype=GRAD_WIRE_DTYPE, name="hg_dw_out")
    do, dgate, dnorm = rowwise_bwd("hg_post_bwd", _hg_post_fn, [(sv["o"], 0, True), (u, 4 * HG_HEADS, True)],
                                   [(p["norm_g"], True)], [dg], W=HG_D, ncb=HG_HEADS, diff_rows=[0, 1], diff_shared=[0])
    (dq0, df0, dv0, dlb0), cres = _own(hg_scan_bwd(u, lb, sv["st0"], do, reverse=False, comm=comm), comm)
    dq, df1, dv, dlb = hg_scan_bwd(u, lb, sv["st1"], do, reverse=True, prev=(dq0, dv0, dlb0))
    du = _cat_mxu([dq, df0, df1, dv, dgate])
    dhgrn_lb = hg_lb_bwd(p["hgrn_lb"], dlb)
    dx_prev, dng, dw_in = _in_out_bwd("hg", sv["x"], sv["ng"], sv["hn"], du, p["w_in"], dx)
    return dx_prev, dict(w_in=dw_in, w_out=dw_out, norm_g=dnorm, hgrn_lb=dhgrn_lb, ng=dng), cres


def _rope_consts(S):
    ar, ac = _rope_tables(S)
    ct = np.concatenate([np.cos(ar), np.cos(ar), np.cos(ac), np.cos(ac)], axis=1).astype(np.float32)
    st = np.concatenate([-np.sin(ar), np.sin(ar), -np.sin(ac), np.sin(ac)], axis=1).astype(np.float32)
    return jnp.asarray(ct), jnp.asarray(st)


def _at_qk(tag, u, col0, nheads, scale, gain, consts, cot=None):
    ct, st = consts
    rows = [(u, col0, True), (ct, 0, False), (st, 0, False)]
    shared = [(gain, False)]
    if cot is None:
        return rowwise_fwd(f"at_{tag}", _make_qk_fn(scale), rows, shared, [(AT_HD, MXU_DTYPE)], W=AT_HD, ncb=nheads)[0]
    return rowwise_bwd(f"at_{tag}_bwd", _make_qk_fn(scale), rows, shared, [cot], W=AT_HD, ncb=nheads,
                       diff_rows=[0], diff_shared=[0])


def at_layer_fwd(x, ng, p, comm=None):
    S = x.shape[0]
    hn = _prenorm("at", x, ng)
    u = matmul(hn, p["w_in"], name="at_in")
    consts = _rope_consts(S)
    qr = _at_qk("q", u, 0, AT_HEADS, AT_HD ** -0.5, p["q_g"], consts)
    kr = _at_qk("k", u, AT_HEADS, AT_KV, 1.0, p["k_g"], consts)
    vc0 = (AT_QW + AT_KW) // AT_HD
    (o, lse), cres = _own(flash_fwd(qr, kr, u, v_col0=vc0, comm=comm), comm)
    g = rowwise_fwd("at_gate", _gate_fn, [(o, 0, True), (u, (AT_QW + 2 * AT_KW) // 1024, True)], [],
                    [(1024, MXU_DTYPE)], W=1024, ncb=AT_QW // 1024)[0]
    xn = matmul(g, p["w_out"], residual=x, name="at_out")
    return xn, dict(x=x, ng=ng, hn=hn, u=u, qr=qr, kr=kr, o=o, lse=lse, g=g), cres


def at_layer_bwd(sv, p, dx, comm=None):
    u, qr, kr = sv["u"], sv["qr"], sv["kr"]
    S = u.shape[0]
    consts = _rope_consts(S)
    vc0 = (AT_QW + AT_KW) // AT_HD
    dg = matmul(dx, p["w_out"], tb=True, name="at_dg")
    dw_out = matmul(sv["g"], dx, ta=True, out_dtype=GRAD_WIRE_DTYPE, name="at_dw_out")
    do, dgate = rowwise_bwd("at_gate_bwd", _gate_fn, [(sv["o"], 0, True), (u, (AT_QW + 2 * AT_KW) // 1024, True)], [],
                            [dg], W=1024, ncb=AT_QW // 1024, diff_rows=[0, 1], diff_shared=[])
    dqs, delta = flash_bwd_dq(qr, kr, u, sv["o"], sv["lse"], do, v_col0=vc0)
    (dkr, dv), cres = _own(flash_bwd_dkv(qr, kr, u, sv["lse"], delta, do, v_col0=vc0, comm=comm), comm)
    dq_raw, dqg = _at_qk("q", u, 0, AT_HEADS, AT_HD ** -0.5, p["q_g"], consts, cot=dqs)
    dk_raw, dkg = _at_qk("k", u, AT_HEADS, AT_KV, 1.0, p["k_g"], consts, cot=dkr)
    du = _cat_mxu([dq_raw, dk_raw, dv, dgate])
    dx_prev, dng, dw_in = _in_out_bwd("at", sv["x"], sv["ng"], sv["hn"], du, p["w_in"], dx)
    return dx_prev, dict(w_in=dw_in, w_out=dw_out, q_g=dqg, k_g=dkg, ng=dng), cres


def _to_stream(t, dil):
    S = t.shape[0]
    return t.reshape(S // dil, dil, DL_HEADS, DL_HD).transpose(2, 1, 0, 3)


def _from_stream(t):
    H, dil, Ls, E = t.shape
    return t.transpose(2, 1, 0, 3).reshape(Ls * dil, H * E)


def _stream_to_hm(t):
    H, dil, Ls, w = t.shape
    return t.transpose(0, 2, 1, 3).reshape(H * Ls * dil, w)


def _hm_to_stream(t, dil):
    w = t.shape[1]
    S = t.shape[0] // DL_HEADS
    return t.reshape(DL_HEADS, S // dil, dil, w).transpose(0, 2, 1, 3)


OX_LSE = DL_HD
DOX_LSE, DOX_DM = DL_HD, DL_HD + 32


def _win(p_ref, c_ref, n_ref, h, T):
    return jnp.concatenate([p_ref[h, 0, T - DL_HALF:T, :], c_ref[h, 0], n_ref[h, 0, 0:DL_HALF, :]], axis=0)


def _win_specs(T, E, nb):
    return [pl.BlockSpec((DL_HEADS, 1, T, E), lambda d, n: (0, d, jnp.maximum(n - 1, 0), 0)),
            pl.BlockSpec((DL_HEADS, 1, T, E), lambda d, n: (0, d, n, 0)),
            pl.BlockSpec((DL_HEADS, 1, T, E), lambda d, n: (0, d, jnp.minimum(n + 1, nb - 1), 0))]


def _band_mask_q(n, T, W, Ls):
    i = lax.broadcasted_iota(jnp.int32, (T, W), 0)
    j = lax.broadcasted_iota(jnp.int32, (T, W), 1)
    kpos = n * T + j - DL_HALF
    return (jnp.abs(j - DL_HALF - i) <= DL_HALF) & (kpos >= 0) & (kpos < Ls)


def band_fwd(q, k, v, bias, *, scale):
    H, dil, Ls, E = q.shape
    T, W = _dl_tiles(Ls)
    nb = Ls // T

    def body(q_ref, kp_ref, kc_ref, kn_ref, vp_ref, vc_ref, vn_ref, b_ref, ox_ref):
        n = pl.program_id(1)
        mask = _band_mask_q(n, T, W, Ls)
        for h in range(H):
            kw = _win(kp_ref, kc_ref, kn_ref, h, T)
            vw = _win(vp_ref, vc_ref, vn_ref, h, T)
            s = _dot(q_ref[h, 0], kw, ((1,), (1,))) * scale + b_ref[h]
            s = jnp.where(mask, s, NEG_BIG)
            m = jnp.max(s, axis=1, keepdims=True)
            lse = m + jnp.log(jnp.sum(jnp.exp(s - m), axis=1, keepdims=True))
            p = jnp.exp(s - lse)
            ox_ref[h, 0, :, 0:E] = _dot(p, vw, ((1,), (0,)))
            ox_ref[h, 0, :, E:2 * E] = lse + jnp.zeros((T, E), F32)

    cur = pl.BlockSpec((H, 1, T, E), lambda d, n: (0, d, n, 0))
    return pl.pallas_call(
        body, name=f"band_fwd_{dil}", grid=(dil, nb),
        in_specs=[cur] + _win_specs(T, E, nb) + _win_specs(T, E, nb) + [pl.BlockSpec((H, T, W), lambda d, n: (0, 0, 0))],
        out_specs=pl.BlockSpec((H, 1, T, 2 * E), lambda d, n: (0, d, n, 0)),
        out_shape=jax.ShapeDtypeStruct((H, dil, Ls, 2 * E), F32),
        compiler_params=_cp("parallel", "parallel"),
    )(q, k, k, k, v, v, v, bias)


def band_bwd_dq(q, k, v, bias, dox, *, scale):
    H, dil, Ls, E = q.shape
    T, W = _dl_tiles(Ls, DL_TB)
    nb = Ls // T

    def body(q_ref, kp_ref, kc_ref, kn_ref, vp_ref, vc_ref, vn_ref, b_ref, dox_ref, dq_ref, db_ref):
        d, n = pl.program_id(0), pl.program_id(1)
        mask = _band_mask_q(n, T, W, Ls)
        first = jnp.logical_and(d == 0, n == 0)

        @pl.when(first)
        def _():
            db_ref[...] = jnp.zeros_like(db_ref)

        for h in range(H):
            kw = _win(kp_ref, kc_ref, kn_ref, h, T)
            vw = _win(vp_ref, vc_ref, vn_ref, h, T)
            dox = dox_ref[h, 0]
            do, lse, dm = dox[:, 0:E], dox[:, DOX_LSE:DOX_LSE + 1], dox[:, DOX_DM:DOX_DM + 1]
            s = _dot(q_ref[h, 0], kw, ((1,), (1,))) * scale + b_ref[h]
            p = jnp.where(mask, jnp.exp(jnp.where(mask, s, 0.0) - lse), 0.0)
            dp = _dot(do, vw, ((1,), (1,)))
            ds = p * (dp - dm)
            dq_ref[h, 0] = (_dot(ds, kw, ((1,), (0,))) * scale).astype(dq_ref.dtype)
            db_ref[h] += ds

    cur = pl.BlockSpec((H, 1, T, E), lambda d, n: (0, d, n, 0))
    bspec = pl.BlockSpec((H, T, W), lambda d, n: (0, 0, 0))
    return pl.pallas_call(
        body, name=f"band_bwd_dq_{dil}", grid=(dil, nb),
        in_specs=[cur] + _win_specs(T, E, nb) + _win_specs(T, E, nb) + [bspec,
                  pl.BlockSpec((H, 1, T, 2 * E), lambda d, n: (0, d, n, 0))],
        out_specs=[cur, bspec],
        out_shape=[jax.ShapeDtypeStruct((H, dil, Ls, E), MXU_DTYPE), jax.ShapeDtypeStruct((H, T, W), F32)],
        compiler_params=_cp("arbitrary", "arbitrary"),
    )(q, k, k, k, v, v, v, bias, dox)


def band_bwd_dkv(q, k, v, bias_t, dox, *, scale):
    H, dil, Ls, E = k.shape
    T, W = _dl_tiles(Ls, DL_TB)
    nb = Ls // T

    def body(qp_ref, qc_ref, qn_ref, k_ref, v_ref, b_ref, dp_ref, dc_ref, dn_ref, dk_ref, dv_ref):
        n = pl.program_id(1)
        iw = lax.broadcasted_iota(jnp.int32, (W, T), 0)
        j = lax.broadcasted_iota(jnp.int32, (W, T), 1)
        qpos = n * T + iw - DL_HALF
        mask = (jnp.abs(j + DL_HALF - iw) <= DL_HALF) & (qpos >= 0) & (qpos < Ls)
        for h in range(H):
            qw = _win(qp_ref, qc_ref, qn_ref, h, T)
            doxw = _win(dp_ref, dc_ref, dn_ref, h, T)
            dow, lsew, dmw = doxw[:, 0:E], doxw[:, DOX_LSE:DOX_LSE + 1], doxw[:, DOX_DM:DOX_DM + 1]
            s = _dot(qw, k_ref[h, 0], ((1,), (1,))) * scale + b_ref[h]
            p = jnp.where(mask, jnp.exp(jnp.where(mask, s, 0.0) - lsew), 0.0)
            dv_ref[h, 0] = _dot(p, dow, ((0,), (0,))).astype(dv_ref.dtype)
            dp = _dot(dow, v_ref[h, 0], ((1,), (1,)))
            ds = p * (dp - dmw)
            dk_ref[h, 0] = (_dot(ds, qw, ((0,), (0,))) * scale).astype(dk_ref.dtype)

    cur = pl.BlockSpec((H, 1, T, E), lambda d, n: (0, d, n, 0))
    return pl.pallas_call(
        body, name=f"band_bwd_dkv_{dil}", grid=(dil, nb),
        in_specs=_win_specs(T, E, nb) + [cur, cur, pl.BlockSpec((H, W, T), lambda d, n: (0, 0, 0))]
        + _win_specs(T, 2 * E, nb),
        out_specs=[cur, cur],
        out_shape=[jax.ShapeDtypeStruct((H, dil, Ls, E), MXU_DTYPE)] * 2,
        compiler_params=_cp("parallel", "parallel"),
    )(q, q, q, k, v, bias_t, dox, dox, dox)


def dl_merge_fwd(oxs, *, R=1024):
    rows = oxs[0].shape[0]
    R = min(R, rows)
    E = DL_HD

    def body(a_ref, b_ref, c_ref, o_ref):
        vals = [r[...] for r in (a_ref, b_ref, c_ref)]
        o_ref[...] = _dl_merge_fn(*[t[:, 0:E] for t in vals], *[t[:, OX_LSE:OX_LSE + 1] for t in vals])[0]

    spec = pl.BlockSpec((R, 2 * E), lambda i: (i, 0))
    return pl.pallas_call(
        body, name="dl_merge", grid=(rows // R,), in_specs=[spec] * 3,
        out_specs=pl.BlockSpec((R, E), lambda i: (i, 0)), out_shape=jax.ShapeDtypeStruct((rows, E), F32),
        compiler_params=_cp("parallel"),
    )(*oxs)


def dl_merge_bwd(oxs, do, *, R=1024):
    rows = oxs[0].shape[0]
    R = min(R, rows)
    E = DL_HD

    def body(a_ref, b_ref, c_ref, do_ref, da_ref, db_ref, dc_ref):
        vals = [r[...] for r in (a_ref, b_ref, c_ref)]
        os_ = [t[:, 0:E] for t in vals]
        ls_ = [t[:, OX_LSE:OX_LSE + 1] for t in vals]
        _, vjp = jax.vjp(_dl_merge_fn, *os_, *ls_)
        g = vjp((do_ref[...],))
        for k, d_ref in enumerate((da_ref, db_ref, dc_ref)):
            dm = jnp.sum(g[k] * os_[k], axis=1, keepdims=True) - g[3 + k]
            d_ref[:, 0:E] = g[k]
            d_ref[:, DOX_LSE:DOX_DM] = ls_[k] + jnp.zeros((R, DOX_DM - DOX_LSE), F32)
            d_ref[:, DOX_DM:2 * E] = dm + jnp.zeros((R, 2 * E - DOX_DM), F32)

    spec = pl.BlockSpec((R, 2 * E), lambda i: (i, 0))
    return pl.pallas_call(
        body, name="dl_merge_bwd", grid=(rows // R,), in_specs=[spec] * 3 + [pl.BlockSpec((R, E), lambda i: (i, 0))],
        out_specs=[spec] * 3, out_shape=[jax.ShapeDtypeStruct((rows, 2 * E), F32)] * 3,
        compiler_params=_cp("parallel"),
    )(*oxs, do)


def _dl_bias_tables(rel_bias, dil, T):
    W = T + 2 * DL_HALF
    bq, bk = _dl_bucket_tables(dil, T)
    idx = np.concatenate([bq.reshape(-1), bk.reshape(-1)])
    onehot_t = (np.arange(REL_BUCKETS)[:, None] == idx[None, :]).astype(np.float32)
    tab = matmul(rel_bias.T, jnp.asarray(onehot_t), exact=True, name=f"dl_bias_{dil}", tm=DL_HEADS, tk=REL_BUCKETS,
                 tn=_tile(2 * T * W, (8192, 4096, 2048, 1024, 512, 256, 128)))
    return tab[:, :T * W].reshape(DL_HEADS, T, W), tab[:, T * W:].reshape(DL_HEADS, W, T), bq


def dl_layer_fwd(x, ng, p):
    S = x.shape[0]
    hn = _prenorm("dl", x, ng)
    nqkv = 3 * len(DL_PAIRS) * DL_W
    uqkv = matmul(hn, p["w_in"], name="dl_in_qkv", b_cols=(0, nqkv), out_dtype=MXU_DTYPE)
    ugate = matmul(hn, p["w_in"], name="dl_in_gate", b_cols=(nqkv, DL_W))
    scale = DL_HD ** -0.5
    per_group, ox_hm = [], []
    for gi, (window, dil) in enumerate(DL_PAIRS):
        base = gi * 3 * DL_W
        Tf, Wf = _dl_tiles(S // dil)
        Tb, _ = _dl_tiles(S // dil, DL_TB)
        qs, ks, vs = [_to_stream(uqkv[:, base + c * DL_W:base + (c + 1) * DL_W], dil) for c in range(3)]
        bias, bias_t, bq = _dl_bias_tables(p["rel_bias"], dil, Tb)
        ox_s = band_fwd(qs, ks, vs, bias[:, :Tf, :Wf], scale=scale)
        per_group.append(dict(qs=qs, ks=ks, vs=vs, bias=bias, bias_t=bias_t, bq=bq, dil=dil))
        ox_hm.append(_stream_to_hm(ox_s))
    om = dl_merge_fwd(ox_hm)
    o = om.reshape(DL_HEADS, S, DL_HD).transpose(1, 0, 2).reshape(S, DL_W)
    g = rowwise_fwd("dl_gate", _gate_fn, [(o, 0, False), (ugate, 0, False)], [], [(DL_W, MXU_DTYPE)], W=DL_W)[0]
    xn = matmul(g, p["w_out"], residual=x, name="dl_out")
    return xn, dict(x=x, ng=ng, hn=hn, ugate=ugate, per_group=per_group, ox_hm=ox_hm, o=o, g=g)


def dl_layer_bwd(sv, p, dx):
    ugate = sv["ugate"]
    S = ugate.shape[0]
    scale = DL_HD ** -0.5
    dg = matmul(dx, p["w_out"], tb=True, name="dl_dg")
    dw_out = matmul(sv["g"], dx, ta=True, out_dtype=GRAD_WIRE_DTYPE, name="dl_dw_out")
    do, dgate = rowwise_bwd("dl_gate_bwd", _gate_fn, [(sv["o"], 0, False), (sv["ugate"], 0, False)], [], [dg], W=DL_W,
                            diff_rows=[0, 1], diff_shared=[])
    do_hm = do.reshape(S, DL_HEADS, DL_HD).transpose(1, 0, 2).reshape(DL_HEADS * S, DL_HD)
    dox_hm = dl_merge_bwd(sv["ox_hm"], do_hm)
    parts, dbs, onehots = [], [], []
    for gi, pg in enumerate(sv["per_group"]):
        dil = pg["dil"]
        T, W = _dl_tiles(S // dil, DL_TB)
        dox_s = _hm_to_stream(dox_hm[gi], dil)
        dq_s, dbias = band_bwd_dq(pg["qs"], pg["ks"], pg["vs"], pg["bias"], dox_s, scale=scale)
        dk_s, dv_s = band_bwd_dkv(pg["qs"], pg["ks"], pg["vs"], pg["bias_t"], dox_s, scale=scale)
        parts += [_from_stream(dq_s), _from_stream(dk_s), _from_stream(dv_s)]
        dbs.append(dbias.reshape(DL_HEADS, T * W))
        onehots.append((pg["bq"].reshape(-1)[:, None] == np.arange(REL_BUCKETS)[None, :]).astype(np.float32))
    drel = matmul(jnp.concatenate(dbs, axis=1), jnp.asarray(np.concatenate(onehots, axis=0)), exact=True,
                  name="dl_drel", tm=DL_HEADS, tn=REL_BUCKETS, tk=2048)
    du = _cat_mxu(parts + [dgate])
    dx_prev, dng, dw_in = _in_out_bwd("dl", sv["x"], sv["ng"], sv["hn"], du, p["w_in"], dx)
    return dx_prev, dict(w_in=dw_in, w_out=dw_out, rel_bias=drel.T, ng=dng)


WEIGHT_ORDER = ['norm_g', 'final_g', 'rel_bias', 'hgrn_lb', 'ssd_w_in', 'ssd_conv_w', 'ssd_conv_b', 'ssd_dt_bias',
                'ssd_a_log', 'ssd_d', 'ssd_norm_g', 'ssd_w_out', 'hg_w_in', 'hg_norm_g', 'hg_w_out', 'at_w_in',
                'at_q_norm_g', 'at_k_norm_g', 'at_w_out', 'dl_w_in', 'dl_w_out']
BIG_IN = ['ssd_w_in', 'hg_w_in', 'at_w_in', 'dl_w_in']
BIG_OUT = ['ssd_w_out', 'hg_w_out', 'at_w_out', 'dl_w_out']
BIG = BIG_IN + BIG_OUT
SMALL = [n for n in WEIGHT_ORDER if n not in BIG]
LANES = 128


def _pack(arrs):
    flat = jnp.concatenate([a.reshape(-1).astype(F32) for a in arrs])
    n = flat.shape[0]
    rows = -(-n // (8 * LANES)) * 8
    return jnp.pad(flat, (0, rows * LANES - n)).reshape(rows, LANES)


def _unpack(buf, shapes):
    flat = buf.reshape(-1)
    out, off = [], 0
    for shp in shapes:
        n = int(np.prod(shp)) if len(shp) else 1
        out.append(flat[off:off + n].reshape(shp))
        off += n
    return out


def kernel(x, norm_g, final_g, rel_bias, hgrn_lb, ssd_w_in, ssd_conv_w, ssd_conv_b, ssd_dt_bias, ssd_a_log, ssd_d, ssd_norm_g, ssd_w_out, hg_w_in, hg_norm_g, hg_w_out, at_w_in, at_q_norm_g, at_k_norm_g, at_w_out, dl_w_in, dl_w_out, loss_target, m_norm_g, m_final_g, m_rel_bias, m_hgrn_lb, m_ssd_w_in, m_ssd_conv_w, m_ssd_conv_b, m_ssd_dt_bias, m_ssd_a_log, m_ssd_d, m_ssd_norm_g, m_ssd_w_out, m_hg_w_in, m_hg_norm_g, m_hg_w_out, m_at_w_in, m_at_q_norm_g, m_at_k_norm_g, m_at_w_out, m_dl_w_in, m_dl_w_out, v_norm_g, v_final_g, v_rel_bias, v_hgrn_lb, v_ssd_w_in, v_ssd_conv_w, v_ssd_conv_b, v_ssd_dt_bias, v_ssd_a_log, v_ssd_d, v_ssd_norm_g, v_ssd_w_out, v_hg_w_in, v_hg_norm_g, v_hg_w_out, v_at_w_in, v_at_q_norm_g, v_at_k_norm_g, v_at_w_out, v_dl_w_in, v_dl_w_out):
    w = dict(norm_g=norm_g, final_g=final_g, rel_bias=rel_bias, hgrn_lb=hgrn_lb, ssd_w_in=ssd_w_in, ssd_conv_w=ssd_conv_w, ssd_conv_b=ssd_conv_b, ssd_dt_bias=ssd_dt_bias, ssd_a_log=ssd_a_log, ssd_d=ssd_d, ssd_norm_g=ssd_norm_g, ssd_w_out=ssd_w_out, hg_w_in=hg_w_in, hg_norm_g=hg_norm_g, hg_w_out=hg_w_out, at_w_in=at_w_in, at_q_norm_g=at_q_norm_g, at_k_norm_g=at_k_norm_g, at_w_out=at_w_out, dl_w_in=dl_w_in, dl_w_out=dl_w_out)
    m = dict(norm_g=m_norm_g, final_g=m_final_g, rel_bias=m_rel_bias, hgrn_lb=m_hgrn_lb, ssd_w_in=m_ssd_w_in, ssd_conv_w=m_ssd_conv_w, ssd_conv_b=m_ssd_conv_b, ssd_dt_bias=m_ssd_dt_bias, ssd_a_log=m_ssd_a_log, ssd_d=m_ssd_d, ssd_norm_g=m_ssd_norm_g, ssd_w_out=m_ssd_w_out, hg_w_in=m_hg_w_in, hg_norm_g=m_hg_norm_g, hg_w_out=m_hg_w_out, at_w_in=m_at_w_in, at_q_norm_g=m_at_q_norm_g, at_k_norm_g=m_at_k_norm_g, at_w_out=m_at_w_out, dl_w_in=m_dl_w_in, dl_w_out=m_dl_w_out)
    v = dict(norm_g=v_norm_g, final_g=v_final_g, rel_bias=v_rel_bias, hgrn_lb=v_hgrn_lb, ssd_w_in=v_ssd_w_in, ssd_conv_w=v_ssd_conv_w, ssd_conv_b=v_ssd_conv_b, ssd_dt_bias=v_ssd_dt_bias, ssd_a_log=v_ssd_a_log, ssd_d=v_ssd_d, ssd_norm_g=v_ssd_norm_g, ssd_w_out=v_ssd_w_out, hg_w_in=v_hg_w_in, hg_norm_g=v_hg_norm_g, hg_w_out=v_hg_w_out, at_w_in=v_at_w_in, at_q_norm_g=v_at_q_norm_g, at_k_norm_g=v_at_k_norm_g, at_w_out=v_at_w_out, dl_w_in=v_dl_w_in, dl_w_out=v_dl_w_out)
    me = 4 * lax.axis_index("x") + 2 * lax.axis_index("y") + lax.axis_index("c")
    xs = x[0]
    S = xs.shape[0]

    shard2d = {n: w[n][0] for n in BIG}
    wire = {n: shard2d[n].astype(MXU_DTYPE) for n in BIG}

    def ag(names):
        return ("ag", [wire[n] for n in names])

    def assemble(names, blks):
        out = {}
        for n, blk in zip(names, blks):
            r, c = shard2d[n].shape
            out[n] = blk.transpose(1, 0, 2).reshape(r, N_DEV * c) if n in BIG_IN else blk.reshape(N_DEV * r, c)
        return out

    def a2a(names, gw):
        bufs = []
        for n in names:
            r, c = shard2d[n].shape
            bufs.append(gw[n].reshape(r, N_DEV, c).transpose(1, 0, 2) if n in BIG_IN else gw[n].reshape(N_DEV, r, c))
        return ("a2a", bufs)

    ssd_w, hg_w, at_w, dl_w = (["ssd_w_in", "ssd_w_out"], ["hg_w_in", "hg_w_out"], ["at_w_in", "at_w_out"],
                               ["dl_w_in", "dl_w_out"])
    full = assemble(ssd_w, allgather_two_level_multi([wire[n] for n in ssd_w], name="allgather_ssd_weights"))
    ncw = ssd_conv_w.shape[2]
    nhg = hg_norm_g.shape[1]
    small_shard = jnp.zeros((8, 512), F32)
    small_shard = small_shard.at[:SSD_CONV, :ncw].set(ssd_conv_w[0]).at[SSD_CONV, :nhg].set(hg_norm_g[0])
    small_all = allgather_two_level(small_shard, name="allgather_small_weights")
    conv_w_full = small_all[:, :SSD_CONV, :ncw].transpose(1, 0, 2).reshape(SSD_CONV, N_DEV * ncw)
    hg_norm_full = small_all[:, SSD_CONV, :nhg].reshape(1, N_DEV * nhg)

    p_ssd = dict(w_in=jnp.pad(full["ssd_w_in"], ((0, 0), (0, SSD_IN_PAD - SSD_IN))), w_out=full["ssd_w_out"],
                 conv_w=conv_w_full, conv_b=ssd_conv_b,
                 dt_bias=jnp.pad(ssd_dt_bias.reshape(1, 2 * SSD_HEADS), ((0, 0), (0, 128 - 2 * SSD_HEADS))),
                 alog=jnp.pad(ssd_a_log.reshape(1, 2 * SSD_HEADS), ((0, 0), (0, 128 - 2 * SSD_HEADS))),
                 dexp=jnp.repeat(ssd_d.reshape(-1), SSD_HEADDIM)[None, :], norm_g=ssd_norm_g)
    x1, sv0, got, got_at_in = ssd_layer_fwd(xs, norm_g[0:1], p_ssd, comm=ag(hg_w), comm1=ag(["at_w_in"]))
    full.update(assemble(hg_w, got))
    p_hg = dict(w_in=full["hg_w_in"], w_out=full["hg_w_out"], norm_g=hg_norm_full, hgrn_lb=hgrn_lb)
    x2, sv1, got_at_out, _ = hg_layer_fwd(x1, norm_g[1:2], p_hg, comm0=ag(["at_w_out"]))
    full.update(assemble(at_w, got_at_in + got_at_out))
    p_at = dict(w_in=full["at_w_in"], w_out=full["at_w_out"], q_g=at_q_norm_g, k_g=at_k_norm_g)
    x3, sv2, got_dl = at_layer_fwd(x2, norm_g[2:3], p_at, comm=ag(dl_w))
    full.update(assemble(dl_w, got_dl))
    p_dl = dict(w_in=full["dl_w_in"], w_out=full["dl_w_out"], rel_bias=rel_bias)
    x4, sv3 = dl_layer_fwd(x3, norm_g[3:4], p_dl)
    loss_part, dx4, dfinal = loss_head(x4, final_g[None, :], loss_target[0])
    dx3, g3 = dl_layer_bwd(sv3, p_dl, dx4)
    dx2, g2, recv_dl = at_layer_bwd(sv2, p_at, dx3, comm=a2a(dl_w, dict(dl_w_in=g3["w_in"], dl_w_out=g3["w_out"])))
    dx1, g1, recv_at = hg_layer_bwd(sv1, p_hg, dx2, comm=a2a(at_w, dict(at_w_in=g2["w_in"], at_w_out=g2["w_out"])))
    dx0, g0, recv_hg, recv_ssd = ssd_layer_bwd(
        sv0, p_ssd, dx1, comm=a2a(hg_w, dict(hg_w_in=g1["w_in"], hg_w_out=g1["w_out"])),
        tail_comm=lambda g_in, g_out: (a2a(["ssd_w_in"], dict(ssd_w_in=g_in)) if g_out is None
                                       else a2a(["ssd_w_out"], dict(ssd_w_out=g_out))))
    recv = dict(zip(ssd_w + hg_w + at_w + dl_w, recv_ssd + recv_hg + recv_at + recv_dl))

    small_full = dict(
        norm_g=jnp.concatenate([g0["ng"], g1["ng"], g2["ng"], g3["ng"]], axis=0), final_g=dfinal[0],
        rel_bias=g3["rel_bias"], hgrn_lb=g1["hgrn_lb"], ssd_conv_w=g0["conv_w"][None], ssd_conv_b=g0["conv_b"],
        ssd_dt_bias=g0["dt_bias"].reshape(1, 2, SSD_HEADS), ssd_a_log=g0["a_log"].reshape(1, 2, SSD_HEADS),
        ssd_d=g0["d"], ssd_norm_g=g0["norm_g"], hg_norm_g=g1["norm_g"], at_q_norm_g=g2["q_g"], at_k_norm_g=g2["k_g"])
    packed = _pack([loss_part[0, 0:1]] + [small_full[n] for n in SMALL])
    summed = sum_parts(allgather_two_level(packed, name="allgather_small_grads"), name="sum_small_grads")
    parts = _unpack(summed, [()] + [small_full[n].shape for n in SMALL])
    loss = parts[0]
    gsmall = dict(zip(SMALL, parts[1:]))
    gsmall["ssd_conv_w"] = lax.dynamic_slice_in_dim(gsmall["ssd_conv_w"], me * ncw, ncw, axis=2)
    gsmall["hg_norm_g"] = lax.dynamic_slice_in_dim(gsmall["hg_norm_g"], me * nhg, nhg, axis=1)
    shapes = [w[n].shape for n in SMALL]
    d_p, m_p, v_p = adamw_plain(_pack([w[n] for n in SMALL]), _pack([gsmall[n] for n in SMALL]),
                                _pack([m[n] for n in SMALL]), _pack([v[n] for n in SMALL]), name="adamw_small")
    grads = dict(gsmall)
    deltas = dict(zip(SMALL, _unpack(d_p, shapes)))
    new_m = dict(zip(SMALL, _unpack(m_p, shapes)))
    new_v = dict(zip(SMALL, _unpack(v_p, shapes)))

    for n in BIG:
        gs, ds, ms, vs = adamw_sum(recv[n], shard2d[n], m[n][0], v[n][0], name=f"adamw_{n}")
        grads[n], deltas[n], new_m[n], new_v[n] = gs[None], ds[None], ms[None], vs[None]

    return (loss, dx0[None], *[grads[n] for n in WEIGHT_ORDER], *[deltas[n] for n in WEIGHT_ORDER],
            *[new_m[n] for n in WEIGHT_ORDER], *[new_v[n] for n in WEIGHT_ORDER])
```

```python
import functools
import math

import jax
import jax.numpy as jnp
import numpy as np
from jax import lax
from jax.experimental import pallas as pl
from jax.experimental.pallas import tpu as pltpu

F32 = jnp.float32
MXU_DTYPE = jnp.bfloat16
GRAD_WIRE_DTYPE = jnp.bfloat16
HIGHEST = lax.Precision.HIGHEST
MESH_ID = pl.DeviceIdType.MESH
N_DEV = 8

D_MODEL = 1024
EPS = 1e-6
NEG_BIG = -1e30

SSD_DI = 2048
SSD_HEADDIM = 64
SSD_HEADS = 32
SSD_GROUPS = 4
SSD_HPG = 8
SSD_STATE = 128
SSD_CONV = 7
SSD_CHUNK = 128
SSD_CONV_CH = SSD_DI + 2 * SSD_GROUPS * SSD_STATE
SSD_IN = SSD_DI + SSD_CONV_CH + 2 * SSD_HEADS
SSD_IN_PAD = 5376

HG_CHUNK = 32
HG_HEADS = 8
HG_D = 128
HG_W = 1024

AT_HEADS = 16
AT_KV = 8
AT_HD = 128
AT_QW = 2048
AT_KW = 1024
GRID_W = 64
ROPE_THETA = 10000.0

DL_PAIRS = ((128, 1), (512, 4), (2048, 16))
DL_HEADS = 16
DL_HD = 64
DL_W = 1024
DL_HALF = 64
REL_BUCKETS = 32
REL_MAX_DIST = 1024

ADAM_LR = 0.001
ADAM_B1 = 0.9
ADAM_B2 = 0.999
ADAM_EPS = 1e-08
ADAM_WD = 0.01
ADAM_STEP = 10

VMEM_LIMIT = 56 * 1024 * 1024


def _cp(*sem):
    return pltpu.CompilerParams(dimension_semantics=tuple(sem), vmem_limit_bytes=VMEM_LIMIT)


def _tile(n, cands=(1024, 768, 512, 384, 256, 128)):
    for c in cands:
        if n % c == 0:
            return c
    return n


def _dot(a, b, dims):
    return lax.dot_general(a.astype(MXU_DTYPE), b.astype(MXU_DTYPE), (dims, ((), ())), preferred_element_type=F32)


def _dot_exact(a, b, dims):
    return lax.dot_general(a, b, (dims, ((), ())), precision=HIGHEST, preferred_element_type=F32)


def _silu(x):
    return x * jax.nn.sigmoid(x)


def _my_pos():
    return lax.axis_index("x"), lax.axis_index("y"), lax.axis_index("c")


def _flat(px, py, pc):
    return 4 * px + 2 * py + pc


def _peers():
    x_, y_, c_ = _my_pos()
    out = []
    for k in range(1, N_DEV):
        fx, fy, fc = (k >> 2) & 1, (k >> 1) & 1, k & 1
        out.append(((1 - x_) if fx else x_, (1 - y_) if fy else y_, (1 - c_) if fc else c_))
    return out


def _comm_copies(kind, in_refs, out_refs, send_sems, recv_sems, local_sems):
    me = _flat(*_my_pos())
    local, starts, waits = [], [], []
    for b, (i_ref, o_ref) in enumerate(zip(in_refs, out_refs)):
        local.append(pltpu.make_async_copy(i_ref if kind == "ag" else i_ref.at[me], o_ref.at[me], local_sems.at[b]))
        for k, p in enumerate(_peers()):
            src = i_ref if kind == "ag" else i_ref.at[_flat(*p)]
            starts.append(pltpu.make_async_remote_copy(
                src_ref=src, dst_ref=o_ref.at[me], send_sem=send_sems.at[b, k], recv_sem=recv_sems.at[b, k],
                device_id=p, device_id_type=MESH_ID))
            waits.append(pltpu.make_async_remote_copy(
                src_ref=src, dst_ref=o_ref.at[_flat(*p)], send_sem=send_sems.at[b, k], recv_sem=recv_sems.at[b, k],
                device_id=p, device_id_type=MESH_ID))
    return local, starts, waits


def pcall(body, comm, *, name, grid, in_specs, out_specs, out_shape, scratch_shapes=(), compiler_params=None):
    single = not isinstance(out_specs, (list, tuple))
    out_specs_l = [out_specs] if single else list(out_specs)
    out_shape_l = [out_shape] if single else list(out_shape)
    if comm is None:
        return pl.pallas_call(body, name=name, grid=grid, in_specs=in_specs, out_specs=out_specs, out_shape=out_shape,
                              scratch_shapes=list(scratch_shapes), compiler_params=compiler_params)
    kind, bufs = comm
    nb, n_in, n_out, n_scr = len(bufs), len(in_specs), len(out_specs_l), len(scratch_shapes)
    c_shape = [jax.ShapeDtypeStruct(((N_DEV,) + b.shape) if kind == "ag" else b.shape, b.dtype) for b in bufs]
    anyspec = pl.BlockSpec(memory_space=pl.ANY)

    def body2(*refs):
        ins, c_ins = refs[:n_in], refs[n_in:n_in + nb]
        outs = refs[n_in + nb:n_in + nb + n_out]
        c_outs = refs[n_in + nb + n_out:n_in + 2 * nb + n_out]
        scr = refs[n_in + 2 * nb + n_out:n_in + 2 * nb + n_out + n_scr]
        send_sems, recv_sems, local_sems = refs[n_in + 2 * nb + n_out + n_scr:]
        first = last = None
        for ax, g in enumerate(grid):
            pid = pl.program_id(ax)
            first = (pid == 0) if first is None else jnp.logical_and(first, pid == 0)
            last = (pid == g - 1) if last is None else jnp.logical_and(last, pid == g - 1)

        @pl.when(first)
        def _():
            local, starts, _ = _comm_copies(kind, c_ins, c_outs, send_sems, recv_sems, local_sems)
            for cp in local + starts:
                cp.start()

        body(*ins, *outs, *scr)

        @pl.when(last)
        def _():
            local, _, waits = _comm_copies(kind, c_ins, c_outs, send_sems, recv_sems, local_sems)
            for cp in waits + local:
                cp.wait()

    call = pl.pallas_call(
        body2, name=name, grid=grid, in_specs=list(in_specs) + [anyspec] * nb,
        out_specs=out_specs_l + [anyspec] * nb, out_shape=out_shape_l + c_shape,
        scratch_shapes=list(scratch_shapes) + [pltpu.SemaphoreType.DMA((nb, N_DEV - 1)),
                                               pltpu.SemaphoreType.DMA((nb, N_DEV - 1)), pltpu.SemaphoreType.DMA((nb,))],
        compiler_params=compiler_params)

    def run(*args):
        res = call(*args, *bufs)
        own = res[:n_out]
        return (own[0] if single else list(own)), list(res[n_out:])

    return run


def matmul(a, b, *, name, ta=False, tb=False, residual=None, out_dtype=F32, exact=False, tm=None, tn=None, tk=None,
           b_cols=None, comm=None):
    M, K = (a.shape[1], a.shape[0]) if ta else a.shape
    n0, N = b_cols if b_cols is not None else (0, b.shape[0] if tb else b.shape[1])
    tm = tm or _tile(M, (1024, 512, 256, 128))
    tn = tn or _tile(N, (1024, 768, 512, 384, 256, 128))
    tk = tk or _tile(K, (2048, 1024, 768, 512, 384, 256, 128))
    nk = K // tk
    dims = (((0,) if ta else (1,)), ((1,) if tb else (0,)))

    def body(*refs):
        if residual is None:
            a_ref, b_ref, o_ref, acc = refs
            r_ref = None
        else:
            a_ref, b_ref, r_ref, o_ref, acc = refs
        k = pl.program_id(2)

        @pl.when(k == 0)
        def _():
            acc[...] = jnp.zeros_like(acc)

        if exact:
            acc[...] += _dot_exact(a_ref[...], b_ref[...], dims)
        else:
            acc[...] += _dot(a_ref[...], b_ref[...], dims)

        @pl.when(k == nk - 1)
        def _():
            r = acc[...]
            if r_ref is not None:
                r = r + r_ref[...]
            o_ref[...] = r.astype(o_ref.dtype)

    a_spec = pl.BlockSpec((tk, tm), lambda i, j, k: (k, i)) if ta else pl.BlockSpec((tm, tk), lambda i, j, k: (i, k))
    assert n0 % tn == 0
    jb = n0 // tn
    b_spec = (pl.BlockSpec((tn, tk), lambda i, j, k: (j + jb, k)) if tb
              else pl.BlockSpec((tk, tn), lambda i, j, k: (k, j + jb)))
    in_specs = [a_spec, b_spec]
    args = [a, b]
    if residual is not None:
        in_specs.append(pl.BlockSpec((tm, tn), lambda i, j, k: (i, j)))
        args.append(residual)
    return pcall(
        body, comm, name=name, grid=(M // tm, N // tn, nk), in_specs=in_specs,
        out_specs=pl.BlockSpec((tm, tn), lambda i, j, k: (i, j)),
        out_shape=jax.ShapeDtypeStruct((M, N), out_dtype),
        scratch_shapes=[pltpu.VMEM((tm, tn), F32)],
        compiler_params=_cp("parallel", "parallel", "arbitrary"),
    )(*args)


def _row_specs2(rows, shared, R, W, ncb):
    specs = []
    for arr, col0, per_j, *wd in rows:
        w = wd[0] if wd else W
        if per_j:
            assert col0 % ncb == 0
            specs.append(pl.BlockSpec((R, ncb * w), lambda i, c=col0 // ncb: (i, c)))
        else:
            specs.append(pl.BlockSpec((R, w), lambda i, c=col0: (i, c)))
    for arr, per_j in shared:
        specs.append(pl.BlockSpec((arr.shape[0], ncb * W if per_j else arr.shape[1]), lambda i: (0, 0)))
    return specs


def _col_block(ref, per_j, j, w):
    return ref[:, j * w:(j + 1) * w] if per_j else ref[...]


def rowwise_fwd(name, fn, rows, shared, outs, *, W, ncb=1, R=256):
    S = rows[0][0].shape[0]
    R = min(R, S)
    nr, ns = len(rows), len(shared)
    widths = [(r[3] if len(r) > 3 else W) for r in rows]
    per_j = [r[2] for r in rows] + [s[1] for s in shared]
    ws = widths + [W] * ns

    def body(*refs):
        for j in range(ncb):
            vals = [_col_block(refs[k], per_j[k], j, ws[k]) for k in range(nr + ns)]
            res = fn(*vals)
            for o_ref, r, (wo, _) in zip(refs[nr + ns:], res, outs):
                o_ref[:, j * wo:(j + 1) * wo] = r.astype(o_ref.dtype)

    return pl.pallas_call(
        body, name=name, grid=(S // R,),
        in_specs=_row_specs2(rows, shared, R, W, ncb),
        out_specs=[pl.BlockSpec((R, ncb * w), lambda i: (i, 0)) for w, _ in outs],
        out_shape=[jax.ShapeDtypeStruct((S, ncb * w), dt) for w, dt in outs],
        compiler_params=_cp("parallel"),
    )(*[r[0] for r in rows], *[s[0] for s in shared])


def rowwise_bwd(name, fn, rows, shared, cots, *, W, ncb=1, R=256, diff_rows, diff_shared, add=None):
    S = rows[0][0].shape[0]
    R = min(R, S)
    nr, ns, nc = len(rows), len(shared), len(cots)
    widths = [(r[3] if len(r) > 3 else W) for r in rows]
    per_j = [r[2] for r in rows] + [s[1] for s in shared]
    ws = widths + [W] * ns
    wo = [c.shape[1] // ncb for c in cots]
    dws = [widths[r] for r in diff_rows]

    def body(*refs):
        ins = refs[:nr + ns]
        ct_refs = refs[nr + ns:nr + ns + nc]
        pos = nr + ns + nc
        add_ref = None
        if add is not None:
            add_ref = refs[pos]
            pos += 1
        drow_refs = refs[pos:pos + len(diff_rows)]
        dsh_refs = refs[pos + len(diff_rows):]
        i = pl.program_id(0)
        tot = [None] * len(diff_shared)
        for j in range(ncb):
            vals = [_col_block(ins[k], per_j[k], j, ws[k]) for k in range(nr + ns)]

            def f(*dv):
                full = list(vals)
                for idx, v in zip(list(diff_rows) + [nr + s for s in diff_shared], dv):
                    full[idx] = v
                return tuple(fn(*full))

            prim = [vals[idx] for idx in diff_rows] + [vals[nr + s] for s in diff_shared]
            _, vjp = jax.vjp(f, *prim)
            grads = vjp(tuple(c[:, j * w:(j + 1) * w] for c, w in zip(ct_refs, wo)))
            for k, (d_ref, w) in enumerate(zip(drow_refs, dws)):
                g = grads[k]
                if k == 0 and add_ref is not None:
                    g = g + add_ref[:, j * w:(j + 1) * w]
                d_ref[:, j * w:(j + 1) * w] = g
            for k, (d_ref, s) in enumerate(zip(dsh_refs, diff_shared)):
                g = grads[len(diff_rows) + k]
                if shared[s][1]:
                    @pl.when(i == 0)
                    def _(d_ref=d_ref, g=g, j=j):
                        d_ref[:, j * W:(j + 1) * W] = g

                    @pl.when(i != 0)
                    def _(d_ref=d_ref, g=g, j=j):
                        d_ref[:, j * W:(j + 1) * W] += g
                else:
                    tot[k] = g if tot[k] is None else tot[k] + g
        for k, (d_ref, s) in enumerate(zip(dsh_refs, diff_shared)):
            if not shared[s][1]:
                @pl.when(i == 0)
                def _(d_ref=d_ref, g=tot[k]):
                    d_ref[...] = g

                @pl.when(i != 0)
                def _(d_ref=d_ref, g=tot[k]):
                    d_ref[...] += g

    in_specs = _row_specs2(rows, shared, R, W, ncb)
    in_specs += [pl.BlockSpec((R, ncb * w), lambda i: (i, 0)) for w in wo]
    args = [r[0] for r in rows] + [s[0] for s in shared] + list(cots)
    if add is not None:
        in_specs.append(pl.BlockSpec((R, ncb * dws[0]), lambda i: (i, 0)))
        args.append(add)
    out_specs = [pl.BlockSpec((R, ncb * w), lambda i: (i, 0)) for w in dws]
    out_shape = [jax.ShapeDtypeStruct((S, ncb * w), F32) for w in dws]
    for s in diff_shared:
        arr, pj = shared[s]
        shp = (arr.shape[0], ncb * W if pj else arr.shape[1])
        out_specs.append(pl.BlockSpec(shp, lambda i: (0, 0)))
        out_shape.append(jax.ShapeDtypeStruct(shp, F32))
    return pl.pallas_call(
        body, name=name, grid=(S // R,), in_specs=in_specs, out_specs=out_specs, out_shape=out_shape,
        compiler_params=_cp("arbitrary"),
    )(*args)


def _rms(x, g):
    return x * lax.rsqrt(jnp.mean(x * x, axis=-1, keepdims=True) + EPS) * g


def _prenorm_fn(x, g):
    return (_rms(x, g),)


def loss_head(x, g, tgt, *, R=256):
    S, D = x.shape
    R = min(R, S)

    def fn(xv, gv, tv):
        err = _rms(xv, gv) - tv
        return 0.5 * jnp.sum(jnp.mean(err * err, axis=-1, keepdims=True), axis=0, keepdims=True)

    def body(x_ref, g_ref, t_ref, loss_ref, dx_ref, dg_ref):
        i = pl.program_id(0)
        tv = t_ref[...]
        val, vjp = jax.vjp(lambda a, b: fn(a, b, tv), x_ref[...], g_ref[...])
        dx, dg = vjp(jnp.ones((1, 1), F32))
        dx_ref[...] = dx

        @pl.when(i == 0)
        def _():
            loss_ref[...] = jnp.zeros_like(loss_ref) + val
            dg_ref[...] = dg

        @pl.when(i != 0)
        def _():
            loss_ref[...] += val
            dg_ref[...] += dg

    return pl.pallas_call(
        body, name="loss_head", grid=(S // R,),
        in_specs=[pl.BlockSpec((R, D), lambda i: (i, 0)), pl.BlockSpec((1, D), lambda i: (0, 0)),
                  pl.BlockSpec((R, D), lambda i: (i, 0))],
        out_specs=[pl.BlockSpec((1, 128), lambda i: (0, 0)), pl.BlockSpec((R, D), lambda i: (i, 0)),
                   pl.BlockSpec((1, D), lambda i: (0, 0))],
        out_shape=[jax.ShapeDtypeStruct((1, 128), F32), jax.ShapeDtypeStruct((S, D), F32),
                   jax.ShapeDtypeStruct((1, D), F32)],
        compiler_params=_cp("arbitrary"),
    )(x, g, tgt)


@jax.custom_vjp
def _softplus(x):
    z = jnp.exp(-jnp.abs(x))
    u = 1.0 + z
    log1p = jnp.where(u == 1.0, z, jnp.log(u) * (z / jnp.where(u == 1.0, 1.0, u - 1.0)))
    return jnp.maximum(x, 0.0) + log1p


def _softplus_fwd(x):
    return _softplus(x), x


def _softplus_bwd(x, ct):
    return (ct * jax.nn.sigmoid(x),)


_softplus.defvjp(_softplus_fwd, _softplus_bwd)


def _dt_fn(raw, bias):
    return (_softplus(raw + bias),)


CONV_CB = 128
CONV_RB = 256
CONV_PAD = 8


def ssd_conv_fwd(u, conv_w, conv_b):
    S = u.shape[0]
    ncb = SSD_CONV_CH // CONV_CB
    col0 = SSD_DI // CONV_CB
    RB = min(CONV_RB, S)

    def body(x_ref, w_ref, b_ref, o_ref, pad):
        pad[0:CONV_PAD, :] = jnp.zeros((CONV_PAD, CONV_CB), F32)
        pad[S + CONV_PAD:S + 2 * CONV_PAD, :] = jnp.zeros((CONV_PAD, CONV_CB), F32)
        pad[CONV_PAD:S + CONV_PAD, :] = x_ref[...]
        w = w_ref[...]
        b = b_ref[...]
        for r in range(S // RB):
            acc = jnp.zeros((RB, CONV_CB), F32) + b
            for k in range(SSD_CONV):
                off = r * RB + CONV_PAD + k - SSD_CONV // 2
                acc = acc + pad[off:off + RB, :] * w[k:k + 1, :]
            o_ref[r * RB:(r + 1) * RB, :] = _silu(acc)

    return pl.pallas_call(
        body, name="ssd_conv_fwd", grid=(ncb,),
        in_specs=[pl.BlockSpec((S, CONV_CB), lambda j: (0, col0 + j)),
                  pl.BlockSpec((SSD_CONV, CONV_CB), lambda j: (0, j)),
                  pl.BlockSpec((1, CONV_CB), lambda j: (0, j))],
        out_specs=pl.BlockSpec((S, CONV_CB), lambda j: (0, j)),
        out_shape=jax.ShapeDtypeStruct((S, SSD_CONV_CH), F32),
        scratch_shapes=[pltpu.VMEM((S + 2 * CONV_PAD, CONV_CB), F32)],
        compiler_params=_cp("parallel"),
    )(u, conv_w, conv_b)


def ssd_conv_bwd(u, conv_w, conv_b, dact):
    S = u.shape[0]
    ncb = SSD_CONV_CH // CONV_CB
    col0 = SSD_DI // CONV_CB
    RB = min(CONV_RB, S)
    half = SSD_CONV // 2

    def body(x_ref, w_ref, b_ref, da_ref, dx_ref, dw_ref, db_ref, xpad, dpad):
        z8 = jnp.zeros((CONV_PAD, CONV_CB), F32)
        xpad[0:CONV_PAD, :] = z8
        xpad[S + CONV_PAD:S + 2 * CONV_PAD, :] = z8
        dpad[0:CONV_PAD, :] = z8
        dpad[S + CONV_PAD:S + 2 * CONV_PAD, :] = z8
        xpad[CONV_PAD:S + CONV_PAD, :] = x_ref[...]
        w = w_ref[...]
        b = b_ref[...]
        dws = [jnp.zeros((1, CONV_CB), F32) for _ in range(SSD_CONV)]
        db = jnp.zeros((1, CONV_CB), F32)
        for r in range(S // RB):
            acc = jnp.zeros((RB, CONV_CB), F32) + b
            xs = []
            for k in range(SSD_CONV):
                off = r * RB + CONV_PAD + k - half
                xk = xpad[off:off + RB, :]
                xs.append(xk)
                acc = acc + xk * w[k:k + 1, :]
            sg = jax.nn.sigmoid(acc)
            dc = da_ref[r * RB:(r + 1) * RB, :] * (sg * (1.0 + acc * (1.0 - sg)))
            dpad[r * RB + CONV_PAD:(r + 1) * RB + CONV_PAD, :] = dc
            db = db + jnp.sum(dc, axis=0, keepdims=True)
            for k in range(SSD_CONV):
                dws[k] = dws[k] + jnp.sum(xs[k] * dc, axis=0, keepdims=True)
        for r in range(S // RB):
            acc = jnp.zeros((RB, CONV_CB), F32)
            for k in range(SSD_CONV):
                off = r * RB + CONV_PAD + half - k
                acc = acc + dpad[off:off + RB, :] * w[k:k + 1, :]
            dx_ref[r * RB:(r + 1) * RB, :] = acc
        for k in range(SSD_CONV):
            dw_ref[k:k + 1, :] = dws[k]
        dw_ref[SSD_CONV:SSD_CONV + 1, :] = jnp.zeros((1, CONV_CB), F32)
        db_ref[...] = db

    return pl.pallas_call(
        body, name="ssd_conv_bwd", grid=(ncb,),
        in_specs=[pl.BlockSpec((S, CONV_CB), lambda j: (0, col0 + j)),
                  pl.BlockSpec((SSD_CONV, CONV_CB), lambda j: (0, j)),
                  pl.BlockSpec((1, CONV_CB), lambda j: (0, j)),
                  pl.BlockSpec((S, CONV_CB), lambda j: (0, j))],
        out_specs=[pl.BlockSpec((S, CONV_CB), lambda j: (0, j)),
                   pl.BlockSpec((SSD_CONV + 1, CONV_CB), lambda j: (0, j)),
                   pl.BlockSpec((1, CONV_CB), lambda j: (0, j))],
        out_shape=[jax.ShapeDtypeStruct((S, SSD_CONV_CH), F32),
                   jax.ShapeDtypeStruct((SSD_CONV + 1, SSD_CONV_CH), F32),
                   jax.ShapeDtypeStruct((1, SSD_CONV_CH), F32)],
        scratch_shapes=[pltpu.VMEM((S + 2 * CONV_PAD, CONV_CB), F32), pltpu.VMEM((S + 2 * CONV_PAD, CONV_CB), F32)],
        compiler_params=_cp("parallel"),
    )(u, conv_w, conv_b, dact)


def _ssd_group_layout(t):
    r = t.shape[0]
    g = t[:, :2 * SSD_HEADS].reshape(r, 2, SSD_GROUPS, SSD_HPG).transpose(2, 0, 1, 3).reshape(SSD_GROUPS, r, 2 * SSD_HPG)
    return jnp.pad(g, ((0, 0), (0, 0), (0, 128 - 2 * SSD_HPG)))


def _ssd_head_layout(t):
    r = t.shape[1]
    h = t[:, :, :2 * SSD_HPG].reshape(SSD_GROUPS, r, 2, SSD_HPG).transpose(1, 2, 0, 3).reshape(r, 2 * SSD_HEADS)
    return jnp.pad(h, ((0, 0), (0, 128 - 2 * SSD_HEADS)))


def _ssd_chunk(state, x, Bg, Cg, dt, alog, dtr, alr, *, reverse):
    Q, P = SSD_CHUNK, SSD_HEADDIM
    r = lax.broadcasted_iota(jnp.int32, (Q, Q), 0)
    c = lax.broadcasted_iota(jnp.int32, (Q, Q), 1)
    keep = (c >= r) if reverse else (c <= r)
    cum_t = jnp.transpose(_cumsum_rows(dt * (-jnp.exp(alog)), reverse))
    cum = _cumsum_rows(dtr * (-jnp.exp(alr)), reverse)
    last = 0 if reverse else Q - 1
    cum_l = cum[last:last + 1, :]
    CB = _dot(Cg, Bg, ((1,), (1,)))
    yoff = _dot(Cg, state, ((1,), (0,))) * jnp.exp(cum)
    xdt = x * dtr
    ys = []
    for h in range(SSD_HPG):
        col = h + (SSD_HPG if reverse else 0)
        hs = slice(h * P, (h + 1) * P)
        cum_q = jnp.concatenate([cum[:, hs]] * (Q // P), axis=1)
        L = jnp.where(keep, jnp.exp(jnp.where(keep, cum_q - cum_t[col:col + 1, :], 0.0)), 0.0)
        ys.append(_dot(CB * L, xdt[:, hs], ((1,), (0,))))
    new_state = jnp.exp(cum_l) * state + _dot(Bg, xdt * jnp.exp(cum_l - cum), ((0,), (0,)))
    return new_state, jnp.concatenate(ys, axis=1) + yoff


def ssd_scan_fwd(act, dt, alog, dtr, alr, *, reverse, y_prev=None, comm=None):
    S = act.shape[0]
    Q, N, P = SSD_CHUNK, SSD_STATE, SSD_HEADDIM
    nc = S // Q
    GW = SSD_HPG * P

    def cidx(i):
        return (nc - 1 - i) if reverse else i

    def body(*refs):
        if y_prev is None:
            x_ref, b_ref, c_ref, dt_ref, al_ref, dtr_ref, alr_ref, y_ref, st_ref, state = refs
            yp_ref = None
        else:
            x_ref, b_ref, c_ref, dt_ref, al_ref, dtr_ref, alr_ref, yp_ref, y_ref, st_ref, state = refs
        i = pl.program_id(1)

        @pl.when(i == 0)
        def _():
            state[...] = jnp.zeros_like(state)

        st = state[...]
        st_ref[0, 0] = st
        ns, y = _ssd_chunk(st, x_ref[...], b_ref[...], c_ref[...], dt_ref[0], al_ref[0], dtr_ref[...], alr_ref[...],
                           reverse=reverse)
        state[...] = ns
        y_ref[...] = y if yp_ref is None else y + yp_ref[...]

    xspec = pl.BlockSpec((Q, GW), lambda g, i: (cidx(i), g))
    in_specs = [xspec,
                pl.BlockSpec((Q, N), lambda g, i: (cidx(i), SSD_DI // N + g)),
                pl.BlockSpec((Q, N), lambda g, i: (cidx(i), SSD_DI // N + SSD_GROUPS + g)),
                pl.BlockSpec((1, Q, 128), lambda g, i: (g, cidx(i), 0)),
                pl.BlockSpec((1, 1, 128), lambda g, i: (g, 0, 0)),
                xspec, pl.BlockSpec((1, GW), lambda g, i: (0, g))]
    args = [act, act, act, dt, alog, dtr, alr]
    if y_prev is not None:
        in_specs.append(xspec)
        args.append(y_prev)
    return pcall(
        body, comm, name=f"ssd_scan_fwd_{int(reverse)}", grid=(SSD_GROUPS, nc), in_specs=in_specs,
        out_specs=[xspec, pl.BlockSpec((1, 1, N, GW), lambda g, i: (cidx(i), g, 0, 0))],
        out_shape=[jax.ShapeDtypeStruct((S, SSD_DI), F32), jax.ShapeDtypeStruct((nc, SSD_GROUPS, N, GW), F32)],
        scratch_shapes=[pltpu.VMEM((N, GW), F32)],
        compiler_params=_cp("arbitrary", "arbitrary"),
    )(*args)


def ssd_scan_bwd(act, dt, alog, dtr, alr, states, dy, prev_x, *, reverse, prev=None, comm=None):
    S = act.shape[0]
    Q, N, P = SSD_CHUNK, SSD_STATE, SSD_HEADDIM
    nc = S // Q
    GW = SSD_HPG * P

    def cidx(i):
        return i if reverse else (nc - 1 - i)

    def body(*refs):
        x_ref, b_ref, c_ref, dt_ref, al_ref, dtr_ref, alr_ref, st_ref, dy_ref, px_ref = refs[:10]
        pos = 10
        if prev is not None:
            pb_ref, pc_ref, pdt_ref, pal_ref = refs[pos:pos + 4]
            pos += 4
        dx_ref, db_ref, dc_ref, ddt_ref, dal_ref, ddtr_ref, dalr_ref, dstate = refs[pos:]
        i = pl.program_id(1)

        @pl.when(i == 0)
        def _():
            dstate[...] = jnp.zeros_like(dstate)

        _, vjp = jax.vjp(functools.partial(_ssd_chunk, reverse=reverse), st_ref[0, 0], x_ref[...], b_ref[...],
                         c_ref[...], dt_ref[0], al_ref[0], dtr_ref[...], alr_ref[...])
        dst, dx, dB, dC, ddt, dal, ddtr, dalr = vjp((dstate[...], dy_ref[...]))
        dstate[...] = dst
        dx_ref[...] = dx + px_ref[...]
        if prev is not None:
            dB = dB + pb_ref[...]
            dC = dC + pc_ref[...]
            ddt = ddt + pdt_ref[0]
        db_ref[...] = dB
        dc_ref[...] = dC
        ddt_ref[0] = ddt
        ddtr_ref[...] = ddtr

        @pl.when(i == 0)
        def _():
            dal_ref[0] = dal + (pal_ref[0] if prev is not None else 0.0)
            dalr_ref[...] = dalr

        @pl.when(i != 0)
        def _():
            dal_ref[0] += dal
            dalr_ref[...] += dalr

    xspec = pl.BlockSpec((Q, GW), lambda g, i: (cidx(i), g))
    gspec = pl.BlockSpec((Q, N), lambda g, i: (cidx(i), g))
    dtspec = pl.BlockSpec((1, Q, 128), lambda g, i: (g, cidx(i), 0))
    alspec = pl.BlockSpec((1, 1, 128), lambda g, i: (g, 0, 0))
    alrspec = pl.BlockSpec((1, GW), lambda g, i: (0, g))
    in_specs = [xspec,
                pl.BlockSpec((Q, N), lambda g, i: (cidx(i), SSD_DI // N + g)),
                pl.BlockSpec((Q, N), lambda g, i: (cidx(i), SSD_DI // N + SSD_GROUPS + g)),
                dtspec, alspec, xspec, alrspec,
                pl.BlockSpec((1, 1, N, GW), lambda g, i: (cidx(i), g, 0, 0)), xspec, xspec]
    args = [act, act, act, dt, alog, dtr, alr, states, dy, prev_x]
    if prev is not None:
        in_specs += [gspec, gspec, dtspec, alspec]
        args += list(prev)
    return pcall(
        body, comm, name=f"ssd_scan_bwd_{int(reverse)}", grid=(SSD_GROUPS, nc), in_specs=in_specs,
        out_specs=[xspec, gspec, gspec, dtspec, alspec, xspec, alrspec],
        out_shape=[jax.ShapeDtypeStruct((S, SSD_DI), F32), jax.ShapeDtypeStruct((S, SSD_GROUPS * N), F32),
                   jax.ShapeDtypeStruct((S, SSD_GROUPS * N), F32),
                   jax.ShapeDtypeStruct((SSD_GROUPS, S, 128), F32), jax.ShapeDtypeStruct((SSD_GROUPS, 1, 128), F32),
                   jax.ShapeDtypeStruct((S, SSD_DI), F32), jax.ShapeDtypeStruct((1, SSD_DI), F32)],
        scratch_shapes=[pltpu.VMEM((N, GW), F32)],
        compiler_params=_cp("arbitrary", "arbitrary"),
    )(*args)


def _ssd_post_fn(y, xs, z, dexp, ng):
    t = (y + xs * dexp) * _silu(z)
    return (_rms(t, ng),)


def _cumsum_rows_impl(x, reverse):
    n = x.shape[0]
    row = lax.broadcasted_iota(jnp.int32, x.shape, 0)
    k = 1
    while k < n:
        if reverse:
            x = x + jnp.where(row < n - k, pltpu.roll(x, n - k, 0), 0.0)
        else:
            x = x + jnp.where(row >= k, pltpu.roll(x, k, 0), 0.0)
        k *= 2
    return x


@functools.partial(jax.custom_vjp, nondiff_argnums=(1,))
def _cumsum_rows(x, reverse):
    return _cumsum_rows_impl(x, reverse)


_cumsum_rows.defvjp(lambda x, reverse: (_cumsum_rows_impl(x, reverse), None),
                    lambda reverse, _, ct: (_cumsum_rows_impl(ct, not reverse),))


def _hg_chunk(state, qraw, fraw, v, lb, *, reverse):
    C = HG_CHUNK
    r = lax.broadcasted_iota(jnp.int32, (C, C), 0)
    c = lax.broadcasted_iota(jnp.int32, (C, C), 1)
    keep = (c >= r) if reverse else (c <= r)
    q = _silu(qraw)
    f = lb + (1.0 - lb) * jax.nn.sigmoid(fraw)
    k = 1.0 - f
    g = jnp.log(f)
    G = _cumsum_rows(g, reverse)
    ref_row = C // 2 - 1 if reverse else C // 2
    last_row = 0 if reverse else C - 1
    Gr = G[ref_row:ref_row + 1, :]
    Gl = G[last_row:last_row + 1, :]
    q_t = q * jnp.exp(G - Gr)
    k_t = k * jnp.exp(Gr - G)
    att = jnp.where(keep, _dot(q_t, k_t, ((1,), (1,))), 0.0)
    o = _dot(att, v, ((1,), (0,))) + _dot(q * jnp.exp(G), state, ((1,), (0,)))
    kd = k * jnp.exp(Gl - G)
    new_state = jnp.transpose(jnp.exp(Gl)) * state + _dot(kd, v, ((0,), (0,)))
    return new_state, o


def hg_scan_fwd(u, lb, *, reverse, o_prev=None, rows=256, comm=None):
    S = u.shape[0]
    nh = HG_HEADS
    rows = min(rows, S)
    nsteps = S // rows
    ncb = rows // HG_CHUNK
    f_sec = 2 if reverse else 1

    def blk(i):
        return (nsteps - 1 - i) if reverse else i

    def body(*refs):
        if o_prev is None:
            q_ref, f_ref, v_ref, lb_ref, o_ref, st_ref, state = refs
            op_ref = None
        else:
            q_ref, f_ref, v_ref, lb_ref, op_ref, o_ref, st_ref, state = refs
        i = pl.program_id(0)

        @pl.when(i == 0)
        def _():
            state[...] = jnp.zeros_like(state)

        def chunk(cc, carry):
            ci = (ncb - 1 - cc) if reverse else cc
            sl = pl.ds(pl.multiple_of(ci * HG_CHUNK, HG_CHUNK), HG_CHUNK)
            for h in range(nh):
                hs = slice(h * HG_D, (h + 1) * HG_D)
                st = state[h]
                st_ref[ci, h] = st
                ns, o = _hg_chunk(st, q_ref[sl, hs], f_ref[sl, hs], v_ref[sl, hs], lb_ref[:, hs], reverse=reverse)
                state[h] = ns
                if op_ref is not None:
                    o = o + op_ref[sl, hs]
                o_ref[sl, hs] = o
            return carry

        lax.fori_loop(0, ncb, chunk, 0)

    rowspec = lambda sec: pl.BlockSpec((rows, HG_W), lambda i: (blk(i), sec))
    in_specs = [rowspec(0), rowspec(f_sec), rowspec(3), pl.BlockSpec((1, HG_W), lambda i: (0, 0))]
    args = [u, u, u, lb]
    if o_prev is not None:
        in_specs.append(rowspec(0))
        args.append(o_prev)
    return pcall(
        body, comm, name=f"hg_scan_fwd_{int(reverse)}", grid=(nsteps,), in_specs=in_specs,
        out_specs=[rowspec(0), pl.BlockSpec((ncb, nh, HG_D, HG_D), lambda i: (blk(i), 0, 0, 0))],
        out_shape=[jax.ShapeDtypeStruct((S, HG_W), F32), jax.ShapeDtypeStruct((S // HG_CHUNK, nh, HG_D, HG_D), F32)],
        scratch_shapes=[pltpu.VMEM((nh, HG_D, HG_D), F32)],
        compiler_params=_cp("arbitrary"),
    )(*args)


def hg_scan_bwd(u, lb, states, do, *, reverse, prev=None, rows=256, comm=None):
    S = u.shape[0]
    nh = HG_HEADS
    rows = min(rows, S)
    nsteps = S // rows
    ncb = rows // HG_CHUNK
    f_sec = 2 if reverse else 1

    def blk(i):
        return i if reverse else (nsteps - 1 - i)

    def body(*refs):
        q_ref, f_ref, v_ref, lb_ref, st_ref, do_ref = refs[:6]
        pos = 6
        if prev is not None:
            pq_ref, pv_ref, plb_ref = refs[pos:pos + 3]
            pos += 3
        dq_ref, df_ref, dv_ref, dlb_ref, dstate = refs[pos:]
        i = pl.program_id(0)

        @pl.when(i == 0)
        def _():
            dstate[...] = jnp.zeros_like(dstate)
            dlb_ref[...] = plb_ref[...] if prev is not None else jnp.zeros_like(dlb_ref)

        def chunk(cc, carry):
            ci = cc if reverse else (ncb - 1 - cc)
            sl = pl.ds(pl.multiple_of(ci * HG_CHUNK, HG_CHUNK), HG_CHUNK)
            for h in range(nh):
                hs = slice(h * HG_D, (h + 1) * HG_D)
                _, vjp = jax.vjp(functools.partial(_hg_chunk, reverse=reverse), st_ref[ci, h],
                                 q_ref[sl, hs], f_ref[sl, hs], v_ref[sl, hs], lb_ref[:, hs])
                dst, dq, df, dv, dlb = vjp((dstate[h], do_ref[sl, hs]))
                dstate[h] = dst
                if prev is not None:
                    dq = dq + pq_ref[sl, hs]
                    dv = dv + pv_ref[sl, hs]
                dq_ref[sl, hs] = dq
                df_ref[sl, hs] = df
                dv_ref[sl, hs] = dv
                dlb_ref[:, hs] += dlb
            return carry

        lax.fori_loop(0, ncb, chunk, 0)

    rowspec = lambda sec: pl.BlockSpec((rows, HG_W), lambda i: (blk(i), sec))
    lbspec = pl.BlockSpec((1, HG_W), lambda i: (0, 0))
    in_specs = [rowspec(0), rowspec(f_sec), rowspec(3), lbspec,
                pl.BlockSpec((ncb, nh, HG_D, HG_D), lambda i: (blk(i), 0, 0, 0)), rowspec(0)]
    args = [u, u, u, lb, states, do]
    if prev is not None:
        in_specs += [rowspec(0), rowspec(0), lbspec]
        args += list(prev)
    return pcall(
        body, comm, name=f"hg_scan_bwd_{int(reverse)}", grid=(nsteps,), in_specs=in_specs,
        out_specs=[rowspec(0), rowspec(0), rowspec(0), lbspec],
        out_shape=[jax.ShapeDtypeStruct((S, HG_W), F32)] * 3 + [jax.ShapeDtypeStruct((1, HG_W), F32)],
        scratch_shapes=[pltpu.VMEM((nh, HG_D, HG_D), F32)],
        compiler_params=_cp("arbitrary"),
    )(*args)


def _hg_lb_fn(lbp):
    m = jnp.max(lbp, axis=0, keepdims=True)
    e = jnp.exp(lbp - m)
    sm = e / jnp.sum(e, axis=0, keepdims=True)
    return ((sm[0:1] + sm[1:2]) - sm[0:1],)


def hg_lb_fwd(lbp):
    def body(x_ref, o_ref):
        o_ref[...] = _hg_lb_fn(x_ref[...])[0]

    return pl.pallas_call(body, name="hg_lb_fwd", out_shape=jax.ShapeDtypeStruct((1, HG_W), F32))(lbp)


def hg_lb_bwd(lbp, dlb):
    def body(x_ref, d_ref, o_ref):
        _, vjp = jax.vjp(_hg_lb_fn, x_ref[...])
        o_ref[...] = vjp((d_ref[...],))[0]

    return pl.pallas_call(body, name="hg_lb_bwd", out_shape=jax.ShapeDtypeStruct(lbp.shape, F32))(lbp, dlb)


def _hg_post_fn(o, gate, ng):
    return (_rms(o, ng) * _silu(gate),)


def _gate_fn(o, gate):
    return (o * _silu(gate),)


def _rope_tables(S):
    t = np.arange(S)
    row = (t // GRID_W).astype(np.float32)
    col = (t % GRID_W).astype(np.float32)
    half = AT_HD // 4
    inv = (ROPE_THETA ** (-np.arange(0, 2 * half, 2, dtype=np.float32) / np.float32(2 * half))).astype(np.float32)
    ar = row[:, None] * inv[None, :]
    ac = col[:, None] * inv[None, :]
    return ar.astype(np.float32), ac.astype(np.float32)


@jax.custom_vjp
def _half_swap(x):
    ax = x.ndim - 1
    lane = lax.broadcasted_iota(jnp.int32, x.shape, ax)
    return jnp.where((lane & 32) == 0, pltpu.roll(x, 96, ax), pltpu.roll(x, 32, ax))


_half_swap.defvjp(lambda x: (_half_swap(x), None), lambda _, ct: (_half_swap(ct),))


def _make_qk_fn(scale):
    def fn(x, ct, st, g):
        n = _rms(x, g)
        return ((n * ct + _half_swap(n) * st) * scale,)
    return fn


def flash_fwd(q, k, v, *, v_col0=0, tq=256, comm=None):
    S = q.shape[0]
    tq = min(tq, S)
    G = AT_HEADS // AT_KV

    def body(q_ref, k_ref, v_ref, o_ref, lse_ref):
        kv, vv = k_ref[...], v_ref[...]
        for g in range(G):
            sl = slice(g * AT_HD, (g + 1) * AT_HD)
            s = _dot(q_ref[:, sl], kv, ((1,), (1,)))
            m = jnp.max(s, axis=1, keepdims=True)
            p = jnp.exp(s - m)
            l = jnp.sum(p, axis=1, keepdims=True)
            o_ref[:, sl] = _dot(p, vv, ((1,), (0,))) / l
            lse_ref[0, :, g:g + 1] = m + jnp.log(l)

    return pcall(
        body, comm, name="flash_fwd", grid=(AT_KV, S // tq),
        in_specs=[pl.BlockSpec((tq, G * AT_HD), lambda h, i: (i, h)),
                  pl.BlockSpec((S, AT_HD), lambda h, i: (0, h)),
                  pl.BlockSpec((S, AT_HD), lambda h, i: (0, v_col0 + h))],
        out_specs=[pl.BlockSpec((tq, G * AT_HD), lambda h, i: (i, h)),
                   pl.BlockSpec((1, tq, G), lambda h, i: (h, i, 0))],
        out_shape=[jax.ShapeDtypeStruct((S, AT_QW), F32), jax.ShapeDtypeStruct((AT_KV, S, G), F32)],
        compiler_params=_cp("parallel", "arbitrary"),
    )(q, k, v)


def flash_bwd_dq(q, k, v, o, lse, do, *, v_col0=0, tq=256):
    S = q.shape[0]
    tq = min(tq, S)
    G = AT_HEADS // AT_KV

    def body(q_ref, k_ref, v_ref, o_ref, lse_ref, do_ref, dq_ref, dl_ref):
        kv, vv = k_ref[...], v_ref[...]
        for g in range(G):
            sl = slice(g * AT_HD, (g + 1) * AT_HD)
            dog = do_ref[:, sl]
            delta = jnp.sum(dog * o_ref[:, sl], axis=1, keepdims=True)
            s = _dot(q_ref[:, sl], kv, ((1,), (1,)))
            p = jnp.exp(s - lse_ref[0, :, g:g + 1])
            dp = _dot(dog, vv, ((1,), (1,)))
            ds = p * (dp - delta)
            dq_ref[:, sl] = _dot(ds, kv, ((1,), (0,)))
            dl_ref[0, :, g:g + 1] = delta

    qspec = pl.BlockSpec((tq, G * AT_HD), lambda h, i: (i, h))
    lspec = pl.BlockSpec((1, tq, G), lambda h, i: (h, i, 0))
    return pl.pallas_call(
        body, name="flash_bwd_dq", grid=(AT_KV, S // tq),
        in_specs=[qspec, pl.BlockSpec((S, AT_HD), lambda h, i: (0, h)),
                  pl.BlockSpec((S, AT_HD), lambda h, i: (0, v_col0 + h)), qspec, lspec, qspec],
        out_specs=[qspec, lspec],
        out_shape=[jax.ShapeDtypeStruct((S, AT_QW), F32), jax.ShapeDtypeStruct((AT_KV, S, G), F32)],
        compiler_params=_cp("parallel", "arbitrary"),
    )(q, k, v, o, lse, do)


def flash_bwd_dkv(q, k, v, lse, delta, do, *, v_col0=0, tk=512, comm=None):
    S = q.shape[0]
    tk = min(tk, S)
    G = AT_HEADS // AT_KV

    def body(q_ref, k_ref, v_ref, lse_ref, dl_ref, do_ref, dk_ref, dv_ref):
        kv, vv = k_ref[...], v_ref[...]
        dk = jnp.zeros((tk, AT_HD), F32)
        dv = jnp.zeros((tk, AT_HD), F32)
        for g in range(G):
            sl = slice(g * AT_HD, (g + 1) * AT_HD)
            qg, dog = q_ref[:, sl], do_ref[:, sl]
            s = _dot(qg, kv, ((1,), (1,)))
            p = jnp.exp(s - lse_ref[0, :, g:g + 1])
            dv = dv + _dot(p, dog, ((0,), (0,)))
            dp = _dot(dog, vv, ((1,), (1,)))
            ds = p * (dp - dl_ref[0, :, g:g + 1])
            dk = dk + _dot(ds, qg, ((0,), (0,)))
        dk_ref[...] = dk
        dv_ref[...] = dv

    qspec = pl.BlockSpec((S, G * AT_HD), lambda h, j: (0, h))
    kspec = pl.BlockSpec((tk, AT_HD), lambda h, j: (j, h))
    lspec = pl.BlockSpec((1, S, G), lambda h, j: (h, 0, 0))
    return pcall(
        body, comm, name="flash_bwd_dkv", grid=(AT_KV, S // tk),
        in_specs=[qspec, kspec, pl.BlockSpec((tk, AT_HD), lambda h, j: (j, v_col0 + h)), lspec, lspec, qspec],
        out_specs=[kspec, kspec],
        out_shape=[jax.ShapeDtypeStruct((S, AT_KW), F32), jax.ShapeDtypeStruct((S, AT_KW), F32)],
        compiler_params=_cp("parallel", "arbitrary"),
    )(q, k, v, lse, delta, do)


def _t5_bucket_np(rel):
    half = REL_BUCKETS // 2
    exact = half // 2
    n = np.abs(rel)
    large = exact + (np.log(np.maximum(n, 1).astype(np.float32) / np.float32(exact))
                     / np.float32(math.log(REL_MAX_DIST / exact)) * np.float32(half - exact)).astype(np.int32)
    large = np.minimum(large, half - 1)
    return np.where(rel > 0, half, 0) + np.where(n < exact, n, large)


DL_TB = 256


def _dl_tiles(Ls, T=128):
    T = min(T, Ls)
    return T, T + 2 * DL_HALF


def _dl_bucket_tables(dil, T):
    W = T + 2 * DL_HALF
    i = np.arange(T)[:, None]
    j = np.arange(W)[None, :]
    bq = _t5_bucket_np((j - DL_HALF - i) * dil)
    iw = np.arange(W)[:, None]
    jk = np.arange(T)[None, :]
    bk = _t5_bucket_np((jk + DL_HALF - iw) * dil)
    return bq.astype(np.int32), bk.astype(np.int32)


def _dl_merge_fn(o0, o1, o2, l0, l1, l2):
    m = jnp.maximum(jnp.maximum(l0, l1), l2)
    e0, e1, e2 = jnp.exp(l0 - m), jnp.exp(l1 - m), jnp.exp(l2 - m)
    den = e0 + e1 + e2
    return ((e0 / den) * o0 + (e1 / den) * o1 + (e2 / den) * o2,)


def _adamw_math(w, g, m, v):
    m = ADAM_B1 * m + (1.0 - ADAM_B1) * g
    v = ADAM_B2 * v + (1.0 - ADAM_B2) * (g * g)
    m_hat = m / (1.0 - ADAM_B1 ** ADAM_STEP)
    v_hat = v / (1.0 - ADAM_B2 ** ADAM_STEP)
    delta = -ADAM_LR * (m_hat / (jnp.sqrt(v_hat) + ADAM_EPS) + ADAM_WD * w)
    return delta, m, v


def adamw_sum(parts, w, m, v, *, name, R=128):
    rows, cols = w.shape
    R = min(R, rows)
    if rows % R:
        R = rows

    def body(p_ref, w_ref, m_ref, v_ref, g_ref, d_ref, nm_ref, nv_ref):
        g = p_ref[0].astype(F32)
        for s in range(1, N_DEV):
            g = g + p_ref[s].astype(F32)
        d, nm, nv = _adamw_math(w_ref[...], g, m_ref[...], v_ref[...])
        g_ref[...] = g
        d_ref[...] = d
        nm_ref[...] = nm
        nv_ref[...] = nv

    spec = pl.BlockSpec((R, cols), lambda i: (i, 0))
    return pl.pallas_call(
        body, name=name, grid=(rows // R,),
        in_specs=[pl.BlockSpec((N_DEV, R, cols), lambda i: (0, i, 0)), spec, spec, spec],
        out_specs=[spec] * 4, out_shape=[jax.ShapeDtypeStruct((rows, cols), F32)] * 4,
        compiler_params=_cp("parallel"),
    )(parts, w, m, v)


def sum_parts(parts, *, name):
    rows, cols = parts.shape[1:]

    def body(p_ref, o_ref):
        g = p_ref[0]
        for s in range(1, N_DEV):
            g = g + p_ref[s]
        o_ref[...] = g

    return pl.pallas_call(body, name=name, out_shape=jax.ShapeDtypeStruct((rows, cols), F32))(parts)


def adamw_plain(w, g, m, v, *, name):
    def body(w_ref, g_ref, m_ref, v_ref, d_ref, nm_ref, nv_ref):
        d, nm, nv = _adamw_math(w_ref[...], g_ref[...], m_ref[...], v_ref[...])
        d_ref[...] = d
        nm_ref[...] = nm
        nv_ref[...] = nv

    return pl.pallas_call(body, name=name, out_shape=[jax.ShapeDtypeStruct(w.shape, F32)] * 3)(w, g, m, v)


def allgather_two_level(x, *, name):
    R, C = x.shape

    def body(x_ref, out_ref, send_sems, recv_sems, local_sem):
        x_, y_, c_ = _my_pos()
        me, sibling = (x_, y_, c_), (x_, y_, 1 - c_)
        chips = [(1 - x_, y_), (x_, 1 - y_), (1 - x_, 1 - y_)]

        def rows(p):
            return out_ref.at[_flat(*p)]

        def copy(k, block, to, src=None):
            return pltpu.make_async_remote_copy(
                src_ref=rows(block) if src is None else src, dst_ref=rows(block),
                send_sem=send_sems.at[k], recv_sem=recv_sems.at[k], device_id=to, device_id_type=MESH_ID)

        mine = pltpu.make_async_copy(x_ref, rows(me), local_sem)
        mine.start()
        first = [copy(0, me, sibling, src=x_ref)]
        first += [copy(1 + j, me, (*chip, c_), src=x_ref) for j, chip in enumerate(chips)]
        for cp in first:
            cp.start()
        passed = [copy(4 + j, (*chip, c_), sibling) for j, chip in enumerate(chips)]
        for j, chip in enumerate(chips):
            copy(1 + j, (*chip, c_), me).wait_recv()
            passed[j].start()
        copy(0, sibling, me).wait_recv()
        for j, chip in enumerate(chips):
            copy(4 + j, (*chip, 1 - c_), me).wait_recv()
        for cp in first + passed:
            cp.wait_send()
        mine.wait()

    return pl.pallas_call(
        body, name=name,
        out_shape=jax.ShapeDtypeStruct((N_DEV, R, C), x.dtype),
        in_specs=[pl.BlockSpec(memory_space=pl.ANY)],
        out_specs=pl.BlockSpec(memory_space=pl.ANY),
        scratch_shapes=[pltpu.SemaphoreType.DMA((7,)), pltpu.SemaphoreType.DMA((7,)), pltpu.SemaphoreType.DMA],
    )(x)


def allgather_two_level_multi(xs, *, name):
    nb = len(xs)

    def body(*refs):
        x_refs, out_refs = refs[:nb], refs[nb:2 * nb]
        send_sems, recv_sems, local_sems = refs[2 * nb:]
        x_, y_, c_ = _my_pos()
        me, sibling = (x_, y_, c_), (x_, y_, 1 - c_)
        chips = [(1 - x_, y_), (x_, 1 - y_), (1 - x_, 1 - y_)]

        def copy(b, k, block, to, own=False):
            rows = out_refs[b].at[_flat(*block)]
            return pltpu.make_async_remote_copy(
                src_ref=x_refs[b] if own else rows, dst_ref=rows,
                send_sem=send_sems.at[b, k], recv_sem=recv_sems.at[b, k], device_id=to, device_id_type=MESH_ID)

        mine = [pltpu.make_async_copy(x_refs[b], out_refs[b].at[_flat(*me)], local_sems.at[b]) for b in range(nb)]
        first = []
        for b in range(nb):
            first.append(copy(b, 0, me, sibling, own=True))
            first += [copy(b, 1 + j, me, (*chip, c_), own=True) for j, chip in enumerate(chips)]
        for cp in mine + first:
            cp.start()
        passed = []
        for j, chip in enumerate(chips):
            for b in range(nb):
                copy(b, 1 + j, (*chip, c_), me).wait_recv()
                fwd = copy(b, 4 + j, (*chip, c_), sibling)
                fwd.start()
                passed.append(fwd)
        for b in range(nb):
            copy(b, 0, sibling, me).wait_recv()
            for j, chip in enumerate(chips):
                copy(b, 4 + j, (*chip, 1 - c_), me).wait_recv()
        for cp in first + passed:
            cp.wait_send()
        for cp in mine:
            cp.wait()

    anyspec = pl.BlockSpec(memory_space=pl.ANY)
    return pl.pallas_call(
        body, name=name,
        out_shape=[jax.ShapeDtypeStruct((N_DEV,) + x.shape, x.dtype) for x in xs],
        in_specs=[anyspec] * nb, out_specs=[anyspec] * nb,
        scratch_shapes=[pltpu.SemaphoreType.DMA((nb, 7)), pltpu.SemaphoreType.DMA((nb, 7)), pltpu.SemaphoreType.DMA((nb,))],
    )(*xs)


def _prenorm(tag, x, ng):
    return rowwise_fwd(f"{tag}_prenorm", _prenorm_fn, [(x, 0, False)], [(ng, False)], [(D_MODEL, MXU_DTYPE)], W=D_MODEL)[0]


def _cat_mxu(parts):
    return jnp.concatenate([t.astype(MXU_DTYPE) for t in parts], axis=1)


def _in_out_bwd(tag, x, ng, hn, du, w_in, dx, tail_comm=None):
    dw_in = matmul(hn, du, ta=True, out_dtype=GRAD_WIRE_DTYPE, name=f"{tag}_dw_in")
    comm = None if tail_comm is None else tail_comm(dw_in)
    dhn, cres = _own(matmul(du, w_in, tb=True, name=f"{tag}_dhn", comm=comm), comm)
    dx_prev, dng = rowwise_bwd(f"{tag}_prenorm_bwd", _prenorm_fn, [(x, 0, False)], [(ng, False)], [dhn],
                               W=D_MODEL, diff_rows=[0], diff_shared=[0], add=dx)
    return (dx_prev, dng, dw_in) if tail_comm is None else (dx_prev, dng, dw_in, cres)


def _own(res, comm):
    return (res, None) if comm is None else res


def ssd_layer_fwd(x, ng, p, comm=None, comm1=None):
    hn = _prenorm("ssd", x, ng)
    u = matmul(hn, p["w_in"], name="ssd_in")
    act = ssd_conv_fwd(u, p["conv_w"], p["conv_b"])
    dt = rowwise_fwd("ssd_dt", _dt_fn, [(u, (SSD_DI + SSD_CONV_CH) // 128, False)], [(p["dt_bias"], False)],
                     [(128, F32)], W=128)[0]
    H = SSD_HEADS
    dtr = [jnp.repeat(dt[:, d * H:(d + 1) * H], SSD_HEADDIM, axis=1) for d in (0, 1)]
    alr = [jnp.repeat(p["alog"][:, d * H:(d + 1) * H], SSD_HEADDIM, axis=1) for d in (0, 1)]
    dt, alog = _ssd_group_layout(dt), _ssd_group_layout(p["alog"])
    (y0, st0), cres = _own(ssd_scan_fwd(act, dt, alog, dtr[0], alr[0], reverse=False, comm=comm), comm)
    (y, st1), cres1 = _own(ssd_scan_fwd(act, dt, alog, dtr[1], alr[1], reverse=True, y_prev=y0, comm=comm1), comm1)
    g = rowwise_fwd("ssd_post", _ssd_post_fn, [(y, 0, True), (act, 0, True), (u, 0, True)],
                    [(p["dexp"], True), (p["norm_g"], True)], [(512, MXU_DTYPE)], W=512, ncb=SSD_GROUPS)[0]
    xn = matmul(g, p["w_out"], residual=x, name="ssd_out")
    return xn, dict(x=x, ng=ng, hn=hn, u=u, act=act, dt=dt, alog=alog, dtr=dtr, alr=alr, y=y, st0=st0, st1=st1,
                    g=g), cres, cres1


def ssd_layer_bwd(sv, p, dx, comm=None, tail_comm=None):
    u, act, dt = sv["u"], sv["act"], sv["dt"]
    S = u.shape[0]
    dg = matmul(dx, p["w_out"], tb=True, name="ssd_dg")
    dw_out = matmul(sv["g"], dx, ta=True, out_dtype=GRAD_WIRE_DTYPE, name="ssd_dw_out")
    dy, dxs_skip, dz, ddexp, dnorm = rowwise_bwd(
        "ssd_post_bwd", _ssd_post_fn, [(sv["y"], 0, True), (act, 0, True), (u, 0, True)],
        [(p["dexp"], True), (p["norm_g"], True)], [dg], W=512, ncb=SSD_GROUPS, diff_rows=[0, 1, 2], diff_shared=[0, 1])
    dtr, alr = sv["dtr"], sv["alr"]
    (dxa, dB, dC, ddt, dal, ddtr0, dalr0), cres = _own(
        ssd_scan_bwd(act, dt, sv["alog"], dtr[0], alr[0], sv["st0"], dy, dxs_skip, reverse=False, comm=comm), comm)
    comm1 = None if tail_comm is None else tail_comm(None, dw_out)
    (dxa, dB, dC, ddt, dal, ddtr1, dalr1), tail_out = _own(
        ssd_scan_bwd(act, dt, sv["alog"], dtr[1], alr[1], sv["st1"], dy, dxa, reverse=True, prev=(dB, dC, ddt, dal),
                     comm=comm1), comm1)
    dact = jnp.concatenate([dxa, dB, dC], axis=1)
    dxbc, dconv_w, dconv_b = ssd_conv_bwd(u, p["conv_w"], p["conv_b"], dact)
    fold = jnp.asarray(np.repeat(np.eye(SSD_HEADS, dtype=np.float32), SSD_HEADDIM, axis=0))
    folded = [matmul(t, fold, exact=True, name=f"ssd_ddt_fold_{d}", tn=SSD_HEADS) for d, t in enumerate((ddtr0, ddtr1))]
    ddt_all = _ssd_head_layout(ddt) + jnp.pad(jnp.concatenate(folded, axis=1), ((0, 0), (0, 128 - 2 * SSD_HEADS)))
    dal_rep = jnp.concatenate([t.reshape(SSD_HEADS, SSD_HEADDIM).sum(axis=1) for t in (dalr0, dalr1)])[None, :]
    ddt_raw, ddt_bias = rowwise_bwd("ssd_dt_bwd", _dt_fn, [(u, (SSD_DI + SSD_CONV_CH) // 128, False)],
                                    [(p["dt_bias"], False)], [ddt_all], W=128, diff_rows=[0], diff_shared=[0])
    du = _cat_mxu([dz, dxbc, ddt_raw, jnp.zeros((S, SSD_IN_PAD - SSD_IN - 64), F32)])
    res = _in_out_bwd("ssd", sv["x"], sv["ng"], sv["hn"], du, p["w_in"], dx,
                      tail_comm=None if tail_comm is None else (lambda g_in: tail_comm(g_in[:, :SSD_IN], None)))
    dx_prev, dng, dw_in = res[:3]
    tail = res[3] + tail_out if tail_comm is not None else None
    grads = dict(
        w_in=dw_in[:, :SSD_IN], w_out=dw_out, conv_w=dconv_w[:SSD_CONV], conv_b=dconv_b,
        dt_bias=ddt_bias[:, :2 * SSD_HEADS], a_log=_ssd_head_layout(dal)[:, :2 * SSD_HEADS] + dal_rep,
        d=ddexp.reshape(SSD_HEADS, SSD_HEADDIM).sum(axis=1)[None, :], norm_g=dnorm, ng=dng)
    return dx_prev, grads, cres, tail


def hg_layer_fwd(x, ng, p, comm0=None, comm1=None):
    hn = _prenorm("hg", x, ng)
    u = matmul(hn, p["w_in"], name="hg_in")
    lb = hg_lb_fwd(p["hgrn_lb"])
    (o0, st0), cres0 = _own(hg_scan_fwd(u, lb, reverse=False, comm=comm0), comm0)
    (o, st1), cres1 = _own(hg_scan_fwd(u, lb, reverse=True, o_prev=o0, comm=comm1), comm1)
    g = rowwise_fwd("hg_post", _hg_post_fn, [(o, 0, True), (u, 4 * HG_HEADS, True)], [(p["norm_g"], True)],
                    [(HG_D, MXU_DTYPE)], W=HG_D, ncb=HG_HEADS)[0]
    xn = matmul(g, p["w_out"], residual=x, name="hg_out")
    return xn, dict(x=x, ng=ng, hn=hn, u=u, lb=lb, o=o, st0=st0, st1=st1, g=g), cres0, cres1


def hg_layer_bwd(sv, p, dx, comm=None):
    u, lb = sv["u"], sv["lb"]
    dg = matmul(dx, p["w_out"], tb=True, name="hg_dg")
    dw_out = matmul(sv["g"], dx, ta=True, out_dtype=GRAD_WIRE_DTYPE, name="hg_dw_out")
    do, dgate, dnorm = rowwise_bwd("hg_post_bwd", _hg_post_fn, [(sv["o"], 0, True), (u, 4 * HG_HEADS, True)],
                                   [(p["norm_g"], True)], [dg], W=HG_D, ncb=HG_HEADS, diff_rows=[0, 1], diff_shared=[0])
    (dq0, df0, dv0, dlb0), cres = _own(hg_scan_bwd(u, lb, sv["st0"], do, reverse=False, comm=comm), comm)
    dq, df1, dv, dlb = hg_scan_bwd(u, lb, sv["st1"], do, reverse=True, prev=(dq0, dv0, dlb0))
    du = _cat_mxu([dq, df0, df1, dv, dgate])
    dhgrn_lb = hg_lb_bwd(p["hgrn_lb"], dlb)
    dx_prev, dng, dw_in = _in_out_bwd("hg", sv["x"], sv["ng"], sv["hn"], du, p["w_in"], dx)
    return dx_prev, dict(w_in=dw_in, w_out=dw_out, norm_g=dnorm, hgrn_lb=dhgrn_lb, ng=dng), cres


def _rope_consts(S):
    ar, ac = _rope_tables(S)
    ct = np.concatenate([np.cos(ar), np.cos(ar), np.cos(ac), np.cos(ac)], axis=1).astype(np.float32)
    st = np.concatenate([-np.sin(ar), np.sin(ar), -np.sin(ac), np.sin(ac)], axis=1).astype(np.float32)
    return jnp.asarray(ct), jnp.asarray(st)


def _at_qk(tag, u, col0, nheads, scale, gain, consts, cot=None):
    ct, st = consts
    rows = [(u, col0, True), (ct, 0, False), (st, 0, False)]
    shared = [(gain, False)]
    if cot is None:
        return rowwise_fwd(f"at_{tag}", _make_qk_fn(scale), rows, shared, [(AT_HD, MXU_DTYPE)], W=AT_HD, ncb=nheads)[0]
    return rowwise_bwd(f"at_{tag}_bwd", _make_qk_fn(scale), rows, shared, [cot], W=AT_HD, ncb=nheads,
                       diff_rows=[0], diff_shared=[0])


def at_layer_fwd(x, ng, p, comm=None):
    S = x.shape[0]
    hn = _prenorm("at", x, ng)
    u = matmul(hn, p["w_in"], name="at_in")
    consts = _rope_consts(S)
    qr = _at_qk("q", u, 0, AT_HEADS, AT_HD ** -0.5, p["q_g"], consts)
    kr = _at_qk("k", u, AT_HEADS, AT_KV, 1.0, p["k_g"], consts)
    vc0 = (AT_QW + AT_KW) // AT_HD
    (o, lse), cres = _own(flash_fwd(qr, kr, u, v_col0=vc0, comm=comm), comm)
    g = rowwise_fwd("at_gate", _gate_fn, [(o, 0, True), (u, (AT_QW + 2 * AT_KW) // 1024, True)], [],
                    [(1024, MXU_DTYPE)], W=1024, ncb=AT_QW // 1024)[0]
    xn = matmul(g, p["w_out"], residual=x, name="at_out")
    return xn, dict(x=x, ng=ng, hn=hn, u=u, qr=qr, kr=kr, o=o, lse=lse, g=g), cres


def at_layer_bwd(sv, p, dx, comm=None):
    u, qr, kr = sv["u"], sv["qr"], sv["kr"]
    S = u.shape[0]
    consts = _rope_consts(S)
    vc0 = (AT_QW + AT_KW) // AT_HD
    dg = matmul(dx, p["w_out"], tb=True, name="at_dg")
    dw_out = matmul(sv["g"], dx, ta=True, out_dtype=GRAD_WIRE_DTYPE, name="at_dw_out")
    do, dgate = rowwise_bwd("at_gate_bwd", _gate_fn, [(sv["o"], 0, True), (u, (AT_QW + 2 * AT_KW) // 1024, True)], [],
                            [dg], W=1024, ncb=AT_QW // 1024, diff_rows=[0, 1], diff_shared=[])
    dqs, delta = flash_bwd_dq(qr, kr, u, sv["o"], sv["lse"], do, v_col0=vc0)
    (dkr, dv), cres = _own(flash_bwd_dkv(qr, kr, u, sv["lse"], delta, do, v_col0=vc0, comm=comm), comm)
    dq_raw, dqg = _at_qk("q", u, 0, AT_HEADS, AT_HD ** -0.5, p["q_g"], consts, cot=dqs)
    dk_raw, dkg = _at_qk("k", u, AT_HEADS, AT_KV, 1.0, p["k_g"], consts, cot=dkr)
    du = _cat_mxu([dq_raw, dk_raw, dv, dgate])
    dx_prev, dng, dw_in = _in_out_bwd("at", sv["x"], sv["ng"], sv["hn"], du, p["w_in"], dx)
    return dx_prev, dict(w_in=dw_in, w_out=dw_out, q_g=dqg, k_g=dkg, ng=dng), cres


def _to_stream(t, dil):
    S = t.shape[0]
    return t.reshape(S // dil, dil, DL_HEADS, DL_HD).transpose(2, 1, 0, 3)


def _from_stream(t):
    H, dil, Ls, E = t.shape
    return t.transpose(2, 1, 0, 3).reshape(Ls * dil, H * E)


def _stream_to_hm(t):
    H, dil, Ls, w = t.shape
    return t.transpose(0, 2, 1, 3).reshape(H * Ls * dil, w)


def _hm_to_stream(t, dil):
    w = t.shape[1]
    S = t.shape[0] // DL_HEADS
    return t.reshape(DL_HEADS, S // dil, dil, w).transpose(0, 2, 1, 3)


OX_LSE = DL_HD
DOX_LSE, DOX_DM = DL_HD, DL_HD + 32


def _win(p_ref, c_ref, n_ref, h, T):
    return jnp.concatenate([p_ref[h, 0, T - DL_HALF:T, :], c_ref[h, 0], n_ref[h, 0, 0:DL_HALF, :]], axis=0)


def _win_specs(T, E, nb):
    return [pl.BlockSpec((DL_HEADS, 1, T, E), lambda d, n: (0, d, jnp.maximum(n - 1, 0), 0)),
            pl.BlockSpec((DL_HEADS, 1, T, E), lambda d, n: (0, d, n, 0)),
            pl.BlockSpec((DL_HEADS, 1, T, E), lambda d, n: (0, d, jnp.minimum(n + 1, nb - 1), 0))]


def _band_mask_q(n, T, W, Ls):
    i = lax.broadcasted_iota(jnp.int32, (T, W), 0)
    j = lax.broadcasted_iota(jnp.int32, (T, W), 1)
    kpos = n * T + j - DL_HALF
    return (jnp.abs(j - DL_HALF - i) <= DL_HALF) & (kpos >= 0) & (kpos < Ls)


def band_fwd(q, k, v, bias, *, scale):
    H, dil, Ls, E = q.shape
    T, W = _dl_tiles(Ls)
    nb = Ls // T

    def body(q_ref, kp_ref, kc_ref, kn_ref, vp_ref, vc_ref, vn_ref, b_ref, ox_ref):
        n = pl.program_id(1)
        mask = _band_mask_q(n, T, W, Ls)
        for h in range(H):
            kw = _win(kp_ref, kc_ref, kn_ref, h, T)
            vw = _win(vp_ref, vc_ref, vn_ref, h, T)
            s = _dot(q_ref[h, 0], kw, ((1,), (1,))) * scale + b_ref[h]
            s = jnp.where(mask, s, NEG_BIG)
            m = jnp.max(s, axis=1, keepdims=True)
            lse = m + jnp.log(jnp.sum(jnp.exp(s - m), axis=1, keepdims=True))
            p = jnp.exp(s - lse)
            ox_ref[h, 0, :, 0:E] = _dot(p, vw, ((1,), (0,)))
            ox_ref[h, 0, :, E:2 * E] = lse + jnp.zeros((T, E), F32)

    cur = pl.BlockSpec((H, 1, T, E), lambda d, n: (0, d, n, 0))
    return pl.pallas_call(
        body, name=f"band_fwd_{dil}", grid=(dil, nb),
        in_specs=[cur] + _win_specs(T, E, nb) + _win_specs(T, E, nb) + [pl.BlockSpec((H, T, W), lambda d, n: (0, 0, 0))],
        out_specs=pl.BlockSpec((H, 1, T, 2 * E), lambda d, n: (0, d, n, 0)),
        out_shape=jax.ShapeDtypeStruct((H, dil, Ls, 2 * E), F32),
        compiler_params=_cp("parallel", "parallel"),
    )(q, k, k, k, v, v, v, bias)


def band_bwd_dq(q, k, v, bias, dox, *, scale):
    H, dil, Ls, E = q.shape
    T, W = _dl_tiles(Ls)
    nb = Ls // T

    def body(q_ref, kp_ref, kc_ref, kn_ref, vp_ref, vc_ref, vn_ref, b_ref, dox_ref, dq_ref, db_ref):
        d, n = pl.program_id(0), pl.program_id(1)
        mask = _band_mask_q(n, T, W, Ls)
        first = jnp.logical_and(d == 0, n == 0)

        @pl.when(first)
        def _():
            db_ref[...] = jnp.zeros_like(db_ref)

        for h in range(H):
            kw = _win(kp_ref, kc_ref, kn_ref, h, T)
            vw = _win(vp_ref, vc_ref, vn_ref, h, T)
            dox = dox_ref[h, 0]
            do, lse, dm = dox[:, 0:E], dox[:, DOX_LSE:DOX_LSE + 1], dox[:, DOX_DM:DOX_DM + 1]
            s = _dot(q_ref[h, 0], kw, ((1,), (1,))) * scale + b_ref[h]
            p = jnp.where(mask, jnp.exp(jnp.where(mask, s, 0.0) - lse), 0.0)
            dp = _dot(do, vw, ((1,), (1,)))
            ds = p * (dp - dm)
            dq_ref[h, 0] = (_dot(ds, kw, ((1,), (0,))) * scale).astype(dq_ref.dtype)
            db_ref[h] += ds

    cur = pl.BlockSpec((H, 1, T, E), lambda d, n: (0, d, n, 0))
    bspec = pl.BlockSpec((H, T, W), lambda d, n: (0, 0, 0))
    return pl.pallas_call(
        body, name=f"band_bwd_dq_{dil}", grid=(dil, nb),
        in_specs=[cur] + _win_specs(T, E, nb) + _win_specs(T, E, nb) + [bspec,
                  pl.BlockSpec((H, 1, T, 2 * E), lambda d, n: (0, d, n, 0))],
        out_specs=[cur, bspec],
        out_shape=[jax.ShapeDtypeStruct((H, dil, Ls, E), MXU_DTYPE), jax.ShapeDtypeStruct((H, T, W), F32)],
        compiler_params=_cp("arbitrary", "arbitrary"),
    )(q, k, k, k, v, v, v, bias, dox)


def band_bwd_dkv(q, k, v, bias_t, dox, *, scale):
    H, dil, Ls, E = k.shape
    T, W = _dl_tiles(Ls, DL_TB)
    nb = Ls // T

    def body(qp_ref, qc_ref, qn_ref, k_ref, v_ref, b_ref, dp_ref, dc_ref, dn_ref, dk_ref, dv_ref):
        n = pl.program_id(1)
        iw = lax.broadcasted_iota(jnp.int32, (W, T), 0)
        j = lax.broadcasted_iota(jnp.int32, (W, T), 1)
        qpos = n * T + iw - DL_HALF
        mask = (jnp.abs(j + DL_HALF - iw) <= DL_HALF) & (qpos >= 0) & (qpos < Ls)
        for h in range(H):
            qw = _win(qp_ref, qc_ref, qn_ref, h, T)
            doxw = _win(dp_ref, dc_ref, dn_ref, h, T)
            dow, lsew, dmw = doxw[:, 0:E], doxw[:, DOX_LSE:DOX_LSE + 1], doxw[:, DOX_DM:DOX_DM + 1]
            s = _dot(qw, k_ref[h, 0], ((1,), (1,))) * scale + b_ref[h]
            p = jnp.where(mask, jnp.exp(jnp.where(mask, s, 0.0) - lsew), 0.0)
            dv_ref[h, 0] = _dot(p, dow, ((0,), (0,))).astype(dv_ref.dtype)
            dp = _dot(dow, v_ref[h, 0], ((1,), (1,)))
            ds = p * (dp - dmw)
            dk_ref[h, 0] = (_dot(ds, qw, ((0,), (0,))) * scale).astype(dk_ref.dtype)

    cur = pl.BlockSpec((H, 1, T, E), lambda d, n: (0, d, n, 0))
    return pl.pallas_call(
        body, name=f"band_bwd_dkv_{dil}", grid=(dil, nb),
        in_specs=_win_specs(T, E, nb) + [cur, cur, pl.BlockSpec((H, W, T), lambda d, n: (0, 0, 0))]
        + _win_specs(T, 2 * E, nb),
        out_specs=[cur, cur],
        out_shape=[jax.ShapeDtypeStruct((H, dil, Ls, E), MXU_DTYPE)] * 2,
        compiler_params=_cp("parallel", "parallel"),
    )(q, q, q, k, v, bias_t, dox, dox, dox)


def dl_merge_fwd(oxs, *, R=1024):
    rows = oxs[0].shape[0]
    R = min(R, rows)
    E = DL_HD

    def body(a_ref, b_ref, c_ref, o_ref):
        vals = [r[...] for r in (a_ref, b_ref, c_ref)]
        o_ref[...] = _dl_merge_fn(*[t[:, 0:E] for t in vals], *[t[:, OX_LSE:OX_LSE + 1] for t in vals])[0]

    spec = pl.BlockSpec((R, 2 * E), lambda i: (i, 0))
    return pl.pallas_call(
        body, name="dl_merge", grid=(rows // R,), in_specs=[spec] * 3,
        out_specs=pl.BlockSpec((R, E), lambda i: (i, 0)), out_shape=jax.ShapeDtypeStruct((rows, E), F32),
        compiler_params=_cp("parallel"),
    )(*oxs)


def dl_merge_bwd(oxs, do, *, R=1024):
    rows = oxs[0].shape[0]
    R = min(R, rows)
    E = DL_HD

    def body(a_ref, b_ref, c_ref, do_ref, da_ref, db_ref, dc_ref):
        vals = [r[...] for r in (a_ref, b_ref, c_ref)]
        os_ = [t[:, 0:E] for t in vals]
        ls_ = [t[:, OX_LSE:OX_LSE + 1] for t in vals]
        _, vjp = jax.vjp(_dl_merge_fn, *os_, *ls_)
        g = vjp((do_ref[...],))
        for k, d_ref in enumerate((da_ref, db_ref, dc_ref)):
            dm = jnp.sum(g[k] * os_[k], axis=1, keepdims=True) - g[3 + k]
            d_ref[:, 0:E] = g[k]
            d_ref[:, DOX_LSE:DOX_DM] = ls_[k] + jnp.zeros((R, DOX_DM - DOX_LSE), F32)
            d_ref[:, DOX_DM:2 * E] = dm + jnp.zeros((R, 2 * E - DOX_DM), F32)

    spec = pl.BlockSpec((R, 2 * E), lambda i: (i, 0))
    return pl.pallas_call(
        body, name="dl_merge_bwd", grid=(rows // R,), in_specs=[spec] * 3 + [pl.BlockSpec((R, E), lambda i: (i, 0))],
        out_specs=[spec] * 3, out_shape=[jax.ShapeDtypeStruct((rows, 2 * E), F32)] * 3,
        compiler_params=_cp("parallel"),
    )(*oxs, do)


def _dl_bias_tables(rel_bias, dil, Tq, Tk):
    Wq, Wk = Tq + 2 * DL_HALF, Tk + 2 * DL_HALF
    bq, bk = _dl_bucket_tables(dil, Tq)[0], _dl_bucket_tables(dil, Tk)[1]
    idx = np.concatenate([bq.reshape(-1), bk.reshape(-1)])
    onehot_t = (np.arange(REL_BUCKETS)[:, None] == idx[None, :]).astype(np.float32)
    tab = matmul(rel_bias.T, jnp.asarray(onehot_t), exact=True, name=f"dl_bias_{dil}", tm=DL_HEADS, tk=REL_BUCKETS,
                 tn=_tile(idx.shape[0], (8192, 4096, 2048, 1024, 512, 256, 128)))
    return tab[:, :Tq * Wq].reshape(DL_HEADS, Tq, Wq), tab[:, Tq * Wq:].reshape(DL_HEADS, Wk, Tk), bq


def dl_layer_fwd(x, ng, p):
    S = x.shape[0]
    hn = _prenorm("dl", x, ng)
    nqkv = 3 * len(DL_PAIRS) * DL_W
    uqkv = matmul(hn, p["w_in"], name="dl_in_qkv", b_cols=(0, nqkv), out_dtype=MXU_DTYPE)
    ugate = matmul(hn, p["w_in"], name="dl_in_gate", b_cols=(nqkv, DL_W))
    scale = DL_HD ** -0.5
    per_group, ox_hm = [], []
    for gi, (window, dil) in enumerate(DL_PAIRS):
        base = gi * 3 * DL_W
        Tq, _ = _dl_tiles(S // dil)
        Tk, _ = _dl_tiles(S // dil, DL_TB)
        qs, ks, vs = [_to_stream(uqkv[:, base + c * DL_W:base + (c + 1) * DL_W], dil) for c in range(3)]
        bias, bias_t, bq = _dl_bias_tables(p["rel_bias"], dil, Tq, Tk)
        ox_s = band_fwd(qs, ks, vs, bias, scale=scale)
        per_group.append(dict(qs=qs, ks=ks, vs=vs, bias=bias, bias_t=bias_t, bq=bq, dil=dil))
        ox_hm.append(_stream_to_hm(ox_s))
    om = dl_merge_fwd(ox_hm)
    o = om.reshape(DL_HEADS, S, DL_HD).transpose(1, 0, 2).reshape(S, DL_W)
    g = rowwise_fwd("dl_gate", _gate_fn, [(o, 0, False), (ugate, 0, False)], [], [(DL_W, MXU_DTYPE)], W=DL_W)[0]
    xn = matmul(g, p["w_out"], residual=x, name="dl_out")
    return xn, dict(x=x, ng=ng, hn=hn, ugate=ugate, per_group=per_group, ox_hm=ox_hm, o=o, g=g)


def dl_layer_bwd(sv, p, dx):
    ugate = sv["ugate"]
    S = ugate.shape[0]
    scale = DL_HD ** -0.5
    dg = matmul(dx, p["w_out"], tb=True, name="dl_dg")
    dw_out = matmul(sv["g"], dx, ta=True, out_dtype=GRAD_WIRE_DTYPE, name="dl_dw_out")
    do, dgate = rowwise_bwd("dl_gate_bwd", _gate_fn, [(sv["o"], 0, False), (sv["ugate"], 0, False)], [], [dg], W=DL_W,
                            diff_rows=[0, 1], diff_shared=[])
    do_hm = do.reshape(S, DL_HEADS, DL_HD).transpose(1, 0, 2).reshape(DL_HEADS * S, DL_HD)
    dox_hm = dl_merge_bwd(sv["ox_hm"], do_hm)
    parts, dbs, onehots = [], [], []
    for gi, pg in enumerate(sv["per_group"]):
        dil = pg["dil"]
        T, W = _dl_tiles(S // dil)
        dox_s = _hm_to_stream(dox_hm[gi], dil)
        dq_s, dbias = band_bwd_dq(pg["qs"], pg["ks"], pg["vs"], pg["bias"], dox_s, scale=scale)
        dk_s, dv_s = band_bwd_dkv(pg["qs"], pg["ks"], pg["vs"], pg["bias_t"], dox_s, scale=scale)
        parts += [_from_stream(dq_s), _from_stream(dk_s), _from_stream(dv_s)]
        dbs.append(dbias.reshape(DL_HEADS, T * W))
        onehots.append((pg["bq"].reshape(-1)[:, None] == np.arange(REL_BUCKETS)[None, :]).astype(np.float32))
    drel = matmul(jnp.concatenate(dbs, axis=1), jnp.asarray(np.concatenate(onehots, axis=0)), exact=True,
                  name="dl_drel", tm=DL_HEADS, tn=REL_BUCKETS, tk=2048)
    du = _cat_mxu(parts + [dgate])
    dx_prev, dng, dw_in = _in_out_bwd("dl", sv["x"], sv["ng"], sv["hn"], du, p["w_in"], dx)
    return dx_prev, dict(w_in=dw_in, w_out=dw_out, rel_bias=drel.T, ng=dng)


WEIGHT_ORDER = ['norm_g', 'final_g', 'rel_bias', 'hgrn_lb', 'ssd_w_in', 'ssd_conv_w', 'ssd_conv_b', 'ssd_dt_bias',
                'ssd_a_log', 'ssd_d', 'ssd_norm_g', 'ssd_w_out', 'hg_w_in', 'hg_norm_g', 'hg_w_out', 'at_w_in',
                'at_q_norm_g', 'at_k_norm_g', 'at_w_out', 'dl_w_in', 'dl_w_out']
BIG_IN = ['ssd_w_in', 'hg_w_in', 'at_w_in', 'dl_w_in']
BIG_OUT = ['ssd_w_out', 'hg_w_out', 'at_w_out', 'dl_w_out']
BIG = BIG_IN + BIG_OUT
SMALL = [n for n in WEIGHT_ORDER if n not in BIG]
LANES = 128


def _pack(arrs):
    flat = jnp.concatenate([a.reshape(-1).astype(F32) for a in arrs])
    n = flat.shape[0]
    rows = -(-n // (8 * LANES)) * 8
    return jnp.pad(flat, (0, rows * LANES - n)).reshape(rows, LANES)


def _unpack(buf, shapes):
    flat = buf.reshape(-1)
    out, off = [], 0
    for shp in shapes:
        n = int(np.prod(shp)) if len(shp) else 1
        out.append(flat[off:off + n].reshape(shp))
        off += n
    return out


def kernel(x, norm_g, final_g, rel_bias, hgrn_lb, ssd_w_in, ssd_conv_w, ssd_conv_b, ssd_dt_bias, ssd_a_log, ssd_d, ssd_norm_g, ssd_w_out, hg_w_in, hg_norm_g, hg_w_out, at_w_in, at_q_norm_g, at_k_norm_g, at_w_out, dl_w_in, dl_w_out, loss_target, m_norm_g, m_final_g, m_rel_bias, m_hgrn_lb, m_ssd_w_in, m_ssd_conv_w, m_ssd_conv_b, m_ssd_dt_bias, m_ssd_a_log, m_ssd_d, m_ssd_norm_g, m_ssd_w_out, m_hg_w_in, m_hg_norm_g, m_hg_w_out, m_at_w_in, m_at_q_norm_g, m_at_k_norm_g, m_at_w_out, m_dl_w_in, m_dl_w_out, v_norm_g, v_final_g, v_rel_bias, v_hgrn_lb, v_ssd_w_in, v_ssd_conv_w, v_ssd_conv_b, v_ssd_dt_bias, v_ssd_a_log, v_ssd_d, v_ssd_norm_g, v_ssd_w_out, v_hg_w_in, v_hg_norm_g, v_hg_w_out, v_at_w_in, v_at_q_norm_g, v_at_k_norm_g, v_at_w_out, v_dl_w_in, v_dl_w_out):
    w = dict(norm_g=norm_g, final_g=final_g, rel_bias=rel_bias, hgrn_lb=hgrn_lb, ssd_w_in=ssd_w_in, ssd_conv_w=ssd_conv_w, ssd_conv_b=ssd_conv_b, ssd_dt_bias=ssd_dt_bias, ssd_a_log=ssd_a_log, ssd_d=ssd_d, ssd_norm_g=ssd_norm_g, ssd_w_out=ssd_w_out, hg_w_in=hg_w_in, hg_norm_g=hg_norm_g, hg_w_out=hg_w_out, at_w_in=at_w_in, at_q_norm_g=at_q_norm_g, at_k_norm_g=at_k_norm_g, at_w_out=at_w_out, dl_w_in=dl_w_in, dl_w_out=dl_w_out)
    m = dict(norm_g=m_norm_g, final_g=m_final_g, rel_bias=m_rel_bias, hgrn_lb=m_hgrn_lb, ssd_w_in=m_ssd_w_in, ssd_conv_w=m_ssd_conv_w, ssd_conv_b=m_ssd_conv_b, ssd_dt_bias=m_ssd_dt_bias, ssd_a_log=m_ssd_a_log, ssd_d=m_ssd_d, ssd_norm_g=m_ssd_norm_g, ssd_w_out=m_ssd_w_out, hg_w_in=m_hg_w_in, hg_norm_g=m_hg_norm_g, hg_w_out=m_hg_w_out, at_w_in=m_at_w_in, at_q_norm_g=m_at_q_norm_g, at_k_norm_g=m_at_k_norm_g, at_w_out=m_at_w_out, dl_w_in=m_dl_w_in, dl_w_out=m_dl_w_out)
    v = dict(norm_g=v_norm_g, final_g=v_final_g, rel_bias=v_rel_bias, hgrn_lb=v_hgrn_lb, ssd_w_in=v_ssd_w_in, ssd_conv_w=v_ssd_conv_w, ssd_conv_b=v_ssd_conv_b, ssd_dt_bias=v_ssd_dt_bias, ssd_a_log=v_ssd_a_log, ssd_d=v_ssd_d, ssd_norm_g=v_ssd_norm_g, ssd_w_out=v_ssd_w_out, hg_w_in=v_hg_w_in, hg_norm_g=v_hg_norm_g, hg_w_out=v_hg_w_out, at_w_in=v_at_w_in, at_q_norm_g=v_at_q_norm_g, at_k_norm_g=v_at_k_norm_g, at_w_out=v_at_w_out, dl_w_in=v_dl_w_in, dl_w_out=v_dl_w_out)
    me = 4 * lax.axis_index("x") + 2 * lax.axis_index("y") + lax.axis_index("c")
    xs = x[0]
    S = xs.shape[0]

    shard2d = {n: w[n][0] for n in BIG}
    wire = {n: shard2d[n].astype(MXU_DTYPE) for n in BIG}

    def ag(names):
        return ("ag", [wire[n] for n in names])

    def assemble(names, blks):
        out = {}
        for n, blk in zip(names, blks):
            r, c = shard2d[n].shape
            out[n] = blk.transpose(1, 0, 2).reshape(r, N_DEV * c) if n in BIG_IN else blk.reshape(N_DEV * r, c)
        return out

    def a2a(names, gw):
        bufs = []
        for n in names:
            r, c = shard2d[n].shape
            bufs.append(gw[n].reshape(r, N_DEV, c).transpose(1, 0, 2) if n in BIG_IN else gw[n].reshape(N_DEV, r, c))
        return ("a2a", bufs)

    ssd_w, hg_w, at_w, dl_w = (["ssd_w_in", "ssd_w_out"], ["hg_w_in", "hg_w_out"], ["at_w_in", "at_w_out"],
                               ["dl_w_in", "dl_w_out"])
    full = assemble(ssd_w, allgather_two_level_multi([wire[n] for n in ssd_w], name="allgather_ssd_weights"))
    ncw = ssd_conv_w.shape[2]
    nhg = hg_norm_g.shape[1]
    small_shard = jnp.zeros((8, 512), F32)
    small_shard = small_shard.at[:SSD_CONV, :ncw].set(ssd_conv_w[0]).at[SSD_CONV, :nhg].set(hg_norm_g[0])
    small_all = allgather_two_level(small_shard, name="allgather_small_weights")
    conv_w_full = small_all[:, :SSD_CONV, :ncw].transpose(1, 0, 2).reshape(SSD_CONV, N_DEV * ncw)
    hg_norm_full = small_all[:, SSD_CONV, :nhg].reshape(1, N_DEV * nhg)

    p_ssd = dict(w_in=jnp.pad(full["ssd_w_in"], ((0, 0), (0, SSD_IN_PAD - SSD_IN))), w_out=full["ssd_w_out"],
                 conv_w=conv_w_full, conv_b=ssd_conv_b,
                 dt_bias=jnp.pad(ssd_dt_bias.reshape(1, 2 * SSD_HEADS), ((0, 0), (0, 128 - 2 * SSD_HEADS))),
                 alog=jnp.pad(ssd_a_log.reshape(1, 2 * SSD_HEADS), ((0, 0), (0, 128 - 2 * SSD_HEADS))),
                 dexp=jnp.repeat(ssd_d.reshape(-1), SSD_HEADDIM)[None, :], norm_g=ssd_norm_g)
    x1, sv0, got, got_at_in = ssd_layer_fwd(xs, norm_g[0:1], p_ssd, comm=ag(hg_w), comm1=ag(["at_w_in"]))
    full.update(assemble(hg_w, got))
    p_hg = dict(w_in=full["hg_w_in"], w_out=full["hg_w_out"], norm_g=hg_norm_full, hgrn_lb=hgrn_lb)
    x2, sv1, got_at_out, _ = hg_layer_fwd(x1, norm_g[1:2], p_hg, comm0=ag(["at_w_out"]))
    full.update(assemble(at_w, got_at_in + got_at_out))
    p_at = dict(w_in=full["at_w_in"], w_out=full["at_w_out"], q_g=at_q_norm_g, k_g=at_k_norm_g)
    x3, sv2, got_dl = at_layer_fwd(x2, norm_g[2:3], p_at, comm=ag(dl_w))
    full.update(assemble(dl_w, got_dl))
    p_dl = dict(w_in=full["dl_w_in"], w_out=full["dl_w_out"], rel_bias=rel_bias)
    x4, sv3 = dl_layer_fwd(x3, norm_g[3:4], p_dl)
    loss_part, dx4, dfinal = loss_head(x4, final_g[None, :], loss_target[0])
    dx3, g3 = dl_layer_bwd(sv3, p_dl, dx4)
    dx2, g2, recv_dl = at_layer_bwd(sv2, p_at, dx3, comm=a2a(dl_w, dict(dl_w_in=g3["w_in"], dl_w_out=g3["w_out"])))
    dx1, g1, recv_at = hg_layer_bwd(sv1, p_hg, dx2, comm=a2a(at_w, dict(at_w_in=g2["w_in"], at_w_out=g2["w_out"])))
    dx0, g0, recv_hg, recv_ssd = ssd_layer_bwd(
        sv0, p_ssd, dx1, comm=a2a(hg_w, dict(hg_w_in=g1["w_in"], hg_w_out=g1["w_out"])),
        tail_comm=lambda g_in, g_out: (a2a(["ssd_w_in"], dict(ssd_w_in=g_in)) if g_out is None
                                       else a2a(["ssd_w_out"], dict(ssd_w_out=g_out))))
    recv = dict(zip(ssd_w + hg_w + at_w + dl_w, recv_ssd + recv_hg + recv_at + recv_dl))

    small_full = dict(
        norm_g=jnp.concatenate([g0["ng"], g1["ng"], g2["ng"], g3["ng"]], axis=0), final_g=dfinal[0],
        rel_bias=g3["rel_bias"], hgrn_lb=g1["hgrn_lb"], ssd_conv_w=g0["conv_w"][None], ssd_conv_b=g0["conv_b"],
        ssd_dt_bias=g0["dt_bias"].reshape(1, 2, SSD_HEADS), ssd_a_log=g0["a_log"].reshape(1, 2, SSD_HEADS),
        ssd_d=g0["d"], ssd_norm_g=g0["norm_g"], hg_norm_g=g1["norm_g"], at_q_norm_g=g2["q_g"], at_k_norm_g=g2["k_g"])
    packed = _pack([loss_part[0, 0:1]] + [small_full[n] for n in SMALL])
    summed = sum_parts(allgather_two_level(packed, name="allgather_small_grads"), name="sum_small_grads")
    parts = _unpack(summed, [()] + [small_full[n].shape for n in SMALL])
    loss = parts[0]
    gsmall = dict(zip(SMALL, parts[1:]))
    gsmall["ssd_conv_w"] = lax.dynamic_slice_in_dim(gsmall["ssd_conv_w"], me * ncw, ncw, axis=2)
    gsmall["hg_norm_g"] = lax.dynamic_slice_in_dim(gsmall["hg_norm_g"], me * nhg, nhg, axis=1)
    shapes = [w[n].shape for n in SMALL]
    d_p, m_p, v_p = adamw_plain(_pack([w[n] for n in SMALL]), _pack([gsmall[n] for n in SMALL]),
                                _pack([m[n] for n in SMALL]), _pack([v[n] for n in SMALL]), name="adamw_small")
    grads = dict(gsmall)
    deltas = dict(zip(SMALL, _unpack(d_p, shapes)))
    new_m = dict(zip(SMALL, _unpack(m_p, shapes)))
    new_v = dict(zip(SMALL, _unpack(v_p, shapes)))

    for n in BIG:
        gs, ds, ms, vs = adamw_sum(recv[n], shard2d[n], m[n][0], v[n][0], name=f"adamw_{n}")
        grads[n], deltas[n], new_m[n], new_v[n] = gs[None], ds[None], ms[None], vs[None]

    return (loss, dx0[None], *[grads[n] for n in WEIGHT_ORDER], *[deltas[n] for n in WEIGHT_ORDER],
            *[new_m[n] for n in WEIGHT_ORDER], *[new_v[n] for n in WEIGHT_ORDER])
```

```python
import functools
import math

import jax
import jax.numpy as jnp
import numpy as np
from jax import lax
from jax.experimental import pallas as pl
from jax.experimental.pallas import tpu as pltpu

F32 = jnp.float32
MXU_DTYPE = jnp.bfloat16
GRAD_WIRE_DTYPE = jnp.bfloat16
HIGHEST = lax.Precision.HIGHEST
MESH_ID = pl.DeviceIdType.MESH
N_DEV = 8

D_MODEL = 1024
EPS = 1e-6
NEG_BIG = -1e30

SSD_DI = 2048
SSD_HEADDIM = 64
SSD_HEADS = 32
SSD_GROUPS = 4
SSD_HPG = 8
SSD_STATE = 128
SSD_CONV = 7
SSD_CHUNK = 128
SSD_CONV_CH = SSD_DI + 2 * SSD_GROUPS * SSD_STATE
SSD_IN = SSD_DI + SSD_CONV_CH + 2 * SSD_HEADS
SSD_IN_PAD = 5376

HG_CHUNK = 32
HG_HEADS = 8
HG_D = 128
HG_W = 1024

AT_HEADS = 16
AT_KV = 8
AT_HD = 128
AT_QW = 2048
AT_KW = 1024
GRID_W = 64
ROPE_THETA = 10000.0

DL_PAIRS = ((128, 1), (512, 4), (2048, 16))
DL_HEADS = 16
DL_HD = 64
DL_W = 1024
DL_HALF = 64
REL_BUCKETS = 32
REL_MAX_DIST = 1024

ADAM_LR = 0.001
ADAM_B1 = 0.9
ADAM_B2 = 0.999
ADAM_EPS = 1e-08
ADAM_WD = 0.01
ADAM_STEP = 10

VMEM_LIMIT = 56 * 1024 * 1024


def _cp(*sem):
    return pltpu.CompilerParams(dimension_semantics=tuple(sem), vmem_limit_bytes=VMEM_LIMIT)


def _tile(n, cands=(1024, 768, 512, 384, 256, 128)):
    for c in cands:
        if n % c == 0:
            return c
    return n


def _dot(a, b, dims):
    return lax.dot_general(a.astype(MXU_DTYPE), b.astype(MXU_DTYPE), (dims, ((), ())), preferred_element_type=F32)


def _dot_exact(a, b, dims):
    return lax.dot_general(a, b, (dims, ((), ())), precision=HIGHEST, preferred_element_type=F32)


def _silu(x):
    return x * jax.nn.sigmoid(x)


def _my_pos():
    return lax.axis_index("x"), lax.axis_index("y"), lax.axis_index("c")


def _flat(px, py, pc):
    return 4 * px + 2 * py + pc


def _peers():
    x_, y_, c_ = _my_pos()
    out = []
    for k in range(1, N_DEV):
        fx, fy, fc = (k >> 2) & 1, (k >> 1) & 1, k & 1
        out.append(((1 - x_) if fx else x_, (1 - y_) if fy else y_, (1 - c_) if fc else c_))
    return out


def _comm_copies(kind, in_refs, out_refs, send_sems, recv_sems, local_sems):
    me = _flat(*_my_pos())
    local, starts, waits = [], [], []
    for b, (i_ref, o_ref) in enumerate(zip(in_refs, out_refs)):
        local.append(pltpu.make_async_copy(i_ref if kind == "ag" else i_ref.at[me], o_ref.at[me], local_sems.at[b]))
        for k, p in enumerate(_peers()):
            src = i_ref if kind == "ag" else i_ref.at[_flat(*p)]
            starts.append(pltpu.make_async_remote_copy(
                src_ref=src, dst_ref=o_ref.at[me], send_sem=send_sems.at[b, k], recv_sem=recv_sems.at[b, k],
                device_id=p, device_id_type=MESH_ID))
            waits.append(pltpu.make_async_remote_copy(
                src_ref=src, dst_ref=o_ref.at[_flat(*p)], send_sem=send_sems.at[b, k], recv_sem=recv_sems.at[b, k],
                device_id=p, device_id_type=MESH_ID))
    return local, starts, waits


def pcall(body, comm, *, name, grid, in_specs, out_specs, out_shape, scratch_shapes=(), compiler_params=None):
    single = not isinstance(out_specs, (list, tuple))
    out_specs_l = [out_specs] if single else list(out_specs)
    out_shape_l = [out_shape] if single else list(out_shape)
    if comm is None:
        return pl.pallas_call(body, name=name, grid=grid, in_specs=in_specs, out_specs=out_specs, out_shape=out_shape,
                              scratch_shapes=list(scratch_shapes), compiler_params=compiler_params)
    kind, bufs = comm
    nb, n_in, n_out, n_scr = len(bufs), len(in_specs), len(out_specs_l), len(scratch_shapes)
    c_shape = [jax.ShapeDtypeStruct(((N_DEV,) + b.shape) if kind == "ag" else b.shape, b.dtype) for b in bufs]
    anyspec = pl.BlockSpec(memory_space=pl.ANY)

    def body2(*refs):
        ins, c_ins = refs[:n_in], refs[n_in:n_in + nb]
        outs = refs[n_in + nb:n_in + nb + n_out]
        c_outs = refs[n_in + nb + n_out:n_in + 2 * nb + n_out]
        scr = refs[n_in + 2 * nb + n_out:n_in + 2 * nb + n_out + n_scr]
        send_sems, recv_sems, local_sems = refs[n_in + 2 * nb + n_out + n_scr:]
        first = last = None
        for ax, g in enumerate(grid):
            pid = pl.program_id(ax)
            first = (pid == 0) if first is None else jnp.logical_and(first, pid == 0)
            last = (pid == g - 1) if last is None else jnp.logical_and(last, pid == g - 1)

        @pl.when(first)
        def _():
            local, starts, _ = _comm_copies(kind, c_ins, c_outs, send_sems, recv_sems, local_sems)
            for cp in local + starts:
                cp.start()

        body(*ins, *outs, *scr)

        @pl.when(last)
        def _():
            local, _, waits = _comm_copies(kind, c_ins, c_outs, send_sems, recv_sems, local_sems)
            for cp in waits + local:
                cp.wait()

    call = pl.pallas_call(
        body2, name=name, grid=grid, in_specs=list(in_specs) + [anyspec] * nb,
        out_specs=out_specs_l + [anyspec] * nb, out_shape=out_shape_l + c_shape,
        scratch_shapes=list(scratch_shapes) + [pltpu.SemaphoreType.DMA((nb, N_DEV - 1)),
                                               pltpu.SemaphoreType.DMA((nb, N_DEV - 1)), pltpu.SemaphoreType.DMA((nb,))],
        compiler_params=compiler_params)

    def run(*args):
        res = call(*args, *bufs)
        own = res[:n_out]
        return (own[0] if single else list(own)), list(res[n_out:])

    return run


def matmul(a, b, *, name, ta=False, tb=False, residual=None, out_dtype=F32, exact=False, tm=None, tn=None, tk=None,
           b_cols=None, comm=None):
    M, K = (a.shape[1], a.shape[0]) if ta else a.shape
    n0, N = b_cols if b_cols is not None else (0, b.shape[0] if tb else b.shape[1])
    tm = tm or _tile(M, (1024, 512, 256, 128))
    tn = tn or _tile(N, (1024, 768, 512, 384, 256, 128))
    tk = tk or _tile(K, (2048, 1024, 768, 512, 384, 256, 128))
    nk = K // tk
    dims = (((0,) if ta else (1,)), ((1,) if tb else (0,)))

    def body(*refs):
        if residual is None:
            a_ref, b_ref, o_ref, acc = refs
            r_ref = None
        else:
            a_ref, b_ref, r_ref, o_ref, acc = refs
        k = pl.program_id(2)

        @pl.when(k == 0)
        def _():
            acc[...] = jnp.zeros_like(acc)

        if exact:
            acc[...] += _dot_exact(a_ref[...], b_ref[...], dims)
        else:
            acc[...] += _dot(a_ref[...], b_ref[...], dims)

        @pl.when(k == nk - 1)
        def _():
            r = acc[...]
            if r_ref is not None:
                r = r + r_ref[...]
            o_ref[...] = r.astype(o_ref.dtype)

    a_spec = pl.BlockSpec((tk, tm), lambda i, j, k: (k, i)) if ta else pl.BlockSpec((tm, tk), lambda i, j, k: (i, k))
    assert n0 % tn == 0
    jb = n0 // tn
    b_spec = (pl.BlockSpec((tn, tk), lambda i, j, k: (j + jb, k)) if tb
              else pl.BlockSpec((tk, tn), lambda i, j, k: (k, j + jb)))
    in_specs = [a_spec, b_spec]
    args = [a, b]
    if residual is not None:
        in_specs.append(pl.BlockSpec((tm, tn), lambda i, j, k: (i, j)))
        args.append(residual)
    return pcall(
        body, comm, name=name, grid=(M // tm, N // tn, nk), in_specs=in_specs,
        out_specs=pl.BlockSpec((tm, tn), lambda i, j, k: (i, j)),
        out_shape=jax.ShapeDtypeStruct((M, N), out_dtype),
        scratch_shapes=[pltpu.VMEM((tm, tn), F32)],
        compiler_params=_cp("parallel", "parallel", "arbitrary"),
    )(*args)


def _row_specs2(rows, shared, R, W, ncb):
    specs = []
    for arr, col0, per_j, *wd in rows:
        w = wd[0] if wd else W
        if per_j:
            assert col0 % ncb == 0
            specs.append(pl.BlockSpec((R, ncb * w), lambda i, c=col0 // ncb: (i, c)))
        else:
            specs.append(pl.BlockSpec((R, w), lambda i, c=col0: (i, c)))
    for arr, per_j in shared:
        specs.append(pl.BlockSpec((arr.shape[0], ncb * W if per_j else arr.shape[1]), lambda i: (0, 0)))
    return specs


def _col_block(ref, per_j, j, w):
    return ref[:, j * w:(j + 1) * w] if per_j else ref[...]


def rowwise_fwd(name, fn, rows, shared, outs, *, W, ncb=1, R=256):
    S = rows[0][0].shape[0]
    R = min(R, S)
    nr, ns = len(rows), len(shared)
    widths = [(r[3] if len(r) > 3 else W) for r in rows]
    per_j = [r[2] for r in rows] + [s[1] for s in shared]
    ws = widths + [W] * ns

    def body(*refs):
        for j in range(ncb):
            vals = [_col_block(refs[k], per_j[k], j, ws[k]) for k in range(nr + ns)]
            res = fn(*vals)
            for o_ref, r, (wo, _) in zip(refs[nr + ns:], res, outs):
                o_ref[:, j * wo:(j + 1) * wo] = r.astype(o_ref.dtype)

    return pl.pallas_call(
        body, name=name, grid=(S // R,),
        in_specs=_row_specs2(rows, shared, R, W, ncb),
        out_specs=[pl.BlockSpec((R, ncb * w), lambda i: (i, 0)) for w, _ in outs],
        out_shape=[jax.ShapeDtypeStruct((S, ncb * w), dt) for w, dt in outs],
        compiler_params=_cp("parallel"),
    )(*[r[0] for r in rows], *[s[0] for s in shared])


def rowwise_bwd(name, fn, rows, shared, cots, *, W, ncb=1, R=256, diff_rows, diff_shared, add=None):
    S = rows[0][0].shape[0]
    R = min(R, S)
    nr, ns, nc = len(rows), len(shared), len(cots)
    widths = [(r[3] if len(r) > 3 else W) for r in rows]
    per_j = [r[2] for r in rows] + [s[1] for s in shared]
    ws = widths + [W] * ns
    wo = [c.shape[1] // ncb for c in cots]
    dws = [widths[r] for r in diff_rows]

    def body(*refs):
        ins = refs[:nr + ns]
        ct_refs = refs[nr + ns:nr + ns + nc]
        pos = nr + ns + nc
        add_ref = None
        if add is not None:
            add_ref = refs[pos]
            pos += 1
        drow_refs = refs[pos:pos + len(diff_rows)]
        dsh_refs = refs[pos + len(diff_rows):]
        i = pl.program_id(0)
        tot = [None] * len(diff_shared)
        for j in range(ncb):
            vals = [_col_block(ins[k], per_j[k], j, ws[k]) for k in range(nr + ns)]

            def f(*dv):
                full = list(vals)
                for idx, v in zip(list(diff_rows) + [nr + s for s in diff_shared], dv):
                    full[idx] = v
                return tuple(fn(*full))

            prim = [vals[idx] for idx in diff_rows] + [vals[nr + s] for s in diff_shared]
            _, vjp = jax.vjp(f, *prim)
            grads = vjp(tuple(c[:, j * w:(j + 1) * w] for c, w in zip(ct_refs, wo)))
            for k, (d_ref, w) in enumerate(zip(drow_refs, dws)):
                g = grads[k]
                if k == 0 and add_ref is not None:
                    g = g + add_ref[:, j * w:(j + 1) * w]
                d_ref[:, j * w:(j + 1) * w] = g
            for k, (d_ref, s) in enumerate(zip(dsh_refs, diff_shared)):
                g = grads[len(diff_rows) + k]
                if shared[s][1]:
                    @pl.when(i == 0)
                    def _(d_ref=d_ref, g=g, j=j):
                        d_ref[:, j * W:(j + 1) * W] = g

                    @pl.when(i != 0)
                    def _(d_ref=d_ref, g=g, j=j):
                        d_ref[:, j * W:(j + 1) * W] += g
                else:
                    tot[k] = g if tot[k] is None else tot[k] + g
        for k, (d_ref, s) in enumerate(zip(dsh_refs, diff_shared)):
            if not shared[s][1]:
                @pl.when(i == 0)
                def _(d_ref=d_ref, g=tot[k]):
                    d_ref[...] = g

                @pl.when(i != 0)
                def _(d_ref=d_ref, g=tot[k]):
                    d_ref[...] += g

    in_specs = _row_specs2(rows, shared, R, W, ncb)
    in_specs += [pl.BlockSpec((R, ncb * w), lambda i: (i, 0)) for w in wo]
    args = [r[0] for r in rows] + [s[0] for s in shared] + list(cots)
    if add is not None:
        in_specs.append(pl.BlockSpec((R, ncb * dws[0]), lambda i: (i, 0)))
        args.append(add)
    out_specs = [pl.BlockSpec((R, ncb * w), lambda i: (i, 0)) for w in dws]
    out_shape = [jax.ShapeDtypeStruct((S, ncb * w), F32) for w in dws]
    for s in diff_shared:
        arr, pj = shared[s]
        shp = (arr.shape[0], ncb * W if pj else arr.shape[1])
        out_specs.append(pl.BlockSpec(shp, lambda i: (0, 0)))
        out_shape.append(jax.ShapeDtypeStruct(shp, F32))
    return pl.pallas_call(
        body, name=name, grid=(S // R,), in_specs=in_specs, out_specs=out_specs, out_shape=out_shape,
        compiler_params=_cp("arbitrary"),
    )(*args)


def _rms(x, g):
    return x * lax.rsqrt(jnp.mean(x * x, axis=-1, keepdims=True) + EPS) * g


def _prenorm_fn(x, g):
    return (_rms(x, g),)


def loss_head(x, g, tgt, *, R=256):
    S, D = x.shape
    R = min(R, S)

    def fn(xv, gv, tv):
        err = _rms(xv, gv) - tv
        return 0.5 * jnp.sum(jnp.mean(err * err, axis=-1, keepdims=True), axis=0, keepdims=True)

    def body(x_ref, g_ref, t_ref, loss_ref, dx_ref, dg_ref):
        i = pl.program_id(0)
        tv = t_ref[...]
        val, vjp = jax.vjp(lambda a, b: fn(a, b, tv), x_ref[...], g_ref[...])
        dx, dg = vjp(jnp.ones((1, 1), F32))
        dx_ref[...] = dx

        @pl.when(i == 0)
        def _():
            loss_ref[...] = jnp.zeros_like(loss_ref) + val
            dg_ref[...] = dg

        @pl.when(i != 0)
        def _():
            loss_ref[...] += val
            dg_ref[...] += dg

    return pl.pallas_call(
        body, name="loss_head", grid=(S // R,),
        in_specs=[pl.BlockSpec((R, D), lambda i: (i, 0)), pl.BlockSpec((1, D), lambda i: (0, 0)),
                  pl.BlockSpec((R, D), lambda i: (i, 0))],
        out_specs=[pl.BlockSpec((1, 128), lambda i: (0, 0)), pl.BlockSpec((R, D), lambda i: (i, 0)),
                   pl.BlockSpec((1, D), lambda i: (0, 0))],
        out_shape=[jax.ShapeDtypeStruct((1, 128), F32), jax.ShapeDtypeStruct((S, D), F32),
                   jax.ShapeDtypeStruct((1, D), F32)],
        compiler_params=_cp("arbitrary"),
    )(x, g, tgt)


@jax.custom_vjp
def _softplus(x):
    z = jnp.exp(-jnp.abs(x))
    u = 1.0 + z
    log1p = jnp.where(u == 1.0, z, jnp.log(u) * (z / jnp.where(u == 1.0, 1.0, u - 1.0)))
    return jnp.maximum(x, 0.0) + log1p


def _softplus_fwd(x):
    return _softplus(x), x


def _softplus_bwd(x, ct):
    return (ct * jax.nn.sigmoid(x),)


_softplus.defvjp(_softplus_fwd, _softplus_bwd)


def _dt_fn(raw, bias):
    return (_softplus(raw + bias),)


CONV_CB = 128
CONV_RB = 256
CONV_PAD = 8


def ssd_conv_fwd(u, conv_w, conv_b):
    S = u.shape[0]
    ncb = SSD_CONV_CH // CONV_CB
    col0 = SSD_DI // CONV_CB
    RB = min(CONV_RB, S)

    def body(x_ref, w_ref, b_ref, o_ref, pad):
        pad[0:CONV_PAD, :] = jnp.zeros((CONV_PAD, CONV_CB), F32)
        pad[S + CONV_PAD:S + 2 * CONV_PAD, :] = jnp.zeros((CONV_PAD, CONV_CB), F32)
        pad[CONV_PAD:S + CONV_PAD, :] = x_ref[...]
        w = w_ref[...]
        b = b_ref[...]
        for r in range(S // RB):
            acc = jnp.zeros((RB, CONV_CB), F32) + b
            for k in range(SSD_CONV):
                off = r * RB + CONV_PAD + k - SSD_CONV // 2
                acc = acc + pad[off:off + RB, :] * w[k:k + 1, :]
            o_ref[r * RB:(r + 1) * RB, :] = _silu(acc)

    return pl.pallas_call(
        body, name="ssd_conv_fwd", grid=(ncb,),
        in_specs=[pl.BlockSpec((S, CONV_CB), lambda j: (0, col0 + j)),
                  pl.BlockSpec((SSD_CONV, CONV_CB), lambda j: (0, j)),
                  pl.BlockSpec((1, CONV_CB), lambda j: (0, j))],
        out_specs=pl.BlockSpec((S, CONV_CB), lambda j: (0, j)),
        out_shape=jax.ShapeDtypeStruct((S, SSD_CONV_CH), F32),
        scratch_shapes=[pltpu.VMEM((S + 2 * CONV_PAD, CONV_CB), F32)],
        compiler_params=_cp("parallel"),
    )(u, conv_w, conv_b)


def ssd_conv_bwd(u, conv_w, conv_b, dact):
    S = u.shape[0]
    ncb = SSD_CONV_CH // CONV_CB
    col0 = SSD_DI // CONV_CB
    RB = min(CONV_RB, S)
    half = SSD_CONV // 2

    def body(x_ref, w_ref, b_ref, da_ref, dx_ref, dw_ref, db_ref, xpad, dpad):
        z8 = jnp.zeros((CONV_PAD, CONV_CB), F32)
        xpad[0:CONV_PAD, :] = z8
        xpad[S + CONV_PAD:S + 2 * CONV_PAD, :] = z8
        dpad[0:CONV_PAD, :] = z8
        dpad[S + CONV_PAD:S + 2 * CONV_PAD, :] = z8
        xpad[CONV_PAD:S + CONV_PAD, :] = x_ref[...]
        w = w_ref[...]
        b = b_ref[...]
        dws = [jnp.zeros((1, CONV_CB), F32) for _ in range(SSD_CONV)]
        db = jnp.zeros((1, CONV_CB), F32)
        for r in range(S // RB):
            acc = jnp.zeros((RB, CONV_CB), F32) + b
            xs = []
            for k in range(SSD_CONV):
                off = r * RB + CONV_PAD + k - half
                xk = xpad[off:off + RB, :]
                xs.append(xk)
                acc = acc + xk * w[k:k + 1, :]
            sg = jax.nn.sigmoid(acc)
            dc = da_ref[r * RB:(r + 1) * RB, :] * (sg * (1.0 + acc * (1.0 - sg)))
            dpad[r * RB + CONV_PAD:(r + 1) * RB + CONV_PAD, :] = dc
            db = db + jnp.sum(dc, axis=0, keepdims=True)
            for k in range(SSD_CONV):
                dws[k] = dws[k] + jnp.sum(xs[k] * dc, axis=0, keepdims=True)
        for r in range(S // RB):
            acc = jnp.zeros((RB, CONV_CB), F32)
            for k in range(SSD_CONV):
                off = r * RB + CONV_PAD + half - k
                acc = acc + dpad[off:off + RB, :] * w[k:k + 1, :]
            dx_ref[r * RB:(r + 1) * RB, :] = acc
        for k in range(SSD_CONV):
            dw_ref[k:k + 1, :] = dws[k]
        dw_ref[SSD_CONV:SSD_CONV + 1, :] = jnp.zeros((1, CONV_CB), F32)
        db_ref[...] = db

    return pl.pallas_call(
        body, name="ssd_conv_bwd", grid=(ncb,),
        in_specs=[pl.BlockSpec((S, CONV_CB), lambda j: (0, col0 + j)),
                  pl.BlockSpec((SSD_CONV, CONV_CB), lambda j: (0, j)),
                  pl.BlockSpec((1, CONV_CB), lambda j: (0, j)),
                  pl.BlockSpec((S, CONV_CB), lambda j: (0, j))],
        out_specs=[pl.BlockSpec((S, CONV_CB), lambda j: (0, j)),
                   pl.BlockSpec((SSD_CONV + 1, CONV_CB), lambda j: (0, j)),
                   pl.BlockSpec((1, CONV_CB), lambda j: (0, j))],
        out_shape=[jax.ShapeDtypeStruct((S, SSD_CONV_CH), F32),
                   jax.ShapeDtypeStruct((SSD_CONV + 1, SSD_CONV_CH), F32),
                   jax.ShapeDtypeStruct((1, SSD_CONV_CH), F32)],
        scratch_shapes=[pltpu.VMEM((S + 2 * CONV_PAD, CONV_CB), F32), pltpu.VMEM((S + 2 * CONV_PAD, CONV_CB), F32)],
        compiler_params=_cp("parallel"),
    )(u, conv_w, conv_b, dact)


def _ssd_group_layout(t):
    r = t.shape[0]
    g = t[:, :2 * SSD_HEADS].reshape(r, 2, SSD_GROUPS, SSD_HPG).transpose(2, 0, 1, 3).reshape(SSD_GROUPS, r, 2 * SSD_HPG)
    return jnp.pad(g, ((0, 0), (0, 0), (0, 128 - 2 * SSD_HPG)))


def _ssd_head_layout(t):
    r = t.shape[1]
    h = t[:, :, :2 * SSD_HPG].reshape(SSD_GROUPS, r, 2, SSD_HPG).transpose(1, 2, 0, 3).reshape(r, 2 * SSD_HEADS)
    return jnp.pad(h, ((0, 0), (0, 128 - 2 * SSD_HEADS)))


def _ssd_chunk(state, x, Bg, Cg, dt, alog, dtr, alr, *, reverse):
    Q, P = SSD_CHUNK, SSD_HEADDIM
    r = lax.broadcasted_iota(jnp.int32, (Q, Q), 0)
    c = lax.broadcasted_iota(jnp.int32, (Q, Q), 1)
    keep = (c >= r) if reverse else (c <= r)
    cum_t = jnp.transpose(_cumsum_rows(dt * (-jnp.exp(alog)), reverse))
    cum = _cumsum_rows(dtr * (-jnp.exp(alr)), reverse)
    last = 0 if reverse else Q - 1
    cum_l = cum[last:last + 1, :]
    CB = _dot(Cg, Bg, ((1,), (1,)))
    yoff = _dot(Cg, state, ((1,), (0,))) * jnp.exp(cum)
    xdt = x * dtr
    ys = []
    for h in range(SSD_HPG):
        col = h + (SSD_HPG if reverse else 0)
        hs = slice(h * P, (h + 1) * P)
        cum_q = jnp.concatenate([cum[:, hs]] * (Q // P), axis=1)
        L = jnp.where(keep, jnp.exp(jnp.where(keep, cum_q - cum_t[col:col + 1, :], 0.0)), 0.0)
        ys.append(_dot(CB * L, xdt[:, hs], ((1,), (0,))))
    new_state = jnp.exp(cum_l) * state + _dot(Bg, xdt * jnp.exp(cum_l - cum), ((0,), (0,)))
    return new_state, jnp.concatenate(ys, axis=1) + yoff


def ssd_scan_fwd(act, dt, alog, dtr, alr, *, reverse, y_prev=None, comm=None):
    S = act.shape[0]
    Q, N, P = SSD_CHUNK, SSD_STATE, SSD_HEADDIM
    nc = S // Q
    GW = SSD_HPG * P

    def cidx(i):
        return (nc - 1 - i) if reverse else i

    def body(*refs):
        if y_prev is None:
            x_ref, b_ref, c_ref, dt_ref, al_ref, dtr_ref, alr_ref, y_ref, st_ref, state = refs
            yp_ref = None
        else:
            x_ref, b_ref, c_ref, dt_ref, al_ref, dtr_ref, alr_ref, yp_ref, y_ref, st_ref, state = refs
        i = pl.program_id(1)

        @pl.when(i == 0)
        def _():
            state[...] = jnp.zeros_like(state)

        st = state[...]
        st_ref[0, 0] = st
        ns, y = _ssd_chunk(st, x_ref[...], b_ref[...], c_ref[...], dt_ref[0], al_ref[0], dtr_ref[...], alr_ref[...],
                           reverse=reverse)
        state[...] = ns
        y_ref[...] = y if yp_ref is None else y + yp_ref[...]

    xspec = pl.BlockSpec((Q, GW), lambda g, i: (cidx(i), g))
    in_specs = [xspec,
                pl.BlockSpec((Q, N), lambda g, i: (cidx(i), SSD_DI // N + g)),
                pl.BlockSpec((Q, N), lambda g, i: (cidx(i), SSD_DI // N + SSD_GROUPS + g)),
                pl.BlockSpec((1, Q, 128), lambda g, i: (g, cidx(i), 0)),
                pl.BlockSpec((1, 1, 128), lambda g, i: (g, 0, 0)),
                xspec, pl.BlockSpec((1, GW), lambda g, i: (0, g))]
    args = [act, act, act, dt, alog, dtr, alr]
    if y_prev is not None:
        in_specs.append(xspec)
        args.append(y_prev)
    return pcall(
        body, comm, name=f"ssd_scan_fwd_{int(reverse)}", grid=(SSD_GROUPS, nc), in_specs=in_specs,
        out_specs=[xspec, pl.BlockSpec((1, 1, N, GW), lambda g, i: (cidx(i), g, 0, 0))],
        out_shape=[jax.ShapeDtypeStruct((S, SSD_DI), F32), jax.ShapeDtypeStruct((nc, SSD_GROUPS, N, GW), F32)],
        scratch_shapes=[pltpu.VMEM((N, GW), F32)],
        compiler_params=_cp("arbitrary", "arbitrary"),
    )(*args)


def ssd_scan_bwd(act, dt, alog, dtr, alr, states, dy, prev_x, *, reverse, prev=None, comm=None):
    S = act.shape[0]
    Q, N, P = SSD_CHUNK, SSD_STATE, SSD_HEADDIM
    nc = S // Q
    GW = SSD_HPG * P

    def cidx(i):
        return i if reverse else (nc - 1 - i)

    def body(*refs):
        x_ref, b_ref, c_ref, dt_ref, al_ref, dtr_ref, alr_ref, st_ref, dy_ref, px_ref = refs[:10]
        pos = 10
        if prev is not None:
            pb_ref, pc_ref, pdt_ref, pal_ref = refs[pos:pos + 4]
            pos += 4
        dx_ref, db_ref, dc_ref, ddt_ref, dal_ref, ddtr_ref, dalr_ref, dstate = refs[pos:]
        i = pl.program_id(1)

        @pl.when(i == 0)
        def _():
            dstate[...] = jnp.zeros_like(dstate)

        _, vjp = jax.vjp(functools.partial(_ssd_chunk, reverse=reverse), st_ref[0, 0], x_ref[...], b_ref[...],
                         c_ref[...], dt_ref[0], al_ref[0], dtr_ref[...], alr_ref[...])
        dst, dx, dB, dC, ddt, dal, ddtr, dalr = vjp((dstate[...], dy_ref[...]))
        dstate[...] = dst
        dx_ref[...] = dx + px_ref[...]
        if prev is not None:
            dB = dB + pb_ref[...]
            dC = dC + pc_ref[...]
            ddt = ddt + pdt_ref[0]
        db_ref[...] = dB
        dc_ref[...] = dC
        ddt_ref[0] = ddt
        ddtr_ref[...] = ddtr

        @pl.when(i == 0)
        def _():
            dal_ref[0] = dal + (pal_ref[0] if prev is not None else 0.0)
            dalr_ref[...] = dalr

        @pl.when(i != 0)
        def _():
            dal_ref[0] += dal
            dalr_ref[...] += dalr

    xspec = pl.BlockSpec((Q, GW), lambda g, i: (cidx(i), g))
    gspec = pl.BlockSpec((Q, N), lambda g, i: (cidx(i), g))
    dtspec = pl.BlockSpec((1, Q, 128), lambda g, i: (g, cidx(i), 0))
    alspec = pl.BlockSpec((1, 1, 128), lambda g, i: (g, 0, 0))
    alrspec = pl.BlockSpec((1, GW), lambda g, i: (0, g))
    in_specs = [xspec,
                pl.BlockSpec((Q, N), lambda g, i: (cidx(i), SSD_DI // N + g)),
                pl.BlockSpec((Q, N), lambda g, i: (cidx(i), SSD_DI // N + SSD_GROUPS + g)),
                dtspec, alspec, xspec, alrspec,
                pl.BlockSpec((1, 1, N, GW), lambda g, i: (cidx(i), g, 0, 0)), xspec, xspec]
    args = [act, act, act, dt, alog, dtr, alr, states, dy, prev_x]
    if prev is not None:
        in_specs += [gspec, gspec, dtspec, alspec]
        args += list(prev)
    return pcall(
        body, comm, name=f"ssd_scan_bwd_{int(reverse)}", grid=(SSD_GROUPS, nc), in_specs=in_specs,
        out_specs=[xspec, gspec, gspec, dtspec, alspec, xspec, alrspec],
        out_shape=[jax.ShapeDtypeStruct((S, SSD_DI), F32), jax.ShapeDtypeStruct((S, SSD_GROUPS * N), F32),
                   jax.ShapeDtypeStruct((S, SSD_GROUPS * N), F32),
                   jax.ShapeDtypeStruct((SSD_GROUPS, S, 128), F32), jax.ShapeDtypeStruct((SSD_GROUPS, 1, 128), F32),
                   jax.ShapeDtypeStruct((S, SSD_DI), F32), jax.ShapeDtypeStruct((1, SSD_DI), F32)],
        scratch_shapes=[pltpu.VMEM((N, GW), F32)],
        compiler_params=_cp("arbitrary", "arbitrary"),
    )(*args)


def _ssd_post_fn(y, xs, z, dexp, ng):
    t = (y + xs * dexp) * _silu(z)
    return (_rms(t, ng),)


def _cumsum_rows_impl(x, reverse):
    n = x.shape[0]
    row = lax.broadcasted_iota(jnp.int32, x.shape, 0)
    k = 1
    while k < n:
        if reverse:
            x = x + jnp.where(row < n - k, pltpu.roll(x, n - k, 0), 0.0)
        else:
            x = x + jnp.where(row >= k, pltpu.roll(x, k, 0), 0.0)
        k *= 2
    return x


@functools.partial(jax.custom_vjp, nondiff_argnums=(1,))
def _cumsum_rows(x, reverse):
    return _cumsum_rows_impl(x, reverse)


_cumsum_rows.defvjp(lambda x, reverse: (_cumsum_rows_impl(x, reverse), None),
                    lambda reverse, _, ct: (_cumsum_rows_impl(ct, not reverse),))


def _hg_chunk(state, qraw, fraw, v, lb, *, reverse):
    C = HG_CHUNK
    r = lax.broadcasted_iota(jnp.int32, (C, C), 0)
    c = lax.broadcasted_iota(jnp.int32, (C, C), 1)
    keep = (c >= r) if reverse else (c <= r)
    q = _silu(qraw)
    f = lb + (1.0 - lb) * jax.nn.sigmoid(fraw)
    k = 1.0 - f
    g = jnp.log(f)
    G = _cumsum_rows(g, reverse)
    ref_row = C // 2 - 1 if reverse else C // 2
    last_row = 0 if reverse else C - 1
    Gr = G[ref_row:ref_row + 1, :]
    Gl = G[last_row:last_row + 1, :]
    q_t = q * jnp.exp(G - Gr)
    k_t = k * jnp.exp(Gr - G)
    att = jnp.where(keep, _dot(q_t, k_t, ((1,), (1,))), 0.0)
    o = _dot(att, v, ((1,), (0,))) + _dot(q * jnp.exp(G), state, ((1,), (0,)))
    kd = k * jnp.exp(Gl - G)
    new_state = jnp.transpose(jnp.exp(Gl)) * state + _dot(kd, v, ((0,), (0,)))
    return new_state, o


def hg_scan_fwd(u, lb, *, reverse, o_prev=None, rows=256, comm=None):
    S = u.shape[0]
    nh = HG_HEADS
    rows = min(rows, S)
    nsteps = S // rows
    ncb = rows // HG_CHUNK
    f_sec = 2 if reverse else 1

    def blk(i):
        return (nsteps - 1 - i) if reverse else i

    def body(*refs):
        if o_prev is None:
            q_ref, f_ref, v_ref, lb_ref, o_ref, st_ref, state = refs
            op_ref = None
        else:
            q_ref, f_ref, v_ref, lb_ref, op_ref, o_ref, st_ref, state = refs
        i = pl.program_id(0)

        @pl.when(i == 0)
        def _():
            state[...] = jnp.zeros_like(state)

        def chunk(cc, carry):
            ci = (ncb - 1 - cc) if reverse else cc
            sl = pl.ds(pl.multiple_of(ci * HG_CHUNK, HG_CHUNK), HG_CHUNK)
            for h in range(nh):
                hs = slice(h * HG_D, (h + 1) * HG_D)
                st = state[h]
                st_ref[ci, h] = st
                ns, o = _hg_chunk(st, q_ref[sl, hs], f_ref[sl, hs], v_ref[sl, hs], lb_ref[:, hs], reverse=reverse)
                state[h] = ns
                if op_ref is not None:
                    o = o + op_ref[sl, hs]
                o_ref[sl, hs] = o
            return carry

        lax.fori_loop(0, ncb, chunk, 0)

    rowspec = lambda sec: pl.BlockSpec((rows, HG_W), lambda i: (blk(i), sec))
    in_specs = [rowspec(0), rowspec(f_sec), rowspec(3), pl.BlockSpec((1, HG_W), lambda i: (0, 0))]
    args = [u, u, u, lb]
    if o_prev is not None:
        in_specs.append(rowspec(0))
        args.append(o_prev)
    return pcall(
        body, comm, name=f"hg_scan_fwd_{int(reverse)}", grid=(nsteps,), in_specs=in_specs,
        out_specs=[rowspec(0), pl.BlockSpec((ncb, nh, HG_D, HG_D), lambda i: (blk(i), 0, 0, 0))],
        out_shape=[jax.ShapeDtypeStruct((S, HG_W), F32), jax.ShapeDtypeStruct((S // HG_CHUNK, nh, HG_D, HG_D), F32)],
        scratch_shapes=[pltpu.VMEM((nh, HG_D, HG_D), F32)],
        compiler_params=_cp("arbitrary"),
    )(*args)


def hg_scan_bwd(u, lb, states, do, *, reverse, prev=None, rows=256, comm=None):
    S = u.shape[0]
    nh = HG_HEADS
    rows = min(rows, S)
    nsteps = S // rows
    ncb = rows // HG_CHUNK
    f_sec = 2 if reverse else 1

    def blk(i):
        return i if reverse else (nsteps - 1 - i)

    def body(*refs):
        q_ref, f_ref, v_ref, lb_ref, st_ref, do_ref = refs[:6]
        pos = 6
        if prev is not None:
            pq_ref, pv_ref, plb_ref = refs[pos:pos + 3]
            pos += 3
        dq_ref, df_ref, dv_ref, dlb_ref, dstate = refs[pos:]
        i = pl.program_id(0)

        @pl.when(i == 0)
        def _():
            dstate[...] = jnp.zeros_like(dstate)
            dlb_ref[...] = plb_ref[...] if prev is not None else jnp.zeros_like(dlb_ref)

        def chunk(cc, carry):
            ci = cc if reverse else (ncb - 1 - cc)
            sl = pl.ds(pl.multiple_of(ci * HG_CHUNK, HG_CHUNK), HG_CHUNK)
            for h in range(nh):
                hs = slice(h * HG_D, (h + 1) * HG_D)
                _, vjp = jax.vjp(functools.partial(_hg_chunk, reverse=reverse), st_ref[ci, h],
                                 q_ref[sl, hs], f_ref[sl, hs], v_ref[sl, hs], lb_ref[:, hs])
                dst, dq, df, dv, dlb = vjp((dstate[h], do_ref[sl, hs]))
                dstate[h] = dst
                if prev is not None:
                    dq = dq + pq_ref[sl, hs]
                    dv = dv + pv_ref[sl, hs]
                dq_ref[sl, hs] = dq
                df_ref[sl, hs] = df
                dv_ref[sl, hs] = dv
                dlb_ref[:, hs] += dlb
            return carry

        lax.fori_loop(0, ncb, chunk, 0)

    rowspec = lambda sec: pl.BlockSpec((rows, HG_W), lambda i: (blk(i), sec))
    lbspec = pl.BlockSpec((1, HG_W), lambda i: (0, 0))
    in_specs = [rowspec(0), rowspec(f_sec), rowspec(3), lbspec,
                pl.BlockSpec((ncb, nh, HG_D, HG_D), lambda i: (blk(i), 0, 0, 0)), rowspec(0)]
    args = [u, u, u, lb, states, do]
    if prev is not None:
        in_specs += [rowspec(0), rowspec(0), lbspec]
        args += list(prev)
    return pcall(
        body, comm, name=f"hg_scan_bwd_{int(reverse)}", grid=(nsteps,), in_specs=in_specs,
        out_specs=[rowspec(0), rowspec(0), rowspec(0), lbspec],
        out_shape=[jax.ShapeDtypeStruct((S, HG_W), F32)] * 3 + [jax.ShapeDtypeStruct((1, HG_W), F32)],
        scratch_shapes=[pltpu.VMEM((nh, HG_D, HG_D), F32)],
        compiler_params=_cp("arbitrary"),
    )(*args)


def _hg_lb_fn(lbp):
    m = jnp.max(lbp, axis=0, keepdims=True)
    e = jnp.exp(lbp - m)
    sm = e / jnp.sum(e, axis=0, keepdims=True)
    return ((sm[0:1] + sm[1:2]) - sm[0:1],)


def hg_lb_fwd(lbp):
    def body(x_ref, o_ref):
        o_ref[...] = _hg_lb_fn(x_ref[...])[0]

    return pl.pallas_call(body, name="hg_lb_fwd", out_shape=jax.ShapeDtypeStruct((1, HG_W), F32))(lbp)


def hg_lb_bwd(lbp, dlb):
    def body(x_ref, d_ref, o_ref):
        _, vjp = jax.vjp(_hg_lb_fn, x_ref[...])
        o_ref[...] = vjp((d_ref[...],))[0]

    return pl.pallas_call(body, name="hg_lb_bwd", out_shape=jax.ShapeDtypeStruct(lbp.shape, F32))(lbp, dlb)


def _hg_post_fn(o, gate, ng):
    return (_rms(o, ng) * _silu(gate),)


def _gate_fn(o, gate):
    return (o * _silu(gate),)


def _rope_tables(S):
    t = np.arange(S)
    row = (t // GRID_W).astype(np.float32)
    col = (t % GRID_W).astype(np.float32)
    half = AT_HD // 4
    inv = (ROPE_THETA ** (-np.arange(0, 2 * half, 2, dtype=np.float32) / np.float32(2 * half))).astype(np.float32)
    ar = row[:, None] * inv[None, :]
    ac = col[:, None] * inv[None, :]
    return ar.astype(np.float32), ac.astype(np.float32)


@jax.custom_vjp
def _half_swap(x):
    ax = x.ndim - 1
    lane = lax.broadcasted_iota(jnp.int32, x.shape, ax)
    return jnp.where((lane & 32) == 0, pltpu.roll(x, 96, ax), pltpu.roll(x, 32, ax))


_half_swap.defvjp(lambda x: (_half_swap(x), None), lambda _, ct: (_half_swap(ct),))


def _make_qk_fn(scale):
    def fn(x, ct, st, g):
        n = _rms(x, g)
        return ((n * ct + _half_swap(n) * st) * scale,)
    return fn


def flash_fwd(q, k, v, *, v_col0=0, tq=256, comm=None):
    S = q.shape[0]
    tq = min(tq, S)
    G = AT_HEADS // AT_KV

    def body(q_ref, k_ref, v_ref, o_ref, lse_ref):
        kv, vv = k_ref[...], v_ref[...]
        for g in range(G):
            sl = slice(g * AT_HD, (g + 1) * AT_HD)
            s = _dot(q_ref[:, sl], kv, ((1,), (1,)))
            m = jnp.max(s, axis=1, keepdims=True)
            p = jnp.exp(s - m)
            l = jnp.sum(p, axis=1, keepdims=True)
            o_ref[:, sl] = _dot(p, vv, ((1,), (0,))) / l
            lse_ref[0, :, g:g + 1] = m + jnp.log(l)

    return pcall(
        body, comm, name="flash_fwd", grid=(AT_KV, S // tq),
        in_specs=[pl.BlockSpec((tq, G * AT_HD), lambda h, i: (i, h)),
                  pl.BlockSpec((S, AT_HD), lambda h, i: (0, h)),
                  pl.BlockSpec((S, AT_HD), lambda h, i: (0, v_col0 + h))],
        out_specs=[pl.BlockSpec((tq, G * AT_HD), lambda h, i: (i, h)),
                   pl.BlockSpec((1, tq, G), lambda h, i: (h, i, 0))],
        out_shape=[jax.ShapeDtypeStruct((S, AT_QW), F32), jax.ShapeDtypeStruct((AT_KV, S, G), F32)],
        compiler_params=_cp("parallel", "arbitrary"),
    )(q, k, v)


def flash_bwd_dq(q, k, v, o, lse, do, *, v_col0=0, tq=256):
    S = q.shape[0]
    tq = min(tq, S)
    G = AT_HEADS // AT_KV

    def body(q_ref, k_ref, v_ref, o_ref, lse_ref, do_ref, dq_ref, dl_ref):
        kv, vv = k_ref[...], v_ref[...]
        for g in range(G):
            sl = slice(g * AT_HD, (g + 1) * AT_HD)
            dog = do_ref[:, sl]
            delta = jnp.sum(dog * o_ref[:, sl], axis=1, keepdims=True)
            s = _dot(q_ref[:, sl], kv, ((1,), (1,)))
            p = jnp.exp(s - lse_ref[0, :, g:g + 1])
            dp = _dot(dog, vv, ((1,), (1,)))
            ds = p * (dp - delta)
            dq_ref[:, sl] = _dot(ds, kv, ((1,), (0,)))
            dl_ref[0, :, g:g + 1] = delta

    qspec = pl.BlockSpec((tq, G * AT_HD), lambda h, i: (i, h))
    lspec = pl.BlockSpec((1, tq, G), lambda h, i: (h, i, 0))
    return pl.pallas_call(
        body, name="flash_bwd_dq", grid=(AT_KV, S // tq),
        in_specs=[qspec, pl.BlockSpec((S, AT_HD), lambda h, i: (0, h)),
                  pl.BlockSpec((S, AT_HD), lambda h, i: (0, v_col0 + h)), qspec, lspec, qspec],
        out_specs=[qspec, lspec],
        out_shape=[jax.ShapeDtypeStruct((S, AT_QW), F32), jax.ShapeDtypeStruct((AT_KV, S, G), F32)],
        compiler_params=_cp("parallel", "arbitrary"),
    )(q, k, v, o, lse, do)


def flash_bwd_dkv(q, k, v, lse, delta, do, *, v_col0=0, tk=512, comm=None):
    S = q.shape[0]
    tk = min(tk, S)
    G = AT_HEADS // AT_KV

    def body(q_ref, k_ref, v_ref, lse_ref, dl_ref, do_ref, dk_ref, dv_ref):
        kv, vv = k_ref[...], v_ref[...]
        dk = jnp.zeros((tk, AT_HD), F32)
        dv = jnp.zeros((tk, AT_HD), F32)
        for g in range(G):
            sl = slice(g * AT_HD, (g + 1) * AT_HD)
            qg, dog = q_ref[:, sl], do_ref[:, sl]
            s = _dot(qg, kv, ((1,), (1,)))
            p = jnp.exp(s - lse_ref[0, :, g:g + 1])
            dv = dv + _dot(p, dog, ((0,), (0,)))
            dp = _dot(dog, vv, ((1,), (1,)))
            ds = p * (dp - dl_ref[0, :, g:g + 1])
            dk = dk + _dot(ds, qg, ((0,), (0,)))
        dk_ref[...] = dk
        dv_ref[...] = dv

    qspec = pl.BlockSpec((S, G * AT_HD), lambda h, j: (0, h))
    kspec = pl.BlockSpec((tk, AT_HD), lambda h, j: (j, h))
    lspec = pl.BlockSpec((1, S, G), lambda h, j: (h, 0, 0))
    return pcall(
        body, comm, name="flash_bwd_dkv", grid=(AT_KV, S // tk),
        in_specs=[qspec, kspec, pl.BlockSpec((tk, AT_HD), lambda h, j: (j, v_col0 + h)), lspec, lspec, qspec],
        out_specs=[kspec, kspec],
        out_shape=[jax.ShapeDtypeStruct((S, AT_KW), F32), jax.ShapeDtypeStruct((S, AT_KW), F32)],
        compiler_params=_cp("parallel", "arbitrary"),
    )(q, k, v, lse, delta, do)


def _t5_bucket_np(rel):
    half = REL_BUCKETS // 2
    exact = half // 2
    n = np.abs(rel)
    large = exact + (np.log(np.maximum(n, 1).astype(np.float32) / np.float32(exact))
                     / np.float32(math.log(REL_MAX_DIST / exact)) * np.float32(half - exact)).astype(np.int32)
    large = np.minimum(large, half - 1)
    return np.where(rel > 0, half, 0) + np.where(n < exact, n, large)


DL_TB = 256


def _dl_tiles(Ls, T=128):
    T = min(T, Ls)
    return T, T + 2 * DL_HALF


def _dl_bucket_tables(dil, T):
    W = T + 2 * DL_HALF
    i = np.arange(T)[:, None]
    j = np.arange(W)[None, :]
    bq = _t5_bucket_np((j - DL_HALF - i) * dil)
    iw = np.arange(W)[:, None]
    jk = np.arange(T)[None, :]
    bk = _t5_bucket_np((jk + DL_HALF - iw) * dil)
    return bq.astype(np.int32), bk.astype(np.int32)


def _dl_merge_fn(o0, o1, o2, l0, l1, l2):
    m = jnp.maximum(jnp.maximum(l0, l1), l2)
    e0, e1, e2 = jnp.exp(l0 - m), jnp.exp(l1 - m), jnp.exp(l2 - m)
    den = e0 + e1 + e2
    return ((e0 / den) * o0 + (e1 / den) * o1 + (e2 / den) * o2,)


def _adamw_math(w, g, m, v):
    m = ADAM_B1 * m + (1.0 - ADAM_B1) * g
    v = ADAM_B2 * v + (1.0 - ADAM_B2) * (g * g)
    m_hat = m / (1.0 - ADAM_B1 ** ADAM_STEP)
    v_hat = v / (1.0 - ADAM_B2 ** ADAM_STEP)
    delta = -ADAM_LR * (m_hat / (jnp.sqrt(v_hat) + ADAM_EPS) + ADAM_WD * w)
    return delta, m, v


def adamw_sum(parts, w, m, v, *, name, R=128):
    rows, cols = w.shape
    R = min(R, rows)
    if rows % R:
        R = rows

    def body(p_ref, w_ref, m_ref, v_ref, g_ref, d_ref, nm_ref, nv_ref):
        g = p_ref[0].astype(F32)
        for s in range(1, N_DEV):
            g = g + p_ref[s].astype(F32)
        d, nm, nv = _adamw_math(w_ref[...], g, m_ref[...], v_ref[...])
        g_ref[...] = g
        d_ref[...] = d
        nm_ref[...] = nm
        nv_ref[...] = nv

    spec = pl.BlockSpec((R, cols), lambda i: (i, 0))
    return pl.pallas_call(
        body, name=name, grid=(rows // R,),
        in_specs=[pl.BlockSpec((N_DEV, R, cols), lambda i: (0, i, 0)), spec, spec, spec],
        out_specs=[spec] * 4, out_shape=[jax.ShapeDtypeStruct((rows, cols), F32)] * 4,
        compiler_params=_cp("parallel"),
    )(parts, w, m, v)


def sum_parts(parts, *, name):
    rows, cols = parts.shape[1:]

    def body(p_ref, o_ref):
        g = p_ref[0]
        for s in range(1, N_DEV):
            g = g + p_ref[s]
        o_ref[...] = g

    return pl.pallas_call(body, name=name, out_shape=jax.ShapeDtypeStruct((rows, cols), F32))(parts)


def adamw_plain(w, g, m, v, *, name):
    def body(w_ref, g_ref, m_ref, v_ref, d_ref, nm_ref, nv_ref):
        d, nm, nv = _adamw_math(w_ref[...], g_ref[...], m_ref[...], v_ref[...])
        d_ref[...] = d
        nm_ref[...] = nm
        nv_ref[...] = nv

    return pl.pallas_call(body, name=name, out_shape=[jax.ShapeDtypeStruct(w.shape, F32)] * 3)(w, g, m, v)


def allgather_two_level(x, *, name):
    R, C = x.shape

    def body(x_ref, out_ref, send_sems, recv_sems, local_sem):
        x_, y_, c_ = _my_pos()
        me, sibling = (x_, y_, c_), (x_, y_, 1 - c_)
        chips = [(1 - x_, y_), (x_, 1 - y_), (1 - x_, 1 - y_)]

        def rows(p):
            return out_ref.at[_flat(*p)]

        def copy(k, block, to, src=None):
            return pltpu.make_async_remote_copy(
                src_ref=rows(block) if src is None else src, dst_ref=rows(block),
                send_sem=send_sems.at[k], recv_sem=recv_sems.at[k], device_id=to, device_id_type=MESH_ID)

        mine = pltpu.make_async_copy(x_ref, rows(me), local_sem)
        mine.start()
        first = [copy(0, me, sibling, src=x_ref)]
        first += [copy(1 + j, me, (*chip, c_), src=x_ref) for j, chip in enumerate(chips)]
        for cp in first:
            cp.start()
        passed = [copy(4 + j, (*chip, c_), sibling) for j, chip in enumerate(chips)]
        for j, chip in enumerate(chips):
            copy(1 + j, (*chip, c_), me).wait_recv()
            passed[j].start()
        copy(0, sibling, me).wait_recv()
        for j, chip in enumerate(chips):
            copy(4 + j, (*chip, 1 - c_), me).wait_recv()
        for cp in first + passed:
            cp.wait_send()
        mine.wait()

    return pl.pallas_call(
        body, name=name,
        out_shape=jax.ShapeDtypeStruct((N_DEV, R, C), x.dtype),
        in_specs=[pl.BlockSpec(memory_space=pl.ANY)],
        out_specs=pl.BlockSpec(memory_space=pl.ANY),
        scratch_shapes=[pltpu.SemaphoreType.DMA((7,)), pltpu.SemaphoreType.DMA((7,)), pltpu.SemaphoreType.DMA],
    )(x)


def allgather_two_level_multi(xs, *, name):
    nb = len(xs)

    def body(*refs):
        x_refs, out_refs = refs[:nb], refs[nb:2 * nb]
        send_sems, recv_sems, local_sems = refs[2 * nb:]
        x_, y_, c_ = _my_pos()
        me, sibling = (x_, y_, c_), (x_, y_, 1 - c_)
        chips = [(1 - x_, y_), (x_, 1 - y_), (1 - x_, 1 - y_)]

        def copy(b, k, block, to, own=False):
            rows = out_refs[b].at[_flat(*block)]
            return pltpu.make_async_remote_copy(
                src_ref=x_refs[b] if own else rows, dst_ref=rows,
                send_sem=send_sems.at[b, k], recv_sem=recv_sems.at[b, k], device_id=to, device_id_type=MESH_ID)

        mine = [pltpu.make_async_copy(x_refs[b], out_refs[b].at[_flat(*me)], local_sems.at[b]) for b in range(nb)]
        first = []
        for b in range(nb):
            first.append(copy(b, 0, me, sibling, own=True))
            first += [copy(b, 1 + j, me, (*chip, c_), own=True) for j, chip in enumerate(chips)]
        for cp in mine + first:
            cp.start()
        passed = []
        for j, chip in enumerate(chips):
            for b in range(nb):
                copy(b, 1 + j, (*chip, c_), me).wait_recv()
                fwd = copy(b, 4 + j, (*chip, c_), sibling)
                fwd.start()
                passed.append(fwd)
        for b in range(nb):
            copy(b, 0, sibling, me).wait_recv()
            for j, chip in enumerate(chips):
                copy(b, 4 + j, (*chip, 1 - c_), me).wait_recv()
        for cp in first + passed:
            cp.wait_send()
        for cp in mine:
            cp.wait()

    anyspec = pl.BlockSpec(memory_space=pl.ANY)
    return pl.pallas_call(
        body, name=name,
        out_shape=[jax.ShapeDtypeStruct((N_DEV,) + x.shape, x.dtype) for x in xs],
        in_specs=[anyspec] * nb, out_specs=[anyspec] * nb,
        scratch_shapes=[pltpu.SemaphoreType.DMA((nb, 7)), pltpu.SemaphoreType.DMA((nb, 7)), pltpu.SemaphoreType.DMA((nb,))],
    )(*xs)


def _prenorm(tag, x, ng):
    return rowwise_fwd(f"{tag}_prenorm", _prenorm_fn, [(x, 0, False)], [(ng, False)], [(D_MODEL, MXU_DTYPE)], W=D_MODEL)[0]


def _cat_mxu(parts):
    return jnp.concatenate([t.astype(MXU_DTYPE) for t in parts], axis=1)


def _in_out_bwd(tag, x, ng, hn, du, w_in, dx, tail_comm=None):
    dw_in = matmul(hn, du, ta=True, out_dtype=GRAD_WIRE_DTYPE, name=f"{tag}_dw_in")
    comm = None if tail_comm is None else tail_comm(dw_in)
    dhn, cres = _own(matmul(du, w_in, tb=True, name=f"{tag}_dhn", comm=comm), comm)
    dx_prev, dng = rowwise_bwd(f"{tag}_prenorm_bwd", _prenorm_fn, [(x, 0, False)], [(ng, False)], [dhn],
                               W=D_MODEL, diff_rows=[0], diff_shared=[0], add=dx)
    return (dx_prev, dng, dw_in) if tail_comm is None else (dx_prev, dng, dw_in, cres)


def _own(res, comm):
    return (res, None) if comm is None else res


def ssd_layer_fwd(x, ng, p, comm=None, comm1=None):
    hn = _prenorm("ssd", x, ng)
    u = matmul(hn, p["w_in"], name="ssd_in")
    act = ssd_conv_fwd(u, p["conv_w"], p["conv_b"])
    dt = rowwise_fwd("ssd_dt", _dt_fn, [(u, (SSD_DI + SSD_CONV_CH) // 128, False)], [(p["dt_bias"], False)],
                     [(128, F32)], W=128)[0]
    H = SSD_HEADS
    dtr = [jnp.repeat(dt[:, d * H:(d + 1) * H], SSD_HEADDIM, axis=1) for d in (0, 1)]
    alr = [jnp.repeat(p["alog"][:, d * H:(d + 1) * H], SSD_HEADDIM, axis=1) for d in (0, 1)]
    dt, alog = _ssd_group_layout(dt), _ssd_group_layout(p["alog"])
    (y0, st0), cres = _own(ssd_scan_fwd(act, dt, alog, dtr[0], alr[0], reverse=False, comm=comm), comm)
    if p["w_out"] is None:
        p["w_out"] = p["w_out_of"](cres)
    (y, st1), cres1 = _own(ssd_scan_fwd(act, dt, alog, dtr[1], alr[1], reverse=True, y_prev=y0, comm=comm1), comm1)
    g = rowwise_fwd("ssd_post", _ssd_post_fn, [(y, 0, True), (act, 0, True), (u, 0, True)],
                    [(p["dexp"], True), (p["norm_g"], True)], [(512, MXU_DTYPE)], W=512, ncb=SSD_GROUPS)[0]
    xn = matmul(g, p["w_out"], residual=x, name="ssd_out")
    return xn, dict(x=x, ng=ng, hn=hn, u=u, act=act, dt=dt, alog=alog, dtr=dtr, alr=alr, y=y, st0=st0, st1=st1,
                    g=g), cres, cres1


def ssd_layer_bwd(sv, p, dx, comm=None, tail_comm=None):
    u, act, dt = sv["u"], sv["act"], sv["dt"]
    S = u.shape[0]
    dg = matmul(dx, p["w_out"], tb=True, name="ssd_dg")
    dw_out = matmul(sv["g"], dx, ta=True, out_dtype=GRAD_WIRE_DTYPE, name="ssd_dw_out")
    dy, dxs_skip, dz, ddexp, dnorm = rowwise_bwd(
        "ssd_post_bwd", _ssd_post_fn, [(sv["y"], 0, True), (act, 0, True), (u, 0, True)],
        [(p["dexp"], True), (p["norm_g"], True)], [dg], W=512, ncb=SSD_GROUPS, diff_rows=[0, 1, 2], diff_shared=[0, 1])
    dtr, alr = sv["dtr"], sv["alr"]
    (dxa, dB, dC, ddt, dal, ddtr0, dalr0), cres = _own(
        ssd_scan_bwd(act, dt, sv["alog"], dtr[0], alr[0], sv["st0"], dy, dxs_skip, reverse=False, comm=comm), comm)
    comm1 = None if tail_comm is None else tail_comm(None, dw_out)
    (dxa, dB, dC, ddt, dal, ddtr1, dalr1), tail_out = _own(
        ssd_scan_bwd(act, dt, sv["alog"], dtr[1], alr[1], sv["st1"], dy, dxa, reverse=True, prev=(dB, dC, ddt, dal),
                     comm=comm1), comm1)
    dact = jnp.concatenate([dxa, dB, dC], axis=1)
    dxbc, dconv_w, dconv_b = ssd_conv_bwd(u, p["conv_w"], p["conv_b"], dact)
    fold = jnp.asarray(np.repeat(np.eye(SSD_HEADS, dtype=np.float32), SSD_HEADDIM, axis=0))
    folded = [matmul(t, fold, exact=True, name=f"ssd_ddt_fold_{d}", tn=SSD_HEADS) for d, t in enumerate((ddtr0, ddtr1))]
    ddt_all = _ssd_head_layout(ddt) + jnp.pad(jnp.concatenate(folded, axis=1), ((0, 0), (0, 128 - 2 * SSD_HEADS)))
    dal_rep = jnp.concatenate([t.reshape(SSD_HEADS, SSD_HEADDIM).sum(axis=1) for t in (dalr0, dalr1)])[None, :]
    ddt_raw, ddt_bias = rowwise_bwd("ssd_dt_bwd", _dt_fn, [(u, (SSD_DI + SSD_CONV_CH) // 128, False)],
                                    [(p["dt_bias"], False)], [ddt_all], W=128, diff_rows=[0], diff_shared=[0])
    du = _cat_mxu([dz, dxbc, ddt_raw, jnp.zeros((S, SSD_IN_PAD - SSD_IN - 64), F32)])
    res = _in_out_bwd("ssd", sv["x"], sv["ng"], sv["hn"], du, p["w_in"], dx,
                      tail_comm=None if tail_comm is None else (lambda g_in: tail_comm(g_in[:, :SSD_IN], None)))
    dx_prev, dng, dw_in = res[:3]
    tail = res[3] + tail_out if tail_comm is not None else None
    grads = dict(
        w_in=dw_in[:, :SSD_IN], w_out=dw_out, conv_w=dconv_w[:SSD_CONV], conv_b=dconv_b,
        dt_bias=ddt_bias[:, :2 * SSD_HEADS], a_log=_ssd_head_layout(dal)[:, :2 * SSD_HEADS] + dal_rep,
        d=ddexp.reshape(SSD_HEADS, SSD_HEADDIM).sum(axis=1)[None, :], norm_g=dnorm, ng=dng)
    return dx_prev, grads, cres, tail


def hg_layer_fwd(x, ng, p, comm0=None, comm1=None):
    hn = _prenorm("hg", x, ng)
    u = matmul(hn, p["w_in"], name="hg_in")
    lb = hg_lb_fwd(p["hgrn_lb"])
    (o0, st0), cres0 = _own(hg_scan_fwd(u, lb, reverse=False, comm=comm0), comm0)
    (o, st1), cres1 = _own(hg_scan_fwd(u, lb, reverse=True, o_prev=o0, comm=comm1), comm1)
    g = rowwise_fwd("hg_post", _hg_post_fn, [(o, 0, True), (u, 4 * HG_HEADS, True)], [(p["norm_g"], True)],
                    [(HG_D, MXU_DTYPE)], W=HG_D, ncb=HG_HEADS)[0]
    xn = matmul(g, p["w_out"], residual=x, name="hg_out")
    return xn, dict(x=x, ng=ng, hn=hn, u=u, lb=lb, o=o, st0=st0, st1=st1, g=g), cres0, cres1


def hg_layer_bwd(sv, p, dx, comm=None):
    u, lb = sv["u"], sv["lb"]
    dg = matmul(dx, p["w_out"], tb=True, name="hg_dg")
    dw_out = matmul(sv["g"], dx, ta=True, out_dtype=GRAD_WIRE_DTYPE, name="hg_dw_out")
    do, dgate, dnorm = rowwise_bwd("hg_post_bwd", _hg_post_fn, [(sv["o"], 0, True), (u, 4 * HG_HEADS, True)],
                                   [(p["norm_g"], True)], [dg], W=HG_D, ncb=HG_HEADS, diff_rows=[0, 1], diff_shared=[0])
    (dq0, df0, dv0, dlb0), cres = _own(hg_scan_bwd(u, lb, sv["st0"], do, reverse=False, comm=comm), comm)
    dq, df1, dv, dlb = hg_scan_bwd(u, lb, sv["st1"], do, reverse=True, prev=(dq0, dv0, dlb0))
    du = _cat_mxu([dq, df0, df1, dv, dgate])
    dhgrn_lb = hg_lb_bwd(p["hgrn_lb"], dlb)
    dx_prev, dng, dw_in = _in_out_bwd("hg", sv["x"], sv["ng"], sv["hn"], du, p["w_in"], dx)
    return dx_prev, dict(w_in=dw_in, w_out=dw_out, norm_g=dnorm, hgrn_lb=dhgrn_lb, ng=dng), cres


def _rope_consts(S):
    ar, ac = _rope_tables(S)
    ct = np.concatenate([np.cos(ar), np.cos(ar), np.cos(ac), np.cos(ac)], axis=1).astype(np.float32)
    st = np.concatenate([-np.sin(ar), np.sin(ar), -np.sin(ac), np.sin(ac)], axis=1).astype(np.float32)
    return jnp.asarray(ct), jnp.asarray(st)


def _at_qk(tag, u, col0, nheads, scale, gain, consts, cot=None):
    ct, st = consts
    rows = [(u, col0, True), (ct, 0, False), (st, 0, False)]
    shared = [(gain, False)]
    if cot is None:
        return rowwise_fwd(f"at_{tag}", _make_qk_fn(scale), rows, shared, [(AT_HD, MXU_DTYPE)], W=AT_HD, ncb=nheads)[0]
    return rowwise_bwd(f"at_{tag}_bwd", _make_qk_fn(scale), rows, shared, [cot], W=AT_HD, ncb=nheads,
                       diff_rows=[0], diff_shared=[0])


def at_layer_fwd(x, ng, p, comm=None):
    S = x.shape[0]
    hn = _prenorm("at", x, ng)
    u = matmul(hn, p["w_in"], name="at_in")
    consts = _rope_consts(S)
    qr = _at_qk("q", u, 0, AT_HEADS, AT_HD ** -0.5, p["q_g"], consts)
    kr = _at_qk("k", u, AT_HEADS, AT_KV, 1.0, p["k_g"], consts)
    vc0 = (AT_QW + AT_KW) // AT_HD
    (o, lse), cres = _own(flash_fwd(qr, kr, u, v_col0=vc0, comm=comm), comm)
    g = rowwise_fwd("at_gate", _gate_fn, [(o, 0, True), (u, (AT_QW + 2 * AT_KW) // 1024, True)], [],
                    [(1024, MXU_DTYPE)], W=1024, ncb=AT_QW // 1024)[0]
    xn = matmul(g, p["w_out"], residual=x, name="at_out")
    return xn, dict(x=x, ng=ng, hn=hn, u=u, qr=qr, kr=kr, o=o, lse=lse, g=g), cres


def at_layer_bwd(sv, p, dx, comm=None):
    u, qr, kr = sv["u"], sv["qr"], sv["kr"]
    S = u.shape[0]
    consts = _rope_consts(S)
    vc0 = (AT_QW + AT_KW) // AT_HD
    dg = matmul(dx, p["w_out"], tb=True, name="at_dg")
    dw_out = matmul(sv["g"], dx, ta=True, out_dtype=GRAD_WIRE_DTYPE, name="at_dw_out")
    do, dgate = rowwise_bwd("at_gate_bwd", _gate_fn, [(sv["o"], 0, True), (u, (AT_QW + 2 * AT_KW) // 1024, True)], [],
                            [dg], W=1024, ncb=AT_QW // 1024, diff_rows=[0, 1], diff_shared=[])
    dqs, delta = flash_bwd_dq(qr, kr, u, sv["o"], sv["lse"], do, v_col0=vc0)
    (dkr, dv), cres = _own(flash_bwd_dkv(qr, kr, u, sv["lse"], delta, do, v_col0=vc0, comm=comm), comm)
    dq_raw, dqg = _at_qk("q", u, 0, AT_HEADS, AT_HD ** -0.5, p["q_g"], consts, cot=dqs)
    dk_raw, dkg = _at_qk("k", u, AT_HEADS, AT_KV, 1.0, p["k_g"], consts, cot=dkr)
    du = _cat_mxu([dq_raw, dk_raw, dv, dgate])
    dx_prev, dng, dw_in = _in_out_bwd("at", sv["x"], sv["ng"], sv["hn"], du, p["w_in"], dx)
    return dx_prev, dict(w_in=dw_in, w_out=dw_out, q_g=dqg, k_g=dkg, ng=dng), cres


def _to_stream(t, dil):
    S = t.shape[0]
    return t.reshape(S // dil, dil, DL_HEADS, DL_HD).transpose(2, 1, 0, 3)


def _from_stream(t):
    H, dil, Ls, E = t.shape
    return t.transpose(2, 1, 0, 3).reshape(Ls * dil, H * E)


def _stream_to_hm(t):
    H, dil, Ls, w = t.shape
    return t.transpose(0, 2, 1, 3).reshape(H * Ls * dil, w)


def _hm_to_stream(t, dil):
    w = t.shape[1]
    S = t.shape[0] // DL_HEADS
    return t.reshape(DL_HEADS, S // dil, dil, w).transpose(0, 2, 1, 3)


OX_LSE = DL_HD
DOX_LSE, DOX_DM = DL_HD, DL_HD + 32


def _win(p_ref, c_ref, n_ref, h, T):
    return jnp.concatenate([p_ref[h, 0, T - DL_HALF:T, :], c_ref[h, 0], n_ref[h, 0, 0:DL_HALF, :]], axis=0)


def _win_specs(T, E, nb):
    return [pl.BlockSpec((DL_HEADS, 1, T, E), lambda d, n: (0, d, jnp.maximum(n - 1, 0), 0)),
            pl.BlockSpec((DL_HEADS, 1, T, E), lambda d, n: (0, d, n, 0)),
            pl.BlockSpec((DL_HEADS, 1, T, E), lambda d, n: (0, d, jnp.minimum(n + 1, nb - 1), 0))]


def _band_mask_q(n, T, W, Ls):
    i = lax.broadcasted_iota(jnp.int32, (T, W), 0)
    j = lax.broadcasted_iota(jnp.int32, (T, W), 1)
    kpos = n * T + j - DL_HALF
    return (jnp.abs(j - DL_HALF - i) <= DL_HALF) & (kpos >= 0) & (kpos < Ls)


def band_fwd(q, k, v, bias, *, scale):
    H, dil, Ls, E = q.shape
    T, W = _dl_tiles(Ls)
    nb = Ls // T

    def body(q_ref, kp_ref, kc_ref, kn_ref, vp_ref, vc_ref, vn_ref, b_ref, ox_ref):
        n = pl.program_id(1)
        mask = _band_mask_q(n, T, W, Ls)
        for h in range(H):
            kw = _win(kp_ref, kc_ref, kn_ref, h, T)
            vw = _win(vp_ref, vc_ref, vn_ref, h, T)
            s = _dot(q_ref[h, 0], kw, ((1,), (1,))) * scale + b_ref[h]
            s = jnp.where(mask, s, NEG_BIG)
            m = jnp.max(s, axis=1, keepdims=True)
            lse = m + jnp.log(jnp.sum(jnp.exp(s - m), axis=1, keepdims=True))
            p = jnp.exp(s - lse)
            ox_ref[h, 0, :, 0:E] = _dot(p, vw, ((1,), (0,)))
            ox_ref[h, 0, :, E:2 * E] = lse + jnp.zeros((T, E), F32)

    cur = pl.BlockSpec((H, 1, T, E), lambda d, n: (0, d, n, 0))
    return pl.pallas_call(
        body, name=f"band_fwd_{dil}", grid=(dil, nb),
        in_specs=[cur] + _win_specs(T, E, nb) + _win_specs(T, E, nb) + [pl.BlockSpec((H, T, W), lambda d, n: (0, 0, 0))],
        out_specs=pl.BlockSpec((H, 1, T, 2 * E), lambda d, n: (0, d, n, 0)),
        out_shape=jax.ShapeDtypeStruct((H, dil, Ls, 2 * E), F32),
        compiler_params=_cp("parallel", "parallel"),
    )(q, k, k, k, v, v, v, bias)


def band_bwd_dq(q, k, v, bias, dox, *, scale):
    H, dil, Ls, E = q.shape
    T, W = _dl_tiles(Ls)
    nb = Ls // T

    def body(q_ref, kp_ref, kc_ref, kn_ref, vp_ref, vc_ref, vn_ref, b_ref, dox_ref, dq_ref, db_ref):
        d, n = pl.program_id(0), pl.program_id(1)
        mask = _band_mask_q(n, T, W, Ls)
        first = jnp.logical_and(d == 0, n == 0)

        @pl.when(first)
        def _():
            db_ref[...] = jnp.zeros_like(db_ref)

        for h in range(H):
            kw = _win(kp_ref, kc_ref, kn_ref, h, T)
            vw = _win(vp_ref, vc_ref, vn_ref, h, T)
            dox = dox_ref[h, 0]
            do, lse, dm = dox[:, 0:E], dox[:, DOX_LSE:DOX_LSE + 1], dox[:, DOX_DM:DOX_DM + 1]
            s = _dot(q_ref[h, 0], kw, ((1,), (1,))) * scale + b_ref[h]
            p = jnp.where(mask, jnp.exp(jnp.where(mask, s, 0.0) - lse), 0.0)
            dp = _dot(do, vw, ((1,), (1,)))
            ds = p * (dp - dm)
            dq_ref[h, 0] = (_dot(ds, kw, ((1,), (0,))) * scale).astype(dq_ref.dtype)
            db_ref[h] += ds

    cur = pl.BlockSpec((H, 1, T, E), lambda d, n: (0, d, n, 0))
    bspec = pl.BlockSpec((H, T, W), lambda d, n: (0, 0, 0))
    return pl.pallas_call(
        body, name=f"band_bwd_dq_{dil}", grid=(dil, nb),
        in_specs=[cur] + _win_specs(T, E, nb) + _win_specs(T, E, nb) + [bspec,
                  pl.BlockSpec((H, 1, T, 2 * E), lambda d, n: (0, d, n, 0))],
        out_specs=[cur, bspec],
        out_shape=[jax.ShapeDtypeStruct((H, dil, Ls, E), MXU_DTYPE), jax.ShapeDtypeStruct((H, T, W), F32)],
        compiler_params=_cp("arbitrary", "arbitrary"),
    )(q, k, k, k, v, v, v, bias, dox)


def band_bwd_dkv(q, k, v, bias_t, dox, *, scale):
    H, dil, Ls, E = k.shape
    T, W = _dl_tiles(Ls, DL_TB)
    nb = Ls // T

    def body(qp_ref, qc_ref, qn_ref, k_ref, v_ref, b_ref, dp_ref, dc_ref, dn_ref, dk_ref, dv_ref):
        n = pl.program_id(1)
        iw = lax.broadcasted_iota(jnp.int32, (W, T), 0)
        j = lax.broadcasted_iota(jnp.int32, (W, T), 1)
        qpos = n * T + iw - DL_HALF
        mask = (jnp.abs(j + DL_HALF - iw) <= DL_HALF) & (qpos >= 0) & (qpos < Ls)
        for h in range(H):
            qw = _win(qp_ref, qc_ref, qn_ref, h, T)
            doxw = _win(dp_ref, dc_ref, dn_ref, h, T)
            dow, lsew, dmw = doxw[:, 0:E], doxw[:, DOX_LSE:DOX_LSE + 1], doxw[:, DOX_DM:DOX_DM + 1]
            s = _dot(qw, k_ref[h, 0], ((1,), (1,))) * scale + b_ref[h]
            p = jnp.where(mask, jnp.exp(jnp.where(mask, s, 0.0) - lsew), 0.0)
            dv_ref[h, 0] = _dot(p, dow, ((0,), (0,))).astype(dv_ref.dtype)
            dp = _dot(dow, v_ref[h, 0], ((1,), (1,)))
            ds = p * (dp - dmw)
            dk_ref[h, 0] = (_dot(ds, qw, ((0,), (0,))) * scale).astype(dk_ref.dtype)

    cur = pl.BlockSpec((H, 1, T, E), lambda d, n: (0, d, n, 0))
    return pl.pallas_call(
        body, name=f"band_bwd_dkv_{dil}", grid=(dil, nb),
        in_specs=_win_specs(T, E, nb) + [cur, cur, pl.BlockSpec((H, W, T), lambda d, n: (0, 0, 0))]
        + _win_specs(T, 2 * E, nb),
        out_specs=[cur, cur],
        out_shape=[jax.ShapeDtypeStruct((H, dil, Ls, E), MXU_DTYPE)] * 2,
        compiler_params=_cp("parallel", "parallel"),
    )(q, q, q, k, v, bias_t, dox, dox, dox)


def dl_merge_fwd(oxs, *, R=1024):
    rows = oxs[0].shape[0]
    R = min(R, rows)
    E = DL_HD

    def body(a_ref, b_ref, c_ref, o_ref):
        vals = [r[...] for r in (a_ref, b_ref, c_ref)]
        o_ref[...] = _dl_merge_fn(*[t[:, 0:E] for t in vals], *[t[:, OX_LSE:OX_LSE + 1] for t in vals])[0]

    spec = pl.BlockSpec((R, 2 * E), lambda i: (i, 0))
    return pl.pallas_call(
        body, name="dl_merge", grid=(rows // R,), in_specs=[spec] * 3,
        out_specs=pl.BlockSpec((R, E), lambda i: (i, 0)), out_shape=jax.ShapeDtypeStruct((rows, E), F32),
        compiler_params=_cp("parallel"),
    )(*oxs)


def dl_merge_bwd(oxs, do, *, R=1024):
    rows = oxs[0].shape[0]
    R = min(R, rows)
    E = DL_HD

    def body(a_ref, b_ref, c_ref, do_ref, da_ref, db_ref, dc_ref):
        vals = [r[...] for r in (a_ref, b_ref, c_ref)]
        os_ = [t[:, 0:E] for t in vals]
        ls_ = [t[:, OX_LSE:OX_LSE + 1] for t in vals]
        _, vjp = jax.vjp(_dl_merge_fn, *os_, *ls_)
        g = vjp((do_ref[...],))
        for k, d_ref in enumerate((da_ref, db_ref, dc_ref)):
            dm = jnp.sum(g[k] * os_[k], axis=1, keepdims=True) - g[3 + k]
            d_ref[:, 0:E] = g[k]
            d_ref[:, DOX_LSE:DOX_DM] = ls_[k] + jnp.zeros((R, DOX_DM - DOX_LSE), F32)
            d_ref[:, DOX_DM:2 * E] = dm + jnp.zeros((R, 2 * E - DOX_DM), F32)

    spec = pl.BlockSpec((R, 2 * E), lambda i: (i, 0))
    return pl.pallas_call(
        body, name="dl_merge_bwd", grid=(rows // R,), in_specs=[spec] * 3 + [pl.BlockSpec((R, E), lambda i: (i, 0))],
        out_specs=[spec] * 3, out_shape=[jax.ShapeDtypeStruct((rows, 2 * E), F32)] * 3,
        compiler_params=_cp("parallel"),
    )(*oxs, do)


def _dl_bias_tables(rel_bias, dil, Tq, Tk):
    Wq, Wk = Tq + 2 * DL_HALF, Tk + 2 * DL_HALF
    bq, bk = _dl_bucket_tables(dil, Tq)[0], _dl_bucket_tables(dil, Tk)[1]
    idx = np.concatenate([bq.reshape(-1), bk.reshape(-1)])
    onehot_t = (np.arange(REL_BUCKETS)[:, None] == idx[None, :]).astype(np.float32)
    tab = matmul(rel_bias.T, jnp.asarray(onehot_t), exact=True, name=f"dl_bias_{dil}", tm=DL_HEADS, tk=REL_BUCKETS,
                 tn=_tile(idx.shape[0], (8192, 4096, 2048, 1024, 512, 256, 128)))
    return tab[:, :Tq * Wq].reshape(DL_HEADS, Tq, Wq), tab[:, Tq * Wq:].reshape(DL_HEADS, Wk, Tk), bq


def dl_layer_fwd(x, ng, p):
    S = x.shape[0]
    hn = _prenorm("dl", x, ng)
    nqkv = 3 * len(DL_PAIRS) * DL_W
    uqkv = matmul(hn, p["w_in"], name="dl_in_qkv", b_cols=(0, nqkv), out_dtype=MXU_DTYPE)
    ugate = matmul(hn, p["w_in"], name="dl_in_gate", b_cols=(nqkv, DL_W))
    scale = DL_HD ** -0.5
    per_group, ox_hm = [], []
    for gi, (window, dil) in enumerate(DL_PAIRS):
        base = gi * 3 * DL_W
        Tq, _ = _dl_tiles(S // dil)
        Tk, _ = _dl_tiles(S // dil, DL_TB)
        qkv = uqkv[:, base:base + 3 * DL_W].reshape(S // dil, dil, 3, DL_HEADS, DL_HD).transpose(2, 3, 1, 0, 4)
        qs, ks, vs = qkv[0], qkv[1], qkv[2]
        bias, bias_t, bq = _dl_bias_tables(p["rel_bias"], dil, Tq, Tk)
        ox_s = band_fwd(qs, ks, vs, bias, scale=scale)
        per_group.append(dict(qs=qs, ks=ks, vs=vs, bias=bias, bias_t=bias_t, bq=bq, dil=dil))
        ox_hm.append(_stream_to_hm(ox_s))
    om = dl_merge_fwd(ox_hm)
    o = om.reshape(DL_HEADS, S, DL_HD).transpose(1, 0, 2).reshape(S, DL_W)
    g = rowwise_fwd("dl_gate", _gate_fn, [(o, 0, False), (ugate, 0, False)], [], [(DL_W, MXU_DTYPE)], W=DL_W)[0]
    xn = matmul(g, p["w_out"], residual=x, name="dl_out")
    return xn, dict(x=x, ng=ng, hn=hn, ugate=ugate, per_group=per_group, ox_hm=ox_hm, o=o, g=g)


def dl_layer_bwd(sv, p, dx):
    ugate = sv["ugate"]
    S = ugate.shape[0]
    scale = DL_HD ** -0.5
    dg = matmul(dx, p["w_out"], tb=True, name="dl_dg")
    dw_out = matmul(sv["g"], dx, ta=True, out_dtype=GRAD_WIRE_DTYPE, name="dl_dw_out")
    do, dgate = rowwise_bwd("dl_gate_bwd", _gate_fn, [(sv["o"], 0, False), (sv["ugate"], 0, False)], [], [dg], W=DL_W,
                            diff_rows=[0, 1], diff_shared=[])
    do_hm = do.reshape(S, DL_HEADS, DL_HD).transpose(1, 0, 2).reshape(DL_HEADS * S, DL_HD)
    dox_hm = dl_merge_bwd(sv["ox_hm"], do_hm)
    parts, dbs, onehots = [], [], []
    for gi, pg in enumerate(sv["per_group"]):
        dil = pg["dil"]
        T, W = _dl_tiles(S // dil)
        dox_s = _hm_to_stream(dox_hm[gi], dil)
        dq_s, dbias = band_bwd_dq(pg["qs"], pg["ks"], pg["vs"], pg["bias"], dox_s, scale=scale)
        dk_s, dv_s = band_bwd_dkv(pg["qs"], pg["ks"], pg["vs"], pg["bias_t"], dox_s, scale=scale)
        parts += [_from_stream(dq_s), _from_stream(dk_s), _from_stream(dv_s)]
        dbs.append(dbias.reshape(DL_HEADS, T * W))
        onehots.append((pg["bq"].reshape(-1)[:, None] == np.arange(REL_BUCKETS)[None, :]).astype(np.float32))
    drel = matmul(jnp.concatenate(dbs, axis=1), jnp.asarray(np.concatenate(onehots, axis=0)), exact=True,
                  name="dl_drel", tm=DL_HEADS, tn=REL_BUCKETS, tk=2048)
    du = _cat_mxu(parts + [dgate])
    dx_prev, dng, dw_in = _in_out_bwd("dl", sv["x"], sv["ng"], sv["hn"], du, p["w_in"], dx)
    return dx_prev, dict(w_in=dw_in, w_out=dw_out, rel_bias=drel.T, ng=dng)


WEIGHT_ORDER = ['norm_g', 'final_g', 'rel_bias', 'hgrn_lb', 'ssd_w_in', 'ssd_conv_w', 'ssd_conv_b', 'ssd_dt_bias',
                'ssd_a_log', 'ssd_d', 'ssd_norm_g', 'ssd_w_out', 'hg_w_in', 'hg_norm_g', 'hg_w_out', 'at_w_in',
                'at_q_norm_g', 'at_k_norm_g', 'at_w_out', 'dl_w_in', 'dl_w_out']
BIG_IN = ['ssd_w_in', 'hg_w_in', 'at_w_in', 'dl_w_in']
BIG_OUT = ['ssd_w_out', 'hg_w_out', 'at_w_out', 'dl_w_out']
BIG = BIG_IN + BIG_OUT
SMALL = [n for n in WEIGHT_ORDER if n not in BIG]
LANES = 128


def _pack(arrs):
    flat = jnp.concatenate([a.reshape(-1).astype(F32) for a in arrs])
    n = flat.shape[0]
    rows = -(-n // (8 * LANES)) * 8
    return jnp.pad(flat, (0, rows * LANES - n)).reshape(rows, LANES)


def _unpack(buf, shapes):
    flat = buf.reshape(-1)
    out, off = [], 0
    for shp in shapes:
        n = int(np.prod(shp)) if len(shp) else 1
        out.append(flat[off:off + n].reshape(shp))
        off += n
    return out


def kernel(x, norm_g, final_g, rel_bias, hgrn_lb, ssd_w_in, ssd_conv_w, ssd_conv_b, ssd_dt_bias, ssd_a_log, ssd_d, ssd_norm_g, ssd_w_out, hg_w_in, hg_norm_g, hg_w_out, at_w_in, at_q_norm_g, at_k_norm_g, at_w_out, dl_w_in, dl_w_out, loss_target, m_norm_g, m_final_g, m_rel_bias, m_hgrn_lb, m_ssd_w_in, m_ssd_conv_w, m_ssd_conv_b, m_ssd_dt_bias, m_ssd_a_log, m_ssd_d, m_ssd_norm_g, m_ssd_w_out, m_hg_w_in, m_hg_norm_g, m_hg_w_out, m_at_w_in, m_at_q_norm_g, m_at_k_norm_g, m_at_w_out, m_dl_w_in, m_dl_w_out, v_norm_g, v_final_g, v_rel_bias, v_hgrn_lb, v_ssd_w_in, v_ssd_conv_w, v_ssd_conv_b, v_ssd_dt_bias, v_ssd_a_log, v_ssd_d, v_ssd_norm_g, v_ssd_w_out, v_hg_w_in, v_hg_norm_g, v_hg_w_out, v_at_w_in, v_at_q_norm_g, v_at_k_norm_g, v_at_w_out, v_dl_w_in, v_dl_w_out):
    w = dict(norm_g=norm_g, final_g=final_g, rel_bias=rel_bias, hgrn_lb=hgrn_lb, ssd_w_in=ssd_w_in, ssd_conv_w=ssd_conv_w, ssd_conv_b=ssd_conv_b, ssd_dt_bias=ssd_dt_bias, ssd_a_log=ssd_a_log, ssd_d=ssd_d, ssd_norm_g=ssd_norm_g, ssd_w_out=ssd_w_out, hg_w_in=hg_w_in, hg_norm_g=hg_norm_g, hg_w_out=hg_w_out, at_w_in=at_w_in, at_q_norm_g=at_q_norm_g, at_k_norm_g=at_k_norm_g, at_w_out=at_w_out, dl_w_in=dl_w_in, dl_w_out=dl_w_out)
    m = dict(norm_g=m_norm_g, final_g=m_final_g, rel_bias=m_rel_bias, hgrn_lb=m_hgrn_lb, ssd_w_in=m_ssd_w_in, ssd_conv_w=m_ssd_conv_w, ssd_conv_b=m_ssd_conv_b, ssd_dt_bias=m_ssd_dt_bias, ssd_a_log=m_ssd_a_log, ssd_d=m_ssd_d, ssd_norm_g=m_ssd_norm_g, ssd_w_out=m_ssd_w_out, hg_w_in=m_hg_w_in, hg_norm_g=m_hg_norm_g, hg_w_out=m_hg_w_out, at_w_in=m_at_w_in, at_q_norm_g=m_at_q_norm_g, at_k_norm_g=m_at_k_norm_g, at_w_out=m_at_w_out, dl_w_in=m_dl_w_in, dl_w_out=m_dl_w_out)
    v = dict(norm_g=v_norm_g, final_g=v_final_g, rel_bias=v_rel_bias, hgrn_lb=v_hgrn_lb, ssd_w_in=v_ssd_w_in, ssd_conv_w=v_ssd_conv_w, ssd_conv_b=v_ssd_conv_b, ssd_dt_bias=v_ssd_dt_bias, ssd_a_log=v_ssd_a_log, ssd_d=v_ssd_d, ssd_norm_g=v_ssd_norm_g, ssd_w_out=v_ssd_w_out, hg_w_in=v_hg_w_in, hg_norm_g=v_hg_norm_g, hg_w_out=v_hg_w_out, at_w_in=v_at_w_in, at_q_norm_g=v_at_q_norm_g, at_k_norm_g=v_at_k_norm_g, at_w_out=v_at_w_out, dl_w_in=v_dl_w_in, dl_w_out=v_dl_w_out)
    me = 4 * lax.axis_index("x") + 2 * lax.axis_index("y") + lax.axis_index("c")
    xs = x[0]
    S = xs.shape[0]

    shard2d = {n: w[n][0] for n in BIG}
    wire = {n: shard2d[n].astype(MXU_DTYPE) for n in BIG}

    def ag(names):
        return ("ag", [wire[n] for n in names])

    def assemble(names, blks):
        out = {}
        for n, blk in zip(names, blks):
            r, c = shard2d[n].shape
            out[n] = blk.transpose(1, 0, 2).reshape(r, N_DEV * c) if n in BIG_IN else blk.reshape(N_DEV * r, c)
        return out

    def a2a(names, gw):
        bufs = []
        for n in names:
            r, c = shard2d[n].shape
            bufs.append(gw[n].reshape(r, N_DEV, c).transpose(1, 0, 2) if n in BIG_IN else gw[n].reshape(N_DEV, r, c))
        return ("a2a", bufs)

    ssd_w, hg_w, at_w, dl_w = (["ssd_w_in", "ssd_w_out"], ["hg_w_in", "hg_w_out"], ["at_w_in", "at_w_out"],
                               ["dl_w_in", "dl_w_out"])
    full = assemble(["ssd_w_in"], allgather_two_level_multi([wire["ssd_w_in"]], name="allgather_ssd_w_in"))
    ncw = ssd_conv_w.shape[2]
    nhg = hg_norm_g.shape[1]
    small_shard = jnp.zeros((8, 512), F32)
    small_shard = small_shard.at[:SSD_CONV, :ncw].set(ssd_conv_w[0]).at[SSD_CONV, :nhg].set(hg_norm_g[0])
    small_all = allgather_two_level(small_shard, name="allgather_small_weights")
    conv_w_full = small_all[:, :SSD_CONV, :ncw].transpose(1, 0, 2).reshape(SSD_CONV, N_DEV * ncw)
    hg_norm_full = small_all[:, SSD_CONV, :nhg].reshape(1, N_DEV * nhg)

    p_ssd = dict(w_in=jnp.pad(full["ssd_w_in"], ((0, 0), (0, SSD_IN_PAD - SSD_IN))), w_out=None,
                 w_out_of=lambda got: assemble(["ssd_w_out"], got[-1:])["ssd_w_out"],
                 conv_w=conv_w_full, conv_b=ssd_conv_b,
                 dt_bias=jnp.pad(ssd_dt_bias.reshape(1, 2 * SSD_HEADS), ((0, 0), (0, 128 - 2 * SSD_HEADS))),
                 alog=jnp.pad(ssd_a_log.reshape(1, 2 * SSD_HEADS), ((0, 0), (0, 128 - 2 * SSD_HEADS))),
                 dexp=jnp.repeat(ssd_d.reshape(-1), SSD_HEADDIM)[None, :], norm_g=ssd_norm_g)
    x1, sv0, got, got_at_in = ssd_layer_fwd(xs, norm_g[0:1], p_ssd, comm=ag(hg_w + ["ssd_w_out"]), comm1=ag(["at_w_in"]))
    full.update(assemble(hg_w, got[:2]))
    p_hg = dict(w_in=full["hg_w_in"], w_out=full["hg_w_out"], norm_g=hg_norm_full, hgrn_lb=hgrn_lb)
    x2, sv1, got_at_out, _ = hg_layer_fwd(x1, norm_g[1:2], p_hg, comm0=ag(["at_w_out"]))
    full.update(assemble(at_w, got_at_in + got_at_out))
    p_at = dict(w_in=full["at_w_in"], w_out=full["at_w_out"], q_g=at_q_norm_g, k_g=at_k_norm_g)
    x3, sv2, got_dl = at_layer_fwd(x2, norm_g[2:3], p_at, comm=ag(dl_w))
    full.update(assemble(dl_w, got_dl))
    p_dl = dict(w_in=full["dl_w_in"], w_out=full["dl_w_out"], rel_bias=rel_bias)
    x4, sv3 = dl_layer_fwd(x3, norm_g[3:4], p_dl)
    loss_part, dx4, dfinal = loss_head(x4, final_g[None, :], loss_target[0])
    dx3, g3 = dl_layer_bwd(sv3, p_dl, dx4)
    dx2, g2, recv_dl = at_layer_bwd(sv2, p_at, dx3, comm=a2a(dl_w, dict(dl_w_in=g3["w_in"], dl_w_out=g3["w_out"])))
    dx1, g1, recv_at = hg_layer_bwd(sv1, p_hg, dx2, comm=a2a(at_w, dict(at_w_in=g2["w_in"], at_w_out=g2["w_out"])))
    dx0, g0, recv_hg, recv_ssd = ssd_layer_bwd(
        sv0, p_ssd, dx1, comm=a2a(hg_w, dict(hg_w_in=g1["w_in"], hg_w_out=g1["w_out"])),
        tail_comm=lambda g_in, g_out: (a2a(["ssd_w_in"], dict(ssd_w_in=g_in)) if g_out is None
                                       else a2a(["ssd_w_out"], dict(ssd_w_out=g_out))))
    recv = dict(zip(ssd_w + hg_w + at_w + dl_w, recv_ssd + recv_hg + recv_at + recv_dl))

    small_full = dict(
        norm_g=jnp.concatenate([g0["ng"], g1["ng"], g2["ng"], g3["ng"]], axis=0), final_g=dfinal[0],
        rel_bias=g3["rel_bias"], hgrn_lb=g1["hgrn_lb"], ssd_conv_w=g0["conv_w"][None], ssd_conv_b=g0["conv_b"],
        ssd_dt_bias=g0["dt_bias"].reshape(1, 2, SSD_HEADS), ssd_a_log=g0["a_log"].reshape(1, 2, SSD_HEADS),
        ssd_d=g0["d"], ssd_norm_g=g0["norm_g"], hg_norm_g=g1["norm_g"], at_q_norm_g=g2["q_g"], at_k_norm_g=g2["k_g"])
    packed = _pack([loss_part[0, 0:1]] + [small_full[n] for n in SMALL])
    summed = sum_parts(allgather_two_level(packed, name="allgather_small_grads"), name="sum_small_grads")
    parts = _unpack(summed, [()] + [small_full[n].shape for n in SMALL])
    loss = parts[0]
    gsmall = dict(zip(SMALL, parts[1:]))
    gsmall["ssd_conv_w"] = lax.dynamic_slice_in_dim(gsmall["ssd_conv_w"], me * ncw, ncw, axis=2)
    gsmall["hg_norm_g"] = lax.dynamic_slice_in_dim(gsmall["hg_norm_g"], me * nhg, nhg, axis=1)
    shapes = [w[n].shape for n in SMALL]
    d_p, m_p, v_p = adamw_plain(_pack([w[n] for n in SMALL]), _pack([gsmall[n] for n in SMALL]),
                                _pack([m[n] for n in SMALL]), _pack([v[n] for n in SMALL]), name="adamw_small")
    grads = dict(gsmall)
    deltas = dict(zip(SMALL, _unpack(d_p, shapes)))
    new_m = dict(zip(SMALL, _unpack(m_p, shapes)))
    new_v = dict(zip(SMALL, _unpack(v_p, shapes)))

    for n in BIG:
        gs, ds, ms, vs = adamw_sum(recv[n], shard2d[n], m[n][0], v[n][0], name=f"adamw_{n}")
        grads[n], deltas[n], new_m[n], new_v[n] = gs[None], ds[None], ms[None], vs[None]

    return (loss, dx0[None], *[grads[n] for n in WEIGHT_ORDER], *[deltas[n] for n in WEIGHT_ORDER],
            *[new_m[n] for n in WEIGHT_ORDER], *[new_v[n] for n in WEIGHT_ORDER])
```

```python
import functools
import math

import jax
import jax.numpy as jnp
import numpy as np
from jax import lax
from jax.experimental import pallas as pl
from jax.experimental.pallas import tpu as pltpu

F32 = jnp.float32
MXU_DTYPE = jnp.bfloat16
GRAD_WIRE_DTYPE = jnp.bfloat16
HIGHEST = lax.Precision.HIGHEST
MESH_ID = pl.DeviceIdType.MESH
N_DEV = 8

D_MODEL = 1024
EPS = 1e-6
NEG_BIG = -1e30

SSD_DI = 2048
SSD_HEADDIM = 64
SSD_HEADS = 32
SSD_GROUPS = 4
SSD_HPG = 8
SSD_STATE = 128
SSD_CONV = 7
SSD_CHUNK = 128
SSD_CHUNKS_PER_STEP = 4
SSD_CONV_CH = SSD_DI + 2 * SSD_GROUPS * SSD_STATE
SSD_IN = SSD_DI + SSD_CONV_CH + 2 * SSD_HEADS
SSD_IN_PAD = 5376

HG_CHUNK = 32
HG_HEADS = 8
HG_D = 128
HG_W = 1024

AT_HEADS = 16
AT_KV = 8
AT_HD = 128
AT_QW = 2048
AT_KW = 1024
GRID_W = 64
ROPE_THETA = 10000.0

DL_PAIRS = ((128, 1), (512, 4), (2048, 16))
DL_HEADS = 16
DL_HD = 64
DL_W = 1024
DL_HALF = 64
REL_BUCKETS = 32
REL_MAX_DIST = 1024

ADAM_LR = 0.001
ADAM_B1 = 0.9
ADAM_B2 = 0.999
ADAM_EPS = 1e-08
ADAM_WD = 0.01
ADAM_STEP = 10

VMEM_LIMIT = 56 * 1024 * 1024


def _cp(*sem):
    return pltpu.CompilerParams(dimension_semantics=tuple(sem), vmem_limit_bytes=VMEM_LIMIT)


def _tile(n, cands=(1024, 768, 512, 384, 256, 128)):
    for c in cands:
        if n % c == 0:
            return c
    return n


def _dot(a, b, dims):
    return lax.dot_general(a.astype(MXU_DTYPE), b.astype(MXU_DTYPE), (dims, ((), ())), preferred_element_type=F32)


def _dot_exact(a, b, dims):
    return lax.dot_general(a, b, (dims, ((), ())), precision=HIGHEST, preferred_element_type=F32)


def _silu(x):
    return x * jax.nn.sigmoid(x)


def _my_pos():
    return lax.axis_index("x"), lax.axis_index("y"), lax.axis_index("c")


def _flat(px, py, pc):
    return 4 * px + 2 * py + pc


def _peers():
    x_, y_, c_ = _my_pos()
    out = []
    for k in range(1, N_DEV):
        fx, fy, fc = (k >> 2) & 1, (k >> 1) & 1, k & 1
        out.append(((1 - x_) if fx else x_, (1 - y_) if fy else y_, (1 - c_) if fc else c_))
    return out


def _comm_copies(kind, in_refs, out_refs, send_sems, recv_sems, local_sems):
    me = _flat(*_my_pos())
    local, starts, waits = [], [], []
    for b, (i_ref, o_ref) in enumerate(zip(in_refs, out_refs)):
        local.append(pltpu.make_async_copy(i_ref if kind == "ag" else i_ref.at[me], o_ref.at[me], local_sems.at[b]))
        for k, p in enumerate(_peers()):
            src = i_ref if kind == "ag" else i_ref.at[_flat(*p)]
            starts.append(pltpu.make_async_remote_copy(
                src_ref=src, dst_ref=o_ref.at[me], send_sem=send_sems.at[b, k], recv_sem=recv_sems.at[b, k],
                device_id=p, device_id_type=MESH_ID))
            waits.append(pltpu.make_async_remote_copy(
                src_ref=src, dst_ref=o_ref.at[_flat(*p)], send_sem=send_sems.at[b, k], recv_sem=recv_sems.at[b, k],
                device_id=p, device_id_type=MESH_ID))
    return local, starts, waits


def pcall(body, comm, *, name, grid, in_specs, out_specs, out_shape, scratch_shapes=(), compiler_params=None):
    single = not isinstance(out_specs, (list, tuple))
    out_specs_l = [out_specs] if single else list(out_specs)
    out_shape_l = [out_shape] if single else list(out_shape)
    if comm is None:
        return pl.pallas_call(body, name=name, grid=grid, in_specs=in_specs, out_specs=out_specs, out_shape=out_shape,
                              scratch_shapes=list(scratch_shapes), compiler_params=compiler_params)
    kind, bufs = comm
    nb, n_in, n_out, n_scr = len(bufs), len(in_specs), len(out_specs_l), len(scratch_shapes)
    c_shape = [jax.ShapeDtypeStruct(((N_DEV,) + b.shape) if kind == "ag" else b.shape, b.dtype) for b in bufs]
    anyspec = pl.BlockSpec(memory_space=pl.ANY)

    def body2(*refs):
        ins, c_ins = refs[:n_in], refs[n_in:n_in + nb]
        outs = refs[n_in + nb:n_in + nb + n_out]
        c_outs = refs[n_in + nb + n_out:n_in + 2 * nb + n_out]
        scr = refs[n_in + 2 * nb + n_out:n_in + 2 * nb + n_out + n_scr]
        send_sems, recv_sems, local_sems = refs[n_in + 2 * nb + n_out + n_scr:]
        first = last = None
        for ax, g in enumerate(grid):
            pid = pl.program_id(ax)
            first = (pid == 0) if first is None else jnp.logical_and(first, pid == 0)
            last = (pid == g - 1) if last is None else jnp.logical_and(last, pid == g - 1)

        @pl.when(first)
        def _():
            local, starts, _ = _comm_copies(kind, c_ins, c_outs, send_sems, recv_sems, local_sems)
            for cp in local + starts:
                cp.start()

        body(*ins, *outs, *scr)

        @pl.when(last)
        def _():
            local, _, waits = _comm_copies(kind, c_ins, c_outs, send_sems, recv_sems, local_sems)
            for cp in waits + local:
                cp.wait()

    call = pl.pallas_call(
        body2, name=name, grid=grid, in_specs=list(in_specs) + [anyspec] * nb,
        out_specs=out_specs_l + [anyspec] * nb, out_shape=out_shape_l + c_shape,
        scratch_shapes=list(scratch_shapes) + [pltpu.SemaphoreType.DMA((nb, N_DEV - 1)),
                                               pltpu.SemaphoreType.DMA((nb, N_DEV - 1)), pltpu.SemaphoreType.DMA((nb,))],
        compiler_params=compiler_params)

    def run(*args):
        res = call(*args, *bufs)
        own = res[:n_out]
        return (own[0] if single else list(own)), list(res[n_out:])

    return run


def matmul(a, b, *, name, ta=False, tb=False, residual=None, out_dtype=F32, exact=False, tm=None, tn=None, tk=None,
           b_cols=None, comm=None):
    M, K = (a.shape[1], a.shape[0]) if ta else a.shape
    n0, N = b_cols if b_cols is not None else (0, b.shape[0] if tb else b.shape[1])
    tm = tm or _tile(M, (1024, 512, 256, 128))
    tn = tn or _tile(N, (1024, 768, 512, 384, 256, 128))
    tk = tk or _tile(K, (2048, 1024, 768, 512, 384, 256, 128))
    nk = K // tk
    dims = (((0,) if ta else (1,)), ((1,) if tb else (0,)))

    def body(*refs):
        if residual is None:
            a_ref, b_ref, o_ref, acc = refs
            r_ref = None
        else:
            a_ref, b_ref, r_ref, o_ref, acc = refs
        k = pl.program_id(2)

        @pl.when(k == 0)
        def _():
            acc[...] = jnp.zeros_like(acc)

        if exact:
            acc[...] += _dot_exact(a_ref[...], b_ref[...], dims)
        else:
            acc[...] += _dot(a_ref[...], b_ref[...], dims)

        @pl.when(k == nk - 1)
        def _():
            r = acc[...]
            if r_ref is not None:
                r = r + r_ref[...]
            o_ref[...] = r.astype(o_ref.dtype)

    a_spec = pl.BlockSpec((tk, tm), lambda i, j, k: (k, i)) if ta else pl.BlockSpec((tm, tk), lambda i, j, k: (i, k))
    assert n0 % tn == 0
    jb = n0 // tn
    b_spec = (pl.BlockSpec((tn, tk), lambda i, j, k: (j + jb, k)) if tb
              else pl.BlockSpec((tk, tn), lambda i, j, k: (k, j + jb)))
    in_specs = [a_spec, b_spec]
    args = [a, b]
    if residual is not None:
        in_specs.append(pl.BlockSpec((tm, tn), lambda i, j, k: (i, j)))
        args.append(residual)
    return pcall(
        body, comm, name=name, grid=(M // tm, N // tn, nk), in_specs=in_specs,
        out_specs=pl.BlockSpec((tm, tn), lambda i, j, k: (i, j)),
        out_shape=jax.ShapeDtypeStruct((M, N), out_dtype),
        scratch_shapes=[pltpu.VMEM((tm, tn), F32)],
        compiler_params=_cp("parallel", "parallel", "arbitrary"),
    )(*args)


def _row_specs2(rows, shared, R, W, ncb):
    specs = []
    for arr, col0, per_j, *wd in rows:
        w = wd[0] if wd else W
        if per_j:
            assert col0 % ncb == 0
            specs.append(pl.BlockSpec((R, ncb * w), lambda i, c=col0 // ncb: (i, c)))
        else:
            specs.append(pl.BlockSpec((R, w), lambda i, c=col0: (i, c)))
    for arr, per_j in shared:
        specs.append(pl.BlockSpec((arr.shape[0], ncb * W if per_j else arr.shape[1]), lambda i: (0, 0)))
    return specs


def _col_block(ref, per_j, j, w):
    return ref[:, j * w:(j + 1) * w] if per_j else ref[...]


def rowwise_fwd(name, fn, rows, shared, outs, *, W, ncb=1, R=256):
    S = rows[0][0].shape[0]
    R = min(R, S)
    nr, ns = len(rows), len(shared)
    widths = [(r[3] if len(r) > 3 else W) for r in rows]
    per_j = [r[2] for r in rows] + [s[1] for s in shared]
    ws = widths + [W] * ns

    def body(*refs):
        for j in range(ncb):
            vals = [_col_block(refs[k], per_j[k], j, ws[k]) for k in range(nr + ns)]
            res = fn(*vals)
            for o_ref, r, (wo, _) in zip(refs[nr + ns:], res, outs):
                o_ref[:, j * wo:(j + 1) * wo] = r.astype(o_ref.dtype)

    return pl.pallas_call(
        body, name=name, grid=(S // R,),
        in_specs=_row_specs2(rows, shared, R, W, ncb),
        out_specs=[pl.BlockSpec((R, ncb * w), lambda i: (i, 0)) for w, _ in outs],
        out_shape=[jax.ShapeDtypeStruct((S, ncb * w), dt) for w, dt in outs],
        compiler_params=_cp("parallel"),
    )(*[r[0] for r in rows], *[s[0] for s in shared])


def rowwise_bwd(name, fn, rows, shared, cots, *, W, ncb=1, R=256, diff_rows, diff_shared, add=None):
    S = rows[0][0].shape[0]
    R = min(R, S)
    nr, ns, nc = len(rows), len(shared), len(cots)
    widths = [(r[3] if len(r) > 3 else W) for r in rows]
    per_j = [r[2] for r in rows] + [s[1] for s in shared]
    ws = widths + [W] * ns
    wo = [c.shape[1] // ncb for c in cots]
    dws = [widths[r] for r in diff_rows]

    def body(*refs):
        ins = refs[:nr + ns]
        ct_refs = refs[nr + ns:nr + ns + nc]
        pos = nr + ns + nc
        add_ref = None
        if add is not None:
            add_ref = refs[pos]
            pos += 1
        drow_refs = refs[pos:pos + len(diff_rows)]
        dsh_refs = refs[pos + len(diff_rows):]
        i = pl.program_id(0)
        tot = [None] * len(diff_shared)
        for j in range(ncb):
            vals = [_col_block(ins[k], per_j[k], j, ws[k]) for k in range(nr + ns)]

            def f(*dv):
                full = list(vals)
                for idx, v in zip(list(diff_rows) + [nr + s for s in diff_shared], dv):
                    full[idx] = v
                return tuple(fn(*full))

            prim = [vals[idx] for idx in diff_rows] + [vals[nr + s] for s in diff_shared]
            _, vjp = jax.vjp(f, *prim)
            grads = vjp(tuple(c[:, j * w:(j + 1) * w] for c, w in zip(ct_refs, wo)))
            for k, (d_ref, w) in enumerate(zip(drow_refs, dws)):
                g = grads[k]
                if k == 0 and add_ref is not None:
                    g = g + add_ref[:, j * w:(j + 1) * w]
                d_ref[:, j * w:(j + 1) * w] = g
            for k, (d_ref, s) in enumerate(zip(dsh_refs, diff_shared)):
                g = grads[len(diff_rows) + k]
                if shared[s][1]:
                    @pl.when(i == 0)
                    def _(d_ref=d_ref, g=g, j=j):
                        d_ref[:, j * W:(j + 1) * W] = g

                    @pl.when(i != 0)
                    def _(d_ref=d_ref, g=g, j=j):
                        d_ref[:, j * W:(j + 1) * W] += g
                else:
                    tot[k] = g if tot[k] is None else tot[k] + g
        for k, (d_ref, s) in enumerate(zip(dsh_refs, diff_shared)):
            if not shared[s][1]:
                @pl.when(i == 0)
                def _(d_ref=d_ref, g=tot[k]):
                    d_ref[...] = g

                @pl.when(i != 0)
                def _(d_ref=d_ref, g=tot[k]):
                    d_ref[...] += g

    in_specs = _row_specs2(rows, shared, R, W, ncb)
    in_specs += [pl.BlockSpec((R, ncb * w), lambda i: (i, 0)) for w in wo]
    args = [r[0] for r in rows] + [s[0] for s in shared] + list(cots)
    if add is not None:
        in_specs.append(pl.BlockSpec((R, ncb * dws[0]), lambda i: (i, 0)))
        args.append(add)
    out_specs = [pl.BlockSpec((R, ncb * w), lambda i: (i, 0)) for w in dws]
    out_shape = [jax.ShapeDtypeStruct((S, ncb * w), F32) for w in dws]
    for s in diff_shared:
        arr, pj = shared[s]
        shp = (arr.shape[0], ncb * W if pj else arr.shape[1])
        out_specs.append(pl.BlockSpec(shp, lambda i: (0, 0)))
        out_shape.append(jax.ShapeDtypeStruct(shp, F32))
    return pl.pallas_call(
        body, name=name, grid=(S // R,), in_specs=in_specs, out_specs=out_specs, out_shape=out_shape,
        compiler_params=_cp("arbitrary"),
    )(*args)


def _rms(x, g):
    return x * lax.rsqrt(jnp.mean(x * x, axis=-1, keepdims=True) + EPS) * g


def _prenorm_fn(x, g):
    return (_rms(x, g),)


def loss_head(x, g, tgt, *, R=256):
    S, D = x.shape
    R = min(R, S)

    def fn(xv, gv, tv):
        err = _rms(xv, gv) - tv
        return 0.5 * jnp.sum(jnp.mean(err * err, axis=-1, keepdims=True), axis=0, keepdims=True)

    def body(x_ref, g_ref, t_ref, loss_ref, dx_ref, dg_ref):
        i = pl.program_id(0)
        tv = t_ref[...]
        val, vjp = jax.vjp(lambda a, b: fn(a, b, tv), x_ref[...], g_ref[...])
        dx, dg = vjp(jnp.ones((1, 1), F32))
        dx_ref[...] = dx

        @pl.when(i == 0)
        def _():
            loss_ref[...] = jnp.zeros_like(loss_ref) + val
            dg_ref[...] = dg

        @pl.when(i != 0)
        def _():
            loss_ref[...] += val
            dg_ref[...] += dg

    return pl.pallas_call(
        body, name="loss_head", grid=(S // R,),
        in_specs=[pl.BlockSpec((R, D), lambda i: (i, 0)), pl.BlockSpec((1, D), lambda i: (0, 0)),
                  pl.BlockSpec((R, D), lambda i: (i, 0))],
        out_specs=[pl.BlockSpec((1, 128), lambda i: (0, 0)), pl.BlockSpec((R, D), lambda i: (i, 0)),
                   pl.BlockSpec((1, D), lambda i: (0, 0))],
        out_shape=[jax.ShapeDtypeStruct((1, 128), F32), jax.ShapeDtypeStruct((S, D), F32),
                   jax.ShapeDtypeStruct((1, D), F32)],
        compiler_params=_cp("arbitrary"),
    )(x, g, tgt)


@jax.custom_vjp
def _softplus(x):
    z = jnp.exp(-jnp.abs(x))
    u = 1.0 + z
    log1p = jnp.where(u == 1.0, z, jnp.log(u) * (z / jnp.where(u == 1.0, 1.0, u - 1.0)))
    return jnp.maximum(x, 0.0) + log1p


def _softplus_fwd(x):
    return _softplus(x), x


def _softplus_bwd(x, ct):
    return (ct * jax.nn.sigmoid(x),)


_softplus.defvjp(_softplus_fwd, _softplus_bwd)


def _dt_fn(raw, bias):
    return (_softplus(raw + bias),)


CONV_CB = 128
CONV_RB = 256
CONV_PAD = 8


def ssd_conv_fwd(u, conv_w, conv_b):
    S = u.shape[0]
    ncb = SSD_CONV_CH // CONV_CB
    col0 = SSD_DI // CONV_CB
    RB = min(CONV_RB, S)

    def body(x_ref, w_ref, b_ref, o_ref, pad):
        pad[0:CONV_PAD, :] = jnp.zeros((CONV_PAD, CONV_CB), F32)
        pad[S + CONV_PAD:S + 2 * CONV_PAD, :] = jnp.zeros((CONV_PAD, CONV_CB), F32)
        pad[CONV_PAD:S + CONV_PAD, :] = x_ref[...]
        w = w_ref[...]
        b = b_ref[...]
        for r in range(S // RB):
            acc = jnp.zeros((RB, CONV_CB), F32) + b
            for k in range(SSD_CONV):
                off = r * RB + CONV_PAD + k - SSD_CONV // 2
                acc = acc + pad[off:off + RB, :] * w[k:k + 1, :]
            o_ref[r * RB:(r + 1) * RB, :] = _silu(acc)

    return pl.pallas_call(
        body, name="ssd_conv_fwd", grid=(ncb,),
        in_specs=[pl.BlockSpec((S, CONV_CB), lambda j: (0, col0 + j)),
                  pl.BlockSpec((SSD_CONV, CONV_CB), lambda j: (0, j)),
                  pl.BlockSpec((1, CONV_CB), lambda j: (0, j))],
        out_specs=pl.BlockSpec((S, CONV_CB), lambda j: (0, j)),
        out_shape=jax.ShapeDtypeStruct((S, SSD_CONV_CH), F32),
        scratch_shapes=[pltpu.VMEM((S + 2 * CONV_PAD, CONV_CB), F32)],
        compiler_params=_cp("parallel"),
    )(u, conv_w, conv_b)


def ssd_conv_bwd(u, conv_w, conv_b, dact):
    S = u.shape[0]
    ncb = SSD_CONV_CH // CONV_CB
    col0 = SSD_DI // CONV_CB
    RB = min(CONV_RB, S)
    half = SSD_CONV // 2

    def body(x_ref, w_ref, b_ref, da_ref, dx_ref, dw_ref, db_ref, xpad, dpad):
        z8 = jnp.zeros((CONV_PAD, CONV_CB), F32)
        xpad[0:CONV_PAD, :] = z8
        xpad[S + CONV_PAD:S + 2 * CONV_PAD, :] = z8
        dpad[0:CONV_PAD, :] = z8
        dpad[S + CONV_PAD:S + 2 * CONV_PAD, :] = z8
        xpad[CONV_PAD:S + CONV_PAD, :] = x_ref[...]
        w = w_ref[...]
        b = b_ref[...]
        dws = [jnp.zeros((1, CONV_CB), F32) for _ in range(SSD_CONV)]
        db = jnp.zeros((1, CONV_CB), F32)
        for r in range(S // RB):
            acc = jnp.zeros((RB, CONV_CB), F32) + b
            xs = []
            for k in range(SSD_CONV):
                off = r * RB + CONV_PAD + k - half
                xk = xpad[off:off + RB, :]
                xs.append(xk)
                acc = acc + xk * w[k:k + 1, :]
            sg = jax.nn.sigmoid(acc)
            dc = da_ref[r * RB:(r + 1) * RB, :] * (sg * (1.0 + acc * (1.0 - sg)))
            dpad[r * RB + CONV_PAD:(r + 1) * RB + CONV_PAD, :] = dc
            db = db + jnp.sum(dc, axis=0, keepdims=True)
            for k in range(SSD_CONV):
                dws[k] = dws[k] + jnp.sum(xs[k] * dc, axis=0, keepdims=True)
        for r in range(S // RB):
            acc = jnp.zeros((RB, CONV_CB), F32)
            for k in range(SSD_CONV):
                off = r * RB + CONV_PAD + half - k
                acc = acc + dpad[off:off + RB, :] * w[k:k + 1, :]
            dx_ref[r * RB:(r + 1) * RB, :] = acc
        for k in range(SSD_CONV):
            dw_ref[k:k + 1, :] = dws[k]
        dw_ref[SSD_CONV:SSD_CONV + 1, :] = jnp.zeros((1, CONV_CB), F32)
        db_ref[...] = db

    return pl.pallas_call(
        body, name="ssd_conv_bwd", grid=(ncb,),
        in_specs=[pl.BlockSpec((S, CONV_CB), lambda j: (0, col0 + j)),
                  pl.BlockSpec((SSD_CONV, CONV_CB), lambda j: (0, j)),
                  pl.BlockSpec((1, CONV_CB), lambda j: (0, j)),
                  pl.BlockSpec((S, CONV_CB), lambda j: (0, j))],
        out_specs=[pl.BlockSpec((S, CONV_CB), lambda j: (0, j)),
                   pl.BlockSpec((SSD_CONV + 1, CONV_CB), lambda j: (0, j)),
                   pl.BlockSpec((1, CONV_CB), lambda j: (0, j))],
        out_shape=[jax.ShapeDtypeStruct((S, SSD_CONV_CH), F32),
                   jax.ShapeDtypeStruct((SSD_CONV + 1, SSD_CONV_CH), F32),
                   jax.ShapeDtypeStruct((1, SSD_CONV_CH), F32)],
        scratch_shapes=[pltpu.VMEM((S + 2 * CONV_PAD, CONV_CB), F32), pltpu.VMEM((S + 2 * CONV_PAD, CONV_CB), F32)],
        compiler_params=_cp("parallel"),
    )(u, conv_w, conv_b, dact)


def _ssd_group_layout(t):
    r = t.shape[0]
    g = t[:, :2 * SSD_HEADS].reshape(r, 2, SSD_GROUPS, SSD_HPG).transpose(2, 0, 1, 3).reshape(SSD_GROUPS, r, 2 * SSD_HPG)
    return jnp.pad(g, ((0, 0), (0, 0), (0, 128 - 2 * SSD_HPG)))


def _ssd_head_layout(t):
    r = t.shape[1]
    h = t[:, :, :2 * SSD_HPG].reshape(SSD_GROUPS, r, 2, SSD_HPG).transpose(1, 2, 0, 3).reshape(r, 2 * SSD_HEADS)
    return jnp.pad(h, ((0, 0), (0, 128 - 2 * SSD_HEADS)))


def _ssd_chunk(state, x, Bg, Cg, dt, alog, dtr, alr, *, reverse):
    Q, P = SSD_CHUNK, SSD_HEADDIM
    r = lax.broadcasted_iota(jnp.int32, (Q, Q), 0)
    c = lax.broadcasted_iota(jnp.int32, (Q, Q), 1)
    keep = (c >= r) if reverse else (c <= r)
    cum_t = jnp.transpose(_cumsum_rows(dt * (-jnp.exp(alog)), reverse))
    cum = _cumsum_rows(dtr * (-jnp.exp(alr)), reverse)
    last = 0 if reverse else Q - 1
    cum_l = cum[last:last + 1, :]
    CB = _dot(Cg, Bg, ((1,), (1,)))
    yoff = _dot(Cg, state, ((1,), (0,))) * jnp.exp(cum)
    xdt = x * dtr
    ys = []
    for h in range(SSD_HPG):
        col = h + (SSD_HPG if reverse else 0)
        hs = slice(h * P, (h + 1) * P)
        cum_q = jnp.concatenate([cum[:, hs]] * (Q // P), axis=1)
        L = jnp.where(keep, jnp.exp(jnp.where(keep, cum_q - cum_t[col:col + 1, :], 0.0)), 0.0)
        ys.append(_dot(CB * L, xdt[:, hs], ((1,), (0,))))
    new_state = jnp.exp(cum_l) * state + _dot(Bg, xdt * jnp.exp(cum_l - cum), ((0,), (0,)))
    return new_state, jnp.concatenate(ys, axis=1) + yoff


def ssd_scan_fwd(act, dt, alog, dtr, alr, *, reverse, y_prev=None, comm=None):
    S = act.shape[0]
    Q, N, P = SSD_CHUNK, SSD_STATE, SSD_HEADDIM
    nc = S // Q
    GW = SSD_HPG * P
    CPS = SSD_CHUNKS_PER_STEP if nc % SSD_CHUNKS_PER_STEP == 0 else 1
    nsteps, R = nc // CPS, Q * CPS

    def cidx(i):
        return (nsteps - 1 - i) if reverse else i

    def body(*refs):
        if y_prev is None:
            x_ref, b_ref, c_ref, dt_ref, al_ref, dtr_ref, alr_ref, y_ref, st_ref, state = refs
            yp_ref = None
        else:
            x_ref, b_ref, c_ref, dt_ref, al_ref, dtr_ref, alr_ref, yp_ref, y_ref, st_ref, state = refs
        i = pl.program_id(1)

        @pl.when(i == 0)
        def _():
            state[...] = jnp.zeros_like(state)

        for cc in (range(CPS - 1, -1, -1) if reverse else range(CPS)):
            rs = slice(cc * Q, (cc + 1) * Q)
            st = state[...]
            st_ref[cc, 0] = st
            ns, y = _ssd_chunk(st, x_ref[rs, :], b_ref[rs, :], c_ref[rs, :], dt_ref[0, rs, :], al_ref[0],
                               dtr_ref[rs, :], alr_ref[...], reverse=reverse)
            state[...] = ns
            y_ref[rs, :] = y if yp_ref is None else y + yp_ref[rs, :]

    xspec = pl.BlockSpec((R, GW), lambda g, i: (cidx(i), g))
    in_specs = [xspec,
                pl.BlockSpec((R, N), lambda g, i: (cidx(i), SSD_DI // N + g)),
                pl.BlockSpec((R, N), lambda g, i: (cidx(i), SSD_DI // N + SSD_GROUPS + g)),
                pl.BlockSpec((1, R, 128), lambda g, i: (g, cidx(i), 0)),
                pl.BlockSpec((1, 1, 128), lambda g, i: (g, 0, 0)),
                xspec, pl.BlockSpec((1, GW), lambda g, i: (0, g))]
    args = [act, act, act, dt, alog, dtr, alr]
    if y_prev is not None:
        in_specs.append(xspec)
        args.append(y_prev)
    return pcall(
        body, comm, name=f"ssd_scan_fwd_{int(reverse)}", grid=(SSD_GROUPS, nsteps), in_specs=in_specs,
        out_specs=[xspec, pl.BlockSpec((CPS, 1, N, GW), lambda g, i: (cidx(i), g, 0, 0))],
        out_shape=[jax.ShapeDtypeStruct((S, SSD_DI), F32), jax.ShapeDtypeStruct((nc, SSD_GROUPS, N, GW), F32)],
        scratch_shapes=[pltpu.VMEM((N, GW), F32)],
        compiler_params=_cp("arbitrary", "arbitrary"),
    )(*args)


def ssd_scan_bwd(act, dt, alog, dtr, alr, states, dy, prev_x, *, reverse, prev=None, comm=None):
    S = act.shape[0]
    Q, N, P = SSD_CHUNK, SSD_STATE, SSD_HEADDIM
    nc = S // Q
    GW = SSD_HPG * P
    CPS = SSD_CHUNKS_PER_STEP if nc % SSD_CHUNKS_PER_STEP == 0 else 1
    nsteps, R = nc // CPS, Q * CPS

    def cidx(i):
        return i if reverse else (nsteps - 1 - i)

    def body(*refs):
        x_ref, b_ref, c_ref, dt_ref, al_ref, dtr_ref, alr_ref, st_ref, dy_ref, px_ref = refs[:10]
        pos = 10
        if prev is not None:
            pb_ref, pc_ref, pdt_ref, pal_ref = refs[pos:pos + 4]
            pos += 4
        dx_ref, db_ref, dc_ref, ddt_ref, dal_ref, ddtr_ref, dalr_ref, dstate = refs[pos:]
        i = pl.program_id(1)

        @pl.when(i == 0)
        def _():
            dstate[...] = jnp.zeros_like(dstate)

        dal_tot, dalr_tot = None, None
        for cc in (range(CPS) if reverse else range(CPS - 1, -1, -1)):
            rs = slice(cc * Q, (cc + 1) * Q)
            _, vjp = jax.vjp(functools.partial(_ssd_chunk, reverse=reverse), st_ref[cc, 0], x_ref[rs, :], b_ref[rs, :],
                             c_ref[rs, :], dt_ref[0, rs, :], al_ref[0], dtr_ref[rs, :], alr_ref[...])
            dst, dx, dB, dC, ddt, dal, ddtr, dalr = vjp((dstate[...], dy_ref[rs, :]))
            dstate[...] = dst
            dx_ref[rs, :] = dx + px_ref[rs, :]
            if prev is not None:
                dB = dB + pb_ref[rs, :]
                dC = dC + pc_ref[rs, :]
                ddt = ddt + pdt_ref[0, rs, :]
            db_ref[rs, :] = dB
            dc_ref[rs, :] = dC
            ddt_ref[0, rs, :] = ddt
            ddtr_ref[rs, :] = ddtr
            dal_tot = dal if dal_tot is None else dal_tot + dal
            dalr_tot = dalr if dalr_tot is None else dalr_tot + dalr

        @pl.when(i == 0)
        def _():
            dal_ref[0] = dal_tot + (pal_ref[0] if prev is not None else 0.0)
            dalr_ref[...] = dalr_tot

        @pl.when(i != 0)
        def _():
            dal_ref[0] += dal_tot
            dalr_ref[...] += dalr_tot

    xspec = pl.BlockSpec((R, GW), lambda g, i: (cidx(i), g))
    gspec = pl.BlockSpec((R, N), lambda g, i: (cidx(i), g))
    dtspec = pl.BlockSpec((1, R, 128), lambda g, i: (g, cidx(i), 0))
    alspec = pl.BlockSpec((1, 1, 128), lambda g, i: (g, 0, 0))
    alrspec = pl.BlockSpec((1, GW), lambda g, i: (0, g))
    in_specs = [xspec,
                pl.BlockSpec((R, N), lambda g, i: (cidx(i), SSD_DI // N + g)),
                pl.BlockSpec((R, N), lambda g, i: (cidx(i), SSD_DI // N + SSD_GROUPS + g)),
                dtspec, alspec, xspec, alrspec,
                pl.BlockSpec((CPS, 1, N, GW), lambda g, i: (cidx(i), g, 0, 0)), xspec, xspec]
    args = [act, act, act, dt, alog, dtr, alr, states, dy, prev_x]
    if prev is not None:
        in_specs += [gspec, gspec, dtspec, alspec]
        args += list(prev)
    return pcall(
        body, comm, name=f"ssd_scan_bwd_{int(reverse)}", grid=(SSD_GROUPS, nsteps), in_specs=in_specs,
        out_specs=[xspec, gspec, gspec, dtspec, alspec, xspec, alrspec],
        out_shape=[jax.ShapeDtypeStruct((S, SSD_DI), F32), jax.ShapeDtypeStruct((S, SSD_GROUPS * N), F32),
                   jax.ShapeDtypeStruct((S, SSD_GROUPS * N), F32),
                   jax.ShapeDtypeStruct((SSD_GROUPS, S, 128), F32), jax.ShapeDtypeStruct((SSD_GROUPS, 1, 128), F32),
                   jax.ShapeDtypeStruct((S, SSD_DI), F32), jax.ShapeDtypeStruct((1, SSD_DI), F32)],
        scratch_shapes=[pltpu.VMEM((N, GW), F32)],
        compiler_params=_cp("arbitrary", "arbitrary"),
    )(*args)


def _ssd_post_fn(y, xs, z, dexp, ng):
    t = (y + xs * dexp) * _silu(z)
    return (_rms(t, ng),)


def _cumsum_rows_impl(x, reverse):
    n = x.shape[0]
    row = lax.broadcasted_iota(jnp.int32, x.shape, 0)
    k = 1
    while k < n:
        if reverse:
            x = x + jnp.where(row < n - k, pltpu.roll(x, n - k, 0), 0.0)
        else:
            x = x + jnp.where(row >= k, pltpu.roll(x, k, 0), 0.0)
        k *= 2
    return x


@functools.partial(jax.custom_vjp, nondiff_argnums=(1,))
def _cumsum_rows(x, reverse):
    return _cumsum_rows_impl(x, reverse)


_cumsum_rows.defvjp(lambda x, reverse: (_cumsum_rows_impl(x, reverse), None),
                    lambda reverse, _, ct: (_cumsum_rows_impl(ct, not reverse),))


def _hg_chunk(state, qraw, fraw, v, lb, *, reverse):
    C = HG_CHUNK
    r = lax.broadcasted_iota(jnp.int32, (C, C), 0)
    c = lax.broadcasted_iota(jnp.int32, (C, C), 1)
    keep = (c >= r) if reverse else (c <= r)
    q = _silu(qraw)
    f = lb + (1.0 - lb) * jax.nn.sigmoid(fraw)
    k = 1.0 - f
    g = jnp.log(f)
    G = _cumsum_rows(g, reverse)
    ref_row = C // 2 - 1 if reverse else C // 2
    last_row = 0 if reverse else C - 1
    Gr = G[ref_row:ref_row + 1, :]
    Gl = G[last_row:last_row + 1, :]
    q_t = q * jnp.exp(G - Gr)
    k_t = k * jnp.exp(Gr - G)
    att = jnp.where(keep, _dot(q_t, k_t, ((1,), (1,))), 0.0)
    o = _dot(att, v, ((1,), (0,))) + _dot(q * jnp.exp(G), state, ((1,), (0,)))
    kd = k * jnp.exp(Gl - G)
    new_state = jnp.transpose(jnp.exp(Gl)) * state + _dot(kd, v, ((0,), (0,)))
    return new_state, o


def hg_scan_fwd(u, lb, *, reverse, o_prev=None, rows=256, comm=None):
    S = u.shape[0]
    nh = HG_HEADS
    rows = min(rows, S)
    nsteps = S // rows
    ncb = rows // HG_CHUNK
    f_sec = 2 if reverse else 1

    def blk(i):
        return (nsteps - 1 - i) if reverse else i

    def body(*refs):
        if o_prev is None:
            q_ref, f_ref, v_ref, lb_ref, o_ref, st_ref, state = refs
            op_ref = None
        else:
            q_ref, f_ref, v_ref, lb_ref, op_ref, o_ref, st_ref, state = refs
        i = pl.program_id(0)

        @pl.when(i == 0)
        def _():
            state[...] = jnp.zeros_like(state)

        def chunk(cc, carry):
            ci = (ncb - 1 - cc) if reverse else cc
            sl = pl.ds(pl.multiple_of(ci * HG_CHUNK, HG_CHUNK), HG_CHUNK)
            for h in range(nh):
                hs = slice(h * HG_D, (h + 1) * HG_D)
                st = state[h]
                st_ref[ci, h] = st
                ns, o = _hg_chunk(st, q_ref[sl, hs], f_ref[sl, hs], v_ref[sl, hs], lb_ref[:, hs], reverse=reverse)
                state[h] = ns
                if op_ref is not None:
                    o = o + op_ref[sl, hs]
                o_ref[sl, hs] = o
            return carry

        lax.fori_loop(0, ncb, chunk, 0)

    rowspec = lambda sec: pl.BlockSpec((rows, HG_W), lambda i: (blk(i), sec))
    in_specs = [rowspec(0), rowspec(f_sec), rowspec(3), pl.BlockSpec((1, HG_W), lambda i: (0, 0))]
    args = [u, u, u, lb]
    if o_prev is not None:
        in_specs.append(rowspec(0))
        args.append(o_prev)
    return pcall(
        body, comm, name=f"hg_scan_fwd_{int(reverse)}", grid=(nsteps,), in_specs=in_specs,
        out_specs=[rowspec(0), pl.BlockSpec((ncb, nh, HG_D, HG_D), lambda i: (blk(i), 0, 0, 0))],
        out_shape=[jax.ShapeDtypeStruct((S, HG_W), F32), jax.ShapeDtypeStruct((S // HG_CHUNK, nh, HG_D, HG_D), F32)],
        scratch_shapes=[pltpu.VMEM((nh, HG_D, HG_D), F32)],
        compiler_params=_cp("arbitrary"),
    )(*args)


def hg_scan_bwd(u, lb, states, do, *, reverse, prev=None, rows=256, comm=None):
    S = u.shape[0]
    nh = HG_HEADS
    rows = min(rows, S)
    nsteps = S // rows
    ncb = rows // HG_CHUNK
    f_sec = 2 if reverse else 1

    def blk(i):
        return i if reverse else (nsteps - 1 - i)

    def body(*refs):
        q_ref, f_ref, v_ref, lb_ref, st_ref, do_ref = refs[:6]
        pos = 6
        if prev is not None:
            pq_ref, pv_ref, plb_ref = refs[pos:pos + 3]
            pos += 3
        dq_ref, df_ref, dv_ref, dlb_ref, dstate = refs[pos:]
        i = pl.program_id(0)

        @pl.when(i == 0)
        def _():
            dstate[...] = jnp.zeros_like(dstate)
            dlb_ref[...] = plb_ref[...] if prev is not None else jnp.zeros_like(dlb_ref)

        def chunk(cc, carry):
            ci = cc if reverse else (ncb - 1 - cc)
            sl = pl.ds(pl.multiple_of(ci * HG_CHUNK, HG_CHUNK), HG_CHUNK)
            for h in range(nh):
                hs = slice(h * HG_D, (h + 1) * HG_D)
                _, vjp = jax.vjp(functools.partial(_hg_chunk, reverse=reverse), st_ref[ci, h],
                                 q_ref[sl, hs], f_ref[sl, hs], v_ref[sl, hs], lb_ref[:, hs])
                dst, dq, df, dv, dlb = vjp((dstate[h], do_ref[sl, hs]))
                dstate[h] = dst
                if prev is not None:
                    dq = dq + pq_ref[sl, hs]
                    dv = dv + pv_ref[sl, hs]
                dq_ref[sl, hs] = dq
                df_ref[sl, hs] = df
                dv_ref[sl, hs] = dv
                dlb_ref[:, hs] += dlb
            return carry

        lax.fori_loop(0, ncb, chunk, 0)

    rowspec = lambda sec: pl.BlockSpec((rows, HG_W), lambda i: (blk(i), sec))
    lbspec = pl.BlockSpec((1, HG_W), lambda i: (0, 0))
    in_specs = [rowspec(0), rowspec(f_sec), rowspec(3), lbspec,
                pl.BlockSpec((ncb, nh, HG_D, HG_D), lambda i: (blk(i), 0, 0, 0)), rowspec(0)]
    args = [u, u, u, lb, states, do]
    if prev is not None:
        in_specs += [rowspec(0), rowspec(0), lbspec]
        args += list(prev)
    return pcall(
        body, comm, name=f"hg_scan_bwd_{int(reverse)}", grid=(nsteps,), in_specs=in_specs,
        out_specs=[rowspec(0), rowspec(0), rowspec(0), lbspec],
        out_shape=[jax.ShapeDtypeStruct((S, HG_W), F32)] * 3 + [jax.ShapeDtypeStruct((1, HG_W), F32)],
        scratch_shapes=[pltpu.VMEM((nh, HG_D, HG_D), F32)],
        compiler_params=_cp("arbitrary"),
    )(*args)


def _hg_lb_fn(lbp):
    m = jnp.max(lbp, axis=0, keepdims=True)
    e = jnp.exp(lbp - m)
    sm = e / jnp.sum(e, axis=0, keepdims=True)
    return ((sm[0:1] + sm[1:2]) - sm[0:1],)


def hg_lb_fwd(lbp):
    def body(x_ref, o_ref):
        o_ref[...] = _hg_lb_fn(x_ref[...])[0]

    return pl.pallas_call(body, name="hg_lb_fwd", out_shape=jax.ShapeDtypeStruct((1, HG_W), F32))(lbp)


def hg_lb_bwd(lbp, dlb):
    def body(x_ref, d_ref, o_ref):
        _, vjp = jax.vjp(_hg_lb_fn, x_ref[...])
        o_ref[...] = vjp((d_ref[...],))[0]

    return pl.pallas_call(body, name="hg_lb_bwd", out_shape=jax.ShapeDtypeStruct(lbp.shape, F32))(lbp, dlb)


def _hg_post_fn(o, gate, ng):
    return (_rms(o, ng) * _silu(gate),)


def _gate_fn(o, gate):
    return (o * _silu(gate),)


def _rope_tables(S):
    t = np.arange(S)
    row = (t // GRID_W).astype(np.float32)
    col = (t % GRID_W).astype(np.float32)
    half = AT_HD // 4
    inv = (ROPE_THETA ** (-np.arange(0, 2 * half, 2, dtype=np.float32) / np.float32(2 * half))).astype(np.float32)
    ar = row[:, None] * inv[None, :]
    ac = col[:, None] * inv[None, :]
    return ar.astype(np.float32), ac.astype(np.float32)


@jax.custom_vjp
def _half_swap(x):
    ax = x.ndim - 1
    lane = lax.broadcasted_iota(jnp.int32, x.shape, ax)
    return jnp.where((lane & 32) == 0, pltpu.roll(x, 96, ax), pltpu.roll(x, 32, ax))


_half_swap.defvjp(lambda x: (_half_swap(x), None), lambda _, ct: (_half_swap(ct),))


def _make_qk_fn(scale):
    def fn(x, ct, st, g):
        n = _rms(x, g)
        return ((n * ct + _half_swap(n) * st) * scale,)
    return fn


def flash_fwd(q, k, v, *, v_col0=0, tq=256, comm=None):
    S = q.shape[0]
    tq = min(tq, S)
    G = AT_HEADS // AT_KV

    def body(q_ref, k_ref, v_ref, o_ref, lse_ref):
        kv, vv = k_ref[...], v_ref[...]
        for g in range(G):
            sl = slice(g * AT_HD, (g + 1) * AT_HD)
            s = _dot(q_ref[:, sl], kv, ((1,), (1,)))
            m = jnp.max(s, axis=1, keepdims=True)
            p = jnp.exp(s - m)
            l = jnp.sum(p, axis=1, keepdims=True)
            o_ref[:, sl] = _dot(p, vv, ((1,), (0,))) / l
            lse_ref[0, :, g:g + 1] = m + jnp.log(l)

    return pcall(
        body, comm, name="flash_fwd", grid=(AT_KV, S // tq),
        in_specs=[pl.BlockSpec((tq, G * AT_HD), lambda h, i: (i, h)),
                  pl.BlockSpec((S, AT_HD), lambda h, i: (0, h)),
                  pl.BlockSpec((S, AT_HD), lambda h, i: (0, v_col0 + h))],
        out_specs=[pl.BlockSpec((tq, G * AT_HD), lambda h, i: (i, h)),
                   pl.BlockSpec((1, tq, G), lambda h, i: (h, i, 0))],
        out_shape=[jax.ShapeDtypeStruct((S, AT_QW), F32), jax.ShapeDtypeStruct((AT_KV, S, G), F32)],
        compiler_params=_cp("parallel", "arbitrary"),
    )(q, k, v)


def flash_bwd_dq(q, k, v, o, lse, do, *, v_col0=0, tq=256):
    S = q.shape[0]
    tq = min(tq, S)
    G = AT_HEADS // AT_KV

    def body(q_ref, k_ref, v_ref, o_ref, lse_ref, do_ref, dq_ref, dl_ref):
        kv, vv = k_ref[...], v_ref[...]
        for g in range(G):
            sl = slice(g * AT_HD, (g + 1) * AT_HD)
            dog = do_ref[:, sl]
            delta = jnp.sum(dog * o_ref[:, sl], axis=1, keepdims=True)
            s = _dot(q_ref[:, sl], kv, ((1,), (1,)))
            p = jnp.exp(s - lse_ref[0, :, g:g + 1])
            dp = _dot(dog, vv, ((1,), (1,)))
            ds = p * (dp - delta)
            dq_ref[:, sl] = _dot(ds, kv, ((1,), (0,)))
            dl_ref[0, :, g:g + 1] = delta

    qspec = pl.BlockSpec((tq, G * AT_HD), lambda h, i: (i, h))
    lspec = pl.BlockSpec((1, tq, G), lambda h, i: (h, i, 0))
    return pl.pallas_call(
        body, name="flash_bwd_dq", grid=(AT_KV, S // tq),
        in_specs=[qspec, pl.BlockSpec((S, AT_HD), lambda h, i: (0, h)),
                  pl.BlockSpec((S, AT_HD), lambda h, i: (0, v_col0 + h)), qspec, lspec, qspec],
        out_specs=[qspec, lspec],
        out_shape=[jax.ShapeDtypeStruct((S, AT_QW), F32), jax.ShapeDtypeStruct((AT_KV, S, G), F32)],
        compiler_params=_cp("parallel", "arbitrary"),
    )(q, k, v, o, lse, do)


def flash_bwd_dkv(q, k, v, lse, delta, do, *, v_col0=0, tk=512, comm=None):
    S = q.shape[0]
    tk = min(tk, S)
    G = AT_HEADS // AT_KV

    def body(q_ref, k_ref, v_ref, lse_ref, dl_ref, do_ref, dk_ref, dv_ref):
        kv, vv = k_ref[...], v_ref[...]
        dk = jnp.zeros((tk, AT_HD), F32)
        dv = jnp.zeros((tk, AT_HD), F32)
        for g in range(G):
            sl = slice(g * AT_HD, (g + 1) * AT_HD)
            qg, dog = q_ref[:, sl], do_ref[:, sl]
            s = _dot(qg, kv, ((1,), (1,)))
            p = jnp.exp(s - lse_ref[0, :, g:g + 1])
            dv = dv + _dot(p, dog, ((0,), (0,)))
            dp = _dot(dog, vv, ((1,), (1,)))
            ds = p * (dp - dl_ref[0, :, g:g + 1])
            dk = dk + _dot(ds, qg, ((0,), (0,)))
        dk_ref[...] = dk
        dv_ref[...] = dv

    qspec = pl.BlockSpec((S, G * AT_HD), lambda h, j: (0, h))
    kspec = pl.BlockSpec((tk, AT_HD), lambda h, j: (j, h))
    lspec = pl.BlockSpec((1, S, G), lambda h, j: (h, 0, 0))
    return pcall(
        body, comm, name="flash_bwd_dkv", grid=(AT_KV, S // tk),
        in_specs=[qspec, kspec, pl.BlockSpec((tk, AT_HD), lambda h, j: (j, v_col0 + h)), lspec, lspec, qspec],
        out_specs=[kspec, kspec],
        out_shape=[jax.ShapeDtypeStruct((S, AT_KW), F32), jax.ShapeDtypeStruct((S, AT_KW), F32)],
        compiler_params=_cp("parallel", "arbitrary"),
    )(q, k, v, lse, delta, do)


def _t5_bucket_np(rel):
    half = REL_BUCKETS // 2
    exact = half // 2
    n = np.abs(rel)
    large = exact + (np.log(np.maximum(n, 1).astype(np.float32) / np.float32(exact))
                     / np.float32(math.log(REL_MAX_DIST / exact)) * np.float32(half - exact)).astype(np.int32)
    large = np.minimum(large, half - 1)
    return np.where(rel > 0, half, 0) + np.where(n < exact, n, large)


DL_TB = 256


def _dl_tiles(Ls, T=128):
    T = min(T, Ls)
    return T, T + 2 * DL_HALF


def _dl_bucket_tables(dil, T):
    W = T + 2 * DL_HALF
    i = np.arange(T)[:, None]
    j = np.arange(W)[None, :]
    bq = _t5_bucket_np((j - DL_HALF - i) * dil)
    iw = np.arange(W)[:, None]
    jk = np.arange(T)[None, :]
    bk = _t5_bucket_np((jk + DL_HALF - iw) * dil)
    return bq.astype(np.int32), bk.astype(np.int32)


def _dl_merge_fn(o0, o1, o2, l0, l1, l2):
    m = jnp.maximum(jnp.maximum(l0, l1), l2)
    e0, e1, e2 = jnp.exp(l0 - m), jnp.exp(l1 - m), jnp.exp(l2 - m)
    den = e0 + e1 + e2
    return ((e0 / den) * o0 + (e1 / den) * o1 + (e2 / den) * o2,)


def _adamw_math(w, g, m, v):
    m = ADAM_B1 * m + (1.0 - ADAM_B1) * g
    v = ADAM_B2 * v + (1.0 - ADAM_B2) * (g * g)
    m_hat = m / (1.0 - ADAM_B1 ** ADAM_STEP)
    v_hat = v / (1.0 - ADAM_B2 ** ADAM_STEP)
    delta = -ADAM_LR * (m_hat / (jnp.sqrt(v_hat) + ADAM_EPS) + ADAM_WD * w)
    return delta, m, v


def adamw_sum(parts, w, m, v, *, name, R=128):
    rows, cols = w.shape
    R = min(R, rows)
    if rows % R:
        R = rows

    def body(p_ref, w_ref, m_ref, v_ref, g_ref, d_ref, nm_ref, nv_ref):
        g = p_ref[0].astype(F32)
        for s in range(1, N_DEV):
            g = g + p_ref[s].astype(F32)
        d, nm, nv = _adamw_math(w_ref[...], g, m_ref[...], v_ref[...])
        g_ref[...] = g
        d_ref[...] = d
        nm_ref[...] = nm
        nv_ref[...] = nv

    spec = pl.BlockSpec((R, cols), lambda i: (i, 0))
    return pl.pallas_call(
        body, name=name, grid=(rows // R,),
        in_specs=[pl.BlockSpec((N_DEV, R, cols), lambda i: (0, i, 0)), spec, spec, spec],
        out_specs=[spec] * 4, out_shape=[jax.ShapeDtypeStruct((rows, cols), F32)] * 4,
        compiler_params=_cp("parallel"),
    )(parts, w, m, v)


def sum_parts(parts, *, name):
    rows, cols = parts.shape[1:]

    def body(p_ref, o_ref):
        g = p_ref[0]
        for s in range(1, N_DEV):
            g = g + p_ref[s]
        o_ref[...] = g

    return pl.pallas_call(body, name=name, out_shape=jax.ShapeDtypeStruct((rows, cols), F32))(parts)


def adamw_plain(w, g, m, v, *, name):
    def body(w_ref, g_ref, m_ref, v_ref, d_ref, nm_ref, nv_ref):
        d, nm, nv = _adamw_math(w_ref[...], g_ref[...], m_ref[...], v_ref[...])
        d_ref[...] = d
        nm_ref[...] = nm
        nv_ref[...] = nv

    return pl.pallas_call(body, name=name, out_shape=[jax.ShapeDtypeStruct(w.shape, F32)] * 3)(w, g, m, v)


def allgather_two_level(x, *, name):
    R, C = x.shape

    def body(x_ref, out_ref, send_sems, recv_sems, local_sem):
        x_, y_, c_ = _my_pos()
        me, sibling = (x_, y_, c_), (x_, y_, 1 - c_)
        chips = [(1 - x_, y_), (x_, 1 - y_), (1 - x_, 1 - y_)]

        def rows(p):
            return out_ref.at[_flat(*p)]

        def copy(k, block, to, src=None):
            return pltpu.make_async_remote_copy(
                src_ref=rows(block) if src is None else src, dst_ref=rows(block),
                send_sem=send_sems.at[k], recv_sem=recv_sems.at[k], device_id=to, device_id_type=MESH_ID)

        mine = pltpu.make_async_copy(x_ref, rows(me), local_sem)
        mine.start()
        first = [copy(0, me, sibling, src=x_ref)]
        first += [copy(1 + j, me, (*chip, c_), src=x_ref) for j, chip in enumerate(chips)]
        for cp in first:
            cp.start()
        passed = [copy(4 + j, (*chip, c_), sibling) for j, chip in enumerate(chips)]
        for j, chip in enumerate(chips):
            copy(1 + j, (*chip, c_), me).wait_recv()
            passed[j].start()
        copy(0, sibling, me).wait_recv()
        for j, chip in enumerate(chips):
            copy(4 + j, (*chip, 1 - c_), me).wait_recv()
        for cp in first + passed:
            cp.wait_send()
        mine.wait()

    return pl.pallas_call(
        body, name=name,
        out_shape=jax.ShapeDtypeStruct((N_DEV, R, C), x.dtype),
        in_specs=[pl.BlockSpec(memory_space=pl.ANY)],
        out_specs=pl.BlockSpec(memory_space=pl.ANY),
        scratch_shapes=[pltpu.SemaphoreType.DMA((7,)), pltpu.SemaphoreType.DMA((7,)), pltpu.SemaphoreType.DMA],
    )(x)


def allgather_two_level_multi(xs, *, name):
    nb = len(xs)

    def body(*refs):
        x_refs, out_refs = refs[:nb], refs[nb:2 * nb]
        send_sems, recv_sems, local_sems = refs[2 * nb:]
        x_, y_, c_ = _my_pos()
        me, sibling = (x_, y_, c_), (x_, y_, 1 - c_)
        chips = [(1 - x_, y_), (x_, 1 - y_), (1 - x_, 1 - y_)]

        def copy(b, k, block, to, own=False):
            rows = out_refs[b].at[_flat(*block)]
            return pltpu.make_async_remote_copy(
                src_ref=x_refs[b] if own else rows, dst_ref=rows,
                send_sem=send_sems.at[b, k], recv_sem=recv_sems.at[b, k], device_id=to, device_id_type=MESH_ID)

        mine = [pltpu.make_async_copy(x_refs[b], out_refs[b].at[_flat(*me)], local_sems.at[b]) for b in range(nb)]
        first = []
        for b in range(nb):
            first.append(copy(b, 0, me, sibling, own=True))
            first += [copy(b, 1 + j, me, (*chip, c_), own=True) for j, chip in enumerate(chips)]
        for cp in mine + first:
            cp.start()
        passed = []
        for j, chip in enumerate(chips):
            for b in range(nb):
                copy(b, 1 + j, (*chip, c_), me).wait_recv()
                fwd = copy(b, 4 + j, (*chip, c_), sibling)
                fwd.start()
                passed.append(fwd)
        for b in range(nb):
            copy(b, 0, sibling, me).wait_recv()
            for j, chip in enumerate(chips):
                copy(b, 4 + j, (*chip, 1 - c_), me).wait_recv()
        for cp in first + passed:
            cp.wait_send()
        for cp in mine:
            cp.wait()

    anyspec = pl.BlockSpec(memory_space=pl.ANY)
    return pl.pallas_call(
        body, name=name,
        out_shape=[jax.ShapeDtypeStruct((N_DEV,) + x.shape, x.dtype) for x in xs],
        in_specs=[anyspec] * nb, out_specs=[anyspec] * nb,
        scratch_shapes=[pltpu.SemaphoreType.DMA((nb, 7)), pltpu.SemaphoreType.DMA((nb, 7)), pltpu.SemaphoreType.DMA((nb,))],
    )(*xs)


def _prenorm(tag, x, ng):
    return rowwise_fwd(f"{tag}_prenorm", _prenorm_fn, [(x, 0, False)], [(ng, False)], [(D_MODEL, MXU_DTYPE)], W=D_MODEL)[0]


def _cat_mxu(parts):
    return jnp.concatenate([t.astype(MXU_DTYPE) for t in parts], axis=1)


def _in_out_bwd(tag, x, ng, hn, du, w_in, dx, tail_comm=None):
    dw_in = matmul(hn, du, ta=True, out_dtype=GRAD_WIRE_DTYPE, name=f"{tag}_dw_in")
    comm = None if tail_comm is None else tail_comm(dw_in)
    dhn, cres = _own(matmul(du, w_in, tb=True, name=f"{tag}_dhn", comm=comm), comm)
    dx_prev, dng = rowwise_bwd(f"{tag}_prenorm_bwd", _prenorm_fn, [(x, 0, False)], [(ng, False)], [dhn],
                               W=D_MODEL, diff_rows=[0], diff_shared=[0], add=dx)
    return (dx_prev, dng, dw_in) if tail_comm is None else (dx_prev, dng, dw_in, cres)


def _own(res, comm):
    return (res, None) if comm is None else res


def ssd_layer_fwd(x, ng, p, comm=None, comm1=None):
    hn = _prenorm("ssd", x, ng)
    u = matmul(hn, p["w_in"], name="ssd_in")
    act = ssd_conv_fwd(u, p["conv_w"], p["conv_b"])
    dt = rowwise_fwd("ssd_dt", _dt_fn, [(u, (SSD_DI + SSD_CONV_CH) // 128, False)], [(p["dt_bias"], False)],
                     [(128, F32)], W=128)[0]
    H = SSD_HEADS
    dtr = [jnp.repeat(dt[:, d * H:(d + 1) * H], SSD_HEADDIM, axis=1) for d in (0, 1)]
    alr = [jnp.repeat(p["alog"][:, d * H:(d + 1) * H], SSD_HEADDIM, axis=1) for d in (0, 1)]
    dt, alog = _ssd_group_layout(dt), _ssd_group_layout(p["alog"])
    (y0, st0), cres = _own(ssd_scan_fwd(act, dt, alog, dtr[0], alr[0], reverse=False, comm=comm), comm)
    (y, st1), cres1 = _own(ssd_scan_fwd(act, dt, alog, dtr[1], alr[1], reverse=True, y_prev=y0, comm=comm1), comm1)
    g = rowwise_fwd("ssd_post", _ssd_post_fn, [(y, 0, True), (act, 0, True), (u, 0, True)],
                    [(p["dexp"], True), (p["norm_g"], True)], [(512, MXU_DTYPE)], W=512, ncb=SSD_GROUPS)[0]
    xn = matmul(g, p["w_out"], residual=x, name="ssd_out")
    return xn, dict(x=x, ng=ng, hn=hn, u=u, act=act, dt=dt, alog=alog, dtr=dtr, alr=alr, y=y, st0=st0, st1=st1,
                    g=g), cres, cres1


def ssd_layer_bwd(sv, p, dx, comm=None, tail_comm=None):
    u, act, dt = sv["u"], sv["act"], sv["dt"]
    S = u.shape[0]
    dg = matmul(dx, p["w_out"], tb=True, name="ssd_dg")
    dw_out = matmul(sv["g"], dx, ta=True, out_dtype=GRAD_WIRE_DTYPE, name="ssd_dw_out")
    dy, dxs_skip, dz, ddexp, dnorm = rowwise_bwd(
        "ssd_post_bwd", _ssd_post_fn, [(sv["y"], 0, True), (act, 0, True), (u, 0, True)],
        [(p["dexp"], True), (p["norm_g"], True)], [dg], W=512, ncb=SSD_GROUPS, diff_rows=[0, 1, 2], diff_shared=[0, 1])
    dtr, alr = sv["dtr"], sv["alr"]
    (dxa, dB, dC, ddt, dal, ddtr0, dalr0), cres = _own(
        ssd_scan_bwd(act, dt, sv["alog"], dtr[0], alr[0], sv["st0"], dy, dxs_skip, reverse=False, comm=comm), comm)
    comm1 = None if tail_comm is None else tail_comm(None, dw_out)
    (dxa, dB, dC, ddt, dal, ddtr1, dalr1), tail_out = _own(
        ssd_scan_bwd(act, dt, sv["alog"], dtr[1], alr[1], sv["st1"], dy, dxa, reverse=True, prev=(dB, dC, ddt, dal),
                     comm=comm1), comm1)
    dact = jnp.concatenate([dxa, dB, dC], axis=1)
    dxbc, dconv_w, dconv_b = ssd_conv_bwd(u, p["conv_w"], p["conv_b"], dact)
    fold = jnp.asarray(np.repeat(np.eye(SSD_HEADS, dtype=np.float32), SSD_HEADDIM, axis=0))
    folded = [matmul(t, fold, exact=True, name=f"ssd_ddt_fold_{d}", tn=SSD_HEADS) for d, t in enumerate((ddtr0, ddtr1))]
    ddt_all = _ssd_head_layout(ddt) + jnp.pad(jnp.concatenate(folded, axis=1), ((0, 0), (0, 128 - 2 * SSD_HEADS)))
    dal_rep = jnp.concatenate([t.reshape(SSD_HEADS, SSD_HEADDIM).sum(axis=1) for t in (dalr0, dalr1)])[None, :]
    ddt_raw, ddt_bias = rowwise_bwd("ssd_dt_bwd", _dt_fn, [(u, (SSD_DI + SSD_CONV_CH) // 128, False)],
                                    [(p["dt_bias"], False)], [ddt_all], W=128, diff_rows=[0], diff_shared=[0])
    du = _cat_mxu([dz, dxbc, ddt_raw, jnp.zeros((S, SSD_IN_PAD - SSD_IN - 64), F32)])
    res = _in_out_bwd("ssd", sv["x"], sv["ng"], sv["hn"], du, p["w_in"], dx,
                      tail_comm=None if tail_comm is None else (lambda g_in: tail_comm(g_in[:, :SSD_IN], None)))
    dx_prev, dng, dw_in = res[:3]
    tail = res[3] + tail_out if tail_comm is not None else None
    grads = dict(
        w_in=dw_in[:, :SSD_IN], w_out=dw_out, conv_w=dconv_w[:SSD_CONV], conv_b=dconv_b,
        dt_bias=ddt_bias[:, :2 * SSD_HEADS], a_log=_ssd_head_layout(dal)[:, :2 * SSD_HEADS] + dal_rep,
        d=ddexp.reshape(SSD_HEADS, SSD_HEADDIM).sum(axis=1)[None, :], norm_g=dnorm, ng=dng)
    return dx_prev, grads, cres, tail


def hg_layer_fwd(x, ng, p, comm0=None, comm1=None):
    hn = _prenorm("hg", x, ng)
    u = matmul(hn, p["w_in"], name="hg_in")
    lb = hg_lb_fwd(p["hgrn_lb"])
    (o0, st0), cres0 = _own(hg_scan_fwd(u, lb, reverse=False, comm=comm0), comm0)
    (o, st1), cres1 = _own(hg_scan_fwd(u, lb, reverse=True, o_prev=o0, comm=comm1), comm1)
    g = rowwise_fwd("hg_post", _hg_post_fn, [(o, 0, True), (u, 4 * HG_HEADS, True)], [(p["norm_g"], True)],
                    [(HG_D, MXU_DTYPE)], W=HG_D, ncb=HG_HEADS)[0]
    xn = matmul(g, p["w_out"], residual=x, name="hg_out")
    return xn, dict(x=x, ng=ng, hn=hn, u=u, lb=lb, o=o, st0=st0, st1=st1, g=g), cres0, cres1


def hg_layer_bwd(sv, p, dx, comm=None):
    u, lb = sv["u"], sv["lb"]
    dg = matmul(dx, p["w_out"], tb=True, name="hg_dg")
    dw_out = matmul(sv["g"], dx, ta=True, out_dtype=GRAD_WIRE_DTYPE, name="hg_dw_out")
    do, dgate, dnorm = rowwise_bwd("hg_post_bwd", _hg_post_fn, [(sv["o"], 0, True), (u, 4 * HG_HEADS, True)],
                                   [(p["norm_g"], True)], [dg], W=HG_D, ncb=HG_HEADS, diff_rows=[0, 1], diff_shared=[0])
    (dq0, df0, dv0, dlb0), cres = _own(hg_scan_bwd(u, lb, sv["st0"], do, reverse=False, comm=comm), comm)
    dq, df1, dv, dlb = hg_scan_bwd(u, lb, sv["st1"], do, reverse=True, prev=(dq0, dv0, dlb0))
    du = _cat_mxu([dq, df0, df1, dv, dgate])
    dhgrn_lb = hg_lb_bwd(p["hgrn_lb"], dlb)
    dx_prev, dng, dw_in = _in_out_bwd("hg", sv["x"], sv["ng"], sv["hn"], du, p["w_in"], dx)
    return dx_prev, dict(w_in=dw_in, w_out=dw_out, norm_g=dnorm, hgrn_lb=dhgrn_lb, ng=dng), cres


def _rope_consts(S):
    ar, ac = _rope_tables(S)
    ct = np.concatenate([np.cos(ar), np.cos(ar), np.cos(ac), np.cos(ac)], axis=1).astype(np.float32)
    st = np.concatenate([-np.sin(ar), np.sin(ar), -np.sin(ac), np.sin(ac)], axis=1).astype(np.float32)
    return jnp.asarray(ct), jnp.asarray(st)


def _at_qk(tag, u, col0, nheads, scale, gain, consts, cot=None):
    ct, st = consts
    rows = [(u, col0, True), (ct, 0, False), (st, 0, False)]
    shared = [(gain, False)]
    if cot is None:
        return rowwise_fwd(f"at_{tag}", _make_qk_fn(scale), rows, shared, [(AT_HD, MXU_DTYPE)], W=AT_HD, ncb=nheads)[0]
    return rowwise_bwd(f"at_{tag}_bwd", _make_qk_fn(scale), rows, shared, [cot], W=AT_HD, ncb=nheads,
                       diff_rows=[0], diff_shared=[0])


def at_layer_fwd(x, ng, p, comm=None):
    S = x.shape[0]
    hn = _prenorm("at", x, ng)
    u = matmul(hn, p["w_in"], name="at_in")
    consts = _rope_consts(S)
    qr = _at_qk("q", u, 0, AT_HEADS, AT_HD ** -0.5, p["q_g"], consts)
    kr = _at_qk("k", u, AT_HEADS, AT_KV, 1.0, p["k_g"], consts)
    vc0 = (AT_QW + AT_KW) // AT_HD
    (o, lse), cres = _own(flash_fwd(qr, kr, u, v_col0=vc0, comm=comm), comm)
    g = rowwise_fwd("at_gate", _gate_fn, [(o, 0, True), (u, (AT_QW + 2 * AT_KW) // 1024, True)], [],
                    [(1024, MXU_DTYPE)], W=1024, ncb=AT_QW // 1024)[0]
    xn = matmul(g, p["w_out"], residual=x, name="at_out")
    return xn, dict(x=x, ng=ng, hn=hn, u=u, qr=qr, kr=kr, o=o, lse=lse, g=g), cres


def at_layer_bwd(sv, p, dx, comm=None):
    u, qr, kr = sv["u"], sv["qr"], sv["kr"]
    S = u.shape[0]
    consts = _rope_consts(S)
    vc0 = (AT_QW + AT_KW) // AT_HD
    dg = matmul(dx, p["w_out"], tb=True, name="at_dg")
    dw_out = matmul(sv["g"], dx, ta=True, out_dtype=GRAD_WIRE_DTYPE, name="at_dw_out")
    do, dgate = rowwise_bwd("at_gate_bwd", _gate_fn, [(sv["o"], 0, True), (u, (AT_QW + 2 * AT_KW) // 1024, True)], [],
                            [dg], W=1024, ncb=AT_QW // 1024, diff_rows=[0, 1], diff_shared=[])
    dqs, delta = flash_bwd_dq(qr, kr, u, sv["o"], sv["lse"], do, v_col0=vc0)
    (dkr, dv), cres = _own(flash_bwd_dkv(qr, kr, u, sv["lse"], delta, do, v_col0=vc0, comm=comm), comm)
    dq_raw, dqg = _at_qk("q", u, 0, AT_HEADS, AT_HD ** -0.5, p["q_g"], consts, cot=dqs)
    dk_raw, dkg = _at_qk("k", u, AT_HEADS, AT_KV, 1.0, p["k_g"], consts, cot=dkr)
    du = _cat_mxu([dq_raw, dk_raw, dv, dgate])
    dx_prev, dng, dw_in = _in_out_bwd("at", sv["x"], sv["ng"], sv["hn"], du, p["w_in"], dx)
    return dx_prev, dict(w_in=dw_in, w_out=dw_out, q_g=dqg, k_g=dkg, ng=dng), cres


def _to_stream(t, dil):
    S = t.shape[0]
    return t.reshape(S // dil, dil, DL_HEADS, DL_HD).transpose(2, 1, 0, 3)


def _from_stream(t):
    H, dil, Ls, E = t.shape
    return t.transpose(2, 1, 0, 3).reshape(Ls * dil, H * E)


def _stream_to_hm(t):
    H, dil, Ls, w = t.shape
    return t.transpose(0, 2, 1, 3).reshape(H * Ls * dil, w)


def _hm_to_stream(t, dil):
    w = t.shape[1]
    S = t.shape[0] // DL_HEADS
    return t.reshape(DL_HEADS, S // dil, dil, w).transpose(0, 2, 1, 3)


OX_LSE = DL_HD
DOX_LSE, DOX_DM = DL_HD, DL_HD + 32


def _win(p_ref, c_ref, n_ref, h, T):
    return jnp.concatenate([p_ref[h, 0, T - DL_HALF:T, :], c_ref[h, 0], n_ref[h, 0, 0:DL_HALF, :]], axis=0)


def _win_specs(T, E, nb):
    return [pl.BlockSpec((DL_HEADS, 1, T, E), lambda d, n: (0, d, jnp.maximum(n - 1, 0), 0)),
            pl.BlockSpec((DL_HEADS, 1, T, E), lambda d, n: (0, d, n, 0)),
            pl.BlockSpec((DL_HEADS, 1, T, E), lambda d, n: (0, d, jnp.minimum(n + 1, nb - 1), 0))]


def _band_mask_q(n, T, W, Ls):
    i = lax.broadcasted_iota(jnp.int32, (T, W), 0)
    j = lax.broadcasted_iota(jnp.int32, (T, W), 1)
    kpos = n * T + j - DL_HALF
    return (jnp.abs(j - DL_HALF - i) <= DL_HALF) & (kpos >= 0) & (kpos < Ls)


def band_fwd(q, k, v, bias, *, scale):
    H, dil, Ls, E = q.shape
    T, W = _dl_tiles(Ls)
    nb = Ls // T

    def body(q_ref, kp_ref, kc_ref, kn_ref, vp_ref, vc_ref, vn_ref, b_ref, ox_ref):
        n = pl.program_id(1)
        mask = _band_mask_q(n, T, W, Ls)
        for h in range(H):
            kw = _win(kp_ref, kc_ref, kn_ref, h, T)
            vw = _win(vp_ref, vc_ref, vn_ref, h, T)
            s = _dot(q_ref[h, 0], kw, ((1,), (1,))) * scale + b_ref[h]
            s = jnp.where(mask, s, NEG_BIG)
            m = jnp.max(s, axis=1, keepdims=True)
            lse = m + jnp.log(jnp.sum(jnp.exp(s - m), axis=1, keepdims=True))
            p = jnp.exp(s - lse)
            ox_ref[h, 0, :, 0:E] = _dot(p, vw, ((1,), (0,)))
            ox_ref[h, 0, :, E:2 * E] = lse + jnp.zeros((T, E), F32)

    cur = pl.BlockSpec((H, 1, T, E), lambda d, n: (0, d, n, 0))
    return pl.pallas_call(
        body, name=f"band_fwd_{dil}", grid=(dil, nb),
        in_specs=[cur] + _win_specs(T, E, nb) + _win_specs(T, E, nb) + [pl.BlockSpec((H, T, W), lambda d, n: (0, 0, 0))],
        out_specs=pl.BlockSpec((H, 1, T, 2 * E), lambda d, n: (0, d, n, 0)),
        out_shape=jax.ShapeDtypeStruct((H, dil, Ls, 2 * E), F32),
        compiler_params=_cp("parallel", "parallel"),
    )(q, k, k, k, v, v, v, bias)


def band_bwd_dq(q, k, v, bias, dox, *, scale):
    H, dil, Ls, E = q.shape
    T, W = _dl_tiles(Ls)
    nb = Ls // T

    def body(q_ref, kp_ref, kc_ref, kn_ref, vp_ref, vc_ref, vn_ref, b_ref, dox_ref, dq_ref, db_ref):
        d, n = pl.program_id(0), pl.program_id(1)
        mask = _band_mask_q(n, T, W, Ls)
        first = jnp.logical_and(d == 0, n == 0)

        @pl.when(first)
        def _():
            db_ref[...] = jnp.zeros_like(db_ref)

        for h in range(H):
            kw = _win(kp_ref, kc_ref, kn_ref, h, T)
            vw = _win(vp_ref, vc_ref, vn_ref, h, T)
            dox = dox_ref[h, 0]
            do, lse, dm = dox[:, 0:E], dox[:, DOX_LSE:DOX_LSE + 1], dox[:, DOX_DM:DOX_DM + 1]
            s = _dot(q_ref[h, 0], kw, ((1,), (1,))) * scale + b_ref[h]
            p = jnp.where(mask, jnp.exp(jnp.where(mask, s, 0.0) - lse), 0.0)
            dp = _dot(do, vw, ((1,), (1,)))
            ds = p * (dp - dm)
            dq_ref[h, 0] = (_dot(ds, kw, ((1,), (0,))) * scale).astype(dq_ref.dtype)
            db_ref[h] += ds

    cur = pl.BlockSpec((H, 1, T, E), lambda d, n: (0, d, n, 0))
    bspec = pl.BlockSpec((H, T, W), lambda d, n: (0, 0, 0))
    return pl.pallas_call(
        body, name=f"band_bwd_dq_{dil}", grid=(dil, nb),
        in_specs=[cur] + _win_specs(T, E, nb) + _win_specs(T, E, nb) + [bspec,
                  pl.BlockSpec((H, 1, T, 2 * E), lambda d, n: (0, d, n, 0))],
        out_specs=[cur, bspec],
        out_shape=[jax.ShapeDtypeStruct((H, dil, Ls, E), MXU_DTYPE), jax.ShapeDtypeStruct((H, T, W), F32)],
        compiler_params=_cp("arbitrary", "arbitrary"),
    )(q, k, k, k, v, v, v, bias, dox)


def band_bwd_dkv(q, k, v, bias_t, dox, *, scale):
    H, dil, Ls, E = k.shape
    T, W = _dl_tiles(Ls, DL_TB)
    nb = Ls // T

    def body(qp_ref, qc_ref, qn_ref, k_ref, v_ref, b_ref, dp_ref, dc_ref, dn_ref, dk_ref, dv_ref):
        n = pl.program_id(1)
        iw = lax.broadcasted_iota(jnp.int32, (W, T), 0)
        j = lax.broadcasted_iota(jnp.int32, (W, T), 1)
        qpos = n * T + iw - DL_HALF
        mask = (jnp.abs(j + DL_HALF - iw) <= DL_HALF) & (qpos >= 0) & (qpos < Ls)
        for h in range(H):
            qw = _win(qp_ref, qc_ref, qn_ref, h, T)
            doxw = _win(dp_ref, dc_ref, dn_ref, h, T)
            dow, lsew, dmw = doxw[:, 0:E], doxw[:, DOX_LSE:DOX_LSE + 1], doxw[:, DOX_DM:DOX_DM + 1]
            s = _dot(qw, k_ref[h, 0], ((1,), (1,))) * scale + b_ref[h]
            p = jnp.where(mask, jnp.exp(jnp.where(mask, s, 0.0) - lsew), 0.0)
            dv_ref[h, 0] = _dot(p, dow, ((0,), (0,))).astype(dv_ref.dtype)
            dp = _dot(dow, v_ref[h, 0], ((1,), (1,)))
            ds = p * (dp - dmw)
            dk_ref[h, 0] = (_dot(ds, qw, ((0,), (0,))) * scale).astype(dk_ref.dtype)

    cur = pl.BlockSpec((H, 1, T, E), lambda d, n: (0, d, n, 0))
    return pl.pallas_call(
        body, name=f"band_bwd_dkv_{dil}", grid=(dil, nb),
        in_specs=_win_specs(T, E, nb) + [cur, cur, pl.BlockSpec((H, W, T), lambda d, n: (0, 0, 0))]
        + _win_specs(T, 2 * E, nb),
        out_specs=[cur, cur],
        out_shape=[jax.ShapeDtypeStruct((H, dil, Ls, E), MXU_DTYPE)] * 2,
        compiler_params=_cp("parallel", "parallel"),
    )(q, q, q, k, v, bias_t, dox, dox, dox)


def dl_merge_fwd(oxs, *, R=1024):
    rows = oxs[0].shape[0]
    R = min(R, rows)
    E = DL_HD

    def body(a_ref, b_ref, c_ref, o_ref):
        vals = [r[...] for r in (a_ref, b_ref, c_ref)]
        o_ref[...] = _dl_merge_fn(*[t[:, 0:E] for t in vals], *[t[:, OX_LSE:OX_LSE + 1] for t in vals])[0]

    spec = pl.BlockSpec((R, 2 * E), lambda i: (i, 0))
    return pl.pallas_call(
        body, name="dl_merge", grid=(rows // R,), in_specs=[spec] * 3,
        out_specs=pl.BlockSpec((R, E), lambda i: (i, 0)), out_shape=jax.ShapeDtypeStruct((rows, E), F32),
        compiler_params=_cp("parallel"),
    )(*oxs)


def dl_merge_bwd(oxs, do, *, R=1024):
    rows = oxs[0].shape[0]
    R = min(R, rows)
    E = DL_HD

    def body(a_ref, b_ref, c_ref, do_ref, da_ref, db_ref, dc_ref):
        vals = [r[...] for r in (a_ref, b_ref, c_ref)]
        os_ = [t[:, 0:E] for t in vals]
        ls_ = [t[:, OX_LSE:OX_LSE + 1] for t in vals]
        _, vjp = jax.vjp(_dl_merge_fn, *os_, *ls_)
        g = vjp((do_ref[...],))
        for k, d_ref in enumerate((da_ref, db_ref, dc_ref)):
            dm = jnp.sum(g[k] * os_[k], axis=1, keepdims=True) - g[3 + k]
            d_ref[:, 0:E] = g[k]
            d_ref[:, DOX_LSE:DOX_DM] = ls_[k] + jnp.zeros((R, DOX_DM - DOX_LSE), F32)
            d_ref[:, DOX_DM:2 * E] = dm + jnp.zeros((R, 2 * E - DOX_DM), F32)

    spec = pl.BlockSpec((R, 2 * E), lambda i: (i, 0))
    return pl.pallas_call(
        body, name="dl_merge_bwd", grid=(rows // R,), in_specs=[spec] * 3 + [pl.BlockSpec((R, E), lambda i: (i, 0))],
        out_specs=[spec] * 3, out_shape=[jax.ShapeDtypeStruct((rows, 2 * E), F32)] * 3,
        compiler_params=_cp("parallel"),
    )(*oxs, do)


def _dl_bias_tables(rel_bias, dil, Tq, Tk):
    Wq, Wk = Tq + 2 * DL_HALF, Tk + 2 * DL_HALF
    bq, bk = _dl_bucket_tables(dil, Tq)[0], _dl_bucket_tables(dil, Tk)[1]
    idx = np.concatenate([bq.reshape(-1), bk.reshape(-1)])
    onehot_t = (np.arange(REL_BUCKETS)[:, None] == idx[None, :]).astype(np.float32)
    tab = matmul(rel_bias.T, jnp.asarray(onehot_t), exact=True, name=f"dl_bias_{dil}", tm=DL_HEADS, tk=REL_BUCKETS,
                 tn=_tile(idx.shape[0], (8192, 4096, 2048, 1024, 512, 256, 128)))
    return tab[:, :Tq * Wq].reshape(DL_HEADS, Tq, Wq), tab[:, Tq * Wq:].reshape(DL_HEADS, Wk, Tk), bq


def dl_layer_fwd(x, ng, p):
    S = x.shape[0]
    hn = _prenorm("dl", x, ng)
    nqkv = 3 * len(DL_PAIRS) * DL_W
    uqkv = matmul(hn, p["w_in"], name="dl_in_qkv", b_cols=(0, nqkv), out_dtype=MXU_DTYPE)
    ugate = matmul(hn, p["w_in"], name="dl_in_gate", b_cols=(nqkv, DL_W))
    scale = DL_HD ** -0.5
    per_group, ox_hm = [], []
    for gi, (window, dil) in enumerate(DL_PAIRS):
        base = gi * 3 * DL_W
        Tq, _ = _dl_tiles(S // dil)
        Tk, _ = _dl_tiles(S // dil, DL_TB)
        qs, ks, vs = [_to_stream(uqkv[:, base + c * DL_W:base + (c + 1) * DL_W], dil) for c in range(3)]
        bias, bias_t, bq = _dl_bias_tables(p["rel_bias"], dil, Tq, Tk)
        ox_s = band_fwd(qs, ks, vs, bias, scale=scale)
        per_group.append(dict(qs=qs, ks=ks, vs=vs, bias=bias, bias_t=bias_t, bq=bq, dil=dil))
        ox_hm.append(_stream_to_hm(ox_s))
    om = dl_merge_fwd(ox_hm)
    o = om.reshape(DL_HEADS, S, DL_HD).transpose(1, 0, 2).reshape(S, DL_W)
    g = rowwise_fwd("dl_gate", _gate_fn, [(o, 0, False), (ugate, 0, False)], [], [(DL_W, MXU_DTYPE)], W=DL_W)[0]
    xn = matmul(g, p["w_out"], residual=x, name="dl_out")
    return xn, dict(x=x, ng=ng, hn=hn, ugate=ugate, per_group=per_group, ox_hm=ox_hm, o=o, g=g)


def dl_layer_bwd(sv, p, dx):
    ugate = sv["ugate"]
    S = ugate.shape[0]
    scale = DL_HD ** -0.5
    dg = matmul(dx, p["w_out"], tb=True, name="dl_dg")
    dw_out = matmul(sv["g"], dx, ta=True, out_dtype=GRAD_WIRE_DTYPE, name="dl_dw_out")
    do, dgate = rowwise_bwd("dl_gate_bwd", _gate_fn, [(sv["o"], 0, False), (sv["ugate"], 0, False)], [], [dg], W=DL_W,
                            diff_rows=[0, 1], diff_shared=[])
    do_hm = do.reshape(S, DL_HEADS, DL_HD).transpose(1, 0, 2).reshape(DL_HEADS * S, DL_HD)
    dox_hm = dl_merge_bwd(sv["ox_hm"], do_hm)
    parts, dbs, onehots = [], [], []
    for gi, pg in enumerate(sv["per_group"]):
        dil = pg["dil"]
        T, W = _dl_tiles(S // dil)
        dox_s = _hm_to_stream(dox_hm[gi], dil)
        dq_s, dbias = band_bwd_dq(pg["qs"], pg["ks"], pg["vs"], pg["bias"], dox_s, scale=scale)
        dk_s, dv_s = band_bwd_dkv(pg["qs"], pg["ks"], pg["vs"], pg["bias_t"], dox_s, scale=scale)
        parts += [_from_stream(dq_s), _from_stream(dk_s), _from_stream(dv_s)]
        dbs.append(dbias.reshape(DL_HEADS, T * W))
        onehots.append((pg["bq"].reshape(-1)[:, None] == np.arange(REL_BUCKETS)[None, :]).astype(np.float32))
    drel = matmul(jnp.concatenate(dbs, axis=1), jnp.asarray(np.concatenate(onehots, axis=0)), exact=True,
                  name="dl_drel", tm=DL_HEADS, tn=REL_BUCKETS, tk=2048)
    du = _cat_mxu(parts + [dgate])
    dx_prev, dng, dw_in = _in_out_bwd("dl", sv["x"], sv["ng"], sv["hn"], du, p["w_in"], dx)
    return dx_prev, dict(w_in=dw_in, w_out=dw_out, rel_bias=drel.T, ng=dng)


WEIGHT_ORDER = ['norm_g', 'final_g', 'rel_bias', 'hgrn_lb', 'ssd_w_in', 'ssd_conv_w', 'ssd_conv_b', 'ssd_dt_bias',
                'ssd_a_log', 'ssd_d', 'ssd_norm_g', 'ssd_w_out', 'hg_w_in', 'hg_norm_g', 'hg_w_out', 'at_w_in',
                'at_q_norm_g', 'at_k_norm_g', 'at_w_out', 'dl_w_in', 'dl_w_out']
BIG_IN = ['ssd_w_in', 'hg_w_in', 'at_w_in', 'dl_w_in']
BIG_OUT = ['ssd_w_out', 'hg_w_out', 'at_w_out', 'dl_w_out']
BIG = BIG_IN + BIG_OUT
SMALL = [n for n in WEIGHT_ORDER if n not in BIG]
LANES = 128


def _pack(arrs):
    flat = jnp.concatenate([a.reshape(-1).astype(F32) for a in arrs])
    n = flat.shape[0]
    rows = -(-n // (8 * LANES)) * 8
    return jnp.pad(flat, (0, rows * LANES - n)).reshape(rows, LANES)


def _unpack(buf, shapes):
    flat = buf.reshape(-1)
    out, off = [], 0
    for shp in shapes:
        n = int(np.prod(shp)) if len(shp) else 1
        out.append(flat[off:off + n].reshape(shp))
        off += n
    return out


def kernel(x, norm_g, final_g, rel_bias, hgrn_lb, ssd_w_in, ssd_conv_w, ssd_conv_b, ssd_dt_bias, ssd_a_log, ssd_d, ssd_norm_g, ssd_w_out, hg_w_in, hg_norm_g, hg_w_out, at_w_in, at_q_norm_g, at_k_norm_g, at_w_out, dl_w_in, dl_w_out, loss_target, m_norm_g, m_final_g, m_rel_bias, m_hgrn_lb, m_ssd_w_in, m_ssd_conv_w, m_ssd_conv_b, m_ssd_dt_bias, m_ssd_a_log, m_ssd_d, m_ssd_norm_g, m_ssd_w_out, m_hg_w_in, m_hg_norm_g, m_hg_w_out, m_at_w_in, m_at_q_norm_g, m_at_k_norm_g, m_at_w_out, m_dl_w_in, m_dl_w_out, v_norm_g, v_final_g, v_rel_bias, v_hgrn_lb, v_ssd_w_in, v_ssd_conv_w, v_ssd_conv_b, v_ssd_dt_bias, v_ssd_a_log, v_ssd_d, v_ssd_norm_g, v_ssd_w_out, v_hg_w_in, v_hg_norm_g, v_hg_w_out, v_at_w_in, v_at_q_norm_g, v_at_k_norm_g, v_at_w_out, v_dl_w_in, v_dl_w_out):
    w = dict(norm_g=norm_g, final_g=final_g, rel_bias=rel_bias, hgrn_lb=hgrn_lb, ssd_w_in=ssd_w_in, ssd_conv_w=ssd_conv_w, ssd_conv_b=ssd_conv_b, ssd_dt_bias=ssd_dt_bias, ssd_a_log=ssd_a_log, ssd_d=ssd_d, ssd_norm_g=ssd_norm_g, ssd_w_out=ssd_w_out, hg_w_in=hg_w_in, hg_norm_g=hg_norm_g, hg_w_out=hg_w_out, at_w_in=at_w_in, at_q_norm_g=at_q_norm_g, at_k_norm_g=at_k_norm_g, at_w_out=at_w_out, dl_w_in=dl_w_in, dl_w_out=dl_w_out)
    m = dict(norm_g=m_norm_g, final_g=m_final_g, rel_bias=m_rel_bias, hgrn_lb=m_hgrn_lb, ssd_w_in=m_ssd_w_in, ssd_conv_w=m_ssd_conv_w, ssd_conv_b=m_ssd_conv_b, ssd_dt_bias=m_ssd_dt_bias, ssd_a_log=m_ssd_a_log, ssd_d=m_ssd_d, ssd_norm_g=m_ssd_norm_g, ssd_w_out=m_ssd_w_out, hg_w_in=m_hg_w_in, hg_norm_g=m_hg_norm_g, hg_w_out=m_hg_w_out, at_w_in=m_at_w_in, at_q_norm_g=m_at_q_norm_g, at_k_norm_g=m_at_k_norm_g, at_w_out=m_at_w_out, dl_w_in=m_dl_w_in, dl_w_out=m_dl_w_out)
    v = dict(norm_g=v_norm_g, final_g=v_final_g, rel_bias=v_rel_bias, hgrn_lb=v_hgrn_lb, ssd_w_in=v_ssd_w_in, ssd_conv_w=v_ssd_conv_w, ssd_conv_b=v_ssd_conv_b, ssd_dt_bias=v_ssd_dt_bias, ssd_a_log=v_ssd_a_log, ssd_d=v_ssd_d, ssd_norm_g=v_ssd_norm_g, ssd_w_out=v_ssd_w_out, hg_w_in=v_hg_w_in, hg_norm_g=v_hg_norm_g, hg_w_out=v_hg_w_out, at_w_in=v_at_w_in, at_q_norm_g=v_at_q_norm_g, at_k_norm_g=v_at_k_norm_g, at_w_out=v_at_w_out, dl_w_in=v_dl_w_in, dl_w_out=v_dl_w_out)
    me = 4 * lax.axis_index("x") + 2 * lax.axis_index("y") + lax.axis_index("c")
    xs = x[0]
    S = xs.shape[0]

    shard2d = {n: w[n][0] for n in BIG}
    wire = {n: shard2d[n].astype(MXU_DTYPE) for n in BIG}

    def ag(names):
        return ("ag", [wire[n] for n in names])

    def assemble(names, blks):
        out = {}
        for n, blk in zip(names, blks):
            r, c = shard2d[n].shape
            out[n] = blk.transpose(1, 0, 2).reshape(r, N_DEV * c) if n in BIG_IN else blk.reshape(N_DEV * r, c)
        return out

    def a2a(names, gw):
        bufs = []
        for n in names:
            r, c = shard2d[n].shape
            bufs.append(gw[n].reshape(r, N_DEV, c).transpose(1, 0, 2) if n in BIG_IN else gw[n].reshape(N_DEV, r, c))
        return ("a2a", bufs)

    ssd_w, hg_w, at_w, dl_w = (["ssd_w_in", "ssd_w_out"], ["hg_w_in", "hg_w_out"], ["at_w_in", "at_w_out"],
                               ["dl_w_in", "dl_w_out"])
    full = assemble(ssd_w, allgather_two_level_multi([wire[n] for n in ssd_w], name="allgather_ssd_weights"))
    ncw = ssd_conv_w.shape[2]
    nhg = hg_norm_g.shape[1]
    small_shard = jnp.zeros((8, 512), F32)
    small_shard = small_shard.at[:SSD_CONV, :ncw].set(ssd_conv_w[0]).at[SSD_CONV, :nhg].set(hg_norm_g[0])
    small_all = allgather_two_level(small_shard, name="allgather_small_weights")
    conv_w_full = small_all[:, :SSD_CONV, :ncw].transpose(1, 0, 2).reshape(SSD_CONV, N_DEV * ncw)
    hg_norm_full = small_all[:, SSD_CONV, :nhg].reshape(1, N_DEV * nhg)

    p_ssd = dict(w_in=jnp.pad(full["ssd_w_in"], ((0, 0), (0, SSD_IN_PAD - SSD_IN))), w_out=full["ssd_w_out"],
                 conv_w=conv_w_full, conv_b=ssd_conv_b,
                 dt_bias=jnp.pad(ssd_dt_bias.reshape(1, 2 * SSD_HEADS), ((0, 0), (0, 128 - 2 * SSD_HEADS))),
                 alog=jnp.pad(ssd_a_log.reshape(1, 2 * SSD_HEADS), ((0, 0), (0, 128 - 2 * SSD_HEADS))),
                 dexp=jnp.repeat(ssd_d.reshape(-1), SSD_HEADDIM)[None, :], norm_g=ssd_norm_g)
    x1, sv0, got, got_at_in = ssd_layer_fwd(xs, norm_g[0:1], p_ssd, comm=ag(hg_w), comm1=ag(["at_w_in"]))
    full.update(assemble(hg_w, got))
    p_hg = dict(w_in=full["hg_w_in"], w_out=full["hg_w_out"], norm_g=hg_norm_full, hgrn_lb=hgrn_lb)
    x2, sv1, got_at_out, _ = hg_layer_fwd(x1, norm_g[1:2], p_hg, comm0=ag(["at_w_out"]))
    full.update(assemble(at_w, got_at_in + got_at_out))
    p_at = dict(w_in=full["at_w_in"], w_out=full["at_w_out"], q_g=at_q_norm_g, k_g=at_k_norm_g)
    x3, sv2, got_dl = at_layer_fwd(x2, norm_g[2:3], p_at, comm=ag(dl_w))
    full.update(assemble(dl_w, got_dl))
    p_dl = dict(w_in=full["dl_w_in"], w_out=full["dl_w_out"], rel_bias=rel_bias)
    x4, sv3 = dl_layer_fwd(x3, norm_g[3:4], p_dl)
    loss_part, dx4, dfinal = loss_head(x4, final_g[None, :], loss_target[0])
    dx3, g3 = dl_layer_bwd(sv3, p_dl, dx4)
    dx2, g2, recv_dl = at_layer_bwd(sv2, p_at, dx3, comm=a2a(dl_w, dict(dl_w_in=g3["w_in"], dl_w_out=g3["w_out"])))
    dx1, g1, recv_at = hg_layer_bwd(sv1, p_hg, dx2, comm=a2a(at_w, dict(at_w_in=g2["w_in"], at_w_out=g2["w_out"])))
    dx0, g0, recv_hg, recv_ssd = ssd_layer_bwd(
        sv0, p_ssd, dx1, comm=a2a(hg_w, dict(hg_w_in=g1["w_in"], hg_w_out=g1["w_out"])),
        tail_comm=lambda g_in, g_out: (a2a(["ssd_w_in"], dict(ssd_w_in=g_in)) if g_out is None
                                       else a2a(["ssd_w_out"], dict(ssd_w_out=g_out))))
    recv = dict(zip(ssd_w + hg_w + at_w + dl_w, recv_ssd + recv_hg + recv_at + recv_dl))

    small_full = dict(
        norm_g=jnp.concatenate([g0["ng"], g1["ng"], g2["ng"], g3["ng"]], axis=0), final_g=dfinal[0],
        rel_bias=g3["rel_bias"], hgrn_lb=g1["hgrn_lb"], ssd_conv_w=g0["conv_w"][None], ssd_conv_b=g0["conv_b"],
        ssd_dt_bias=g0["dt_bias"].reshape(1, 2, SSD_HEADS), ssd_a_log=g0["a_log"].reshape(1, 2, SSD_HEADS),
        ssd_d=g0["d"], ssd_norm_g=g0["norm_g"], hg_norm_g=g1["norm_g"], at_q_norm_g=g2["q_g"], at_k_norm_g=g2["k_g"])
    packed = _pack([loss_part[0, 0:1]] + [small_full[n] for n in SMALL])
    summed = sum_parts(allgather_two_level(packed, name="allgather_small_grads"), name="sum_small_grads")
    parts = _unpack(summed, [()] + [small_full[n].shape for n in SMALL])
    loss = parts[0]
    gsmall = dict(zip(SMALL, parts[1:]))
    gsmall["ssd_conv_w"] = lax.dynamic_slice_in_dim(gsmall["ssd_conv_w"], me * ncw, ncw, axis=2)
    gsmall["hg_norm_g"] = lax.dynamic_slice_in_dim(gsmall["hg_norm_g"], me * nhg, nhg, axis=1)
    shapes = [w[n].shape for n in SMALL]
    d_p, m_p, v_p = adamw_plain(_pack([w[n] for n in SMALL]), _pack([gsmall[n] for n in SMALL]),
                                _pack([m[n] for n in SMALL]), _pack([v[n] for n in SMALL]), name="adamw_small")
    grads = dict(gsmall)
    deltas = dict(zip(SMALL, _unpack(d_p, shapes)))
    new_m = dict(zip(SMALL, _unpack(m_p, shapes)))
    new_v = dict(zip(SMALL, _unpack(v_p, shapes)))

    for n in BIG:
        gs, ds, ms, vs = adamw_sum(recv[n], shard2d[n], m[n][0], v[n][0], name=f"adamw_{n}")
        grads[n], deltas[n], new_m[n], new_v[n] = gs[None], ds[None], ms[None], vs[None]

    return (loss, dx0[None], *[grads[n] for n in WEIGHT_ORDER], *[deltas[n] for n in WEIGHT_ORDER],
            *[new_m[n] for n in WEIGHT_ORDER], *[new_v[n] for n in WEIGHT_ORDER])
```

```python
import functools
import math

import jax
import jax.numpy as jnp
import numpy as np
from jax import lax
from jax.experimental import pallas as pl
from jax.experimental.pallas import tpu as pltpu

F32 = jnp.float32
MXU_DTYPE = jnp.bfloat16
GRAD_WIRE_DTYPE = jnp.bfloat16
HIGHEST = lax.Precision.HIGHEST
MESH_ID = pl.DeviceIdType.MESH
N_DEV = 8

D_MODEL = 1024
EPS = 1e-6
NEG_BIG = -1e30

SSD_DI = 2048
SSD_HEADDIM = 64
SSD_HEADS = 32
SSD_GROUPS = 4
SSD_HPG = 8
SSD_STATE = 128
SSD_CONV = 7
SSD_CHUNK = 128
SSD_CHUNKS_PER_STEP = 4
SSD_CONV_CH = SSD_DI + 2 * SSD_GROUPS * SSD_STATE
SSD_IN = SSD_DI + SSD_CONV_CH + 2 * SSD_HEADS
SSD_IN_PAD = 5376

HG_CHUNK = 32
HG_HEADS = 8
HG_D = 128
HG_W = 1024

AT_HEADS = 16
AT_KV = 8
AT_HD = 128
AT_QW = 2048
AT_KW = 1024
GRID_W = 64
ROPE_THETA = 10000.0

DL_PAIRS = ((128, 1), (512, 4), (2048, 16))
DL_HEADS = 16
DL_HD = 64
DL_W = 1024
DL_HALF = 64
REL_BUCKETS = 32
REL_MAX_DIST = 1024

ADAM_LR = 0.001
ADAM_B1 = 0.9
ADAM_B2 = 0.999
ADAM_EPS = 1e-08
ADAM_WD = 0.01
ADAM_STEP = 10

VMEM_LIMIT = 56 * 1024 * 1024


def _cp(*sem):
    return pltpu.CompilerParams(dimension_semantics=tuple(sem), vmem_limit_bytes=VMEM_LIMIT)


def _tile(n, cands=(1024, 768, 512, 384, 256, 128)):
    for c in cands:
        if n % c == 0:
            return c
    return n


def _dot(a, b, dims):
    return lax.dot_general(a.astype(MXU_DTYPE), b.astype(MXU_DTYPE), (dims, ((), ())), preferred_element_type=F32)


def _dot_exact(a, b, dims):
    return lax.dot_general(a, b, (dims, ((), ())), precision=HIGHEST, preferred_element_type=F32)


def _silu(x):
    return x * jax.nn.sigmoid(x)


def _my_pos():
    return lax.axis_index("x"), lax.axis_index("y"), lax.axis_index("c")


def _flat(px, py, pc):
    return 4 * px + 2 * py + pc


def _peers():
    x_, y_, c_ = _my_pos()
    out = []
    for k in range(1, N_DEV):
        fx, fy, fc = (k >> 2) & 1, (k >> 1) & 1, k & 1
        out.append(((1 - x_) if fx else x_, (1 - y_) if fy else y_, (1 - c_) if fc else c_))
    return out


def _comm_copies(kind, in_refs, out_refs, send_sems, recv_sems, local_sems):
    me = _flat(*_my_pos())
    local, starts, waits = [], [], []
    for b, (i_ref, o_ref) in enumerate(zip(in_refs, out_refs)):
        local.append(pltpu.make_async_copy(i_ref if kind == "ag" else i_ref.at[me], o_ref.at[me], local_sems.at[b]))
        for k, p in enumerate(_peers()):
            src = i_ref if kind == "ag" else i_ref.at[_flat(*p)]
            starts.append(pltpu.make_async_remote_copy(
                src_ref=src, dst_ref=o_ref.at[me], send_sem=send_sems.at[b, k], recv_sem=recv_sems.at[b, k],
                device_id=p, device_id_type=MESH_ID))
            waits.append(pltpu.make_async_remote_copy(
                src_ref=src, dst_ref=o_ref.at[_flat(*p)], send_sem=send_sems.at[b, k], recv_sem=recv_sems.at[b, k],
                device_id=p, device_id_type=MESH_ID))
    return local, starts, waits


def pcall(body, comm, *, name, grid, in_specs, out_specs, out_shape, scratch_shapes=(), compiler_params=None):
    single = not isinstance(out_specs, (list, tuple))
    out_specs_l = [out_specs] if single else list(out_specs)
    out_shape_l = [out_shape] if single else list(out_shape)
    if comm is None:
        return pl.pallas_call(body, name=name, grid=grid, in_specs=in_specs, out_specs=out_specs, out_shape=out_shape,
                              scratch_shapes=list(scratch_shapes), compiler_params=compiler_params)
    kind, bufs = comm
    nb, n_in, n_out, n_scr = len(bufs), len(in_specs), len(out_specs_l), len(scratch_shapes)
    c_shape = [jax.ShapeDtypeStruct(((N_DEV,) + b.shape) if kind == "ag" else b.shape, b.dtype) for b in bufs]
    anyspec = pl.BlockSpec(memory_space=pl.ANY)

    def body2(*refs):
        ins, c_ins = refs[:n_in], refs[n_in:n_in + nb]
        outs = refs[n_in + nb:n_in + nb + n_out]
        c_outs = refs[n_in + nb + n_out:n_in + 2 * nb + n_out]
        scr = refs[n_in + 2 * nb + n_out:n_in + 2 * nb + n_out + n_scr]
        send_sems, recv_sems, local_sems = refs[n_in + 2 * nb + n_out + n_scr:]
        first = last = None
        for ax, g in enumerate(grid):
            pid = pl.program_id(ax)
            first = (pid == 0) if first is None else jnp.logical_and(first, pid == 0)
            last = (pid == g - 1) if last is None else jnp.logical_and(last, pid == g - 1)

        @pl.when(first)
        def _():
            local, starts, _ = _comm_copies(kind, c_ins, c_outs, send_sems, recv_sems, local_sems)
            for cp in local + starts:
                cp.start()

        body(*ins, *outs, *scr)

        @pl.when(last)
        def _():
            local, _, waits = _comm_copies(kind, c_ins, c_outs, send_sems, recv_sems, local_sems)
            for cp in waits + local:
                cp.wait()

    call = pl.pallas_call(
        body2, name=name, grid=grid, in_specs=list(in_specs) + [anyspec] * nb,
        out_specs=out_specs_l + [anyspec] * nb, out_shape=out_shape_l + c_shape,
        scratch_shapes=list(scratch_shapes) + [pltpu.SemaphoreType.DMA((nb, N_DEV - 1)),
                                               pltpu.SemaphoreType.DMA((nb, N_DEV - 1)), pltpu.SemaphoreType.DMA((nb,))],
        compiler_params=compiler_params)

    def run(*args):
        res = call(*args, *bufs)
        own = res[:n_out]
        return (own[0] if single else list(own)), list(res[n_out:])

    return run


def matmul(a, b, *, name, ta=False, tb=False, residual=None, out_dtype=F32, exact=False, tm=None, tn=None, tk=None,
           b_cols=None, comm=None):
    M, K = (a.shape[1], a.shape[0]) if ta else a.shape
    n0, N = b_cols if b_cols is not None else (0, b.shape[0] if tb else b.shape[1])
    tm = tm or _tile(M, (1024, 512, 256, 128))
    tn = tn or _tile(N, (1024, 768, 512, 384, 256, 128))
    tk = tk or _tile(K, (2048, 1024, 768, 512, 384, 256, 128))
    nk = K // tk
    dims = (((0,) if ta else (1,)), ((1,) if tb else (0,)))

    def body(*refs):
        if residual is None:
            a_ref, b_ref, o_ref, acc = refs
            r_ref = None
        else:
            a_ref, b_ref, r_ref, o_ref, acc = refs
        k = pl.program_id(2)

        @pl.when(k == 0)
        def _():
            acc[...] = jnp.zeros_like(acc)

        if exact:
            acc[...] += _dot_exact(a_ref[...], b_ref[...], dims)
        else:
            acc[...] += _dot(a_ref[...], b_ref[...], dims)

        @pl.when(k == nk - 1)
        def _():
            r = acc[...]
            if r_ref is not None:
                r = r + r_ref[...]
            o_ref[...] = r.astype(o_ref.dtype)

    a_spec = pl.BlockSpec((tk, tm), lambda i, j, k: (k, i)) if ta else pl.BlockSpec((tm, tk), lambda i, j, k: (i, k))
    assert n0 % tn == 0
    jb = n0 // tn
    b_spec = (pl.BlockSpec((tn, tk), lambda i, j, k: (j + jb, k)) if tb
              else pl.BlockSpec((tk, tn), lambda i, j, k: (k, j + jb)))
    in_specs = [a_spec, b_spec]
    args = [a, b]
    if residual is not None:
        in_specs.append(pl.BlockSpec((tm, tn), lambda i, j, k: (i, j)))
        args.append(residual)
    return pcall(
        body, comm, name=name, grid=(M // tm, N // tn, nk), in_specs=in_specs,
        out_specs=pl.BlockSpec((tm, tn), lambda i, j, k: (i, j)),
        out_shape=jax.ShapeDtypeStruct((M, N), out_dtype),
        scratch_shapes=[pltpu.VMEM((tm, tn), F32)],
        compiler_params=_cp("parallel", "parallel", "arbitrary"),
    )(*args)


def _row_specs2(rows, shared, R, W, ncb):
    specs = []
    for arr, col0, per_j, *wd in rows:
        w = wd[0] if wd else W
        if per_j:
            assert col0 % ncb == 0
            specs.append(pl.BlockSpec((R, ncb * w), lambda i, c=col0 // ncb: (i, c)))
        else:
            specs.append(pl.BlockSpec((R, w), lambda i, c=col0: (i, c)))
    for arr, per_j in shared:
        specs.append(pl.BlockSpec((arr.shape[0], ncb * W if per_j else arr.shape[1]), lambda i: (0, 0)))
    return specs


def _col_block(ref, per_j, j, w):
    return ref[:, j * w:(j + 1) * w] if per_j else ref[...]


def rowwise_fwd(name, fn, rows, shared, outs, *, W, ncb=1, R=256):
    S = rows[0][0].shape[0]
    R = min(R, S)
    nr, ns = len(rows), len(shared)
    widths = [(r[3] if len(r) > 3 else W) for r in rows]
    per_j = [r[2] for r in rows] + [s[1] for s in shared]
    ws = widths + [W] * ns

    def body(*refs):
        for j in range(ncb):
            vals = [_col_block(refs[k], per_j[k], j, ws[k]) for k in range(nr + ns)]
            res = fn(*vals)
            for o_ref, r, (wo, _) in zip(refs[nr + ns:], res, outs):
                o_ref[:, j * wo:(j + 1) * wo] = r.astype(o_ref.dtype)

    return pl.pallas_call(
        body, name=name, grid=(S // R,),
        in_specs=_row_specs2(rows, shared, R, W, ncb),
        out_specs=[pl.BlockSpec((R, ncb * w), lambda i: (i, 0)) for w, _ in outs],
        out_shape=[jax.ShapeDtypeStruct((S, ncb * w), dt) for w, dt in outs],
        compiler_params=_cp("parallel"),
    )(*[r[0] for r in rows], *[s[0] for s in shared])


def rowwise_bwd(name, fn, rows, shared, cots, *, W, ncb=1, R=256, diff_rows, diff_shared, add=None):
    S = rows[0][0].shape[0]
    R = min(R, S)
    nr, ns, nc = len(rows), len(shared), len(cots)
    widths = [(r[3] if len(r) > 3 else W) for r in rows]
    per_j = [r[2] for r in rows] + [s[1] for s in shared]
    ws = widths + [W] * ns
    wo = [c.shape[1] // ncb for c in cots]
    dws = [widths[r] for r in diff_rows]

    def body(*refs):
        ins = refs[:nr + ns]
        ct_refs = refs[nr + ns:nr + ns + nc]
        pos = nr + ns + nc
        add_ref = None
        if add is not None:
            add_ref = refs[pos]
            pos += 1
        drow_refs = refs[pos:pos + len(diff_rows)]
        dsh_refs = refs[pos + len(diff_rows):]
        i = pl.program_id(0)
        tot = [None] * len(diff_shared)
        for j in range(ncb):
            vals = [_col_block(ins[k], per_j[k], j, ws[k]) for k in range(nr + ns)]

            def f(*dv):
                full = list(vals)
                for idx, v in zip(list(diff_rows) + [nr + s for s in diff_shared], dv):
                    full[idx] = v
                return tuple(fn(*full))

            prim = [vals[idx] for idx in diff_rows] + [vals[nr + s] for s in diff_shared]
            _, vjp = jax.vjp(f, *prim)
            grads = vjp(tuple(c[:, j * w:(j + 1) * w] for c, w in zip(ct_refs, wo)))
            for k, (d_ref, w) in enumerate(zip(drow_refs, dws)):
                g = grads[k]
                if k == 0 and add_ref is not None:
                    g = g + add_ref[:, j * w:(j + 1) * w]
                d_ref[:, j * w:(j + 1) * w] = g
            for k, (d_ref, s) in enumerate(zip(dsh_refs, diff_shared)):
                g = grads[len(diff_rows) + k]
                if shared[s][1]:
                    @pl.when(i == 0)
                    def _(d_ref=d_ref, g=g, j=j):
                        d_ref[:, j * W:(j + 1) * W] = g

                    @pl.when(i != 0)
                    def _(d_ref=d_ref, g=g, j=j):
                        d_ref[:, j * W:(j + 1) * W] += g
                else:
                    tot[k] = g if tot[k] is None else tot[k] + g
        for k, (d_ref, s) in enumerate(zip(dsh_refs, diff_shared)):
            if not shared[s][1]:
                @pl.when(i == 0)
                def _(d_ref=d_ref, g=tot[k]):
                    d_ref[...] = g

                @pl.when(i != 0)
                def _(d_ref=d_ref, g=tot[k]):
                    d_ref[...] += g

    in_specs = _row_specs2(rows, shared, R, W, ncb)
    in_specs += [pl.BlockSpec((R, ncb * w), lambda i: (i, 0)) for w in wo]
    args = [r[0] for r in rows] + [s[0] for s in shared] + list(cots)
    if add is not None:
        in_specs.append(pl.BlockSpec((R, ncb * dws[0]), lambda i: (i, 0)))
        args.append(add)
    out_specs = [pl.BlockSpec((R, ncb * w), lambda i: (i, 0)) for w in dws]
    out_shape = [jax.ShapeDtypeStruct((S, ncb * w), F32) for w in dws]
    for s in diff_shared:
        arr, pj = shared[s]
        shp = (arr.shape[0], ncb * W if pj else arr.shape[1])
        out_specs.append(pl.BlockSpec(shp, lambda i: (0, 0)))
        out_shape.append(jax.ShapeDtypeStruct(shp, F32))
    return pl.pallas_call(
        body, name=name, grid=(S // R,), in_specs=in_specs, out_specs=out_specs, out_shape=out_shape,
        compiler_params=_cp("arbitrary"),
    )(*args)


def _rms(x, g):
    return x * lax.rsqrt(jnp.mean(x * x, axis=-1, keepdims=True) + EPS) * g


def _prenorm_fn(x, g):
    return (_rms(x, g),)


def loss_head(x, g, tgt, *, R=256):
    S, D = x.shape
    R = min(R, S)

    def fn(xv, gv, tv):
        err = _rms(xv, gv) - tv
        return 0.5 * jnp.sum(jnp.mean(err * err, axis=-1, keepdims=True), axis=0, keepdims=True)

    def body(x_ref, g_ref, t_ref, loss_ref, dx_ref, dg_ref):
        i = pl.program_id(0)
        tv = t_ref[...]
        val, vjp = jax.vjp(lambda a, b: fn(a, b, tv), x_ref[...], g_ref[...])
        dx, dg = vjp(jnp.ones((1, 1), F32))
        dx_ref[...] = dx

        @pl.when(i == 0)
        def _():
            loss_ref[...] = jnp.zeros_like(loss_ref) + val
            dg_ref[...] = dg

        @pl.when(i != 0)
        def _():
            loss_ref[...] += val
            dg_ref[...] += dg

    return pl.pallas_call(
        body, name="loss_head", grid=(S // R,),
        in_specs=[pl.BlockSpec((R, D), lambda i: (i, 0)), pl.BlockSpec((1, D), lambda i: (0, 0)),
                  pl.BlockSpec((R, D), lambda i: (i, 0))],
        out_specs=[pl.BlockSpec((1, 128), lambda i: (0, 0)), pl.BlockSpec((R, D), lambda i: (i, 0)),
                   pl.BlockSpec((1, D), lambda i: (0, 0))],
        out_shape=[jax.ShapeDtypeStruct((1, 128), F32), jax.ShapeDtypeStruct((S, D), F32),
                   jax.ShapeDtypeStruct((1, D), F32)],
        compiler_params=_cp("arbitrary"),
    )(x, g, tgt)


@jax.custom_vjp
def _softplus(x):
    z = jnp.exp(-jnp.abs(x))
    u = 1.0 + z
    log1p = jnp.where(u == 1.0, z, jnp.log(u) * (z / jnp.where(u == 1.0, 1.0, u - 1.0)))
    return jnp.maximum(x, 0.0) + log1p


def _softplus_fwd(x):
    return _softplus(x), x


def _softplus_bwd(x, ct):
    return (ct * jax.nn.sigmoid(x),)


_softplus.defvjp(_softplus_fwd, _softplus_bwd)


def _dt_fn(raw, bias):
    return (_softplus(raw + bias),)


CONV_CB = 128
CONV_RB = 256
CONV_PAD = 8


def ssd_conv_fwd(u, conv_w, conv_b):
    S = u.shape[0]
    ncb = SSD_CONV_CH // CONV_CB
    col0 = SSD_DI // CONV_CB
    RB = min(CONV_RB, S)

    def body(x_ref, w_ref, b_ref, o_ref, pad):
        pad[0:CONV_PAD, :] = jnp.zeros((CONV_PAD, CONV_CB), F32)
        pad[S + CONV_PAD:S + 2 * CONV_PAD, :] = jnp.zeros((CONV_PAD, CONV_CB), F32)
        pad[CONV_PAD:S + CONV_PAD, :] = x_ref[...]
        w = w_ref[...]
        b = b_ref[...]
        for r in range(S // RB):
            acc = jnp.zeros((RB, CONV_CB), F32) + b
            for k in range(SSD_CONV):
                off = r * RB + CONV_PAD + k - SSD_CONV // 2
                acc = acc + pad[off:off + RB, :] * w[k:k + 1, :]
            o_ref[r * RB:(r + 1) * RB, :] = _silu(acc)

    return pl.pallas_call(
        body, name="ssd_conv_fwd", grid=(ncb,),
        in_specs=[pl.BlockSpec((S, CONV_CB), lambda j: (0, col0 + j)),
                  pl.BlockSpec((SSD_CONV, CONV_CB), lambda j: (0, j)),
                  pl.BlockSpec((1, CONV_CB), lambda j: (0, j))],
        out_specs=pl.BlockSpec((S, CONV_CB), lambda j: (0, j)),
        out_shape=jax.ShapeDtypeStruct((S, SSD_CONV_CH), F32),
        scratch_shapes=[pltpu.VMEM((S + 2 * CONV_PAD, CONV_CB), F32)],
        compiler_params=_cp("parallel"),
    )(u, conv_w, conv_b)


def ssd_conv_bwd(u, conv_w, conv_b, dact):
    S = u.shape[0]
    ncb = SSD_CONV_CH // CONV_CB
    col0 = SSD_DI // CONV_CB
    RB = min(CONV_RB, S)
    half = SSD_CONV // 2

    def body(x_ref, w_ref, b_ref, da_ref, dx_ref, dw_ref, db_ref, xpad, dpad):
        z8 = jnp.zeros((CONV_PAD, CONV_CB), F32)
        xpad[0:CONV_PAD, :] = z8
        xpad[S + CONV_PAD:S + 2 * CONV_PAD, :] = z8
        dpad[0:CONV_PAD, :] = z8
        dpad[S + CONV_PAD:S + 2 * CONV_PAD, :] = z8
        xpad[CONV_PAD:S + CONV_PAD, :] = x_ref[...]
        w = w_ref[...]
        b = b_ref[...]
        dws = [jnp.zeros((1, CONV_CB), F32) for _ in range(SSD_CONV)]
        db = jnp.zeros((1, CONV_CB), F32)
        for r in range(S // RB):
            acc = jnp.zeros((RB, CONV_CB), F32) + b
            xs = []
            for k in range(SSD_CONV):
                off = r * RB + CONV_PAD + k - half
                xk = xpad[off:off + RB, :]
                xs.append(xk)
                acc = acc + xk * w[k:k + 1, :]
            sg = jax.nn.sigmoid(acc)
            dc = da_ref[r * RB:(r + 1) * RB, :] * (sg * (1.0 + acc * (1.0 - sg)))
            dpad[r * RB + CONV_PAD:(r + 1) * RB + CONV_PAD, :] = dc
            db = db + jnp.sum(dc, axis=0, keepdims=True)
            for k in range(SSD_CONV):
                dws[k] = dws[k] + jnp.sum(xs[k] * dc, axis=0, keepdims=True)
        for r in range(S // RB):
            acc = jnp.zeros((RB, CONV_CB), F32)
            for k in range(SSD_CONV):
                off = r * RB + CONV_PAD + half - k
                acc = acc + dpad[off:off + RB, :] * w[k:k + 1, :]
            dx_ref[r * RB:(r + 1) * RB, :] = acc
        for k in range(SSD_CONV):
            dw_ref[k:k + 1, :] = dws[k]
        dw_ref[SSD_CONV:SSD_CONV + 1, :] = jnp.zeros((1, CONV_CB), F32)
        db_ref[...] = db

    return pl.pallas_call(
        body, name="ssd_conv_bwd", grid=(ncb,),
        in_specs=[pl.BlockSpec((S, CONV_CB), lambda j: (0, col0 + j)),
                  pl.BlockSpec((SSD_CONV, CONV_CB), lambda j: (0, j)),
                  pl.BlockSpec((1, CONV_CB), lambda j: (0, j)),
                  pl.BlockSpec((S, CONV_CB), lambda j: (0, j))],
        out_specs=[pl.BlockSpec((S, CONV_CB), lambda j: (0, j)),
                   pl.BlockSpec((SSD_CONV + 1, CONV_CB), lambda j: (0, j)),
                   pl.BlockSpec((1, CONV_CB), lambda j: (0, j))],
        out_shape=[jax.ShapeDtypeStruct((S, SSD_CONV_CH), F32),
                   jax.ShapeDtypeStruct((SSD_CONV + 1, SSD_CONV_CH), F32),
                   jax.ShapeDtypeStruct((1, SSD_CONV_CH), F32)],
        scratch_shapes=[pltpu.VMEM((S + 2 * CONV_PAD, CONV_CB), F32), pltpu.VMEM((S + 2 * CONV_PAD, CONV_CB), F32)],
        compiler_params=_cp("parallel"),
    )(u, conv_w, conv_b, dact)


def _ssd_group_layout(t):
    r = t.shape[0]
    g = t[:, :2 * SSD_HEADS].reshape(r, 2, SSD_GROUPS, SSD_HPG).transpose(2, 0, 1, 3).reshape(SSD_GROUPS, r, 2 * SSD_HPG)
    return jnp.pad(g, ((0, 0), (0, 0), (0, 128 - 2 * SSD_HPG)))


def _ssd_head_layout(t):
    r = t.shape[1]
    h = t[:, :, :2 * SSD_HPG].reshape(SSD_GROUPS, r, 2, SSD_HPG).transpose(1, 2, 0, 3).reshape(r, 2 * SSD_HEADS)
    return jnp.pad(h, ((0, 0), (0, 128 - 2 * SSD_HEADS)))


def _ssd_chunk(state, x, Bg, Cg, dt, alog, dtr, alr, *, reverse):
    Q, P = SSD_CHUNK, SSD_HEADDIM
    r = lax.broadcasted_iota(jnp.int32, (Q, Q), 0)
    c = lax.broadcasted_iota(jnp.int32, (Q, Q), 1)
    keep = (c >= r) if reverse else (c <= r)
    cum_t = jnp.transpose(_cumsum_rows(dt * (-jnp.exp(alog)), reverse))
    cum = _cumsum_rows(dtr * (-jnp.exp(alr)), reverse)
    last = 0 if reverse else Q - 1
    cum_l = cum[last:last + 1, :]
    CB = _dot(Cg, Bg, ((1,), (1,)))
    yoff = _dot(Cg, state, ((1,), (0,))) * jnp.exp(cum)
    xdt = x * dtr
    ys = []
    for h in range(SSD_HPG):
        col = h + (SSD_HPG if reverse else 0)
        hs = slice(h * P, (h + 1) * P)
        cum_q = jnp.concatenate([cum[:, hs]] * (Q // P), axis=1)
        L = jnp.where(keep, jnp.exp(jnp.where(keep, cum_q - cum_t[col:col + 1, :], 0.0)), 0.0)
        ys.append(_dot(CB * L, xdt[:, hs], ((1,), (0,))))
    new_state = jnp.exp(cum_l) * state + _dot(Bg, xdt * jnp.exp(cum_l - cum), ((0,), (0,)))
    return new_state, jnp.concatenate(ys, axis=1) + yoff


def ssd_scan_fwd(act, dt, alog, dtr, alr, *, reverse, y_prev=None, comm=None):
    S = act.shape[0]
    Q, N, P = SSD_CHUNK, SSD_STATE, SSD_HEADDIM
    nc = S // Q
    GW = SSD_HPG * P
    CPS = SSD_CHUNKS_PER_STEP if nc % SSD_CHUNKS_PER_STEP == 0 else 1
    nsteps, R = nc // CPS, Q * CPS

    def cidx(i):
        return (nsteps - 1 - i) if reverse else i

    def body(*refs):
        if y_prev is None:
            x_ref, b_ref, c_ref, dt_ref, al_ref, dtr_ref, alr_ref, y_ref, st_ref, state = refs
            yp_ref = None
        else:
            x_ref, b_ref, c_ref, dt_ref, al_ref, dtr_ref, alr_ref, yp_ref, y_ref, st_ref, state = refs
        i = pl.program_id(1)

        @pl.when(i == 0)
        def _():
            state[...] = jnp.zeros_like(state)

        for cc in (range(CPS - 1, -1, -1) if reverse else range(CPS)):
            rs = slice(cc * Q, (cc + 1) * Q)
            st = state[...]
            st_ref[cc, 0] = st
            ns, y = _ssd_chunk(st, x_ref[rs, :], b_ref[rs, :], c_ref[rs, :], dt_ref[0, rs, :], al_ref[0],
                               dtr_ref[rs, :], alr_ref[...], reverse=reverse)
            state[...] = ns
            y_ref[rs, :] = y if yp_ref is None else y + yp_ref[rs, :]

    xspec = pl.BlockSpec((R, GW), lambda g, i: (cidx(i), g))
    in_specs = [xspec,
                pl.BlockSpec((R, N), lambda g, i: (cidx(i), SSD_DI // N + g)),
                pl.BlockSpec((R, N), lambda g, i: (cidx(i), SSD_DI // N + SSD_GROUPS + g)),
                pl.BlockSpec((1, R, 128), lambda g, i: (g, cidx(i), 0)),
                pl.BlockSpec((1, 1, 128), lambda g, i: (g, 0, 0)),
                xspec, pl.BlockSpec((1, GW), lambda g, i: (0, g))]
    args = [act, act, act, dt, alog, dtr, alr]
    if y_prev is not None:
        in_specs.append(xspec)
        args.append(y_prev)
    return pcall(
        body, comm, name=f"ssd_scan_fwd_{int(reverse)}", grid=(SSD_GROUPS, nsteps), in_specs=in_specs,
        out_specs=[xspec, pl.BlockSpec((CPS, 1, N, GW), lambda g, i: (cidx(i), g, 0, 0))],
        out_shape=[jax.ShapeDtypeStruct((S, SSD_DI), F32), jax.ShapeDtypeStruct((nc, SSD_GROUPS, N, GW), F32)],
        scratch_shapes=[pltpu.VMEM((N, GW), F32)],
        compiler_params=_cp("arbitrary", "arbitrary"),
    )(*args)


def ssd_scan_bwd(act, dt, alog, dtr, alr, states, dy, prev_x, *, reverse, prev=None, comm=None):
    S = act.shape[0]
    Q, N, P = SSD_CHUNK, SSD_STATE, SSD_HEADDIM
    nc = S // Q
    GW = SSD_HPG * P
    CPS = SSD_CHUNKS_PER_STEP if nc % SSD_CHUNKS_PER_STEP == 0 else 1
    nsteps, R = nc // CPS, Q * CPS

    def cidx(i):
        return i if reverse else (nsteps - 1 - i)

    def body(*refs):
        x_ref, b_ref, c_ref, dt_ref, al_ref, dtr_ref, alr_ref, st_ref, dy_ref, px_ref = refs[:10]
        pos = 10
        if prev is not None:
            pb_ref, pc_ref, pdt_ref, pal_ref = refs[pos:pos + 4]
            pos += 4
        dx_ref, db_ref, dc_ref, ddt_ref, dal_ref, ddtr_ref, dalr_ref, dstate = refs[pos:]
        i = pl.program_id(1)

        @pl.when(i == 0)
        def _():
            dstate[...] = jnp.zeros_like(dstate)

        dal_tot, dalr_tot = None, None
        for cc in (range(CPS) if reverse else range(CPS - 1, -1, -1)):
            rs = slice(cc * Q, (cc + 1) * Q)
            _, vjp = jax.vjp(functools.partial(_ssd_chunk, reverse=reverse), st_ref[cc, 0], x_ref[rs, :], b_ref[rs, :],
                             c_ref[rs, :], dt_ref[0, rs, :], al_ref[0], dtr_ref[rs, :], alr_ref[...])
            dst, dx, dB, dC, ddt, dal, ddtr, dalr = vjp((dstate[...], dy_ref[rs, :]))
            dstate[...] = dst
            dx_ref[rs, :] = dx + px_ref[rs, :]
            if prev is not None:
                dB = dB + pb_ref[rs, :]
                dC = dC + pc_ref[rs, :]
                ddt = ddt + pdt_ref[0, rs, :]
            db_ref[rs, :] = dB
            dc_ref[rs, :] = dC
            ddt_ref[0, rs, :] = ddt
            ddtr_ref[rs, :] = ddtr
            dal_tot = dal if dal_tot is None else dal_tot + dal
            dalr_tot = dalr if dalr_tot is None else dalr_tot + dalr

        @pl.when(i == 0)
        def _():
            dal_ref[0] = dal_tot + (pal_ref[0] if prev is not None else 0.0)
            dalr_ref[...] = dalr_tot

        @pl.when(i != 0)
        def _():
            dal_ref[0] += dal_tot
            dalr_ref[...] += dalr_tot

    xspec = pl.BlockSpec((R, GW), lambda g, i: (cidx(i), g))
    gspec = pl.BlockSpec((R, N), lambda g, i: (cidx(i), g))
    dtspec = pl.BlockSpec((1, R, 128), lambda g, i: (g, cidx(i), 0))
    alspec = pl.BlockSpec((1, 1, 128), lambda g, i: (g, 0, 0))
    alrspec = pl.BlockSpec((1, GW), lambda g, i: (0, g))
    in_specs = [xspec,
                pl.BlockSpec((R, N), lambda g, i: (cidx(i), SSD_DI // N + g)),
                pl.BlockSpec((R, N), lambda g, i: (cidx(i), SSD_DI // N + SSD_GROUPS + g)),
                dtspec, alspec, xspec, alrspec,
                pl.BlockSpec((CPS, 1, N, GW), lambda g, i: (cidx(i), g, 0, 0)), xspec, xspec]
    args = [act, act, act, dt, alog, dtr, alr, states, dy, prev_x]
    if prev is not None:
        in_specs += [gspec, gspec, dtspec, alspec]
        args += list(prev)
    return pcall(
        body, comm, name=f"ssd_scan_bwd_{int(reverse)}", grid=(SSD_GROUPS, nsteps), in_specs=in_specs,
        out_specs=[xspec, gspec, gspec, dtspec, alspec, xspec, alrspec],
        out_shape=[jax.ShapeDtypeStruct((S, SSD_DI), F32), jax.ShapeDtypeStruct((S, SSD_GROUPS * N), F32),
                   jax.ShapeDtypeStruct((S, SSD_GROUPS * N), F32),
                   jax.ShapeDtypeStruct((SSD_GROUPS, S, 128), F32), jax.ShapeDtypeStruct((SSD_GROUPS, 1, 128), F32),
                   jax.ShapeDtypeStruct((S, SSD_DI), F32), jax.ShapeDtypeStruct((1, SSD_DI), F32)],
        scratch_shapes=[pltpu.VMEM((N, GW), F32)],
        compiler_params=_cp("arbitrary", "arbitrary"),
    )(*args)


def _ssd_post_fn(y, xs, z, dexp, ng):
    t = (y + xs * dexp) * _silu(z)
    return (_rms(t, ng),)


def _cumsum_rows_impl(x, reverse):
    n = x.shape[0]
    row = lax.broadcasted_iota(jnp.int32, x.shape, 0)
    k = 1
    while k < n:
        if reverse:
            x = x + jnp.where(row < n - k, pltpu.roll(x, n - k, 0), 0.0)
        else:
            x = x + jnp.where(row >= k, pltpu.roll(x, k, 0), 0.0)
        k *= 2
    return x


@functools.partial(jax.custom_vjp, nondiff_argnums=(1,))
def _cumsum_rows(x, reverse):
    return _cumsum_rows_impl(x, reverse)


_cumsum_rows.defvjp(lambda x, reverse: (_cumsum_rows_impl(x, reverse), None),
                    lambda reverse, _, ct: (_cumsum_rows_impl(ct, not reverse),))


def _hg_chunk(state, qraw, fraw, v, lb, *, reverse):
    C = HG_CHUNK
    r = lax.broadcasted_iota(jnp.int32, (C, C), 0)
    c = lax.broadcasted_iota(jnp.int32, (C, C), 1)
    keep = (c >= r) if reverse else (c <= r)
    q = _silu(qraw)
    f = lb + (1.0 - lb) * jax.nn.sigmoid(fraw)
    k = 1.0 - f
    g = jnp.log(f)
    G = _cumsum_rows(g, reverse)
    ref_row = C // 2 - 1 if reverse else C // 2
    last_row = 0 if reverse else C - 1
    Gr = G[ref_row:ref_row + 1, :]
    Gl = G[last_row:last_row + 1, :]
    q_t = q * jnp.exp(G - Gr)
    k_t = k * jnp.exp(Gr - G)
    att = jnp.where(keep, _dot(q_t, k_t, ((1,), (1,))), 0.0)
    o = _dot(att, v, ((1,), (0,))) + _dot(q * jnp.exp(G), state, ((1,), (0,)))
    kd = k * jnp.exp(Gl - G)
    new_state = jnp.transpose(jnp.exp(Gl)) * state + _dot(kd, v, ((0,), (0,)))
    return new_state, o


def hg_scan_fwd(u, lb, *, reverse, o_prev=None, rows=256, comm=None):
    S = u.shape[0]
    nh = HG_HEADS
    rows = min(rows, S)
    nsteps = S // rows
    ncb = rows // HG_CHUNK
    f_sec = 2 if reverse else 1

    def blk(i):
        return (nsteps - 1 - i) if reverse else i

    def body(*refs):
        if o_prev is None:
            q_ref, f_ref, v_ref, lb_ref, o_ref, st_ref, state = refs
            op_ref = None
        else:
            q_ref, f_ref, v_ref, lb_ref, op_ref, o_ref, st_ref, state = refs
        i = pl.program_id(0)

        @pl.when(i == 0)
        def _():
            state[...] = jnp.zeros_like(state)

        def chunk(cc, carry):
            ci = (ncb - 1 - cc) if reverse else cc
            sl = pl.ds(pl.multiple_of(ci * HG_CHUNK, HG_CHUNK), HG_CHUNK)
            for h in range(nh):
                hs = slice(h * HG_D, (h + 1) * HG_D)
                st = state[h]
                st_ref[ci, h] = st
                ns, o = _hg_chunk(st, q_ref[sl, hs], f_ref[sl, hs], v_ref[sl, hs], lb_ref[:, hs], reverse=reverse)
                state[h] = ns
                if op_ref is not None:
                    o = o + op_ref[sl, hs]
                o_ref[sl, hs] = o
            return carry

        lax.fori_loop(0, ncb, chunk, 0)

    rowspec = lambda sec: pl.BlockSpec((rows, HG_W), lambda i: (blk(i), sec))
    in_specs = [rowspec(0), rowspec(f_sec), rowspec(3), pl.BlockSpec((1, HG_W), lambda i: (0, 0))]
    args = [u, u, u, lb]
    if o_prev is not None:
        in_specs.append(rowspec(0))
        args.append(o_prev)
    return pcall(
        body, comm, name=f"hg_scan_fwd_{int(reverse)}", grid=(nsteps,), in_specs=in_specs,
        out_specs=[rowspec(0), pl.BlockSpec((ncb, nh, HG_D, HG_D), lambda i: (blk(i), 0, 0, 0))],
        out_shape=[jax.ShapeDtypeStruct((S, HG_W), F32), jax.ShapeDtypeStruct((S // HG_CHUNK, nh, HG_D, HG_D), F32)],
        scratch_shapes=[pltpu.VMEM((nh, HG_D, HG_D), F32)],
        compiler_params=_cp("arbitrary"),
    )(*args)


def hg_scan_bwd(u, lb, states, do, *, reverse, prev=None, rows=256, comm=None):
    S = u.shape[0]
    nh = HG_HEADS
    rows = min(rows, S)
    nsteps = S // rows
    ncb = rows // HG_CHUNK
    f_sec = 2 if reverse else 1

    def blk(i):
        return i if reverse else (nsteps - 1 - i)

    def body(*refs):
        q_ref, f_ref, v_ref, lb_ref, st_ref, do_ref = refs[:6]
        pos = 6
        if prev is not None:
            pq_ref, pv_ref, plb_ref = refs[pos:pos + 3]
            pos += 3
        dq_ref, df_ref, dv_ref, dlb_ref, dstate = refs[pos:]
        i = pl.program_id(0)

        @pl.when(i == 0)
        def _():
            dstate[...] = jnp.zeros_like(dstate)
            dlb_ref[...] = plb_ref[...] if prev is not None else jnp.zeros_like(dlb_ref)

        def chunk(cc, carry):
            ci = cc if reverse else (ncb - 1 - cc)
            sl = pl.ds(pl.multiple_of(ci * HG_CHUNK, HG_CHUNK), HG_CHUNK)
            for h in range(nh):
                hs = slice(h * HG_D, (h + 1) * HG_D)
                _, vjp = jax.vjp(functools.partial(_hg_chunk, reverse=reverse), st_ref[ci, h],
                                 q_ref[sl, hs], f_ref[sl, hs], v_ref[sl, hs], lb_ref[:, hs])
                dst, dq, df, dv, dlb = vjp((dstate[h], do_ref[sl, hs]))
                dstate[h] = dst
                if prev is not None:
                    dq = dq + pq_ref[sl, hs]
                    dv = dv + pv_ref[sl, hs]
                dq_ref[sl, hs] = dq
                df_ref[sl, hs] = df
                dv_ref[sl, hs] = dv
                dlb_ref[:, hs] += dlb
            return carry

        lax.fori_loop(0, ncb, chunk, 0)

    rowspec = lambda sec: pl.BlockSpec((rows, HG_W), lambda i: (blk(i), sec))
    lbspec = pl.BlockSpec((1, HG_W), lambda i: (0, 0))
    in_specs = [rowspec(0), rowspec(f_sec), rowspec(3), lbspec,
                pl.BlockSpec((ncb, nh, HG_D, HG_D), lambda i: (blk(i), 0, 0, 0)), rowspec(0)]
    args = [u, u, u, lb, states, do]
    if prev is not None:
        in_specs += [rowspec(0), rowspec(0), lbspec]
        args += list(prev)
    return pcall(
        body, comm, name=f"hg_scan_bwd_{int(reverse)}", grid=(nsteps,), in_specs=in_specs,
        out_specs=[rowspec(0), rowspec(0), rowspec(0), lbspec],
        out_shape=[jax.ShapeDtypeStruct((S, HG_W), F32)] * 3 + [jax.ShapeDtypeStruct((1, HG_W), F32)],
        scratch_shapes=[pltpu.VMEM((nh, HG_D, HG_D), F32)],
        compiler_params=_cp("arbitrary"),
    )(*args)


def _hg_lb_fn(lbp):
    m = jnp.max(lbp, axis=0, keepdims=True)
    e = jnp.exp(lbp - m)
    sm = e / jnp.sum(e, axis=0, keepdims=True)
    return ((sm[0:1] + sm[1:2]) - sm[0:1],)


def hg_lb_fwd(lbp):
    def body(x_ref, o_ref):
        o_ref[...] = _hg_lb_fn(x_ref[...])[0]

    return pl.pallas_call(body, name="hg_lb_fwd", out_shape=jax.ShapeDtypeStruct((1, HG_W), F32))(lbp)


def hg_lb_bwd(lbp, dlb):
    def body(x_ref, d_ref, o_ref):
        _, vjp = jax.vjp(_hg_lb_fn, x_ref[...])
        o_ref[...] = vjp((d_ref[...],))[0]

    return pl.pallas_call(body, name="hg_lb_bwd", out_shape=jax.ShapeDtypeStruct(lbp.shape, F32))(lbp, dlb)


def _hg_post_fn(o, gate, ng):
    return (_rms(o, ng) * _silu(gate),)


def _gate_fn(o, gate):
    return (o * _silu(gate),)


def _rope_tables(S):
    t = np.arange(S)
    row = (t // GRID_W).astype(np.float32)
    col = (t % GRID_W).astype(np.float32)
    half = AT_HD // 4
    inv = (ROPE_THETA ** (-np.arange(0, 2 * half, 2, dtype=np.float32) / np.float32(2 * half))).astype(np.float32)
    ar = row[:, None] * inv[None, :]
    ac = col[:, None] * inv[None, :]
    return ar.astype(np.float32), ac.astype(np.float32)


@jax.custom_vjp
def _half_swap(x):
    ax = x.ndim - 1
    lane = lax.broadcasted_iota(jnp.int32, x.shape, ax)
    return jnp.where((lane & 32) == 0, pltpu.roll(x, 96, ax), pltpu.roll(x, 32, ax))


_half_swap.defvjp(lambda x: (_half_swap(x), None), lambda _, ct: (_half_swap(ct),))


def _make_qk_fn(scale):
    def fn(x, ct, st, g):
        n = _rms(x, g)
        return ((n * ct + _half_swap(n) * st) * scale,)
    return fn


def flash_fwd(q, k, v, *, v_col0=0, tq=256, comm=None):
    S = q.shape[0]
    tq = min(tq, S)
    G = AT_HEADS // AT_KV

    def body(q_ref, k_ref, v_ref, o_ref, lse_ref):
        kv, vv = k_ref[...], v_ref[...]
        for g in range(G):
            sl = slice(g * AT_HD, (g + 1) * AT_HD)
            s = _dot(q_ref[:, sl], kv, ((1,), (1,)))
            m = jnp.max(s, axis=1, keepdims=True)
            p = jnp.exp(s - m)
            l = jnp.sum(p, axis=1, keepdims=True)
            o_ref[:, sl] = _dot(p, vv, ((1,), (0,))) / l
            lse_ref[0, :, g:g + 1] = m + jnp.log(l)

    return pcall(
        body, comm, name="flash_fwd", grid=(AT_KV, S // tq),
        in_specs=[pl.BlockSpec((tq, G * AT_HD), lambda h, i: (i, h)),
                  pl.BlockSpec((S, AT_HD), lambda h, i: (0, h)),
                  pl.BlockSpec((S, AT_HD), lambda h, i: (0, v_col0 + h))],
        out_specs=[pl.BlockSpec((tq, G * AT_HD), lambda h, i: (i, h)),
                   pl.BlockSpec((1, tq, G), lambda h, i: (h, i, 0))],
        out_shape=[jax.ShapeDtypeStruct((S, AT_QW), F32), jax.ShapeDtypeStruct((AT_KV, S, G), F32)],
        compiler_params=_cp("parallel", "arbitrary"),
    )(q, k, v)


def flash_bwd_dq(q, k, v, o, lse, do, *, v_col0=0, tq=256):
    S = q.shape[0]
    tq = min(tq, S)
    G = AT_HEADS // AT_KV

    def body(q_ref, k_ref, v_ref, o_ref, lse_ref, do_ref, dq_ref, dl_ref):
        kv, vv = k_ref[...], v_ref[...]
        for g in range(G):
            sl = slice(g * AT_HD, (g + 1) * AT_HD)
            dog = do_ref[:, sl]
            delta = jnp.sum(dog * o_ref[:, sl], axis=1, keepdims=True)
            s = _dot(q_ref[:, sl], kv, ((1,), (1,)))
            p = jnp.exp(s - lse_ref[0, :, g:g + 1])
            dp = _dot(dog, vv, ((1,), (1,)))
            ds = p * (dp - delta)
            dq_ref[:, sl] = _dot(ds, kv, ((1,), (0,)))
            dl_ref[0, :, g:g + 1] = delta

    qspec = pl.BlockSpec((tq, G * AT_HD), lambda h, i: (i, h))
    lspec = pl.BlockSpec((1, tq, G), lambda h, i: (h, i, 0))
    return pl.pallas_call(
        body, name="flash_bwd_dq", grid=(AT_KV, S // tq),
        in_specs=[qspec, pl.BlockSpec((S, AT_HD), lambda h, i: (0, h)),
                  pl.BlockSpec((S, AT_HD), lambda h, i: (0, v_col0 + h)), qspec, lspec, qspec],
        out_specs=[qspec, lspec],
        out_shape=[jax.ShapeDtypeStruct((S, AT_QW), F32), jax.ShapeDtypeStruct((AT_KV, S, G), F32)],
        compiler_params=_cp("parallel", "arbitrary"),
    )(q, k, v, o, lse, do)


def flash_bwd_dkv(q, k, v, lse, delta, do, *, v_col0=0, tk=512, comm=None):
    S = q.shape[0]
    tk = min(tk, S)
    G = AT_HEADS // AT_KV

    def body(q_ref, k_ref, v_ref, lse_ref, dl_ref, do_ref, dk_ref, dv_ref):
        kv, vv = k_ref[...], v_ref[...]
        dk = jnp.zeros((tk, AT_HD), F32)
        dv = jnp.zeros((tk, AT_HD), F32)
        for g in range(G):
            sl = slice(g * AT_HD, (g + 1) * AT_HD)
            qg, dog = q_ref[:, sl], do_ref[:, sl]
            s = _dot(qg, kv, ((1,), (1,)))
            p = jnp.exp(s - lse_ref[0, :, g:g + 1])
            dv = dv + _dot(p, dog, ((0,), (0,)))
            dp = _dot(dog, vv, ((1,), (1,)))
            ds = p * (dp - dl_ref[0, :, g:g + 1])
            dk = dk + _dot(ds, qg, ((0,), (0,)))
        dk_ref[...] = dk
        dv_ref[...] = dv

    qspec = pl.BlockSpec((S, G * AT_HD), lambda h, j: (0, h))
    kspec = pl.BlockSpec((tk, AT_HD), lambda h, j: (j, h))
    lspec = pl.BlockSpec((1, S, G), lambda h, j: (h, 0, 0))
    return pcall(
        body, comm, name="flash_bwd_dkv", grid=(AT_KV, S // tk),
        in_specs=[qspec, kspec, pl.BlockSpec((tk, AT_HD), lambda h, j: (j, v_col0 + h)), lspec, lspec, qspec],
        out_specs=[kspec, kspec],
        out_shape=[jax.ShapeDtypeStruct((S, AT_KW), F32), jax.ShapeDtypeStruct((S, AT_KW), F32)],
        compiler_params=_cp("parallel", "arbitrary"),
    )(q, k, v, lse, delta, do)


def _t5_bucket_np(rel):
    half = REL_BUCKETS // 2
    exact = half // 2
    n = np.abs(rel)
    large = exact + (np.log(np.maximum(n, 1).astype(np.float32) / np.float32(exact))
                     / np.float32(math.log(REL_MAX_DIST / exact)) * np.float32(half - exact)).astype(np.int32)
    large = np.minimum(large, half - 1)
    return np.where(rel > 0, half, 0) + np.where(n < exact, n, large)


DL_TB = 256


def _dl_tiles(Ls, T=128):
    T = min(T, Ls)
    return T, T + 2 * DL_HALF


def _dl_bucket_tables(dil, T):
    W = T + 2 * DL_HALF
    i = np.arange(T)[:, None]
    j = np.arange(W)[None, :]
    bq = _t5_bucket_np((j - DL_HALF - i) * dil)
    iw = np.arange(W)[:, None]
    jk = np.arange(T)[None, :]
    bk = _t5_bucket_np((jk + DL_HALF - iw) * dil)
    return bq.astype(np.int32), bk.astype(np.int32)


def _dl_merge_fn(o0, o1, o2, l0, l1, l2):
    m = jnp.maximum(jnp.maximum(l0, l1), l2)
    e0, e1, e2 = jnp.exp(l0 - m), jnp.exp(l1 - m), jnp.exp(l2 - m)
    den = e0 + e1 + e2
    return ((e0 / den) * o0 + (e1 / den) * o1 + (e2 / den) * o2,)


def _adamw_math(w, g, m, v):
    m = ADAM_B1 * m + (1.0 - ADAM_B1) * g
    v = ADAM_B2 * v + (1.0 - ADAM_B2) * (g * g)
    m_hat = m / (1.0 - ADAM_B1 ** ADAM_STEP)
    v_hat = v / (1.0 - ADAM_B2 ** ADAM_STEP)
    delta = -ADAM_LR * (m_hat / (jnp.sqrt(v_hat) + ADAM_EPS) + ADAM_WD * w)
    return delta, m, v


def adamw_sum(parts, w, m, v, *, name, R=128):
    rows, cols = w.shape
    R = min(R, rows)
    if rows % R:
        R = rows

    def body(p_ref, w_ref, m_ref, v_ref, g_ref, d_ref, nm_ref, nv_ref):
        g = p_ref[0].astype(F32)
        for s in range(1, N_DEV):
            g = g + p_ref[s].astype(F32)
        d, nm, nv = _adamw_math(w_ref[...], g, m_ref[...], v_ref[...])
        g_ref[...] = g
        d_ref[...] = d
        nm_ref[...] = nm
        nv_ref[...] = nv

    spec = pl.BlockSpec((R, cols), lambda i: (i, 0))
    return pl.pallas_call(
        body, name=name, grid=(rows // R,),
        in_specs=[pl.BlockSpec((N_DEV, R, cols), lambda i: (0, i, 0)), spec, spec, spec],
        out_specs=[spec] * 4, out_shape=[jax.ShapeDtypeStruct((rows, cols), F32)] * 4,
        compiler_params=_cp("parallel"),
    )(parts, w, m, v)


def sum_parts(parts, *, name):
    rows, cols = parts.shape[1:]

    def body(p_ref, o_ref):
        g = p_ref[0]
        for s in range(1, N_DEV):
            g = g + p_ref[s]
        o_ref[...] = g

    return pl.pallas_call(body, name=name, out_shape=jax.ShapeDtypeStruct((rows, cols), F32))(parts)


def adamw_plain(w, g, m, v, *, name):
    def body(w_ref, g_ref, m_ref, v_ref, d_ref, nm_ref, nv_ref):
        d, nm, nv = _adamw_math(w_ref[...], g_ref[...], m_ref[...], v_ref[...])
        d_ref[...] = d
        nm_ref[...] = nm
        nv_ref[...] = nv

    return pl.pallas_call(body, name=name, out_shape=[jax.ShapeDtypeStruct(w.shape, F32)] * 3)(w, g, m, v)


def allgather_two_level(x, *, name):
    R, C = x.shape

    def body(x_ref, out_ref, send_sems, recv_sems, local_sem):
        x_, y_, c_ = _my_pos()
        me, sibling = (x_, y_, c_), (x_, y_, 1 - c_)
        chips = [(1 - x_, y_), (x_, 1 - y_), (1 - x_, 1 - y_)]

        def rows(p):
            return out_ref.at[_flat(*p)]

        def copy(k, block, to, src=None):
            return pltpu.make_async_remote_copy(
                src_ref=rows(block) if src is None else src, dst_ref=rows(block),
                send_sem=send_sems.at[k], recv_sem=recv_sems.at[k], device_id=to, device_id_type=MESH_ID)

        mine = pltpu.make_async_copy(x_ref, rows(me), local_sem)
        mine.start()
        first = [copy(0, me, sibling, src=x_ref)]
        first += [copy(1 + j, me, (*chip, c_), src=x_ref) for j, chip in enumerate(chips)]
        for cp in first:
            cp.start()
        passed = [copy(4 + j, (*chip, c_), sibling) for j, chip in enumerate(chips)]
        for j, chip in enumerate(chips):
            copy(1 + j, (*chip, c_), me).wait_recv()
            passed[j].start()
        copy(0, sibling, me).wait_recv()
        for j, chip in enumerate(chips):
            copy(4 + j, (*chip, 1 - c_), me).wait_recv()
        for cp in first + passed:
            cp.wait_send()
        mine.wait()

    return pl.pallas_call(
        body, name=name,
        out_shape=jax.ShapeDtypeStruct((N_DEV, R, C), x.dtype),
        in_specs=[pl.BlockSpec(memory_space=pl.ANY)],
        out_specs=pl.BlockSpec(memory_space=pl.ANY),
        scratch_shapes=[pltpu.SemaphoreType.DMA((7,)), pltpu.SemaphoreType.DMA((7,)), pltpu.SemaphoreType.DMA],
    )(x)


def allgather_two_level_multi(xs, *, name):
    nb = len(xs)

    def body(*refs):
        x_refs, out_refs = refs[:nb], refs[nb:2 * nb]
        send_sems, recv_sems, local_sems = refs[2 * nb:]
        x_, y_, c_ = _my_pos()
        me, sibling = (x_, y_, c_), (x_, y_, 1 - c_)
        chips = [(1 - x_, y_), (x_, 1 - y_), (1 - x_, 1 - y_)]

        def copy(b, k, block, to, own=False):
            rows = out_refs[b].at[_flat(*block)]
            return pltpu.make_async_remote_copy(
                src_ref=x_refs[b] if own else rows, dst_ref=rows,
                send_sem=send_sems.at[b, k], recv_sem=recv_sems.at[b, k], device_id=to, device_id_type=MESH_ID)

        mine = [pltpu.make_async_copy(x_refs[b], out_refs[b].at[_flat(*me)], local_sems.at[b]) for b in range(nb)]
        first = []
        for b in range(nb):
            first.append(copy(b, 0, me, sibling, own=True))
            first += [copy(b, 1 + j, me, (*chip, c_), own=True) for j, chip in enumerate(chips)]
        for cp in mine + first:
            cp.start()
        passed = []
        for j, chip in enumerate(chips):
            for b in range(nb):
                copy(b, 1 + j, (*chip, c_), me).wait_recv()
                fwd = copy(b, 4 + j, (*chip, c_), sibling)
                fwd.start()
                passed.append(fwd)
        for b in range(nb):
            copy(b, 0, sibling, me).wait_recv()
            for j, chip in enumerate(chips):
                copy(b, 4 + j, (*chip, 1 - c_), me).wait_recv()
        for cp in first + passed:
            cp.wait_send()
        for cp in mine:
            cp.wait()

    anyspec = pl.BlockSpec(memory_space=pl.ANY)
    return pl.pallas_call(
        body, name=name,
        out_shape=[jax.ShapeDtypeStruct((N_DEV,) + x.shape, x.dtype) for x in xs],
        in_specs=[anyspec] * nb, out_specs=[anyspec] * nb,
        scratch_shapes=[pltpu.SemaphoreType.DMA((nb, 7)), pltpu.SemaphoreType.DMA((nb, 7)), pltpu.SemaphoreType.DMA((nb,))],
    )(*xs)


def _prenorm(tag, x, ng):
    return rowwise_fwd(f"{tag}_prenorm", _prenorm_fn, [(x, 0, False)], [(ng, False)], [(D_MODEL, MXU_DTYPE)], W=D_MODEL)[0]


def _cat_mxu(parts):
    return jnp.concatenate([t.astype(MXU_DTYPE) for t in parts], axis=1)


def _in_out_bwd(tag, x, ng, hn, du, w_in, dx, tail_comm=None):
    dw_in = matmul(hn, du, ta=True, out_dtype=GRAD_WIRE_DTYPE, name=f"{tag}_dw_in")
    comm = None if tail_comm is None else tail_comm(dw_in)
    dhn, cres = _own(matmul(du, w_in, tb=True, name=f"{tag}_dhn", comm=comm), comm)
    dx_prev, dng = rowwise_bwd(f"{tag}_prenorm_bwd", _prenorm_fn, [(x, 0, False)], [(ng, False)], [dhn],
                               W=D_MODEL, diff_rows=[0], diff_shared=[0], add=dx)
    return (dx_prev, dng, dw_in) if tail_comm is None else (dx_prev, dng, dw_in, cres)


def _own(res, comm):
    return (res, None) if comm is None else res


def ssd_layer_fwd(x, ng, p, comm=None, comm1=None):
    hn = _prenorm("ssd", x, ng)
    u = matmul(hn, p["w_in"], name="ssd_in")
    act = ssd_conv_fwd(u, p["conv_w"], p["conv_b"])
    dt = rowwise_fwd("ssd_dt", _dt_fn, [(u, (SSD_DI + SSD_CONV_CH) // 128, False)], [(p["dt_bias"], False)],
                     [(128, F32)], W=128)[0]
    H = SSD_HEADS
    dtr = [jnp.repeat(dt[:, d * H:(d + 1) * H], SSD_HEADDIM, axis=1) for d in (0, 1)]
    alr = [jnp.repeat(p["alog"][:, d * H:(d + 1) * H], SSD_HEADDIM, axis=1) for d in (0, 1)]
    dt, alog = _ssd_group_layout(dt), _ssd_group_layout(p["alog"])
    (y0, st0), cres = _own(ssd_scan_fwd(act, dt, alog, dtr[0], alr[0], reverse=False, comm=comm), comm)
    (y, st1), cres1 = _own(ssd_scan_fwd(act, dt, alog, dtr[1], alr[1], reverse=True, y_prev=y0, comm=comm1), comm1)
    g = rowwise_fwd("ssd_post", _ssd_post_fn, [(y, 0, True), (act, 0, True), (u, 0, True)],
                    [(p["dexp"], True), (p["norm_g"], True)], [(512, MXU_DTYPE)], W=512, ncb=SSD_GROUPS)[0]
    xn = matmul(g, p["w_out"], residual=x, name="ssd_out")
    return xn, dict(x=x, ng=ng, hn=hn, u=u, act=act, dt=dt, alog=alog, dtr=dtr, alr=alr, y=y, st0=st0, st1=st1,
                    g=g), cres, cres1


def ssd_layer_bwd(sv, p, dx, comm=None, tail_comm=None):
    u, act, dt = sv["u"], sv["act"], sv["dt"]
    S = u.shape[0]
    dg = matmul(dx, p["w_out"], tb=True, name="ssd_dg")
    dw_out = matmul(sv["g"], dx, ta=True, out_dtype=GRAD_WIRE_DTYPE, name="ssd_dw_out")
    dy, dxs_skip, dz, ddexp, dnorm = rowwise_bwd(
        "ssd_post_bwd", _ssd_post_fn, [(sv["y"], 0, True), (act, 0, True), (u, 0, True)],
        [(p["dexp"], True), (p["norm_g"], True)], [dg], W=512, ncb=SSD_GROUPS, diff_rows=[0, 1, 2], diff_shared=[0, 1])
    dtr, alr = sv["dtr"], sv["alr"]
    (dxa, dB, dC, ddt, dal, ddtr0, dalr0), cres = _own(
        ssd_scan_bwd(act, dt, sv["alog"], dtr[0], alr[0], sv["st0"], dy, dxs_skip, reverse=False, comm=comm), comm)
    comm1 = None if tail_comm is None else tail_comm(None, dw_out)
    (dxa, dB, dC, ddt, dal, ddtr1, dalr1), tail_out = _own(
        ssd_scan_bwd(act, dt, sv["alog"], dtr[1], alr[1], sv["st1"], dy, dxa, reverse=True, prev=(dB, dC, ddt, dal),
                     comm=comm1), comm1)
    dact = jnp.concatenate([dxa, dB, dC], axis=1)
    dxbc, dconv_w, dconv_b = ssd_conv_bwd(u, p["conv_w"], p["conv_b"], dact)
    fold = jnp.asarray(np.repeat(np.eye(SSD_HEADS, dtype=np.float32), SSD_HEADDIM, axis=0))
    folded = [matmul(t, fold, exact=True, name=f"ssd_ddt_fold_{d}", tn=SSD_HEADS) for d, t in enumerate((ddtr0, ddtr1))]
    ddt_all = _ssd_head_layout(ddt) + jnp.pad(jnp.concatenate(folded, axis=1), ((0, 0), (0, 128 - 2 * SSD_HEADS)))
    dal_rep = jnp.concatenate([t.reshape(SSD_HEADS, SSD_HEADDIM).sum(axis=1) for t in (dalr0, dalr1)])[None, :]
    ddt_raw, ddt_bias = rowwise_bwd("ssd_dt_bwd", _dt_fn, [(u, (SSD_DI + SSD_CONV_CH) // 128, False)],
                                    [(p["dt_bias"], False)], [ddt_all], W=128, diff_rows=[0], diff_shared=[0])
    du = _cat_mxu([dz, dxbc, ddt_raw, jnp.zeros((S, SSD_IN_PAD - SSD_IN - 64), F32)])
    res = _in_out_bwd("ssd", sv["x"], sv["ng"], sv["hn"], du, p["w_in"], dx,
                      tail_comm=None if tail_comm is None else (lambda g_in: tail_comm(g_in[:, :SSD_IN], None)))
    dx_prev, dng, dw_in = res[:3]
    tail = res[3] + tail_out if tail_comm is not None else None
    grads = dict(
        w_in=dw_in[:, :SSD_IN], w_out=dw_out, conv_w=dconv_w[:SSD_CONV], conv_b=dconv_b,
        dt_bias=ddt_bias[:, :2 * SSD_HEADS], a_log=_ssd_head_layout(dal)[:, :2 * SSD_HEADS] + dal_rep,
        d=ddexp.reshape(SSD_HEADS, SSD_HEADDIM).sum(axis=1)[None, :], norm_g=dnorm, ng=dng)
    return dx_prev, grads, cres, tail


def hg_layer_fwd(x, ng, p, comm0=None, comm1=None):
    hn = _prenorm("hg", x, ng)
    u = matmul(hn, p["w_in"], name="hg_in")
    lb = hg_lb_fwd(p["hgrn_lb"])
    (o0, st0), cres0 = _own(hg_scan_fwd(u, lb, reverse=False, comm=comm0), comm0)
    (o, st1), cres1 = _own(hg_scan_fwd(u, lb, reverse=True, o_prev=o0, comm=comm1), comm1)
    g = rowwise_fwd("hg_post", _hg_post_fn, [(o, 0, True), (u, 4 * HG_HEADS, True)], [(p["norm_g"], True)],
                    [(HG_D, MXU_DTYPE)], W=HG_D, ncb=HG_HEADS)[0]
    xn = matmul(g, p["w_out"], residual=x, name="hg_out")
    return xn, dict(x=x, ng=ng, hn=hn, u=u, lb=lb, o=o, st0=st0, st1=st1, g=g), cres0, cres1


def hg_layer_bwd(sv, p, dx, comm=None):
    u, lb = sv["u"], sv["lb"]
    dg = matmul(dx, p["w_out"], tb=True, name="hg_dg")
    dw_out = matmul(sv["g"], dx, ta=True, out_dtype=GRAD_WIRE_DTYPE, name="hg_dw_out")
    do, dgate, dnorm = rowwise_bwd("hg_post_bwd", _hg_post_fn, [(sv["o"], 0, True), (u, 4 * HG_HEADS, True)],
                                   [(p["norm_g"], True)], [dg], W=HG_D, ncb=HG_HEADS, diff_rows=[0, 1], diff_shared=[0])
    (dq0, df0, dv0, dlb0), cres = _own(hg_scan_bwd(u, lb, sv["st0"], do, reverse=False, comm=comm), comm)
    dq, df1, dv, dlb = hg_scan_bwd(u, lb, sv["st1"], do, reverse=True, prev=(dq0, dv0, dlb0))
    du = _cat_mxu([dq, df0, df1, dv, dgate])
    dhgrn_lb = hg_lb_bwd(p["hgrn_lb"], dlb)
    dx_prev, dng, dw_in = _in_out_bwd("hg", sv["x"], sv["ng"], sv["hn"], du, p["w_in"], dx)
    return dx_prev, dict(w_in=dw_in, w_out=dw_out, norm_g=dnorm, hgrn_lb=dhgrn_lb, ng=dng), cres


def _rope_consts(S):
    ar, ac = _rope_tables(S)
    ct = np.concatenate([np.cos(ar), np.cos(ar), np.cos(ac), np.cos(ac)], axis=1).astype(np.float32)
    st = np.concatenate([-np.sin(ar), np.sin(ar), -np.sin(ac), np.sin(ac)], axis=1).astype(np.float32)
    return jnp.asarray(ct), jnp.asarray(st)


def _at_qk(tag, u, col0, nheads, scale, gain, consts, cot=None):
    ct, st = consts
    rows = [(u, col0, True), (ct, 0, False), (st, 0, False)]
    shared = [(gain, False)]
    if cot is None:
        return rowwise_fwd(f"at_{tag}", _make_qk_fn(scale), rows, shared, [(AT_HD, MXU_DTYPE)], W=AT_HD, ncb=nheads)[0]
    return rowwise_bwd(f"at_{tag}_bwd", _make_qk_fn(scale), rows, shared, [cot], W=AT_HD, ncb=nheads,
                       diff_rows=[0], diff_shared=[0])


def at_layer_fwd(x, ng, p, comm=None):
    S = x.shape[0]
    hn = _prenorm("at", x, ng)
    u = matmul(hn, p["w_in"], name="at_in")
    consts = _rope_consts(S)
    qr = _at_qk("q", u, 0, AT_HEADS, AT_HD ** -0.5, p["q_g"], consts)
    kr = _at_qk("k", u, AT_HEADS, AT_KV, 1.0, p["k_g"], consts)
    vb = u[:, AT_QW + AT_KW:AT_QW + 2 * AT_KW].astype(MXU_DTYPE)
    (o, lse), cres = _own(flash_fwd(qr, kr, vb, comm=comm), comm)
    g = rowwise_fwd("at_gate", _gate_fn, [(o, 0, True), (u, (AT_QW + 2 * AT_KW) // 1024, True)], [],
                    [(1024, MXU_DTYPE)], W=1024, ncb=AT_QW // 1024)[0]
    xn = matmul(g, p["w_out"], residual=x, name="at_out")
    return xn, dict(x=x, ng=ng, hn=hn, u=u, qr=qr, kr=kr, vb=vb, o=o, lse=lse, g=g), cres


def at_layer_bwd(sv, p, dx, comm=None):
    u, qr, kr = sv["u"], sv["qr"], sv["kr"]
    S = u.shape[0]
    consts = _rope_consts(S)
    vc0 = (AT_QW + AT_KW) // AT_HD
    dg = matmul(dx, p["w_out"], tb=True, name="at_dg")
    dw_out = matmul(sv["g"], dx, ta=True, out_dtype=GRAD_WIRE_DTYPE, name="at_dw_out")
    do, dgate = rowwise_bwd("at_gate_bwd", _gate_fn, [(sv["o"], 0, True), (u, (AT_QW + 2 * AT_KW) // 1024, True)], [],
                            [dg], W=1024, ncb=AT_QW // 1024, diff_rows=[0, 1], diff_shared=[])
    dqs, delta = flash_bwd_dq(qr, kr, sv["vb"], sv["o"], sv["lse"], do)
    (dkr, dv), cres = _own(flash_bwd_dkv(qr, kr, sv["vb"], sv["lse"], delta, do.astype(MXU_DTYPE), comm=comm), comm)
    dq_raw, dqg = _at_qk("q", u, 0, AT_HEADS, AT_HD ** -0.5, p["q_g"], consts, cot=dqs)
    dk_raw, dkg = _at_qk("k", u, AT_HEADS, AT_KV, 1.0, p["k_g"], consts, cot=dkr)
    du = _cat_mxu([dq_raw, dk_raw, dv, dgate])
    dx_prev, dng, dw_in = _in_out_bwd("at", sv["x"], sv["ng"], sv["hn"], du, p["w_in"], dx)
    return dx_prev, dict(w_in=dw_in, w_out=dw_out, q_g=dqg, k_g=dkg, ng=dng), cres


def _to_stream(t, dil):
    S = t.shape[0]
    return t.reshape(S // dil, dil, DL_HEADS, DL_HD).transpose(2, 1, 0, 3)


def _from_stream(t):
    H, dil, Ls, E = t.shape
    return t.transpose(2, 1, 0, 3).reshape(Ls * dil, H * E)


def _stream_to_hm(t):
    H, dil, Ls, w = t.shape
    return t.transpose(0, 2, 1, 3).reshape(H * Ls * dil, w)


def _hm_to_stream(t, dil):
    w = t.shape[1]
    S = t.shape[0] // DL_HEADS
    return t.reshape(DL_HEADS, S // dil, dil, w).transpose(0, 2, 1, 3)


OX_LSE = DL_HD
DOX_LSE, DOX_DM = DL_HD, DL_HD + 32


def _win(p_ref, c_ref, n_ref, h, T):
    return jnp.concatenate([p_ref[h, 0, T - DL_HALF:T, :], c_ref[h, 0], n_ref[h, 0, 0:DL_HALF, :]], axis=0)


def _win_specs(T, E, nb):
    return [pl.BlockSpec((DL_HEADS, 1, T, E), lambda d, n: (0, d, jnp.maximum(n - 1, 0), 0)),
            pl.BlockSpec((DL_HEADS, 1, T, E), lambda d, n: (0, d, n, 0)),
            pl.BlockSpec((DL_HEADS, 1, T, E), lambda d, n: (0, d, jnp.minimum(n + 1, nb - 1), 0))]


def _band_mask_q(n, T, W, Ls):
    i = lax.broadcasted_iota(jnp.int32, (T, W), 0)
    j = lax.broadcasted_iota(jnp.int32, (T, W), 1)
    kpos = n * T + j - DL_HALF
    return (jnp.abs(j - DL_HALF - i) <= DL_HALF) & (kpos >= 0) & (kpos < Ls)


def band_fwd(q, k, v, bias, *, scale):
    H, dil, Ls, E = q.shape
    T, W = _dl_tiles(Ls)
    nb = Ls // T

    def body(q_ref, kp_ref, kc_ref, kn_ref, vp_ref, vc_ref, vn_ref, b_ref, ox_ref):
        n = pl.program_id(1)
        mask = _band_mask_q(n, T, W, Ls)
        for h in range(H):
            kw = _win(kp_ref, kc_ref, kn_ref, h, T)
            vw = _win(vp_ref, vc_ref, vn_ref, h, T)
            s = _dot(q_ref[h, 0], kw, ((1,), (1,))) * scale + b_ref[h]
            s = jnp.where(mask, s, NEG_BIG)
            m = jnp.max(s, axis=1, keepdims=True)
            lse = m + jnp.log(jnp.sum(jnp.exp(s - m), axis=1, keepdims=True))
            p = jnp.exp(s - lse)
            ox_ref[h, 0, :, 0:E] = _dot(p, vw, ((1,), (0,)))
            ox_ref[h, 0, :, E:2 * E] = lse + jnp.zeros((T, E), F32)

    cur = pl.BlockSpec((H, 1, T, E), lambda d, n: (0, d, n, 0))
    return pl.pallas_call(
        body, name=f"band_fwd_{dil}", grid=(dil, nb),
        in_specs=[cur] + _win_specs(T, E, nb) + _win_specs(T, E, nb) + [pl.BlockSpec((H, T, W), lambda d, n: (0, 0, 0))],
        out_specs=pl.BlockSpec((H, 1, T, 2 * E), lambda d, n: (0, d, n, 0)),
        out_shape=jax.ShapeDtypeStruct((H, dil, Ls, 2 * E), F32),
        compiler_params=_cp("parallel", "parallel"),
    )(q, k, k, k, v, v, v, bias)


def band_bwd_dq(q, k, v, bias, dox, *, scale):
    H, dil, Ls, E = q.shape
    T, W = _dl_tiles(Ls)
    nb = Ls // T

    def body(q_ref, kp_ref, kc_ref, kn_ref, vp_ref, vc_ref, vn_ref, b_ref, dox_ref, dq_ref, db_ref):
        d, n = pl.program_id(0), pl.program_id(1)
        mask = _band_mask_q(n, T, W, Ls)
        first = jnp.logical_and(d == 0, n == 0)

        @pl.when(first)
        def _():
            db_ref[...] = jnp.zeros_like(db_ref)

        for h in range(H):
            kw = _win(kp_ref, kc_ref, kn_ref, h, T)
            vw = _win(vp_ref, vc_ref, vn_ref, h, T)
            dox = dox_ref[h, 0]
            do, lse, dm = dox[:, 0:E], dox[:, DOX_LSE:DOX_LSE + 1], dox[:, DOX_DM:DOX_DM + 1]
            s = _dot(q_ref[h, 0], kw, ((1,), (1,))) * scale + b_ref[h]
            p = jnp.where(mask, jnp.exp(jnp.where(mask, s, 0.0) - lse), 0.0)
            dp = _dot(do, vw, ((1,), (1,)))
            ds = p * (dp - dm)
            dq_ref[h, 0] = (_dot(ds, kw, ((1,), (0,))) * scale).astype(dq_ref.dtype)
            db_ref[h] += ds

    cur = pl.BlockSpec((H, 1, T, E), lambda d, n: (0, d, n, 0))
    bspec = pl.BlockSpec((H, T, W), lambda d, n: (0, 0, 0))
    return pl.pallas_call(
        body, name=f"band_bwd_dq_{dil}", grid=(dil, nb),
        in_specs=[cur] + _win_specs(T, E, nb) + _win_specs(T, E, nb) + [bspec,
                  pl.BlockSpec((H, 1, T, 2 * E), lambda d, n: (0, d, n, 0))],
        out_specs=[cur, bspec],
        out_shape=[jax.ShapeDtypeStruct((H, dil, Ls, E), MXU_DTYPE), jax.ShapeDtypeStruct((H, T, W), F32)],
        compiler_params=_cp("arbitrary", "arbitrary"),
    )(q, k, k, k, v, v, v, bias, dox)


def band_bwd_dkv(q, k, v, bias_t, dox, *, scale):
    H, dil, Ls, E = k.shape
    T, W = _dl_tiles(Ls, DL_TB)
    nb = Ls // T

    def body(qp_ref, qc_ref, qn_ref, k_ref, v_ref, b_ref, dp_ref, dc_ref, dn_ref, dk_ref, dv_ref):
        n = pl.program_id(1)
        iw = lax.broadcasted_iota(jnp.int32, (W, T), 0)
        j = lax.broadcasted_iota(jnp.int32, (W, T), 1)
        qpos = n * T + iw - DL_HALF
        mask = (jnp.abs(j + DL_HALF - iw) <= DL_HALF) & (qpos >= 0) & (qpos < Ls)
        for h in range(H):
            qw = _win(qp_ref, qc_ref, qn_ref, h, T)
            doxw = _win(dp_ref, dc_ref, dn_ref, h, T)
            dow, lsew, dmw = doxw[:, 0:E], doxw[:, DOX_LSE:DOX_LSE + 1], doxw[:, DOX_DM:DOX_DM + 1]
            s = _dot(qw, k_ref[h, 0], ((1,), (1,))) * scale + b_ref[h]
            p = jnp.where(mask, jnp.exp(jnp.where(mask, s, 0.0) - lsew), 0.0)
            dv_ref[h, 0] = _dot(p, dow, ((0,), (0,))).astype(dv_ref.dtype)
            dp = _dot(dow, v_ref[h, 0], ((1,), (1,)))
            ds = p * (dp - dmw)
            dk_ref[h, 0] = (_dot(ds, qw, ((0,), (0,))) * scale).astype(dk_ref.dtype)

    cur = pl.BlockSpec((H, 1, T, E), lambda d, n: (0, d, n, 0))
    return pl.pallas_call(
        body, name=f"band_bwd_dkv_{dil}", grid=(dil, nb),
        in_specs=_win_specs(T, E, nb) + [cur, cur, pl.BlockSpec((H, W, T), lambda d, n: (0, 0, 0))]
        + _win_specs(T, 2 * E, nb),
        out_specs=[cur, cur],
        out_shape=[jax.ShapeDtypeStruct((H, dil, Ls, E), MXU_DTYPE)] * 2,
        compiler_params=_cp("parallel", "parallel"),
    )(q, q, q, k, v, bias_t, dox, dox, dox)


def dl_merge_fwd(oxs, *, R=1024):
    rows = oxs[0].shape[0]
    R = min(R, rows)
    E = DL_HD

    def body(a_ref, b_ref, c_ref, o_ref):
        vals = [r[...] for r in (a_ref, b_ref, c_ref)]
        o_ref[...] = _dl_merge_fn(*[t[:, 0:E] for t in vals], *[t[:, OX_LSE:OX_LSE + 1] for t in vals])[0]

    spec = pl.BlockSpec((R, 2 * E), lambda i: (i, 0))
    return pl.pallas_call(
        body, name="dl_merge", grid=(rows // R,), in_specs=[spec] * 3,
        out_specs=pl.BlockSpec((R, E), lambda i: (i, 0)), out_shape=jax.ShapeDtypeStruct((rows, E), F32),
        compiler_params=_cp("parallel"),
    )(*oxs)


def dl_merge_bwd(oxs, do, *, R=1024):
    rows = oxs[0].shape[0]
    R = min(R, rows)
    E = DL_HD

    def body(a_ref, b_ref, c_ref, do_ref, da_ref, db_ref, dc_ref):
        vals = [r[...] for r in (a_ref, b_ref, c_ref)]
        os_ = [t[:, 0:E] for t in vals]
        ls_ = [t[:, OX_LSE:OX_LSE + 1] for t in vals]
        _, vjp = jax.vjp(_dl_merge_fn, *os_, *ls_)
        g = vjp((do_ref[...],))
        for k, d_ref in enumerate((da_ref, db_ref, dc_ref)):
            dm = jnp.sum(g[k] * os_[k], axis=1, keepdims=True) - g[3 + k]
            d_ref[:, 0:E] = g[k]
            d_ref[:, DOX_LSE:DOX_DM] = ls_[k] + jnp.zeros((R, DOX_DM - DOX_LSE), F32)
            d_ref[:, DOX_DM:2 * E] = dm + jnp.zeros((R, 2 * E - DOX_DM), F32)

    spec = pl.BlockSpec((R, 2 * E), lambda i: (i, 0))
    return pl.pallas_call(
        body, name="dl_merge_bwd", grid=(rows // R,), in_specs=[spec] * 3 + [pl.BlockSpec((R, E), lambda i: (i, 0))],
        out_specs=[spec] * 3, out_shape=[jax.ShapeDtypeStruct((rows, 2 * E), F32)] * 3,
        compiler_params=_cp("parallel"),
    )(*oxs, do)


def _dl_bias_tables(rel_bias, dil, Tq, Tk):
    Wq, Wk = Tq + 2 * DL_HALF, Tk + 2 * DL_HALF
    bq, bk = _dl_bucket_tables(dil, Tq)[0], _dl_bucket_tables(dil, Tk)[1]
    idx = np.concatenate([bq.reshape(-1), bk.reshape(-1)])
    onehot_t = (np.arange(REL_BUCKETS)[:, None] == idx[None, :]).astype(np.float32)
    tab = matmul(rel_bias.T, jnp.asarray(onehot_t), exact=True, name=f"dl_bias_{dil}", tm=DL_HEADS, tk=REL_BUCKETS,
                 tn=_tile(idx.shape[0], (8192, 4096, 2048, 1024, 512, 256, 128)))
    return tab[:, :Tq * Wq].reshape(DL_HEADS, Tq, Wq), tab[:, Tq * Wq:].reshape(DL_HEADS, Wk, Tk), bq


def dl_layer_fwd(x, ng, p):
    S = x.shape[0]
    hn = _prenorm("dl", x, ng)
    nqkv = 3 * len(DL_PAIRS) * DL_W
    uqkv = matmul(hn, p["w_in"], name="dl_in_qkv", b_cols=(0, nqkv), out_dtype=MXU_DTYPE)
    ugate = matmul(hn, p["w_in"], name="dl_in_gate", b_cols=(nqkv, DL_W))
    scale = DL_HD ** -0.5
    per_group, ox_hm = [], []
    for gi, (window, dil) in enumerate(DL_PAIRS):
        base = gi * 3 * DL_W
        Tq, _ = _dl_tiles(S // dil)
        Tk, _ = _dl_tiles(S // dil, DL_TB)
        qs, ks, vs = [_to_stream(uqkv[:, base + c * DL_W:base + (c + 1) * DL_W], dil) for c in range(3)]
        bias, bias_t, bq = _dl_bias_tables(p["rel_bias"], dil, Tq, Tk)
        ox_s = band_fwd(qs, ks, vs, bias, scale=scale)
        per_group.append(dict(qs=qs, ks=ks, vs=vs, bias=bias, bias_t=bias_t, bq=bq, dil=dil))
        ox_hm.append(_stream_to_hm(ox_s))
    om = dl_merge_fwd(ox_hm)
    o = om.reshape(DL_HEADS, S, DL_HD).transpose(1, 0, 2).reshape(S, DL_W)
    g = rowwise_fwd("dl_gate", _gate_fn, [(o, 0, False), (ugate, 0, False)], [], [(DL_W, MXU_DTYPE)], W=DL_W)[0]
    xn = matmul(g, p["w_out"], residual=x, name="dl_out")
    return xn, dict(x=x, ng=ng, hn=hn, ugate=ugate, per_group=per_group, ox_hm=ox_hm, o=o, g=g)


def dl_layer_bwd(sv, p, dx):
    ugate = sv["ugate"]
    S = ugate.shape[0]
    scale = DL_HD ** -0.5
    dg = matmul(dx, p["w_out"], tb=True, name="dl_dg")
    dw_out = matmul(sv["g"], dx, ta=True, out_dtype=GRAD_WIRE_DTYPE, name="dl_dw_out")
    do, dgate = rowwise_bwd("dl_gate_bwd", _gate_fn, [(sv["o"], 0, False), (sv["ugate"], 0, False)], [], [dg], W=DL_W,
                            diff_rows=[0, 1], diff_shared=[])
    do_hm = do.reshape(S, DL_HEADS, DL_HD).transpose(1, 0, 2).reshape(DL_HEADS * S, DL_HD)
    dox_hm = dl_merge_bwd(sv["ox_hm"], do_hm)
    parts, dbs, onehots = [], [], []
    for gi, pg in enumerate(sv["per_group"]):
        dil = pg["dil"]
        T, W = _dl_tiles(S // dil)
        dox_s = _hm_to_stream(dox_hm[gi], dil)
        dq_s, dbias = band_bwd_dq(pg["qs"], pg["ks"], pg["vs"], pg["bias"], dox_s, scale=scale)
        dk_s, dv_s = band_bwd_dkv(pg["qs"], pg["ks"], pg["vs"], pg["bias_t"], dox_s, scale=scale)
        parts += [_from_stream(dq_s), _from_stream(dk_s), _from_stream(dv_s)]
        dbs.append(dbias.reshape(DL_HEADS, T * W))
        onehots.append((pg["bq"].reshape(-1)[:, None] == np.arange(REL_BUCKETS)[None, :]).astype(np.float32))
    drel = matmul(jnp.concatenate(dbs, axis=1), jnp.asarray(np.concatenate(onehots, axis=0)), exact=True,
                  name="dl_drel", tm=DL_HEADS, tn=REL_BUCKETS, tk=2048)
    du = _cat_mxu(parts + [dgate])
    dx_prev, dng, dw_in = _in_out_bwd("dl", sv["x"], sv["ng"], sv["hn"], du, p["w_in"], dx)
    return dx_prev, dict(w_in=dw_in, w_out=dw_out, rel_bias=drel.T, ng=dng)


WEIGHT_ORDER = ['norm_g', 'final_g', 'rel_bias', 'hgrn_lb', 'ssd_w_in', 'ssd_conv_w', 'ssd_conv_b', 'ssd_dt_bias',
                'ssd_a_log', 'ssd_d', 'ssd_norm_g', 'ssd_w_out', 'hg_w_in', 'hg_norm_g', 'hg_w_out', 'at_w_in',
                'at_q_norm_g', 'at_k_norm_g', 'at_w_out', 'dl_w_in', 'dl_w_out']
BIG_IN = ['ssd_w_in', 'hg_w_in', 'at_w_in', 'dl_w_in']
BIG_OUT = ['ssd_w_out', 'hg_w_out', 'at_w_out', 'dl_w_out']
BIG = BIG_IN + BIG_OUT
SMALL = [n for n in WEIGHT_ORDER if n not in BIG]
LANES = 128


def _pack(arrs):
    flat = jnp.concatenate([a.reshape(-1).astype(F32) for a in arrs])
    n = flat.shape[0]
    rows = -(-n // (8 * LANES)) * 8
    return jnp.pad(flat, (0, rows * LANES - n)).reshape(rows, LANES)


def _unpack(buf, shapes):
    flat = buf.reshape(-1)
    out, off = [], 0
    for shp in shapes:
        n = int(np.prod(shp)) if len(shp) else 1
        out.append(flat[off:off + n].reshape(shp))
        off += n
    return out


def kernel(x, norm_g, final_g, rel_bias, hgrn_lb, ssd_w_in, ssd_conv_w, ssd_conv_b, ssd_dt_bias, ssd_a_log, ssd_d, ssd_norm_g, ssd_w_out, hg_w_in, hg_norm_g, hg_w_out, at_w_in, at_q_norm_g, at_k_norm_g, at_w_out, dl_w_in, dl_w_out, loss_target, m_norm_g, m_final_g, m_rel_bias, m_hgrn_lb, m_ssd_w_in, m_ssd_conv_w, m_ssd_conv_b, m_ssd_dt_bias, m_ssd_a_log, m_ssd_d, m_ssd_norm_g, m_ssd_w_out, m_hg_w_in, m_hg_norm_g, m_hg_w_out, m_at_w_in, m_at_q_norm_g, m_at_k_norm_g, m_at_w_out, m_dl_w_in, m_dl_w_out, v_norm_g, v_final_g, v_rel_bias, v_hgrn_lb, v_ssd_w_in, v_ssd_conv_w, v_ssd_conv_b, v_ssd_dt_bias, v_ssd_a_log, v_ssd_d, v_ssd_norm_g, v_ssd_w_out, v_hg_w_in, v_hg_norm_g, v_hg_w_out, v_at_w_in, v_at_q_norm_g, v_at_k_norm_g, v_at_w_out, v_dl_w_in, v_dl_w_out):
    w = dict(norm_g=norm_g, final_g=final_g, rel_bias=rel_bias, hgrn_lb=hgrn_lb, ssd_w_in=ssd_w_in, ssd_conv_w=ssd_conv_w, ssd_conv_b=ssd_conv_b, ssd_dt_bias=ssd_dt_bias, ssd_a_log=ssd_a_log, ssd_d=ssd_d, ssd_norm_g=ssd_norm_g, ssd_w_out=ssd_w_out, hg_w_in=hg_w_in, hg_norm_g=hg_norm_g, hg_w_out=hg_w_out, at_w_in=at_w_in, at_q_norm_g=at_q_norm_g, at_k_norm_g=at_k_norm_g, at_w_out=at_w_out, dl_w_in=dl_w_in, dl_w_out=dl_w_out)
    m = dict(norm_g=m_norm_g, final_g=m_final_g, rel_bias=m_rel_bias, hgrn_lb=m_hgrn_lb, ssd_w_in=m_ssd_w_in, ssd_conv_w=m_ssd_conv_w, ssd_conv_b=m_ssd_conv_b, ssd_dt_bias=m_ssd_dt_bias, ssd_a_log=m_ssd_a_log, ssd_d=m_ssd_d, ssd_norm_g=m_ssd_norm_g, ssd_w_out=m_ssd_w_out, hg_w_in=m_hg_w_in, hg_norm_g=m_hg_norm_g, hg_w_out=m_hg_w_out, at_w_in=m_at_w_in, at_q_norm_g=m_at_q_norm_g, at_k_norm_g=m_at_k_norm_g, at_w_out=m_at_w_out, dl_w_in=m_dl_w_in, dl_w_out=m_dl_w_out)
    v = dict(norm_g=v_norm_g, final_g=v_final_g, rel_bias=v_rel_bias, hgrn_lb=v_hgrn_lb, ssd_w_in=v_ssd_w_in, ssd_conv_w=v_ssd_conv_w, ssd_conv_b=v_ssd_conv_b, ssd_dt_bias=v_ssd_dt_bias, ssd_a_log=v_ssd_a_log, ssd_d=v_ssd_d, ssd_norm_g=v_ssd_norm_g, ssd_w_out=v_ssd_w_out, hg_w_in=v_hg_w_in, hg_norm_g=v_hg_norm_g, hg_w_out=v_hg_w_out, at_w_in=v_at_w_in, at_q_norm_g=v_at_q_norm_g, at_k_norm_g=v_at_k_norm_g, at_w_out=v_at_w_out, dl_w_in=v_dl_w_in, dl_w_out=v_dl_w_out)
    me = 4 * lax.axis_index("x") + 2 * lax.axis_index("y") + lax.axis_index("c")
    xs = x[0]
    S = xs.shape[0]

    shard2d = {n: w[n][0] for n in BIG}
    wire = {n: shard2d[n].astype(MXU_DTYPE) for n in BIG}

    def ag(names):
        return ("ag", [wire[n] for n in names])

    def assemble(names, blks):
        out = {}
        for n, blk in zip(names, blks):
            r, c = shard2d[n].shape
            out[n] = blk.transpose(1, 0, 2).reshape(r, N_DEV * c) if n in BIG_IN else blk.reshape(N_DEV * r, c)
        return out

    def a2a(names, gw):
        bufs = []
        for n in names:
            r, c = shard2d[n].shape
            bufs.append(gw[n].reshape(r, N_DEV, c).transpose(1, 0, 2) if n in BIG_IN else gw[n].reshape(N_DEV, r, c))
        return ("a2a", bufs)

    ssd_w, hg_w, at_w, dl_w = (["ssd_w_in", "ssd_w_out"], ["hg_w_in", "hg_w_out"], ["at_w_in", "at_w_out"],
                               ["dl_w_in", "dl_w_out"])
    full = assemble(ssd_w, allgather_two_level_multi([wire[n] for n in ssd_w], name="allgather_ssd_weights"))
    ncw = ssd_conv_w.shape[2]
    nhg = hg_norm_g.shape[1]
    small_shard = jnp.zeros((8, 512), F32)
    small_shard = small_shard.at[:SSD_CONV, :ncw].set(ssd_conv_w[0]).at[SSD_CONV, :nhg].set(hg_norm_g[0])
    small_all = allgather_two_level(small_shard, name="allgather_small_weights")
    conv_w_full = small_all[:, :SSD_CONV, :ncw].transpose(1, 0, 2).reshape(SSD_CONV, N_DEV * ncw)
    hg_norm_full = small_all[:, SSD_CONV, :nhg].reshape(1, N_DEV * nhg)

    p_ssd = dict(w_in=jnp.pad(full["ssd_w_in"], ((0, 0), (0, SSD_IN_PAD - SSD_IN))), w_out=full["ssd_w_out"],
                 conv_w=conv_w_full, conv_b=ssd_conv_b,
                 dt_bias=jnp.pad(ssd_dt_bias.reshape(1, 2 * SSD_HEADS), ((0, 0), (0, 128 - 2 * SSD_HEADS))),
                 alog=jnp.pad(ssd_a_log.reshape(1, 2 * SSD_HEADS), ((0, 0), (0, 128 - 2 * SSD_HEADS))),
                 dexp=jnp.repeat(ssd_d.reshape(-1), SSD_HEADDIM)[None, :], norm_g=ssd_norm_g)
    x1, sv0, got, got_at_in = ssd_layer_fwd(xs, norm_g[0:1], p_ssd, comm=ag(hg_w), comm1=ag(["at_w_in"]))
    full.update(assemble(hg_w, got))
    p_hg = dict(w_in=full["hg_w_in"], w_out=full["hg_w_out"], norm_g=hg_norm_full, hgrn_lb=hgrn_lb)
    x2, sv1, got_at_out, _ = hg_layer_fwd(x1, norm_g[1:2], p_hg, comm0=ag(["at_w_out"]))
    full.update(assemble(at_w, got_at_in + got_at_out))
    p_at = dict(w_in=full["at_w_in"], w_out=full["at_w_out"], q_g=at_q_norm_g, k_g=at_k_norm_g)
    x3, sv2, got_dl = at_layer_fwd(x2, norm_g[2:3], p_at, comm=ag(dl_w))
    full.update(assemble(dl_w, got_dl))
    p_dl = dict(w_in=full["dl_w_in"], w_out=full["dl_w_out"], rel_bias=rel_bias)
    x4, sv3 = dl_layer_fwd(x3, norm_g[3:4], p_dl)
    loss_part, dx4, dfinal = loss_head(x4, final_g[None, :], loss_target[0])
    dx3, g3 = dl_layer_bwd(sv3, p_dl, dx4)
    dx2, g2, recv_dl = at_layer_bwd(sv2, p_at, dx3, comm=a2a(dl_w, dict(dl_w_in=g3["w_in"], dl_w_out=g3["w_out"])))
    dx1, g1, recv_at = hg_layer_bwd(sv1, p_hg, dx2, comm=a2a(at_w, dict(at_w_in=g2["w_in"], at_w_out=g2["w_out"])))
    dx0, g0, recv_hg, recv_ssd = ssd_layer_bwd(
        sv0, p_ssd, dx1, comm=a2a(hg_w, dict(hg_w_in=g1["w_in"], hg_w_out=g1["w_out"])),
        tail_comm=lambda g_in, g_out: (a2a(["ssd_w_in"], dict(ssd_w_in=g_in)) if g_out is None
                                       else a2a(["ssd_w_out"], dict(ssd_w_out=g_out))))
    recv = dict(zip(ssd_w + hg_w + at_w + dl_w, recv_ssd + recv_hg + recv_at + recv_dl))

    small_full = dict(
        norm_g=jnp.concatenate([g0["ng"], g1["ng"], g2["ng"], g3["ng"]], axis=0), final_g=dfinal[0],
        rel_bias=g3["rel_bias"], hgrn_lb=g1["hgrn_lb"], ssd_conv_w=g0["conv_w"][None], ssd_conv_b=g0["conv_b"],
        ssd_dt_bias=g0["dt_bias"].reshape(1, 2, SSD_HEADS), ssd_a_log=g0["a_log"].reshape(1, 2, SSD_HEADS),
        ssd_d=g0["d"], ssd_norm_g=g0["norm_g"], hg_norm_g=g1["norm_g"], at_q_norm_g=g2["q_g"], at_k_norm_g=g2["k_g"])
    packed = _pack([loss_part[0, 0:1]] + [small_full[n] for n in SMALL])
    summed = sum_parts(allgather_two_level(packed, name="allgather_small_grads"), name="sum_small_grads")
    parts = _unpack(summed, [()] + [small_full[n].shape for n in SMALL])
    loss = parts[0]
    gsmall = dict(zip(SMALL, parts[1:]))
    gsmall["ssd_conv_w"] = lax.dynamic_slice_in_dim(gsmall["ssd_conv_w"], me * ncw, ncw, axis=2)
    gsmall["hg_norm_g"] = lax.dynamic_slice_in_dim(gsmall["hg_norm_g"], me * nhg, nhg, axis=1)
    shapes = [w[n].shape for n in SMALL]
    d_p, m_p, v_p = adamw_plain(_pack([w[n] for n in SMALL]), _pack([gsmall[n] for n in SMALL]),
                                _pack([m[n] for n in SMALL]), _pack([v[n] for n in SMALL]), name="adamw_small")
    grads = dict(gsmall)
    deltas = dict(zip(SMALL, _unpack(d_p, shapes)))
    new_m = dict(zip(SMALL, _unpack(m_p, shapes)))
    new_v = dict(zip(SMALL, _unpack(v_p, shapes)))

    for n in BIG:
        gs, ds, ms, vs = adamw_sum(recv[n], shard2d[n], m[n][0], v[n][0], name=f"adamw_{n}")
        grads[n], deltas[n], new_m[n], new_v[n] = gs[None], ds[None], ms[None], vs[None]

    return (loss, dx0[None], *[grads[n] for n in WEIGHT_ORDER], *[deltas[n] for n in WEIGHT_ORDER],
            *[new_m[n] for n in WEIGHT_ORDER], *[new_v[n] for n in WEIGHT_ORDER])
```
